```python
import jax, jax.numpy as jnp
from jax import lax
import numpy as np

D_MODEL = 1024
BATCH = 8
SEQ = 4096
DEPTH = 2

CHUNK = 64
CONV_CH = 512
CONV_WIDTH = 31
HG_HEADS = 4
HG_DK = 128
HG_DV = 128
HG_WIDTH = HG_HEADS * HG_DK
SB_HEADS = 8
SB_DH = 64
SB_WIDTH = SB_HEADS * SB_DH
N_BRANCH = 3
D_FF = 4 * D_MODEL
QBLK = 128
EPS = 1e-6

IN_SIZES = (CONV_CH, CONV_CH, HG_WIDTH, HG_WIDTH, HG_HEADS * HG_DV, HG_HEADS * HG_DV,
            SB_WIDTH, SB_WIDTH, SB_WIDTH, N_BRANCH * D_MODEL)
D_IN = int(sum(IN_SIZES))
SPLIT_IDX = [int(v) for v in np.cumsum(IN_SIZES)[:-1]]

kernel_name = "hybrid_conv_hgrn2_stickbreak_block"


def rms_norm(x, g):
    xf = x.astype(jnp.float32)
    y = xf * lax.rsqrt(jnp.mean(xf * xf, axis=-1, keepdims=True) + EPS)
    return (y * g.astype(jnp.float32)).astype(x.dtype)


def layer_norm(x, g, b):
    xf = x.astype(jnp.float32)
    mu = jnp.mean(xf, axis=-1, keepdims=True)
    var = jnp.mean(jnp.square(xf - mu), axis=-1, keepdims=True)
    y = (xf - mu) * lax.rsqrt(var + EPS)
    return (y * g.astype(jnp.float32) + b.astype(jnp.float32)).astype(x.dtype)


def conv_branch(a, gate, w, b, ln_g, ln_b, w_proj):
    u = a * jax.nn.sigmoid(gate)
    u = lax.conv_general_dilated(
        u, w[:, None, :], window_strides=(1,), padding=((CONV_WIDTH - 1, 0),),
        dimension_numbers=('NWC', 'WIO', 'NWC'), feature_group_count=CONV_CH) + b
    u = jax.nn.silu(layer_norm(u, ln_g, ln_b))
    return u @ w_proj


def hgrn2_branch(q, f, i, g, lb, norm_g, w_proj):
    B, S, _ = q.shape
    n_chunks = S // CHUNK
    f32 = jnp.float32

    def heads(t, d):
        return t.reshape(B, n_chunks, CHUNK, HG_HEADS, d).transpose(1, 0, 3, 2, 4)

    k = (1.0 - lb.astype(f32)) * jax.nn.sigmoid(-f.astype(f32))
    log_f = jnp.log1p(-k)
    qh = heads(jax.nn.silu(q.astype(f32)), HG_DK)
    kh = heads(k, HG_DK)
    lfh = heads(log_f, HG_DK)
    vh = heads(i.astype(f32), HG_DV)
    causal = jnp.tril(jnp.ones((CHUNK, CHUNK), dtype=bool))[:, :, None]

    def step(state, inp):
        qc, kc, lc, vc = inp
        b = jnp.cumsum(lc, axis=2)
        o_inter = jnp.einsum('bhtk,bhkv->bhtv', qc * jnp.exp(b), state)
        rel = b[:, :, :, None, :] - b[:, :, None, :, :]
        decay = jnp.exp(jnp.where(causal, rel, -jnp.inf))
        scores = jnp.einsum('bhtsk,bhsk->bhts', qc[:, :, :, None, :] * decay, kc)
        o = o_inter + jnp.einsum('bhts,bhsv->bhtv', scores, vc)
        b_last = b[:, :, -1:, :]
        new_state = (jnp.exp(b_last[:, :, 0, :])[..., None] * state
                     + jnp.einsum('bhsk,bhsv->bhkv', kc * jnp.exp(b_last - b), vc))
        return new_state, o

    s0 = jnp.zeros((B, HG_HEADS, HG_DK, HG_DV), f32)
    _, o = lax.scan(step, s0, (qh, kh, lfh, vh))
    o = o.transpose(1, 0, 3, 2, 4).reshape(B, S, HG_HEADS, HG_DV)
    o = rms_norm(o, norm_g).reshape(B, S, HG_HEADS * HG_DV)
    o = (o * jax.nn.silu(g.astype(f32))).astype(q.dtype)
    return o @ w_proj


def stick_breaking_branch(q, k, v, qn_g, kn_g, w_proj):
    B, S, _ = q.shape
    qh = rms_norm(q.reshape(B, S, SB_HEADS, SB_DH), qn_g).transpose(0, 2, 1, 3)
    kh = rms_norm(k.reshape(B, S, SB_HEADS, SB_DH), kn_g).transpose(0, 2, 1, 3)
    vh = v.reshape(B, S, SB_HEADS, SB_DH).transpose(0, 2, 1, 3)
    scale = SB_DH ** -0.5
    outs = []
    for blk in range(S // QBLK):
        start, end = blk * QBLK, (blk + 1) * QBLK
        qb = qh[:, :, start:end]
        kb = kh[:, :, :end]
        vb = vh[:, :, :end]
        z = jnp.einsum('bhtd,bhsd->bhts', qb, kb).astype(jnp.float32) * scale
        t_pos = start + jnp.arange(QBLK)
        s_pos = jnp.arange(end)
        mask = s_pos[None, :] < t_pos[:, None]
        log_keep = jnp.where(mask, jax.nn.log_sigmoid(-z), 0.0)
        between = lax.cumsum(log_keep, axis=3, reverse=True) - log_keep
        a = jnp.where(mask, jnp.exp(jax.nn.log_sigmoid(z) + between), 0.0)
        outs.append(jnp.einsum('bhts,bhsd->bhtd', a.astype(vb.dtype), vb))
    o = jnp.concatenate(outs, axis=2).transpose(0, 2, 1, 3).reshape(B, S, SB_WIDTH)
    return o @ w_proj


def _fwd_setup_inputs(seed: int = 0) -> dict:
    key = jax.random.key(seed)
    ks = jax.random.split(key, 24)

    def nrm(k, shape, scale):
        return jax.random.normal(k, shape, jnp.float32) * scale

    L, D = DEPTH, D_MODEL
    return {
        "x": nrm(ks[0], (BATCH, SEQ, D), 1.0),
        "c": nrm(ks[1], (BATCH, D), 1.0),
        "mod_w": nrm(ks[2], (L, D, 6 * D), 0.5 * D ** -0.5),
        "mod_b": nrm(ks[3], (L, 6 * D), 0.01),
        "norm1_g": 1.0 + nrm(ks[4], (L, D), 0.02),
        "w_in": nrm(ks[5], (L, D, D_IN), D ** -0.5),
        "gate_b": nrm(ks[6], (L, N_BRANCH * D), 0.01),
        "conv_w": nrm(ks[7], (L, CONV_WIDTH, CONV_CH), CONV_WIDTH ** -0.5),
        "conv_b": nrm(ks[8], (L, CONV_CH), 0.01),
        "conv_ln_g": 1.0 + nrm(ks[9], (L, CONV_CH), 0.02),
        "conv_ln_b": nrm(ks[10], (L, CONV_CH), 0.01),
        "w_conv_proj": nrm(ks[11], (L, CONV_CH, D), CONV_CH ** -0.5),
        "hgrn_lb": nrm(ks[12], (L, HG_WIDTH), 0.5),
        "hgrn_norm_g": 1.0 + nrm(ks[13], (L, HG_DV), 0.02),
        "w_hgrn_proj": nrm(ks[14], (L, HG_HEADS * HG_DV, D), (HG_HEADS * HG_DV) ** -0.5),
        "sb_qn_g": 1.0 + nrm(ks[15], (L, SB_DH), 0.02),
        "sb_kn_g": 1.0 + nrm(ks[16], (L, SB_DH), 0.02),
        "w_sb_proj": nrm(ks[17], (L, SB_WIDTH, D), SB_WIDTH ** -0.5),
        "w_out": nrm(ks[18], (L, D, D), D ** -0.5),
        "norm2_g": 1.0 + nrm(ks[19], (L, D), 0.02),
        "mlp_w1": nrm(ks[20], (L, D, D_FF), D ** -0.5),
        "mlp_w2": nrm(ks[21], (L, D_FF, D), D_FF ** -0.5),
    }


def _fwd_reference(x, c, mod_w, mod_b, norm1_g, w_in, gate_b, conv_w, conv_b, conv_ln_g,
              conv_ln_b, w_conv_proj, hgrn_lb, hgrn_norm_g, w_hgrn_proj, sb_qn_g,
              sb_kn_g, w_sb_proj, w_out, norm2_g, mlp_w1, mlp_w2):
    B, S, D = x.shape
    p = jax.nn.softmax(hgrn_lb.astype(jnp.float32), axis=0)
    lower_bounds = jnp.cumsum(p, axis=0) - p[0:1]
    c_act = jax.nn.silu(c)
    for l in range(DEPTH):
        mod = c_act @ mod_w[l] + mod_b[l]
        sh1, sc1, g1, sh2, sc2, g2 = [m[:, None, :] for m in jnp.split(mod, 6, axis=-1)]

        h = rms_norm(x, norm1_g[l]) * (1.0 + sc1) + sh1
        proj = h @ w_in[l]
        (cv_a, cv_g, hg_q, hg_f, hg_i, hg_g, sb_q, sb_k, sb_v, gl) = jnp.split(proj, SPLIT_IDX, axis=-1)
        y_conv = conv_branch(cv_a, cv_g, conv_w[l], conv_b[l], conv_ln_g[l], conv_ln_b[l], w_conv_proj[l])
        y_hgrn = hgrn2_branch(hg_q, hg_f, hg_i, hg_g, lower_bounds[l], hgrn_norm_g[l], w_hgrn_proj[l])
        y_sb = stick_breaking_branch(sb_q, sb_k, sb_v, sb_qn_g[l], sb_kn_g[l], w_sb_proj[l])
        gates = jax.nn.sigmoid(gl + gate_b[l]).reshape(B, S, N_BRANCH, D)
        merged = gates[:, :, 0] * y_conv + gates[:, :, 1] * y_hgrn + gates[:, :, 2] * y_sb
        x = x + g1 * (merged @ w_out[l])

        h2 = rms_norm(x, norm2_g[l]) * (1.0 + sc2) + sh2
        x = x + g2 * (jnp.square(jax.nn.relu(h2 @ mlp_w1[l])) @ mlp_w2[l])
    return x


import jax as _jax
import jax.numpy as _jnp

TWIN_FORMAT = 'train_step'
FWD_PARAMS = ['x', 'c', 'mod_w', 'mod_b', 'norm1_g', 'w_in', 'gate_b', 'conv_w', 'conv_b', 'conv_ln_g', 'conv_ln_b', 'w_conv_proj', 'hgrn_lb', 'hgrn_norm_g', 'w_hgrn_proj', 'sb_qn_g', 'sb_kn_g', 'w_sb_proj', 'w_out', 'norm2_g', 'mlp_w1', 'mlp_w2']
TWIN_WEIGHTS = ['mod_w', 'mod_b', 'norm1_g', 'w_in', 'gate_b', 'conv_w', 'conv_b', 'conv_ln_g', 'conv_ln_b', 'w_conv_proj', 'hgrn_lb', 'hgrn_norm_g', 'w_hgrn_proj', 'sb_qn_g', 'sb_kn_g', 'w_sb_proj', 'w_out', 'norm2_g', 'mlp_w1', 'mlp_w2']
TWIN_DIFF_INPUT = 'x'
TWIN_INPUTS = ['x', 'c', 'mod_w', 'mod_b', 'norm1_g', 'w_in', 'gate_b', 'conv_w', 'conv_b', 'conv_ln_g', 'conv_ln_b', 'w_conv_proj', 'hgrn_lb', 'hgrn_norm_g', 'w_hgrn_proj', 'sb_qn_g', 'sb_kn_g', 'w_sb_proj', 'w_out', 'norm2_g', 'mlp_w1', 'mlp_w2', 'loss_target', 'm_mod_w', 'm_mod_b', 'm_norm1_g', 'm_w_in', 'm_gate_b', 'm_conv_w', 'm_conv_b', 'm_conv_ln_g', 'm_conv_ln_b', 'm_w_conv_proj', 'm_hgrn_lb', 'm_hgrn_norm_g', 'm_w_hgrn_proj', 'm_sb_qn_g', 'm_sb_kn_g', 'm_w_sb_proj', 'm_w_out', 'm_norm2_g', 'm_mlp_w1', 'm_mlp_w2', 'v_mod_w', 'v_mod_b', 'v_norm1_g', 'v_w_in', 'v_gate_b', 'v_conv_w', 'v_conv_b', 'v_conv_ln_g', 'v_conv_ln_b', 'v_w_conv_proj', 'v_hgrn_lb', 'v_hgrn_norm_g', 'v_w_hgrn_proj', 'v_sb_qn_g', 'v_sb_kn_g', 'v_w_sb_proj', 'v_w_out', 'v_norm2_g', 'v_mlp_w1', 'v_mlp_w2']
TWIN_OUTPUTS = ['loss', 'grad_x', 'grad_mod_w', 'grad_mod_b', 'grad_norm1_g', 'grad_w_in', 'grad_gate_b', 'grad_conv_w', 'grad_conv_b', 'grad_conv_ln_g', 'grad_conv_ln_b', 'grad_w_conv_proj', 'grad_hgrn_lb', 'grad_hgrn_norm_g', 'grad_w_hgrn_proj', 'grad_sb_qn_g', 'grad_sb_kn_g', 'grad_w_sb_proj', 'grad_w_out', 'grad_norm2_g', 'grad_mlp_w1', 'grad_mlp_w2', 'delta_mod_w', 'delta_mod_b', 'delta_norm1_g', 'delta_w_in', 'delta_gate_b', 'delta_conv_w', 'delta_conv_b', 'delta_conv_ln_g', 'delta_conv_ln_b', 'delta_w_conv_proj', 'delta_hgrn_lb', 'delta_hgrn_norm_g', 'delta_w_hgrn_proj', 'delta_sb_qn_g', 'delta_sb_kn_g', 'delta_w_sb_proj', 'delta_w_out', 'delta_norm2_g', 'delta_mlp_w1', 'delta_mlp_w2', 'new_m_mod_w', 'new_m_mod_b', 'new_m_norm1_g', 'new_m_w_in', 'new_m_gate_b', 'new_m_conv_w', 'new_m_conv_b', 'new_m_conv_ln_g', 'new_m_conv_ln_b', 'new_m_w_conv_proj', 'new_m_hgrn_lb', 'new_m_hgrn_norm_g', 'new_m_w_hgrn_proj', 'new_m_sb_qn_g', 'new_m_sb_kn_g', 'new_m_w_sb_proj', 'new_m_w_out', 'new_m_norm2_g', 'new_m_mlp_w1', 'new_m_mlp_w2', 'new_v_mod_w', 'new_v_mod_b', 'new_v_norm1_g', 'new_v_w_in', 'new_v_gate_b', 'new_v_conv_w', 'new_v_conv_b', 'new_v_conv_ln_g', 'new_v_conv_ln_b', 'new_v_w_conv_proj', 'new_v_hgrn_lb', 'new_v_hgrn_norm_g', 'new_v_w_hgrn_proj', 'new_v_sb_qn_g', 'new_v_sb_kn_g', 'new_v_w_sb_proj', 'new_v_w_out', 'new_v_norm2_g', 'new_v_mlp_w1', 'new_v_mlp_w2']
TWIN_LEAF_KINDS = {'loss': 'loss', 'grad_x': 'grad_x', 'grad_mod_w': 'grad_w', 'grad_mod_b': 'grad_w', 'grad_norm1_g': 'grad_w', 'grad_w_in': 'grad_w', 'grad_gate_b': 'grad_w', 'grad_conv_w': 'grad_w', 'grad_conv_b': 'grad_w', 'grad_conv_ln_g': 'grad_w', 'grad_conv_ln_b': 'grad_w', 'grad_w_conv_proj': 'grad_w', 'grad_hgrn_lb': 'grad_w', 'grad_hgrn_norm_g': 'grad_w', 'grad_w_hgrn_proj': 'grad_w', 'grad_sb_qn_g': 'grad_w', 'grad_sb_kn_g': 'grad_w', 'grad_w_sb_proj': 'grad_w', 'grad_w_out': 'grad_w', 'grad_norm2_g': 'grad_w', 'grad_mlp_w1': 'grad_w', 'grad_mlp_w2': 'grad_w', 'delta_mod_w': 'delta_w', 'delta_mod_b': 'delta_w', 'delta_norm1_g': 'delta_w', 'delta_w_in': 'delta_w', 'delta_gate_b': 'delta_w', 'delta_conv_w': 'delta_w', 'delta_conv_b': 'delta_w', 'delta_conv_ln_g': 'delta_w', 'delta_conv_ln_b': 'delta_w', 'delta_w_conv_proj': 'delta_w', 'delta_hgrn_lb': 'delta_w', 'delta_hgrn_norm_g': 'delta_w', 'delta_w_hgrn_proj': 'delta_w', 'delta_sb_qn_g': 'delta_w', 'delta_sb_kn_g': 'delta_w', 'delta_w_sb_proj': 'delta_w', 'delta_w_out': 'delta_w', 'delta_norm2_g': 'delta_w', 'delta_mlp_w1': 'delta_w', 'delta_mlp_w2': 'delta_w', 'new_m_mod_w': 'new_m', 'new_m_mod_b': 'new_m', 'new_m_norm1_g': 'new_m', 'new_m_w_in': 'new_m', 'new_m_gate_b': 'new_m', 'new_m_conv_w': 'new_m', 'new_m_conv_b': 'new_m', 'new_m_conv_ln_g': 'new_m', 'new_m_conv_ln_b': 'new_m', 'new_m_w_conv_proj': 'new_m', 'new_m_hgrn_lb': 'new_m', 'new_m_hgrn_norm_g': 'new_m', 'new_m_w_hgrn_proj': 'new_m', 'new_m_sb_qn_g': 'new_m', 'new_m_sb_kn_g': 'new_m', 'new_m_w_sb_proj': 'new_m', 'new_m_w_out': 'new_m', 'new_m_norm2_g': 'new_m', 'new_m_mlp_w1': 'new_m', 'new_m_mlp_w2': 'new_m', 'new_v_mod_w': 'new_v', 'new_v_mod_b': 'new_v', 'new_v_norm1_g': 'new_v', 'new_v_w_in': 'new_v', 'new_v_gate_b': 'new_v', 'new_v_conv_w': 'new_v', 'new_v_conv_b': 'new_v', 'new_v_conv_ln_g': 'new_v', 'new_v_conv_ln_b': 'new_v', 'new_v_w_conv_proj': 'new_v', 'new_v_hgrn_lb': 'new_v', 'new_v_hgrn_norm_g': 'new_v', 'new_v_w_hgrn_proj': 'new_v', 'new_v_sb_qn_g': 'new_v', 'new_v_sb_kn_g': 'new_v', 'new_v_w_sb_proj': 'new_v', 'new_v_w_out': 'new_v', 'new_v_norm2_g': 'new_v', 'new_v_mlp_w1': 'new_v', 'new_v_mlp_w2': 'new_v'}


def _forward(args):
    return _fwd_reference(*[args[k] for k in FWD_PARAMS])


def _output_shape():
    out = _jax.eval_shape(lambda: _forward(_fwd_setup_inputs(0)))
    return out.shape, out.dtype

N_MICROBATCH = 1
ADAM_LR = 0.001
ADAM_B1 = 0.9
ADAM_B2 = 0.999
ADAM_EPS = 1e-08
ADAM_WD = 0.01
ADAM_STEP = 10
PER_EXAMPLE_BATCH_AXIS = {'x': 0, 'c': 0, 'loss_target': 0}
SHARED_INPUTS = []
_WEIGHT_DTYPES = {'mod_w': _jnp.float32, 'mod_b': _jnp.float32, 'norm1_g': _jnp.float32, 'w_in': _jnp.float32, 'gate_b': _jnp.float32, 'conv_w': _jnp.float32, 'conv_b': _jnp.float32, 'conv_ln_g': _jnp.float32, 'conv_ln_b': _jnp.float32, 'w_conv_proj': _jnp.float32, 'hgrn_lb': _jnp.float32, 'hgrn_norm_g': _jnp.float32, 'w_hgrn_proj': _jnp.float32, 'sb_qn_g': _jnp.float32, 'sb_kn_g': _jnp.float32, 'w_sb_proj': _jnp.float32, 'w_out': _jnp.float32, 'norm2_g': _jnp.float32, 'mlp_w1': _jnp.float32, 'mlp_w2': _jnp.float32}
MOMENT_SCALE = {'mod_w': 3.456805e+00, 'mod_b': 7.319858e+00, 'norm1_g': 1.017600e+00, 'w_in': 1.268759e-01, 'gate_b': 1.658150e-01, 'conv_w': 1.820235e-01, 'conv_b': 1.169882e+00, 'conv_ln_g': 9.935598e-01, 'conv_ln_b': 9.312674e-01, 'w_conv_proj': 2.255239e-01, 'hgrn_lb': 4.788066e-03, 'hgrn_norm_g': 3.299731e+00, 'w_hgrn_proj': 1.303975e-01, 'sb_qn_g': 9.508791e-01, 'sb_kn_g': 9.502526e-01, 'w_sb_proj': 3.018850e-01, 'w_out': 3.872237e-01, 'norm2_g': 1.227509e+01, 'mlp_w1': 4.264052e-01, 'mlp_w2': 1.574713e+00}


def _to_microbatches(a, axis):
    t = _jnp.moveaxis(a, axis, 0)
    t = t.reshape((N_MICROBATCH, t.shape[0] // N_MICROBATCH) + t.shape[1:])
    return _jnp.moveaxis(t, 1, axis + 1)


def setup_inputs(seed: int = 0) -> dict:
    inp = _fwd_setup_inputs(seed)
    key = _jax.random.fold_in(_jax.random.key(seed), 7919)
    shape, _ = _output_shape()
    out = dict(inp)
    out["loss_target"] = _jax.random.normal(_jax.random.fold_in(key, 0), shape, _jnp.float32)
    for i, name in enumerate(TWIN_WEIGHTS):
        w = inp[name].astype(_jnp.float32)
        if MOMENT_SCALE is None:
            s = _jnp.sqrt(_jnp.mean(_jnp.square(w)) + 1e-30)
        else:
            s = MOMENT_SCALE[name]
        km, kv = _jax.random.split(_jax.random.fold_in(key, i + 1))
        out[name] = w
        out["m_" + name] = s * _jax.random.normal(km, w.shape, _jnp.float32)
        out["v_" + name] = (s * s) * _jax.random.uniform(kv, w.shape, _jnp.float32, 0.5, 1.5)
    if N_MICROBATCH > 1:
        for name, axis in PER_EXAMPLE_BATCH_AXIS.items():
            out[name] = _to_microbatches(out[name], axis)
    return {'x': out['x'], 'c': out['c'], 'mod_w': out['mod_w'], 'mod_b': out['mod_b'], 'norm1_g': out['norm1_g'], 'w_in': out['w_in'], 'gate_b': out['gate_b'], 'conv_w': out['conv_w'], 'conv_b': out['conv_b'], 'conv_ln_g': out['conv_ln_g'], 'conv_ln_b': out['conv_ln_b'], 'w_conv_proj': out['w_conv_proj'], 'hgrn_lb': out['hgrn_lb'], 'hgrn_norm_g': out['hgrn_norm_g'], 'w_hgrn_proj': out['w_hgrn_proj'], 'sb_qn_g': out['sb_qn_g'], 'sb_kn_g': out['sb_kn_g'], 'w_sb_proj': out['w_sb_proj'], 'w_out': out['w_out'], 'norm2_g': out['norm2_g'], 'mlp_w1': out['mlp_w1'], 'mlp_w2': out['mlp_w2'], 'loss_target': out['loss_target'], 'm_mod_w': out['m_mod_w'], 'm_mod_b': out['m_mod_b'], 'm_norm1_g': out['m_norm1_g'], 'm_w_in': out['m_w_in'], 'm_gate_b': out['m_gate_b'], 'm_conv_w': out['m_conv_w'], 'm_conv_b': out['m_conv_b'], 'm_conv_ln_g': out['m_conv_ln_g'], 'm_conv_ln_b': out['m_conv_ln_b'], 'm_w_conv_proj': out['m_w_conv_proj'], 'm_hgrn_lb': out['m_hgrn_lb'], 'm_hgrn_norm_g': out['m_hgrn_norm_g'], 'm_w_hgrn_proj': out['m_w_hgrn_proj'], 'm_sb_qn_g': out['m_sb_qn_g'], 'm_sb_kn_g': out['m_sb_kn_g'], 'm_w_sb_proj': out['m_w_sb_proj'], 'm_w_out': out['m_w_out'], 'm_norm2_g': out['m_norm2_g'], 'm_mlp_w1': out['m_mlp_w1'], 'm_mlp_w2': out['m_mlp_w2'], 'v_mod_w': out['v_mod_w'], 'v_mod_b': out['v_mod_b'], 'v_norm1_g': out['v_norm1_g'], 'v_w_in': out['v_w_in'], 'v_gate_b': out['v_gate_b'], 'v_conv_w': out['v_conv_w'], 'v_conv_b': out['v_conv_b'], 'v_conv_ln_g': out['v_conv_ln_g'], 'v_conv_ln_b': out['v_conv_ln_b'], 'v_w_conv_proj': out['v_w_conv_proj'], 'v_hgrn_lb': out['v_hgrn_lb'], 'v_hgrn_norm_g': out['v_hgrn_norm_g'], 'v_w_hgrn_proj': out['v_w_hgrn_proj'], 'v_sb_qn_g': out['v_sb_qn_g'], 'v_sb_kn_g': out['v_sb_kn_g'], 'v_w_sb_proj': out['v_w_sb_proj'], 'v_w_out': out['v_w_out'], 'v_norm2_g': out['v_norm2_g'], 'v_mlp_w1': out['v_mlp_w1'], 'v_mlp_w2': out['v_mlp_w2']}


def _loss(weights, diff, rest, loss_target):
    with _jax.named_scope("forward"):
        args = {**rest, TWIN_DIFF_INPUT: diff, **{k: w.astype(_WEIGHT_DTYPES[k]) for k, w in weights.items()}}
        y = _forward(args)
    with _jax.named_scope("loss_head"):
        err = _jnp.square(y.astype(_jnp.float32) - loss_target)
        return 0.5 * _jnp.sum(_jnp.mean(err, axis=-1)) if err.ndim else 0.5 * err


def _adamw(w, g, m, v):
    m = ADAM_B1 * m + (1.0 - ADAM_B1) * g
    v = ADAM_B2 * v + (1.0 - ADAM_B2) * _jnp.square(g)
    m_hat = m / (1.0 - ADAM_B1 ** ADAM_STEP)
    v_hat = v / (1.0 - ADAM_B2 ** ADAM_STEP)
    delta = -ADAM_LR * (m_hat / (_jnp.sqrt(v_hat) + ADAM_EPS) + ADAM_WD * w)
    return delta, m, v


def reference(x, c, mod_w, mod_b, norm1_g, w_in, gate_b, conv_w, conv_b, conv_ln_g, conv_ln_b, w_conv_proj, hgrn_lb, hgrn_norm_g, w_hgrn_proj, sb_qn_g, sb_kn_g, w_sb_proj, w_out, norm2_g, mlp_w1, mlp_w2, loss_target, m_mod_w, m_mod_b, m_norm1_g, m_w_in, m_gate_b, m_conv_w, m_conv_b, m_conv_ln_g, m_conv_ln_b, m_w_conv_proj, m_hgrn_lb, m_hgrn_norm_g, m_w_hgrn_proj, m_sb_qn_g, m_sb_kn_g, m_w_sb_proj, m_w_out, m_norm2_g, m_mlp_w1, m_mlp_w2, v_mod_w, v_mod_b, v_norm1_g, v_w_in, v_gate_b, v_conv_w, v_conv_b, v_conv_ln_g, v_conv_ln_b, v_w_conv_proj, v_hgrn_lb, v_hgrn_norm_g, v_w_hgrn_proj, v_sb_qn_g, v_sb_kn_g, v_w_sb_proj, v_w_out, v_norm2_g, v_mlp_w1, v_mlp_w2):
    given = dict(x=x, c=c, mod_w=mod_w, mod_b=mod_b, norm1_g=norm1_g, w_in=w_in, gate_b=gate_b, conv_w=conv_w, conv_b=conv_b, conv_ln_g=conv_ln_g, conv_ln_b=conv_ln_b, w_conv_proj=w_conv_proj, hgrn_lb=hgrn_lb, hgrn_norm_g=hgrn_norm_g, w_hgrn_proj=w_hgrn_proj, sb_qn_g=sb_qn_g, sb_kn_g=sb_kn_g, w_sb_proj=w_sb_proj, w_out=w_out, norm2_g=norm2_g, mlp_w1=mlp_w1, mlp_w2=mlp_w2, loss_target=loss_target, m_mod_w=m_mod_w, m_mod_b=m_mod_b, m_norm1_g=m_norm1_g, m_w_in=m_w_in, m_gate_b=m_gate_b, m_conv_w=m_conv_w, m_conv_b=m_conv_b, m_conv_ln_g=m_conv_ln_g, m_conv_ln_b=m_conv_ln_b, m_w_conv_proj=m_w_conv_proj, m_hgrn_lb=m_hgrn_lb, m_hgrn_norm_g=m_hgrn_norm_g, m_w_hgrn_proj=m_w_hgrn_proj, m_sb_qn_g=m_sb_qn_g, m_sb_kn_g=m_sb_kn_g, m_w_sb_proj=m_w_sb_proj, m_w_out=m_w_out, m_norm2_g=m_norm2_g, m_mlp_w1=m_mlp_w1, m_mlp_w2=m_mlp_w2, v_mod_w=v_mod_w, v_mod_b=v_mod_b, v_norm1_g=v_norm1_g, v_w_in=v_w_in, v_gate_b=v_gate_b, v_conv_w=v_conv_w, v_conv_b=v_conv_b, v_conv_ln_g=v_conv_ln_g, v_conv_ln_b=v_conv_ln_b, v_w_conv_proj=v_w_conv_proj, v_hgrn_lb=v_hgrn_lb, v_hgrn_norm_g=v_hgrn_norm_g, v_w_hgrn_proj=v_w_hgrn_proj, v_sb_qn_g=v_sb_qn_g, v_sb_kn_g=v_sb_kn_g, v_w_sb_proj=v_w_sb_proj, v_w_out=v_w_out, v_norm2_g=v_norm2_g, v_mlp_w1=v_mlp_w1, v_mlp_w2=v_mlp_w2)
    weights = {n: given[n] for n in TWIN_WEIGHTS}
    shared = {n: given[n] for n in SHARED_INPUTS}
    per_example = {n: given[n] for n in ['x', 'c']}
    grad_fn = _jax.value_and_grad(_loss, argnums=(0, 1))

    def one_microbatch(ex, loss_target):
        ex = dict(ex)
        diff = ex.pop(TWIN_DIFF_INPUT)
        return grad_fn(weights, diff, {**shared, **ex}, loss_target)

    if N_MICROBATCH == 1:
        loss, (grad_w, grad_x) = one_microbatch(per_example, given["loss_target"])
    else:
        def body(carry, xs):
            loss_sum, grad_sum = carry
            l_k, (gw_k, gx_k) = one_microbatch(xs[0], xs[1])
            with _jax.named_scope("update"):
                return (loss_sum + l_k, _jax.tree.map(_jnp.add, grad_sum, gw_k)), gx_k

        init = (_jnp.zeros((), _jnp.float32), _jax.tree.map(_jnp.zeros_like, weights))
        (loss, grad_w), grad_x = _jax.lax.scan(body, init, (per_example, given["loss_target"]))
    with _jax.named_scope("update"):
        delta_w, new_m, new_v = {}, {}, {}
        for n in TWIN_WEIGHTS:
            delta_w[n], new_m[n], new_v[n] = _adamw(weights[n], grad_w[n], given["m_" + n], given["v_" + n])
    return (loss, grad_x, *[grad_w[n] for n in TWIN_WEIGHTS], *[delta_w[n] for n in TWIN_WEIGHTS],
            *[new_m[n] for n in TWIN_WEIGHTS], *[new_v[n] for n in TWIN_WEIGHTS])
```

```python
import functools

import jax
import jax.numpy as jnp
import numpy as np
from jax import lax
from jax.experimental import pallas as pl
from jax.experimental.pallas import tpu as pltpu

F32 = jnp.float32
BF16 = jnp.bfloat16

D_MODEL = 1024
DEPTH = 2
N_DEV = 8
CONV_CH = 512
CONV_WIDTH = 31
CONV_HALO = 32
HG_HEADS = 4
HG_DK = 128
SB_HEADS = 8
SB_DH = 64
D_IN = 7680
D_FF = 4096
EPS = 1e-6
SB_BLK = 128
SB_DEAD = -104.0
HG_CHUNK = 64

ADAM_LR = 0.001
ADAM_B1 = 0.9
ADAM_B2 = 0.999
ADAM_EPS = 1e-08
ADAM_WD = 0.01
ADAM_STEP = 10

VMEM_LIMIT = 48 * 1024 * 1024

NN = ((1,), (0,))
NT = ((1,), (1,))
TN = ((0,), (0,))
_DIMS = {"nn": NN, "nt": NT, "tn": TN}


def _sds(shape, dtype):
    return jax.ShapeDtypeStruct(shape, dtype)


def _params(*semantics):
    return pltpu.CompilerParams(dimension_semantics=semantics, vmem_limit_bytes=VMEM_LIMIT)


def _dot(a, b, dims):
    return lax.dot_general(a, b, (dims, ((), ())), preferred_element_type=F32)


@functools.partial(jax.custom_vjp, nondiff_argnums=(2,))
def _bdot(a, b, mode):
    return _dot(a.astype(BF16), b.astype(BF16), _DIMS[mode])


def _bdot_fwd(a, b, mode):
    return _bdot(a, b, mode), (a.astype(BF16), b.astype(BF16))


def _bdot_bwd(mode, res, g):
    a, b = res
    g = g.astype(BF16)
    if mode == "nn":
        return _dot(g, b, NT), _dot(a, g, TN)
    if mode == "nt":
        return _dot(g, b, NN), _dot(g, a, TN)
    return _dot(b, g, NT), _dot(a, g, NN)


_bdot.defvjp(_bdot_fwd, _bdot_bwd)


def _split(x):
    hi = x.astype(BF16)
    lo = (x - hi.astype(F32)).astype(BF16)
    return hi, lo


def _xdot_right(x, m, dims=NN):
    hi, lo = _split(x)
    return _dot(hi, m, dims) + _dot(lo, m, dims)


def _xdot_left(m, x, dims=NN):
    hi, lo = _split(x)
    return _dot(m, hi, dims) + _dot(m, lo, dims)


@jax.custom_vjp
def _xr(x, m):
    return _xdot_right(x, m)


def _xr_fwd(x, m):
    return _xdot_right(x, m), m


def _xr_bwd(m, g):
    return _xdot_right(g, m, NT), jnp.zeros_like(m)


_xr.defvjp(_xr_fwd, _xr_bwd)


def _norm_mod(x, g, sc, sh):
    r = lax.rsqrt(jnp.mean(x * x, axis=-1, keepdims=True) + EPS)
    return x * r * g * (1.0 + sc) + sh


def _matmul(a, b, mode, out_dtype, tm, tn, tk, name):
    if mode == "nn":
        (m, k), (_, n) = a.shape, b.shape
    elif mode == "nt":
        (m, k), (n, _) = a.shape, b.shape
    else:
        (k, m), (_, n) = a.shape, b.shape
    tm, tn, tk = min(tm, m), min(tn, n), min(tk, k)
    assert m % tm == 0 and n % tn == 0 and k % tk == 0, (name, m, n, k, tm, tn, tk)
    nk = k // tk
    dims = _DIMS[mode]

    def body(a_ref, b_ref, o_ref, acc_ref):
        kk = pl.program_id(2)

        @pl.when(kk == 0)
        def _():
            acc_ref[...] = jnp.zeros_like(acc_ref)

        acc_ref[...] += _dot(a_ref[...], b_ref[...], dims)

        @pl.when(kk == nk - 1)
        def _():
            o_ref[...] = acc_ref[...].astype(out_dtype)

    if mode == "tn":
        a_spec = pl.BlockSpec((tk, tm), lambda i, j, kk: (kk, i))
        b_spec = pl.BlockSpec((tk, tn), lambda i, j, kk: (kk, j))
    elif mode == "nn":
        a_spec = pl.BlockSpec((tm, tk), lambda i, j, kk: (i, kk))
        b_spec = pl.BlockSpec((tk, tn), lambda i, j, kk: (kk, j))
    else:
        a_spec = pl.BlockSpec((tm, tk), lambda i, j, kk: (i, kk))
        b_spec = pl.BlockSpec((tn, tk), lambda i, j, kk: (j, kk))
    return pl.pallas_call(
        body, grid=(m // tm, n // tn, nk), in_specs=[a_spec, b_spec],
        out_specs=pl.BlockSpec((tm, tn), lambda i, j, kk: (i, j)),
        out_shape=_sds((m, n), out_dtype), scratch_shapes=[pltpu.VMEM((tm, tn), F32)],
        compiler_params=_params("parallel", "parallel", "arbitrary"), name=name)(a, b)


ROW_T = 512


def _prenorm(x, pv, name):
    s, d = x.shape
    t = min(ROW_T, s)

    def body(x_ref, pv_ref, h_ref):
        h = _norm_mod(x_ref[...], pv_ref[6:7, :], pv_ref[1:2, :], pv_ref[0:1, :])
        h_ref[...] = h.astype(BF16)

    return pl.pallas_call(
        body, grid=(s // t,),
        in_specs=[pl.BlockSpec((t, d), lambda i: (i, 0)), pl.BlockSpec((16, d), lambda i: (0, 0))],
        out_specs=pl.BlockSpec((t, d), lambda i: (i, 0)), out_shape=_sds((s, d), BF16),
        compiler_params=_params("parallel"), name=name)(x, pv)


def _prenorm_bwd(dh, dres, x, pv, name):
    s, d = x.shape
    t = min(ROW_T, s)

    def body(dh_ref, dres_ref, x_ref, pv_ref, dx_ref, sg_ref):
        i = pl.program_id(0)

        @pl.when(i == 0)
        def _():
            sg_ref[...] = jnp.zeros_like(sg_ref)

        _, vjp = jax.vjp(_norm_mod, x_ref[...], pv_ref[6:7, :], pv_ref[1:2, :], pv_ref[0:1, :])
        dx, dg, dsc, dsh = vjp(dh_ref[...])
        dx_ref[...] = dres_ref[...] + dx
        sg_ref[0:1, :] += dsh
        sg_ref[1:2, :] += dsc
        sg_ref[2:3, :] += dg

    row = pl.BlockSpec((t, d), lambda i: (i, 0))
    return pl.pallas_call(
        body, grid=(s // t,),
        in_specs=[row, row, row, pl.BlockSpec((16, d), lambda i: (0, 0))],
        out_specs=[row, pl.BlockSpec((8, d), lambda i: (0, 0))],
        out_shape=[_sds((s, d), F32), _sds((8, d), F32)],
        compiler_params=_params("arbitrary"), name=name)(dh, dres, x, pv)


CONV_T = 256


def _conv_tile(a_ext, g_ext, w, b, ln_g, ln_b, n_out):
    u0 = a_ext * jax.nn.sigmoid(g_ext)
    off = CONV_HALO - (CONV_WIDTH - 1)
    acc = jnp.zeros((n_out, u0.shape[1]), F32) + b
    for k in range(CONV_WIDTH):
        acc = acc + w[k:k + 1, :] * u0[off + k: off + k + n_out, :]
    mu = jnp.mean(acc, axis=-1, keepdims=True)
    var = jnp.mean(jnp.square(acc - mu), axis=-1, keepdims=True)
    y = (acc - mu) * lax.rsqrt(var + EPS) * ln_g + ln_b
    return y * jax.nn.sigmoid(y)


def _conv_fwd(proj, conv_w, cp, name):
    s = proj.shape[0]
    t = min(CONV_T, s)
    c, h = CONV_CH, CONV_HALO

    def body(ap_ref, ac_ref, gp_ref, gc_ref, w_ref, cp_ref, o_ref):
        i = pl.program_id(0)
        live = (i > 0).astype(F32)
        a_ext = jnp.concatenate([ap_ref[t - h:, :] * live, ac_ref[...]], axis=0)
        g_ext = jnp.concatenate([gp_ref[t - h:, :], gc_ref[...]], axis=0)
        u = _conv_tile(a_ext, g_ext, w_ref[...], cp_ref[0:1, :], cp_ref[1:2, :], cp_ref[2:3, :], t)
        o_ref[...] = u.astype(BF16)

    prev = lambda col: pl.BlockSpec((t, c), lambda i: (jnp.maximum(i - 1, 0), col))
    cur = lambda col: pl.BlockSpec((t, c), lambda i: (i, col))
    return pl.pallas_call(
        body, grid=(s // t,),
        in_specs=[prev(0), cur(0), prev(1), cur(1),
                  pl.BlockSpec((CONV_WIDTH, c), lambda i: (0, 0)), pl.BlockSpec((8, c), lambda i: (0, 0))],
        out_specs=pl.BlockSpec((t, c), lambda i: (i, 0)), out_shape=_sds((s, c), BF16),
        compiler_params=_params("parallel"), name=name)(proj, proj, proj, proj, conv_w, cp)


def _conv_bwd(proj, do, conv_w, cp, name):
    s = proj.shape[0]
    t = min(CONV_T, s)
    c, h = CONV_CH, CONV_HALO
    nt = s // t

    def body(ap_ref, ac_ref, an_ref, gp_ref, gc_ref, gn_ref, doc_ref, don_ref, w_ref, cp_ref,
             da_ref, dg_ref, dw_ref, sg_ref):
        i = pl.program_id(0)

        @pl.when(i == 0)
        def _():
            dw_ref[...] = jnp.zeros_like(dw_ref)
            sg_ref[...] = jnp.zeros_like(sg_ref)

        first = (i > 0).astype(F32)
        last = (i < nt - 1).astype(F32)
        a_ext = jnp.concatenate([ap_ref[t - h:, :] * first, ac_ref[...], an_ref[:h, :] * last], axis=0)
        g_ext = jnp.concatenate([gp_ref[t - h:, :], gc_ref[...], gn_ref[:h, :]], axis=0)
        fn = functools.partial(_conv_tile, n_out=t + h)
        _, vjp = jax.vjp(fn, a_ext, g_ext, w_ref[...], cp_ref[0:1, :], cp_ref[1:2, :], cp_ref[2:3, :])
        ct_own = jnp.concatenate([doc_ref[...], jnp.zeros((h, c), F32)], axis=0)
        ct_all = jnp.concatenate([doc_ref[...], don_ref[:h, :] * last], axis=0)
        _, _, dw, db, dlg, dlb = vjp(ct_own)
        da, dg, _, _, _, _ = vjp(ct_all)
        da_ref[...] = da[h:h + t, :].astype(BF16)
        dg_ref[...] = dg[h:h + t, :].astype(BF16)
        dw_ref[...] += dw
        sg_ref[0:1, :] += db
        sg_ref[1:2, :] += dlg
        sg_ref[2:3, :] += dlb

    prev = lambda col: pl.BlockSpec((t, c), lambda i: (jnp.maximum(i - 1, 0), col))
    cur = lambda col: pl.BlockSpec((t, c), lambda i: (i, col))
    nxt = lambda col: pl.BlockSpec((t, c), lambda i: (jnp.minimum(i + 1, nt - 1), col))
    return pl.pallas_call(
        body, grid=(nt,),
        in_specs=[prev(0), cur(0), nxt(0), prev(1), cur(1), nxt(1), cur(0), nxt(0),
                  pl.BlockSpec((CONV_WIDTH, c), lambda i: (0, 0)), pl.BlockSpec((8, c), lambda i: (0, 0))],
        out_specs=[cur(0), cur(0), pl.BlockSpec((CONV_WIDTH, c), lambda i: (0, 0)),
                   pl.BlockSpec((8, c), lambda i: (0, 0))],
        out_shape=[_sds((s, c), BF16), _sds((s, c), BF16), _sds((CONV_WIDTH, c), F32), _sds((8, c), F32)],
        compiler_params=_params("arbitrary"), name=name)(proj, proj, proj, proj, proj, proj, do, do, conv_w, cp)


def _hgrn_levels(c):
    out, m = [], c // 2
    while m >= 1:
        out.append(m)
        m //= 2
    return out


def _hgrn_consts(c):
    t = np.arange(c)[:, None]
    j = np.arange(c)[None, :]
    mats = [j <= t, j > t]
    for m in _hgrn_levels(c):
        same = (t // m) == (j // m)
        mats += [same & (j <= t), same & (j > t)]
    return jnp.asarray(np.concatenate(mats, axis=0).astype(np.float32), dtype=BF16)


@jax.custom_vjp
def _cums(lc, mall):
    c = lc.shape[0]
    full = _xdot_left(mall, lc)
    return tuple(full[i * c:(i + 1) * c, :] for i in range(mall.shape[0] // c))


def _cums_fwd(lc, mall):
    return _cums(lc, mall), mall


def _cums_bwd(mall, cts):
    return _xdot_left(mall, jnp.concatenate(cts, axis=0), TN), jnp.zeros_like(mall)


_cums.defvjp(_cums_fwd, _cums_bwd)


def _hgrn_chunk(q, f, v, g, lbs, ng, st_in, mall):
    c = q.shape[0]
    keep = jax.nn.sigmoid(-f)
    if lbs:
        keep = (1.0 - jax.nn.sigmoid(lbs[1] - lbs[0])) * keep
    lc = jnp.log1p(-keep)
    qs = q * jax.nn.sigmoid(q)
    cs = _cums(lc, mall)
    o = _bdot(qs * jnp.exp(cs[0]), st_in, "nt")
    total = jnp.sum(lc, axis=0, keepdims=True)
    st_out = st_in * jnp.exp(total) + _bdot(v, keep * jnp.exp(cs[1]), "tn")
    r = lax.broadcasted_iota(jnp.int32, q.shape, 0)
    tt = lax.broadcasted_iota(jnp.int32, (c, c), 0)
    ss = lax.broadcasted_iota(jnp.int32, (c, c), 1)
    sc = jnp.where(tt == ss, jnp.sum(qs * keep, axis=-1, keepdims=True), 0.0)
    for li, m in enumerate(_hgrn_levels(c)):
        lg = m.bit_length() - 1
        odd = ((r >> lg) & 1) == 1
        qm = jnp.where(odd, qs * jnp.exp(cs[2 + 2 * li]), 0.0)
        km = jnp.where(odd, 0.0, keep * jnp.exp(cs[3 + 2 * li]))
        pair = (((tt >> lg) & 1) == 1) & ((ss >> lg) == (tt >> lg) - 1)
        sc = sc + jnp.where(pair, _bdot(qm, km, "nt"), 0.0)
    o = o + _bdot(sc, v, "nn")
    on = o * lax.rsqrt(jnp.mean(o * o, axis=-1, keepdims=True) + EPS) * ng
    return on * (g * jax.nn.sigmoid(g)), st_out


def _hgrn_fwd(proj, lb, ng, name):
    s = proj.shape[0]
    c = HG_CHUNK
    nc = s // c
    mall = _hgrn_consts(c)
    col0 = 1024 // HG_DK

    def body(*refs):
        q_ref, f_ref, v_ref, g_ref = refs[:4]
        if lb is None:
            ng_ref, m_ref, y_ref, st_ref, scr = refs[4:]
            lbs = ()
        else:
            lb_ref, ng_ref, m_ref, y_ref, st_ref, scr = refs[4:]
            lbs = (lb_ref[0:1, :], lb_ref[1:2, :])
        ci = pl.program_id(1)

        @pl.when(ci == 0)
        def _():
            scr[...] = jnp.zeros_like(scr)

        st_in = scr[...]
        st_ref[...] = st_in
        y, st_out = _hgrn_chunk(q_ref[...], f_ref[...], v_ref[...], g_ref[...], lbs, ng_ref[...], st_in, m_ref[...])
        y_ref[...] = y.astype(BF16)
        scr[...] = st_out

    col = lambda k: pl.BlockSpec((c, HG_DK), lambda h, ci: (ci, col0 + 4 * k + h))
    in_specs = [col(0), col(1), col(2), col(3)]
    args = [proj, proj, proj, proj]
    if lb is not None:
        in_specs.append(pl.BlockSpec((2, HG_DK), lambda h, ci: (0, h)))
        args.append(lb)
    in_specs += [pl.BlockSpec((1, HG_DK), lambda h, ci: (0, 0)), pl.BlockSpec(mall.shape, lambda h, ci: (0, 0))]
    args += [ng, mall]
    return pl.pallas_call(
        body, grid=(HG_HEADS, nc), in_specs=in_specs,
        out_specs=[pl.BlockSpec((c, HG_DK), lambda h, ci: (ci, h)),
                   pl.BlockSpec((None, None, HG_DK, HG_DK), lambda h, ci: (h, ci, 0, 0))],
        out_shape=[_sds((s, HG_HEADS * HG_DK), BF16), _sds((HG_HEADS, nc, HG_DK, HG_DK), F32)],
        scratch_shapes=[pltpu.VMEM((HG_DK, HG_DK), F32)],
        compiler_params=_params("parallel", "arbitrary"), name=name)(*args)


def _hgrn_bwd(proj, states, dy, lb, ng, name):
    s = proj.shape[0]
    c = HG_CHUNK
    nc = s // c
    mall = _hgrn_consts(c)
    col0 = 1024 // HG_DK

    def body(*refs):
        q_ref, f_ref, v_ref, g_ref, st_ref, dy_ref = refs[:6]
        if lb is None:
            ng_ref, m_ref, dq_ref, df_ref, dv_ref, dg_ref, dlb_ref, dng_ref, scr = refs[6:]
            lbs = ()
        else:
            lb_ref, ng_ref, m_ref, dq_ref, df_ref, dv_ref, dg_ref, dlb_ref, dng_ref, scr = refs[6:]
            lbs = (lb_ref[0:1, :], lb_ref[1:2, :])
        h = pl.program_id(0)
        ci = pl.program_id(1)

        @pl.when(ci == 0)
        def _():
            scr[...] = jnp.zeros_like(scr)
            dlb_ref[...] = jnp.zeros_like(dlb_ref)

        @pl.when((ci == 0) & (h == 0))
        def _():
            dng_ref[...] = jnp.zeros_like(dng_ref)

        mall_v = m_ref[...]
        fn = lambda q, f, v, g, lbs_, ng_, st: _hgrn_chunk(q, f, v, g, lbs_, ng_, st, mall_v)
        _, vjp = jax.vjp(fn, q_ref[...], f_ref[...], v_ref[...], g_ref[...], lbs, ng_ref[...], st_ref[...])
        dq, df, dv, dg, dlbs, dng, dst = vjp((dy_ref[...], scr[...]))
        dq_ref[...] = dq.astype(BF16)
        df_ref[...] = df.astype(BF16)
        dv_ref[...] = dv.astype(BF16)
        dg_ref[...] = dg.astype(BF16)
        scr[...] = dst
        dng_ref[0:1, :] += dng
        if lbs:
            dlb_ref[0:1, :] += dlbs[0]
            dlb_ref[1:2, :] += dlbs[1]

    rev = lambda ci: nc - 1 - ci
    col = lambda k: pl.BlockSpec((c, HG_DK), lambda h, ci: (rev(ci), col0 + 4 * k + h))
    out_col = pl.BlockSpec((c, HG_DK), lambda h, ci: (rev(ci), h))
    in_specs = [col(0), col(1), col(2), col(3),
                pl.BlockSpec((None, None, HG_DK, HG_DK), lambda h, ci: (h, rev(ci), 0, 0)), out_col]
    args = [proj, proj, proj, proj, states, dy]
    if lb is not None:
        in_specs.append(pl.BlockSpec((2, HG_DK), lambda h, ci: (0, h)))
        args.append(lb)
    in_specs += [pl.BlockSpec((1, HG_DK), lambda h, ci: (0, 0)), pl.BlockSpec(mall.shape, lambda h, ci: (0, 0))]
    args += [ng, mall]
    w = HG_HEADS * HG_DK
    return pl.pallas_call(
        body, grid=(HG_HEADS, nc), in_specs=in_specs,
        out_specs=[out_col, out_col, out_col, out_col,
                   pl.BlockSpec((2, HG_DK), lambda h, ci: (0, h)), pl.BlockSpec((8, HG_DK), lambda h, ci: (0, 0))],
        out_shape=[_sds((s, w), BF16)] * 4 + [_sds((2, w), F32), _sds((8, HG_DK), F32)],
        scratch_shapes=[pltpu.VMEM((HG_DK, HG_DK), F32)],
        compiler_params=_params("arbitrary", "arbitrary"), name=name)(*args)


def _head_avg():
    w = SB_HEADS * SB_DH
    i = np.arange(w)
    return jnp.asarray(((i[:, None] // SB_DH) == (i[None, :] // SB_DH)).astype(np.float32) / SB_DH, dtype=BF16)


def _sb_norm(x, g_tiled, avg):
    ms = _xr(x * x, avg)
    return x * lax.rsqrt(ms + EPS) * g_tiled


def _sb_prep(proj, gq, gk, name):
    s = proj.shape[0]
    t = min(ROW_T, s)
    w = SB_HEADS * SB_DH
    avg = _head_avg()

    def body(q_ref, k_ref, v_ref, gq_ref, gk_ref, avg_ref, qn_ref, kn_ref, vb_ref):
        qn_ref[...] = _sb_norm(q_ref[...], gq_ref[...], avg_ref[...]).astype(BF16)
        kn_ref[...] = _sb_norm(k_ref[...], gk_ref[...], avg_ref[...]).astype(BF16)
        vb_ref[...] = v_ref[...].astype(BF16)

    col = lambda k: pl.BlockSpec((t, w), lambda i: (i, 6 + k))
    vec = pl.BlockSpec((1, w), lambda i: (0, 0))
    out = pl.BlockSpec((t, w), lambda i: (i, 0))
    return pl.pallas_call(
        body, grid=(s // t,), in_specs=[col(0), col(1), col(2), vec, vec, pl.BlockSpec((w, w), lambda i: (0, 0))],
        out_specs=[out, out, out], out_shape=[_sds((s, w), BF16)] * 3,
        compiler_params=_params("parallel"), name=name)(proj, proj, proj, gq, gk, avg)


def _sb_prep_bwd(proj, dqn, dkn, gq, gk, name):
    s = proj.shape[0]
    t = min(ROW_T, s)
    w = SB_HEADS * SB_DH
    avg = _head_avg()

    def body(q_ref, k_ref, dqn_ref, dkn_ref, gq_ref, gk_ref, avg_ref, dq_ref, dk_ref, sg_ref):
        i = pl.program_id(0)

        @pl.when(i == 0)
        def _():
            sg_ref[...] = jnp.zeros_like(sg_ref)

        avg_v = avg_ref[...]
        fn = lambda x, g: _sb_norm(x, g, avg_v)
        _, vq = jax.vjp(fn, q_ref[...], gq_ref[...])
        dq, dgq = vq(dqn_ref[...])
        _, vk = jax.vjp(fn, k_ref[...], gk_ref[...])
        dk, dgk = vk(dkn_ref[...])
        dq_ref[...] = dq.astype(BF16)
        dk_ref[...] = dk.astype(BF16)
        sg_ref[0:1, :] += dgq
        sg_ref[1:2, :] += dgk

    col = lambda k: pl.BlockSpec((t, w), lambda i: (i, 6 + k))
    vec = pl.BlockSpec((1, w), lambda i: (0, 0))
    row = pl.BlockSpec((t, w), lambda i: (i, 0))
    return pl.pallas_call(
        body, grid=(s // t,),
        in_specs=[col(0), col(1), row, row, vec, vec, pl.BlockSpec((w, w), lambda i: (0, 0))],
        out_specs=[row, row, pl.BlockSpec((8, w), lambda i: (0, 0))],
        out_shape=[_sds((s, w), BF16), _sds((s, w), BF16), _sds((8, w), F32)],
        compiler_params=_params("arbitrary"), name=name)(proj, proj, dqn, dkn, gq, gk, avg)


def _sb_tri(kind):
    j = np.arange(SB_BLK)[:, None]
    s = np.arange(SB_BLK)[None, :]
    tri = (j > s) if kind == "suffix" else (j < s)
    return jnp.asarray(np.concatenate([tri, np.ones_like(tri)], axis=1).astype(np.float32), dtype=BF16)


def _sb_scores(qm, kblk, mask):
    z = _dot(qm, kblk, NT) * (SB_DH ** -0.5)
    sp = jnp.maximum(z, 0.0) + jnp.log1p(jnp.exp(-jnp.abs(z)))
    return z, sp, jnp.where(mask, -sp, 0.0)


def _sb_fwd(qn, kn, vb, name):
    s, w = qn.shape
    b = SB_BLK
    nq = s // b
    tri = _sb_tri("suffix")

    def body(q_ref, k_ref, v_ref, tri_ref, o_ref):
        i = pl.program_id(1)
        lane = lax.broadcasted_iota(jnp.int32, (b, b), 1)
        tt = lax.broadcasted_iota(jnp.int32, (b, b), 0)
        q = q_ref[...]
        out = jnp.zeros((b, b), F32)
        for hh in range(2):
            mine = (lane < SB_DH) if hh == 0 else (lane >= SB_DH)
            qm = jnp.where(mine, q, jnp.zeros_like(q))

            def cond(carry):
                j, run, _ = carry
                return (j <= i) & (jnp.max(run) > SB_DEAD)

            def step(carry):
                j, run, acc = carry
                kb = i - j
                off = pl.multiple_of(kb * b, b)
                kblk = k_ref[pl.ds(off, b), :]
                vblk = v_ref[pl.ds(off, b), :]
                mask = (kb < i) | (lane < tt)
                z, sp, lk = _sb_scores(qm, kblk, mask)
                both = _xdot_right(lk, tri_ref[...])
                a = jnp.where(mask, jnp.exp(z - sp + both[:, :b] + run), 0.0)
                return j + 1, run + both[:, b:], acc + _dot(a.astype(BF16), vblk, NN)

            zero = jnp.zeros((b, b), F32)
            _, _, acc = lax.while_loop(cond, step, (jnp.int32(0), zero, zero))
            out = out + jnp.where(mine, acc, 0.0)
        o_ref[...] = out.astype(BF16)

    blk = pl.BlockSpec((b, b), lambda p, i: (i, p))
    full = pl.BlockSpec((s, b), lambda p, i: (0, p))
    return pl.pallas_call(
        body, grid=(w // b, nq), in_specs=[blk, full, full, pl.BlockSpec(tri.shape, lambda p, i: (0, 0))],
        out_specs=blk, out_shape=_sds((s, w), BF16),
        compiler_params=_params("parallel", "arbitrary"), name=name)(qn, kn, vb, tri)


def _sb_bwd(qn, kn, vb, do, name):
    s, w = qn.shape
    b = SB_BLK
    nq = s // b
    tri_s = _sb_tri("suffix")
    tri_p = _sb_tri("prefix")

    def body(q_ref, k_ref, v_ref, do_ref, ts_ref, tp_ref, dq_ref, dk_ref, dv_ref, dk_acc, dv_acc, dp_scr):
        i = pl.program_id(1)

        @pl.when(i == 0)
        def _():
            dk_acc[...] = jnp.zeros_like(dk_acc)
            dv_acc[...] = jnp.zeros_like(dv_acc)

        lane = lax.broadcasted_iota(jnp.int32, (b, b), 1)
        tt = lax.broadcasted_iota(jnp.int32, (b, b), 0)
        q = q_ref[...]
        dout = do_ref[...].astype(BF16)
        zero = jnp.zeros((b, b), F32)
        dq_out = zero
        for hh in range(2):
            mine = (lane < SB_DH) if hh == 0 else (lane >= SB_DH)
            qm = jnp.where(mine, q, jnp.zeros_like(q))
            dom = jnp.where(mine, dout, jnp.zeros_like(dout))

            def cond(carry):
                j, run = carry
                return (j <= i) & (jnp.max(run) > SB_DEAD)

            def sweep_down(carry):
                j, run = carry
                kb = i - j
                off = pl.multiple_of(kb * b, b)
                kblk = k_ref[pl.ds(off, b), :]
                vblk = v_ref[pl.ds(off, b), :]
                mask = (kb < i) | (lane < tt)
                z, sp, lk = _sb_scores(qm, kblk, mask)
                both = _xdot_right(lk, ts_ref[...])
                a = jnp.where(mask, jnp.exp(z - sp + both[:, :b] + run), 0.0)
                dp_scr[kb] = _dot(dom, vblk, NT) * a
                dv_acc[pl.ds(off, b), :] += _dot(a.astype(BF16), dom, TN)
                return j + 1, run + both[:, b:]

            n_live, _ = lax.while_loop(cond, sweep_down, (jnp.int32(0), zero))

            def sweep_up(jj, carry):
                pre, dq = carry
                kb = i - n_live + 1 + jj
                off = pl.multiple_of(kb * b, b)
                kblk = k_ref[pl.ds(off, b), :]
                mask = (kb < i) | (lane < tt)
                z = _dot(qm, kblk, NT) * (SB_DH ** -0.5)
                sig = jax.nn.sigmoid(z)
                dp = dp_scr[kb]
                both = _xdot_right(dp, tp_ref[...])
                dz = jnp.where(mask, dp * (1.0 - sig) - sig * (both[:, :b] + pre), 0.0) * (SB_DH ** -0.5)
                dz = dz.astype(BF16)
                dk_acc[pl.ds(off, b), :] += _dot(dz, qm, TN)
                return pre + both[:, b:], dq + _dot(dz, kblk, NN)

            _, dq = lax.fori_loop(0, n_live, sweep_up, (zero, zero))
            dq_out = dq_out + jnp.where(mine, dq, 0.0)
        dq_ref[...] = dq_out

        @pl.when(i == nq - 1)
        def _():
            dk_ref[...] = dk_acc[...]
            dv_ref[...] = dv_acc[...].astype(BF16)

    blk = pl.BlockSpec((b, b), lambda p, i: (i, p))
    full = pl.BlockSpec((s, b), lambda p, i: (0, p))
    tri = pl.BlockSpec(tri_s.shape, lambda p, i: (0, 0))
    return pl.pallas_call(
        body, grid=(w // b, nq), in_specs=[blk, full, full, blk, tri, tri],
        out_specs=[blk, full, full], out_shape=[_sds((s, w), F32), _sds((s, w), F32), _sds((s, w), BF16)],
        scratch_shapes=[pltpu.VMEM((s, b), F32), pltpu.VMEM((s, b), F32), pltpu.VMEM((nq, b, b), F32)],
        compiler_params=_params("arbitrary", "arbitrary"), name=name)(qn, kn, vb, do, tri_s, tri_p)


MIX_T = 256
HALF = 512


def _gate_slices(ga, gb):
    return [(ga[:, 0:512], ga[:, 512:1024]), (ga[:, 1024:1536], gb[:, 0:512]), (gb[:, 512:1024], gb[:, 1024:1536])]


def _mix_fwd(u3, oh, osb, proj, x, pv, wc, wh, ws, wo, name):
    s, d = x.shape
    t = min(MIX_T, s)

    def body(u3_ref, oh_ref, os_ref, ga_ref, gb_ref, x_ref, pv_ref, wc_ref, wh_ref, ws_ref, wo_ref,
             x1_ref, h2_ref, mg_ref, mo_ref):
        ys = [_dot(u3_ref[...], wc_ref[...], NT), _dot(oh_ref[...], wh_ref[...], NT), _dot(os_ref[...], ws_ref[...], NT)]
        gl = _gate_slices(ga_ref[...], gb_ref[...])
        halves = []
        for hf in range(2):
            lo = hf * HALF
            acc = jnp.zeros((t, HALF), F32)
            for br in range(3):
                gate = jax.nn.sigmoid(gl[br][hf] + pv_ref[8 + br:9 + br, lo:lo + HALF])
                acc = acc + gate * ys[br][:, lo:lo + HALF]
            halves.append(acc)
        merged = jnp.concatenate(halves, axis=1).astype(BF16)
        mg_ref[...] = merged
        mo = _dot(merged, wo_ref[...], NN)
        mo_ref[...] = mo.astype(BF16)
        x1 = x_ref[...] + pv_ref[2:3, :] * mo
        x1_ref[...] = x1
        h2_ref[...] = _norm_mod(x1, pv_ref[7:8, :], pv_ref[4:5, :], pv_ref[3:4, :]).astype(BF16)

    br_spec = pl.BlockSpec((t, CONV_CH), lambda i: (i, 0))
    row = pl.BlockSpec((t, d), lambda i: (i, 0))
    wproj = pl.BlockSpec((d, CONV_CH), lambda i: (0, 0))
    return pl.pallas_call(
        body, grid=(s // t,),
        in_specs=[br_spec, br_spec, br_spec, pl.BlockSpec((t, 1536), lambda i: (i, 3)),
                  pl.BlockSpec((t, 1536), lambda i: (i, 4)), row, pl.BlockSpec((16, d), lambda i: (0, 0)),
                  wproj, wproj, wproj, pl.BlockSpec((d, d), lambda i: (0, 0))],
        out_specs=[row, row, row, row],
        out_shape=[_sds((s, d), F32), _sds((s, d), BF16), _sds((s, d), BF16), _sds((s, d), BF16)],
        compiler_params=_params("parallel"), name=name)(u3, oh, osb, proj, proj, x, pv, wc, wh, ws, wo)


def _mix_bwd(dx1, mo1, u3, oh, osb, proj, pv, wc, wh, ws, wo, name):
    s, d = dx1.shape
    t = min(MIX_T, s)

    def body(dx_ref, mo_ref, u3_ref, oh_ref, os_ref, ga_ref, gb_ref, pv_ref, wc_ref, wh_ref, ws_ref, wo_ref,
             dmo_ref, dyc_ref, dyh_ref, dys_ref, doc_ref, doh_ref, dos_ref, dgl_ref, sg_ref):
        i = pl.program_id(0)

        @pl.when(i == 0)
        def _():
            sg_ref[...] = jnp.zeros_like(sg_ref)

        dx = dx_ref[...]
        dmo = (dx * pv_ref[2:3, :]).astype(BF16)
        dmo_ref[...] = dmo
        sg_ref[0:1, :] += jnp.sum(dx * mo_ref[...].astype(F32), axis=0, keepdims=True)
        dmerged = _dot(dmo, wo_ref[...], NT)
        branches = [(u3_ref, wc_ref, dyc_ref, doc_ref), (oh_ref, wh_ref, dyh_ref, doh_ref), (os_ref, ws_ref, dys_ref, dos_ref)]
        gl = _gate_slices(ga_ref[...], gb_ref[...])
        for br, (o_ref, w_ref, dy_ref, do_ref) in enumerate(branches):
            y = _dot(o_ref[...], w_ref[...], NT)
            dys = []
            for hf in range(2):
                lo = hf * HALF
                gate = jax.nn.sigmoid(gl[br][hf] + pv_ref[8 + br:9 + br, lo:lo + HALF])
                dm = dmerged[:, lo:lo + HALF]
                dys.append(dm * gate)
                dgl = dm * y[:, lo:lo + HALF] * gate * (1.0 - gate)
                dgl_ref[:, br * d + lo: br * d + lo + HALF] = dgl.astype(BF16)
                sg_ref[1 + br:2 + br, lo:lo + HALF] += jnp.sum(dgl, axis=0, keepdims=True)
            dy = jnp.concatenate(dys, axis=1).astype(BF16)
            dy_ref[...] = dy
            do_ref[...] = _dot(dy, w_ref[...], NN)

    br_spec = pl.BlockSpec((t, CONV_CH), lambda i: (i, 0))
    row = pl.BlockSpec((t, d), lambda i: (i, 0))
    wproj = pl.BlockSpec((d, CONV_CH), lambda i: (0, 0))
    return pl.pallas_call(
        body, grid=(s // t,),
        in_specs=[row, row, br_spec, br_spec, br_spec, pl.BlockSpec((t, 1536), lambda i: (i, 3)),
                  pl.BlockSpec((t, 1536), lambda i: (i, 4)), pl.BlockSpec((16, d), lambda i: (0, 0)),
                  wproj, wproj, wproj, pl.BlockSpec((d, d), lambda i: (0, 0))],
        out_specs=[row, row, row, row, br_spec, br_spec, br_spec, pl.BlockSpec((t, 3 * d), lambda i: (i, 0)),
                   pl.BlockSpec((8, d), lambda i: (0, 0))],
        out_shape=[_sds((s, d), BF16)] * 4 + [_sds((s, CONV_CH), F32)] * 3 + [_sds((s, 3 * d), BF16), _sds((8, d), F32)],
        compiler_params=_params("arbitrary"), name=name)(dx1, mo1, u3, oh, osb, proj, proj, pv, wc, wh, ws, wo)


MLP_T = 512
MLP_F = 512


def _mlp_fwd(h2, x1, pv, w1t, w2, name):
    s, d = x1.shape
    t = min(MLP_T, s)
    nf = D_FF // MLP_F

    def body(h_ref, x_ref, pv_ref, w1_ref, w2_ref, x2_ref, mo_ref, acc_ref):
        f = pl.program_id(1)

        @pl.when(f == 0)
        def _():
            acc_ref[...] = jnp.zeros_like(acc_ref)

        a = jnp.maximum(_dot(h_ref[...], w1_ref[...], NT), 0.0)
        acc_ref[...] += _dot((a * a).astype(BF16), w2_ref[...], NN)

        @pl.when(f == nf - 1)
        def _():
            mo = acc_ref[...]
            mo_ref[...] = mo.astype(BF16)
            x2_ref[...] = x_ref[...] + pv_ref[5:6, :] * mo

    row = pl.BlockSpec((t, d), lambda i, f: (i, 0))
    wblk = pl.BlockSpec((MLP_F, d), lambda i, f: (f, 0))
    return pl.pallas_call(
        body, grid=(s // t, nf), in_specs=[row, row, pl.BlockSpec((16, d), lambda i, f: (0, 0)), wblk, wblk],
        out_specs=[row, row], out_shape=[_sds((s, d), F32), _sds((s, d), BF16)],
        scratch_shapes=[pltpu.VMEM((t, d), F32)],
        compiler_params=_params("parallel", "arbitrary"), name=name)(h2, x1, pv, w1t, w2)


def _mlp_bwd(dx2, h2, x1, mo2, pv, w1t, w2, name):
    s, d = x1.shape
    t = min(MLP_T, s)
    nf = D_FF // MLP_F

    def body(dx_ref, h_ref, x_ref, mo_ref, pv_ref, w1_ref, w2_ref, dx1_ref, da_ref, b_ref, dmo_ref, sg_ref, acc_ref):
        i = pl.program_id(0)
        f = pl.program_id(1)

        @pl.when((i == 0) & (f == 0))
        def _():
            sg_ref[...] = jnp.zeros_like(sg_ref)

        @pl.when(f == 0)
        def _():
            acc_ref[...] = jnp.zeros_like(acc_ref)
            dx = dx_ref[...]
            dmo_ref[...] = (dx * pv_ref[5:6, :]).astype(BF16)
            sg_ref[0:1, :] += jnp.sum(dx * mo_ref[...].astype(F32), axis=0, keepdims=True)

        r = jnp.maximum(_dot(h_ref[...], w1_ref[...], NT), 0.0)
        b_ref[...] = (r * r).astype(BF16)
        da = (_dot(dmo_ref[...], w2_ref[...], NT) * (2.0 * r)).astype(BF16)
        da_ref[...] = da
        acc_ref[...] += _dot(da, w1_ref[...], NN)

        @pl.when(f == nf - 1)
        def _():
            _, vjp = jax.vjp(_norm_mod, x_ref[...], pv_ref[7:8, :], pv_ref[4:5, :], pv_ref[3:4, :])
            dxn, dg, dsc, dsh = vjp(acc_ref[...])
            dx1_ref[...] = dx_ref[...] + dxn
            sg_ref[1:2, :] += dsh
            sg_ref[2:3, :] += dsc
            sg_ref[3:4, :] += dg

    row = pl.BlockSpec((t, d), lambda i, f: (i, 0))
    wblk = pl.BlockSpec((MLP_F, d), lambda i, f: (f, 0))
    hid = pl.BlockSpec((t, MLP_F), lambda i, f: (i, f))
    return pl.pallas_call(
        body, grid=(s // t, nf),
        in_specs=[row, row, row, row, pl.BlockSpec((16, d), lambda i, f: (0, 0)), wblk, wblk],
        out_specs=[row, hid, hid, row, pl.BlockSpec((8, d), lambda i, f: (0, 0))],
        out_shape=[_sds((s, d), F32), _sds((s, D_FF), BF16), _sds((s, D_FF), BF16), _sds((s, d), BF16), _sds((8, d), F32)],
        scratch_shapes=[pltpu.VMEM((t, d), F32)],
        compiler_params=_params("arbitrary", "arbitrary"), name=name)(dx2, h2, x1, mo2, pv, w1t, w2)


def _loss_head(y, target, name):
    s, d = y.shape
    t = min(ROW_T, s)

    def body(y_ref, t_ref, dy_ref, ls_ref):
        i = pl.program_id(0)

        @pl.when(i == 0)
        def _():
            ls_ref[...] = jnp.zeros_like(ls_ref)

        e = y_ref[...] - t_ref[...]
        dy_ref[...] = e * (1.0 / d)
        ls_ref[...] += jnp.sum((e * e).reshape(t // 8, 8, d), axis=0)

    row = pl.BlockSpec((t, d), lambda i: (i, 0))
    return pl.pallas_call(
        body, grid=(s // t,), in_specs=[row, row], out_specs=[row, pl.BlockSpec((8, d), lambda i: (0, 0))],
        out_shape=[_sds((s, d), F32), _sds((8, d), F32)],
        compiler_params=_params("arbitrary"), name=name)(y, target)


def _layer_vectors(l, mod, sm):
    d = D_MODEL
    pv = jnp.concatenate([mod[l].reshape(6, d), sm["norm1_g"][l][None], sm["norm2_g"][l][None],
                          sm["gate_b"][l].reshape(3, d), jnp.zeros((5, d), F32)], axis=0)
    cp = jnp.concatenate([sm["conv_b"][l][None], sm["conv_ln_g"][l][None], sm["conv_ln_b"][l][None],
                          jnp.zeros((5, CONV_CH), F32)], axis=0)
    return dict(pv=pv, cp=cp, conv_w=sm["conv_w"][l], lb=(sm["hgrn_lb"] if l > 0 else None),
                ng=sm["hgrn_norm_g"][l][None], gq=jnp.tile(sm["sb_qn_g"][l], SB_HEADS)[None],
                gk=jnp.tile(sm["sb_kn_g"][l], SB_HEADS)[None])


def _layer_fwd(x, vec, w, tag):
    h = _prenorm(x, vec["pv"], f"prenorm{tag}")
    proj = _matmul(h, w["win_t"], "nt", F32, 512, 768, 1024, f"proj{tag}")
    u3 = _conv_fwd(proj, vec["conv_w"], vec["cp"], f"conv_fwd{tag}")
    oh, states = _hgrn_fwd(proj, vec["lb"], vec["ng"], f"hgrn_fwd{tag}")
    qn, kn, vb = _sb_prep(proj, vec["gq"], vec["gk"], f"sb_prep{tag}")
    osb = _sb_fwd(qn, kn, vb, f"sb_fwd{tag}")
    x1, h2, merged, mo1 = _mix_fwd(u3, oh, osb, proj, x, vec["pv"], w["wc_t"], w["wh_t"], w["ws_t"], w["wo"], f"mix_fwd{tag}")
    x2, mo2 = _mlp_fwd(h2, x1, vec["pv"], w["w1_t"], w["w2"], f"mlp_fwd{tag}")
    saved = dict(x=x, h=h, proj=proj, u3=u3, oh=oh, states=states, qn=qn, kn=kn, vb=vb, osb=osb,
                 x1=x1, h2=h2, merged=merged, mo1=mo1, mo2=mo2)
    return x2, saved


def _layer_bwd(dx2, sv, vec, w, tag):
    pv = vec["pv"]
    dx1, da, bsq, dmo2, sg_mlp = _mlp_bwd(dx2, sv["h2"], sv["x1"], sv["mo2"], pv, w["w1_t"], w["w2"], f"mlp_bwd{tag}")
    big = {}
    big["w1_t"] = _matmul(da, sv["h2"], "tn", BF16, 512, 1024, 512, f"dw1{tag}")
    big["w2"] = _matmul(bsq, dmo2, "tn", BF16, 512, 1024, 512, f"dw2{tag}")
    dmo1, dyc, dyh, dys, doc, doh, dos, dgl, sg_mix = _mix_bwd(
        dx1, sv["mo1"], sv["u3"], sv["oh"], sv["osb"], sv["proj"], pv, w["wc_t"], w["wh_t"], w["ws_t"], w["wo"], f"mix_bwd{tag}")
    big["wo"] = _matmul(sv["merged"], dmo1, "tn", BF16, 512, 1024, 512, f"dwo{tag}")
    big["wc_t"] = _matmul(dyc, sv["u3"], "tn", BF16, 512, 512, 512, f"dwc{tag}")
    big["wh_t"] = _matmul(dyh, sv["oh"], "tn", BF16, 512, 512, 512, f"dwh{tag}")
    big["ws_t"] = _matmul(dys, sv["osb"], "tn", BF16, 512, 512, 512, f"dws{tag}")
    da_c, dg_c, dconv_w, sg_conv = _conv_bwd(sv["proj"], doc, vec["conv_w"], vec["cp"], f"conv_bwd{tag}")
    dq_h, df_h, di_h, dg_h, dlb, dng = _hgrn_bwd(sv["proj"], sv["states"], doh, vec["lb"], vec["ng"], f"hgrn_bwd{tag}")
    dqn, dkn, dv_s = _sb_bwd(sv["qn"], sv["kn"], sv["vb"], dos, f"sb_bwd{tag}")
    dq_s, dk_s, sg_sb = _sb_prep_bwd(sv["proj"], dqn, dkn, vec["gq"], vec["gk"], f"sb_prep_bwd{tag}")
    dproj = jnp.concatenate([da_c, dg_c, dq_h, df_h, di_h, dg_h, dq_s, dk_s, dv_s, dgl], axis=1)
    dh = _matmul(dproj, w["win_t"], "nn", F32, 512, 1024, 768, f"dh{tag}")
    big["win_t"] = _matmul(dproj, sv["h"], "tn", BF16, 768, 1024, 512, f"dwin{tag}")
    dx, sg_pre = _prenorm_bwd(dh, dx1, sv["x"], pv, f"prenorm_bwd{tag}")
    small = dict(
        mod=jnp.stack([sg_pre[0], sg_pre[1], sg_mix[0], sg_mlp[1], sg_mlp[2], sg_mlp[0]]).reshape(6 * D_MODEL),
        norm1_g=sg_pre[2], norm2_g=sg_mlp[3], gate_b=sg_mix[1:4].reshape(3 * D_MODEL),
        conv_w=dconv_w, conv_b=sg_conv[0], conv_ln_g=sg_conv[1], conv_ln_b=sg_conv[2],
        hgrn_lb=dlb, hgrn_norm_g=dng[0],
        sb_qn_g=sg_sb[0].reshape(SB_HEADS, SB_DH).sum(0), sb_kn_g=sg_sb[1].reshape(SB_HEADS, SB_DH).sum(0))
    return dx, big, small


def _local_step(x, target, mod, sm, wts):
    vecs = [_layer_vectors(l, mod, sm) for l in range(DEPTH)]
    saved = []
    y = x
    for l in range(DEPTH):
        y, sv = _layer_fwd(y, vecs[l], wts[l], f"_l{l}")
        saved.append(sv)
    dy, sq = _loss_head(y, target, "loss_head")
    bigs, smalls = [None] * DEPTH, [None] * DEPTH
    for l in reversed(range(DEPTH)):
        dy, bigs[l], smalls[l] = _layer_bwd(dy, saved[l], vecs[l], wts[l], f"_l{l}")
    return sq, dy, bigs, smalls


MESH = pl.DeviceIdType.MESH
HBM_SPEC = pl.BlockSpec(memory_space=pltpu.HBM)


def _mesh_place():
    return lax.axis_index("x"), lax.axis_index("y"), lax.axis_index("c")


def _block_of(px, py, pc):
    return 4 * px + 2 * py + pc


def _gather(xs, name):
    n = len(xs)

    def body(*refs):
        x_refs, out_refs = refs[:n], refs[n:2 * n]
        send_sems, recv_sems, local_sems = refs[2 * n:]
        x, y, c = _mesh_place()
        me, sibling = (x, y, c), (x, y, 1 - c)
        chips = [(1 - x, y), (x, 1 - y), (1 - x, 1 - y)]

        def copy(a, k, block, to, src=None):
            rows = out_refs[a].at[_block_of(*block)]
            return pltpu.make_async_remote_copy(
                src_ref=rows if src is None else src, dst_ref=rows, send_sem=send_sems.at[a, k],
                recv_sem=recv_sems.at[a, k], device_id=to, device_id_type=MESH)

        local = [pltpu.make_async_copy(x_refs[a], out_refs[a].at[_block_of(*me)], local_sems.at[a]) for a in range(n)]
        first = []
        for a in range(n):
            local[a].start()
            first.append(copy(a, 0, me, sibling, src=x_refs[a]))
            first += [copy(a, 1 + j, me, (*chip, c), src=x_refs[a]) for j, chip in enumerate(chips)]
        for cp in first:
            cp.start()
        passed = []
        for j, chip in enumerate(chips):
            for a in range(n):
                copy(a, 1 + j, (*chip, c), me).wait_recv()
                fwd = copy(a, 4 + j, (*chip, c), sibling)
                fwd.start()
                passed.append(fwd)
        for a in range(n):
            copy(a, 0, sibling, me).wait_recv()
            for j, chip in enumerate(chips):
                copy(a, 4 + j, (*chip, 1 - c), me).wait_recv()
        for cp in first + passed:
            cp.wait_send()
        for cp in local:
            cp.wait()

    return pl.pallas_call(
        body, in_specs=[HBM_SPEC] * n, out_specs=[HBM_SPEC] * n,
        out_shape=[_sds((N_DEV, *v.shape), v.dtype) for v in xs],
        scratch_shapes=[pltpu.SemaphoreType.DMA((n, 7)), pltpu.SemaphoreType.DMA((n, 7)), pltpu.SemaphoreType.DMA((n,))],
        name=name)(*xs)


def _exchange(packs, name):
    n = len(packs)

    def body(*refs):
        in_refs, out_refs = refs[:n], refs[n:2 * n]
        send_sems, recv_sems, local_sems = refs[2 * n:]
        x, y, c = _mesh_place()
        mine = _block_of(x, y, c)
        peers = []
        for k in range(1, N_DEV):
            peers.append((1 - x if k & 4 else x, 1 - y if k & 2 else y, 1 - c if k & 1 else c))

        def copy(a, k):
            peer = peers[k]
            return pltpu.make_async_remote_copy(
                src_ref=in_refs[a].at[_block_of(*peer)], dst_ref=out_refs[a].at[mine], send_sem=send_sems.at[a, k],
                recv_sem=recv_sems.at[a, k], device_id=peer, device_id_type=MESH)

        def arrival(a, k):
            slot = out_refs[a].at[_block_of(*peers[k])]
            return pltpu.make_async_remote_copy(
                src_ref=slot, dst_ref=slot, send_sem=send_sems.at[a, k], recv_sem=recv_sems.at[a, k],
                device_id=peers[k], device_id_type=MESH)

        local = [pltpu.make_async_copy(in_refs[a].at[mine], out_refs[a].at[mine], local_sems.at[a]) for a in range(n)]
        sends = [copy(a, k) for a in range(n) for k in range(N_DEV - 1)]
        for cp in local + sends:
            cp.start()
        for a in range(n):
            for k in range(N_DEV - 1):
                arrival(a, k).wait_recv()
        for cp in sends:
            cp.wait_send()
        for cp in local:
            cp.wait()

    return pl.pallas_call(
        body, in_specs=[HBM_SPEC] * n, out_specs=[HBM_SPEC] * n,
        out_shape=[_sds(v.shape, v.dtype) for v in packs],
        scratch_shapes=[pltpu.SemaphoreType.DMA((n, 7)), pltpu.SemaphoreType.DMA((n, 7)), pltpu.SemaphoreType.DMA((n,))],
        name=name)(*packs)


def _row_tile(r, cap=512):
    t = min(r, cap)
    while r % t or (t % 8 and t != r):
        t -= 1
    return t


def _sum8(z, name):
    _, r, c = z.shape
    t = _row_tile(r, 128 if c >= 1024 else 512)

    def body(z_ref, o_ref):
        acc = z_ref[0].astype(F32)
        for j in range(1, N_DEV):
            acc = acc + z_ref[j].astype(F32)
        o_ref[...] = acc

    return pl.pallas_call(
        body, grid=(r // t,), in_specs=[pl.BlockSpec((N_DEV, t, c), lambda i: (0, i, 0))],
        out_specs=pl.BlockSpec((t, c), lambda i: (i, 0)), out_shape=_sds((r, c), F32),
        compiler_params=_params("parallel"), name=name)(z)


def _adamw(w, g, m, v, name):
    r, c = w.shape
    t = _row_tile(r, 256)

    def body(w_ref, g_ref, m_ref, v_ref, d_ref, nm_ref, nv_ref):
        g_ = g_ref[...]
        nm = ADAM_B1 * m_ref[...] + (1.0 - ADAM_B1) * g_
        nv = ADAM_B2 * v_ref[...] + (1.0 - ADAM_B2) * jnp.square(g_)
        m_hat = nm / (1.0 - ADAM_B1 ** ADAM_STEP)
        v_hat = nv / (1.0 - ADAM_B2 ** ADAM_STEP)
        d_ref[...] = -ADAM_LR * (m_hat / (jnp.sqrt(v_hat) + ADAM_EPS) + ADAM_WD * w_ref[...])
        nm_ref[...] = nm
        nv_ref[...] = nv

    blk = pl.BlockSpec((t, c), lambda i: (i, 0))
    return pl.pallas_call(
        body, grid=(r // t,), in_specs=[blk] * 4, out_specs=[blk] * 3, out_shape=[_sds((r, c), F32)] * 3,
        compiler_params=_params("parallel"), name=name)(w, g, m, v)


def _mod_local(c_all, mod_w, name):
    depth, d, cols = mod_w.shape

    def body(c_ref, w_ref, o_ref):
        cv = c_ref[...]
        act = cv * jax.nn.sigmoid(cv)
        o_ref[...] = jnp.dot(act, w_ref[...], precision=lax.Precision.HIGHEST, preferred_element_type=F32)

    return pl.pallas_call(
        body, grid=(depth,),
        in_specs=[pl.BlockSpec((N_DEV, d), lambda l: (0, 0)), pl.BlockSpec((None, d, cols), lambda l: (l, 0, 0))],
        out_specs=pl.BlockSpec((None, N_DEV, cols), lambda l: (l, 0, 0)), out_shape=_sds((depth, N_DEV, cols), F32),
        compiler_params=_params("parallel"), name=name)(c_all, mod_w)


def _modw_grad(c_all, dmod, name):
    depth, _, cols = dmod.shape
    d = c_all.shape[1]

    def body(c_ref, g_ref, o_ref):
        cv = c_ref[...]
        act = cv * jax.nn.sigmoid(cv)
        o_ref[...] = lax.dot_general(act, g_ref[...], (TN, ((), ())), precision=lax.Precision.HIGHEST,
                                     preferred_element_type=F32)

    return pl.pallas_call(
        body, grid=(depth,),
        in_specs=[pl.BlockSpec((N_DEV, d), lambda l: (0, 0)), pl.BlockSpec((None, N_DEV, cols), lambda l: (l, 0, 0))],
        out_specs=pl.BlockSpec((None, d, cols), lambda l: (l, 0, 0)), out_shape=_sds((depth, d, cols), F32),
        compiler_params=_params("parallel"), name=name)(c_all, dmod)


LANE = 128
PACK1 = (("w_in", 960, True), ("w_out", 128, False), ("mlp_w2", 512, False), ("mlp_w1", 512, True))
PACK2 = (("w_conv_proj", 128, True), ("w_hgrn_proj", 128, True), ("w_sb_proj", 128, True))
BIG_KEY = {"w_in": "win_t", "w_out": "wo", "mlp_w2": "w2", "mlp_w1": "w1_t",
           "w_conv_proj": "wc_t", "w_hgrn_proj": "wh_t", "w_sb_proj": "ws_t"}
SMALL = (("mod_b", 6144), ("norm1_g", 1024), ("gate_b", 3072), ("conv_w", CONV_WIDTH * CONV_CH), ("conv_b", 512),
         ("conv_ln_g", 512), ("conv_ln_b", 512), ("hgrn_lb", 512), ("hgrn_norm_g", 128), ("sb_qn_g", 64),
         ("sb_kn_g", 64), ("norm2_g", 1024))


def _pack_rows(parts, width):
    flat = jnp.concatenate([p.reshape(-1) for p in parts])
    rows = -(-flat.shape[0] // width)
    rows = -(-rows // 8) * 8
    return jnp.pad(flat, (0, rows * width - flat.shape[0])).reshape(rows, width)


def _pack_weights(spec, params):
    parts = []
    for name, _, transposed in spec:
        for l in range(DEPTH):
            w = params[name][l]
            parts.append((w.T if transposed else w).astype(BF16))
    return jnp.concatenate(parts, axis=0)


def _unpack_gathered(spec, g):
    out = [{} for _ in range(DEPTH)]
    off = 0
    for name, rows, _ in spec:
        for l in range(DEPTH):
            out[l][BIG_KEY[name]] = g[:, off:off + rows].reshape(N_DEV * rows, g.shape[2])
            off += rows
    return out


def _pack_grads(spec, bigs):
    parts = []
    for name, rows, _ in spec:
        for l in range(DEPTH):
            gmat = bigs[l][BIG_KEY[name]]
            parts.append(gmat.reshape(N_DEV, rows, gmat.shape[1]))
    return jnp.concatenate(parts, axis=1)


def _unpack_shard_grads(spec, gsum):
    out = {}
    off = 0
    for name, rows, transposed in spec:
        per_layer = []
        for l in range(DEPTH):
            blk = gsum[off:off + rows]
            per_layer.append(blk.T if transposed else blk)
            off += rows
        out[name] = jnp.stack(per_layer)
    return out


def _adamw_nd(w, g, m, v, name):
    shape = w.shape
    two = lambda a: a.reshape(-1, shape[-1])
    return [o.reshape(shape) for o in _adamw(two(w), two(g), two(m), two(v), name)]


WEIGHTS = ("mod_w", "mod_b", "norm1_g", "w_in", "gate_b", "conv_w", "conv_b", "conv_ln_g", "conv_ln_b", "w_conv_proj",
           "hgrn_lb", "hgrn_norm_g", "w_hgrn_proj", "sb_qn_g", "sb_kn_g", "w_sb_proj", "w_out", "norm2_g", "mlp_w1",
           "mlp_w2")


def kernel(x, c, mod_w, mod_b, norm1_g, w_in, gate_b, conv_w, conv_b, conv_ln_g, conv_ln_b, w_conv_proj, hgrn_lb, hgrn_norm_g, w_hgrn_proj, sb_qn_g, sb_kn_g, w_sb_proj, w_out, norm2_g, mlp_w1, mlp_w2, loss_target, m_mod_w, m_mod_b, m_norm1_g, m_w_in, m_gate_b, m_conv_w, m_conv_b, m_conv_ln_g, m_conv_ln_b, m_w_conv_proj, m_hgrn_lb, m_hgrn_norm_g, m_w_hgrn_proj, m_sb_qn_g, m_sb_kn_g, m_w_sb_proj, m_w_out, m_norm2_g, m_mlp_w1, m_mlp_w2, v_mod_w, v_mod_b, v_norm1_g, v_w_in, v_gate_b, v_conv_w, v_conv_b, v_conv_ln_g, v_conv_ln_b, v_w_conv_proj, v_hgrn_lb, v_hgrn_norm_g, v_w_hgrn_proj, v_sb_qn_g, v_sb_kn_g, v_w_sb_proj, v_w_out, v_norm2_g, v_mlp_w1, v_mlp_w2):
    params = dict(mod_w=mod_w, mod_b=mod_b, norm1_g=norm1_g, w_in=w_in, gate_b=gate_b, conv_w=conv_w, conv_b=conv_b,
                  conv_ln_g=conv_ln_g, conv_ln_b=conv_ln_b, w_conv_proj=w_conv_proj, hgrn_lb=hgrn_lb,
                  hgrn_norm_g=hgrn_norm_g, w_hgrn_proj=w_hgrn_proj, sb_qn_g=sb_qn_g, sb_kn_g=sb_kn_g,
                  w_sb_proj=w_sb_proj, w_out=w_out, norm2_g=norm2_g, mlp_w1=mlp_w1, mlp_w2=mlp_w2)
    mom1 = dict(mod_w=m_mod_w, mod_b=m_mod_b, norm1_g=m_norm1_g, w_in=m_w_in, gate_b=m_gate_b, conv_w=m_conv_w,
                conv_b=m_conv_b, conv_ln_g=m_conv_ln_g, conv_ln_b=m_conv_ln_b, w_conv_proj=m_w_conv_proj,
                hgrn_lb=m_hgrn_lb, hgrn_norm_g=m_hgrn_norm_g, w_hgrn_proj=m_w_hgrn_proj, sb_qn_g=m_sb_qn_g,
                sb_kn_g=m_sb_kn_g, w_sb_proj=m_w_sb_proj, w_out=m_w_out, norm2_g=m_norm2_g, mlp_w1=m_mlp_w1,
                mlp_w2=m_mlp_w2)
    mom2 = dict(mod_w=v_mod_w, mod_b=v_mod_b, norm1_g=v_norm1_g, w_in=v_w_in, gate_b=v_gate_b, conv_w=v_conv_w,
                conv_b=v_conv_b, conv_ln_g=v_conv_ln_g, conv_ln_b=v_conv_ln_b, w_conv_proj=v_w_conv_proj,
                hgrn_lb=v_hgrn_lb, hgrn_norm_g=v_hgrn_norm_g, w_hgrn_proj=v_w_hgrn_proj, sb_qn_g=v_sb_qn_g,
                sb_kn_g=v_sb_kn_g, w_sb_proj=v_w_sb_proj, w_out=v_w_out, norm2_g=v_norm2_g, mlp_w1=v_mlp_w1,
                mlp_w2=v_mlp_w2)
    xi, yi, ci = _mesh_place()
    me = _block_of(xi, yi, ci)
    cw_cols = conv_w.shape[2]

    tiny = _pack_rows([c, conv_w], LANE)
    g_tiny, g1, g2 = _gather([tiny, _pack_weights(PACK1, params), _pack_weights(PACK2, params)], "gather_weights")
    c_rows = D_MODEL // LANE
    c_all = g_tiny[:, :c_rows].reshape(N_DEV, D_MODEL)
    n_cw = DEPTH * CONV_WIDTH * cw_cols
    conv_w_full = g_tiny[:, c_rows:c_rows + n_cw // LANE].reshape(N_DEV, DEPTH, CONV_WIDTH, cw_cols)
    conv_w_full = conv_w_full.transpose(1, 2, 0, 3).reshape(DEPTH, CONV_WIDTH, CONV_CH)
    wts = _unpack_gathered(PACK1, g1)
    for l, extra in enumerate(_unpack_gathered(PACK2, g2)):
        wts[l].update(extra)

    (g_mod,) = _gather([_mod_local(c_all, mod_w, "mod_local")], "gather_mod")
    mod = lax.dynamic_index_in_dim(g_mod, me, axis=2, keepdims=False)
    mod = mod.transpose(1, 0, 2).reshape(DEPTH, 6 * D_MODEL) + mod_b

    sm = dict(norm1_g=norm1_g, norm2_g=norm2_g, gate_b=gate_b, conv_w=conv_w_full, conv_b=conv_b, conv_ln_g=conv_ln_g,
              conv_ln_b=conv_ln_b, hgrn_lb=hgrn_lb, hgrn_norm_g=hgrn_norm_g, sb_qn_g=sb_qn_g, sb_kn_g=sb_kn_g)
    sq, dx, bigs, smalls = _local_step(x[0], loss_target[0], mod, sm, wts)
    loss = lax.psum(0.5 * jnp.sum(sq) / D_MODEL, ("x", "y", "c"))

    r1, r2 = _exchange([_pack_grads(PACK1, bigs), _pack_grads(PACK2, bigs)], "exchange_grads")
    small_parts = []
    for name, _ in SMALL:
        key = "mod" if name == "mod_b" else name
        if name == "hgrn_lb":
            small_parts.append(smalls[0][key] + smalls[1][key])
        else:
            small_parts.append(jnp.stack([smalls[l][key] for l in range(DEPTH)]))
    (g_small,) = _gather([_pack_rows(small_parts, LANE)], "gather_small_grads")
    grads = _unpack_shard_grads(PACK1, _sum8(r1, "sum_grads_wide"))
    grads.update(_unpack_shard_grads(PACK2, _sum8(r2, "sum_grads_narrow")))
    small_sum = _sum8(g_small, "sum_small_grads").reshape(-1)
    off = 0
    for name, per_layer in SMALL:
        grads[name] = small_sum[off:off + DEPTH * per_layer].reshape(params[name].shape if name != "conv_w" else (DEPTH, CONV_WIDTH, CONV_CH))
        off += DEPTH * per_layer
    grads["conv_w"] = lax.dynamic_slice_in_dim(grads["conv_w"], me * cw_cols, cw_cols, axis=2)
    cols = mod_w.shape[2]
    dmod_all = g_small.reshape(N_DEV, -1)[:, :DEPTH * 6 * D_MODEL].reshape(N_DEV, DEPTH, 6 * D_MODEL)
    dmod_mine = lax.dynamic_slice_in_dim(dmod_all, me * cols, cols, axis=2).transpose(1, 0, 2)
    grads["mod_w"] = _modw_grad(c_all, dmod_mine, "mod_w_grad")

    delta, new_m, new_v = {}, {}, {}
    small_names = [n for n, _ in SMALL]
    for name in WEIGHTS:
        if name not in small_names:
            delta[name], new_m[name], new_v[name] = _adamw_nd(params[name], grads[name], mom1[name], mom2[name], f"adamw_{name}")
    packed = [_pack_rows([d[n] for n in small_names], LANE) for d in (params, grads, mom1, mom2)]
    outs = [o.reshape(-1) for o in _adamw(*packed, "adamw_small")]
    off = 0
    for name in small_names:
        size = params[name].size
        for dst, o in zip((delta, new_m, new_v), outs):
            dst[name] = o[off:off + size].reshape(params[name].shape)
        off += size
    return (loss, dx[None], *[grads[n] for n in WEIGHTS], *[delta[n] for n in WEIGHTS],
            *[new_m[n] for n in WEIGHTS], *[new_v[n] for n in WEIGHTS])
```

```python
import functools

import jax
import jax.numpy as jnp
import numpy as np
from jax import lax
from jax.experimental import pallas as pl
from jax.experimental.pallas import tpu as pltpu

F32 = jnp.float32
BF16 = jnp.bfloat16

D_MODEL = 1024
DEPTH = 2
N_DEV = 8
CONV_CH = 512
CONV_WIDTH = 31
CONV_HALO = 32
HG_HEADS = 4
HG_DK = 128
SB_HEADS = 8
SB_DH = 64
D_IN = 7680
D_FF = 4096
EPS = 1e-6
SB_BLK = 128
SB_DEAD = -104.0
SB_FIXED = 3
HG_CHUNK = 64

ADAM_LR = 0.001
ADAM_B1 = 0.9
ADAM_B2 = 0.999
ADAM_EPS = 1e-08
ADAM_WD = 0.01
ADAM_STEP = 10

VMEM_LIMIT = 48 * 1024 * 1024

NN = ((1,), (0,))
NT = ((1,), (1,))
TN = ((0,), (0,))
_DIMS = {"nn": NN, "nt": NT, "tn": TN}


def _sds(shape, dtype):
    return jax.ShapeDtypeStruct(shape, dtype)


def _params(*semantics):
    return pltpu.CompilerParams(dimension_semantics=semantics, vmem_limit_bytes=VMEM_LIMIT)


def _dot(a, b, dims):
    return lax.dot_general(a, b, (dims, ((), ())), preferred_element_type=F32)


@functools.partial(jax.custom_vjp, nondiff_argnums=(2,))
def _bdot(a, b, mode):
    return _dot(a.astype(BF16), b.astype(BF16), _DIMS[mode])


def _bdot_fwd(a, b, mode):
    return _bdot(a, b, mode), (a.astype(BF16), b.astype(BF16))


def _bdot_bwd(mode, res, g):
    a, b = res
    g = g.astype(BF16)
    if mode == "nn":
        return _dot(g, b, NT), _dot(a, g, TN)
    if mode == "nt":
        return _dot(g, b, NN), _dot(g, a, TN)
    return _dot(b, g, NT), _dot(a, g, NN)


_bdot.defvjp(_bdot_fwd, _bdot_bwd)


def _split(x):
    hi = x.astype(BF16)
    lo = (x - hi.astype(F32)).astype(BF16)
    return hi, lo


def _xdot_right(x, m, dims=NN):
    hi, lo = _split(x)
    return _dot(hi, m, dims) + _dot(lo, m, dims)


def _xdot_left(m, x, dims=NN):
    hi, lo = _split(x)
    return _dot(m, hi, dims) + _dot(m, lo, dims)


@jax.custom_vjp
def _xr(x, m):
    return _xdot_right(x, m)


def _xr_fwd(x, m):
    return _xdot_right(x, m), m


def _xr_bwd(m, g):
    return _xdot_right(g, m, NT), jnp.zeros_like(m)


_xr.defvjp(_xr_fwd, _xr_bwd)


def _norm_mod(x, g, sc, sh):
    r = lax.rsqrt(jnp.mean(x * x, axis=-1, keepdims=True) + EPS)
    return x * r * g * (1.0 + sc) + sh


def _matmul(a, b, mode, out_dtype, tm, tn, tk, name):
    if mode == "nn":
        (m, k), (_, n) = a.shape, b.shape
    elif mode == "nt":
        (m, k), (n, _) = a.shape, b.shape
    else:
        (k, m), (_, n) = a.shape, b.shape
    tm, tn, tk = min(tm, m), min(tn, n), min(tk, k)
    assert m % tm == 0 and n % tn == 0 and k % tk == 0, (name, m, n, k, tm, tn, tk)
    nk = k // tk
    dims = _DIMS[mode]

    def body(a_ref, b_ref, o_ref, acc_ref):
        kk = pl.program_id(2)

        @pl.when(kk == 0)
        def _():
            acc_ref[...] = jnp.zeros_like(acc_ref)

        acc_ref[...] += _dot(a_ref[...], b_ref[...], dims)

        @pl.when(kk == nk - 1)
        def _():
            o_ref[...] = acc_ref[...].astype(out_dtype)

    if mode == "tn":
        a_spec = pl.BlockSpec((tk, tm), lambda i, j, kk: (kk, i))
        b_spec = pl.BlockSpec((tk, tn), lambda i, j, kk: (kk, j))
    elif mode == "nn":
        a_spec = pl.BlockSpec((tm, tk), lambda i, j, kk: (i, kk))
        b_spec = pl.BlockSpec((tk, tn), lambda i, j, kk: (kk, j))
    else:
        a_spec = pl.BlockSpec((tm, tk), lambda i, j, kk: (i, kk))
        b_spec = pl.BlockSpec((tn, tk), lambda i, j, kk: (j, kk))
    return pl.pallas_call(
        body, grid=(m // tm, n // tn, nk), in_specs=[a_spec, b_spec],
        out_specs=pl.BlockSpec((tm, tn), lambda i, j, kk: (i, j)),
        out_shape=_sds((m, n), out_dtype), scratch_shapes=[pltpu.VMEM((tm, tn), F32)],
        compiler_params=_params("parallel", "parallel", "arbitrary"), name=name)(a, b)


ROW_T = 512


def _prenorm(x, pv, name):
    s, d = x.shape
    t = min(ROW_T, s)

    def body(x_ref, pv_ref, h_ref):
        h = _norm_mod(x_ref[...], pv_ref[6:7, :], pv_ref[1:2, :], pv_ref[0:1, :])
        h_ref[...] = h.astype(BF16)

    return pl.pallas_call(
        body, grid=(s // t,),
        in_specs=[pl.BlockSpec((t, d), lambda i: (i, 0)), pl.BlockSpec((16, d), lambda i: (0, 0))],
        out_specs=pl.BlockSpec((t, d), lambda i: (i, 0)), out_shape=_sds((s, d), BF16),
        compiler_params=_params("parallel"), name=name)(x, pv)


def _prenorm_bwd(dh, dres, x, pv, name):
    s, d = x.shape
    t = min(ROW_T, s)

    def body(dh_ref, dres_ref, x_ref, pv_ref, dx_ref, sg_ref):
        i = pl.program_id(0)

        @pl.when(i == 0)
        def _():
            sg_ref[...] = jnp.zeros_like(sg_ref)

        _, vjp = jax.vjp(_norm_mod, x_ref[...], pv_ref[6:7, :], pv_ref[1:2, :], pv_ref[0:1, :])
        dx, dg, dsc, dsh = vjp(dh_ref[...])
        dx_ref[...] = dres_ref[...] + dx
        sg_ref[0:1, :] += dsh
        sg_ref[1:2, :] += dsc
        sg_ref[2:3, :] += dg

    row = pl.BlockSpec((t, d), lambda i: (i, 0))
    return pl.pallas_call(
        body, grid=(s // t,),
        in_specs=[row, row, row, pl.BlockSpec((16, d), lambda i: (0, 0))],
        out_specs=[row, pl.BlockSpec((8, d), lambda i: (0, 0))],
        out_shape=[_sds((s, d), F32), _sds((8, d), F32)],
        compiler_params=_params("arbitrary"), name=name)(dh, dres, x, pv)


CONV_T = 256


def _conv_tile(a_ext, g_ext, w, b, ln_g, ln_b, n_out):
    u0 = a_ext * jax.nn.sigmoid(g_ext)
    off = CONV_HALO - (CONV_WIDTH - 1)
    acc = jnp.zeros((n_out, u0.shape[1]), F32) + b
    for k in range(CONV_WIDTH):
        acc = acc + w[k:k + 1, :] * u0[off + k: off + k + n_out, :]
    mu = jnp.mean(acc, axis=-1, keepdims=True)
    var = jnp.mean(jnp.square(acc - mu), axis=-1, keepdims=True)
    y = (acc - mu) * lax.rsqrt(var + EPS) * ln_g + ln_b
    return y * jax.nn.sigmoid(y)


def _conv_fwd(proj, conv_w, cp, name):
    s = proj.shape[0]
    t = min(CONV_T, s)
    c, h = CONV_CH, CONV_HALO

    def body(ap_ref, ac_ref, gp_ref, gc_ref, w_ref, cp_ref, o_ref):
        i = pl.program_id(0)
        live = (i > 0).astype(F32)
        a_ext = jnp.concatenate([ap_ref[t - h:, :] * live, ac_ref[...]], axis=0)
        g_ext = jnp.concatenate([gp_ref[t - h:, :], gc_ref[...]], axis=0)
        u = _conv_tile(a_ext, g_ext, w_ref[...], cp_ref[0:1, :], cp_ref[1:2, :], cp_ref[2:3, :], t)
        o_ref[...] = u.astype(BF16)

    prev = lambda col: pl.BlockSpec((t, c), lambda i: (jnp.maximum(i - 1, 0), col))
    cur = lambda col: pl.BlockSpec((t, c), lambda i: (i, col))
    return pl.pallas_call(
        body, grid=(s // t,),
        in_specs=[prev(0), cur(0), prev(1), cur(1),
                  pl.BlockSpec((CONV_WIDTH, c), lambda i: (0, 0)), pl.BlockSpec((8, c), lambda i: (0, 0))],
        out_specs=pl.BlockSpec((t, c), lambda i: (i, 0)), out_shape=_sds((s, c), BF16),
        compiler_params=_params("parallel"), name=name)(proj, proj, proj, proj, conv_w, cp)


def _conv_bwd(proj, do, conv_w, cp, name):
    s = proj.shape[0]
    t = min(CONV_T, s)
    c, h = CONV_CH, CONV_HALO
    nt = s // t

    def body(ap_ref, ac_ref, an_ref, gp_ref, gc_ref, gn_ref, doc_ref, don_ref, w_ref, cp_ref,
             da_ref, dg_ref, dw_ref, sg_ref):
        i = pl.program_id(0)

        @pl.when(i == 0)
        def _():
            dw_ref[...] = jnp.zeros_like(dw_ref)
            sg_ref[...] = jnp.zeros_like(sg_ref)

        first = (i > 0).astype(F32)
        last = (i < nt - 1).astype(F32)
        a_ext = jnp.concatenate([ap_ref[t - h:, :] * first, ac_ref[...], an_ref[:h, :] * last], axis=0)
        g_ext = jnp.concatenate([gp_ref[t - h:, :], gc_ref[...], gn_ref[:h, :]], axis=0)
        fn = functools.partial(_conv_tile, n_out=t + h)
        _, vjp = jax.vjp(fn, a_ext, g_ext, w_ref[...], cp_ref[0:1, :], cp_ref[1:2, :], cp_ref[2:3, :])
        ct_own = jnp.concatenate([doc_ref[...], jnp.zeros((h, c), F32)], axis=0)
        ct_all = jnp.concatenate([doc_ref[...], don_ref[:h, :] * last], axis=0)
        _, _, dw, db, dlg, dlb = vjp(ct_own)
        da, dg, _, _, _, _ = vjp(ct_all)
        da_ref[...] = da[h:h + t, :].astype(BF16)
        dg_ref[...] = dg[h:h + t, :].astype(BF16)
        dw_ref[...] += dw
        sg_ref[0:1, :] += db
        sg_ref[1:2, :] += dlg
        sg_ref[2:3, :] += dlb

    prev = lambda col: pl.BlockSpec((t, c), lambda i: (jnp.maximum(i - 1, 0), col))
    cur = lambda col: pl.BlockSpec((t, c), lambda i: (i, col))
    nxt = lambda col: pl.BlockSpec((t, c), lambda i: (jnp.minimum(i + 1, nt - 1), col))
    return pl.pallas_call(
        body, grid=(nt,),
        in_specs=[prev(0), cur(0), nxt(0), prev(1), cur(1), nxt(1), cur(0), nxt(0),
                  pl.BlockSpec((CONV_WIDTH, c), lambda i: (0, 0)), pl.BlockSpec((8, c), lambda i: (0, 0))],
        out_specs=[cur(0), cur(0), pl.BlockSpec((CONV_WIDTH, c), lambda i: (0, 0)),
                   pl.BlockSpec((8, c), lambda i: (0, 0))],
        out_shape=[_sds((s, c), BF16), _sds((s, c), BF16), _sds((CONV_WIDTH, c), F32), _sds((8, c), F32)],
        compiler_params=_params("arbitrary"), name=name)(proj, proj, proj, proj, proj, proj, do, do, conv_w, cp)


def _hgrn_levels(c):
    out, m = [], c // 2
    while m >= 1:
        out.append(m)
        m //= 2
    return out


def _hgrn_consts(c):
    t = np.arange(c)[:, None]
    j = np.arange(c)[None, :]
    mats = [j <= t, j > t]
    for m in _hgrn_levels(c):
        same = (t // m) == (j // m)
        mats += [same & (j <= t), same & (j > t)]
    return jnp.asarray(np.concatenate(mats, axis=0).astype(np.float32), dtype=BF16)


@jax.custom_vjp
def _cums(lc, mall):
    c = lc.shape[0]
    full = _xdot_left(mall, lc)
    return tuple(full[i * c:(i + 1) * c, :] for i in range(mall.shape[0] // c))


def _cums_fwd(lc, mall):
    return _cums(lc, mall), mall


def _cums_bwd(mall, cts):
    return _xdot_left(mall, jnp.concatenate(cts, axis=0), TN), jnp.zeros_like(mall)


_cums.defvjp(_cums_fwd, _cums_bwd)


def _hgrn_chunk(q, f, v, g, lbs, ng, st_in, mall):
    c = q.shape[0]
    keep = jax.nn.sigmoid(-f)
    if lbs:
        keep = (1.0 - jax.nn.sigmoid(lbs[1] - lbs[0])) * keep
    lc = jnp.log1p(-keep)
    qs = q * jax.nn.sigmoid(q)
    cs = _cums(lc, mall)
    o = _bdot(qs * jnp.exp(cs[0]), st_in, "nt")
    total = jnp.sum(lc, axis=0, keepdims=True)
    st_out = st_in * jnp.exp(total) + _bdot(v, keep * jnp.exp(cs[1]), "tn")
    r = lax.broadcasted_iota(jnp.int32, q.shape, 0)
    tt = lax.broadcasted_iota(jnp.int32, (c, c), 0)
    ss = lax.broadcasted_iota(jnp.int32, (c, c), 1)
    sc = jnp.where(tt == ss, jnp.sum(qs * keep, axis=-1, keepdims=True), 0.0)
    for li, m in enumerate(_hgrn_levels(c)):
        lg = m.bit_length() - 1
        odd = ((r >> lg) & 1) == 1
        qm = jnp.where(odd, qs * jnp.exp(cs[2 + 2 * li]), 0.0)
        km = jnp.where(odd, 0.0, keep * jnp.exp(cs[3 + 2 * li]))
        pair = (((tt >> lg) & 1) == 1) & ((ss >> lg) == (tt >> lg) - 1)
        sc = sc + jnp.where(pair, _bdot(qm, km, "nt"), 0.0)
    o = o + _bdot(sc, v, "nn")
    on = o * lax.rsqrt(jnp.mean(o * o, axis=-1, keepdims=True) + EPS) * ng
    return on * (g * jax.nn.sigmoid(g)), st_out


def _hgrn_fwd(proj, lb, ng, name):
    s = proj.shape[0]
    c = HG_CHUNK
    nc = s // c
    mall = _hgrn_consts(c)
    col0 = 1024 // (HG_HEADS * HG_DK)

    def body(*refs):
        q_ref, f_ref, v_ref, g_ref = refs[:4]
        if lb is None:
            ng_ref, m_ref, y_ref, st_ref, scr = refs[4:]
        else:
            lb_ref, ng_ref, m_ref, y_ref, st_ref, scr = refs[4:]
        ci = pl.program_id(0)

        @pl.when(ci == 0)
        def _():
            scr[...] = jnp.zeros_like(scr)

        mall_v = m_ref[...]
        for h in range(HG_HEADS):
            hs = slice(h * HG_DK, (h + 1) * HG_DK)
            lbs = () if lb is None else (lb_ref[0:1, hs], lb_ref[1:2, hs])
            st_in = scr[h]
            st_ref[h] = st_in
            y, st_out = _hgrn_chunk(q_ref[:, hs], f_ref[:, hs], v_ref[:, hs], g_ref[:, hs], lbs, ng_ref[...], st_in, mall_v)
            y_ref[:, hs] = y.astype(BF16)
            scr[h] = st_out

    w = HG_HEADS * HG_DK
    col = lambda k: pl.BlockSpec((c, w), lambda ci: (ci, col0 + k))
    in_specs = [col(0), col(1), col(2), col(3)]
    args = [proj, proj, proj, proj]
    if lb is not None:
        in_specs.append(pl.BlockSpec((2, w), lambda ci: (0, 0)))
        args.append(lb)
    in_specs += [pl.BlockSpec((1, HG_DK), lambda ci: (0, 0)), pl.BlockSpec(mall.shape, lambda ci: (0, 0))]
    args += [ng, mall]
    return pl.pallas_call(
        body, grid=(nc,), in_specs=in_specs,
        out_specs=[pl.BlockSpec((c, w), lambda ci: (ci, 0)),
                   pl.BlockSpec((HG_HEADS, None, HG_DK, HG_DK), lambda ci: (0, ci, 0, 0))],
        out_shape=[_sds((s, w), BF16), _sds((HG_HEADS, nc, HG_DK, HG_DK), F32)],
        scratch_shapes=[pltpu.VMEM((HG_HEADS, HG_DK, HG_DK), F32)],
        compiler_params=_params("arbitrary"), name=name)(*args)


def _hgrn_bwd(proj, states, dy, lb, ng, name):
    s = proj.shape[0]
    c = HG_CHUNK
    nc = s // c
    mall = _hgrn_consts(c)
    col0 = 1024 // (HG_HEADS * HG_DK)

    def body(*refs):
        q_ref, f_ref, v_ref, g_ref, st_ref, dy_ref = refs[:6]
        if lb is None:
            ng_ref, m_ref, dq_ref, df_ref, dv_ref, dg_ref, dlb_ref, dng_ref, scr = refs[6:]
        else:
            lb_ref, ng_ref, m_ref, dq_ref, df_ref, dv_ref, dg_ref, dlb_ref, dng_ref, scr = refs[6:]
        ci = pl.program_id(0)

        @pl.when(ci == 0)
        def _():
            scr[...] = jnp.zeros_like(scr)
            dlb_ref[...] = jnp.zeros_like(dlb_ref)
            dng_ref[...] = jnp.zeros_like(dng_ref)

        mall_v = m_ref[...]
        fn = lambda q, f, v, g, lbs_, ng_, st: _hgrn_chunk(q, f, v, g, lbs_, ng_, st, mall_v)
        for h in range(HG_HEADS):
            hs = slice(h * HG_DK, (h + 1) * HG_DK)
            lbs = () if lb is None else (lb_ref[0:1, hs], lb_ref[1:2, hs])
            _, vjp = jax.vjp(fn, q_ref[:, hs], f_ref[:, hs], v_ref[:, hs], g_ref[:, hs], lbs, ng_ref[...], st_ref[h])
            dq, df, dv, dg, dlbs, dng, dst = vjp((dy_ref[:, hs], scr[h]))
            dq_ref[:, hs] = dq.astype(BF16)
            df_ref[:, hs] = df.astype(BF16)
            dv_ref[:, hs] = dv.astype(BF16)
            dg_ref[:, hs] = dg.astype(BF16)
            scr[h] = dst
            dng_ref[0:1, :] += dng
            if lbs:
                dlb_ref[0:1, hs] += dlbs[0]
                dlb_ref[1:2, hs] += dlbs[1]

    w = HG_HEADS * HG_DK
    rev = lambda ci: nc - 1 - ci
    col = lambda k: pl.BlockSpec((c, w), lambda ci: (rev(ci), col0 + k))
    out_col = pl.BlockSpec((c, w), lambda ci: (rev(ci), 0))
    in_specs = [col(0), col(1), col(2), col(3),
                pl.BlockSpec((HG_HEADS, None, HG_DK, HG_DK), lambda ci: (0, rev(ci), 0, 0)), out_col]
    args = [proj, proj, proj, proj, states, dy]
    if lb is not None:
        in_specs.append(pl.BlockSpec((2, w), lambda ci: (0, 0)))
        args.append(lb)
    in_specs += [pl.BlockSpec((1, HG_DK), lambda ci: (0, 0)), pl.BlockSpec(mall.shape, lambda ci: (0, 0))]
    args += [ng, mall]
    return pl.pallas_call(
        body, grid=(nc,), in_specs=in_specs,
        out_specs=[out_col, out_col, out_col, out_col,
                   pl.BlockSpec((2, w), lambda ci: (0, 0)), pl.BlockSpec((8, HG_DK), lambda ci: (0, 0))],
        out_shape=[_sds((s, w), BF16)] * 4 + [_sds((2, w), F32), _sds((8, HG_DK), F32)],
        scratch_shapes=[pltpu.VMEM((HG_HEADS, HG_DK, HG_DK), F32)],
        compiler_params=_params("arbitrary"), name=name)(*args)


def _head_avg():
    w = SB_HEADS * SB_DH
    i = np.arange(w)
    return jnp.asarray(((i[:, None] // SB_DH) == (i[None, :] // SB_DH)).astype(np.float32) / SB_DH, dtype=BF16)


def _sb_norm(x, g_tiled, avg):
    ms = _xr(x * x, avg)
    return x * lax.rsqrt(ms + EPS) * g_tiled


def _sb_prep(proj, gq, gk, name):
    s = proj.shape[0]
    t = min(ROW_T, s)
    w = SB_HEADS * SB_DH
    avg = _head_avg()

    def body(q_ref, k_ref, v_ref, gq_ref, gk_ref, avg_ref, qn_ref, kn_ref, vb_ref):
        qn_ref[...] = _sb_norm(q_ref[...], gq_ref[...], avg_ref[...]).astype(BF16)
        kn_ref[...] = _sb_norm(k_ref[...], gk_ref[...], avg_ref[...]).astype(BF16)
        vb_ref[...] = v_ref[...].astype(BF16)

    col = lambda k: pl.BlockSpec((t, w), lambda i: (i, 6 + k))
    vec = pl.BlockSpec((1, w), lambda i: (0, 0))
    out = pl.BlockSpec((t, w), lambda i: (i, 0))
    return pl.pallas_call(
        body, grid=(s // t,), in_specs=[col(0), col(1), col(2), vec, vec, pl.BlockSpec((w, w), lambda i: (0, 0))],
        out_specs=[out, out, out], out_shape=[_sds((s, w), BF16)] * 3,
        compiler_params=_params("parallel"), name=name)(proj, proj, proj, gq, gk, avg)


def _sb_prep_bwd(proj, dqn, dkn, gq, gk, name):
    s = proj.shape[0]
    t = min(ROW_T, s)
    w = SB_HEADS * SB_DH
    avg = _head_avg()

    def body(q_ref, k_ref, dqn_ref, dkn_ref, gq_ref, gk_ref, avg_ref, dq_ref, dk_ref, sg_ref):
        i = pl.program_id(0)

        @pl.when(i == 0)
        def _():
            sg_ref[...] = jnp.zeros_like(sg_ref)

        avg_v = avg_ref[...]
        fn = lambda x, g: _sb_norm(x, g, avg_v)
        _, vq = jax.vjp(fn, q_ref[...], gq_ref[...])
        dq, dgq = vq(dqn_ref[...])
        _, vk = jax.vjp(fn, k_ref[...], gk_ref[...])
        dk, dgk = vk(dkn_ref[...])
        dq_ref[...] = dq.astype(BF16)
        dk_ref[...] = dk.astype(BF16)
        sg_ref[0:1, :] += dgq
        sg_ref[1:2, :] += dgk

    col = lambda k: pl.BlockSpec((t, w), lambda i: (i, 6 + k))
    vec = pl.BlockSpec((1, w), lambda i: (0, 0))
    row = pl.BlockSpec((t, w), lambda i: (i, 0))
    return pl.pallas_call(
        body, grid=(s // t,),
        in_specs=[col(0), col(1), row, row, vec, vec, pl.BlockSpec((w, w), lambda i: (0, 0))],
        out_specs=[row, row, pl.BlockSpec((8, w), lambda i: (0, 0))],
        out_shape=[_sds((s, w), BF16), _sds((s, w), BF16), _sds((8, w), F32)],
        compiler_params=_params("arbitrary"), name=name)(proj, proj, dqn, dkn, gq, gk, avg)


def _sb_tri(kind):
    j = np.arange(SB_BLK)[:, None]
    s = np.arange(SB_BLK)[None, :]
    tri = (j > s) if kind == "suffix" else (j < s)
    return jnp.asarray(np.concatenate([tri, np.ones_like(tri)], axis=1).astype(np.float32), dtype=BF16)


def _sb_scores(qm, kblk, mask):
    z = _dot(qm, kblk, NT) * (SB_DH ** -0.5)
    sp = jnp.maximum(z, 0.0) + jnp.log(1.0 + jnp.exp(-jnp.abs(z)))
    return z, sp, jnp.where(mask, -sp, 0.0)


def _sb_setup(b):
    lane = lax.broadcasted_iota(jnp.int32, (2 * b, b), 1)
    row = lax.broadcasted_iota(jnp.int32, (2 * b, b), 0)
    mine = (row >> (b.bit_length() - 1)) == (lane >> (SB_DH.bit_length() - 1))
    return lane, row & (b - 1), mine


def _sb_fwd(qn, kn, vb, name):
    s, w = qn.shape
    b = SB_BLK
    nq = s // b
    tri = _sb_tri("suffix")

    def body(q_ref, k_ref, v_ref, tri_ref, o_ref):
        i = pl.program_id(1)
        lane, tt, mine = _sb_setup(b)
        q = q_ref[...]
        q2 = jnp.concatenate([q, q], axis=0)
        qm = jnp.where(mine, q2, jnp.zeros_like(q2))
        tri_v = tri_ref[...]

        def block(kb, lim, run, acc):
            off = pl.multiple_of(kb * b, b)
            kblk = k_ref[pl.ds(off, b), :]
            vblk = v_ref[pl.ds(off, b), :]
            mask = lane < lim
            z, sp, lk = _sb_scores(qm, kblk, mask)
            both = _xdot_right(lk, tri_v)
            a = jnp.where(mask, jnp.exp(z - sp + both[:, :b] + run), 0.0)
            return run + both[:, b:], acc + _dot(a.astype(BF16), vblk, NN)

        run = acc = jnp.zeros((2 * b, b), F32)
        for j in range(SB_FIXED):
            lim = tt if j == 0 else jnp.where(i >= j, b, 0)
            run, acc = block(jnp.maximum(i - j, 0), lim, run, acc)

        def cond(carry):
            j, run_, _ = carry
            return (j <= i) & (jnp.max(run_) > SB_DEAD)

        def step(carry):
            j, run_, acc_ = carry
            run_, acc_ = block(i - j, b, run_, acc_)
            return j + 1, run_, acc_

        _, _, acc = lax.while_loop(cond, step, (jnp.int32(SB_FIXED), run, acc))
        o_ref[...] = jnp.where(lane[:b] < SB_DH, acc[:b], acc[b:]).astype(BF16)

    blk = pl.BlockSpec((b, b), lambda p, i: (i, p))
    full = pl.BlockSpec((s, b), lambda p, i: (0, p))
    return pl.pallas_call(
        body, grid=(w // b, nq), in_specs=[blk, full, full, pl.BlockSpec(tri.shape, lambda p, i: (0, 0))],
        out_specs=blk, out_shape=_sds((s, w), BF16),
        compiler_params=_params("parallel", "arbitrary"), name=name)(qn, kn, vb, tri)


def _sb_bwd(qn, kn, vb, do, name):
    s, w = qn.shape
    b = SB_BLK
    nq = s // b
    tri_s = _sb_tri("suffix")
    tri_p = _sb_tri("prefix")
    scale = SB_DH ** -0.5

    def body(q_ref, k_ref, v_ref, do_ref, ts_ref, tp_ref, dq_ref, dk_ref, dv_ref, dk_acc, dv_acc, dp_scr):
        i = pl.program_id(1)

        @pl.when(i == 0)
        def _():
            dk_acc[...] = jnp.zeros_like(dk_acc)
            dv_acc[...] = jnp.zeros_like(dv_acc)

        lane, tt, mine = _sb_setup(b)
        q = q_ref[...]
        q2 = jnp.concatenate([q, q], axis=0)
        qm = jnp.where(mine, q2, jnp.zeros_like(q2))
        dout = do_ref[...].astype(BF16)
        d2 = jnp.concatenate([dout, dout], axis=0)
        dom = jnp.where(mine, d2, jnp.zeros_like(d2))
        ts_v = ts_ref[...]
        tp_v = tp_ref[...]
        zero = jnp.zeros((2 * b, b), F32)

        def down(kb, lim, run):
            off = pl.multiple_of(kb * b, b)
            kblk = k_ref[pl.ds(off, b), :]
            vblk = v_ref[pl.ds(off, b), :]
            mask = lane < lim
            z, sp, lk = _sb_scores(qm, kblk, mask)
            both = _xdot_right(lk, ts_v)
            a = jnp.where(mask, jnp.exp(z - sp + both[:, :b] + run), 0.0)
            dv_acc[pl.ds(off, b), :] += _dot(a.astype(BF16), dom, TN)
            return _dot(dom, vblk, NT) * a, run + both[:, b:]

        def up(kb, lim, dp, pre, dq):
            off = pl.multiple_of(kb * b, b)
            kblk = k_ref[pl.ds(off, b), :]
            sig = jax.nn.sigmoid(_dot(qm, kblk, NT) * scale)
            both = _xdot_right(dp, tp_v)
            dz = jnp.where(lane < lim, dp * (1.0 - sig) - sig * (both[:, :b] + pre), 0.0) * scale
            dz = dz.astype(BF16)
            dk_acc[pl.ds(off, b), :] += _dot(dz, qm, TN)
            return pre + both[:, b:], dq + _dot(dz, kblk, NN)

        lims = [tt if j == 0 else jnp.where(i >= j, b, 0) for j in range(SB_FIXED)]
        run = zero
        dps = []
        for j in range(SB_FIXED):
            dp, run = down(jnp.maximum(i - j, 0), lims[j], run)
            dps.append(dp)

        def cond(carry):
            j, run_ = carry
            return (j <= i) & (jnp.max(run_) > SB_DEAD)

        def sweep_down(carry):
            j, run_ = carry
            dp, run_ = down(i - j, b, run_)
            dp_scr[i - j] = dp
            return j + 1, run_

        n_live, _ = lax.while_loop(cond, sweep_down, (jnp.int32(SB_FIXED), run))

        def sweep_up(jj, carry):
            kb = i - n_live + 1 + jj
            return up(kb, b, dp_scr[kb], *carry)

        pre, dq = lax.fori_loop(0, n_live - SB_FIXED, sweep_up, (zero, zero))
        for j in reversed(range(SB_FIXED)):
            pre, dq = up(jnp.maximum(i - j, 0), lims[j], dps[j], pre, dq)
        dq_ref[...] = jnp.where(lane[:b] < SB_DH, dq[:b], dq[b:])

        @pl.when(i == nq - 1)
        def _():
            dk_ref[...] = dk_acc[...]
            dv_ref[...] = dv_acc[...].astype(BF16)

    blk = pl.BlockSpec((b, b), lambda p, i: (i, p))
    full = pl.BlockSpec((s, b), lambda p, i: (0, p))
    tri = pl.BlockSpec(tri_s.shape, lambda p, i: (0, 0))
    return pl.pallas_call(
        body, grid=(w // b, nq), in_specs=[blk, full, full, blk, tri, tri],
        out_specs=[blk, full, full], out_shape=[_sds((s, w), F32), _sds((s, w), F32), _sds((s, w), BF16)],
        scratch_shapes=[pltpu.VMEM((s, b), F32), pltpu.VMEM((s, b), F32), pltpu.VMEM((nq, 2 * b, b), F32)],
        compiler_params=_params("arbitrary", "arbitrary"), name=name)(qn, kn, vb, do, tri_s, tri_p)


MIX_T = 256
HALF = 512


def _gate_slices(ga, gb):
    return [(ga[:, 0:512], ga[:, 512:1024]), (ga[:, 1024:1536], gb[:, 0:512]), (gb[:, 512:1024], gb[:, 1024:1536])]


def _mix_fwd(u3, oh, osb, proj, x, pv, wc, wh, ws, wo, name):
    s, d = x.shape
    t = min(MIX_T, s)

    def body(u3_ref, oh_ref, os_ref, ga_ref, gb_ref, x_ref, pv_ref, wc_ref, wh_ref, ws_ref, wo_ref,
             x1_ref, h2_ref, mg_ref, mo_ref):
        ys = [_dot(u3_ref[...], wc_ref[...], NT), _dot(oh_ref[...], wh_ref[...], NT), _dot(os_ref[...], ws_ref[...], NT)]
        gl = _gate_slices(ga_ref[...], gb_ref[...])
        halves = []
        for hf in range(2):
            lo = hf * HALF
            acc = jnp.zeros((t, HALF), F32)
            for br in range(3):
                gate = jax.nn.sigmoid(gl[br][hf] + pv_ref[8 + br:9 + br, lo:lo + HALF])
                acc = acc + gate * ys[br][:, lo:lo + HALF]
            halves.append(acc)
        merged = jnp.concatenate(halves, axis=1).astype(BF16)
        mg_ref[...] = merged
        mo = _dot(merged, wo_ref[...], NN)
        mo_ref[...] = mo.astype(BF16)
        x1 = x_ref[...] + pv_ref[2:3, :] * mo
        x1_ref[...] = x1
        h2_ref[...] = _norm_mod(x1, pv_ref[7:8, :], pv_ref[4:5, :], pv_ref[3:4, :]).astype(BF16)

    br_spec = pl.BlockSpec((t, CONV_CH), lambda i: (i, 0))
    row = pl.BlockSpec((t, d), lambda i: (i, 0))
    wproj = pl.BlockSpec((d, CONV_CH), lambda i: (0, 0))
    return pl.pallas_call(
        body, grid=(s // t,),
        in_specs=[br_spec, br_spec, br_spec, pl.BlockSpec((t, 1536), lambda i: (i, 3)),
                  pl.BlockSpec((t, 1536), lambda i: (i, 4)), row, pl.BlockSpec((16, d), lambda i: (0, 0)),
                  wproj, wproj, wproj, pl.BlockSpec((d, d), lambda i: (0, 0))],
        out_specs=[row, row, row, row],
        out_shape=[_sds((s, d), F32), _sds((s, d), BF16), _sds((s, d), BF16), _sds((s, d), BF16)],
        compiler_params=_params("parallel"), name=name)(u3, oh, osb, proj, proj, x, pv, wc, wh, ws, wo)


def _mix_bwd(dx1, mo1, u3, oh, osb, proj, pv, wc, wh, ws, wo, name):
    s, d = dx1.shape
    t = min(MIX_T, s)

    def body(dx_ref, mo_ref, u3_ref, oh_ref, os_ref, ga_ref, gb_ref, pv_ref, wc_ref, wh_ref, ws_ref, wo_ref,
             dmo_ref, dyc_ref, dyh_ref, dys_ref, doc_ref, doh_ref, dos_ref, dgl_ref, sg_ref):
        i = pl.program_id(0)

        @pl.when(i == 0)
        def _():
            sg_ref[...] = jnp.zeros_like(sg_ref)

        dx = dx_ref[...]
        dmo = (dx * pv_ref[2:3, :]).astype(BF16)
        dmo_ref[...] = dmo
        sg_ref[0:1, :] += jnp.sum(dx * mo_ref[...].astype(F32), axis=0, keepdims=True)
        dmerged = _dot(dmo, wo_ref[...], NT)
        branches = [(u3_ref, wc_ref, dyc_ref, doc_ref), (oh_ref, wh_ref, dyh_ref, doh_ref), (os_ref, ws_ref, dys_ref, dos_ref)]
        gl = _gate_slices(ga_ref[...], gb_ref[...])
        for br, (o_ref, w_ref, dy_ref, do_ref) in enumerate(branches):
            y = _dot(o_ref[...], w_ref[...], NT)
            dys = []
            for hf in range(2):
                lo = hf * HALF
                gate = jax.nn.sigmoid(gl[br][hf] + pv_ref[8 + br:9 + br, lo:lo + HALF])
                dm = dmerged[:, lo:lo + HALF]
                dys.append(dm * gate)
                dgl = dm * y[:, lo:lo + HALF] * gate * (1.0 - gate)
                dgl_ref[:, br * d + lo: br * d + lo + HALF] = dgl.astype(BF16)
                sg_ref[1 + br:2 + br, lo:lo + HALF] += jnp.sum(dgl, axis=0, keepdims=True)
            dy = jnp.concatenate(dys, axis=1).astype(BF16)
            dy_ref[...] = dy
            do_ref[...] = _dot(dy, w_ref[...], NN)

    br_spec = pl.BlockSpec((t, CONV_CH), lambda i: (i, 0))
    row = pl.BlockSpec((t, d), lambda i: (i, 0))
    wproj = pl.BlockSpec((d, CONV_CH), lambda i: (0, 0))
    return pl.pallas_call(
        body, grid=(s // t,),
        in_specs=[row, row, br_spec, br_spec, br_spec, pl.BlockSpec((t, 1536), lambda i: (i, 3)),
                  pl.BlockSpec((t, 1536), lambda i: (i, 4)), pl.BlockSpec((16, d), lambda i: (0, 0)),
                  wproj, wproj, wproj, pl.BlockSpec((d, d), lambda i: (0, 0))],
        out_specs=[row, row, row, row, br_spec, br_spec, br_spec, pl.BlockSpec((t, 3 * d), lambda i: (i, 0)),
                   pl.BlockSpec((8, d), lambda i: (0, 0))],
        out_shape=[_sds((s, d), BF16)] * 4 + [_sds((s, CONV_CH), F32)] * 3 + [_sds((s, 3 * d), BF16), _sds((8, d), F32)],
        compiler_params=_params("arbitrary"), name=name)(dx1, mo1, u3, oh, osb, proj, proj, pv, wc, wh, ws, wo)


MLP_T = 512
MLP_F = 512


def _mlp_fwd(h2, x1, pv, w1t, w2, name):
    s, d = x1.shape
    t = min(MLP_T, s)
    nf = D_FF // MLP_F

    def body(h_ref, x_ref, pv_ref, w1_ref, w2_ref, x2_ref, mo_ref, acc_ref):
        f = pl.program_id(1)

        @pl.when(f == 0)
        def _():
            acc_ref[...] = jnp.zeros_like(acc_ref)

        a = jnp.maximum(_dot(h_ref[...], w1_ref[...], NT), 0.0)
        acc_ref[...] += _dot((a * a).astype(BF16), w2_ref[...], NN)

        @pl.when(f == nf - 1)
        def _():
            mo = acc_ref[...]
            mo_ref[...] = mo.astype(BF16)
            x2_ref[...] = x_ref[...] + pv_ref[5:6, :] * mo

    row = pl.BlockSpec((t, d), lambda i, f: (i, 0))
    wblk = pl.BlockSpec((MLP_F, d), lambda i, f: (f, 0))
    return pl.pallas_call(
        body, grid=(s // t, nf), in_specs=[row, row, pl.BlockSpec((16, d), lambda i, f: (0, 0)), wblk, wblk],
        out_specs=[row, row], out_shape=[_sds((s, d), F32), _sds((s, d), BF16)],
        scratch_shapes=[pltpu.VMEM((t, d), F32)],
        compiler_params=_params("parallel", "arbitrary"), name=name)(h2, x1, pv, w1t, w2)


def _mlp_bwd(dx2, h2, x1, mo2, pv, w1t, w2, name):
    s, d = x1.shape
    t = min(MLP_T, s)
    nf = D_FF // MLP_F

    def body(dx_ref, h_ref, x_ref, mo_ref, pv_ref, w1_ref, w2_ref, dx1_ref, da_ref, b_ref, dmo_ref, sg_ref, acc_ref):
        i = pl.program_id(0)
        f = pl.program_id(1)

        @pl.when((i == 0) & (f == 0))
        def _():
            sg_ref[...] = jnp.zeros_like(sg_ref)

        @pl.when(f == 0)
        def _():
            acc_ref[...] = jnp.zeros_like(acc_ref)
            dx = dx_ref[...]
            dmo_ref[...] = (dx * pv_ref[5:6, :]).astype(BF16)
            sg_ref[0:1, :] += jnp.sum(dx * mo_ref[...].astype(F32), axis=0, keepdims=True)

        r = jnp.maximum(_dot(h_ref[...], w1_ref[...], NT), 0.0)
        b_ref[...] = (r * r).astype(BF16)
        da = (_dot(dmo_ref[...], w2_ref[...], NT) * (2.0 * r)).astype(BF16)
        da_ref[...] = da
        acc_ref[...] += _dot(da, w1_ref[...], NN)

        @pl.when(f == nf - 1)
        def _():
            _, vjp = jax.vjp(_norm_mod, x_ref[...], pv_ref[7:8, :], pv_ref[4:5, :], pv_ref[3:4, :])
            dxn, dg, dsc, dsh = vjp(acc_ref[...])
            dx1_ref[...] = dx_ref[...] + dxn
            sg_ref[1:2, :] += dsh
            sg_ref[2:3, :] += dsc
            sg_ref[3:4, :] += dg

    row = pl.BlockSpec((t, d), lambda i, f: (i, 0))
    wblk = pl.BlockSpec((MLP_F, d), lambda i, f: (f, 0))
    hid = pl.BlockSpec((t, MLP_F), lambda i, f: (i, f))
    return pl.pallas_call(
        body, grid=(s // t, nf),
        in_specs=[row, row, row, row, pl.BlockSpec((16, d), lambda i, f: (0, 0)), wblk, wblk],
        out_specs=[row, hid, hid, row, pl.BlockSpec((8, d), lambda i, f: (0, 0))],
        out_shape=[_sds((s, d), F32), _sds((s, D_FF), BF16), _sds((s, D_FF), BF16), _sds((s, d), BF16), _sds((8, d), F32)],
        scratch_shapes=[pltpu.VMEM((t, d), F32)],
        compiler_params=_params("arbitrary", "arbitrary"), name=name)(dx2, h2, x1, mo2, pv, w1t, w2)


def _loss_head(y, target, name):
    s, d = y.shape
    t = min(ROW_T, s)

    def body(y_ref, t_ref, dy_ref, ls_ref):
        i = pl.program_id(0)

        @pl.when(i == 0)
        def _():
            ls_ref[...] = jnp.zeros_like(ls_ref)

        e = y_ref[...] - t_ref[...]
        dy_ref[...] = e * (1.0 / d)
        ls_ref[...] += jnp.sum((e * e).reshape(t // 8, 8, d), axis=0)

    row = pl.BlockSpec((t, d), lambda i: (i, 0))
    return pl.pallas_call(
        body, grid=(s // t,), in_specs=[row, row], out_specs=[row, pl.BlockSpec((8, d), lambda i: (0, 0))],
        out_shape=[_sds((s, d), F32), _sds((8, d), F32)],
        compiler_params=_params("arbitrary"), name=name)(y, target)


def _layer_vectors(l, mod, sm):
    d = D_MODEL
    pv = jnp.concatenate([mod[l].reshape(6, d), sm["norm1_g"][l][None], sm["norm2_g"][l][None],
                          sm["gate_b"][l].reshape(3, d), jnp.zeros((5, d), F32)], axis=0)
    cp = jnp.concatenate([sm["conv_b"][l][None], sm["conv_ln_g"][l][None], sm["conv_ln_b"][l][None],
                          jnp.zeros((5, CONV_CH), F32)], axis=0)
    return dict(pv=pv, cp=cp, conv_w=sm["conv_w"][l], lb=(sm["hgrn_lb"] if l > 0 else None),
                ng=sm["hgrn_norm_g"][l][None], gq=jnp.tile(sm["sb_qn_g"][l], SB_HEADS)[None],
                gk=jnp.tile(sm["sb_kn_g"][l], SB_HEADS)[None])


def _layer_fwd(x, vec, w, tag):
    h = _prenorm(x, vec["pv"], f"prenorm{tag}")
    proj = _matmul(h, w["win_t"], "nt", F32, 512, 768, 1024, f"proj{tag}")
    u3 = _conv_fwd(proj, vec["conv_w"], vec["cp"], f"conv_fwd{tag}")
    oh, states = _hgrn_fwd(proj, vec["lb"], vec["ng"], f"hgrn_fwd{tag}")
    qn, kn, vb = _sb_prep(proj, vec["gq"], vec["gk"], f"sb_prep{tag}")
    osb = _sb_fwd(qn, kn, vb, f"sb_fwd{tag}")
    x1, h2, merged, mo1 = _mix_fwd(u3, oh, osb, proj, x, vec["pv"], w["wc_t"], w["wh_t"], w["ws_t"], w["wo"], f"mix_fwd{tag}")
    x2, mo2 = _mlp_fwd(h2, x1, vec["pv"], w["w1_t"], w["w2"], f"mlp_fwd{tag}")
    saved = dict(x=x, h=h, proj=proj, u3=u3, oh=oh, states=states, qn=qn, kn=kn, vb=vb, osb=osb,
                 x1=x1, h2=h2, merged=merged, mo1=mo1, mo2=mo2)
    return x2, saved


def _layer_bwd(dx2, sv, vec, w, tag):
    pv = vec["pv"]
    dx1, da, bsq, dmo2, sg_mlp = _mlp_bwd(dx2, sv["h2"], sv["x1"], sv["mo2"], pv, w["w1_t"], w["w2"], f"mlp_bwd{tag}")
    big = {}
    big["w1_t"] = _matmul(da, sv["h2"], "tn", BF16, 512, 1024, 512, f"dw1{tag}")
    big["w2"] = _matmul(bsq, dmo2, "tn", BF16, 512, 1024, 512, f"dw2{tag}")
    dmo1, dyc, dyh, dys, doc, doh, dos, dgl, sg_mix = _mix_bwd(
        dx1, sv["mo1"], sv["u3"], sv["oh"], sv["osb"], sv["proj"], pv, w["wc_t"], w["wh_t"], w["ws_t"], w["wo"], f"mix_bwd{tag}")
    big["wo"] = _matmul(sv["merged"], dmo1, "tn", BF16, 512, 1024, 512, f"dwo{tag}")
    big["wc_t"] = _matmul(dyc, sv["u3"], "tn", BF16, 512, 512, 512, f"dwc{tag}")
    big["wh_t"] = _matmul(dyh, sv["oh"], "tn", BF16, 512, 512, 512, f"dwh{tag}")
    big["ws_t"] = _matmul(dys, sv["osb"], "tn", BF16, 512, 512, 512, f"dws{tag}")
    da_c, dg_c, dconv_w, sg_conv = _conv_bwd(sv["proj"], doc, vec["conv_w"], vec["cp"], f"conv_bwd{tag}")
    dq_h, df_h, di_h, dg_h, dlb, dng = _hgrn_bwd(sv["proj"], sv["states"], doh, vec["lb"], vec["ng"], f"hgrn_bwd{tag}")
    dqn, dkn, dv_s = _sb_bwd(sv["qn"], sv["kn"], sv["vb"], dos, f"sb_bwd{tag}")
    dq_s, dk_s, sg_sb = _sb_prep_bwd(sv["proj"], dqn, dkn, vec["gq"], vec["gk"], f"sb_prep_bwd{tag}")
    dproj = jnp.concatenate([da_c, dg_c, dq_h, df_h, di_h, dg_h, dq_s, dk_s, dv_s, dgl], axis=1)
    dh = _matmul(dproj, w["win_t"], "nn", F32, 512, 1024, 768, f"dh{tag}")
    big["win_t"] = _matmul(dproj, sv["h"], "tn", BF16, 768, 1024, 512, f"dwin{tag}")
    dx, sg_pre = _prenorm_bwd(dh, dx1, sv["x"], pv, f"prenorm_bwd{tag}")
    small = dict(
        mod=jnp.stack([sg_pre[0], sg_pre[1], sg_mix[0], sg_mlp[1], sg_mlp[2], sg_mlp[0]]).reshape(6 * D_MODEL),
        norm1_g=sg_pre[2], norm2_g=sg_mlp[3], gate_b=sg_mix[1:4].reshape(3 * D_MODEL),
        conv_w=dconv_w, conv_b=sg_conv[0], conv_ln_g=sg_conv[1], conv_ln_b=sg_conv[2],
        hgrn_lb=dlb, hgrn_norm_g=dng[0],
        sb_qn_g=sg_sb[0].reshape(SB_HEADS, SB_DH).sum(0), sb_kn_g=sg_sb[1].reshape(SB_HEADS, SB_DH).sum(0))
    return dx, big, small


def _local_step(x, target, mod, sm, wts):
    vecs = [_layer_vectors(l, mod, sm) for l in range(DEPTH)]
    saved = []
    y = x
    for l in range(DEPTH):
        y, sv = _layer_fwd(y, vecs[l], wts[l], f"_l{l}")
        saved.append(sv)
    dy, sq = _loss_head(y, target, "loss_head")
    bigs, smalls = [None] * DEPTH, [None] * DEPTH
    for l in reversed(range(DEPTH)):
        dy, bigs[l], smalls[l] = _layer_bwd(dy, saved[l], vecs[l], wts[l], f"_l{l}")
    return sq, dy, bigs, smalls


MESH = pl.DeviceIdType.MESH
HBM_SPEC = pl.BlockSpec(memory_space=pltpu.HBM)


def _mesh_place():
    return lax.axis_index("x"), lax.axis_index("y"), lax.axis_index("c")


def _block_of(px, py, pc):
    return 4 * px + 2 * py + pc


def _gather(xs, name):
    n = len(xs)

    def body(*refs):
        x_refs, out_refs = refs[:n], refs[n:2 * n]
        send_sems, recv_sems, local_sems = refs[2 * n:]
        x, y, c = _mesh_place()
        me, sibling = (x, y, c), (x, y, 1 - c)
        chips = [(1 - x, y), (x, 1 - y), (1 - x, 1 - y)]

        def copy(a, k, block, to, src=None):
            rows = out_refs[a].at[_block_of(*block)]
            return pltpu.make_async_remote_copy(
                src_ref=rows if src is None else src, dst_ref=rows, send_sem=send_sems.at[a, k],
                recv_sem=recv_sems.at[a, k], device_id=to, device_id_type=MESH)

        local = [pltpu.make_async_copy(x_refs[a], out_refs[a].at[_block_of(*me)], local_sems.at[a]) for a in range(n)]
        first = []
        for a in range(n):
            local[a].start()
            first.append(copy(a, 0, me, sibling, src=x_refs[a]))
            first += [copy(a, 1 + j, me, (*chip, c), src=x_refs[a]) for j, chip in enumerate(chips)]
        for cp in first:
            cp.start()
        passed = []
        for j, chip in enumerate(chips):
            for a in range(n):
                copy(a, 1 + j, (*chip, c), me).wait_recv()
                fwd = copy(a, 4 + j, (*chip, c), sibling)
                fwd.start()
                passed.append(fwd)
        for a in range(n):
            copy(a, 0, sibling, me).wait_recv()
            for j, chip in enumerate(chips):
                copy(a, 4 + j, (*chip, 1 - c), me).wait_recv()
        for cp in first + passed:
            cp.wait_send()
        for cp in local:
            cp.wait()

    return pl.pallas_call(
        body, in_specs=[HBM_SPEC] * n, out_specs=[HBM_SPEC] * n,
        out_shape=[_sds((N_DEV, *v.shape), v.dtype) for v in xs],
        scratch_shapes=[pltpu.SemaphoreType.DMA((n, 7)), pltpu.SemaphoreType.DMA((n, 7)), pltpu.SemaphoreType.DMA((n,))],
        name=name)(*xs)


def _exchange(packs, name):
    n = len(packs)

    def body(*refs):
        in_refs, out_refs = refs[:n], refs[n:2 * n]
        send_sems, recv_sems, local_sems = refs[2 * n:]
        x, y, c = _mesh_place()
        mine = _block_of(x, y, c)
        peers = []
        for k in range(1, N_DEV):
            peers.append((1 - x if k & 4 else x, 1 - y if k & 2 else y, 1 - c if k & 1 else c))

        def copy(a, k):
            peer = peers[k]
            return pltpu.make_async_remote_copy(
                src_ref=in_refs[a].at[_block_of(*peer)], dst_ref=out_refs[a].at[mine], send_sem=send_sems.at[a, k],
                recv_sem=recv_sems.at[a, k], device_id=peer, device_id_type=MESH)

        def arrival(a, k):
            slot = out_refs[a].at[_block_of(*peers[k])]
            return pltpu.make_async_remote_copy(
                src_ref=slot, dst_ref=slot, send_sem=send_sems.at[a, k], recv_sem=recv_sems.at[a, k],
                device_id=peers[k], device_id_type=MESH)

        local = [pltpu.make_async_copy(in_refs[a].at[mine], out_refs[a].at[mine], local_sems.at[a]) for a in range(n)]
        sends = [copy(a, k) for a in range(n) for k in range(N_DEV - 1)]
        for cp in local + sends:
            cp.start()
        for a in range(n):
            for k in range(N_DEV - 1):
                arrival(a, k).wait_recv()
        for cp in sends:
            cp.wait_send()
        for cp in local:
            cp.wait()

    return pl.pallas_call(
        body, in_specs=[HBM_SPEC] * n, out_specs=[HBM_SPEC] * n,
        out_shape=[_sds(v.shape, v.dtype) for v in packs],
        scratch_shapes=[pltpu.SemaphoreType.DMA((n, 7)), pltpu.SemaphoreType.DMA((n, 7)), pltpu.SemaphoreType.DMA((n,))],
        name=name)(*packs)


def _row_tile(r, cap=512):
    t = min(r, cap)
    while r % t or (t % 8 and t != r):
        t -= 1
    return t


def _sum8(z, name):
    _, r, c = z.shape
    t = _row_tile(r, 128 if c >= 1024 else 512)

    def body(z_ref, o_ref):
        acc = z_ref[0].astype(F32)
        for j in range(1, N_DEV):
            acc = acc + z_ref[j].astype(F32)
        o_ref[...] = acc

    return pl.pallas_call(
        body, grid=(r // t,), in_specs=[pl.BlockSpec((N_DEV, t, c), lambda i: (0, i, 0))],
        out_specs=pl.BlockSpec((t, c), lambda i: (i, 0)), out_shape=_sds((r, c), F32),
        compiler_params=_params("parallel"), name=name)(z)


def _adamw(w, g, m, v, name):
    r, c = w.shape
    t = _row_tile(r, 256)

    def body(w_ref, g_ref, m_ref, v_ref, d_ref, nm_ref, nv_ref):
        g_ = g_ref[...]
        nm = ADAM_B1 * m_ref[...] + (1.0 - ADAM_B1) * g_
        nv = ADAM_B2 * v_ref[...] + (1.0 - ADAM_B2) * jnp.square(g_)
        m_hat = nm / (1.0 - ADAM_B1 ** ADAM_STEP)
        v_hat = nv / (1.0 - ADAM_B2 ** ADAM_STEP)
        d_ref[...] = -ADAM_LR * (m_hat / (jnp.sqrt(v_hat) + ADAM_EPS) + ADAM_WD * w_ref[...])
        nm_ref[...] = nm
        nv_ref[...] = nv

    blk = pl.BlockSpec((t, c), lambda i: (i, 0))
    return pl.pallas_call(
        body, grid=(r // t,), in_specs=[blk] * 4, out_specs=[blk] * 3, out_shape=[_sds((r, c), F32)] * 3,
        compiler_params=_params("parallel"), name=name)(w, g, m, v)


def _mod_local(c_all, mod_w, name):
    depth, d, cols = mod_w.shape

    def body(c_ref, w_ref, o_ref):
        cv = c_ref[...]
        act = cv * jax.nn.sigmoid(cv)
        o_ref[...] = jnp.dot(act, w_ref[...], precision=lax.Precision.HIGHEST, preferred_element_type=F32)

    return pl.pallas_call(
        body, grid=(depth,),
        in_specs=[pl.BlockSpec((N_DEV, d), lambda l: (0, 0)), pl.BlockSpec((None, d, cols), lambda l: (l, 0, 0))],
        out_specs=pl.BlockSpec((None, N_DEV, cols), lambda l: (l, 0, 0)), out_shape=_sds((depth, N_DEV, cols), F32),
        compiler_params=_params("parallel"), name=name)(c_all, mod_w)


def _modw_grad(c_all, dmod, name):
    depth, _, cols = dmod.shape
    d = c_all.shape[1]

    def body(c_ref, g_ref, o_ref):
        cv = c_ref[...]
        act = cv * jax.nn.sigmoid(cv)
        o_ref[...] = lax.dot_general(act, g_ref[...], (TN, ((), ())), precision=lax.Precision.HIGHEST,
                                     preferred_element_type=F32)

    return pl.pallas_call(
        body, grid=(depth,),
        in_specs=[pl.BlockSpec((N_DEV, d), lambda l: (0, 0)), pl.BlockSpec((None, N_DEV, cols), lambda l: (l, 0, 0))],
        out_specs=pl.BlockSpec((None, d, cols), lambda l: (l, 0, 0)), out_shape=_sds((depth, d, cols), F32),
        compiler_params=_params("parallel"), name=name)(c_all, dmod)


LANE = 128
PACK1 = (("w_in", 960, True), ("w_out", 128, False), ("mlp_w2", 512, False), ("mlp_w1", 512, True))
PACK2 = (("w_conv_proj", 128, True), ("w_hgrn_proj", 128, True), ("w_sb_proj", 128, True))
BIG_KEY = {"w_in": "win_t", "w_out": "wo", "mlp_w2": "w2", "mlp_w1": "w1_t",
           "w_conv_proj": "wc_t", "w_hgrn_proj": "wh_t", "w_sb_proj": "ws_t"}
SMALL = (("mod_b", 6144), ("norm1_g", 1024), ("gate_b", 3072), ("conv_w", CONV_WIDTH * CONV_CH), ("conv_b", 512),
         ("conv_ln_g", 512), ("conv_ln_b", 512), ("hgrn_lb", 512), ("hgrn_norm_g", 128), ("sb_qn_g", 64),
         ("sb_kn_g", 64), ("norm2_g", 1024))


def _pack_rows(parts, width):
    flat = jnp.concatenate([p.reshape(-1) for p in parts])
    rows = -(-flat.shape[0] // width)
    rows = -(-rows // 8) * 8
    return jnp.pad(flat, (0, rows * width - flat.shape[0])).reshape(rows, width)


def _pack_weights(spec, params):
    parts = []
    for name, _, transposed in spec:
        for l in range(DEPTH):
            w = params[name][l]
            parts.append((w.T if transposed else w).astype(BF16))
    return jnp.concatenate(parts, axis=0)


def _unpack_gathered(spec, g):
    out = [{} for _ in range(DEPTH)]
    off = 0
    for name, rows, _ in spec:
        for l in range(DEPTH):
            out[l][BIG_KEY[name]] = g[:, off:off + rows].reshape(N_DEV * rows, g.shape[2])
            off += rows
    return out


def _pack_grads(spec, bigs):
    parts = []
    for name, rows, _ in spec:
        for l in range(DEPTH):
            gmat = bigs[l][BIG_KEY[name]]
            parts.append(gmat.reshape(N_DEV, rows, gmat.shape[1]))
    return jnp.concatenate(parts, axis=1)


def _unpack_shard_grads(spec, gsum):
    out = {}
    off = 0
    for name, rows, transposed in spec:
        per_layer = []
        for l in range(DEPTH):
            blk = gsum[off:off + rows]
            per_layer.append(blk.T if transposed else blk)
            off += rows
        out[name] = jnp.stack(per_layer)
    return out


def _adamw_nd(w, g, m, v, name):
    shape = w.shape
    two = lambda a: a.reshape(-1, shape[-1])
    return [o.reshape(shape) for o in _adamw(two(w), two(g), two(m), two(v), name)]


WEIGHTS = ("mod_w", "mod_b", "norm1_g", "w_in", "gate_b", "conv_w", "conv_b", "conv_ln_g", "conv_ln_b", "w_conv_proj",
           "hgrn_lb", "hgrn_norm_g", "w_hgrn_proj", "sb_qn_g", "sb_kn_g", "w_sb_proj", "w_out", "norm2_g", "mlp_w1",
           "mlp_w2")


def kernel(x, c, mod_w, mod_b, norm1_g, w_in, gate_b, conv_w, conv_b, conv_ln_g, conv_ln_b, w_conv_proj, hgrn_lb, hgrn_norm_g, w_hgrn_proj, sb_qn_g, sb_kn_g, w_sb_proj, w_out, norm2_g, mlp_w1, mlp_w2, loss_target, m_mod_w, m_mod_b, m_norm1_g, m_w_in, m_gate_b, m_conv_w, m_conv_b, m_conv_ln_g, m_conv_ln_b, m_w_conv_proj, m_hgrn_lb, m_hgrn_norm_g, m_w_hgrn_proj, m_sb_qn_g, m_sb_kn_g, m_w_sb_proj, m_w_out, m_norm2_g, m_mlp_w1, m_mlp_w2, v_mod_w, v_mod_b, v_norm1_g, v_w_in, v_gate_b, v_conv_w, v_conv_b, v_conv_ln_g, v_conv_ln_b, v_w_conv_proj, v_hgrn_lb, v_hgrn_norm_g, v_w_hgrn_proj, v_sb_qn_g, v_sb_kn_g, v_w_sb_proj, v_w_out, v_norm2_g, v_mlp_w1, v_mlp_w2):
    params = dict(mod_w=mod_w, mod_b=mod_b, norm1_g=norm1_g, w_in=w_in, gate_b=gate_b, conv_w=conv_w, conv_b=conv_b,
                  conv_ln_g=conv_ln_g, conv_ln_b=conv_ln_b, w_conv_proj=w_conv_proj, hgrn_lb=hgrn_lb,
                  hgrn_norm_g=hgrn_norm_g, w_hgrn_proj=w_hgrn_proj, sb_qn_g=sb_qn_g, sb_kn_g=sb_kn_g,
                  w_sb_proj=w_sb_proj, w_out=w_out, norm2_g=norm2_g, mlp_w1=mlp_w1, mlp_w2=mlp_w2)
    mom1 = dict(mod_w=m_mod_w, mod_b=m_mod_b, norm1_g=m_norm1_g, w_in=m_w_in, gate_b=m_gate_b, conv_w=m_conv_w,
                conv_b=m_conv_b, conv_ln_g=m_conv_ln_g, conv_ln_b=m_conv_ln_b, w_conv_proj=m_w_conv_proj,
                hgrn_lb=m_hgrn_lb, hgrn_norm_g=m_hgrn_norm_g, w_hgrn_proj=m_w_hgrn_proj, sb_qn_g=m_sb_qn_g,
                sb_kn_g=m_sb_kn_g, w_sb_proj=m_w_sb_proj, w_out=m_w_out, norm2_g=m_norm2_g, mlp_w1=m_mlp_w1,
                mlp_w2=m_mlp_w2)
    mom2 = dict(mod_w=v_mod_w, mod_b=v_mod_b, norm1_g=v_norm1_g, w_in=v_w_in, gate_b=v_gate_b, conv_w=v_conv_w,
                conv_b=v_conv_b, conv_ln_g=v_conv_ln_g, conv_ln_b=v_conv_ln_b, w_conv_proj=v_w_conv_proj,
                hgrn_lb=v_hgrn_lb, hgrn_norm_g=v_hgrn_norm_g, w_hgrn_proj=v_w_hgrn_proj, sb_qn_g=v_sb_qn_g,
                sb_kn_g=v_sb_kn_g, w_sb_proj=v_w_sb_proj, w_out=v_w_out, norm2_g=v_norm2_g, mlp_w1=v_mlp_w1,
                mlp_w2=v_mlp_w2)
    xi, yi, ci = _mesh_place()
    me = _block_of(xi, yi, ci)
    cw_cols = conv_w.shape[2]

    tiny = _pack_rows([c, conv_w], LANE)
    g_tiny, g1, g2 = _gather([tiny, _pack_weights(PACK1, params), _pack_weights(PACK2, params)], "gather_weights")
    c_rows = D_MODEL // LANE
    c_all = g_tiny[:, :c_rows].reshape(N_DEV, D_MODEL)
    n_cw = DEPTH * CONV_WIDTH * cw_cols
    conv_w_full = g_tiny[:, c_rows:c_rows + n_cw // LANE].reshape(N_DEV, DEPTH, CONV_WIDTH, cw_cols)
    conv_w_full = conv_w_full.transpose(1, 2, 0, 3).reshape(DEPTH, CONV_WIDTH, CONV_CH)
    wts = _unpack_gathered(PACK1, g1)
    for l, extra in enumerate(_unpack_gathered(PACK2, g2)):
        wts[l].update(extra)

    (g_mod,) = _gather([_mod_local(c_all, mod_w, "mod_local")], "gather_mod")
    mod = lax.dynamic_index_in_dim(g_mod, me, axis=2, keepdims=False)
    mod = mod.transpose(1, 0, 2).reshape(DEPTH, 6 * D_MODEL) + mod_b

    sm = dict(norm1_g=norm1_g, norm2_g=norm2_g, gate_b=gate_b, conv_w=conv_w_full, conv_b=conv_b, conv_ln_g=conv_ln_g,
              conv_ln_b=conv_ln_b, hgrn_lb=hgrn_lb, hgrn_norm_g=hgrn_norm_g, sb_qn_g=sb_qn_g, sb_kn_g=sb_kn_g)
    sq, dx, bigs, smalls = _local_step(x[0], loss_target[0], mod, sm, wts)
    loss = lax.psum(0.5 * jnp.sum(sq) / D_MODEL, ("x", "y", "c"))

    r1, r2 = _exchange([_pack_grads(PACK1, bigs), _pack_grads(PACK2, bigs)], "exchange_grads")
    small_parts = []
    for name, _ in SMALL:
        key = "mod" if name == "mod_b" else name
        if name == "hgrn_lb":
            small_parts.append(smalls[0][key] + smalls[1][key])
        else:
            small_parts.append(jnp.stack([smalls[l][key] for l in range(DEPTH)]))
    (g_small,) = _gather([_pack_rows(small_parts, LANE)], "gather_small_grads")
    grads = _unpack_shard_grads(PACK1, _sum8(r1, "sum_grads_wide"))
    grads.update(_unpack_shard_grads(PACK2, _sum8(r2, "sum_grads_narrow")))
    small_sum = _sum8(g_small, "sum_small_grads").reshape(-1)
    off = 0
    for name, per_layer in SMALL:
        grads[name] = small_sum[off:off + DEPTH * per_layer].reshape(params[name].shape if name != "conv_w" else (DEPTH, CONV_WIDTH, CONV_CH))
        off += DEPTH * per_layer
    grads["conv_w"] = lax.dynamic_slice_in_dim(grads["conv_w"], me * cw_cols, cw_cols, axis=2)
    cols = mod_w.shape[2]
    dmod_all = g_small.reshape(N_DEV, -1)[:, :DEPTH * 6 * D_MODEL].reshape(N_DEV, DEPTH, 6 * D_MODEL)
    dmod_mine = lax.dynamic_slice_in_dim(dmod_all, me * cols, cols, axis=2).transpose(1, 0, 2)
    grads["mod_w"] = _modw_grad(c_all, dmod_mine, "mod_w_grad")

    delta, new_m, new_v = {}, {}, {}
    small_names = [n for n, _ in SMALL]
    for name in WEIGHTS:
        if name not in small_names:
            delta[name], new_m[name], new_v[name] = _adamw_nd(params[name], grads[name], mom1[name], mom2[name], f"adamw_{name}")
    packed = [_pack_rows([d[n] for n in small_names], LANE) for d in (params, grads, mom1, mom2)]
    outs = [o.reshape(-1) for o in _adamw(*packed, "adamw_small")]
    off = 0
    for name in small_names:
        size = params[name].size
        for dst, o in zip((delta, new_m, new_v), outs):
            dst[name] = o[off:off + size].reshape(params[name].shape)
        off += size
    return (loss, dx[None], *[grads[n] for n in WEIGHTS], *[delta[n] for n in WEIGHTS],
            *[new_m[n] for n in WEIGHTS], *[new_v[n] for n in WEIGHTS])
```

```python
import functools

import jax
import jax.numpy as jnp
import numpy as np
from jax import lax
from jax.experimental import pallas as pl
from jax.experimental.pallas import tpu as pltpu

F32 = jnp.float32
BF16 = jnp.bfloat16

D_MODEL = 1024
DEPTH = 2
N_DEV = 8
CONV_CH = 512
CONV_WIDTH = 31
CONV_HALO = 32
HG_HEADS = 4
HG_DK = 128
SB_HEADS = 8
SB_DH = 64
D_IN = 7680
D_FF = 4096
EPS = 1e-6
SB_BLK = 128
SB_DEAD = -104.0
SB_FIXED = 3
HG_CHUNK = 64

ADAM_LR = 0.001
ADAM_B1 = 0.9
ADAM_B2 = 0.999
ADAM_EPS = 1e-08
ADAM_WD = 0.01
ADAM_STEP = 10

VMEM_LIMIT = 48 * 1024 * 1024

NN = ((1,), (0,))
NT = ((1,), (1,))
TN = ((0,), (0,))
_DIMS = {"nn": NN, "nt": NT, "tn": TN}


def _sds(shape, dtype):
    return jax.ShapeDtypeStruct(shape, dtype)


def _params(*semantics):
    return pltpu.CompilerParams(dimension_semantics=semantics, vmem_limit_bytes=VMEM_LIMIT)


def _dot(a, b, dims):
    return lax.dot_general(a, b, (dims, ((), ())), preferred_element_type=F32)


@functools.partial(jax.custom_vjp, nondiff_argnums=(2,))
def _bdot(a, b, mode):
    return _dot(a.astype(BF16), b.astype(BF16), _DIMS[mode])


def _bdot_fwd(a, b, mode):
    return _bdot(a, b, mode), (a.astype(BF16), b.astype(BF16))


def _bdot_bwd(mode, res, g):
    a, b = res
    g = g.astype(BF16)
    if mode == "nn":
        return _dot(g, b, NT), _dot(a, g, TN)
    if mode == "nt":
        return _dot(g, b, NN), _dot(g, a, TN)
    return _dot(b, g, NT), _dot(a, g, NN)


_bdot.defvjp(_bdot_fwd, _bdot_bwd)


def _split(x):
    hi = x.astype(BF16)
    lo = (x - hi.astype(F32)).astype(BF16)
    return hi, lo


def _xdot_right(x, m, dims=NN):
    hi, lo = _split(x)
    return _dot(hi, m, dims) + _dot(lo, m, dims)


def _xdot_left(m, x, dims=NN):
    hi, lo = _split(x)
    return _dot(m, hi, dims) + _dot(m, lo, dims)


@jax.custom_vjp
def _xr(x, m):
    return _xdot_right(x, m)


def _xr_fwd(x, m):
    return _xdot_right(x, m), m


def _xr_bwd(m, g):
    return _xdot_right(g, m, NT), jnp.zeros_like(m)


_xr.defvjp(_xr_fwd, _xr_bwd)


def _norm_mod(x, g, sc, sh):
    r = lax.rsqrt(jnp.mean(x * x, axis=-1, keepdims=True) + EPS)
    return x * r * g * (1.0 + sc) + sh


MESH = pl.DeviceIdType.MESH
HBM_SPEC = pl.BlockSpec(memory_space=pltpu.HBM)


def _mesh_place():
    return lax.axis_index("x"), lax.axis_index("y"), lax.axis_index("c")


def _block_of(px, py, pc):
    return 4 * px + 2 * py + pc


def _sem_scratch(n):
    return [pltpu.SemaphoreType.DMA((n, N_DEV - 1)), pltpu.SemaphoreType.DMA((n, N_DEV - 1)), pltpu.SemaphoreType.DMA((n,))]


class _GatherPlan:
    def __init__(self, xs):
        self.xs = list(xs)
        self.n = len(self.xs)
        self.out_shape = [_sds((N_DEV, *v.shape), v.dtype) for v in self.xs]
        self.scratch = _sem_scratch(self.n)

    def _parts(self, x_refs, out_refs, sems):
        send_sems, recv_sems, local_sems = sems
        x, y, c = _mesh_place()
        me, sibling = (x, y, c), (x, y, 1 - c)
        chips = [(1 - x, y), (x, 1 - y), (1 - x, 1 - y)]

        def copy(a, k, block, to, src=None):
            rows = out_refs[a].at[_block_of(*block)]
            return pltpu.make_async_remote_copy(
                src_ref=rows if src is None else src, dst_ref=rows, send_sem=send_sems.at[a, k],
                recv_sem=recv_sems.at[a, k], device_id=to, device_id_type=MESH)

        local = [pltpu.make_async_copy(x_refs[a], out_refs[a].at[_block_of(*me)], local_sems.at[a])
                 for a in range(self.n)]
        first = []
        for a in range(self.n):
            first.append(copy(a, 0, me, sibling, src=x_refs[a]))
            first += [copy(a, 1 + j, me, (*chip, c), src=x_refs[a]) for j, chip in enumerate(chips)]
        return me, sibling, chips, c, copy, local, first

    def start(self, x_refs, out_refs, sems):
        *_, local, first = self._parts(x_refs, out_refs, sems)
        for cp in local + first:
            cp.start()

    def finish(self, x_refs, out_refs, sems):
        me, sibling, chips, c, copy, local, first = self._parts(x_refs, out_refs, sems)
        passed = []
        for j, chip in enumerate(chips):
            for a in range(self.n):
                copy(a, 1 + j, (*chip, c), me).wait_recv()
                fwd = copy(a, 4 + j, (*chip, c), sibling)
                fwd.start()
                passed.append(fwd)
        for a in range(self.n):
            copy(a, 0, sibling, me).wait_recv()
            for j, chip in enumerate(chips):
                copy(a, 4 + j, (*chip, 1 - c), me).wait_recv()
        for cp in first + passed:
            cp.wait_send()
        for cp in local:
            cp.wait()


class _ExchangePlan:
    def __init__(self, xs):
        self.xs = list(xs)
        self.n = len(self.xs)
        self.out_shape = [_sds(v.shape, v.dtype) for v in self.xs]
        self.scratch = _sem_scratch(self.n)

    def _parts(self, in_refs, out_refs, sems):
        send_sems, recv_sems, local_sems = sems
        x, y, c = _mesh_place()
        mine = _block_of(x, y, c)
        peers = [(1 - x if k & 4 else x, 1 - y if k & 2 else y, 1 - c if k & 1 else c) for k in range(1, N_DEV)]

        def copy(a, k, slot_src, slot_dst):
            return pltpu.make_async_remote_copy(
                src_ref=in_refs[a].at[slot_src], dst_ref=out_refs[a].at[slot_dst], send_sem=send_sems.at[a, k],
                recv_sem=recv_sems.at[a, k], device_id=peers[k], device_id_type=MESH)

        local = [pltpu.make_async_copy(in_refs[a].at[mine], out_refs[a].at[mine], local_sems.at[a])
                 for a in range(self.n)]
        sends = [copy(a, k, _block_of(*peers[k]), mine) for a in range(self.n) for k in range(N_DEV - 1)]
        arrivals = [copy(a, k, _block_of(*peers[k]), _block_of(*peers[k])) for a in range(self.n) for k in range(N_DEV - 1)]
        return local, sends, arrivals

    def start(self, in_refs, out_refs, sems):
        local, sends, _ = self._parts(in_refs, out_refs, sems)
        for cp in local + sends:
            cp.start()

    def finish(self, in_refs, out_refs, sems):
        local, sends, arrivals = self._parts(in_refs, out_refs, sems)
        for cp in arrivals:
            cp.wait_recv()
        for cp in sends:
            cp.wait_send()
        for cp in local:
            cp.wait()


def _call(body, args, *, grid, in_specs, out_specs, out_shape, scratch_shapes=(), semantics, name, comm=None):
    if comm is None:
        return pl.pallas_call(
            body, grid=grid, in_specs=list(in_specs), out_specs=list(out_specs), out_shape=list(out_shape),
            scratch_shapes=list(scratch_shapes), compiler_params=_params(*semantics), name=name)(*args)
    n_in, n_out, n_scr, n = len(in_specs), len(out_specs), len(scratch_shapes), comm.n

    def hosted(*refs):
        ins, rest = refs[:n_in], refs[n_in:]
        cin, rest = rest[:n], rest[n:]
        outs, rest = rest[:n_out], rest[n_out:]
        cout, rest = rest[:n], rest[n:]
        scr, sems = rest[:n_scr], rest[n_scr:]
        pids = [pl.program_id(d) for d in range(len(grid))]
        first = functools.reduce(jnp.logical_and, [p == 0 for p in pids])
        last = functools.reduce(jnp.logical_and, [p == g - 1 for p, g in zip(pids, grid)])

        @pl.when(first)
        def _():
            comm.start(cin, cout, sems)

        body(*ins, *outs, *scr)

        @pl.when(last)
        def _():
            comm.finish(cin, cout, sems)

    res = pl.pallas_call(
        hosted, grid=grid, in_specs=list(in_specs) + [HBM_SPEC] * n, out_specs=list(out_specs) + [HBM_SPEC] * n,
        out_shape=list(out_shape) + comm.out_shape, scratch_shapes=list(scratch_shapes) + comm.scratch,
        compiler_params=_params(*["arbitrary"] * len(grid)), name=name)(*args, *comm.xs)
    return res[:n_out], res[n_out:]


def _comm_alone(comm, name):
    def body(*refs):
        n = comm.n
        comm.start(refs[:n], refs[n:2 * n], refs[2 * n:])
        comm.finish(refs[:n], refs[n:2 * n], refs[2 * n:])

    return pl.pallas_call(
        body, in_specs=[HBM_SPEC] * comm.n, out_specs=[HBM_SPEC] * comm.n, out_shape=comm.out_shape,
        scratch_shapes=comm.scratch, name=name)(*comm.xs)


def _matmul(a, b, mode, out_dtype, tm, tn, tk, name, comm=None):
    if mode == "nn":
        (m, k), (_, n) = a.shape, b.shape
    elif mode == "nt":
        (m, k), (n, _) = a.shape, b.shape
    else:
        (k, m), (_, n) = a.shape, b.shape
    tm, tn, tk = min(tm, m), min(tn, n), min(tk, k)
    assert m % tm == 0 and n % tn == 0 and k % tk == 0, (name, m, n, k, tm, tn, tk)
    nk = k // tk
    dims = _DIMS[mode]

    def body(a_ref, b_ref, o_ref, acc_ref):
        kk = pl.program_id(2)

        @pl.when(kk == 0)
        def _():
            acc_ref[...] = jnp.zeros_like(acc_ref)

        acc_ref[...] += _dot(a_ref[...], b_ref[...], dims)

        @pl.when(kk == nk - 1)
        def _():
            o_ref[...] = acc_ref[...].astype(out_dtype)

    if mode == "tn":
        a_spec = pl.BlockSpec((tk, tm), lambda i, j, kk: (kk, i))
        b_spec = pl.BlockSpec((tk, tn), lambda i, j, kk: (kk, j))
    elif mode == "nn":
        a_spec = pl.BlockSpec((tm, tk), lambda i, j, kk: (i, kk))
        b_spec = pl.BlockSpec((tk, tn), lambda i, j, kk: (kk, j))
    else:
        a_spec = pl.BlockSpec((tm, tk), lambda i, j, kk: (i, kk))
        b_spec = pl.BlockSpec((tn, tk), lambda i, j, kk: (j, kk))
    res = _call(
        body, (a, b), grid=(m // tm, n // tn, nk), in_specs=[a_spec, b_spec],
        out_specs=[pl.BlockSpec((tm, tn), lambda i, j, kk: (i, j))],
        out_shape=[_sds((m, n), out_dtype)], scratch_shapes=[pltpu.VMEM((tm, tn), F32)],
        semantics=("parallel", "parallel", "arbitrary"), name=name, comm=comm)
    return res[0] if comm is None else (res[0][0], res[1])


ROW_T = 512


def _prenorm(x, pv, name):
    s, d = x.shape
    t = min(ROW_T, s)

    def body(x_ref, pv_ref, h_ref):
        h = _norm_mod(x_ref[...], pv_ref[6:7, :], pv_ref[1:2, :], pv_ref[0:1, :])
        h_ref[...] = h.astype(BF16)

    return pl.pallas_call(
        body, grid=(s // t,),
        in_specs=[pl.BlockSpec((t, d), lambda i: (i, 0)), pl.BlockSpec((16, d), lambda i: (0, 0))],
        out_specs=pl.BlockSpec((t, d), lambda i: (i, 0)), out_shape=_sds((s, d), BF16),
        compiler_params=_params("parallel"), name=name)(x, pv)


def _prenorm_bwd(dh, dres, x, pv, name):
    s, d = x.shape
    t = min(ROW_T, s)

    def body(dh_ref, dres_ref, x_ref, pv_ref, dx_ref, sg_ref):
        i = pl.program_id(0)

        @pl.when(i == 0)
        def _():
            sg_ref[...] = jnp.zeros_like(sg_ref)

        _, vjp = jax.vjp(_norm_mod, x_ref[...], pv_ref[6:7, :], pv_ref[1:2, :], pv_ref[0:1, :])
        dx, dg, dsc, dsh = vjp(dh_ref[...])
        dx_ref[...] = dres_ref[...] + dx
        sg_ref[0:1, :] += dsh
        sg_ref[1:2, :] += dsc
        sg_ref[2:3, :] += dg

    row = pl.BlockSpec((t, d), lambda i: (i, 0))
    return pl.pallas_call(
        body, grid=(s // t,),
        in_specs=[row, row, row, pl.BlockSpec((16, d), lambda i: (0, 0))],
        out_specs=[row, pl.BlockSpec((8, d), lambda i: (0, 0))],
        out_shape=[_sds((s, d), F32), _sds((8, d), F32)],
        compiler_params=_params("arbitrary"), name=name)(dh, dres, x, pv)


CONV_T = 256


def _conv_tile(a_ext, g_ext, w, b, ln_g, ln_b, n_out):
    u0 = a_ext * jax.nn.sigmoid(g_ext)
    off = CONV_HALO - (CONV_WIDTH - 1)
    acc = jnp.zeros((n_out, u0.shape[1]), F32) + b
    for k in range(CONV_WIDTH):
        acc = acc + w[k:k + 1, :] * u0[off + k: off + k + n_out, :]
    mu = jnp.mean(acc, axis=-1, keepdims=True)
    var = jnp.mean(jnp.square(acc - mu), axis=-1, keepdims=True)
    y = (acc - mu) * lax.rsqrt(var + EPS) * ln_g + ln_b
    return y * jax.nn.sigmoid(y)


def _conv_fwd(proj, conv_w, cp, name):
    s = proj.shape[0]
    t = min(CONV_T, s)
    c, h = CONV_CH, CONV_HALO

    def body(ap_ref, ac_ref, gp_ref, gc_ref, w_ref, cp_ref, o_ref):
        i = pl.program_id(0)
        live = (i > 0).astype(F32)
        a_ext = jnp.concatenate([ap_ref[t - h:, :] * live, ac_ref[...]], axis=0)
        g_ext = jnp.concatenate([gp_ref[t - h:, :], gc_ref[...]], axis=0)
        u = _conv_tile(a_ext, g_ext, w_ref[...], cp_ref[0:1, :], cp_ref[1:2, :], cp_ref[2:3, :], t)
        o_ref[...] = u.astype(BF16)

    prev = lambda col: pl.BlockSpec((t, c), lambda i: (jnp.maximum(i - 1, 0), col))
    cur = lambda col: pl.BlockSpec((t, c), lambda i: (i, col))
    return pl.pallas_call(
        body, grid=(s // t,),
        in_specs=[prev(0), cur(0), prev(1), cur(1),
                  pl.BlockSpec((CONV_WIDTH, c), lambda i: (0, 0)), pl.BlockSpec((8, c), lambda i: (0, 0))],
        out_specs=pl.BlockSpec((t, c), lambda i: (i, 0)), out_shape=_sds((s, c), BF16),
        compiler_params=_params("parallel"), name=name)(proj, proj, proj, proj, conv_w, cp)


def _conv_bwd(proj, do, conv_w, cp, name):
    s = proj.shape[0]
    t = min(CONV_T, s)
    c, h = CONV_CH, CONV_HALO
    nt = s // t

    def body(ap_ref, ac_ref, an_ref, gp_ref, gc_ref, gn_ref, doc_ref, don_ref, w_ref, cp_ref,
             da_ref, dg_ref, dw_ref, sg_ref):
        i = pl.program_id(0)

        @pl.when(i == 0)
        def _():
            dw_ref[...] = jnp.zeros_like(dw_ref)
            sg_ref[...] = jnp.zeros_like(sg_ref)

        first = (i > 0).astype(F32)
        last = (i < nt - 1).astype(F32)
        a_ext = jnp.concatenate([ap_ref[t - h:, :] * first, ac_ref[...], an_ref[:h, :] * last], axis=0)
        g_ext = jnp.concatenate([gp_ref[t - h:, :], gc_ref[...], gn_ref[:h, :]], axis=0)
        fn = functools.partial(_conv_tile, n_out=t + h)
        _, vjp = jax.vjp(fn, a_ext, g_ext, w_ref[...], cp_ref[0:1, :], cp_ref[1:2, :], cp_ref[2:3, :])
        ct_own = jnp.concatenate([doc_ref[...], jnp.zeros((h, c), F32)], axis=0)
        ct_all = jnp.concatenate([doc_ref[...], don_ref[:h, :] * last], axis=0)
        _, _, dw, db, dlg, dlb = vjp(ct_own)
        da, dg, _, _, _, _ = vjp(ct_all)
        da_ref[...] = da[h:h + t, :].astype(BF16)
        dg_ref[...] = dg[h:h + t, :].astype(BF16)
        dw_ref[...] += dw
        sg_ref[0:1, :] += db
        sg_ref[1:2, :] += dlg
        sg_ref[2:3, :] += dlb

    prev = lambda col: pl.BlockSpec((t, c), lambda i: (jnp.maximum(i - 1, 0), col))
    cur = lambda col: pl.BlockSpec((t, c), lambda i: (i, col))
    nxt = lambda col: pl.BlockSpec((t, c), lambda i: (jnp.minimum(i + 1, nt - 1), col))
    return pl.pallas_call(
        body, grid=(nt,),
        in_specs=[prev(0), cur(0), nxt(0), prev(1), cur(1), nxt(1), cur(0), nxt(0),
                  pl.BlockSpec((CONV_WIDTH, c), lambda i: (0, 0)), pl.BlockSpec((8, c), lambda i: (0, 0))],
        out_specs=[cur(0), cur(0), pl.BlockSpec((CONV_WIDTH, c), lambda i: (0, 0)),
                   pl.BlockSpec((8, c), lambda i: (0, 0))],
        out_shape=[_sds((s, c), BF16), _sds((s, c), BF16), _sds((CONV_WIDTH, c), F32), _sds((8, c), F32)],
        compiler_params=_params("arbitrary"), name=name)(proj, proj, proj, proj, proj, proj, do, do, conv_w, cp)


def _hgrn_levels(c):
    out, m = [], c // 2
    while m >= 1:
        out.append(m)
        m //= 2
    return out


def _hgrn_consts(c):
    t = np.arange(c)[:, None]
    j = np.arange(c)[None, :]
    mats = [j <= t, j > t]
    for m in _hgrn_levels(c):
        same = (t // m) == (j // m)
        mats += [same & (j <= t), same & (j > t)]
    return jnp.asarray(np.concatenate(mats, axis=0).astype(np.float32), dtype=BF16)


@jax.custom_vjp
def _cums(lc, mall):
    c = lc.shape[0]
    full = _xdot_left(mall, lc)
    return tuple(full[i * c:(i + 1) * c, :] for i in range(mall.shape[0] // c))


def _cums_fwd(lc, mall):
    return _cums(lc, mall), mall


def _cums_bwd(mall, cts):
    return _xdot_left(mall, jnp.concatenate(cts, axis=0), TN), jnp.zeros_like(mall)


_cums.defvjp(_cums_fwd, _cums_bwd)


def _hgrn_chunk(q, f, v, g, lbs, ng, st_in, mall):
    c = q.shape[0]
    keep = jax.nn.sigmoid(-f)
    if lbs:
        keep = (1.0 - jax.nn.sigmoid(lbs[1] - lbs[0])) * keep
    lc = jnp.log1p(-keep)
    qs = q * jax.nn.sigmoid(q)
    cs = _cums(lc, mall)
    o = _bdot(qs * jnp.exp(cs[0]), st_in, "nt")
    total = jnp.sum(lc, axis=0, keepdims=True)
    st_out = st_in * jnp.exp(total) + _bdot(v, keep * jnp.exp(cs[1]), "tn")
    r = lax.broadcasted_iota(jnp.int32, q.shape, 0)
    tt = lax.broadcasted_iota(jnp.int32, (c, c), 0)
    ss = lax.broadcasted_iota(jnp.int32, (c, c), 1)
    sc = jnp.where(tt == ss, jnp.sum(qs * keep, axis=-1, keepdims=True), 0.0)
    for li, m in enumerate(_hgrn_levels(c)):
        lg = m.bit_length() - 1
        odd = ((r >> lg) & 1) == 1
        qm = jnp.where(odd, qs * jnp.exp(cs[2 + 2 * li]), 0.0)
        km = jnp.where(odd, 0.0, keep * jnp.exp(cs[3 + 2 * li]))
        pair = (((tt >> lg) & 1) == 1) & ((ss >> lg) == (tt >> lg) - 1)
        sc = sc + jnp.where(pair, _bdot(qm, km, "nt"), 0.0)
    o = o + _bdot(sc, v, "nn")
    on = o * lax.rsqrt(jnp.mean(o * o, axis=-1, keepdims=True) + EPS) * ng
    return on * (g * jax.nn.sigmoid(g)), st_out


def _hgrn_fwd(proj, lb, ng, name, comm=None):
    s = proj.shape[0]
    c = HG_CHUNK
    nc = s // c
    mall = _hgrn_consts(c)
    col0 = 1024 // (HG_HEADS * HG_DK)

    def body(*refs):
        q_ref, f_ref, v_ref, g_ref = refs[:4]
        if lb is None:
            ng_ref, m_ref, y_ref, st_ref, scr = refs[4:]
        else:
            lb_ref, ng_ref, m_ref, y_ref, st_ref, scr = refs[4:]
        ci = pl.program_id(0)

        @pl.when(ci == 0)
        def _():
            scr[...] = jnp.zeros_like(scr)

        mall_v = m_ref[...]
        for h in range(HG_HEADS):
            hs = slice(h * HG_DK, (h + 1) * HG_DK)
            lbs = () if lb is None else (lb_ref[0:1, hs], lb_ref[1:2, hs])
            st_in = scr[h]
            st_ref[h] = st_in
            y, st_out = _hgrn_chunk(q_ref[:, hs], f_ref[:, hs], v_ref[:, hs], g_ref[:, hs], lbs, ng_ref[...], st_in, mall_v)
            y_ref[:, hs] = y.astype(BF16)
            scr[h] = st_out

    w = HG_HEADS * HG_DK
    col = lambda k: pl.BlockSpec((c, w), lambda ci: (ci, col0 + k))
    in_specs = [col(0), col(1), col(2), col(3)]
    args = [proj, proj, proj, proj]
    if lb is not None:
        in_specs.append(pl.BlockSpec((2, w), lambda ci: (0, 0)))
        args.append(lb)
    in_specs += [pl.BlockSpec((1, HG_DK), lambda ci: (0, 0)), pl.BlockSpec(mall.shape, lambda ci: (0, 0))]
    args += [ng, mall]
    return _call(
        body, args, grid=(nc,), in_specs=in_specs,
        out_specs=[pl.BlockSpec((c, w), lambda ci: (ci, 0)),
                   pl.BlockSpec((HG_HEADS, None, HG_DK, HG_DK), lambda ci: (0, ci, 0, 0))],
        out_shape=[_sds((s, w), BF16), _sds((HG_HEADS, nc, HG_DK, HG_DK), F32)],
        scratch_shapes=[pltpu.VMEM((HG_HEADS, HG_DK, HG_DK), F32)],
        semantics=("arbitrary",), name=name, comm=comm)


def _hgrn_bwd(proj, states, dy, lb, ng, name, comm=None):
    s = proj.shape[0]
    c = HG_CHUNK
    nc = s // c
    mall = _hgrn_consts(c)
    col0 = 1024 // (HG_HEADS * HG_DK)

    def body(*refs):
        q_ref, f_ref, v_ref, g_ref, st_ref, dy_ref = refs[:6]
        if lb is None:
            ng_ref, m_ref, dq_ref, df_ref, dv_ref, dg_ref, dlb_ref, dng_ref, scr = refs[6:]
        else:
            lb_ref, ng_ref, m_ref, dq_ref, df_ref, dv_ref, dg_ref, dlb_ref, dng_ref, scr = refs[6:]
        ci = pl.program_id(0)

        @pl.when(ci == 0)
        def _():
            scr[...] = jnp.zeros_like(scr)
            dlb_ref[...] = jnp.zeros_like(dlb_ref)
            dng_ref[...] = jnp.zeros_like(dng_ref)

        mall_v = m_ref[...]
        fn = lambda q, f, v, g, lbs_, ng_, st: _hgrn_chunk(q, f, v, g, lbs_, ng_, st, mall_v)
        for h in range(HG_HEADS):
            hs = slice(h * HG_DK, (h + 1) * HG_DK)
            lbs = () if lb is None else (lb_ref[0:1, hs], lb_ref[1:2, hs])
            _, vjp = jax.vjp(fn, q_ref[:, hs], f_ref[:, hs], v_ref[:, hs], g_ref[:, hs], lbs, ng_ref[...], st_ref[h])
            dq, df, dv, dg, dlbs, dng, dst = vjp((dy_ref[:, hs], scr[h]))
            dq_ref[:, hs] = dq.astype(BF16)
            df_ref[:, hs] = df.astype(BF16)
            dv_ref[:, hs] = dv.astype(BF16)
            dg_ref[:, hs] = dg.astype(BF16)
            scr[h] = dst
            dng_ref[0:1, :] += dng
            if lbs:
                dlb_ref[0:1, hs] += dlbs[0]
                dlb_ref[1:2, hs] += dlbs[1]

    w = HG_HEADS * HG_DK
    rev = lambda ci: nc - 1 - ci
    col = lambda k: pl.BlockSpec((c, w), lambda ci: (rev(ci), col0 + k))
    out_col = pl.BlockSpec((c, w), lambda ci: (rev(ci), 0))
    in_specs = [col(0), col(1), col(2), col(3),
                pl.BlockSpec((HG_HEADS, None, HG_DK, HG_DK), lambda ci: (0, rev(ci), 0, 0)), out_col]
    args = [proj, proj, proj, proj, states, dy]
    if lb is not None:
        in_specs.append(pl.BlockSpec((2, w), lambda ci: (0, 0)))
        args.append(lb)
    in_specs += [pl.BlockSpec((1, HG_DK), lambda ci: (0, 0)), pl.BlockSpec(mall.shape, lambda ci: (0, 0))]
    args += [ng, mall]
    return _call(
        body, args, grid=(nc,), in_specs=in_specs,
        out_specs=[out_col, out_col, out_col, out_col,
                   pl.BlockSpec((2, w), lambda ci: (0, 0)), pl.BlockSpec((8, HG_DK), lambda ci: (0, 0))],
        out_shape=[_sds((s, w), BF16)] * 4 + [_sds((2, w), F32), _sds((8, HG_DK), F32)],
        scratch_shapes=[pltpu.VMEM((HG_HEADS, HG_DK, HG_DK), F32)],
        semantics=("arbitrary",), name=name, comm=comm)


def _head_avg():
    w = SB_HEADS * SB_DH
    i = np.arange(w)
    return jnp.asarray(((i[:, None] // SB_DH) == (i[None, :] // SB_DH)).astype(np.float32) / SB_DH, dtype=BF16)


def _sb_norm(x, g_tiled, avg):
    ms = _xr(x * x, avg)
    return x * lax.rsqrt(ms + EPS) * g_tiled


def _sb_prep(proj, gq, gk, name):
    s = proj.shape[0]
    t = min(ROW_T, s)
    w = SB_HEADS * SB_DH
    avg = _head_avg()

    def body(q_ref, k_ref, v_ref, gq_ref, gk_ref, avg_ref, qn_ref, kn_ref, vb_ref):
        qn_ref[...] = _sb_norm(q_ref[...], gq_ref[...], avg_ref[...]).astype(BF16)
        kn_ref[...] = _sb_norm(k_ref[...], gk_ref[...], avg_ref[...]).astype(BF16)
        vb_ref[...] = v_ref[...].astype(BF16)

    col = lambda k: pl.BlockSpec((t, w), lambda i: (i, 6 + k))
    vec = pl.BlockSpec((1, w), lambda i: (0, 0))
    out = pl.BlockSpec((t, w), lambda i: (i, 0))
    return pl.pallas_call(
        body, grid=(s // t,), in_specs=[col(0), col(1), col(2), vec, vec, pl.BlockSpec((w, w), lambda i: (0, 0))],
        out_specs=[out, out, out], out_shape=[_sds((s, w), BF16)] * 3,
        compiler_params=_params("parallel"), name=name)(proj, proj, proj, gq, gk, avg)


def _sb_prep_bwd(proj, dqn, dkn, gq, gk, name):
    s = proj.shape[0]
    t = min(ROW_T, s)
    w = SB_HEADS * SB_DH
    avg = _head_avg()

    def body(q_ref, k_ref, dqn_ref, dkn_ref, gq_ref, gk_ref, avg_ref, dq_ref, dk_ref, sg_ref):
        i = pl.program_id(0)

        @pl.when(i == 0)
        def _():
            sg_ref[...] = jnp.zeros_like(sg_ref)

        avg_v = avg_ref[...]
        fn = lambda x, g: _sb_norm(x, g, avg_v)
        _, vq = jax.vjp(fn, q_ref[...], gq_ref[...])
        dq, dgq = vq(dqn_ref[...])
        _, vk = jax.vjp(fn, k_ref[...], gk_ref[...])
        dk, dgk = vk(dkn_ref[...])
        dq_ref[...] = dq.astype(BF16)
        dk_ref[...] = dk.astype(BF16)
        sg_ref[0:1, :] += dgq
        sg_ref[1:2, :] += dgk

    col = lambda k: pl.BlockSpec((t, w), lambda i: (i, 6 + k))
    vec = pl.BlockSpec((1, w), lambda i: (0, 0))
    row = pl.BlockSpec((t, w), lambda i: (i, 0))
    return pl.pallas_call(
        body, grid=(s // t,),
        in_specs=[col(0), col(1), row, row, vec, vec, pl.BlockSpec((w, w), lambda i: (0, 0))],
        out_specs=[row, row, pl.BlockSpec((8, w), lambda i: (0, 0))],
        out_shape=[_sds((s, w), BF16), _sds((s, w), BF16), _sds((8, w), F32)],
        compiler_params=_params("arbitrary"), name=name)(proj, proj, dqn, dkn, gq, gk, avg)


def _sb_tri(kind):
    j = np.arange(SB_BLK)[:, None]
    s = np.arange(SB_BLK)[None, :]
    tri = (j > s) if kind == "suffix" else (j < s)
    return jnp.asarray(np.concatenate([tri, np.ones_like(tri)], axis=1).astype(np.float32), dtype=BF16)


def _sb_scores(qm, kblk, mask):
    z = _dot(qm, kblk, NT) * (SB_DH ** -0.5)
    sp = jnp.maximum(z, 0.0) + jnp.log(1.0 + jnp.exp(-jnp.abs(z)))
    return z, sp, jnp.where(mask, -sp, 0.0)


def _sb_setup(b):
    lane = lax.broadcasted_iota(jnp.int32, (2 * b, b), 1)
    row = lax.broadcasted_iota(jnp.int32, (2 * b, b), 0)
    mine = (row >> (b.bit_length() - 1)) == (lane >> (SB_DH.bit_length() - 1))
    return lane, row & (b - 1), mine


def _sb_fwd(qn, kn, vb, name, comm=None):
    s, w = qn.shape
    b = SB_BLK
    nq = s // b
    tri = _sb_tri("suffix")

    def body(q_ref, k_ref, v_ref, tri_ref, o_ref):
        i = pl.program_id(1)
        lane, tt, mine = _sb_setup(b)
        q = q_ref[...]
        q2 = jnp.concatenate([q, q], axis=0)
        qm = jnp.where(mine, q2, jnp.zeros_like(q2))
        tri_v = tri_ref[...]

        def block(kb, lim, run, acc):
            off = pl.multiple_of(kb * b, b)
            kblk = k_ref[pl.ds(off, b), :]
            vblk = v_ref[pl.ds(off, b), :]
            mask = lane < lim
            z, sp, lk = _sb_scores(qm, kblk, mask)
            both = _xdot_right(lk, tri_v)
            a = jnp.where(mask, jnp.exp(z - sp + both[:, :b] + run), 0.0)
            return run + both[:, b:], acc + _dot(a.astype(BF16), vblk, NN)

        run = acc = jnp.zeros((2 * b, b), F32)
        for j in range(SB_FIXED):
            lim = tt if j == 0 else jnp.where(i >= j, b, 0)
            run, acc = block(jnp.maximum(i - j, 0), lim, run, acc)

        def cond(carry):
            j, run_, _ = carry
            return (j <= i) & (jnp.max(run_) > SB_DEAD)

        def step(carry):
            j, run_, acc_ = carry
            run_, acc_ = block(i - j, b, run_, acc_)
            return j + 1, run_, acc_

        _, _, acc = lax.while_loop(cond, step, (jnp.int32(SB_FIXED), run, acc))
        o_ref[...] = jnp.where(lane[:b] < SB_DH, acc[:b], acc[b:]).astype(BF16)

    blk = pl.BlockSpec((b, b), lambda p, i: (i, p))
    full = pl.BlockSpec((s, b), lambda p, i: (0, p))
    return _call(
        body, (qn, kn, vb, tri), grid=(w // b, nq),
        in_specs=[blk, full, full, pl.BlockSpec(tri.shape, lambda p, i: (0, 0))],
        out_specs=[blk], out_shape=[_sds((s, w), BF16)],
        semantics=("parallel", "arbitrary"), name=name, comm=comm)


def _sb_bwd(qn, kn, vb, do, name, comm=None):
    s, w = qn.shape
    b = SB_BLK
    nq = s // b
    tri_s = _sb_tri("suffix")
    tri_p = _sb_tri("prefix")
    scale = SB_DH ** -0.5

    def body(q_ref, k_ref, v_ref, do_ref, ts_ref, tp_ref, dq_ref, dk_ref, dv_ref, dk_acc, dv_acc, dp_scr):
        i = pl.program_id(1)

        @pl.when(i == 0)
        def _():
            dk_acc[...] = jnp.zeros_like(dk_acc)
            dv_acc[...] = jnp.zeros_like(dv_acc)

        lane, tt, mine = _sb_setup(b)
        q = q_ref[...]
        q2 = jnp.concatenate([q, q], axis=0)
        qm = jnp.where(mine, q2, jnp.zeros_like(q2))
        dout = do_ref[...].astype(BF16)
        d2 = jnp.concatenate([dout, dout], axis=0)
        dom = jnp.where(mine, d2, jnp.zeros_like(d2))
        ts_v = ts_ref[...]
        tp_v = tp_ref[...]
        zero = jnp.zeros((2 * b, b), F32)

        def down(kb, lim, run):
            off = pl.multiple_of(kb * b, b)
            kblk = k_ref[pl.ds(off, b), :]
            vblk = v_ref[pl.ds(off, b), :]
            mask = lane < lim
            z, sp, lk = _sb_scores(qm, kblk, mask)
            both = _xdot_right(lk, ts_v)
            a = jnp.where(mask, jnp.exp(z - sp + both[:, :b] + run), 0.0)
            dv_acc[pl.ds(off, b), :] += _dot(a.astype(BF16), dom, TN)
            return _dot(dom, vblk, NT) * a, run + both[:, b:]

        def up(kb, lim, dp, pre, dq):
            off = pl.multiple_of(kb * b, b)
            kblk = k_ref[pl.ds(off, b), :]
            sig = jax.nn.sigmoid(_dot(qm, kblk, NT) * scale)
            both = _xdot_right(dp, tp_v)
            dz = jnp.where(lane < lim, dp * (1.0 - sig) - sig * (both[:, :b] + pre), 0.0) * scale
            dz = dz.astype(BF16)
            dk_acc[pl.ds(off, b), :] += _dot(dz, qm, TN)
            return pre + both[:, b:], dq + _dot(dz, kblk, NN)

        lims = [tt if j == 0 else jnp.where(i >= j, b, 0) for j in range(SB_FIXED)]
        run = zero
        dps = []
        for j in range(SB_FIXED):
            dp, run = down(jnp.maximum(i - j, 0), lims[j], run)
            dps.append(dp)

        def cond(carry):
            j, run_ = carry
            return (j <= i) & (jnp.max(run_) > SB_DEAD)

        def sweep_down(carry):
            j, run_ = carry
            dp, run_ = down(i - j, b, run_)
            dp_scr[i - j] = dp
            return j + 1, run_

        n_live, _ = lax.while_loop(cond, sweep_down, (jnp.int32(SB_FIXED), run))

        def sweep_up(jj, carry):
            kb = i - n_live + 1 + jj
            return up(kb, b, dp_scr[kb], *carry)

        pre, dq = lax.fori_loop(0, n_live - SB_FIXED, sweep_up, (zero, zero))
        for j in reversed(range(SB_FIXED)):
            pre, dq = up(jnp.maximum(i - j, 0), lims[j], dps[j], pre, dq)
        dq_ref[...] = jnp.where(lane[:b] < SB_DH, dq[:b], dq[b:])

        @pl.when(i == nq - 1)
        def _():
            dk_ref[...] = dk_acc[...]
            dv_ref[...] = dv_acc[...].astype(BF16)

    blk = pl.BlockSpec((b, b), lambda p, i: (i, p))
    full = pl.BlockSpec((s, b), lambda p, i: (0, p))
    tri = pl.BlockSpec(tri_s.shape, lambda p, i: (0, 0))
    return _call(
        body, (qn, kn, vb, do, tri_s, tri_p), grid=(w // b, nq), in_specs=[blk, full, full, blk, tri, tri],
        out_specs=[blk, full, full], out_shape=[_sds((s, w), F32), _sds((s, w), F32), _sds((s, w), BF16)],
        scratch_shapes=[pltpu.VMEM((s, b), F32), pltpu.VMEM((s, b), F32), pltpu.VMEM((nq, 2 * b, b), F32)],
        semantics=("arbitrary", "arbitrary"), name=name, comm=comm)


MIX_T = 256
HALF = 512


def _gate_slices(ga, gb):
    return [(ga[:, 0:512], ga[:, 512:1024]), (ga[:, 1024:1536], gb[:, 0:512]), (gb[:, 512:1024], gb[:, 1024:1536])]


def _mix_fwd(u3, oh, osb, proj, x, pv, wc, wh, ws, wo, name):
    s, d = x.shape
    t = min(MIX_T, s)

    def body(u3_ref, oh_ref, os_ref, ga_ref, gb_ref, x_ref, pv_ref, wc_ref, wh_ref, ws_ref, wo_ref,
             x1_ref, h2_ref, mg_ref, mo_ref):
        ys = [_dot(u3_ref[...], wc_ref[...], NT), _dot(oh_ref[...], wh_ref[...], NT), _dot(os_ref[...], ws_ref[...], NT)]
        gl = _gate_slices(ga_ref[...], gb_ref[...])
        halves = []
        for hf in range(2):
            lo = hf * HALF
            acc = jnp.zeros((t, HALF), F32)
            for br in range(3):
                gate = jax.nn.sigmoid(gl[br][hf] + pv_ref[8 + br:9 + br, lo:lo + HALF])
                acc = acc + gate * ys[br][:, lo:lo + HALF]
            halves.append(acc)
        merged = jnp.concatenate(halves, axis=1).astype(BF16)
        mg_ref[...] = merged
        mo = _dot(merged, wo_ref[...], NN)
        mo_ref[...] = mo.astype(BF16)
        x1 = x_ref[...] + pv_ref[2:3, :] * mo
        x1_ref[...] = x1
        h2_ref[...] = _norm_mod(x1, pv_ref[7:8, :], pv_ref[4:5, :], pv_ref[3:4, :]).astype(BF16)

    br_spec = pl.BlockSpec((t, CONV_CH), lambda i: (i, 0))
    row = pl.BlockSpec((t, d), lambda i: (i, 0))
    wproj = pl.BlockSpec((d, CONV_CH), lambda i: (0, 0))
    return pl.pallas_call(
        body, grid=(s // t,),
        in_specs=[br_spec, br_spec, br_spec, pl.BlockSpec((t, 1536), lambda i: (i, 3)),
                  pl.BlockSpec((t, 1536), lambda i: (i, 4)), row, pl.BlockSpec((16, d), lambda i: (0, 0)),
                  wproj, wproj, wproj, pl.BlockSpec((d, d), lambda i: (0, 0))],
        out_specs=[row, row, row, row],
        out_shape=[_sds((s, d), F32), _sds((s, d), BF16), _sds((s, d), BF16), _sds((s, d), BF16)],
        compiler_params=_params("parallel"), name=name)(u3, oh, osb, proj, proj, x, pv, wc, wh, ws, wo)


def _mix_bwd(dx1, mo1, u3, oh, osb, proj, pv, wc, wh, ws, wo, name):
    s, d = dx1.shape
    t = min(MIX_T, s)

    def body(dx_ref, mo_ref, u3_ref, oh_ref, os_ref, ga_ref, gb_ref, pv_ref, wc_ref, wh_ref, ws_ref, wo_ref,
             dmo_ref, dyc_ref, dyh_ref, dys_ref, doc_ref, doh_ref, dos_ref, dgl_ref, sg_ref):
        i = pl.program_id(0)

        @pl.when(i == 0)
        def _():
            sg_ref[...] = jnp.zeros_like(sg_ref)

        dx = dx_ref[...]
        dmo = (dx * pv_ref[2:3, :]).astype(BF16)
        dmo_ref[...] = dmo
        sg_ref[0:1, :] += jnp.sum(dx * mo_ref[...].astype(F32), axis=0, keepdims=True)
        dmerged = _dot(dmo, wo_ref[...], NT)
        branches = [(u3_ref, wc_ref, dyc_ref, doc_ref), (oh_ref, wh_ref, dyh_ref, doh_ref), (os_ref, ws_ref, dys_ref, dos_ref)]
        gl = _gate_slices(ga_ref[...], gb_ref[...])
        for br, (o_ref, w_ref, dy_ref, do_ref) in enumerate(branches):
            y = _dot(o_ref[...], w_ref[...], NT)
            dys = []
            for hf in range(2):
                lo = hf * HALF
                gate = jax.nn.sigmoid(gl[br][hf] + pv_ref[8 + br:9 + br, lo:lo + HALF])
                dm = dmerged[:, lo:lo + HALF]
                dys.append(dm * gate)
                dgl = dm * y[:, lo:lo + HALF] * gate * (1.0 - gate)
                dgl_ref[:, br * d + lo: br * d + lo + HALF] = dgl.astype(BF16)
                sg_ref[1 + br:2 + br, lo:lo + HALF] += jnp.sum(dgl, axis=0, keepdims=True)
            dy = jnp.concatenate(dys, axis=1).astype(BF16)
            dy_ref[...] = dy
            do_ref[...] = _dot(dy, w_ref[...], NN)

    br_spec = pl.BlockSpec((t, CONV_CH), lambda i: (i, 0))
    row = pl.BlockSpec((t, d), lambda i: (i, 0))
    wproj = pl.BlockSpec((d, CONV_CH), lambda i: (0, 0))
    return pl.pallas_call(
        body, grid=(s // t,),
        in_specs=[row, row, br_spec, br_spec, br_spec, pl.BlockSpec((t, 1536), lambda i: (i, 3)),
                  pl.BlockSpec((t, 1536), lambda i: (i, 4)), pl.BlockSpec((16, d), lambda i: (0, 0)),
                  wproj, wproj, wproj, pl.BlockSpec((d, d), lambda i: (0, 0))],
        out_specs=[row, row, row, row, br_spec, br_spec, br_spec, pl.BlockSpec((t, 3 * d), lambda i: (i, 0)),
                   pl.BlockSpec((8, d), lambda i: (0, 0))],
        out_shape=[_sds((s, d), BF16)] * 4 + [_sds((s, CONV_CH), F32)] * 3 + [_sds((s, 3 * d), BF16), _sds((8, d), F32)],
        compiler_params=_params("arbitrary"), name=name)(dx1, mo1, u3, oh, osb, proj, proj, pv, wc, wh, ws, wo)


MLP_T = 512
MLP_F = 512


def _mlp_fwd(h2, x1, pv, w1t, w2, name):
    s, d = x1.shape
    t = min(MLP_T, s)
    nf = D_FF // MLP_F

    def body(h_ref, x_ref, pv_ref, w1_ref, w2_ref, x2_ref, mo_ref, acc_ref):
        f = pl.program_id(1)

        @pl.when(f == 0)
        def _():
            acc_ref[...] = jnp.zeros_like(acc_ref)

        a = jnp.maximum(_dot(h_ref[...], w1_ref[...], NT), 0.0)
        acc_ref[...] += _dot((a * a).astype(BF16), w2_ref[...], NN)

        @pl.when(f == nf - 1)
        def _():
            mo = acc_ref[...]
            mo_ref[...] = mo.astype(BF16)
            x2_ref[...] = x_ref[...] + pv_ref[5:6, :] * mo

    row = pl.BlockSpec((t, d), lambda i, f: (i, 0))
    wblk = pl.BlockSpec((MLP_F, d), lambda i, f: (f, 0))
    return pl.pallas_call(
        body, grid=(s // t, nf), in_specs=[row, row, pl.BlockSpec((16, d), lambda i, f: (0, 0)), wblk, wblk],
        out_specs=[row, row], out_shape=[_sds((s, d), F32), _sds((s, d), BF16)],
        scratch_shapes=[pltpu.VMEM((t, d), F32)],
        compiler_params=_params("parallel", "arbitrary"), name=name)(h2, x1, pv, w1t, w2)


def _mlp_bwd(dx2, h2, x1, mo2, pv, w1t, w2, name):
    s, d = x1.shape
    t = min(MLP_T, s)
    nf = D_FF // MLP_F

    def body(dx_ref, h_ref, x_ref, mo_ref, pv_ref, w1_ref, w2_ref, dx1_ref, da_ref, b_ref, dmo_ref, sg_ref, acc_ref):
        i = pl.program_id(0)
        f = pl.program_id(1)

        @pl.when((i == 0) & (f == 0))
        def _():
            sg_ref[...] = jnp.zeros_like(sg_ref)

        @pl.when(f == 0)
        def _():
            acc_ref[...] = jnp.zeros_like(acc_ref)
            dx = dx_ref[...]
            dmo_ref[...] = (dx * pv_ref[5:6, :]).astype(BF16)
            sg_ref[0:1, :] += jnp.sum(dx * mo_ref[...].astype(F32), axis=0, keepdims=True)

        r = jnp.maximum(_dot(h_ref[...], w1_ref[...], NT), 0.0)
        b_ref[...] = (r * r).astype(BF16)
        da = (_dot(dmo_ref[...], w2_ref[...], NT) * (2.0 * r)).astype(BF16)
        da_ref[...] = da
        acc_ref[...] += _dot(da, w1_ref[...], NN)

        @pl.when(f == nf - 1)
        def _():
            _, vjp = jax.vjp(_norm_mod, x_ref[...], pv_ref[7:8, :], pv_ref[4:5, :], pv_ref[3:4, :])
            dxn, dg, dsc, dsh = vjp(acc_ref[...])
            dx1_ref[...] = dx_ref[...] + dxn
            sg_ref[1:2, :] += dsh
            sg_ref[2:3, :] += dsc
            sg_ref[3:4, :] += dg

    row = pl.BlockSpec((t, d), lambda i, f: (i, 0))
    wblk = pl.BlockSpec((MLP_F, d), lambda i, f: (f, 0))
    hid = pl.BlockSpec((t, MLP_F), lambda i, f: (i, f))
    return pl.pallas_call(
        body, grid=(s // t, nf),
        in_specs=[row, row, row, row, pl.BlockSpec((16, d), lambda i, f: (0, 0)), wblk, wblk],
        out_specs=[row, hid, hid, row, pl.BlockSpec((8, d), lambda i, f: (0, 0))],
        out_shape=[_sds((s, d), F32), _sds((s, D_FF), BF16), _sds((s, D_FF), BF16), _sds((s, d), BF16), _sds((8, d), F32)],
        scratch_shapes=[pltpu.VMEM((t, d), F32)],
        compiler_params=_params("arbitrary", "arbitrary"), name=name)(dx2, h2, x1, mo2, pv, w1t, w2)


def _loss_head(y, target, name):
    s, d = y.shape
    t = min(ROW_T, s)

    def body(y_ref, t_ref, dy_ref, ls_ref):
        i = pl.program_id(0)

        @pl.when(i == 0)
        def _():
            ls_ref[...] = jnp.zeros_like(ls_ref)

        e = y_ref[...] - t_ref[...]
        dy_ref[...] = e * (1.0 / d)
        ls_ref[...] += jnp.sum((e * e).reshape(t // 8, 8, d), axis=0)

    row = pl.BlockSpec((t, d), lambda i: (i, 0))
    return pl.pallas_call(
        body, grid=(s // t,), in_specs=[row, row], out_specs=[row, pl.BlockSpec((8, d), lambda i: (0, 0))],
        out_shape=[_sds((s, d), F32), _sds((8, d), F32)],
        compiler_params=_params("arbitrary"), name=name)(y, target)


def _layer_vectors(l, mod, sm):
    d = D_MODEL
    pv = jnp.concatenate([mod[l].reshape(6, d), sm["norm1_g"][l][None], sm["norm2_g"][l][None],
                          sm["gate_b"][l].reshape(3, d), jnp.zeros((5, d), F32)], axis=0)
    cp = jnp.concatenate([sm["conv_b"][l][None], sm["conv_ln_g"][l][None], sm["conv_ln_b"][l][None],
                          jnp.zeros((5, CONV_CH), F32)], axis=0)
    return dict(pv=pv, cp=cp, conv_w=sm["conv_w"][l], lb=(sm["hgrn_lb"] if l > 0 else None),
                ng=sm["hgrn_norm_g"][l][None], gq=jnp.tile(sm["sb_qn_g"][l], SB_HEADS)[None],
                gk=jnp.tile(sm["sb_kn_g"][l], SB_HEADS)[None])


def _hosted(res, comm):
    return res if comm is not None else (res, None)


def _layer_fwd_mixers(x, vec, win_t, tag, comm_hgrn=None, comm_sb=None):
    h = _prenorm(x, vec["pv"], f"prenorm{tag}")
    proj = _matmul(h, win_t, "nt", F32, 512, 768, 1024, f"proj{tag}")
    u3 = _conv_fwd(proj, vec["conv_w"], vec["cp"], f"conv_fwd{tag}")
    (oh, states), got_hgrn = _hosted(_hgrn_fwd(proj, vec["lb"], vec["ng"], f"hgrn_fwd{tag}", comm_hgrn), comm_hgrn)
    qn, kn, vb = _sb_prep(proj, vec["gq"], vec["gk"], f"sb_prep{tag}")
    (osb,), got_sb = _hosted(_sb_fwd(qn, kn, vb, f"sb_fwd{tag}", comm_sb), comm_sb)
    saved = dict(x=x, h=h, proj=proj, u3=u3, oh=oh, states=states, qn=qn, kn=kn, vb=vb, osb=osb)
    return saved, got_hgrn, got_sb


def _layer_fwd_out(sv, vec, w, tag):
    x1, h2, merged, mo1 = _mix_fwd(sv["u3"], sv["oh"], sv["osb"], sv["proj"], sv["x"], vec["pv"],
                                   w["wc_t"], w["wh_t"], w["ws_t"], w["wo"], f"mix_fwd{tag}")
    x2, mo2 = _mlp_fwd(h2, x1, vec["pv"], w["w1_t"], w["w2"], f"mlp_fwd{tag}")
    sv.update(x1=x1, h2=h2, merged=merged, mo1=mo1, mo2=mo2)
    return x2


def _layer_bwd(dx2, sv, vec, w, tag, plans=None):
    plans = plans or {}
    got = {}

    def plan_for(key, big_now):
        return plans[key](big_now) if key in plans else None

    pv = vec["pv"]
    dx1, da, bsq, dmo2, sg_mlp = _mlp_bwd(dx2, sv["h2"], sv["x1"], sv["mo2"], pv, w["w1_t"], w["w2"], f"mlp_bwd{tag}")
    big = {}
    big["w1_t"] = _matmul(da, sv["h2"], "tn", BF16, 512, 1024, 512, f"dw1{tag}")
    big["w2"] = _matmul(bsq, dmo2, "tn", BF16, 512, 1024, 512, f"dw2{tag}")
    dmo1, dyc, dyh, dys, doc, doh, dos, dgl, sg_mix = _mix_bwd(
        dx1, sv["mo1"], sv["u3"], sv["oh"], sv["osb"], sv["proj"], pv, w["wc_t"], w["wh_t"], w["ws_t"], w["wo"], f"mix_bwd{tag}")
    big["wo"] = _matmul(sv["merged"], dmo1, "tn", BF16, 512, 1024, 512, f"dwo{tag}")
    big["wc_t"] = _matmul(dyc, sv["u3"], "tn", BF16, 512, 512, 512, f"dwc{tag}")
    big["wh_t"] = _matmul(dyh, sv["oh"], "tn", BF16, 512, 512, 512, f"dwh{tag}")
    big["ws_t"] = _matmul(dys, sv["osb"], "tn", BF16, 512, 512, 512, f"dws{tag}")
    da_c, dg_c, dconv_w, sg_conv = _conv_bwd(sv["proj"], doc, vec["conv_w"], vec["cp"], f"conv_bwd{tag}")
    comm = plan_for("hgrn", big)
    (dq_h, df_h, di_h, dg_h, dlb, dng), got["hgrn"] = _hosted(
        _hgrn_bwd(sv["proj"], sv["states"], doh, vec["lb"], vec["ng"], f"hgrn_bwd{tag}", comm), comm)
    comm = plan_for("sb", big)
    (dqn, dkn, dv_s), got["sb"] = _hosted(_sb_bwd(sv["qn"], sv["kn"], sv["vb"], dos, f"sb_bwd{tag}", comm), comm)
    dq_s, dk_s, sg_sb = _sb_prep_bwd(sv["proj"], dqn, dkn, vec["gq"], vec["gk"], f"sb_prep_bwd{tag}")
    dproj = jnp.concatenate([da_c, dg_c, dq_h, df_h, di_h, dg_h, dq_s, dk_s, dv_s, dgl], axis=1)
    big["win_t"] = _matmul(dproj, sv["h"], "tn", BF16, 768, 1024, 512, f"dwin{tag}")
    comm = plan_for("dh", big)
    dh, got["dh"] = _hosted(_matmul(dproj, w["win_t"], "nn", F32, 512, 1024, 768, f"dh{tag}", comm), comm)
    dx, sg_pre = _prenorm_bwd(dh, dx1, sv["x"], pv, f"prenorm_bwd{tag}")
    small = dict(
        mod=jnp.stack([sg_pre[0], sg_pre[1], sg_mix[0], sg_mlp[1], sg_mlp[2], sg_mlp[0]]).reshape(6 * D_MODEL),
        norm1_g=sg_pre[2], norm2_g=sg_mlp[3], gate_b=sg_mix[1:4].reshape(3 * D_MODEL),
        conv_w=dconv_w, conv_b=sg_conv[0], conv_ln_g=sg_conv[1], conv_ln_b=sg_conv[2],
        hgrn_lb=dlb, hgrn_norm_g=dng[0],
        sb_qn_g=sg_sb[0].reshape(SB_HEADS, SB_DH).sum(0), sb_kn_g=sg_sb[1].reshape(SB_HEADS, SB_DH).sum(0))
    return dx, big, small, got


def _row_tile(r, cap=512):
    t = min(r, cap)
    while r % t or (t % 8 and t != r):
        t -= 1
    return t


def _sum8(z, name):
    _, r, c = z.shape
    t = _row_tile(r, 128 if c >= 1024 else 512)

    def body(z_ref, o_ref):
        acc = z_ref[0].astype(F32)
        for j in range(1, N_DEV):
            acc = acc + z_ref[j].astype(F32)
        o_ref[...] = acc

    return pl.pallas_call(
        body, grid=(r // t,), in_specs=[pl.BlockSpec((N_DEV, t, c), lambda i: (0, i, 0))],
        out_specs=pl.BlockSpec((t, c), lambda i: (i, 0)), out_shape=_sds((r, c), F32),
        compiler_params=_params("parallel"), name=name)(z)


def _adamw(w, g, m, v, name):
    r, c = w.shape
    t = _row_tile(r, 256)

    def body(w_ref, g_ref, m_ref, v_ref, d_ref, nm_ref, nv_ref):
        g_ = g_ref[...]
        nm = ADAM_B1 * m_ref[...] + (1.0 - ADAM_B1) * g_
        nv = ADAM_B2 * v_ref[...] + (1.0 - ADAM_B2) * jnp.square(g_)
        m_hat = nm / (1.0 - ADAM_B1 ** ADAM_STEP)
        v_hat = nv / (1.0 - ADAM_B2 ** ADAM_STEP)
        d_ref[...] = -ADAM_LR * (m_hat / (jnp.sqrt(v_hat) + ADAM_EPS) + ADAM_WD * w_ref[...])
        nm_ref[...] = nm
        nv_ref[...] = nv

    blk = pl.BlockSpec((t, c), lambda i: (i, 0))
    return pl.pallas_call(
        body, grid=(r // t,), in_specs=[blk] * 4, out_specs=[blk] * 3, out_shape=[_sds((r, c), F32)] * 3,
        compiler_params=_params("parallel"), name=name)(w, g, m, v)


def _mod_local(c_all, mod_w, name):
    depth, d, cols = mod_w.shape

    def body(c_ref, w_ref, o_ref):
        cv = c_ref[...]
        act = cv * jax.nn.sigmoid(cv)
        o_ref[...] = jnp.dot(act, w_ref[...], precision=lax.Precision.HIGHEST, preferred_element_type=F32)

    return pl.pallas_call(
        body, grid=(depth,),
        in_specs=[pl.BlockSpec((N_DEV, d), lambda l: (0, 0)), pl.BlockSpec((None, d, cols), lambda l: (l, 0, 0))],
        out_specs=pl.BlockSpec((None, N_DEV, cols), lambda l: (l, 0, 0)), out_shape=_sds((depth, N_DEV, cols), F32),
        compiler_params=_params("parallel"), name=name)(c_all, mod_w)


def _modw_grad(c_all, dmod, name):
    depth, _, cols = dmod.shape
    d = c_all.shape[1]

    def body(c_ref, g_ref, o_ref):
        cv = c_ref[...]
        act = cv * jax.nn.sigmoid(cv)
        o_ref[...] = lax.dot_general(act, g_ref[...], (TN, ((), ())), precision=lax.Precision.HIGHEST,
                                     preferred_element_type=F32)

    return pl.pallas_call(
        body, grid=(depth,),
        in_specs=[pl.BlockSpec((N_DEV, d), lambda l: (0, 0)), pl.BlockSpec((None, N_DEV, cols), lambda l: (l, 0, 0))],
        out_specs=pl.BlockSpec((None, d, cols), lambda l: (l, 0, 0)), out_shape=_sds((depth, d, cols), F32),
        compiler_params=_params("parallel"), name=name)(c_all, dmod)


LANE = 128
W_IN = (("w_in", 960, True),)
WIDE_REST = (("w_out", 128, False), ("mlp_w2", 512, False), ("mlp_w1", 512, True))
NARROW = (("w_conv_proj", 128, True), ("w_hgrn_proj", 128, True), ("w_sb_proj", 128, True))
BIG_KEY = {"w_in": "win_t", "w_out": "wo", "mlp_w2": "w2", "mlp_w1": "w1_t",
           "w_conv_proj": "wc_t", "w_hgrn_proj": "wh_t", "w_sb_proj": "ws_t"}
SMALL = (("mod_b", 6144), ("norm1_g", 1024), ("gate_b", 3072), ("conv_w", CONV_WIDTH * CONV_CH), ("conv_b", 512),
         ("conv_ln_g", 512), ("conv_ln_b", 512), ("hgrn_lb", 512), ("hgrn_norm_g", 128), ("sb_qn_g", 64),
         ("sb_kn_g", 64), ("norm2_g", 1024))


def _pack_rows(parts, width):
    flat = jnp.concatenate([p.reshape(-1) for p in parts])
    rows = -(-flat.shape[0] // width)
    rows = -(-rows // 8) * 8
    return jnp.pad(flat, (0, rows * width - flat.shape[0])).reshape(rows, width)


def _pack_weights(spec, params, l):
    parts = []
    for name, _, transposed in spec:
        w = params[name][l]
        parts.append((w.T if transposed else w).astype(BF16))
    return jnp.concatenate(parts, axis=0)


def _unpack_gathered(spec, g):
    out = {}
    off = 0
    for name, rows, _ in spec:
        out[BIG_KEY[name]] = g[:, off:off + rows].reshape(N_DEV * rows, g.shape[2])
        off += rows
    return out


def _pack_grads(spec, big):
    parts = []
    for name, rows, _ in spec:
        gmat = big[BIG_KEY[name]]
        parts.append(gmat.reshape(N_DEV, rows, gmat.shape[1]))
    return jnp.concatenate(parts, axis=1)


def _unpack_shard_grads(spec, gsum):
    out = {}
    off = 0
    for name, rows, transposed in spec:
        blk = gsum[off:off + rows]
        out[name] = blk.T if transposed else blk
        off += rows
    return out


def _adamw_nd(w, g, m, v, name):
    shape = w.shape
    two = lambda a: a.reshape(-1, shape[-1])
    return [o.reshape(shape) for o in _adamw(two(w), two(g), two(m), two(v), name)]


WEIGHTS = ("mod_w", "mod_b", "norm1_g", "w_in", "gate_b", "conv_w", "conv_b", "conv_ln_g", "conv_ln_b", "w_conv_proj",
           "hgrn_lb", "hgrn_norm_g", "w_hgrn_proj", "sb_qn_g", "sb_kn_g", "w_sb_proj", "w_out", "norm2_g", "mlp_w1",
           "mlp_w2")


def kernel(x, c, mod_w, mod_b, norm1_g, w_in, gate_b, conv_w, conv_b, conv_ln_g, conv_ln_b, w_conv_proj, hgrn_lb, hgrn_norm_g, w_hgrn_proj, sb_qn_g, sb_kn_g, w_sb_proj, w_out, norm2_g, mlp_w1, mlp_w2, loss_target, m_mod_w, m_mod_b, m_norm1_g, m_w_in, m_gate_b, m_conv_w, m_conv_b, m_conv_ln_g, m_conv_ln_b, m_w_conv_proj, m_hgrn_lb, m_hgrn_norm_g, m_w_hgrn_proj, m_sb_qn_g, m_sb_kn_g, m_w_sb_proj, m_w_out, m_norm2_g, m_mlp_w1, m_mlp_w2, v_mod_w, v_mod_b, v_norm1_g, v_w_in, v_gate_b, v_conv_w, v_conv_b, v_conv_ln_g, v_conv_ln_b, v_w_conv_proj, v_hgrn_lb, v_hgrn_norm_g, v_w_hgrn_proj, v_sb_qn_g, v_sb_kn_g, v_w_sb_proj, v_w_out, v_norm2_g, v_mlp_w1, v_mlp_w2):
    params = dict(mod_w=mod_w, mod_b=mod_b, norm1_g=norm1_g, w_in=w_in, gate_b=gate_b, conv_w=conv_w, conv_b=conv_b,
                  conv_ln_g=conv_ln_g, conv_ln_b=conv_ln_b, w_conv_proj=w_conv_proj, hgrn_lb=hgrn_lb,
                  hgrn_norm_g=hgrn_norm_g, w_hgrn_proj=w_hgrn_proj, sb_qn_g=sb_qn_g, sb_kn_g=sb_kn_g,
                  w_sb_proj=w_sb_proj, w_out=w_out, norm2_g=norm2_g, mlp_w1=mlp_w1, mlp_w2=mlp_w2)
    mom1 = dict(mod_w=m_mod_w, mod_b=m_mod_b, norm1_g=m_norm1_g, w_in=m_w_in, gate_b=m_gate_b, conv_w=m_conv_w,
                conv_b=m_conv_b, conv_ln_g=m_conv_ln_g, conv_ln_b=m_conv_ln_b, w_conv_proj=m_w_conv_proj,
                hgrn_lb=m_hgrn_lb, hgrn_norm_g=m_hgrn_norm_g, w_hgrn_proj=m_w_hgrn_proj, sb_qn_g=m_sb_qn_g,
                sb_kn_g=m_sb_kn_g, w_sb_proj=m_w_sb_proj, w_out=m_w_out, norm2_g=m_norm2_g, mlp_w1=m_mlp_w1,
                mlp_w2=m_mlp_w2)
    mom2 = dict(mod_w=v_mod_w, mod_b=v_mod_b, norm1_g=v_norm1_g, w_in=v_w_in, gate_b=v_gate_b, conv_w=v_conv_w,
                conv_b=v_conv_b, conv_ln_g=v_conv_ln_g, conv_ln_b=v_conv_ln_b, w_conv_proj=v_w_conv_proj,
                hgrn_lb=v_hgrn_lb, hgrn_norm_g=v_hgrn_norm_g, w_hgrn_proj=v_w_hgrn_proj, sb_qn_g=v_sb_qn_g,
                sb_kn_g=v_sb_kn_g, w_sb_proj=v_w_sb_proj, w_out=v_w_out, norm2_g=v_norm2_g, mlp_w1=v_mlp_w1,
                mlp_w2=v_mlp_w2)
    xi, yi, ci = _mesh_place()
    me = _block_of(xi, yi, ci)
    cw_cols = conv_w.shape[2]

    tiny = _pack_rows([c, conv_w], LANE)
    g_tiny, g_win0 = _comm_alone(_GatherPlan([tiny, _pack_weights(W_IN, params, 0)]), "gather_first")
    c_rows = D_MODEL // LANE
    c_all = g_tiny[:, :c_rows].reshape(N_DEV, D_MODEL)
    n_cw = DEPTH * CONV_WIDTH * cw_cols
    conv_w_full = g_tiny[:, c_rows:c_rows + n_cw // LANE].reshape(N_DEV, DEPTH, CONV_WIDTH, cw_cols)
    conv_w_full = conv_w_full.transpose(1, 2, 0, 3).reshape(DEPTH, CONV_WIDTH, CONV_CH)

    (g_mod,) = _comm_alone(_GatherPlan([_mod_local(c_all, mod_w, "mod_local")]), "gather_mod")
    mod = lax.dynamic_index_in_dim(g_mod, me, axis=2, keepdims=False)
    mod = mod.transpose(1, 0, 2).reshape(DEPTH, 6 * D_MODEL) + mod_b

    sm = dict(norm1_g=norm1_g, norm2_g=norm2_g, gate_b=gate_b, conv_w=conv_w_full, conv_b=conv_b, conv_ln_g=conv_ln_g,
              conv_ln_b=conv_ln_b, hgrn_lb=hgrn_lb, hgrn_norm_g=hgrn_norm_g, sb_qn_g=sb_qn_g, sb_kn_g=sb_kn_g)
    vecs = [_layer_vectors(l, mod, sm) for l in range(DEPTH)]

    wts = [_unpack_gathered(W_IN, g_win0), None]
    rest0 = _GatherPlan([_pack_weights(WIDE_REST, params, 0), _pack_weights(NARROW, params, 0)])
    all1 = _GatherPlan([_pack_weights(W_IN + WIDE_REST, params, 1), _pack_weights(NARROW, params, 1)])
    sv0, got_rest0, got_all1 = _layer_fwd_mixers(x[0], vecs[0], wts[0]["win_t"], "_l0", rest0, all1)
    wts[0].update(_unpack_gathered(WIDE_REST, got_rest0[0]))
    wts[0].update(_unpack_gathered(NARROW, got_rest0[1]))
    wts[1] = _unpack_gathered(W_IN + WIDE_REST, got_all1[0])
    wts[1].update(_unpack_gathered(NARROW, got_all1[1]))
    y = _layer_fwd_out(sv0, vecs[0], wts[0], "_l0")
    sv1, _, _ = _layer_fwd_mixers(y, vecs[1], wts[1]["win_t"], "_l1")
    y = _layer_fwd_out(sv1, vecs[1], wts[1], "_l1")
    dy, sq = _loss_head(y, loss_target[0], "loss_head")
    loss = lax.psum(0.5 * jnp.sum(sq) / D_MODEL, ("x", "y", "c"))

    dy, big1, small1, _ = _layer_bwd(dy, sv1, vecs[1], wts[1], "_l1")
    plans = dict(
        hgrn=lambda big: _ExchangePlan([_pack_grads(WIDE_REST, big), _pack_grads(NARROW, big)]),
        sb=lambda big: _ExchangePlan([_pack_grads(W_IN + WIDE_REST, big1), _pack_grads(NARROW, big1)]),
        dh=lambda big: _ExchangePlan([_pack_grads(W_IN, big)]))
    dx, _, small0, got = _layer_bwd(dy, sv0, vecs[0], wts[0], "_l0", plans)
    smalls = [small0, small1]
    shard = [{}, {}]
    shard[0].update(_unpack_shard_grads(W_IN, _sum8(got["dh"][0], "sum_grads_win0")))
    shard[0].update(_unpack_shard_grads(WIDE_REST, _sum8(got["hgrn"][0], "sum_grads_wide0")))
    shard[0].update(_unpack_shard_grads(NARROW, _sum8(got["hgrn"][1], "sum_grads_narrow0")))
    shard[1].update(_unpack_shard_grads(W_IN + WIDE_REST, _sum8(got["sb"][0], "sum_grads_wide1")))
    shard[1].update(_unpack_shard_grads(NARROW, _sum8(got["sb"][1], "sum_grads_narrow1")))
    grads = {name: jnp.stack([shard[l][name] for l in range(DEPTH)]) for name in shard[0]}

    small_parts = []
    for name, _ in SMALL:
        key = "mod" if name == "mod_b" else name
        if name == "hgrn_lb":
            small_parts.append(smalls[0][key] + smalls[1][key])
        else:
            small_parts.append(jnp.stack([smalls[l][key] for l in range(DEPTH)]))
    (g_small,) = _comm_alone(_GatherPlan([_pack_rows(small_parts, LANE)]), "gather_small_grads")
    small_sum = _sum8(g_small, "sum_small_grads").reshape(-1)
    off = 0
    for name, per_layer in SMALL:
        grads[name] = small_sum[off:off + DEPTH * per_layer].reshape(params[name].shape if name != "conv_w" else (DEPTH, CONV_WIDTH, CONV_CH))
        off += DEPTH * per_layer
    grads["conv_w"] = lax.dynamic_slice_in_dim(grads["conv_w"], me * cw_cols, cw_cols, axis=2)
    cols = mod_w.shape[2]
    dmod_all = g_small.reshape(N_DEV, -1)[:, :DEPTH * 6 * D_MODEL].reshape(N_DEV, DEPTH, 6 * D_MODEL)
    dmod_mine = lax.dynamic_slice_in_dim(dmod_all, me * cols, cols, axis=2).transpose(1, 0, 2)
    grads["mod_w"] = _modw_grad(c_all, dmod_mine, "mod_w_grad")

    delta, new_m, new_v = {}, {}, {}
    small_names = [n for n, _ in SMALL]
    for name in WEIGHTS:
        if name not in small_names:
            delta[name], new_m[name], new_v[name] = _adamw_nd(params[name], grads[name], mom1[name], mom2[name], f"adamw_{name}")
    packed = [_pack_rows([d[n] for n in small_names], LANE) for d in (params, grads, mom1, mom2)]
    outs = [o.reshape(-1) for o in _adamw(*packed, "adamw_small")]
    off = 0
    for name in small_names:
        size = params[name].size
        for dst, o in zip((delta, new_m, new_v), outs):
            dst[name] = o[off:off + size].reshape(params[name].shape)
        off += size
    return (loss, dx[None], *[grads[n] for n in WEIGHTS], *[delta[n] for n in WEIGHTS],
            *[new_m[n] for n in WEIGHTS], *[new_v[n] for n in WEIGHTS])
```

```python
import functools

import jax
import jax.numpy as jnp
import numpy as np
from jax import lax
from jax.experimental import pallas as pl
from jax.experimental.pallas import tpu as pltpu

F32 = jnp.float32
BF16 = jnp.bfloat16

D_MODEL = 1024
DEPTH = 2
N_DEV = 8
CONV_CH = 512
CONV_WIDTH = 31
CONV_HALO = 32
HG_HEADS = 4
HG_DK = 128
SB_HEADS = 8
SB_DH = 64
D_IN = 7680
D_FF = 4096
EPS = 1e-6
SB_BLK = 128
SB_DEAD = -104.0
SB_FIXED = 3
HG_CHUNK = 128

ADAM_LR = 0.001
ADAM_B1 = 0.9
ADAM_B2 = 0.999
ADAM_EPS = 1e-08
ADAM_WD = 0.01
ADAM_STEP = 10

VMEM_LIMIT = 48 * 1024 * 1024

NN = ((1,), (0,))
NT = ((1,), (1,))
TN = ((0,), (0,))
_DIMS = {"nn": NN, "nt": NT, "tn": TN}


def _sds(shape, dtype):
    return jax.ShapeDtypeStruct(shape, dtype)


def _params(*semantics):
    return pltpu.CompilerParams(dimension_semantics=semantics, vmem_limit_bytes=VMEM_LIMIT)


def _dot(a, b, dims):
    return lax.dot_general(a, b, (dims, ((), ())), preferred_element_type=F32)


@functools.partial(jax.custom_vjp, nondiff_argnums=(2,))
def _bdot(a, b, mode):
    return _dot(a.astype(BF16), b.astype(BF16), _DIMS[mode])


def _bdot_fwd(a, b, mode):
    return _bdot(a, b, mode), (a.astype(BF16), b.astype(BF16))


def _bdot_bwd(mode, res, g):
    a, b = res
    g = g.astype(BF16)
    if mode == "nn":
        return _dot(g, b, NT), _dot(a, g, TN)
    if mode == "nt":
        return _dot(g, b, NN), _dot(g, a, TN)
    return _dot(b, g, NT), _dot(a, g, NN)


_bdot.defvjp(_bdot_fwd, _bdot_bwd)


def _split(x):
    hi = x.astype(BF16)
    lo = (x - hi.astype(F32)).astype(BF16)
    return hi, lo


def _xdot_right(x, m, dims=NN):
    hi, lo = _split(x)
    return _dot(hi, m, dims) + _dot(lo, m, dims)


def _xdot_left(m, x, dims=NN):
    hi, lo = _split(x)
    return _dot(m, hi, dims) + _dot(m, lo, dims)


@jax.custom_vjp
def _xr(x, m):
    return _xdot_right(x, m)


def _xr_fwd(x, m):
    return _xdot_right(x, m), m


def _xr_bwd(m, g):
    return _xdot_right(g, m, NT), jnp.zeros_like(m)


_xr.defvjp(_xr_fwd, _xr_bwd)


def _norm_mod(x, g, sc, sh):
    r = lax.rsqrt(jnp.mean(x * x, axis=-1, keepdims=True) + EPS)
    return x * r * g * (1.0 + sc) + sh


MESH = pl.DeviceIdType.MESH
HBM_SPEC = pl.BlockSpec(memory_space=pltpu.HBM)


def _mesh_place():
    return lax.axis_index("x"), lax.axis_index("y"), lax.axis_index("c")


def _block_of(px, py, pc):
    return 4 * px + 2 * py + pc


def _sem_scratch(n):
    return [pltpu.SemaphoreType.DMA((n, N_DEV - 1)), pltpu.SemaphoreType.DMA((n, N_DEV - 1)), pltpu.SemaphoreType.DMA((n,))]


class _GatherPlan:
    def __init__(self, xs):
        self.xs = list(xs)
        self.n = len(self.xs)
        self.out_shape = [_sds((N_DEV, *v.shape), v.dtype) for v in self.xs]
        self.scratch = _sem_scratch(self.n)

    def _parts(self, x_refs, out_refs, sems):
        send_sems, recv_sems, local_sems = sems
        x, y, c = _mesh_place()
        me, sibling = (x, y, c), (x, y, 1 - c)
        chips = [(1 - x, y), (x, 1 - y), (1 - x, 1 - y)]

        def copy(a, k, block, to, src=None):
            rows = out_refs[a].at[_block_of(*block)]
            return pltpu.make_async_remote_copy(
                src_ref=rows if src is None else src, dst_ref=rows, send_sem=send_sems.at[a, k],
                recv_sem=recv_sems.at[a, k], device_id=to, device_id_type=MESH)

        local = [pltpu.make_async_copy(x_refs[a], out_refs[a].at[_block_of(*me)], local_sems.at[a])
                 for a in range(self.n)]
        first = []
        for a in range(self.n):
            first.append(copy(a, 0, me, sibling, src=x_refs[a]))
            first += [copy(a, 1 + j, me, (*chip, c), src=x_refs[a]) for j, chip in enumerate(chips)]
        return me, sibling, chips, c, copy, local, first

    def start(self, x_refs, out_refs, sems):
        *_, local, first = self._parts(x_refs, out_refs, sems)
        for cp in local + first:
            cp.start()

    def finish(self, x_refs, out_refs, sems):
        me, sibling, chips, c, copy, local, first = self._parts(x_refs, out_refs, sems)
        passed = []
        for j, chip in enumerate(chips):
            for a in range(self.n):
                copy(a, 1 + j, (*chip, c), me).wait_recv()
                fwd = copy(a, 4 + j, (*chip, c), sibling)
                fwd.start()
                passed.append(fwd)
        for a in range(self.n):
            copy(a, 0, sibling, me).wait_recv()
            for j, chip in enumerate(chips):
                copy(a, 4 + j, (*chip, 1 - c), me).wait_recv()
        for cp in first + passed:
            cp.wait_send()
        for cp in local:
            cp.wait()


class _ExchangePlan:
    def __init__(self, xs):
        self.xs = list(xs)
        self.n = len(self.xs)
        self.out_shape = [_sds(v.shape, v.dtype) for v in self.xs]
        self.scratch = _sem_scratch(self.n)

    def _parts(self, in_refs, out_refs, sems):
        send_sems, recv_sems, local_sems = sems
        x, y, c = _mesh_place()
        mine = _block_of(x, y, c)
        peers = [(1 - x if k & 4 else x, 1 - y if k & 2 else y, 1 - c if k & 1 else c) for k in range(1, N_DEV)]

        def copy(a, k, slot_src, slot_dst):
            return pltpu.make_async_remote_copy(
                src_ref=in_refs[a].at[slot_src], dst_ref=out_refs[a].at[slot_dst], send_sem=send_sems.at[a, k],
                recv_sem=recv_sems.at[a, k], device_id=peers[k], device_id_type=MESH)

        local = [pltpu.make_async_copy(in_refs[a].at[mine], out_refs[a].at[mine], local_sems.at[a])
                 for a in range(self.n)]
        sends = [copy(a, k, _block_of(*peers[k]), mine) for a in range(self.n) for k in range(N_DEV - 1)]
        arrivals = [copy(a, k, _block_of(*peers[k]), _block_of(*peers[k])) for a in range(self.n) for k in range(N_DEV - 1)]
        return local, sends, arrivals

    def start(self, in_refs, out_refs, sems):
        local, sends, _ = self._parts(in_refs, out_refs, sems)
        for cp in local + sends:
            cp.start()

    def finish(self, in_refs, out_refs, sems):
        local, sends, arrivals = self._parts(in_refs, out_refs, sems)
        for cp in arrivals:
            cp.wait_recv()
        for cp in sends:
            cp.wait_send()
        for cp in local:
            cp.wait()


def _call(body, args, *, grid, in_specs, out_specs, out_shape, scratch_shapes=(), semantics, name, comm=None):
    if comm is None:
        return pl.pallas_call(
            body, grid=grid, in_specs=list(in_specs), out_specs=list(out_specs), out_shape=list(out_shape),
            scratch_shapes=list(scratch_shapes), compiler_params=_params(*semantics), name=name)(*args)
    n_in, n_out, n_scr, n = len(in_specs), len(out_specs), len(scratch_shapes), comm.n

    def hosted(*refs):
        ins, rest = refs[:n_in], refs[n_in:]
        cin, rest = rest[:n], rest[n:]
        outs, rest = rest[:n_out], rest[n_out:]
        cout, rest = rest[:n], rest[n:]
        scr, sems = rest[:n_scr], rest[n_scr:]
        pids = [pl.program_id(d) for d in range(len(grid))]
        first = functools.reduce(jnp.logical_and, [p == 0 for p in pids])
        last = functools.reduce(jnp.logical_and, [p == g - 1 for p, g in zip(pids, grid)])

        @pl.when(first)
        def _():
            comm.start(cin, cout, sems)

        body(*ins, *outs, *scr)

        @pl.when(last)
        def _():
            comm.finish(cin, cout, sems)

    res = pl.pallas_call(
        hosted, grid=grid, in_specs=list(in_specs) + [HBM_SPEC] * n, out_specs=list(out_specs) + [HBM_SPEC] * n,
        out_shape=list(out_shape) + comm.out_shape, scratch_shapes=list(scratch_shapes) + comm.scratch,
        compiler_params=_params(*["arbitrary"] * len(grid)), name=name)(*args, *comm.xs)
    return res[:n_out], res[n_out:]


def _comm_alone(comm, name):
    def body(*refs):
        n = comm.n
        comm.start(refs[:n], refs[n:2 * n], refs[2 * n:])
        comm.finish(refs[:n], refs[n:2 * n], refs[2 * n:])

    return pl.pallas_call(
        body, in_specs=[HBM_SPEC] * comm.n, out_specs=[HBM_SPEC] * comm.n, out_shape=comm.out_shape,
        scratch_shapes=comm.scratch, name=name)(*comm.xs)


def _matmul(a, b, mode, out_dtype, tm, tn, tk, name, comm=None):
    if mode == "nn":
        (m, k), (_, n) = a.shape, b.shape
    elif mode == "nt":
        (m, k), (n, _) = a.shape, b.shape
    else:
        (k, m), (_, n) = a.shape, b.shape
    tm, tn, tk = min(tm, m), min(tn, n), min(tk, k)
    assert m % tm == 0 and n % tn == 0 and k % tk == 0, (name, m, n, k, tm, tn, tk)
    nk = k // tk
    dims = _DIMS[mode]

    def body(a_ref, b_ref, o_ref, acc_ref):
        if nk == 1:
            o_ref[...] = _dot(a_ref[...], b_ref[...], dims).astype(out_dtype)
            return
        kk = pl.program_id(2)

        @pl.when(kk == 0)
        def _():
            acc_ref[...] = _dot(a_ref[...], b_ref[...], dims)

        @pl.when((kk > 0) & (kk < nk - 1))
        def _():
            acc_ref[...] += _dot(a_ref[...], b_ref[...], dims)

        @pl.when(kk == nk - 1)
        def _():
            o_ref[...] = (acc_ref[...] + _dot(a_ref[...], b_ref[...], dims)).astype(out_dtype)

    if mode == "tn":
        a_spec = pl.BlockSpec((tk, tm), lambda i, j, kk: (kk, i))
        b_spec = pl.BlockSpec((tk, tn), lambda i, j, kk: (kk, j))
    elif mode == "nn":
        a_spec = pl.BlockSpec((tm, tk), lambda i, j, kk: (i, kk))
        b_spec = pl.BlockSpec((tk, tn), lambda i, j, kk: (kk, j))
    else:
        a_spec = pl.BlockSpec((tm, tk), lambda i, j, kk: (i, kk))
        b_spec = pl.BlockSpec((tn, tk), lambda i, j, kk: (j, kk))
    res = _call(
        body, (a, b), grid=(m // tm, n // tn, nk), in_specs=[a_spec, b_spec],
        out_specs=[pl.BlockSpec((tm, tn), lambda i, j, kk: (i, j))],
        out_shape=[_sds((m, n), out_dtype)], scratch_shapes=[pltpu.VMEM((tm, tn), F32)],
        semantics=("parallel", "parallel", "arbitrary"), name=name, comm=comm)
    return res[0] if comm is None else (res[0][0], res[1])


ROW_T = 512


def _prenorm(x, pv, name):
    s, d = x.shape
    t = min(ROW_T, s)

    def body(x_ref, pv_ref, h_ref):
        h = _norm_mod(x_ref[...], pv_ref[6:7, :], pv_ref[1:2, :], pv_ref[0:1, :])
        h_ref[...] = h.astype(BF16)

    return pl.pallas_call(
        body, grid=(s // t,),
        in_specs=[pl.BlockSpec((t, d), lambda i: (i, 0)), pl.BlockSpec((16, d), lambda i: (0, 0))],
        out_specs=pl.BlockSpec((t, d), lambda i: (i, 0)), out_shape=_sds((s, d), BF16),
        compiler_params=_params("parallel"), name=name)(x, pv)


def _prenorm_bwd(dh, dres, x, pv, name):
    s, d = x.shape
    t = min(ROW_T, s)

    def body(dh_ref, dres_ref, x_ref, pv_ref, dx_ref, sg_ref):
        i = pl.program_id(0)

        @pl.when(i == 0)
        def _():
            sg_ref[...] = jnp.zeros_like(sg_ref)

        _, vjp = jax.vjp(_norm_mod, x_ref[...], pv_ref[6:7, :], pv_ref[1:2, :], pv_ref[0:1, :])
        dx, dg, dsc, dsh = vjp(dh_ref[...])
        dx_ref[...] = dres_ref[...] + dx
        sg_ref[0:1, :] += dsh
        sg_ref[1:2, :] += dsc
        sg_ref[2:3, :] += dg

    row = pl.BlockSpec((t, d), lambda i: (i, 0))
    return pl.pallas_call(
        body, grid=(s // t,),
        in_specs=[row, row, row, pl.BlockSpec((16, d), lambda i: (0, 0))],
        out_specs=[row, pl.BlockSpec((8, d), lambda i: (0, 0))],
        out_shape=[_sds((s, d), F32), _sds((8, d), F32)],
        compiler_params=_params("arbitrary"), name=name)(dh, dres, x, pv)


CONV_T = 256


def _conv_tile(a_ext, g_ext, w, b, ln_g, ln_b, n_out):
    u0 = a_ext * jax.nn.sigmoid(g_ext)
    off = CONV_HALO - (CONV_WIDTH - 1)
    acc = jnp.zeros((n_out, u0.shape[1]), F32) + b
    for k in range(CONV_WIDTH):
        acc = acc + w[k:k + 1, :] * u0[off + k: off + k + n_out, :]
    mu = jnp.mean(acc, axis=-1, keepdims=True)
    var = jnp.mean(jnp.square(acc - mu), axis=-1, keepdims=True)
    y = (acc - mu) * lax.rsqrt(var + EPS) * ln_g + ln_b
    return y * jax.nn.sigmoid(y)


def _conv_fwd(proj, conv_w, cp, name):
    s = proj.shape[0]
    t = min(CONV_T, s)
    c, h = CONV_CH, CONV_HALO

    def body(ap_ref, ac_ref, gp_ref, gc_ref, w_ref, cp_ref, o_ref):
        i = pl.program_id(0)
        live = (i > 0).astype(F32)
        a_ext = jnp.concatenate([ap_ref[t - h:, :] * live, ac_ref[...]], axis=0)
        g_ext = jnp.concatenate([gp_ref[t - h:, :], gc_ref[...]], axis=0)
        u = _conv_tile(a_ext, g_ext, w_ref[...], cp_ref[0:1, :], cp_ref[1:2, :], cp_ref[2:3, :], t)
        o_ref[...] = u.astype(BF16)

    prev = lambda col: pl.BlockSpec((t, c), lambda i: (jnp.maximum(i - 1, 0), col))
    cur = lambda col: pl.BlockSpec((t, c), lambda i: (i, col))
    return pl.pallas_call(
        body, grid=(s // t,),
        in_specs=[prev(0), cur(0), prev(1), cur(1),
                  pl.BlockSpec((CONV_WIDTH, c), lambda i: (0, 0)), pl.BlockSpec((8, c), lambda i: (0, 0))],
        out_specs=pl.BlockSpec((t, c), lambda i: (i, 0)), out_shape=_sds((s, c), BF16),
        compiler_params=_params("parallel"), name=name)(proj, proj, proj, proj, conv_w, cp)


def _conv_bwd(proj, do, conv_w, cp, name):
    s = proj.shape[0]
    t = min(CONV_T, s)
    c, h = CONV_CH, CONV_HALO
    nt = s // t

    def body(ap_ref, ac_ref, an_ref, gp_ref, gc_ref, gn_ref, doc_ref, don_ref, w_ref, cp_ref,
             da_ref, dg_ref, dw_ref, sg_ref):
        i = pl.program_id(0)

        @pl.when(i == 0)
        def _():
            dw_ref[...] = jnp.zeros_like(dw_ref)
            sg_ref[...] = jnp.zeros_like(sg_ref)

        first = (i > 0).astype(F32)
        last = (i < nt - 1).astype(F32)
        a_ext = jnp.concatenate([ap_ref[t - h:, :] * first, ac_ref[...], an_ref[:h, :] * last], axis=0)
        g_ext = jnp.concatenate([gp_ref[t - h:, :], gc_ref[...], gn_ref[:h, :]], axis=0)
        fn = functools.partial(_conv_tile, n_out=t + h)
        _, vjp = jax.vjp(fn, a_ext, g_ext, w_ref[...], cp_ref[0:1, :], cp_ref[1:2, :], cp_ref[2:3, :])
        ct_own = jnp.concatenate([doc_ref[...], jnp.zeros((h, c), F32)], axis=0)
        ct_all = jnp.concatenate([doc_ref[...], don_ref[:h, :] * last], axis=0)
        _, _, dw, db, dlg, dlb = vjp(ct_own)
        da, dg, _, _, _, _ = vjp(ct_all)
        da_ref[...] = da[h:h + t, :].astype(BF16)
        dg_ref[...] = dg[h:h + t, :].astype(BF16)
        dw_ref[...] += dw
        sg_ref[0:1, :] += db
        sg_ref[1:2, :] += dlg
        sg_ref[2:3, :] += dlb

    prev = lambda col: pl.BlockSpec((t, c), lambda i: (jnp.maximum(i - 1, 0), col))
    cur = lambda col: pl.BlockSpec((t, c), lambda i: (i, col))
    nxt = lambda col: pl.BlockSpec((t, c), lambda i: (jnp.minimum(i + 1, nt - 1), col))
    return pl.pallas_call(
        body, grid=(nt,),
        in_specs=[prev(0), cur(0), nxt(0), prev(1), cur(1), nxt(1), cur(0), nxt(0),
                  pl.BlockSpec((CONV_WIDTH, c), lambda i: (0, 0)), pl.BlockSpec((8, c), lambda i: (0, 0))],
        out_specs=[cur(0), cur(0), pl.BlockSpec((CONV_WIDTH, c), lambda i: (0, 0)),
                   pl.BlockSpec((8, c), lambda i: (0, 0))],
        out_shape=[_sds((s, c), BF16), _sds((s, c), BF16), _sds((CONV_WIDTH, c), F32), _sds((8, c), F32)],
        compiler_params=_params("arbitrary"), name=name)(proj, proj, proj, proj, proj, proj, do, do, conv_w, cp)


def _hgrn_levels(c):
    out, m = [], c // 2
    while m >= 1:
        out.append(m)
        m //= 2
    return out


def _hgrn_consts(c):
    t = np.arange(c)[:, None]
    j = np.arange(c)[None, :]
    mats = [j <= t, j > t]
    for m in _hgrn_levels(c):
        same = (t // m) == (j // m)
        mats += [same & (j <= t), same & (j > t)]
    return jnp.asarray(np.concatenate(mats, axis=0).astype(np.float32), dtype=BF16)


@jax.custom_vjp
def _cums(lc, mall):
    c = lc.shape[0]
    full = _xdot_left(mall, lc)
    return tuple(full[i * c:(i + 1) * c, :] for i in range(mall.shape[0] // c))


def _cums_fwd(lc, mall):
    return _cums(lc, mall), mall


def _cums_bwd(mall, cts):
    return _xdot_left(mall, jnp.concatenate(cts, axis=0), TN), jnp.zeros_like(mall)


_cums.defvjp(_cums_fwd, _cums_bwd)


def _hgrn_chunk(q, f, v, g, lbs, ng, st_in, mall):
    c = q.shape[0]
    keep = jax.nn.sigmoid(-f)
    if lbs:
        keep = (1.0 - jax.nn.sigmoid(lbs[1] - lbs[0])) * keep
    lc = jnp.log1p(-keep)
    qs = q * jax.nn.sigmoid(q)
    cs = _cums(lc, mall)
    o = _bdot(qs * jnp.exp(cs[0]), st_in, "nt")
    total = jnp.sum(lc, axis=0, keepdims=True)
    st_out = st_in * jnp.exp(total) + _bdot(v, keep * jnp.exp(cs[1]), "tn")
    r = lax.broadcasted_iota(jnp.int32, q.shape, 0)
    tt = lax.broadcasted_iota(jnp.int32, (c, c), 0)
    ss = lax.broadcasted_iota(jnp.int32, (c, c), 1)
    sc = jnp.where(tt == ss, jnp.sum(qs * keep, axis=-1, keepdims=True), 0.0)
    for li, m in enumerate(_hgrn_levels(c)):
        lg = m.bit_length() - 1
        odd = ((r >> lg) & 1) == 1
        qm = jnp.where(odd, qs * jnp.exp(cs[2 + 2 * li]), 0.0)
        km = jnp.where(odd, 0.0, keep * jnp.exp(cs[3 + 2 * li]))
        pair = (((tt >> lg) & 1) == 1) & ((ss >> lg) == (tt >> lg) - 1)
        sc = sc + jnp.where(pair, _bdot(qm, km, "nt"), 0.0)
    o = o + _bdot(sc, v, "nn")
    on = o * lax.rsqrt(jnp.mean(o * o, axis=-1, keepdims=True) + EPS) * ng
    return on * (g * jax.nn.sigmoid(g)), st_out


def _hgrn_fwd(proj, lb, ng, name, comm=None):
    s = proj.shape[0]
    c = HG_CHUNK
    nc = s // c
    mall = _hgrn_consts(c)
    col0 = 1024 // (HG_HEADS * HG_DK)

    def body(*refs):
        q_ref, f_ref, v_ref, g_ref = refs[:4]
        if lb is None:
            ng_ref, m_ref, y_ref, st_ref, scr = refs[4:]
        else:
            lb_ref, ng_ref, m_ref, y_ref, st_ref, scr = refs[4:]
        ci = pl.program_id(0)

        @pl.when(ci == 0)
        def _():
            scr[...] = jnp.zeros_like(scr)

        mall_v = m_ref[...]
        for h in range(HG_HEADS):
            hs = slice(h * HG_DK, (h + 1) * HG_DK)
            lbs = () if lb is None else (lb_ref[0:1, hs], lb_ref[1:2, hs])
            st_in = scr[h]
            st_ref[h] = st_in
            y, st_out = _hgrn_chunk(q_ref[:, hs], f_ref[:, hs], v_ref[:, hs], g_ref[:, hs], lbs, ng_ref[...], st_in, mall_v)
            y_ref[:, hs] = y.astype(BF16)
            scr[h] = st_out

    w = HG_HEADS * HG_DK
    col = lambda k: pl.BlockSpec((c, w), lambda ci: (ci, col0 + k))
    in_specs = [col(0), col(1), col(2), col(3)]
    args = [proj, proj, proj, proj]
    if lb is not None:
        in_specs.append(pl.BlockSpec((2, w), lambda ci: (0, 0)))
        args.append(lb)
    in_specs += [pl.BlockSpec((1, HG_DK), lambda ci: (0, 0)), pl.BlockSpec(mall.shape, lambda ci: (0, 0))]
    args += [ng, mall]
    return _call(
        body, args, grid=(nc,), in_specs=in_specs,
        out_specs=[pl.BlockSpec((c, w), lambda ci: (ci, 0)),
                   pl.BlockSpec((HG_HEADS, None, HG_DK, HG_DK), lambda ci: (0, ci, 0, 0))],
        out_shape=[_sds((s, w), BF16), _sds((HG_HEADS, nc, HG_DK, HG_DK), F32)],
        scratch_shapes=[pltpu.VMEM((HG_HEADS, HG_DK, HG_DK), F32)],
        semantics=("arbitrary",), name=name, comm=comm)


def _hgrn_bwd(proj, states, dy, lb, ng, name, comm=None):
    s = proj.shape[0]
    c = HG_CHUNK
    nc = s // c
    mall = _hgrn_consts(c)
    col0 = 1024 // (HG_HEADS * HG_DK)

    def body(*refs):
        q_ref, f_ref, v_ref, g_ref, st_ref, dy_ref = refs[:6]
        if lb is None:
            ng_ref, m_ref, dq_ref, df_ref, dv_ref, dg_ref, dlb_ref, dng_ref, scr = refs[6:]
        else:
            lb_ref, ng_ref, m_ref, dq_ref, df_ref, dv_ref, dg_ref, dlb_ref, dng_ref, scr = refs[6:]
        ci = pl.program_id(0)

        @pl.when(ci == 0)
        def _():
            scr[...] = jnp.zeros_like(scr)
            dlb_ref[...] = jnp.zeros_like(dlb_ref)
            dng_ref[...] = jnp.zeros_like(dng_ref)

        mall_v = m_ref[...]
        fn = lambda q, f, v, g, lbs_, ng_, st: _hgrn_chunk(q, f, v, g, lbs_, ng_, st, mall_v)
        for h in range(HG_HEADS):
            hs = slice(h * HG_DK, (h + 1) * HG_DK)
            lbs = () if lb is None else (lb_ref[0:1, hs], lb_ref[1:2, hs])
            _, vjp = jax.vjp(fn, q_ref[:, hs], f_ref[:, hs], v_ref[:, hs], g_ref[:, hs], lbs, ng_ref[...], st_ref[h])
            dq, df, dv, dg, dlbs, dng, dst = vjp((dy_ref[:, hs], scr[h]))
            dq_ref[:, hs] = dq.astype(BF16)
            df_ref[:, hs] = df.astype(BF16)
            dv_ref[:, hs] = dv.astype(BF16)
            dg_ref[:, hs] = dg.astype(BF16)
            scr[h] = dst
            dng_ref[0:1, :] += dng
            if lbs:
                dlb_ref[0:1, hs] += dlbs[0]
                dlb_ref[1:2, hs] += dlbs[1]

    w = HG_HEADS * HG_DK
    rev = lambda ci: nc - 1 - ci
    col = lambda k: pl.BlockSpec((c, w), lambda ci: (rev(ci), col0 + k))
    out_col = pl.BlockSpec((c, w), lambda ci: (rev(ci), 0))
    in_specs = [col(0), col(1), col(2), col(3),
                pl.BlockSpec((HG_HEADS, None, HG_DK, HG_DK), lambda ci: (0, rev(ci), 0, 0)), out_col]
    args = [proj, proj, proj, proj, states, dy]
    if lb is not None:
        in_specs.append(pl.BlockSpec((2, w), lambda ci: (0, 0)))
        args.append(lb)
    in_specs += [pl.BlockSpec((1, HG_DK), lambda ci: (0, 0)), pl.BlockSpec(mall.shape, lambda ci: (0, 0))]
    args += [ng, mall]
    return _call(
        body, args, grid=(nc,), in_specs=in_specs,
        out_specs=[out_col, out_col, out_col, out_col,
                   pl.BlockSpec((2, w), lambda ci: (0, 0)), pl.BlockSpec((8, HG_DK), lambda ci: (0, 0))],
        out_shape=[_sds((s, w), BF16)] * 4 + [_sds((2, w), F32), _sds((8, HG_DK), F32)],
        scratch_shapes=[pltpu.VMEM((HG_HEADS, HG_DK, HG_DK), F32)],
        semantics=("arbitrary",), name=name, comm=comm)


def _head_avg():
    w = SB_HEADS * SB_DH
    i = np.arange(w)
    return jnp.asarray(((i[:, None] // SB_DH) == (i[None, :] // SB_DH)).astype(np.float32) / SB_DH, dtype=BF16)


def _sb_norm(x, g_tiled, avg):
    ms = _xr(x * x, avg)
    return x * lax.rsqrt(ms + EPS) * g_tiled


def _sb_prep(proj, gq, gk, name):
    s = proj.shape[0]
    t = min(ROW_T, s)
    w = SB_HEADS * SB_DH
    avg = _head_avg()

    def body(q_ref, k_ref, v_ref, gq_ref, gk_ref, avg_ref, qn_ref, kn_ref, vb_ref):
        qn_ref[...] = _sb_norm(q_ref[...], gq_ref[...], avg_ref[...]).astype(BF16)
        kn_ref[...] = _sb_norm(k_ref[...], gk_ref[...], avg_ref[...]).astype(BF16)
        vb_ref[...] = v_ref[...].astype(BF16)

    col = lambda k: pl.BlockSpec((t, w), lambda i: (i, 6 + k))
    vec = pl.BlockSpec((1, w), lambda i: (0, 0))
    out = pl.BlockSpec((t, w), lambda i: (i, 0))
    return pl.pallas_call(
        body, grid=(s // t,), in_specs=[col(0), col(1), col(2), vec, vec, pl.BlockSpec((w, w), lambda i: (0, 0))],
        out_specs=[out, out, out], out_shape=[_sds((s, w), BF16)] * 3,
        compiler_params=_params("parallel"), name=name)(proj, proj, proj, gq, gk, avg)


def _sb_prep_bwd(proj, dqn, dkn, gq, gk, name):
    s = proj.shape[0]
    t = min(ROW_T, s)
    w = SB_HEADS * SB_DH
    avg = _head_avg()

    def body(q_ref, k_ref, dqn_ref, dkn_ref, gq_ref, gk_ref, avg_ref, dq_ref, dk_ref, sg_ref):
        i = pl.program_id(0)

        @pl.when(i == 0)
        def _():
            sg_ref[...] = jnp.zeros_like(sg_ref)

        avg_v = avg_ref[...]
        fn = lambda x, g: _sb_norm(x, g, avg_v)
        _, vq = jax.vjp(fn, q_ref[...], gq_ref[...])
        dq, dgq = vq(dqn_ref[...])
        _, vk = jax.vjp(fn, k_ref[...], gk_ref[...])
        dk, dgk = vk(dkn_ref[...])
        dq_ref[...] = dq.astype(BF16)
        dk_ref[...] = dk.astype(BF16)
        sg_ref[0:1, :] += dgq
        sg_ref[1:2, :] += dgk

    col = lambda k: pl.BlockSpec((t, w), lambda i: (i, 6 + k))
    vec = pl.BlockSpec((1, w), lambda i: (0, 0))
    row = pl.BlockSpec((t, w), lambda i: (i, 0))
    return pl.pallas_call(
        body, grid=(s // t,),
        in_specs=[col(0), col(1), row, row, vec, vec, pl.BlockSpec((w, w), lambda i: (0, 0))],
        out_specs=[row, row, pl.BlockSpec((8, w), lambda i: (0, 0))],
        out_shape=[_sds((s, w), BF16), _sds((s, w), BF16), _sds((8, w), F32)],
        compiler_params=_params("arbitrary"), name=name)(proj, proj, dqn, dkn, gq, gk, avg)


def _sb_tri(kind):
    j = np.arange(SB_BLK)[:, None]
    s = np.arange(SB_BLK)[None, :]
    tri = (j > s) if kind == "suffix" else (j < s)
    return jnp.asarray(np.concatenate([tri, np.ones_like(tri)], axis=1).astype(np.float32), dtype=BF16)


def _sb_scores(qm, kblk, mask):
    z = _dot(qm, kblk, NT) * (SB_DH ** -0.5)
    sp = jnp.maximum(z, 0.0) + jnp.log(1.0 + jnp.exp(-jnp.abs(z)))
    return z, sp, jnp.where(mask, -sp, 0.0)


def _sb_setup(b):
    lane = lax.broadcasted_iota(jnp.int32, (2 * b, b), 1)
    row = lax.broadcasted_iota(jnp.int32, (2 * b, b), 0)
    mine = (row >> (b.bit_length() - 1)) == (lane >> (SB_DH.bit_length() - 1))
    return lane, row & (b - 1), mine


def _sb_fwd(qn, kn, vb, name, comm=None):
    s, w = qn.shape
    b = SB_BLK
    nq = s // b
    tri = _sb_tri("suffix")

    def body(q_ref, k_ref, v_ref, tri_ref, o_ref):
        i = pl.program_id(1)
        lane, tt, mine = _sb_setup(b)
        q = q_ref[...]
        q2 = jnp.concatenate([q, q], axis=0)
        qm = jnp.where(mine, q2, jnp.zeros_like(q2))
        tri_v = tri_ref[...]

        def block(kb, lim, run, acc):
            off = pl.multiple_of(kb * b, b)
            kblk = k_ref[pl.ds(off, b), :]
            vblk = v_ref[pl.ds(off, b), :]
            mask = lane < lim
            z, sp, lk = _sb_scores(qm, kblk, mask)
            both = _xdot_right(lk, tri_v)
            a = jnp.where(mask, jnp.exp(z - sp + both[:, :b] + run), 0.0)
            return run + both[:, b:], acc + _dot(a.astype(BF16), vblk, NN)

        offs = [pl.multiple_of(jnp.maximum(i - j, 0) * b, b) for j in range(SB_FIXED)]
        masks = [lane < (tt if j == 0 else jnp.where(i >= j, b, 0)) for j in range(SB_FIXED)]
        scores = [_sb_scores(qm, k_ref[pl.ds(off, b), :], m) for off, m in zip(offs, masks)]
        boths = [_xdot_right(lk, tri_v) for _, _, lk in scores]
        run = acc = jnp.zeros((2 * b, b), F32)
        for j in range(SB_FIXED):
            z, sp, _ = scores[j]
            a = jnp.where(masks[j], jnp.exp(z - sp + boths[j][:, :b] + run), 0.0)
            acc = acc + _dot(a.astype(BF16), v_ref[pl.ds(offs[j], b), :], NN)
            run = run + boths[j][:, b:]

        def cond(carry):
            j, run_, _ = carry
            return (j <= i) & (jnp.max(run_) > SB_DEAD)

        def step(carry):
            j, run_, acc_ = carry
            run_, acc_ = block(i - j, b, run_, acc_)
            return j + 1, run_, acc_

        _, _, acc = lax.while_loop(cond, step, (jnp.int32(SB_FIXED), run, acc))
        o_ref[...] = jnp.where(lane[:b] < SB_DH, acc[:b], acc[b:]).astype(BF16)

    blk = pl.BlockSpec((b, b), lambda p, i: (i, p))
    full = pl.BlockSpec((s, b), lambda p, i: (0, p))
    return _call(
        body, (qn, kn, vb, tri), grid=(w // b, nq),
        in_specs=[blk, full, full, pl.BlockSpec(tri.shape, lambda p, i: (0, 0))],
        out_specs=[blk], out_shape=[_sds((s, w), BF16)],
        semantics=("parallel", "arbitrary"), name=name, comm=comm)


def _sb_bwd(qn, kn, vb, do, name, comm=None):
    s, w = qn.shape
    b = SB_BLK
    nq = s // b
    tri_s = _sb_tri("suffix")
    tri_p = _sb_tri("prefix")
    scale = SB_DH ** -0.5

    def body(q_ref, k_ref, v_ref, do_ref, ts_ref, tp_ref, dq_ref, dk_ref, dv_ref, dk_acc, dv_acc, dp_scr):
        i = pl.program_id(1)

        @pl.when(i == 0)
        def _():
            dk_acc[...] = jnp.zeros_like(dk_acc)
            dv_acc[...] = jnp.zeros_like(dv_acc)

        lane, tt, mine = _sb_setup(b)
        q = q_ref[...]
        q2 = jnp.concatenate([q, q], axis=0)
        qm = jnp.where(mine, q2, jnp.zeros_like(q2))
        dout = do_ref[...].astype(BF16)
        d2 = jnp.concatenate([dout, dout], axis=0)
        dom = jnp.where(mine, d2, jnp.zeros_like(d2))
        ts_v = ts_ref[...]
        tp_v = tp_ref[...]
        zero = jnp.zeros((2 * b, b), F32)

        def down(kb, lim, run):
            off = pl.multiple_of(kb * b, b)
            kblk = k_ref[pl.ds(off, b), :]
            vblk = v_ref[pl.ds(off, b), :]
            mask = lane < lim
            z, sp, lk = _sb_scores(qm, kblk, mask)
            both = _xdot_right(lk, ts_v)
            a = jnp.where(mask, jnp.exp(z - sp + both[:, :b] + run), 0.0)
            dv_acc[pl.ds(off, b), :] += _dot(a.astype(BF16), dom, TN)
            return _dot(dom, vblk, NT) * a, run + both[:, b:]

        def up(kb, lim, dp, pre, dq):
            off = pl.multiple_of(kb * b, b)
            kblk = k_ref[pl.ds(off, b), :]
            sig = jax.nn.sigmoid(_dot(qm, kblk, NT) * scale)
            both = _xdot_right(dp, tp_v)
            dz = jnp.where(lane < lim, dp * (1.0 - sig) - sig * (both[:, :b] + pre), 0.0) * scale
            dz = dz.astype(BF16)
            dk_acc[pl.ds(off, b), :] += _dot(dz, qm, TN)
            return pre + both[:, b:], dq + _dot(dz, kblk, NN)

        offs = [pl.multiple_of(jnp.maximum(i - j, 0) * b, b) for j in range(SB_FIXED)]
        masks = [lane < (tt if j == 0 else jnp.where(i >= j, b, 0)) for j in range(SB_FIXED)]
        kblks = [k_ref[pl.ds(off, b), :] for off in offs]
        scores = [_sb_scores(qm, kblk, m) for kblk, m in zip(kblks, masks)]
        das = [_dot(dom, v_ref[pl.ds(off, b), :], NT) for off in offs]
        boths = [_xdot_right(lk, ts_v) for _, _, lk in scores]
        run = zero
        dps = []
        for j in range(SB_FIXED):
            z, sp, _ = scores[j]
            a = jnp.where(masks[j], jnp.exp(z - sp + boths[j][:, :b] + run), 0.0)
            dps.append(das[j] * a)
            dv_acc[pl.ds(offs[j], b), :] += _dot(a.astype(BF16), dom, TN)
            run = run + boths[j][:, b:]

        def cond(carry):
            j, run_ = carry
            return (j <= i) & (jnp.max(run_) > SB_DEAD)

        def sweep_down(carry):
            j, run_ = carry
            dp, run_ = down(i - j, b, run_)
            dp_scr[i - j] = dp
            return j + 1, run_

        n_live, _ = lax.while_loop(cond, sweep_down, (jnp.int32(SB_FIXED), run))

        def sweep_up(jj, carry):
            kb = i - n_live + 1 + jj
            return up(kb, b, dp_scr[kb], *carry)

        pre, dq = lax.fori_loop(0, n_live - SB_FIXED, sweep_up, (zero, zero))
        pres = [_xdot_right(dp, tp_v) for dp in dps]
        for j in reversed(range(SB_FIXED)):
            z, sp, _ = scores[j]
            sig = jnp.exp(z - sp)
            dz = jnp.where(masks[j], dps[j] * (1.0 - sig) - sig * (pres[j][:, :b] + pre), 0.0) * scale
            dz = dz.astype(BF16)
            dk_acc[pl.ds(offs[j], b), :] += _dot(dz, qm, TN)
            dq = dq + _dot(dz, kblks[j], NN)
            pre = pre + pres[j][:, b:]
        dq_ref[...] = jnp.where(lane[:b] < SB_DH, dq[:b], dq[b:])

        @pl.when(i == nq - 1)
        def _():
            dk_ref[...] = dk_acc[...]
            dv_ref[...] = dv_acc[...].astype(BF16)

    blk = pl.BlockSpec((b, b), lambda p, i: (i, p))
    full = pl.BlockSpec((s, b), lambda p, i: (0, p))
    tri = pl.BlockSpec(tri_s.shape, lambda p, i: (0, 0))
    return _call(
        body, (qn, kn, vb, do, tri_s, tri_p), grid=(w // b, nq), in_specs=[blk, full, full, blk, tri, tri],
        out_specs=[blk, full, full], out_shape=[_sds((s, w), F32), _sds((s, w), F32), _sds((s, w), BF16)],
        scratch_shapes=[pltpu.VMEM((s, b), F32), pltpu.VMEM((s, b), F32), pltpu.VMEM((nq, 2 * b, b), F32)],
        semantics=("arbitrary", "arbitrary"), name=name, comm=comm)


MIX_T = 256
HALF = 512


def _gate_slices(ga, gb):
    return [(ga[:, 0:512], ga[:, 512:1024]), (ga[:, 1024:1536], gb[:, 0:512]), (gb[:, 512:1024], gb[:, 1024:1536])]


def _mix_fwd(u3, oh, osb, proj, x, pv, wc, wh, ws, wo, name):
    s, d = x.shape
    t = min(MIX_T, s)

    def body(u3_ref, oh_ref, os_ref, ga_ref, gb_ref, x_ref, pv_ref, wc_ref, wh_ref, ws_ref, wo_ref,
             x1_ref, h2_ref, mg_ref, mo_ref):
        ys = [_dot(u3_ref[...], wc_ref[...], NT), _dot(oh_ref[...], wh_ref[...], NT), _dot(os_ref[...], ws_ref[...], NT)]
        gl = _gate_slices(ga_ref[...], gb_ref[...])
        halves = []
        for hf in range(2):
            lo = hf * HALF
            acc = jnp.zeros((t, HALF), F32)
            for br in range(3):
                gate = jax.nn.sigmoid(gl[br][hf] + pv_ref[8 + br:9 + br, lo:lo + HALF])
                acc = acc + gate * ys[br][:, lo:lo + HALF]
            halves.append(acc)
        merged = jnp.concatenate(halves, axis=1).astype(BF16)
        mg_ref[...] = merged
        mo = _dot(merged, wo_ref[...], NN)
        mo_ref[...] = mo.astype(BF16)
        x1 = x_ref[...] + pv_ref[2:3, :] * mo
        x1_ref[...] = x1
        h2_ref[...] = _norm_mod(x1, pv_ref[7:8, :], pv_ref[4:5, :], pv_ref[3:4, :]).astype(BF16)

    br_spec = pl.BlockSpec((t, CONV_CH), lambda i: (i, 0))
    row = pl.BlockSpec((t, d), lambda i: (i, 0))
    wproj = pl.BlockSpec((d, CONV_CH), lambda i: (0, 0))
    return pl.pallas_call(
        body, grid=(s // t,),
        in_specs=[br_spec, br_spec, br_spec, pl.BlockSpec((t, 1536), lambda i: (i, 3)),
                  pl.BlockSpec((t, 1536), lambda i: (i, 4)), row, pl.BlockSpec((16, d), lambda i: (0, 0)),
                  wproj, wproj, wproj, pl.BlockSpec((d, d), lambda i: (0, 0))],
        out_specs=[row, row, row, row],
        out_shape=[_sds((s, d), F32), _sds((s, d), BF16), _sds((s, d), BF16), _sds((s, d), BF16)],
        compiler_params=_params("parallel"), name=name)(u3, oh, osb, proj, proj, x, pv, wc, wh, ws, wo)


def _mix_bwd(dx1, mo1, u3, oh, osb, proj, pv, wc, wh, ws, wo, name):
    s, d = dx1.shape
    t = min(MIX_T, s)

    def body(dx_ref, mo_ref, u3_ref, oh_ref, os_ref, ga_ref, gb_ref, pv_ref, wc_ref, wh_ref, ws_ref, wo_ref,
             dmo_ref, dyc_ref, dyh_ref, dys_ref, doc_ref, doh_ref, dos_ref, dgl_ref, sg_ref):
        i = pl.program_id(0)

        @pl.when(i == 0)
        def _():
            sg_ref[...] = jnp.zeros_like(sg_ref)

        dx = dx_ref[...]
        dmo = (dx * pv_ref[2:3, :]).astype(BF16)
        dmo_ref[...] = dmo
        sg_ref[0:1, :] += jnp.sum(dx * mo_ref[...].astype(F32), axis=0, keepdims=True)
        dmerged = _dot(dmo, wo_ref[...], NT)
        branches = [(u3_ref, wc_ref, dyc_ref, doc_ref), (oh_ref, wh_ref, dyh_ref, doh_ref), (os_ref, ws_ref, dys_ref, dos_ref)]
        gl = _gate_slices(ga_ref[...], gb_ref[...])
        for br, (o_ref, w_ref, dy_ref, do_ref) in enumerate(branches):
            y = _dot(o_ref[...], w_ref[...], NT)
            dys = []
            for hf in range(2):
                lo = hf * HALF
                gate = jax.nn.sigmoid(gl[br][hf] + pv_ref[8 + br:9 + br, lo:lo + HALF])
                dm = dmerged[:, lo:lo + HALF]
                dys.append(dm * gate)
                dgl = dm * y[:, lo:lo + HALF] * gate * (1.0 - gate)
                dgl_ref[:, br * d + lo: br * d + lo + HALF] = dgl.astype(BF16)
                sg_ref[1 + br:2 + br, lo:lo + HALF] += jnp.sum(dgl, axis=0, keepdims=True)
            dy = jnp.concatenate(dys, axis=1).astype(BF16)
            dy_ref[...] = dy
            do_ref[...] = _dot(dy, w_ref[...], NN)

    br_spec = pl.BlockSpec((t, CONV_CH), lambda i: (i, 0))
    row = pl.BlockSpec((t, d), lambda i: (i, 0))
    wproj = pl.BlockSpec((d, CONV_CH), lambda i: (0, 0))
    return pl.pallas_call(
        body, grid=(s // t,),
        in_specs=[row, row, br_spec, br_spec, br_spec, pl.BlockSpec((t, 1536), lambda i: (i, 3)),
                  pl.BlockSpec((t, 1536), lambda i: (i, 4)), pl.BlockSpec((16, d), lambda i: (0, 0)),
                  wproj, wproj, wproj, pl.BlockSpec((d, d), lambda i: (0, 0))],
        out_specs=[row, row, row, row, br_spec, br_spec, br_spec, pl.BlockSpec((t, 3 * d), lambda i: (i, 0)),
                   pl.BlockSpec((8, d), lambda i: (0, 0))],
        out_shape=[_sds((s, d), BF16)] * 4 + [_sds((s, CONV_CH), F32)] * 3 + [_sds((s, 3 * d), BF16), _sds((8, d), F32)],
        compiler_params=_params("arbitrary"), name=name)(dx1, mo1, u3, oh, osb, proj, proj, pv, wc, wh, ws, wo)


MLP_T = 512
MLP_F = 512


def _mlp_fwd(h2, x1, pv, w1t, w2, name):
    s, d = x1.shape
    t = min(MLP_T, s)
    nf = D_FF // MLP_F

    def body(h_ref, x_ref, pv_ref, w1_ref, w2_ref, x2_ref, mo_ref, acc_ref):
        f = pl.program_id(1)

        @pl.when(f == 0)
        def _():
            acc_ref[...] = jnp.zeros_like(acc_ref)

        a = jnp.maximum(_dot(h_ref[...], w1_ref[...], NT), 0.0)
        acc_ref[...] += _dot((a * a).astype(BF16), w2_ref[...], NN)

        @pl.when(f == nf - 1)
        def _():
            mo = acc_ref[...]
            mo_ref[...] = mo.astype(BF16)
            x2_ref[...] = x_ref[...] + pv_ref[5:6, :] * mo

    row = pl.BlockSpec((t, d), lambda i, f: (i, 0))
    wblk = pl.BlockSpec((MLP_F, d), lambda i, f: (f, 0))
    return pl.pallas_call(
        body, grid=(s // t, nf), in_specs=[row, row, pl.BlockSpec((16, d), lambda i, f: (0, 0)), wblk, wblk],
        out_specs=[row, row], out_shape=[_sds((s, d), F32), _sds((s, d), BF16)],
        scratch_shapes=[pltpu.VMEM((t, d), F32)],
        compiler_params=_params("parallel", "arbitrary"), name=name)(h2, x1, pv, w1t, w2)


def _mlp_bwd(dx2, h2, x1, mo2, pv, w1t, w2, name):
    s, d = x1.shape
    t = min(MLP_T, s)
    nf = D_FF // MLP_F

    def body(dx_ref, h_ref, x_ref, mo_ref, pv_ref, w1_ref, w2_ref, dx1_ref, da_ref, b_ref, dmo_ref, sg_ref, acc_ref):
        i = pl.program_id(0)
        f = pl.program_id(1)

        @pl.when((i == 0) & (f == 0))
        def _():
            sg_ref[...] = jnp.zeros_like(sg_ref)

        @pl.when(f == 0)
        def _():
            acc_ref[...] = jnp.zeros_like(acc_ref)
            dx = dx_ref[...]
            dmo_ref[...] = (dx * pv_ref[5:6, :]).astype(BF16)
            sg_ref[0:1, :] += jnp.sum(dx * mo_ref[...].astype(F32), axis=0, keepdims=True)

        r = jnp.maximum(_dot(h_ref[...], w1_ref[...], NT), 0.0)
        b_ref[...] = (r * r).astype(BF16)
        da = (_dot(dmo_ref[...], w2_ref[...], NT) * (2.0 * r)).astype(BF16)
        da_ref[...] = da
        acc_ref[...] += _dot(da, w1_ref[...], NN)

        @pl.when(f == nf - 1)
        def _():
            _, vjp = jax.vjp(_norm_mod, x_ref[...], pv_ref[7:8, :], pv_ref[4:5, :], pv_ref[3:4, :])
            dxn, dg, dsc, dsh = vjp(acc_ref[...])
            dx1_ref[...] = dx_ref[...] + dxn
            sg_ref[1:2, :] += dsh
            sg_ref[2:3, :] += dsc
            sg_ref[3:4, :] += dg

    row = pl.BlockSpec((t, d), lambda i, f: (i, 0))
    wblk = pl.BlockSpec((MLP_F, d), lambda i, f: (f, 0))
    hid = pl.BlockSpec((t, MLP_F), lambda i, f: (i, f))
    return pl.pallas_call(
        body, grid=(s // t, nf),
        in_specs=[row, row, row, row, pl.BlockSpec((16, d), lambda i, f: (0, 0)), wblk, wblk],
        out_specs=[row, hid, hid, row, pl.BlockSpec((8, d), lambda i, f: (0, 0))],
        out_shape=[_sds((s, d), F32), _sds((s, D_FF), BF16), _sds((s, D_FF), BF16), _sds((s, d), BF16), _sds((8, d), F32)],
        scratch_shapes=[pltpu.VMEM((t, d), F32)],
        compiler_params=_params("arbitrary", "arbitrary"), name=name)(dx2, h2, x1, mo2, pv, w1t, w2)


def _loss_head(y, target, name):
    s, d = y.shape
    t = min(ROW_T, s)

    def body(y_ref, t_ref, dy_ref, ls_ref):
        i = pl.program_id(0)

        @pl.when(i == 0)
        def _():
            ls_ref[...] = jnp.zeros_like(ls_ref)

        e = y_ref[...] - t_ref[...]
        dy_ref[...] = e * (1.0 / d)
        ls_ref[...] += jnp.sum((e * e).reshape(t // 8, 8, d), axis=0)

    row = pl.BlockSpec((t, d), lambda i: (i, 0))
    return pl.pallas_call(
        body, grid=(s // t,), in_specs=[row, row], out_specs=[row, pl.BlockSpec((8, d), lambda i: (0, 0))],
        out_shape=[_sds((s, d), F32), _sds((8, d), F32)],
        compiler_params=_params("arbitrary"), name=name)(y, target)


def _layer_vectors(l, mod, sm):
    d = D_MODEL
    pv = jnp.concatenate([mod[l].reshape(6, d), sm["norm1_g"][l][None], sm["norm2_g"][l][None],
                          sm["gate_b"][l].reshape(3, d), jnp.zeros((5, d), F32)], axis=0)
    cp = jnp.concatenate([sm["conv_b"][l][None], sm["conv_ln_g"][l][None], sm["conv_ln_b"][l][None],
                          jnp.zeros((5, CONV_CH), F32)], axis=0)
    return dict(pv=pv, cp=cp, conv_w=sm["conv_w"][l], lb=(sm["hgrn_lb"] if l > 0 else None),
                ng=sm["hgrn_norm_g"][l][None], gq=jnp.tile(sm["sb_qn_g"][l], SB_HEADS)[None],
                gk=jnp.tile(sm["sb_kn_g"][l], SB_HEADS)[None])


def _hosted(res, comm):
    return res if comm is not None else (res, None)


def _layer_fwd_mixers(x, vec, win_t, tag, comm_hgrn=None, comm_sb=None):
    h = _prenorm(x, vec["pv"], f"prenorm{tag}")
    proj = _matmul(h, win_t, "nt", F32, 1024, 768, 1024, f"proj{tag}")
    u3 = _conv_fwd(proj, vec["conv_w"], vec["cp"], f"conv_fwd{tag}")
    (oh, states), got_hgrn = _hosted(_hgrn_fwd(proj, vec["lb"], vec["ng"], f"hgrn_fwd{tag}", comm_hgrn), comm_hgrn)
    qn, kn, vb = _sb_prep(proj, vec["gq"], vec["gk"], f"sb_prep{tag}")
    (osb,), got_sb = _hosted(_sb_fwd(qn, kn, vb, f"sb_fwd{tag}", comm_sb), comm_sb)
    saved = dict(x=x, h=h, proj=proj, u3=u3, oh=oh, states=states, qn=qn, kn=kn, vb=vb, osb=osb)
    return saved, got_hgrn, got_sb


def _layer_fwd_out(sv, vec, w, tag):
    x1, h2, merged, mo1 = _mix_fwd(sv["u3"], sv["oh"], sv["osb"], sv["proj"], sv["x"], vec["pv"],
                                   w["wc_t"], w["wh_t"], w["ws_t"], w["wo"], f"mix_fwd{tag}")
    x2, mo2 = _mlp_fwd(h2, x1, vec["pv"], w["w1_t"], w["w2"], f"mlp_fwd{tag}")
    sv.update(x1=x1, h2=h2, merged=merged, mo1=mo1, mo2=mo2)
    return x2


def _layer_bwd(dx2, sv, vec, w, tag, plans=None):
    plans = plans or {}
    got = {}

    def plan_for(key, big_now):
        return plans[key](big_now) if key in plans else None

    pv = vec["pv"]
    dx1, da, bsq, dmo2, sg_mlp = _mlp_bwd(dx2, sv["h2"], sv["x1"], sv["mo2"], pv, w["w1_t"], w["w2"], f"mlp_bwd{tag}")
    big = {}
    big["w1_t"] = _matmul(da, sv["h2"], "tn", BF16, 1024, 1024, 1024, f"dw1{tag}")
    big["w2"] = _matmul(bsq, dmo2, "tn", BF16, 1024, 1024, 1024, f"dw2{tag}")
    dmo1, dyc, dyh, dys, doc, doh, dos, dgl, sg_mix = _mix_bwd(
        dx1, sv["mo1"], sv["u3"], sv["oh"], sv["osb"], sv["proj"], pv, w["wc_t"], w["wh_t"], w["ws_t"], w["wo"], f"mix_bwd{tag}")
    big["wo"] = _matmul(sv["merged"], dmo1, "tn", BF16, 1024, 1024, 1024, f"dwo{tag}")
    big["wc_t"] = _matmul(dyc, sv["u3"], "tn", BF16, 1024, 512, 1024, f"dwc{tag}")
    big["wh_t"] = _matmul(dyh, sv["oh"], "tn", BF16, 1024, 512, 1024, f"dwh{tag}")
    big["ws_t"] = _matmul(dys, sv["osb"], "tn", BF16, 1024, 512, 1024, f"dws{tag}")
    da_c, dg_c, dconv_w, sg_conv = _conv_bwd(sv["proj"], doc, vec["conv_w"], vec["cp"], f"conv_bwd{tag}")
    comm = plan_for("hgrn", big)
    (dq_h, df_h, di_h, dg_h, dlb, dng), got["hgrn"] = _hosted(
        _hgrn_bwd(sv["proj"], sv["states"], doh, vec["lb"], vec["ng"], f"hgrn_bwd{tag}", comm), comm)
    comm = plan_for("sb", big)
    (dqn, dkn, dv_s), got["sb"] = _hosted(_sb_bwd(sv["qn"], sv["kn"], sv["vb"], dos, f"sb_bwd{tag}", comm), comm)
    dq_s, dk_s, sg_sb = _sb_prep_bwd(sv["proj"], dqn, dkn, vec["gq"], vec["gk"], f"sb_prep_bwd{tag}")
    dproj = jnp.concatenate([da_c, dg_c, dq_h, df_h, di_h, dg_h, dq_s, dk_s, dv_s, dgl], axis=1)
    big["win_t"] = _matmul(dproj, sv["h"], "tn", BF16, 768, 1024, 1024, f"dwin{tag}")
    comm = plan_for("dh", big)
    dh, got["dh"] = _hosted(_matmul(dproj, w["win_t"], "nn", F32, 512, 1024, 1920, f"dh{tag}", comm), comm)
    dx, sg_pre = _prenorm_bwd(dh, dx1, sv["x"], pv, f"prenorm_bwd{tag}")
    small = dict(
        mod=jnp.stack([sg_pre[0], sg_pre[1], sg_mix[0], sg_mlp[1], sg_mlp[2], sg_mlp[0]]).reshape(6 * D_MODEL),
        norm1_g=sg_pre[2], norm2_g=sg_mlp[3], gate_b=sg_mix[1:4].reshape(3 * D_MODEL),
        conv_w=dconv_w, conv_b=sg_conv[0], conv_ln_g=sg_conv[1], conv_ln_b=sg_conv[2],
        hgrn_lb=dlb, hgrn_norm_g=dng[0],
        sb_qn_g=sg_sb[0].reshape(SB_HEADS, SB_DH).sum(0), sb_kn_g=sg_sb[1].reshape(SB_HEADS, SB_DH).sum(0))
    return dx, big, small, got


def _row_tile(r, cap=512):
    t = min(r, cap)
    while r % t or (t % 8 and t != r):
        t -= 1
    return t


def _sum8(z, name):
    _, r, c = z.shape
    t = _row_tile(r, 128 if c >= 1024 else 512)

    def body(z_ref, o_ref):
        acc = z_ref[0].astype(F32)
        for j in range(1, N_DEV):
            acc = acc + z_ref[j].astype(F32)
        o_ref[...] = acc

    return pl.pallas_call(
        body, grid=(r // t,), in_specs=[pl.BlockSpec((N_DEV, t, c), lambda i: (0, i, 0))],
        out_specs=pl.BlockSpec((t, c), lambda i: (i, 0)), out_shape=_sds((r, c), F32),
        compiler_params=_params("parallel"), name=name)(z)


def _adamw(w, g, m, v, name):
    r, c = w.shape
    t = _row_tile(r, 256)

    def body(w_ref, g_ref, m_ref, v_ref, d_ref, nm_ref, nv_ref):
        g_ = g_ref[...]
        nm = ADAM_B1 * m_ref[...] + (1.0 - ADAM_B1) * g_
        nv = ADAM_B2 * v_ref[...] + (1.0 - ADAM_B2) * jnp.square(g_)
        m_hat = nm / (1.0 - ADAM_B1 ** ADAM_STEP)
        v_hat = nv / (1.0 - ADAM_B2 ** ADAM_STEP)
        d_ref[...] = -ADAM_LR * (m_hat / (jnp.sqrt(v_hat) + ADAM_EPS) + ADAM_WD * w_ref[...])
        nm_ref[...] = nm
        nv_ref[...] = nv

    blk = pl.BlockSpec((t, c), lambda i: (i, 0))
    return pl.pallas_call(
        body, grid=(r // t,), in_specs=[blk] * 4, out_specs=[blk] * 3, out_shape=[_sds((r, c), F32)] * 3,
        compiler_params=_params("parallel"), name=name)(w, g, m, v)


def _mod_local(c_all, mod_w, name):
    depth, d, cols = mod_w.shape

    def body(c_ref, w_ref, o_ref):
        cv = c_ref[...]
        act = cv * jax.nn.sigmoid(cv)
        o_ref[...] = jnp.dot(act, w_ref[...], precision=lax.Precision.HIGHEST, preferred_element_type=F32)

    return pl.pallas_call(
        body, grid=(depth,),
        in_specs=[pl.BlockSpec((N_DEV, d), lambda l: (0, 0)), pl.BlockSpec((None, d, cols), lambda l: (l, 0, 0))],
        out_specs=pl.BlockSpec((None, N_DEV, cols), lambda l: (l, 0, 0)), out_shape=_sds((depth, N_DEV, cols), F32),
        compiler_params=_params("parallel"), name=name)(c_all, mod_w)


def _modw_grad(c_all, dmod, name):
    depth, _, cols = dmod.shape
    d = c_all.shape[1]

    def body(c_ref, g_ref, o_ref):
        cv = c_ref[...]
        act = cv * jax.nn.sigmoid(cv)
        o_ref[...] = lax.dot_general(act, g_ref[...], (TN, ((), ())), precision=lax.Precision.HIGHEST,
                                     preferred_element_type=F32)

    return pl.pallas_call(
        body, grid=(depth,),
        in_specs=[pl.BlockSpec((N_DEV, d), lambda l: (0, 0)), pl.BlockSpec((None, N_DEV, cols), lambda l: (l, 0, 0))],
        out_specs=pl.BlockSpec((None, d, cols), lambda l: (l, 0, 0)), out_shape=_sds((depth, d, cols), F32),
        compiler_params=_params("parallel"), name=name)(c_all, dmod)


LANE = 128
W_IN = (("w_in", 960, True),)
WIDE_REST = (("w_out", 128, False), ("mlp_w2", 512, False), ("mlp_w1", 512, True))
NARROW = (("w_conv_proj", 128, True), ("w_hgrn_proj", 128, True), ("w_sb_proj", 128, True))
BIG_KEY = {"w_in": "win_t", "w_out": "wo", "mlp_w2": "w2", "mlp_w1": "w1_t",
           "w_conv_proj": "wc_t", "w_hgrn_proj": "wh_t", "w_sb_proj": "ws_t"}
SMALL = (("mod_b", 6144), ("norm1_g", 1024), ("gate_b", 3072), ("conv_w", CONV_WIDTH * CONV_CH), ("conv_b", 512),
         ("conv_ln_g", 512), ("conv_ln_b", 512), ("hgrn_lb", 512), ("hgrn_norm_g", 128), ("sb_qn_g", 64),
         ("sb_kn_g", 64), ("norm2_g", 1024))


def _pack_rows(parts, width):
    flat = jnp.concatenate([p.reshape(-1) for p in parts])
    rows = -(-flat.shape[0] // width)
    rows = -(-rows // 8) * 8
    return jnp.pad(flat, (0, rows * width - flat.shape[0])).reshape(rows, width)


def _pack_weights(spec, params, l):
    parts = []
    for name, _, transposed in spec:
        w = params[name][l]
        parts.append((w.T if transposed else w).astype(BF16))
    return jnp.concatenate(parts, axis=0)


def _unpack_gathered(spec, g):
    out = {}
    off = 0
    for name, rows, _ in spec:
        out[BIG_KEY[name]] = g[:, off:off + rows].reshape(N_DEV * rows, g.shape[2])
        off += rows
    return out


def _pack_grads(spec, big):
    parts = []
    for name, rows, _ in spec:
        gmat = big[BIG_KEY[name]]
        parts.append(gmat.reshape(N_DEV, rows, gmat.shape[1]))
    return jnp.concatenate(parts, axis=1)


def _unpack_shard_grads(spec, gsum):
    out = {}
    off = 0
    for name, rows, transposed in spec:
        blk = gsum[off:off + rows]
        out[name] = blk.T if transposed else blk
        off += rows
    return out


def _adamw_nd(w, g, m, v, name):
    shape = w.shape
    two = lambda a: a.reshape(-1, shape[-1])
    return [o.reshape(shape) for o in _adamw(two(w), two(g), two(m), two(v), name)]


WEIGHTS = ("mod_w", "mod_b", "norm1_g", "w_in", "gate_b", "conv_w", "conv_b", "conv_ln_g", "conv_ln_b", "w_conv_proj",
           "hgrn_lb", "hgrn_norm_g", "w_hgrn_proj", "sb_qn_g", "sb_kn_g", "w_sb_proj", "w_out", "norm2_g", "mlp_w1",
           "mlp_w2")


def kernel(x, c, mod_w, mod_b, norm1_g, w_in, gate_b, conv_w, conv_b, conv_ln_g, conv_ln_b, w_conv_proj, hgrn_lb, hgrn_norm_g, w_hgrn_proj, sb_qn_g, sb_kn_g, w_sb_proj, w_out, norm2_g, mlp_w1, mlp_w2, loss_target, m_mod_w, m_mod_b, m_norm1_g, m_w_in, m_gate_b, m_conv_w, m_conv_b, m_conv_ln_g, m_conv_ln_b, m_w_conv_proj, m_hgrn_lb, m_hgrn_norm_g, m_w_hgrn_proj, m_sb_qn_g, m_sb_kn_g, m_w_sb_proj, m_w_out, m_norm2_g, m_mlp_w1, m_mlp_w2, v_mod_w, v_mod_b, v_norm1_g, v_w_in, v_gate_b, v_conv_w, v_conv_b, v_conv_ln_g, v_conv_ln_b, v_w_conv_proj, v_hgrn_lb, v_hgrn_norm_g, v_w_hgrn_proj, v_sb_qn_g, v_sb_kn_g, v_w_sb_proj, v_w_out, v_norm2_g, v_mlp_w1, v_mlp_w2):
    params = dict(mod_w=mod_w, mod_b=mod_b, norm1_g=norm1_g, w_in=w_in, gate_b=gate_b, conv_w=conv_w, conv_b=conv_b,
                  conv_ln_g=conv_ln_g, conv_ln_b=conv_ln_b, w_conv_proj=w_conv_proj, hgrn_lb=hgrn_lb,
                  hgrn_norm_g=hgrn_norm_g, w_hgrn_proj=w_hgrn_proj, sb_qn_g=sb_qn_g, sb_kn_g=sb_kn_g,
                  w_sb_proj=w_sb_proj, w_out=w_out, norm2_g=norm2_g, mlp_w1=mlp_w1, mlp_w2=mlp_w2)
    mom1 = dict(mod_w=m_mod_w, mod_b=m_mod_b, norm1_g=m_norm1_g, w_in=m_w_in, gate_b=m_gate_b, conv_w=m_conv_w,
                conv_b=m_conv_b, conv_ln_g=m_conv_ln_g, conv_ln_b=m_conv_ln_b, w_conv_proj=m_w_conv_proj,
                hgrn_lb=m_hgrn_lb, hgrn_norm_g=m_hgrn_norm_g, w_hgrn_proj=m_w_hgrn_proj, sb_qn_g=m_sb_qn_g,
                sb_kn_g=m_sb_kn_g, w_sb_proj=m_w_sb_proj, w_out=m_w_out, norm2_g=m_norm2_g, mlp_w1=m_mlp_w1,
                mlp_w2=m_mlp_w2)
    mom2 = dict(mod_w=v_mod_w, mod_b=v_mod_b, norm1_g=v_norm1_g, w_in=v_w_in, gate_b=v_gate_b, conv_w=v_conv_w,
                conv_b=v_conv_b, conv_ln_g=v_conv_ln_g, conv_ln_b=v_conv_ln_b, w_conv_proj=v_w_conv_proj,
                hgrn_lb=v_hgrn_lb, hgrn_norm_g=v_hgrn_norm_g, w_hgrn_proj=v_w_hgrn_proj, sb_qn_g=v_sb_qn_g,
                sb_kn_g=v_sb_kn_g, w_sb_proj=v_w_sb_proj, w_out=v_w_out, norm2_g=v_norm2_g, mlp_w1=v_mlp_w1,
                mlp_w2=v_mlp_w2)
    xi, yi, ci = _mesh_place()
    me = _block_of(xi, yi, ci)
    cw_cols = conv_w.shape[2]

    tiny = _pack_rows([c, conv_w], LANE)
    g_tiny, g_win0 = _comm_alone(_GatherPlan([tiny, _pack_weights(W_IN, params, 0)]), "gather_first")
    c_rows = D_MODEL // LANE
    c_all = g_tiny[:, :c_rows].reshape(N_DEV, D_MODEL)
    n_cw = DEPTH * CONV_WIDTH * cw_cols
    conv_w_full = g_tiny[:, c_rows:c_rows + n_cw // LANE].reshape(N_DEV, DEPTH, CONV_WIDTH, cw_cols)
    conv_w_full = conv_w_full.transpose(1, 2, 0, 3).reshape(DEPTH, CONV_WIDTH, CONV_CH)

    (g_mod,) = _comm_alone(_GatherPlan([_mod_local(c_all, mod_w, "mod_local")]), "gather_mod")
    mod = lax.dynamic_index_in_dim(g_mod, me, axis=2, keepdims=False)
    mod = mod.transpose(1, 0, 2).reshape(DEPTH, 6 * D_MODEL) + mod_b

    sm = dict(norm1_g=norm1_g, norm2_g=norm2_g, gate_b=gate_b, conv_w=conv_w_full, conv_b=conv_b, conv_ln_g=conv_ln_g,
              conv_ln_b=conv_ln_b, hgrn_lb=hgrn_lb, hgrn_norm_g=hgrn_norm_g, sb_qn_g=sb_qn_g, sb_kn_g=sb_kn_g)
    vecs = [_layer_vectors(l, mod, sm) for l in range(DEPTH)]

    wts = [_unpack_gathered(W_IN, g_win0), None]
    rest0 = _GatherPlan([_pack_weights(WIDE_REST, params, 0), _pack_weights(NARROW, params, 0)])
    all1 = _GatherPlan([_pack_weights(W_IN + WIDE_REST, params, 1), _pack_weights(NARROW, params, 1)])
    sv0, got_rest0, got_all1 = _layer_fwd_mixers(x[0], vecs[0], wts[0]["win_t"], "_l0", rest0, all1)
    wts[0].update(_unpack_gathered(WIDE_REST, got_rest0[0]))
    wts[0].update(_unpack_gathered(NARROW, got_rest0[1]))
    wts[1] = _unpack_gathered(W_IN + WIDE_REST, got_all1[0])
    wts[1].update(_unpack_gathered(NARROW, got_all1[1]))
    y = _layer_fwd_out(sv0, vecs[0], wts[0], "_l0")
    sv1, _, _ = _layer_fwd_mixers(y, vecs[1], wts[1]["win_t"], "_l1")
    y = _layer_fwd_out(sv1, vecs[1], wts[1], "_l1")
    dy, sq = _loss_head(y, loss_target[0], "loss_head")
    loss = lax.psum(0.5 * jnp.sum(sq) / D_MODEL, ("x", "y", "c"))

    dy, big1, small1, _ = _layer_bwd(dy, sv1, vecs[1], wts[1], "_l1")
    plans = dict(
        hgrn=lambda big: _ExchangePlan([_pack_grads(WIDE_REST, big), _pack_grads(NARROW, big)]),
        sb=lambda big: _ExchangePlan([_pack_grads(W_IN + WIDE_REST, big1), _pack_grads(NARROW, big1)]),
        dh=lambda big: _ExchangePlan([_pack_grads(W_IN, big)]))
    dx, _, small0, got = _layer_bwd(dy, sv0, vecs[0], wts[0], "_l0", plans)
    smalls = [small0, small1]
    shard = [{}, {}]
    shard[0].update(_unpack_shard_grads(W_IN, _sum8(got["dh"][0], "sum_grads_win0")))
    shard[0].update(_unpack_shard_grads(WIDE_REST, _sum8(got["hgrn"][0], "sum_grads_wide0")))
    shard[0].update(_unpack_shard_grads(NARROW, _sum8(got["hgrn"][1], "sum_grads_narrow0")))
    shard[1].update(_unpack_shard_grads(W_IN + WIDE_REST, _sum8(got["sb"][0], "sum_grads_wide1")))
    shard[1].update(_unpack_shard_grads(NARROW, _sum8(got["sb"][1], "sum_grads_narrow1")))
    grads = {name: jnp.stack([shard[l][name] for l in range(DEPTH)]) for name in shard[0]}

    small_parts = []
    for name, _ in SMALL:
        key = "mod" if name == "mod_b" else name
        if name == "hgrn_lb":
            small_parts.append(smalls[0][key] + smalls[1][key])
        else:
            small_parts.append(jnp.stack([smalls[l][key] for l in range(DEPTH)]))
    (g_small,) = _comm_alone(_GatherPlan([_pack_rows(small_parts, LANE)]), "gather_small_grads")
    small_sum = _sum8(g_small, "sum_small_grads").reshape(-1)
    off = 0
    for name, per_layer in SMALL:
        grads[name] = small_sum[off:off + DEPTH * per_layer].reshape(params[name].shape if name != "conv_w" else (DEPTH, CONV_WIDTH, CONV_CH))
        off += DEPTH * per_layer
    grads["conv_w"] = lax.dynamic_slice_in_dim(grads["conv_w"], me * cw_cols, cw_cols, axis=2)
    cols = mod_w.shape[2]
    dmod_all = g_small.reshape(N_DEV, -1)[:, :DEPTH * 6 * D_MODEL].reshape(N_DEV, DEPTH, 6 * D_MODEL)
    dmod_mine = lax.dynamic_slice_in_dim(dmod_all, me * cols, cols, axis=2).transpose(1, 0, 2)
    grads["mod_w"] = _modw_grad(c_all, dmod_mine, "mod_w_grad")

    delta, new_m, new_v = {}, {}, {}
    small_names = [n for n, _ in SMALL]
    for name in WEIGHTS:
        if name not in small_names:
            delta[name], new_m[name], new_v[name] = _adamw_nd(params[name], grads[name], mom1[name], mom2[name], f"adamw_{name}")
    packed = [_pack_rows([d[n] for n in small_names], LANE) for d in (params, grads, mom1, mom2)]
    outs = [o.reshape(-1) for o in _adamw(*packed, "adamw_small")]
    off = 0
    for name in small_names:
        size = params[name].size
        for dst, o in zip((delta, new_m, new_v), outs):
            dst[name] = o[off:off + size].reshape(params[name].shape)
        off += size
    return (loss, dx[None], *[grads[n] for n in WEIGHTS], *[delta[n] for n in WEIGHTS],
            *[new_m[n] for n in WEIGHTS], *[new_v[n] for n in WEIGHTS])
```

```python
import functools

import jax
import jax.numpy as jnp
import numpy as np
from jax import lax
from jax.experimental import pallas as pl
from jax.experimental.pallas import tpu as pltpu

F32 = jnp.float32
BF16 = jnp.bfloat16

D_MODEL = 1024
DEPTH = 2
N_DEV = 8
CONV_CH = 512
CONV_WIDTH = 31
CONV_HALO = 32
HG_HEADS = 4
HG_DK = 128
SB_HEADS = 8
SB_DH = 64
D_IN = 7680
D_FF = 4096
EPS = 1e-6
SB_BLK = 128
SB_DEAD = -104.0
SB_FIXED = 3
HG_CHUNK = 128

ADAM_LR = 0.001
ADAM_B1 = 0.9
ADAM_B2 = 0.999
ADAM_EPS = 1e-08
ADAM_WD = 0.01
ADAM_STEP = 10

VMEM_LIMIT = 48 * 1024 * 1024

NN = ((1,), (0,))
NT = ((1,), (1,))
TN = ((0,), (0,))
_DIMS = {"nn": NN, "nt": NT, "tn": TN}


def _sds(shape, dtype):
    return jax.ShapeDtypeStruct(shape, dtype)


def _params(*semantics):
    return pltpu.CompilerParams(dimension_semantics=semantics, vmem_limit_bytes=VMEM_LIMIT)


def _dot(a, b, dims):
    return lax.dot_general(a, b, (dims, ((), ())), preferred_element_type=F32)


@functools.partial(jax.custom_vjp, nondiff_argnums=(2,))
def _bdot(a, b, mode):
    return _dot(a.astype(BF16), b.astype(BF16), _DIMS[mode])


def _bdot_fwd(a, b, mode):
    return _bdot(a, b, mode), (a.astype(BF16), b.astype(BF16))


def _bdot_bwd(mode, res, g):
    a, b = res
    g = g.astype(BF16)
    if mode == "nn":
        return _dot(g, b, NT), _dot(a, g, TN)
    if mode == "nt":
        return _dot(g, b, NN), _dot(g, a, TN)
    return _dot(b, g, NT), _dot(a, g, NN)


_bdot.defvjp(_bdot_fwd, _bdot_bwd)


def _split(x):
    hi = x.astype(BF16)
    lo = (x - hi.astype(F32)).astype(BF16)
    return hi, lo


def _xdot_right(x, m, dims=NN):
    hi, lo = _split(x)
    return _dot(hi, m, dims) + _dot(lo, m, dims)


def _xdot_left(m, x, dims=NN):
    hi, lo = _split(x)
    return _dot(m, hi, dims) + _dot(m, lo, dims)


@jax.custom_vjp
def _xr(x, m):
    return _xdot_right(x, m)


def _xr_fwd(x, m):
    return _xdot_right(x, m), m


def _xr_bwd(m, g):
    return _xdot_right(g, m, NT), jnp.zeros_like(m)


_xr.defvjp(_xr_fwd, _xr_bwd)


def _norm_mod(x, g, sc, sh):
    r = lax.rsqrt(jnp.mean(x * x, axis=-1, keepdims=True) + EPS)
    return x * r * g * (1.0 + sc) + sh


MESH = pl.DeviceIdType.MESH
HBM_SPEC = pl.BlockSpec(memory_space=pltpu.HBM)


def _mesh_place():
    return lax.axis_index("x"), lax.axis_index("y"), lax.axis_index("c")


def _block_of(px, py, pc):
    return 4 * px + 2 * py + pc


def _sem_scratch(n):
    return [pltpu.SemaphoreType.DMA((n, N_DEV - 1)), pltpu.SemaphoreType.DMA((n, N_DEV - 1)), pltpu.SemaphoreType.DMA((n,))]


class _GatherPlan:
    def __init__(self, xs):
        self.xs = list(xs)
        self.n = len(self.xs)
        self.out_shape = [_sds((N_DEV, *v.shape), v.dtype) for v in self.xs]
        self.scratch = _sem_scratch(self.n)

    def _parts(self, x_refs, out_refs, sems):
        send_sems, recv_sems, local_sems = sems
        x, y, c = _mesh_place()
        me, sibling = (x, y, c), (x, y, 1 - c)
        chips = [(1 - x, y), (x, 1 - y), (1 - x, 1 - y)]

        def copy(a, k, block, to, src=None):
            rows = out_refs[a].at[_block_of(*block)]
            return pltpu.make_async_remote_copy(
                src_ref=rows if src is None else src, dst_ref=rows, send_sem=send_sems.at[a, k],
                recv_sem=recv_sems.at[a, k], device_id=to, device_id_type=MESH)

        local = [pltpu.make_async_copy(x_refs[a], out_refs[a].at[_block_of(*me)], local_sems.at[a])
                 for a in range(self.n)]
        first = []
        for a in range(self.n):
            first.append(copy(a, 0, me, sibling, src=x_refs[a]))
            first += [copy(a, 1 + j, me, (*chip, c), src=x_refs[a]) for j, chip in enumerate(chips)]
        return me, sibling, chips, c, copy, local, first

    def start(self, x_refs, out_refs, sems):
        *_, local, first = self._parts(x_refs, out_refs, sems)
        for cp in local + first:
            cp.start()

    def finish(self, x_refs, out_refs, sems):
        me, sibling, chips, c, copy, local, first = self._parts(x_refs, out_refs, sems)
        passed = []
        for j, chip in enumerate(chips):
            for a in range(self.n):
                copy(a, 1 + j, (*chip, c), me).wait_recv()
                fwd = copy(a, 4 + j, (*chip, c), sibling)
                fwd.start()
                passed.append(fwd)
        for a in range(self.n):
            copy(a, 0, sibling, me).wait_recv()
            for j, chip in enumerate(chips):
                copy(a, 4 + j, (*chip, 1 - c), me).wait_recv()
        for cp in first + passed:
            cp.wait_send()
        for cp in local:
            cp.wait()


class _ExchangePlan:
    def __init__(self, xs):
        self.xs = list(xs)
        self.n = len(self.xs)
        self.out_shape = [_sds(v.shape, v.dtype) for v in self.xs]
        self.scratch = _sem_scratch(self.n)

    def _parts(self, in_refs, out_refs, sems):
        send_sems, recv_sems, local_sems = sems
        x, y, c = _mesh_place()
        mine = _block_of(x, y, c)
        peers = [(1 - x if k & 4 else x, 1 - y if k & 2 else y, 1 - c if k & 1 else c) for k in range(1, N_DEV)]

        def copy(a, k, slot_src, slot_dst):
            return pltpu.make_async_remote_copy(
                src_ref=in_refs[a].at[slot_src], dst_ref=out_refs[a].at[slot_dst], send_sem=send_sems.at[a, k],
                recv_sem=recv_sems.at[a, k], device_id=peers[k], device_id_type=MESH)

        local = [pltpu.make_async_copy(in_refs[a].at[mine], out_refs[a].at[mine], local_sems.at[a])
                 for a in range(self.n)]
        sends = [copy(a, k, _block_of(*peers[k]), mine) for a in range(self.n) for k in range(N_DEV - 1)]
        arrivals = [copy(a, k, _block_of(*peers[k]), _block_of(*peers[k])) for a in range(self.n) for k in range(N_DEV - 1)]
        return local, sends, arrivals

    def start(self, in_refs, out_refs, sems):
        local, sends, _ = self._parts(in_refs, out_refs, sems)
        for cp in local + sends:
            cp.start()

    def finish(self, in_refs, out_refs, sems):
        local, sends, arrivals = self._parts(in_refs, out_refs, sems)
        for cp in arrivals:
            cp.wait_recv()
        for cp in sends:
            cp.wait_send()
        for cp in local:
            cp.wait()


def _call(body, args, *, grid, in_specs, out_specs, out_shape, scratch_shapes=(), semantics, name, comm=None):
    if comm is None:
        return pl.pallas_call(
            body, grid=grid, in_specs=list(in_specs), out_specs=list(out_specs), out_shape=list(out_shape),
            scratch_shapes=list(scratch_shapes), compiler_params=_params(*semantics), name=name)(*args)
    n_in, n_out, n_scr, n = len(in_specs), len(out_specs), len(scratch_shapes), comm.n

    def hosted(*refs):
        ins, rest = refs[:n_in], refs[n_in:]
        cin, rest = rest[:n], rest[n:]
        outs, rest = rest[:n_out], rest[n_out:]
        cout, rest = rest[:n], rest[n:]
        scr, sems = rest[:n_scr], rest[n_scr:]
        pids = [pl.program_id(d) for d in range(len(grid))]
        first = functools.reduce(jnp.logical_and, [p == 0 for p in pids])
        last = functools.reduce(jnp.logical_and, [p == g - 1 for p, g in zip(pids, grid)])

        @pl.when(first)
        def _():
            comm.start(cin, cout, sems)

        body(*ins, *outs, *scr)

        @pl.when(last)
        def _():
            comm.finish(cin, cout, sems)

    res = pl.pallas_call(
        hosted, grid=grid, in_specs=list(in_specs) + [HBM_SPEC] * n, out_specs=list(out_specs) + [HBM_SPEC] * n,
        out_shape=list(out_shape) + comm.out_shape, scratch_shapes=list(scratch_shapes) + comm.scratch,
        compiler_params=_params(*["arbitrary"] * len(grid)), name=name)(*args, *comm.xs)
    return res[:n_out], res[n_out:]


def _comm_alone(comm, name):
    def body(*refs):
        n = comm.n
        comm.start(refs[:n], refs[n:2 * n], refs[2 * n:])
        comm.finish(refs[:n], refs[n:2 * n], refs[2 * n:])

    return pl.pallas_call(
        body, in_specs=[HBM_SPEC] * comm.n, out_specs=[HBM_SPEC] * comm.n, out_shape=comm.out_shape,
        scratch_shapes=comm.scratch, name=name)(*comm.xs)


def _matmul(a, b, mode, out_dtype, tm, tn, tk, name, comm=None):
    if mode == "nn":
        (m, k), (_, n) = a.shape, b.shape
    elif mode == "nt":
        (m, k), (n, _) = a.shape, b.shape
    else:
        (k, m), (_, n) = a.shape, b.shape
    tm, tn, tk = min(tm, m), min(tn, n), min(tk, k)
    assert m % tm == 0 and n % tn == 0 and k % tk == 0, (name, m, n, k, tm, tn, tk)
    nk = k // tk
    dims = _DIMS[mode]

    def body(a_ref, b_ref, o_ref, acc_ref):
        if nk == 1:
            o_ref[...] = _dot(a_ref[...], b_ref[...], dims).astype(out_dtype)
            return
        kk = pl.program_id(2)

        @pl.when(kk == 0)
        def _():
            acc_ref[...] = _dot(a_ref[...], b_ref[...], dims)

        @pl.when((kk > 0) & (kk < nk - 1))
        def _():
            acc_ref[...] += _dot(a_ref[...], b_ref[...], dims)

        @pl.when(kk == nk - 1)
        def _():
            o_ref[...] = (acc_ref[...] + _dot(a_ref[...], b_ref[...], dims)).astype(out_dtype)

    if mode == "tn":
        a_spec = pl.BlockSpec((tk, tm), lambda i, j, kk: (kk, i))
        b_spec = pl.BlockSpec((tk, tn), lambda i, j, kk: (kk, j))
    elif mode == "nn":
        a_spec = pl.BlockSpec((tm, tk), lambda i, j, kk: (i, kk))
        b_spec = pl.BlockSpec((tk, tn), lambda i, j, kk: (kk, j))
    else:
        a_spec = pl.BlockSpec((tm, tk), lambda i, j, kk: (i, kk))
        b_spec = pl.BlockSpec((tn, tk), lambda i, j, kk: (j, kk))
    res = _call(
        body, (a, b), grid=(m // tm, n // tn, nk), in_specs=[a_spec, b_spec],
        out_specs=[pl.BlockSpec((tm, tn), lambda i, j, kk: (i, j))],
        out_shape=[_sds((m, n), out_dtype)], scratch_shapes=[pltpu.VMEM((tm, tn), F32)],
        semantics=("parallel", "parallel", "arbitrary"), name=name, comm=comm)
    return res[0] if comm is None else (res[0][0], res[1])


ROW_T = 512


def _prenorm(x, pv, name):
    s, d = x.shape
    t = min(ROW_T, s)

    def body(x_ref, pv_ref, h_ref):
        h = _norm_mod(x_ref[...], pv_ref[6:7, :], pv_ref[1:2, :], pv_ref[0:1, :])
        h_ref[...] = h.astype(BF16)

    return pl.pallas_call(
        body, grid=(s // t,),
        in_specs=[pl.BlockSpec((t, d), lambda i: (i, 0)), pl.BlockSpec((16, d), lambda i: (0, 0))],
        out_specs=pl.BlockSpec((t, d), lambda i: (i, 0)), out_shape=_sds((s, d), BF16),
        compiler_params=_params("parallel"), name=name)(x, pv)


def _prenorm_bwd(dh, dres, x, pv, name):
    s, d = x.shape
    t = min(ROW_T, s)

    def body(dh_ref, dres_ref, x_ref, pv_ref, dx_ref, sg_ref):
        i = pl.program_id(0)

        @pl.when(i == 0)
        def _():
            sg_ref[...] = jnp.zeros_like(sg_ref)

        _, vjp = jax.vjp(_norm_mod, x_ref[...], pv_ref[6:7, :], pv_ref[1:2, :], pv_ref[0:1, :])
        dx, dg, dsc, dsh = vjp(dh_ref[...])
        dx_ref[...] = dres_ref[...] + dx
        sg_ref[0:1, :] += dsh
        sg_ref[1:2, :] += dsc
        sg_ref[2:3, :] += dg

    row = pl.BlockSpec((t, d), lambda i: (i, 0))
    return pl.pallas_call(
        body, grid=(s // t,),
        in_specs=[row, row, row, pl.BlockSpec((16, d), lambda i: (0, 0))],
        out_specs=[row, pl.BlockSpec((8, d), lambda i: (0, 0))],
        out_shape=[_sds((s, d), F32), _sds((8, d), F32)],
        compiler_params=_params("arbitrary"), name=name)(dh, dres, x, pv)


CONV_T = 256


def _conv_tile(a_ext, g_ext, w, b, ln_g, ln_b, n_out):
    u0 = a_ext * jax.nn.sigmoid(g_ext)
    off = CONV_HALO - (CONV_WIDTH - 1)
    acc = jnp.zeros((n_out, u0.shape[1]), F32) + b
    for r in range(8):
        taps = [k for k in range(CONV_WIDTH) if (off + k) % 8 == r]
        rows = n_out if r == 0 else n_out + 8
        part = None
        for k in taps:
            lo = (off + k) // 8 * 8
            term = w[k:k + 1, :] * u0[lo: lo + rows, :]
            part = term if part is None else part + term
        acc = acc + part[r: r + n_out, :]
    mu = jnp.mean(acc, axis=-1, keepdims=True)
    var = jnp.mean(jnp.square(acc - mu), axis=-1, keepdims=True)
    y = (acc - mu) * lax.rsqrt(var + EPS) * ln_g + ln_b
    return y * jax.nn.sigmoid(y)


def _conv_fwd(proj, conv_w, cp, name):
    s = proj.shape[0]
    t = min(CONV_T, s)
    c, h = CONV_CH, CONV_HALO

    def body(ap_ref, ac_ref, gp_ref, gc_ref, w_ref, cp_ref, o_ref):
        i = pl.program_id(0)
        live = (i > 0).astype(F32)
        a_ext = jnp.concatenate([ap_ref[t - h:, :] * live, ac_ref[...]], axis=0)
        g_ext = jnp.concatenate([gp_ref[t - h:, :], gc_ref[...]], axis=0)
        u = _conv_tile(a_ext, g_ext, w_ref[...], cp_ref[0:1, :], cp_ref[1:2, :], cp_ref[2:3, :], t)
        o_ref[...] = u.astype(BF16)

    prev = lambda col: pl.BlockSpec((t, c), lambda i: (jnp.maximum(i - 1, 0), col))
    cur = lambda col: pl.BlockSpec((t, c), lambda i: (i, col))
    return pl.pallas_call(
        body, grid=(s // t,),
        in_specs=[prev(0), cur(0), prev(1), cur(1),
                  pl.BlockSpec((CONV_WIDTH, c), lambda i: (0, 0)), pl.BlockSpec((8, c), lambda i: (0, 0))],
        out_specs=pl.BlockSpec((t, c), lambda i: (i, 0)), out_shape=_sds((s, c), BF16),
        compiler_params=_params("parallel"), name=name)(proj, proj, proj, proj, conv_w, cp)


def _conv_bwd(proj, do, conv_w, cp, name):
    s = proj.shape[0]
    t = min(CONV_T, s)
    c, h = CONV_CH, CONV_HALO
    nt = s // t

    def body(ap_ref, ac_ref, an_ref, gp_ref, gc_ref, gn_ref, doc_ref, don_ref, w_ref, cp_ref,
             da_ref, dg_ref, dw_ref, sg_ref):
        i = pl.program_id(0)

        @pl.when(i == 0)
        def _():
            dw_ref[...] = jnp.zeros_like(dw_ref)
            sg_ref[...] = jnp.zeros_like(sg_ref)

        first = (i > 0).astype(F32)
        last = (i < nt - 1).astype(F32)
        a_ext = jnp.concatenate([ap_ref[t - h:, :] * first, ac_ref[...], an_ref[:h, :] * last], axis=0)
        g_ext = jnp.concatenate([gp_ref[t - h:, :], gc_ref[...], gn_ref[:h, :]], axis=0)
        fn = functools.partial(_conv_tile, n_out=t + h)
        _, vjp = jax.vjp(fn, a_ext, g_ext, w_ref[...], cp_ref[0:1, :], cp_ref[1:2, :], cp_ref[2:3, :])
        ct_own = jnp.concatenate([doc_ref[...], jnp.zeros((h, c), F32)], axis=0)
        ct_all = jnp.concatenate([doc_ref[...], don_ref[:h, :] * last], axis=0)
        _, _, dw, db, dlg, dlb = vjp(ct_own)
        da, dg, _, _, _, _ = vjp(ct_all)
        da_ref[...] = da[h:h + t, :].astype(BF16)
        dg_ref[...] = dg[h:h + t, :].astype(BF16)
        dw_ref[...] += dw
        sg_ref[0:1, :] += db
        sg_ref[1:2, :] += dlg
        sg_ref[2:3, :] += dlb

    prev = lambda col: pl.BlockSpec((t, c), lambda i: (jnp.maximum(i - 1, 0), col))
    cur = lambda col: pl.BlockSpec((t, c), lambda i: (i, col))
    nxt = lambda col: pl.BlockSpec((t, c), lambda i: (jnp.minimum(i + 1, nt - 1), col))
    return pl.pallas_call(
        body, grid=(nt,),
        in_specs=[prev(0), cur(0), nxt(0), prev(1), cur(1), nxt(1), cur(0), nxt(0),
                  pl.BlockSpec((CONV_WIDTH, c), lambda i: (0, 0)), pl.BlockSpec((8, c), lambda i: (0, 0))],
        out_specs=[cur(0), cur(0), pl.BlockSpec((CONV_WIDTH, c), lambda i: (0, 0)),
                   pl.BlockSpec((8, c), lambda i: (0, 0))],
        out_shape=[_sds((s, c), BF16), _sds((s, c), BF16), _sds((CONV_WIDTH, c), F32), _sds((8, c), F32)],
        compiler_params=_params("arbitrary"), name=name)(proj, proj, proj, proj, proj, proj, do, do, conv_w, cp)


def _hgrn_levels(c):
    out, m = [], c // 2
    while m >= 1:
        out.append(m)
        m //= 2
    return out


def _hgrn_consts(c):
    t = np.arange(c)[:, None]
    j = np.arange(c)[None, :]
    mats = [j <= t, j > t]
    for m in _hgrn_levels(c):
        same = (t // m) == (j // m)
        mats += [same & (j <= t), same & (j > t)]
    return jnp.asarray(np.concatenate(mats, axis=0).astype(np.float32), dtype=BF16)


@jax.custom_vjp
def _cums(lc, mall):
    c = lc.shape[0]
    full = _xdot_left(mall, lc)
    return tuple(full[i * c:(i + 1) * c, :] for i in range(mall.shape[0] // c))


def _cums_fwd(lc, mall):
    return _cums(lc, mall), mall


def _cums_bwd(mall, cts):
    return _xdot_left(mall, jnp.concatenate(cts, axis=0), TN), jnp.zeros_like(mall)


_cums.defvjp(_cums_fwd, _cums_bwd)


def _hgrn_chunk(q, f, v, g, lbs, ng, st_in, mall):
    c = q.shape[0]
    keep = jax.nn.sigmoid(-f)
    if lbs:
        keep = (1.0 - jax.nn.sigmoid(lbs[1] - lbs[0])) * keep
    lc = jnp.log1p(-keep)
    qs = q * jax.nn.sigmoid(q)
    cs = _cums(lc, mall)
    o = _bdot(qs * jnp.exp(cs[0]), st_in, "nt")
    total = jnp.sum(lc, axis=0, keepdims=True)
    st_out = st_in * jnp.exp(total) + _bdot(v, keep * jnp.exp(cs[1]), "tn")
    r = lax.broadcasted_iota(jnp.int32, q.shape, 0)
    tt = lax.broadcasted_iota(jnp.int32, (c, c), 0)
    ss = lax.broadcasted_iota(jnp.int32, (c, c), 1)
    sc = jnp.where(tt == ss, jnp.sum(qs * keep, axis=-1, keepdims=True), 0.0)
    for li, m in enumerate(_hgrn_levels(c)):
        lg = m.bit_length() - 1
        odd = ((r >> lg) & 1) == 1
        qm = jnp.where(odd, qs * jnp.exp(cs[2 + 2 * li]), 0.0)
        km = jnp.where(odd, 0.0, keep * jnp.exp(cs[3 + 2 * li]))
        pair = (((tt >> lg) & 1) == 1) & ((ss >> lg) == (tt >> lg) - 1)
        sc = sc + jnp.where(pair, _bdot(qm, km, "nt"), 0.0)
    o = o + _bdot(sc, v, "nn")
    on = o * lax.rsqrt(jnp.mean(o * o, axis=-1, keepdims=True) + EPS) * ng
    return on * (g * jax.nn.sigmoid(g)), st_out


def _hgrn_fwd(proj, lb, ng, name, comm=None):
    s = proj.shape[0]
    c = HG_CHUNK
    nc = s // c
    mall = _hgrn_consts(c)
    col0 = 1024 // (HG_HEADS * HG_DK)

    def body(*refs):
        q_ref, f_ref, v_ref, g_ref = refs[:4]
        if lb is None:
            ng_ref, m_ref, y_ref, st_ref, scr = refs[4:]
        else:
            lb_ref, ng_ref, m_ref, y_ref, st_ref, scr = refs[4:]
        ci = pl.program_id(0)

        @pl.when(ci == 0)
        def _():
            scr[...] = jnp.zeros_like(scr)

        mall_v = m_ref[...]
        for h in range(HG_HEADS):
            hs = slice(h * HG_DK, (h + 1) * HG_DK)
            lbs = () if lb is None else (lb_ref[0:1, hs], lb_ref[1:2, hs])
            st_in = scr[h]
            st_ref[h] = st_in
            y, st_out = _hgrn_chunk(q_ref[:, hs], f_ref[:, hs], v_ref[:, hs], g_ref[:, hs], lbs, ng_ref[...], st_in, mall_v)
            y_ref[:, hs] = y.astype(BF16)
            scr[h] = st_out

    w = HG_HEADS * HG_DK
    col = lambda k: pl.BlockSpec((c, w), lambda ci: (ci, col0 + k))
    in_specs = [col(0), col(1), col(2), col(3)]
    args = [proj, proj, proj, proj]
    if lb is not None:
        in_specs.append(pl.BlockSpec((2, w), lambda ci: (0, 0)))
        args.append(lb)
    in_specs += [pl.BlockSpec((1, HG_DK), lambda ci: (0, 0)), pl.BlockSpec(mall.shape, lambda ci: (0, 0))]
    args += [ng, mall]
    return _call(
        body, args, grid=(nc,), in_specs=in_specs,
        out_specs=[pl.BlockSpec((c, w), lambda ci: (ci, 0)),
                   pl.BlockSpec((HG_HEADS, None, HG_DK, HG_DK), lambda ci: (0, ci, 0, 0))],
        out_shape=[_sds((s, w), BF16), _sds((HG_HEADS, nc, HG_DK, HG_DK), F32)],
        scratch_shapes=[pltpu.VMEM((HG_HEADS, HG_DK, HG_DK), F32)],
        semantics=("arbitrary",), name=name, comm=comm)


def _hgrn_bwd(proj, states, dy, lb, ng, name, comm=None):
    s = proj.shape[0]
    c = HG_CHUNK
    nc = s // c
    mall = _hgrn_consts(c)
    col0 = 1024 // (HG_HEADS * HG_DK)

    def body(*refs):
        q_ref, f_ref, v_ref, g_ref, st_ref, dy_ref = refs[:6]
        if lb is None:
            ng_ref, m_ref, dq_ref, df_ref, dv_ref, dg_ref, dlb_ref, dng_ref, scr = refs[6:]
        else:
            lb_ref, ng_ref, m_ref, dq_ref, df_ref, dv_ref, dg_ref, dlb_ref, dng_ref, scr = refs[6:]
        ci = pl.program_id(0)

        @pl.when(ci == 0)
        def _():
            scr[...] = jnp.zeros_like(scr)
            dlb_ref[...] = jnp.zeros_like(dlb_ref)
            dng_ref[...] = jnp.zeros_like(dng_ref)

        mall_v = m_ref[...]
        fn = lambda q, f, v, g, lbs_, ng_, st: _hgrn_chunk(q, f, v, g, lbs_, ng_, st, mall_v)
        for h in range(HG_HEADS):
            hs = slice(h * HG_DK, (h + 1) * HG_DK)
            lbs = () if lb is None else (lb_ref[0:1, hs], lb_ref[1:2, hs])
            _, vjp = jax.vjp(fn, q_ref[:, hs], f_ref[:, hs], v_ref[:, hs], g_ref[:, hs], lbs, ng_ref[...], st_ref[h])
            dq, df, dv, dg, dlbs, dng, dst = vjp((dy_ref[:, hs], scr[h]))
            dq_ref[:, hs] = dq.astype(BF16)
            df_ref[:, hs] = df.astype(BF16)
            dv_ref[:, hs] = dv.astype(BF16)
            dg_ref[:, hs] = dg.astype(BF16)
            scr[h] = dst
            dng_ref[0:1, :] += dng
            if lbs:
                dlb_ref[0:1, hs] += dlbs[0]
                dlb_ref[1:2, hs] += dlbs[1]

    w = HG_HEADS * HG_DK
    rev = lambda ci: nc - 1 - ci
    col = lambda k: pl.BlockSpec((c, w), lambda ci: (rev(ci), col0 + k))
    out_col = pl.BlockSpec((c, w), lambda ci: (rev(ci), 0))
    in_specs = [col(0), col(1), col(2), col(3),
                pl.BlockSpec((HG_HEADS, None, HG_DK, HG_DK), lambda ci: (0, rev(ci), 0, 0)), out_col]
    args = [proj, proj, proj, proj, states, dy]
    if lb is not None:
        in_specs.append(pl.BlockSpec((2, w), lambda ci: (0, 0)))
        args.append(lb)
    in_specs += [pl.BlockSpec((1, HG_DK), lambda ci: (0, 0)), pl.BlockSpec(mall.shape, lambda ci: (0, 0))]
    args += [ng, mall]
    return _call(
        body, args, grid=(nc,), in_specs=in_specs,
        out_specs=[out_col, out_col, out_col, out_col,
                   pl.BlockSpec((2, w), lambda ci: (0, 0)), pl.BlockSpec((8, HG_DK), lambda ci: (0, 0))],
        out_shape=[_sds((s, w), BF16)] * 4 + [_sds((2, w), F32), _sds((8, HG_DK), F32)],
        scratch_shapes=[pltpu.VMEM((HG_HEADS, HG_DK, HG_DK), F32)],
        semantics=("arbitrary",), name=name, comm=comm)


def _head_avg():
    w = SB_HEADS * SB_DH
    i = np.arange(w)
    return jnp.asarray(((i[:, None] // SB_DH) == (i[None, :] // SB_DH)).astype(np.float32) / SB_DH, dtype=BF16)


def _sb_norm(x, g_tiled, avg):
    ms = _xr(x * x, avg)
    return x * lax.rsqrt(ms + EPS) * g_tiled


def _sb_prep(proj, gq, gk, name):
    s = proj.shape[0]
    t = min(ROW_T, s)
    w = SB_HEADS * SB_DH
    avg = _head_avg()

    def body(q_ref, k_ref, v_ref, gq_ref, gk_ref, avg_ref, qn_ref, kn_ref, vb_ref):
        qn_ref[...] = _sb_norm(q_ref[...], gq_ref[...], avg_ref[...]).astype(BF16)
        kn_ref[...] = _sb_norm(k_ref[...], gk_ref[...], avg_ref[...]).astype(BF16)
        vb_ref[...] = v_ref[...].astype(BF16)

    col = lambda k: pl.BlockSpec((t, w), lambda i: (i, 6 + k))
    vec = pl.BlockSpec((1, w), lambda i: (0, 0))
    out = pl.BlockSpec((t, w), lambda i: (i, 0))
    return pl.pallas_call(
        body, grid=(s // t,), in_specs=[col(0), col(1), col(2), vec, vec, pl.BlockSpec((w, w), lambda i: (0, 0))],
        out_specs=[out, out, out], out_shape=[_sds((s, w), BF16)] * 3,
        compiler_params=_params("parallel"), name=name)(proj, proj, proj, gq, gk, avg)


def _sb_prep_bwd(proj, dqn, dkn, gq, gk, name):
    s = proj.shape[0]
    t = min(ROW_T, s)
    w = SB_HEADS * SB_DH
    avg = _head_avg()

    def body(q_ref, k_ref, dqn_ref, dkn_ref, gq_ref, gk_ref, avg_ref, dq_ref, dk_ref, sg_ref):
        i = pl.program_id(0)

        @pl.when(i == 0)
        def _():
            sg_ref[...] = jnp.zeros_like(sg_ref)

        avg_v = avg_ref[...]
        fn = lambda x, g: _sb_norm(x, g, avg_v)
        _, vq = jax.vjp(fn, q_ref[...], gq_ref[...])
        dq, dgq = vq(dqn_ref[...])
        _, vk = jax.vjp(fn, k_ref[...], gk_ref[...])
        dk, dgk = vk(dkn_ref[...])
        dq_ref[...] = dq.astype(BF16)
        dk_ref[...] = dk.astype(BF16)
        sg_ref[0:1, :] += dgq
        sg_ref[1:2, :] += dgk

    col = lambda k: pl.BlockSpec((t, w), lambda i: (i, 6 + k))
    vec = pl.BlockSpec((1, w), lambda i: (0, 0))
    row = pl.BlockSpec((t, w), lambda i: (i, 0))
    return pl.pallas_call(
        body, grid=(s // t,),
        in_specs=[col(0), col(1), row, row, vec, vec, pl.BlockSpec((w, w), lambda i: (0, 0))],
        out_specs=[row, row, pl.BlockSpec((8, w), lambda i: (0, 0))],
        out_shape=[_sds((s, w), BF16), _sds((s, w), BF16), _sds((8, w), F32)],
        compiler_params=_params("arbitrary"), name=name)(proj, proj, dqn, dkn, gq, gk, avg)


def _sb_tri(kind):
    j = np.arange(SB_BLK)[:, None]
    s = np.arange(SB_BLK)[None, :]
    tri = (j > s) if kind == "suffix" else (j < s)
    return jnp.asarray(np.concatenate([tri, np.ones_like(tri)], axis=1).astype(np.float32), dtype=BF16)


def _sb_scores(qm, kblk, mask):
    z = _dot(qm, kblk, NT) * (SB_DH ** -0.5)
    sp = jnp.maximum(z, 0.0) + jnp.log(1.0 + jnp.exp(-jnp.abs(z)))
    return z, sp, jnp.where(mask, -sp, 0.0)


def _sb_setup(b):
    lane = lax.broadcasted_iota(jnp.int32, (2 * b, b), 1)
    row = lax.broadcasted_iota(jnp.int32, (2 * b, b), 0)
    mine = (row >> (b.bit_length() - 1)) == (lane >> (SB_DH.bit_length() - 1))
    return lane, row & (b - 1), mine


def _sb_fwd(qn, kn, vb, name, comm=None):
    s, w = qn.shape
    b = SB_BLK
    nq = s // b
    tri = _sb_tri("suffix")

    def body(q_ref, k_ref, v_ref, tri_ref, o_ref):
        i = pl.program_id(1)
        lane, tt, mine = _sb_setup(b)
        q = q_ref[...]
        q2 = jnp.concatenate([q, q], axis=0)
        qm = jnp.where(mine, q2, jnp.zeros_like(q2))
        tri_v = tri_ref[...]

        def block(kb, lim, run, acc):
            off = pl.multiple_of(kb * b, b)
            kblk = k_ref[pl.ds(off, b), :]
            vblk = v_ref[pl.ds(off, b), :]
            mask = lane < lim
            z, sp, lk = _sb_scores(qm, kblk, mask)
            both = _xdot_right(lk, tri_v)
            a = jnp.where(mask, jnp.exp(z - sp + both[:, :b] + run), 0.0)
            return run + both[:, b:], acc + _dot(a.astype(BF16), vblk, NN)

        offs = [pl.multiple_of(jnp.maximum(i - j, 0) * b, b) for j in range(SB_FIXED)]
        masks = [lane < (tt if j == 0 else jnp.where(i >= j, b, 0)) for j in range(SB_FIXED)]
        scores = [_sb_scores(qm, k_ref[pl.ds(off, b), :], m) for off, m in zip(offs, masks)]
        boths = [_xdot_right(lk, tri_v) for _, _, lk in scores]
        run = acc = jnp.zeros((2 * b, b), F32)
        for j in range(SB_FIXED):
            z, sp, _ = scores[j]
            a = jnp.where(masks[j], jnp.exp(z - sp + boths[j][:, :b] + run), 0.0)
            acc = acc + _dot(a.astype(BF16), v_ref[pl.ds(offs[j], b), :], NN)
            run = run + boths[j][:, b:]

        def cond(carry):
            j, run_, _ = carry
            return (j <= i) & (jnp.max(run_) > SB_DEAD)

        def step(carry):
            j, run_, acc_ = carry
            run_, acc_ = block(i - j, b, run_, acc_)
            return j + 1, run_, acc_

        _, _, acc = lax.while_loop(cond, step, (jnp.int32(SB_FIXED), run, acc))
        o_ref[...] = jnp.where(lane[:b] < SB_DH, acc[:b], acc[b:]).astype(BF16)

    blk = pl.BlockSpec((b, b), lambda p, i: (i, p))
    full = pl.BlockSpec((s, b), lambda p, i: (0, p))
    return _call(
        body, (qn, kn, vb, tri), grid=(w // b, nq),
        in_specs=[blk, full, full, pl.BlockSpec(tri.shape, lambda p, i: (0, 0))],
        out_specs=[blk], out_shape=[_sds((s, w), BF16)],
        semantics=("parallel", "arbitrary"), name=name, comm=comm)


def _sb_bwd(qn, kn, vb, do, name, comm=None):
    s, w = qn.shape
    b = SB_BLK
    nq = s // b
    tri_s = _sb_tri("suffix")
    tri_p = _sb_tri("prefix")
    scale = SB_DH ** -0.5

    def body(q_ref, k_ref, v_ref, do_ref, ts_ref, tp_ref, dq_ref, dk_ref, dv_ref, dk_acc, dv_acc, dp_scr):
        i = pl.program_id(1)

        @pl.when(i == 0)
        def _():
            dk_acc[...] = jnp.zeros_like(dk_acc)
            dv_acc[...] = jnp.zeros_like(dv_acc)

        lane, tt, mine = _sb_setup(b)
        q = q_ref[...]
        q2 = jnp.concatenate([q, q], axis=0)
        qm = jnp.where(mine, q2, jnp.zeros_like(q2))
        dout = do_ref[...].astype(BF16)
        d2 = jnp.concatenate([dout, dout], axis=0)
        dom = jnp.where(mine, d2, jnp.zeros_like(d2))
        ts_v = ts_ref[...]
        tp_v = tp_ref[...]
        zero = jnp.zeros((2 * b, b), F32)

        def down(kb, lim, run):
            off = pl.multiple_of(kb * b, b)
            kblk = k_ref[pl.ds(off, b), :]
            vblk = v_ref[pl.ds(off, b), :]
            mask = lane < lim
            z, sp, lk = _sb_scores(qm, kblk, mask)
            both = _xdot_right(lk, ts_v)
            a = jnp.where(mask, jnp.exp(z - sp + both[:, :b] + run), 0.0)
            dv_acc[pl.ds(off, b), :] += _dot(a.astype(BF16), dom, TN)
            return _dot(dom, vblk, NT) * a, run + both[:, b:]

        def up(kb, lim, dp, pre, dq):
            off = pl.multiple_of(kb * b, b)
            kblk = k_ref[pl.ds(off, b), :]
            sig = jax.nn.sigmoid(_dot(qm, kblk, NT) * scale)
            both = _xdot_right(dp, tp_v)
            dz = jnp.where(lane < lim, dp * (1.0 - sig) - sig * (both[:, :b] + pre), 0.0) * scale
            dz = dz.astype(BF16)
            dk_acc[pl.ds(off, b), :] += _dot(dz, qm, TN)
            return pre + both[:, b:], dq + _dot(dz, kblk, NN)

        offs = [pl.multiple_of(jnp.maximum(i - j, 0) * b, b) for j in range(SB_FIXED)]
        masks = [lane < (tt if j == 0 else jnp.where(i >= j, b, 0)) for j in range(SB_FIXED)]
        kblks = [k_ref[pl.ds(off, b), :] for off in offs]
        scores = [_sb_scores(qm, kblk, m) for kblk, m in zip(kblks, masks)]
        das = [_dot(dom, v_ref[pl.ds(off, b), :], NT) for off in offs]
        boths = [_xdot_right(lk, ts_v) for _, _, lk in scores]
        run = zero
        dps = []
        for j in range(SB_FIXED):
            z, sp, _ = scores[j]
            a = jnp.where(masks[j], jnp.exp(z - sp + boths[j][:, :b] + run), 0.0)
            dps.append(das[j] * a)
            dv_acc[pl.ds(offs[j], b), :] += _dot(a.astype(BF16), dom, TN)
            run = run + boths[j][:, b:]

        def cond(carry):
            j, run_ = carry
            return (j <= i) & (jnp.max(run_) > SB_DEAD)

        def sweep_down(carry):
            j, run_ = carry
            dp, run_ = down(i - j, b, run_)
            dp_scr[i - j] = dp
            return j + 1, run_

        n_live, _ = lax.while_loop(cond, sweep_down, (jnp.int32(SB_FIXED), run))

        def sweep_up(jj, carry):
            kb = i - n_live + 1 + jj
            return up(kb, b, dp_scr[kb], *carry)

        pre, dq = lax.fori_loop(0, n_live - SB_FIXED, sweep_up, (zero, zero))
        pres = [_xdot_right(dp, tp_v) for dp in dps]
        for j in reversed(range(SB_FIXED)):
            z, sp, _ = scores[j]
            sig = jnp.exp(z - sp)
            dz = jnp.where(masks[j], dps[j] * (1.0 - sig) - sig * (pres[j][:, :b] + pre), 0.0) * scale
            dz = dz.astype(BF16)
            dk_acc[pl.ds(offs[j], b), :] += _dot(dz, qm, TN)
            dq = dq + _dot(dz, kblks[j], NN)
            pre = pre + pres[j][:, b:]
        dq_ref[...] = jnp.where(lane[:b] < SB_DH, dq[:b], dq[b:])

        @pl.when(i == nq - 1)
        def _():
            dk_ref[...] = dk_acc[...]
            dv_ref[...] = dv_acc[...].astype(BF16)

    blk = pl.BlockSpec((b, b), lambda p, i: (i, p))
    full = pl.BlockSpec((s, b), lambda p, i: (0, p))
    tri = pl.BlockSpec(tri_s.shape, lambda p, i: (0, 0))
    return _call(
        body, (qn, kn, vb, do, tri_s, tri_p), grid=(w // b, nq), in_specs=[blk, full, full, blk, tri, tri],
        out_specs=[blk, full, full], out_shape=[_sds((s, w), F32), _sds((s, w), F32), _sds((s, w), BF16)],
        scratch_shapes=[pltpu.VMEM((s, b), F32), pltpu.VMEM((s, b), F32), pltpu.VMEM((nq, 2 * b, b), F32)],
        semantics=("arbitrary", "arbitrary"), name=name, comm=comm)


MIX_T = 256
HALF = 512


def _gate_slices(ga, gb):
    return [(ga[:, 0:512], ga[:, 512:1024]), (ga[:, 1024:1536], gb[:, 0:512]), (gb[:, 512:1024], gb[:, 1024:1536])]


def _mix_fwd(u3, oh, osb, proj, x, pv, wc, wh, ws, wo, name):
    s, d = x.shape
    t = min(MIX_T, s)

    def body(u3_ref, oh_ref, os_ref, ga_ref, gb_ref, x_ref, pv_ref, wc_ref, wh_ref, ws_ref, wo_ref,
             x1_ref, h2_ref, mg_ref, mo_ref):
        ys = [_dot(u3_ref[...], wc_ref[...], NT), _dot(oh_ref[...], wh_ref[...], NT), _dot(os_ref[...], ws_ref[...], NT)]
        gl = _gate_slices(ga_ref[...], gb_ref[...])
        halves = []
        for hf in range(2):
            lo = hf * HALF
            acc = jnp.zeros((t, HALF), F32)
            for br in range(3):
                gate = jax.nn.sigmoid(gl[br][hf] + pv_ref[8 + br:9 + br, lo:lo + HALF])
                acc = acc + gate * ys[br][:, lo:lo + HALF]
            halves.append(acc)
        merged = jnp.concatenate(halves, axis=1).astype(BF16)
        mg_ref[...] = merged
        mo = _dot(merged, wo_ref[...], NN)
        mo_ref[...] = mo.astype(BF16)
        x1 = x_ref[...] + pv_ref[2:3, :] * mo
        x1_ref[...] = x1
        h2_ref[...] = _norm_mod(x1, pv_ref[7:8, :], pv_ref[4:5, :], pv_ref[3:4, :]).astype(BF16)

    br_spec = pl.BlockSpec((t, CONV_CH), lambda i: (i, 0))
    row = pl.BlockSpec((t, d), lambda i: (i, 0))
    wproj = pl.BlockSpec((d, CONV_CH), lambda i: (0, 0))
    return pl.pallas_call(
        body, grid=(s // t,),
        in_specs=[br_spec, br_spec, br_spec, pl.BlockSpec((t, 1536), lambda i: (i, 3)),
                  pl.BlockSpec((t, 1536), lambda i: (i, 4)), row, pl.BlockSpec((16, d), lambda i: (0, 0)),
                  wproj, wproj, wproj, pl.BlockSpec((d, d), lambda i: (0, 0))],
        out_specs=[row, row, row, row],
        out_shape=[_sds((s, d), F32), _sds((s, d), BF16), _sds((s, d), BF16), _sds((s, d), BF16)],
        compiler_params=_params("parallel"), name=name)(u3, oh, osb, proj, proj, x, pv, wc, wh, ws, wo)


def _mix_bwd(dx1, mo1, u3, oh, osb, proj, pv, wc, wh, ws, wo, name):
    s, d = dx1.shape
    t = min(MIX_T, s)

    def body(dx_ref, mo_ref, u3_ref, oh_ref, os_ref, ga_ref, gb_ref, pv_ref, wc_ref, wh_ref, ws_ref, wo_ref,
             dmo_ref, dyc_ref, dyh_ref, dys_ref, doc_ref, doh_ref, dos_ref, dgl_ref, sg_ref):
        i = pl.program_id(0)

        @pl.when(i == 0)
        def _():
            sg_ref[...] = jnp.zeros_like(sg_ref)

        dx = dx_ref[...]
        dmo = (dx * pv_ref[2:3, :]).astype(BF16)
        dmo_ref[...] = dmo
        sg_ref[0:1, :] += jnp.sum(dx * mo_ref[...].astype(F32), axis=0, keepdims=True)
        dmerged = _dot(dmo, wo_ref[...], NT)
        branches = [(u3_ref, wc_ref, dyc_ref, doc_ref), (oh_ref, wh_ref, dyh_ref, doh_ref), (os_ref, ws_ref, dys_ref, dos_ref)]
        gl = _gate_slices(ga_ref[...], gb_ref[...])
        for br, (o_ref, w_ref, dy_ref, do_ref) in enumerate(branches):
            y = _dot(o_ref[...], w_ref[...], NT)
            dys = []
            for hf in range(2):
                lo = hf * HALF
                gate = jax.nn.sigmoid(gl[br][hf] + pv_ref[8 + br:9 + br, lo:lo + HALF])
                dm = dmerged[:, lo:lo + HALF]
                dys.append(dm * gate)
                dgl = dm * y[:, lo:lo + HALF] * gate * (1.0 - gate)
                dgl_ref[:, br * d + lo: br * d + lo + HALF] = dgl.astype(BF16)
                sg_ref[1 + br:2 + br, lo:lo + HALF] += jnp.sum(dgl, axis=0, keepdims=True)
            dy = jnp.concatenate(dys, axis=1).astype(BF16)
            dy_ref[...] = dy
            do_ref[...] = _dot(dy, w_ref[...], NN)

    br_spec = pl.BlockSpec((t, CONV_CH), lambda i: (i, 0))
    row = pl.BlockSpec((t, d), lambda i: (i, 0))
    wproj = pl.BlockSpec((d, CONV_CH), lambda i: (0, 0))
    return pl.pallas_call(
        body, grid=(s // t,),
        in_specs=[row, row, br_spec, br_spec, br_spec, pl.BlockSpec((t, 1536), lambda i: (i, 3)),
                  pl.BlockSpec((t, 1536), lambda i: (i, 4)), pl.BlockSpec((16, d), lambda i: (0, 0)),
                  wproj, wproj, wproj, pl.BlockSpec((d, d), lambda i: (0, 0))],
        out_specs=[row, row, row, row, br_spec, br_spec, br_spec, pl.BlockSpec((t, 3 * d), lambda i: (i, 0)),
                   pl.BlockSpec((8, d), lambda i: (0, 0))],
        out_shape=[_sds((s, d), BF16)] * 4 + [_sds((s, CONV_CH), F32)] * 3 + [_sds((s, 3 * d), BF16), _sds((8, d), F32)],
        compiler_params=_params("arbitrary"), name=name)(dx1, mo1, u3, oh, osb, proj, proj, pv, wc, wh, ws, wo)


MLP_T = 512
MLP_F = 512


def _mlp_fwd(h2, x1, pv, w1t, w2, name):
    s, d = x1.shape
    t = min(MLP_T, s)
    nf = D_FF // MLP_F

    def body(h_ref, x_ref, pv_ref, w1_ref, w2_ref, x2_ref, mo_ref, acc_ref):
        f = pl.program_id(1)

        @pl.when(f == 0)
        def _():
            acc_ref[...] = jnp.zeros_like(acc_ref)

        a = jnp.maximum(_dot(h_ref[...], w1_ref[...], NT), 0.0)
        acc_ref[...] += _dot((a * a).astype(BF16), w2_ref[...], NN)

        @pl.when(f == nf - 1)
        def _():
            mo = acc_ref[...]
            mo_ref[...] = mo.astype(BF16)
            x2_ref[...] = x_ref[...] + pv_ref[5:6, :] * mo

    row = pl.BlockSpec((t, d), lambda i, f: (i, 0))
    wblk = pl.BlockSpec((MLP_F, d), lambda i, f: (f, 0))
    return pl.pallas_call(
        body, grid=(s // t, nf), in_specs=[row, row, pl.BlockSpec((16, d), lambda i, f: (0, 0)), wblk, wblk],
        out_specs=[row, row], out_shape=[_sds((s, d), F32), _sds((s, d), BF16)],
        scratch_shapes=[pltpu.VMEM((t, d), F32)],
        compiler_params=_params("parallel", "arbitrary"), name=name)(h2, x1, pv, w1t, w2)


def _mlp_bwd(dx2, h2, x1, mo2, pv, w1t, w2, name, comm=None):
    s, d = x1.shape
    t = min(MLP_T, s)
    nf = D_FF // MLP_F

    def body(dx_ref, h_ref, x_ref, mo_ref, pv_ref, w1_ref, w2_ref, dx1_ref, da_ref, b_ref, dmo_ref, sg_ref, acc_ref):
        i = pl.program_id(0)
        f = pl.program_id(1)

        @pl.when((i == 0) & (f == 0))
        def _():
            sg_ref[...] = jnp.zeros_like(sg_ref)

        @pl.when(f == 0)
        def _():
            acc_ref[...] = jnp.zeros_like(acc_ref)
            dx = dx_ref[...]
            dmo_ref[...] = (dx * pv_ref[5:6, :]).astype(BF16)
            sg_ref[0:1, :] += jnp.sum(dx * mo_ref[...].astype(F32), axis=0, keepdims=True)

        r = jnp.maximum(_dot(h_ref[...], w1_ref[...], NT), 0.0)
        b_ref[...] = (r * r).astype(BF16)
        da = (_dot(dmo_ref[...], w2_ref[...], NT) * (2.0 * r)).astype(BF16)
        da_ref[...] = da
        acc_ref[...] += _dot(da, w1_ref[...], NN)

        @pl.when(f == nf - 1)
        def _():
            _, vjp = jax.vjp(_norm_mod, x_ref[...], pv_ref[7:8, :], pv_ref[4:5, :], pv_ref[3:4, :])
            dxn, dg, dsc, dsh = vjp(acc_ref[...])
            dx1_ref[...] = dx_ref[...] + dxn
            sg_ref[1:2, :] += dsh
            sg_ref[2:3, :] += dsc
            sg_ref[3:4, :] += dg

    row = pl.BlockSpec((t, d), lambda i, f: (i, 0))
    wblk = pl.BlockSpec((MLP_F, d), lambda i, f: (f, 0))
    hid = pl.BlockSpec((t, MLP_F), lambda i, f: (i, f))
    return _call(
        body, (dx2, h2, x1, mo2, pv, w1t, w2), grid=(s // t, nf),
        in_specs=[row, row, row, row, pl.BlockSpec((16, d), lambda i, f: (0, 0)), wblk, wblk],
        out_specs=[row, hid, hid, row, pl.BlockSpec((8, d), lambda i, f: (0, 0))],
        out_shape=[_sds((s, d), F32), _sds((s, D_FF), BF16), _sds((s, D_FF), BF16), _sds((s, d), BF16), _sds((8, d), F32)],
        scratch_shapes=[pltpu.VMEM((t, d), F32)],
        semantics=("arbitrary", "arbitrary"), name=name, comm=comm)


def _loss_head(y, target, name):
    s, d = y.shape
    t = min(ROW_T, s)

    def body(y_ref, t_ref, dy_ref, ls_ref):
        i = pl.program_id(0)

        @pl.when(i == 0)
        def _():
            ls_ref[...] = jnp.zeros_like(ls_ref)

        e = y_ref[...] - t_ref[...]
        dy_ref[...] = e * (1.0 / d)
        ls_ref[...] += jnp.sum((e * e).reshape(t // 8, 8, d), axis=0)

    row = pl.BlockSpec((t, d), lambda i: (i, 0))
    return pl.pallas_call(
        body, grid=(s // t,), in_specs=[row, row], out_specs=[row, pl.BlockSpec((8, d), lambda i: (0, 0))],
        out_shape=[_sds((s, d), F32), _sds((8, d), F32)],
        compiler_params=_params("arbitrary"), name=name)(y, target)


def _layer_vectors(l, mod, sm):
    d = D_MODEL
    pv = jnp.concatenate([mod[l].reshape(6, d), sm["norm1_g"][l][None], sm["norm2_g"][l][None],
                          sm["gate_b"][l].reshape(3, d), jnp.zeros((5, d), F32)], axis=0)
    cp = jnp.concatenate([sm["conv_b"][l][None], sm["conv_ln_g"][l][None], sm["conv_ln_b"][l][None],
                          jnp.zeros((5, CONV_CH), F32)], axis=0)
    return dict(pv=pv, cp=cp, conv_w=sm["conv_w"][l], lb=(sm["hgrn_lb"] if l > 0 else None),
                ng=sm["hgrn_norm_g"][l][None], gq=jnp.tile(sm["sb_qn_g"][l], SB_HEADS)[None],
                gk=jnp.tile(sm["sb_kn_g"][l], SB_HEADS)[None])


def _hosted(res, comm):
    return res if comm is not None else (res, None)


def _layer_fwd_mixers(x, vec, win_t, tag, comm_proj=None, comm_hgrn=None, comm_sb=None):
    h = _prenorm(x, vec["pv"], f"prenorm{tag}")
    proj, got_proj = _hosted(_matmul(h, win_t, "nt", F32, 1024, 768, 1024, f"proj{tag}", comm_proj), comm_proj)
    u3 = _conv_fwd(proj, vec["conv_w"], vec["cp"], f"conv_fwd{tag}")
    (oh, states), got_hgrn = _hosted(_hgrn_fwd(proj, vec["lb"], vec["ng"], f"hgrn_fwd{tag}", comm_hgrn), comm_hgrn)
    qn, kn, vb = _sb_prep(proj, vec["gq"], vec["gk"], f"sb_prep{tag}")
    (osb,), got_sb = _hosted(_sb_fwd(qn, kn, vb, f"sb_fwd{tag}", comm_sb), comm_sb)
    saved = dict(x=x, h=h, proj=proj, u3=u3, oh=oh, states=states, qn=qn, kn=kn, vb=vb, osb=osb)
    return saved, (got_proj, got_hgrn, got_sb)


def _layer_fwd_out(sv, vec, w, tag):
    x1, h2, merged, mo1 = _mix_fwd(sv["u3"], sv["oh"], sv["osb"], sv["proj"], sv["x"], vec["pv"],
                                   w["wc_t"], w["wh_t"], w["ws_t"], w["wo"], f"mix_fwd{tag}")
    x2, mo2 = _mlp_fwd(h2, x1, vec["pv"], w["w1_t"], w["w2"], f"mlp_fwd{tag}")
    sv.update(x1=x1, h2=h2, merged=merged, mo1=mo1, mo2=mo2)
    return x2


def _layer_bwd(dx2, sv, vec, w, tag, plans=None):
    plans = plans or {}
    got = {}

    def plan_for(key, big_now):
        return plans[key](big_now) if key in plans else None

    pv = vec["pv"]
    big = {}
    comm = plan_for("mlp", big)
    (dx1, da, bsq, dmo2, sg_mlp), got["mlp"] = _hosted(
        _mlp_bwd(dx2, sv["h2"], sv["x1"], sv["mo2"], pv, w["w1_t"], w["w2"], f"mlp_bwd{tag}", comm), comm)
    big["w1_t"] = _matmul(da, sv["h2"], "tn", BF16, 1024, 1024, 1024, f"dw1{tag}")
    big["w2"] = _matmul(bsq, dmo2, "tn", BF16, 1024, 1024, 1024, f"dw2{tag}")
    dmo1, dyc, dyh, dys, doc, doh, dos, dgl, sg_mix = _mix_bwd(
        dx1, sv["mo1"], sv["u3"], sv["oh"], sv["osb"], sv["proj"], pv, w["wc_t"], w["wh_t"], w["ws_t"], w["wo"], f"mix_bwd{tag}")
    big["wo"] = _matmul(sv["merged"], dmo1, "tn", BF16, 1024, 1024, 1024, f"dwo{tag}")
    big["wc_t"] = _matmul(dyc, sv["u3"], "tn", BF16, 1024, 512, 1024, f"dwc{tag}")
    big["wh_t"] = _matmul(dyh, sv["oh"], "tn", BF16, 1024, 512, 1024, f"dwh{tag}")
    big["ws_t"] = _matmul(dys, sv["osb"], "tn", BF16, 1024, 512, 1024, f"dws{tag}")
    da_c, dg_c, dconv_w, sg_conv = _conv_bwd(sv["proj"], doc, vec["conv_w"], vec["cp"], f"conv_bwd{tag}")
    comm = plan_for("hgrn", big)
    (dq_h, df_h, di_h, dg_h, dlb, dng), got["hgrn"] = _hosted(
        _hgrn_bwd(sv["proj"], sv["states"], doh, vec["lb"], vec["ng"], f"hgrn_bwd{tag}", comm), comm)
    comm = plan_for("sb", big)
    (dqn, dkn, dv_s), got["sb"] = _hosted(_sb_bwd(sv["qn"], sv["kn"], sv["vb"], dos, f"sb_bwd{tag}", comm), comm)
    dq_s, dk_s, sg_sb = _sb_prep_bwd(sv["proj"], dqn, dkn, vec["gq"], vec["gk"], f"sb_prep_bwd{tag}")
    dproj = jnp.concatenate([da_c, dg_c, dq_h, df_h, di_h, dg_h, dq_s, dk_s, dv_s, dgl], axis=1)
    big["win_t"] = _matmul(dproj, sv["h"], "tn", BF16, 768, 1024, 1024, f"dwin{tag}")
    comm = plan_for("dh", big)
    dh, got["dh"] = _hosted(_matmul(dproj, w["win_t"], "nn", F32, 512, 1024, 1920, f"dh{tag}", comm), comm)
    dx, sg_pre = _prenorm_bwd(dh, dx1, sv["x"], pv, f"prenorm_bwd{tag}")
    small = dict(
        mod=jnp.stack([sg_pre[0], sg_pre[1], sg_mix[0], sg_mlp[1], sg_mlp[2], sg_mlp[0]]).reshape(6 * D_MODEL),
        norm1_g=sg_pre[2], norm2_g=sg_mlp[3], gate_b=sg_mix[1:4].reshape(3 * D_MODEL),
        conv_w=dconv_w, conv_b=sg_conv[0], conv_ln_g=sg_conv[1], conv_ln_b=sg_conv[2],
        hgrn_lb=dlb, hgrn_norm_g=dng[0],
        sb_qn_g=sg_sb[0].reshape(SB_HEADS, SB_DH).sum(0), sb_kn_g=sg_sb[1].reshape(SB_HEADS, SB_DH).sum(0))
    return dx, big, small, got


def _row_tile(r, cap=512):
    t = min(r, cap)
    while r % t or (t % 8 and t != r):
        t -= 1
    return t


def _sum8(z, name):
    _, r, c = z.shape
    t = _row_tile(r, 128 if c >= 1024 else 512)

    def body(z_ref, o_ref):
        acc = z_ref[0].astype(F32)
        for j in range(1, N_DEV):
            acc = acc + z_ref[j].astype(F32)
        o_ref[...] = acc

    return pl.pallas_call(
        body, grid=(r // t,), in_specs=[pl.BlockSpec((N_DEV, t, c), lambda i: (0, i, 0))],
        out_specs=pl.BlockSpec((t, c), lambda i: (i, 0)), out_shape=_sds((r, c), F32),
        compiler_params=_params("parallel"), name=name)(z)


def _adamw(w, g, m, v, name):
    r, c = w.shape
    t = _row_tile(r, 256)

    def body(w_ref, g_ref, m_ref, v_ref, d_ref, nm_ref, nv_ref):
        g_ = g_ref[...]
        nm = ADAM_B1 * m_ref[...] + (1.0 - ADAM_B1) * g_
        nv = ADAM_B2 * v_ref[...] + (1.0 - ADAM_B2) * jnp.square(g_)
        m_hat = nm / (1.0 - ADAM_B1 ** ADAM_STEP)
        v_hat = nv / (1.0 - ADAM_B2 ** ADAM_STEP)
        d_ref[...] = -ADAM_LR * (m_hat / (jnp.sqrt(v_hat) + ADAM_EPS) + ADAM_WD * w_ref[...])
        nm_ref[...] = nm
        nv_ref[...] = nv

    blk = pl.BlockSpec((t, c), lambda i: (i, 0))
    return pl.pallas_call(
        body, grid=(r // t,), in_specs=[blk] * 4, out_specs=[blk] * 3, out_shape=[_sds((r, c), F32)] * 3,
        compiler_params=_params("parallel"), name=name)(w, g, m, v)


def _mod_local(c_all, mod_w, name):
    depth, d, cols = mod_w.shape

    def body(c_ref, w_ref, o_ref):
        cv = c_ref[...]
        act = cv * jax.nn.sigmoid(cv)
        o_ref[...] = jnp.dot(act, w_ref[...], precision=lax.Precision.HIGHEST, preferred_element_type=F32)

    return pl.pallas_call(
        body, grid=(depth,),
        in_specs=[pl.BlockSpec((N_DEV, d), lambda l: (0, 0)), pl.BlockSpec((None, d, cols), lambda l: (l, 0, 0))],
        out_specs=pl.BlockSpec((None, N_DEV, cols), lambda l: (l, 0, 0)), out_shape=_sds((depth, N_DEV, cols), F32),
        compiler_params=_params("parallel"), name=name)(c_all, mod_w)


def _modw_grad(c_all, dmod, name):
    depth, _, cols = dmod.shape
    d = c_all.shape[1]

    def body(c_ref, g_ref, o_ref):
        cv = c_ref[...]
        act = cv * jax.nn.sigmoid(cv)
        o_ref[...] = lax.dot_general(act, g_ref[...], (TN, ((), ())), precision=lax.Precision.HIGHEST,
                                     preferred_element_type=F32)

    return pl.pallas_call(
        body, grid=(depth,),
        in_specs=[pl.BlockSpec((N_DEV, d), lambda l: (0, 0)), pl.BlockSpec((None, N_DEV, cols), lambda l: (l, 0, 0))],
        out_specs=pl.BlockSpec((None, d, cols), lambda l: (l, 0, 0)), out_shape=_sds((depth, d, cols), F32),
        compiler_params=_params("parallel"), name=name)(c_all, dmod)


LANE = 128
W_IN = (("w_in", 960, True),)
WIDE_REST = (("w_out", 128, False), ("mlp_w2", 512, False), ("mlp_w1", 512, True))
NARROW = (("w_conv_proj", 128, True), ("w_hgrn_proj", 128, True), ("w_sb_proj", 128, True))
BIG_KEY = {"w_in": "win_t", "w_out": "wo", "mlp_w2": "w2", "mlp_w1": "w1_t",
           "w_conv_proj": "wc_t", "w_hgrn_proj": "wh_t", "w_sb_proj": "ws_t"}
SMALL = (("mod_b", 6144), ("norm1_g", 1024), ("gate_b", 3072), ("conv_w", CONV_WIDTH * CONV_CH), ("conv_b", 512),
         ("conv_ln_g", 512), ("conv_ln_b", 512), ("hgrn_lb", 512), ("hgrn_norm_g", 128), ("sb_qn_g", 64),
         ("sb_kn_g", 64), ("norm2_g", 1024))


def _pack_rows(parts, width):
    flat = jnp.concatenate([p.reshape(-1) for p in parts])
    rows = -(-flat.shape[0] // width)
    rows = -(-rows // 8) * 8
    return jnp.pad(flat, (0, rows * width - flat.shape[0])).reshape(rows, width)


def _pack_weights(spec, params, l):
    parts = []
    for name, _, transposed in spec:
        w = params[name][l]
        parts.append((w.T if transposed else w).astype(BF16))
    return jnp.concatenate(parts, axis=0)


def _unpack_gathered(spec, g):
    out = {}
    off = 0
    for name, rows, _ in spec:
        out[BIG_KEY[name]] = g[:, off:off + rows].reshape(N_DEV * rows, g.shape[2])
        off += rows
    return out


def _pack_grads(spec, big):
    parts = []
    for name, rows, _ in spec:
        gmat = big[BIG_KEY[name]]
        parts.append(gmat.reshape(N_DEV, rows, gmat.shape[1]))
    return jnp.concatenate(parts, axis=1)


def _unpack_shard_grads(spec, gsum):
    out = {}
    off = 0
    for name, rows, transposed in spec:
        blk = gsum[off:off + rows]
        out[name] = blk.T if transposed else blk
        off += rows
    return out


def _adamw_nd(w, g, m, v, name):
    shape = w.shape
    two = lambda a: a.reshape(-1, shape[-1])
    return [o.reshape(shape) for o in _adamw(two(w), two(g), two(m), two(v), name)]


WEIGHTS = ("mod_w", "mod_b", "norm1_g", "w_in", "gate_b", "conv_w", "conv_b", "conv_ln_g", "conv_ln_b", "w_conv_proj",
           "hgrn_lb", "hgrn_norm_g", "w_hgrn_proj", "sb_qn_g", "sb_kn_g", "w_sb_proj", "w_out", "norm2_g", "mlp_w1",
           "mlp_w2")


def kernel(x, c, mod_w, mod_b, norm1_g, w_in, gate_b, conv_w, conv_b, conv_ln_g, conv_ln_b, w_conv_proj, hgrn_lb, hgrn_norm_g, w_hgrn_proj, sb_qn_g, sb_kn_g, w_sb_proj, w_out, norm2_g, mlp_w1, mlp_w2, loss_target, m_mod_w, m_mod_b, m_norm1_g, m_w_in, m_gate_b, m_conv_w, m_conv_b, m_conv_ln_g, m_conv_ln_b, m_w_conv_proj, m_hgrn_lb, m_hgrn_norm_g, m_w_hgrn_proj, m_sb_qn_g, m_sb_kn_g, m_w_sb_proj, m_w_out, m_norm2_g, m_mlp_w1, m_mlp_w2, v_mod_w, v_mod_b, v_norm1_g, v_w_in, v_gate_b, v_conv_w, v_conv_b, v_conv_ln_g, v_conv_ln_b, v_w_conv_proj, v_hgrn_lb, v_hgrn_norm_g, v_w_hgrn_proj, v_sb_qn_g, v_sb_kn_g, v_w_sb_proj, v_w_out, v_norm2_g, v_mlp_w1, v_mlp_w2):
    params = dict(mod_w=mod_w, mod_b=mod_b, norm1_g=norm1_g, w_in=w_in, gate_b=gate_b, conv_w=conv_w, conv_b=conv_b,
                  conv_ln_g=conv_ln_g, conv_ln_b=conv_ln_b, w_conv_proj=w_conv_proj, hgrn_lb=hgrn_lb,
                  hgrn_norm_g=hgrn_norm_g, w_hgrn_proj=w_hgrn_proj, sb_qn_g=sb_qn_g, sb_kn_g=sb_kn_g,
                  w_sb_proj=w_sb_proj, w_out=w_out, norm2_g=norm2_g, mlp_w1=mlp_w1, mlp_w2=mlp_w2)
    mom1 = dict(mod_w=m_mod_w, mod_b=m_mod_b, norm1_g=m_norm1_g, w_in=m_w_in, gate_b=m_gate_b, conv_w=m_conv_w,
                conv_b=m_conv_b, conv_ln_g=m_conv_ln_g, conv_ln_b=m_conv_ln_b, w_conv_proj=m_w_conv_proj,
                hgrn_lb=m_hgrn_lb, hgrn_norm_g=m_hgrn_norm_g, w_hgrn_proj=m_w_hgrn_proj, sb_qn_g=m_sb_qn_g,
                sb_kn_g=m_sb_kn_g, w_sb_proj=m_w_sb_proj, w_out=m_w_out, norm2_g=m_norm2_g, mlp_w1=m_mlp_w1,
                mlp_w2=m_mlp_w2)
    mom2 = dict(mod_w=v_mod_w, mod_b=v_mod_b, norm1_g=v_norm1_g, w_in=v_w_in, gate_b=v_gate_b, conv_w=v_conv_w,
                conv_b=v_conv_b, conv_ln_g=v_conv_ln_g, conv_ln_b=v_conv_ln_b, w_conv_proj=v_w_conv_proj,
                hgrn_lb=v_hgrn_lb, hgrn_norm_g=v_hgrn_norm_g, w_hgrn_proj=v_w_hgrn_proj, sb_qn_g=v_sb_qn_g,
                sb_kn_g=v_sb_kn_g, w_sb_proj=v_w_sb_proj, w_out=v_w_out, norm2_g=v_norm2_g, mlp_w1=v_mlp_w1,
                mlp_w2=v_mlp_w2)
    xi, yi, ci = _mesh_place()
    me = _block_of(xi, yi, ci)
    cw_cols = conv_w.shape[2]

    tiny = _pack_rows([c, conv_w], LANE)
    g_tiny, g_win0 = _comm_alone(_GatherPlan([tiny, _pack_weights(W_IN, params, 0)]), "gather_first")
    c_rows = D_MODEL // LANE
    c_all = g_tiny[:, :c_rows].reshape(N_DEV, D_MODEL)
    n_cw = DEPTH * CONV_WIDTH * cw_cols
    conv_w_full = g_tiny[:, c_rows:c_rows + n_cw // LANE].reshape(N_DEV, DEPTH, CONV_WIDTH, cw_cols)
    conv_w_full = conv_w_full.transpose(1, 2, 0, 3).reshape(DEPTH, CONV_WIDTH, CONV_CH)

    (g_mod,) = _comm_alone(_GatherPlan([_mod_local(c_all, mod_w, "mod_local")]), "gather_mod")
    mod = lax.dynamic_index_in_dim(g_mod, me, axis=2, keepdims=False)
    mod = mod.transpose(1, 0, 2).reshape(DEPTH, 6 * D_MODEL) + mod_b

    sm = dict(norm1_g=norm1_g, norm2_g=norm2_g, gate_b=gate_b, conv_w=conv_w_full, conv_b=conv_b, conv_ln_g=conv_ln_g,
              conv_ln_b=conv_ln_b, hgrn_lb=hgrn_lb, hgrn_norm_g=hgrn_norm_g, sb_qn_g=sb_qn_g, sb_kn_g=sb_kn_g)
    vecs = [_layer_vectors(l, mod, sm) for l in range(DEPTH)]

    wts = [_unpack_gathered(W_IN, g_win0), None]
    win1 = _GatherPlan([_pack_weights(W_IN, params, 1)])
    rest0 = _GatherPlan([_pack_weights(WIDE_REST, params, 0), _pack_weights(NARROW, params, 0)])
    rest1 = _GatherPlan([_pack_weights(WIDE_REST, params, 1), _pack_weights(NARROW, params, 1)])
    sv0, (got_win1, got_rest0, got_rest1) = _layer_fwd_mixers(x[0], vecs[0], wts[0]["win_t"], "_l0", win1, rest0, rest1)
    wts[0].update(_unpack_gathered(WIDE_REST, got_rest0[0]))
    wts[0].update(_unpack_gathered(NARROW, got_rest0[1]))
    wts[1] = _unpack_gathered(W_IN, got_win1[0])
    wts[1].update(_unpack_gathered(WIDE_REST, got_rest1[0]))
    wts[1].update(_unpack_gathered(NARROW, got_rest1[1]))
    y = _layer_fwd_out(sv0, vecs[0], wts[0], "_l0")
    sv1, _ = _layer_fwd_mixers(y, vecs[1], wts[1]["win_t"], "_l1")
    y = _layer_fwd_out(sv1, vecs[1], wts[1], "_l1")
    dy, sq = _loss_head(y, loss_target[0], "loss_head")
    loss = lax.psum(0.5 * jnp.sum(sq) / D_MODEL, ("x", "y", "c"))

    dy, big1, small1, _ = _layer_bwd(dy, sv1, vecs[1], wts[1], "_l1")
    plans = dict(
        mlp=lambda big: _ExchangePlan([_pack_grads(W_IN, big1)]),
        sb=lambda big: _ExchangePlan([_pack_grads(WIDE_REST, big1), _pack_grads(NARROW, big1)]),
        hgrn=lambda big: _ExchangePlan([_pack_grads(WIDE_REST, big), _pack_grads(NARROW, big)]),
        dh=lambda big: _ExchangePlan([_pack_grads(W_IN, big)]))
    dx, _, small0, got = _layer_bwd(dy, sv0, vecs[0], wts[0], "_l0", plans)
    smalls = [small0, small1]
    shard = [{}, {}]
    shard[0].update(_unpack_shard_grads(W_IN, _sum8(got["dh"][0], "sum_grads_win0")))
    shard[0].update(_unpack_shard_grads(WIDE_REST, _sum8(got["hgrn"][0], "sum_grads_wide0")))
    shard[0].update(_unpack_shard_grads(NARROW, _sum8(got["hgrn"][1], "sum_grads_narrow0")))
    shard[1].update(_unpack_shard_grads(W_IN, _sum8(got["mlp"][0], "sum_grads_win1")))
    shard[1].update(_unpack_shard_grads(WIDE_REST, _sum8(got["sb"][0], "sum_grads_wide1")))
    shard[1].update(_unpack_shard_grads(NARROW, _sum8(got["sb"][1], "sum_grads_narrow1")))
    grads = {name: jnp.stack([shard[l][name] for l in range(DEPTH)]) for name in shard[0]}

    small_parts = []
    for name, _ in SMALL:
        key = "mod" if name == "mod_b" else name
        if name == "hgrn_lb":
            small_parts.append(smalls[0][key] + smalls[1][key])
        else:
            small_parts.append(jnp.stack([smalls[l][key] for l in range(DEPTH)]))
    (g_small,) = _comm_alone(_GatherPlan([_pack_rows(small_parts, LANE)]), "gather_small_grads")
    small_sum = _sum8(g_small, "sum_small_grads").reshape(-1)
    off = 0
    for name, per_layer in SMALL:
        grads[name] = small_sum[off:off + DEPTH * per_layer].reshape(params[name].shape if name != "conv_w" else (DEPTH, CONV_WIDTH, CONV_CH))
        off += DEPTH * per_layer
    grads["conv_w"] = lax.dynamic_slice_in_dim(grads["conv_w"], me * cw_cols, cw_cols, axis=2)
    cols = mod_w.shape[2]
    dmod_all = g_small.reshape(N_DEV, -1)[:, :DEPTH * 6 * D_MODEL].reshape(N_DEV, DEPTH, 6 * D_MODEL)
    dmod_mine = lax.dynamic_slice_in_dim(dmod_all, me * cols, cols, axis=2).transpose(1, 0, 2)
    grads["mod_w"] = _modw_grad(c_all, dmod_mine, "mod_w_grad")

    delta, new_m, new_v = {}, {}, {}
    small_names = [n for n, _ in SMALL]
    for name in WEIGHTS:
        if name not in small_names:
            delta[name], new_m[name], new_v[name] = _adamw_nd(params[name], grads[name], mom1[name], mom2[name], f"adamw_{name}")
    packed = [_pack_rows([d[n] for n in small_names], LANE) for d in (params, grads, mom1, mom2)]
    outs = [o.reshape(-1) for o in _adamw(*packed, "adamw_small")]
    off = 0
    for name in small_names:
        size = params[name].size
        for dst, o in zip((delta, new_m, new_v), outs):
            dst[name] = o[off:off + size].reshape(params[name].shape)
        off += size
    return (loss, dx[None], *[grads[n] for n in WEIGHTS], *[delta[n] for n in WEIGHTS],
            *[new_m[n] for n in WEIGHTS], *[new_v[n] for n in WEIGHTS])
```

```python
import functools

import jax
import jax.numpy as jnp
import numpy as np
from jax import lax
from jax.experimental import pallas as pl
from jax.experimental.pallas import tpu as pltpu

F32 = jnp.float32
BF16 = jnp.bfloat16

D_MODEL = 1024
DEPTH = 2
N_DEV = 8
CONV_CH = 512
CONV_WIDTH = 31
CONV_HALO = 32
HG_HEADS = 4
HG_DK = 128
SB_HEADS = 8
SB_DH = 64
D_IN = 7680
D_FF = 4096
EPS = 1e-6
SB_BLK = 128
SB_DEAD = -104.0
SB_FIXED = 3
HG_CHUNK = 128

ADAM_LR = 0.001
ADAM_B1 = 0.9
ADAM_B2 = 0.999
ADAM_EPS = 1e-08
ADAM_WD = 0.01
ADAM_STEP = 10

VMEM_LIMIT = 48 * 1024 * 1024

NN = ((1,), (0,))
NT = ((1,), (1,))
TN = ((0,), (0,))
_DIMS = {"nn": NN, "nt": NT, "tn": TN}


def _sds(shape, dtype):
    return jax.ShapeDtypeStruct(shape, dtype)


def _params(*semantics):
    return pltpu.CompilerParams(dimension_semantics=semantics, vmem_limit_bytes=VMEM_LIMIT)


def _dot(a, b, dims):
    return lax.dot_general(a, b, (dims, ((), ())), preferred_element_type=F32)


@functools.partial(jax.custom_vjp, nondiff_argnums=(2,))
def _bdot(a, b, mode):
    return _dot(a.astype(BF16), b.astype(BF16), _DIMS[mode])


def _bdot_fwd(a, b, mode):
    return _bdot(a, b, mode), (a.astype(BF16), b.astype(BF16))


def _bdot_bwd(mode, res, g):
    a, b = res
    g = g.astype(BF16)
    if mode == "nn":
        return _dot(g, b, NT), _dot(a, g, TN)
    if mode == "nt":
        return _dot(g, b, NN), _dot(g, a, TN)
    return _dot(b, g, NT), _dot(a, g, NN)


_bdot.defvjp(_bdot_fwd, _bdot_bwd)


def _split(x):
    hi = x.astype(BF16)
    lo = (x - hi.astype(F32)).astype(BF16)
    return hi, lo


def _xdot_right(x, m, dims=NN):
    hi, lo = _split(x)
    if dims == NN:
        return _dot(jnp.concatenate([hi, lo], axis=1), jnp.concatenate([m, m], axis=0), NN)
    return _dot(jnp.concatenate([hi, lo], axis=1), jnp.concatenate([m, m], axis=1), NT)


def _xdot_left(m, x, dims=NN):
    hi, lo = _split(x)
    if dims == NN:
        return _dot(jnp.concatenate([m, m], axis=1), jnp.concatenate([hi, lo], axis=0), NN)
    return _dot(jnp.concatenate([m, m], axis=0), jnp.concatenate([hi, lo], axis=0), TN)


@jax.custom_vjp
def _xr(x, m):
    return _xdot_right(x, m)


def _xr_fwd(x, m):
    return _xdot_right(x, m), m


def _xr_bwd(m, g):
    return _xdot_right(g, m, NT), jnp.zeros_like(m)


_xr.defvjp(_xr_fwd, _xr_bwd)


def _norm_mod(x, g, sc, sh):
    r = lax.rsqrt(jnp.mean(x * x, axis=-1, keepdims=True) + EPS)
    return x * r * g * (1.0 + sc) + sh


MESH = pl.DeviceIdType.MESH
HBM_SPEC = pl.BlockSpec(memory_space=pltpu.HBM)


def _mesh_place():
    return lax.axis_index("x"), lax.axis_index("y"), lax.axis_index("c")


def _block_of(px, py, pc):
    return 4 * px + 2 * py + pc


def _sem_scratch(n):
    return [pltpu.SemaphoreType.DMA((n, N_DEV - 1)), pltpu.SemaphoreType.DMA((n, N_DEV - 1)), pltpu.SemaphoreType.DMA((n,))]


class _GatherPlan:
    def __init__(self, xs):
        self.xs = list(xs)
        self.n = len(self.xs)
        self.out_shape = [_sds((N_DEV, *v.shape), v.dtype) for v in self.xs]
        self.scratch = _sem_scratch(self.n)

    def _parts(self, x_refs, out_refs, sems):
        send_sems, recv_sems, local_sems = sems
        x, y, c = _mesh_place()
        me, sibling = (x, y, c), (x, y, 1 - c)
        chips = [(1 - x, y), (x, 1 - y), (1 - x, 1 - y)]

        def copy(a, k, block, to, src=None):
            rows = out_refs[a].at[_block_of(*block)]
            return pltpu.make_async_remote_copy(
                src_ref=rows if src is None else src, dst_ref=rows, send_sem=send_sems.at[a, k],
                recv_sem=recv_sems.at[a, k], device_id=to, device_id_type=MESH)

        local = [pltpu.make_async_copy(x_refs[a], out_refs[a].at[_block_of(*me)], local_sems.at[a])
                 for a in range(self.n)]
        first = []
        for a in range(self.n):
            first.append(copy(a, 0, me, sibling, src=x_refs[a]))
            first += [copy(a, 1 + j, me, (*chip, c), src=x_refs[a]) for j, chip in enumerate(chips)]
        return me, sibling, chips, c, copy, local, first

    def start(self, x_refs, out_refs, sems):
        *_, local, first = self._parts(x_refs, out_refs, sems)
        for cp in local + first:
            cp.start()

    def finish(self, x_refs, out_refs, sems):
        me, sibling, chips, c, copy, local, first = self._parts(x_refs, out_refs, sems)
        passed = []
        for j, chip in enumerate(chips):
            for a in range(self.n):
                copy(a, 1 + j, (*chip, c), me).wait_recv()
                fwd = copy(a, 4 + j, (*chip, c), sibling)
                fwd.start()
                passed.append(fwd)
        for a in range(self.n):
            copy(a, 0, sibling, me).wait_recv()
            for j, chip in enumerate(chips):
                copy(a, 4 + j, (*chip, 1 - c), me).wait_recv()
        for cp in first + passed:
            cp.wait_send()
        for cp in local:
            cp.wait()


class _ExchangePlan:
    def __init__(self, xs):
        self.xs = list(xs)
        self.n = len(self.xs)
        self.out_shape = [_sds(v.shape, v.dtype) for v in self.xs]
        self.scratch = _sem_scratch(self.n)

    def _parts(self, in_refs, out_refs, sems):
        send_sems, recv_sems, local_sems = sems
        x, y, c = _mesh_place()
        mine = _block_of(x, y, c)
        peers = [(1 - x if k & 4 else x, 1 - y if k & 2 else y, 1 - c if k & 1 else c) for k in range(1, N_DEV)]

        def copy(a, k, slot_src, slot_dst):
            return pltpu.make_async_remote_copy(
                src_ref=in_refs[a].at[slot_src], dst_ref=out_refs[a].at[slot_dst], send_sem=send_sems.at[a, k],
                recv_sem=recv_sems.at[a, k], device_id=peers[k], device_id_type=MESH)

        local = [pltpu.make_async_copy(in_refs[a].at[mine], out_refs[a].at[mine], local_sems.at[a])
                 for a in range(self.n)]
        sends = [copy(a, k, _block_of(*peers[k]), mine) for a in range(self.n) for k in range(N_DEV - 1)]
        arrivals = [copy(a, k, _block_of(*peers[k]), _block_of(*peers[k])) for a in range(self.n) for k in range(N_DEV - 1)]
        return local, sends, arrivals

    def start(self, in_refs, out_refs, sems):
        local, sends, _ = self._parts(in_refs, out_refs, sems)
        for cp in local + sends:
            cp.start()

    def finish(self, in_refs, out_refs, sems):
        local, sends, arrivals = self._parts(in_refs, out_refs, sems)
        for cp in arrivals:
            cp.wait_recv()
        for cp in sends:
            cp.wait_send()
        for cp in local:
            cp.wait()


def _call(body, args, *, grid, in_specs, out_specs, out_shape, scratch_shapes=(), semantics, name, comm=None):
    if comm is None:
        return pl.pallas_call(
            body, grid=grid, in_specs=list(in_specs), out_specs=list(out_specs), out_shape=list(out_shape),
            scratch_shapes=list(scratch_shapes), compiler_params=_params(*semantics), name=name)(*args)
    n_in, n_out, n_scr, n = len(in_specs), len(out_specs), len(scratch_shapes), comm.n

    def hosted(*refs):
        ins, rest = refs[:n_in], refs[n_in:]
        cin, rest = rest[:n], rest[n:]
        outs, rest = rest[:n_out], rest[n_out:]
        cout, rest = rest[:n], rest[n:]
        scr, sems = rest[:n_scr], rest[n_scr:]
        pids = [pl.program_id(d) for d in range(len(grid))]
        first = functools.reduce(jnp.logical_and, [p == 0 for p in pids])
        last = functools.reduce(jnp.logical_and, [p == g - 1 for p, g in zip(pids, grid)])

        @pl.when(first)
        def _():
            comm.start(cin, cout, sems)

        body(*ins, *outs, *scr)

        @pl.when(last)
        def _():
            comm.finish(cin, cout, sems)

    res = pl.pallas_call(
        hosted, grid=grid, in_specs=list(in_specs) + [HBM_SPEC] * n, out_specs=list(out_specs) + [HBM_SPEC] * n,
        out_shape=list(out_shape) + comm.out_shape, scratch_shapes=list(scratch_shapes) + comm.scratch,
        compiler_params=_params(*["arbitrary"] * len(grid)), name=name)(*args, *comm.xs)
    return res[:n_out], res[n_out:]


def _comm_alone(comm, name):
    def body(*refs):
        n = comm.n
        comm.start(refs[:n], refs[n:2 * n], refs[2 * n:])
        comm.finish(refs[:n], refs[n:2 * n], refs[2 * n:])

    return pl.pallas_call(
        body, in_specs=[HBM_SPEC] * comm.n, out_specs=[HBM_SPEC] * comm.n, out_shape=comm.out_shape,
        scratch_shapes=comm.scratch, name=name)(*comm.xs)


def _matmul(a, b, mode, out_dtype, tm, tn, tk, name, comm=None):
    if mode == "nn":
        (m, k), (_, n) = a.shape, b.shape
    elif mode == "nt":
        (m, k), (n, _) = a.shape, b.shape
    else:
        (k, m), (_, n) = a.shape, b.shape
    tm, tn, tk = min(tm, m), min(tn, n), min(tk, k)
    assert m % tm == 0 and n % tn == 0 and k % tk == 0, (name, m, n, k, tm, tn, tk)
    nk = k // tk
    dims = _DIMS[mode]

    def body(a_ref, b_ref, o_ref, acc_ref):
        if nk == 1:
            o_ref[...] = _dot(a_ref[...], b_ref[...], dims).astype(out_dtype)
            return
        kk = pl.program_id(2)

        @pl.when(kk == 0)
        def _():
            acc_ref[...] = _dot(a_ref[...], b_ref[...], dims)

        @pl.when((kk > 0) & (kk < nk - 1))
        def _():
            acc_ref[...] += _dot(a_ref[...], b_ref[...], dims)

        @pl.when(kk == nk - 1)
        def _():
            o_ref[...] = (acc_ref[...] + _dot(a_ref[...], b_ref[...], dims)).astype(out_dtype)

    if mode == "tn":
        a_spec = pl.BlockSpec((tk, tm), lambda i, j, kk: (kk, i))
        b_spec = pl.BlockSpec((tk, tn), lambda i, j, kk: (kk, j))
    elif mode == "nn":
        a_spec = pl.BlockSpec((tm, tk), lambda i, j, kk: (i, kk))
        b_spec = pl.BlockSpec((tk, tn), lambda i, j, kk: (kk, j))
    else:
        a_spec = pl.BlockSpec((tm, tk), lambda i, j, kk: (i, kk))
        b_spec = pl.BlockSpec((tn, tk), lambda i, j, kk: (j, kk))
    res = _call(
        body, (a, b), grid=(m // tm, n // tn, nk), in_specs=[a_spec, b_spec],
        out_specs=[pl.BlockSpec((tm, tn), lambda i, j, kk: (i, j))],
        out_shape=[_sds((m, n), out_dtype)], scratch_shapes=[pltpu.VMEM((tm, tn), F32)],
        semantics=("parallel", "parallel", "arbitrary"), name=name, comm=comm)
    return res[0] if comm is None else (res[0][0], res[1])


ROW_T = 512


def _prenorm(x, pv, name):
    s, d = x.shape
    t = min(ROW_T, s)

    def body(x_ref, pv_ref, h_ref):
        h = _norm_mod(x_ref[...], pv_ref[6:7, :], pv_ref[1:2, :], pv_ref[0:1, :])
        h_ref[...] = h.astype(BF16)

    return pl.pallas_call(
        body, grid=(s // t,),
        in_specs=[pl.BlockSpec((t, d), lambda i: (i, 0)), pl.BlockSpec((16, d), lambda i: (0, 0))],
        out_specs=pl.BlockSpec((t, d), lambda i: (i, 0)), out_shape=_sds((s, d), BF16),
        compiler_params=_params("parallel"), name=name)(x, pv)


def _prenorm_bwd(dh, dres, x, pv, name):
    s, d = x.shape
    t = min(ROW_T, s)

    def body(dh_ref, dres_ref, x_ref, pv_ref, dx_ref, sg_ref):
        i = pl.program_id(0)

        @pl.when(i == 0)
        def _():
            sg_ref[...] = jnp.zeros_like(sg_ref)

        _, vjp = jax.vjp(_norm_mod, x_ref[...], pv_ref[6:7, :], pv_ref[1:2, :], pv_ref[0:1, :])
        dx, dg, dsc, dsh = vjp(dh_ref[...])
        dx_ref[...] = dres_ref[...] + dx
        sg_ref[0:1, :] += dsh
        sg_ref[1:2, :] += dsc
        sg_ref[2:3, :] += dg

    row = pl.BlockSpec((t, d), lambda i: (i, 0))
    return pl.pallas_call(
        body, grid=(s // t,),
        in_specs=[row, row, row, pl.BlockSpec((16, d), lambda i: (0, 0))],
        out_specs=[row, pl.BlockSpec((8, d), lambda i: (0, 0))],
        out_shape=[_sds((s, d), F32), _sds((8, d), F32)],
        compiler_params=_params("arbitrary"), name=name)(dh, dres, x, pv)


CONV_T = 256


def _conv_tile(a_ext, g_ext, w, b, ln_g, ln_b, n_out):
    u0 = a_ext * jax.nn.sigmoid(g_ext)
    off = CONV_HALO - (CONV_WIDTH - 1)
    acc = jnp.zeros((n_out, u0.shape[1]), F32) + b
    for r in range(8):
        taps = [k for k in range(CONV_WIDTH) if (off + k) % 8 == r]
        rows = n_out if r == 0 else n_out + 8
        part = None
        for k in taps:
            lo = (off + k) // 8 * 8
            term = w[k:k + 1, :] * u0[lo: lo + rows, :]
            part = term if part is None else part + term
        acc = acc + part[r: r + n_out, :]
    mu = jnp.mean(acc, axis=-1, keepdims=True)
    var = jnp.mean(jnp.square(acc - mu), axis=-1, keepdims=True)
    y = (acc - mu) * lax.rsqrt(var + EPS) * ln_g + ln_b
    return y * jax.nn.sigmoid(y)


def _conv_fwd(proj, conv_w, cp, name):
    s = proj.shape[0]
    t = min(CONV_T, s)
    c, h = CONV_CH, CONV_HALO

    def body(ap_ref, ac_ref, gp_ref, gc_ref, w_ref, cp_ref, o_ref):
        i = pl.program_id(0)
        live = (i > 0).astype(F32)
        a_ext = jnp.concatenate([ap_ref[t - h:, :] * live, ac_ref[...]], axis=0)
        g_ext = jnp.concatenate([gp_ref[t - h:, :], gc_ref[...]], axis=0)
        u = _conv_tile(a_ext, g_ext, w_ref[...], cp_ref[0:1, :], cp_ref[1:2, :], cp_ref[2:3, :], t)
        o_ref[...] = u.astype(BF16)

    prev = lambda col: pl.BlockSpec((t, c), lambda i: (jnp.maximum(i - 1, 0), col))
    cur = lambda col: pl.BlockSpec((t, c), lambda i: (i, col))
    return pl.pallas_call(
        body, grid=(s // t,),
        in_specs=[prev(0), cur(0), prev(1), cur(1),
                  pl.BlockSpec((CONV_WIDTH, c), lambda i: (0, 0)), pl.BlockSpec((8, c), lambda i: (0, 0))],
        out_specs=pl.BlockSpec((t, c), lambda i: (i, 0)), out_shape=_sds((s, c), BF16),
        compiler_params=_params("parallel"), name=name)(proj, proj, proj, proj, conv_w, cp)


def _conv_bwd(proj, do, conv_w, cp, name, comm=None):
    s = proj.shape[0]
    t = min(CONV_T, s)
    c, h = CONV_CH, CONV_HALO
    nt = s // t

    def body(ap_ref, ac_ref, an_ref, gp_ref, gc_ref, gn_ref, doc_ref, don_ref, w_ref, cp_ref,
             da_ref, dg_ref, dw_ref, sg_ref):
        i = pl.program_id(0)

        @pl.when(i == 0)
        def _():
            dw_ref[...] = jnp.zeros_like(dw_ref)
            sg_ref[...] = jnp.zeros_like(sg_ref)

        first = (i > 0).astype(F32)
        last = (i < nt - 1).astype(F32)
        a_ext = jnp.concatenate([ap_ref[t - h:, :] * first, ac_ref[...], an_ref[:h, :] * last], axis=0)
        g_ext = jnp.concatenate([gp_ref[t - h:, :], gc_ref[...], gn_ref[:h, :]], axis=0)
        fn = functools.partial(_conv_tile, n_out=t + h)
        _, vjp = jax.vjp(fn, a_ext, g_ext, w_ref[...], cp_ref[0:1, :], cp_ref[1:2, :], cp_ref[2:3, :])
        ct_own = jnp.concatenate([doc_ref[...], jnp.zeros((h, c), F32)], axis=0)
        ct_all = jnp.concatenate([doc_ref[...], don_ref[:h, :] * last], axis=0)
        _, _, dw, db, dlg, dlb = vjp(ct_own)
        da, dg, _, _, _, _ = vjp(ct_all)
        da_ref[...] = da[h:h + t, :].astype(BF16)
        dg_ref[...] = dg[h:h + t, :].astype(BF16)
        dw_ref[...] += dw
        sg_ref[0:1, :] += db
        sg_ref[1:2, :] += dlg
        sg_ref[2:3, :] += dlb

    prev = lambda col: pl.BlockSpec((t, c), lambda i: (jnp.maximum(i - 1, 0), col))
    cur = lambda col: pl.BlockSpec((t, c), lambda i: (i, col))
    nxt = lambda col: pl.BlockSpec((t, c), lambda i: (jnp.minimum(i + 1, nt - 1), col))
    return _call(
        body, (proj, proj, proj, proj, proj, proj, do, do, conv_w, cp), grid=(nt,),
        in_specs=[prev(0), cur(0), nxt(0), prev(1), cur(1), nxt(1), cur(0), nxt(0),
                  pl.BlockSpec((CONV_WIDTH, c), lambda i: (0, 0)), pl.BlockSpec((8, c), lambda i: (0, 0))],
        out_specs=[cur(0), cur(0), pl.BlockSpec((CONV_WIDTH, c), lambda i: (0, 0)),
                   pl.BlockSpec((8, c), lambda i: (0, 0))],
        out_shape=[_sds((s, c), BF16), _sds((s, c), BF16), _sds((CONV_WIDTH, c), F32), _sds((8, c), F32)],
        semantics=("arbitrary",), name=name, comm=comm)


def _hgrn_levels(c):
    out, m = [], c // 2
    while m >= 1:
        out.append(m)
        m //= 2
    return out


def _hgrn_consts(c):
    t = np.arange(c)[:, None]
    j = np.arange(c)[None, :]
    mats = [j <= t, j > t]
    for m in _hgrn_levels(c):
        same = (t // m) == (j // m)
        mats += [same & (j <= t), same & (j > t)]
    return jnp.asarray(np.concatenate(mats, axis=0).astype(np.float32), dtype=BF16)


@jax.custom_vjp
def _cums(lc, mall):
    c = lc.shape[0]
    full = _xdot_left(mall, lc)
    return tuple(full[i * c:(i + 1) * c, :] for i in range(mall.shape[0] // c))


def _cums_fwd(lc, mall):
    return _cums(lc, mall), mall


def _cums_bwd(mall, cts):
    return _xdot_left(mall, jnp.concatenate(cts, axis=0), TN), jnp.zeros_like(mall)


_cums.defvjp(_cums_fwd, _cums_bwd)


def _hgrn_chunk(q, f, v, g, lbs, ng, st_in, mall):
    c = q.shape[0]
    keep = jax.nn.sigmoid(-f)
    if lbs:
        keep = (1.0 - jax.nn.sigmoid(lbs[1] - lbs[0])) * keep
    lc = jnp.log1p(-keep)
    qs = q * jax.nn.sigmoid(q)
    cs = _cums(lc, mall)
    o = _bdot(qs * jnp.exp(cs[0]), st_in, "nt")
    total = jnp.sum(lc, axis=0, keepdims=True)
    st_out = st_in * jnp.exp(total) + _bdot(v, keep * jnp.exp(cs[1]), "tn")
    r = lax.broadcasted_iota(jnp.int32, q.shape, 0)
    tt = lax.broadcasted_iota(jnp.int32, (c, c), 0)
    ss = lax.broadcasted_iota(jnp.int32, (c, c), 1)
    sc = jnp.where(tt == ss, jnp.sum(qs * keep, axis=-1, keepdims=True), 0.0)
    for li, m in enumerate(_hgrn_levels(c)):
        lg = m.bit_length() - 1
        odd = ((r >> lg) & 1) == 1
        qm = jnp.where(odd, qs * jnp.exp(cs[2 + 2 * li]), 0.0)
        km = jnp.where(odd, 0.0, keep * jnp.exp(cs[3 + 2 * li]))
        pair = (((tt >> lg) & 1) == 1) & ((ss >> lg) == (tt >> lg) - 1)
        sc = sc + jnp.where(pair, _bdot(qm, km, "nt"), 0.0)
    o = o + _bdot(sc, v, "nn")
    on = o * lax.rsqrt(jnp.mean(o * o, axis=-1, keepdims=True) + EPS) * ng
    return on * (g * jax.nn.sigmoid(g)), st_out


def _hgrn_fwd(proj, lb, ng, name, comm=None):
    s = proj.shape[0]
    c = HG_CHUNK
    nc = s // c
    mall = _hgrn_consts(c)
    col0 = 1024 // (HG_HEADS * HG_DK)

    def body(*refs):
        q_ref, f_ref, v_ref, g_ref = refs[:4]
        if lb is None:
            ng_ref, m_ref, y_ref, st_ref, scr = refs[4:]
        else:
            lb_ref, ng_ref, m_ref, y_ref, st_ref, scr = refs[4:]
        ci = pl.program_id(0)

        @pl.when(ci == 0)
        def _():
            scr[...] = jnp.zeros_like(scr)

        mall_v = m_ref[...]
        for h in range(HG_HEADS):
            hs = slice(h * HG_DK, (h + 1) * HG_DK)
            lbs = () if lb is None else (lb_ref[0:1, hs], lb_ref[1:2, hs])
            st_in = scr[h]
            st_ref[h] = st_in
            y, st_out = _hgrn_chunk(q_ref[:, hs], f_ref[:, hs], v_ref[:, hs], g_ref[:, hs], lbs, ng_ref[...], st_in, mall_v)
            y_ref[:, hs] = y.astype(BF16)
            scr[h] = st_out

    w = HG_HEADS * HG_DK
    col = lambda k: pl.BlockSpec((c, w), lambda ci: (ci, col0 + k))
    in_specs = [col(0), col(1), col(2), col(3)]
    args = [proj, proj, proj, proj]
    if lb is not None:
        in_specs.append(pl.BlockSpec((2, w), lambda ci: (0, 0)))
        args.append(lb)
    in_specs += [pl.BlockSpec((1, HG_DK), lambda ci: (0, 0)), pl.BlockSpec(mall.shape, lambda ci: (0, 0))]
    args += [ng, mall]
    return _call(
        body, args, grid=(nc,), in_specs=in_specs,
        out_specs=[pl.BlockSpec((c, w), lambda ci: (ci, 0)),
                   pl.BlockSpec((HG_HEADS, None, HG_DK, HG_DK), lambda ci: (0, ci, 0, 0))],
        out_shape=[_sds((s, w), BF16), _sds((HG_HEADS, nc, HG_DK, HG_DK), F32)],
        scratch_shapes=[pltpu.VMEM((HG_HEADS, HG_DK, HG_DK), F32)],
        semantics=("arbitrary",), name=name, comm=comm)


def _hgrn_bwd(proj, states, dy, lb, ng, name, comm=None):
    s = proj.shape[0]
    c = HG_CHUNK
    nc = s // c
    mall = _hgrn_consts(c)
    col0 = 1024 // (HG_HEADS * HG_DK)

    def body(*refs):
        q_ref, f_ref, v_ref, g_ref, st_ref, dy_ref = refs[:6]
        if lb is None:
            ng_ref, m_ref, dq_ref, df_ref, dv_ref, dg_ref, dlb_ref, dng_ref, scr = refs[6:]
        else:
            lb_ref, ng_ref, m_ref, dq_ref, df_ref, dv_ref, dg_ref, dlb_ref, dng_ref, scr = refs[6:]
        ci = pl.program_id(0)

        @pl.when(ci == 0)
        def _():
            scr[...] = jnp.zeros_like(scr)
            dlb_ref[...] = jnp.zeros_like(dlb_ref)
            dng_ref[...] = jnp.zeros_like(dng_ref)

        mall_v = m_ref[...]
        fn = lambda q, f, v, g, lbs_, ng_, st: _hgrn_chunk(q, f, v, g, lbs_, ng_, st, mall_v)
        for h in range(HG_HEADS):
            hs = slice(h * HG_DK, (h + 1) * HG_DK)
            lbs = () if lb is None else (lb_ref[0:1, hs], lb_ref[1:2, hs])
            _, vjp = jax.vjp(fn, q_ref[:, hs], f_ref[:, hs], v_ref[:, hs], g_ref[:, hs], lbs, ng_ref[...], st_ref[h])
            dq, df, dv, dg, dlbs, dng, dst = vjp((dy_ref[:, hs], scr[h]))
            dq_ref[:, hs] = dq.astype(BF16)
            df_ref[:, hs] = df.astype(BF16)
            dv_ref[:, hs] = dv.astype(BF16)
            dg_ref[:, hs] = dg.astype(BF16)
            scr[h] = dst
            dng_ref[0:1, :] += dng
            if lbs:
                dlb_ref[0:1, hs] += dlbs[0]
                dlb_ref[1:2, hs] += dlbs[1]

    w = HG_HEADS * HG_DK
    rev = lambda ci: nc - 1 - ci
    col = lambda k: pl.BlockSpec((c, w), lambda ci: (rev(ci), col0 + k))
    out_col = pl.BlockSpec((c, w), lambda ci: (rev(ci), 0))
    in_specs = [col(0), col(1), col(2), col(3),
                pl.BlockSpec((HG_HEADS, None, HG_DK, HG_DK), lambda ci: (0, rev(ci), 0, 0)), out_col]
    args = [proj, proj, proj, proj, states, dy]
    if lb is not None:
        in_specs.append(pl.BlockSpec((2, w), lambda ci: (0, 0)))
        args.append(lb)
    in_specs += [pl.BlockSpec((1, HG_DK), lambda ci: (0, 0)), pl.BlockSpec(mall.shape, lambda ci: (0, 0))]
    args += [ng, mall]
    return _call(
        body, args, grid=(nc,), in_specs=in_specs,
        out_specs=[out_col, out_col, out_col, out_col,
                   pl.BlockSpec((2, w), lambda ci: (0, 0)), pl.BlockSpec((8, HG_DK), lambda ci: (0, 0))],
        out_shape=[_sds((s, w), BF16)] * 4 + [_sds((2, w), F32), _sds((8, HG_DK), F32)],
        scratch_shapes=[pltpu.VMEM((HG_HEADS, HG_DK, HG_DK), F32)],
        semantics=("arbitrary",), name=name, comm=comm)


def _head_avg():
    w = SB_HEADS * SB_DH
    i = np.arange(w)
    return jnp.asarray(((i[:, None] // SB_DH) == (i[None, :] // SB_DH)).astype(np.float32) / SB_DH, dtype=BF16)


def _sb_norm(x, g_tiled, avg):
    ms = _xr(x * x, avg)
    return x * lax.rsqrt(ms + EPS) * g_tiled


def _sb_prep(proj, gq, gk, name):
    s = proj.shape[0]
    t = min(ROW_T, s)
    w = SB_HEADS * SB_DH
    avg = _head_avg()

    def body(q_ref, k_ref, v_ref, gq_ref, gk_ref, avg_ref, qn_ref, kn_ref, vb_ref):
        qn_ref[...] = _sb_norm(q_ref[...], gq_ref[...], avg_ref[...]).astype(BF16)
        kn_ref[...] = _sb_norm(k_ref[...], gk_ref[...], avg_ref[...]).astype(BF16)
        vb_ref[...] = v_ref[...].astype(BF16)

    col = lambda k: pl.BlockSpec((t, w), lambda i: (i, 6 + k))
    vec = pl.BlockSpec((1, w), lambda i: (0, 0))
    out = pl.BlockSpec((t, w), lambda i: (i, 0))
    return pl.pallas_call(
        body, grid=(s // t,), in_specs=[col(0), col(1), col(2), vec, vec, pl.BlockSpec((w, w), lambda i: (0, 0))],
        out_specs=[out, out, out], out_shape=[_sds((s, w), BF16)] * 3,
        compiler_params=_params("parallel"), name=name)(proj, proj, proj, gq, gk, avg)


def _sb_prep_bwd(proj, dqn, dkn, gq, gk, name):
    s = proj.shape[0]
    t = min(ROW_T, s)
    w = SB_HEADS * SB_DH
    avg = _head_avg()

    def body(q_ref, k_ref, dqn_ref, dkn_ref, gq_ref, gk_ref, avg_ref, dq_ref, dk_ref, sg_ref):
        i = pl.program_id(0)

        @pl.when(i == 0)
        def _():
            sg_ref[...] = jnp.zeros_like(sg_ref)

        avg_v = avg_ref[...]
        fn = lambda x, g: _sb_norm(x, g, avg_v)
        _, vq = jax.vjp(fn, q_ref[...], gq_ref[...])
        dq, dgq = vq(dqn_ref[...])
        _, vk = jax.vjp(fn, k_ref[...], gk_ref[...])
        dk, dgk = vk(dkn_ref[...])
        dq_ref[...] = dq.astype(BF16)
        dk_ref[...] = dk.astype(BF16)
        sg_ref[0:1, :] += dgq
        sg_ref[1:2, :] += dgk

    col = lambda k: pl.BlockSpec((t, w), lambda i: (i, 6 + k))
    vec = pl.BlockSpec((1, w), lambda i: (0, 0))
    row = pl.BlockSpec((t, w), lambda i: (i, 0))
    return pl.pallas_call(
        body, grid=(s // t,),
        in_specs=[col(0), col(1), row, row, vec, vec, pl.BlockSpec((w, w), lambda i: (0, 0))],
        out_specs=[row, row, pl.BlockSpec((8, w), lambda i: (0, 0))],
        out_shape=[_sds((s, w), BF16), _sds((s, w), BF16), _sds((8, w), F32)],
        compiler_params=_params("arbitrary"), name=name)(proj, proj, dqn, dkn, gq, gk, avg)


def _sb_tri(kind):
    j = np.arange(SB_BLK)[:, None]
    s = np.arange(SB_BLK)[None, :]
    tri = (j > s) if kind == "suffix" else (j < s)
    return jnp.asarray(np.concatenate([tri, np.ones_like(tri)], axis=1).astype(np.float32), dtype=BF16)


def _sb_scores(qm, kblk, mask):
    z = _dot(qm, kblk, NT) * (SB_DH ** -0.5)
    sp = jnp.maximum(z, 0.0) + jnp.log(1.0 + jnp.exp(-jnp.abs(z)))
    return z, sp, jnp.where(mask, -sp, 0.0)


def _sb_setup(b):
    lane = lax.broadcasted_iota(jnp.int32, (2 * b, b), 1)
    row = lax.broadcasted_iota(jnp.int32, (2 * b, b), 0)
    mine = (row >> (b.bit_length() - 1)) == (lane >> (SB_DH.bit_length() - 1))
    return lane, row & (b - 1), mine


def _sb_fwd(qn, kn, vb, name, comm=None):
    s, w = qn.shape
    b = SB_BLK
    nq = s // b
    tri = _sb_tri("suffix")

    def body(q_ref, k_ref, v_ref, tri_ref, o_ref):
        i = pl.program_id(1)
        lane, tt, mine = _sb_setup(b)
        q = q_ref[...]
        q2 = jnp.concatenate([q, q], axis=0)
        qm = jnp.where(mine, q2, jnp.zeros_like(q2))
        tri_v = tri_ref[...]

        def block(kb, lim, run, acc):
            off = pl.multiple_of(kb * b, b)
            kblk = k_ref[pl.ds(off, b), :]
            vblk = v_ref[pl.ds(off, b), :]
            mask = lane < lim
            z, sp, lk = _sb_scores(qm, kblk, mask)
            both = _xdot_right(lk, tri_v)
            a = jnp.where(mask, jnp.exp(z - sp + both[:, :b] + run), 0.0)
            return run + both[:, b:], acc + _dot(a.astype(BF16), vblk, NN)

        offs = [pl.multiple_of(jnp.maximum(i - j, 0) * b, b) for j in range(SB_FIXED)]
        masks = [lane < (tt if j == 0 else jnp.where(i >= j, b, 0)) for j in range(SB_FIXED)]
        scores = [_sb_scores(qm, k_ref[pl.ds(off, b), :], m) for off, m in zip(offs, masks)]
        boths = [_xdot_right(lk, tri_v) for _, _, lk in scores]
        run = acc = jnp.zeros((2 * b, b), F32)
        for j in range(SB_FIXED):
            z, sp, _ = scores[j]
            a = jnp.where(masks[j], jnp.exp(z - sp + boths[j][:, :b] + run), 0.0)
            acc = acc + _dot(a.astype(BF16), v_ref[pl.ds(offs[j], b), :], NN)
            run = run + boths[j][:, b:]

        def cond(carry):
            j, run_, _ = carry
            return (j <= i) & (jnp.max(run_) > SB_DEAD)

        def step(carry):
            j, run_, acc_ = carry
            run_, acc_ = block(i - j, b, run_, acc_)
            return j + 1, run_, acc_

        _, _, acc = lax.while_loop(cond, step, (jnp.int32(SB_FIXED), run, acc))
        o_ref[...] = jnp.where(lane[:b] < SB_DH, acc[:b], acc[b:]).astype(BF16)

    blk = pl.BlockSpec((b, b), lambda p, i: (i, p))
    full = pl.BlockSpec((s, b), lambda p, i: (0, p))
    return _call(
        body, (qn, kn, vb, tri), grid=(w // b, nq),
        in_specs=[blk, full, full, pl.BlockSpec(tri.shape, lambda p, i: (0, 0))],
        out_specs=[blk], out_shape=[_sds((s, w), BF16)],
        semantics=("parallel", "arbitrary"), name=name, comm=comm)


def _sb_bwd(qn, kn, vb, do, name, comm=None):
    s, w = qn.shape
    b = SB_BLK
    nq = s // b
    tri_s = _sb_tri("suffix")
    tri_p = _sb_tri("prefix")
    scale = SB_DH ** -0.5

    def body(q_ref, k_ref, v_ref, do_ref, ts_ref, tp_ref, dq_ref, dk_ref, dv_ref, dk_acc, dv_acc, dp_scr):
        i = pl.program_id(1)

        @pl.when(i == 0)
        def _():
            dk_acc[...] = jnp.zeros_like(dk_acc)
            dv_acc[...] = jnp.zeros_like(dv_acc)

        lane, tt, mine = _sb_setup(b)
        q = q_ref[...]
        q2 = jnp.concatenate([q, q], axis=0)
        qm = jnp.where(mine, q2, jnp.zeros_like(q2))
        dout = do_ref[...].astype(BF16)
        d2 = jnp.concatenate([dout, dout], axis=0)
        dom = jnp.where(mine, d2, jnp.zeros_like(d2))
        ts_v = ts_ref[...]
        tp_v = tp_ref[...]
        zero = jnp.zeros((2 * b, b), F32)

        def down(kb, lim, run):
            off = pl.multiple_of(kb * b, b)
            kblk = k_ref[pl.ds(off, b), :]
            vblk = v_ref[pl.ds(off, b), :]
            mask = lane < lim
            z, sp, lk = _sb_scores(qm, kblk, mask)
            both = _xdot_right(lk, ts_v)
            a = jnp.where(mask, jnp.exp(z - sp + both[:, :b] + run), 0.0)
            dv_acc[pl.ds(off, b), :] += _dot(a.astype(BF16), dom, TN)
            return _dot(dom, vblk, NT) * a, run + both[:, b:]

        def up(kb, lim, dp, pre, dq):
            off = pl.multiple_of(kb * b, b)
            kblk = k_ref[pl.ds(off, b), :]
            sig = jax.nn.sigmoid(_dot(qm, kblk, NT) * scale)
            both = _xdot_right(dp, tp_v)
            dz = jnp.where(lane < lim, dp * (1.0 - sig) - sig * (both[:, :b] + pre), 0.0) * scale
            dz = dz.astype(BF16)
            dk_acc[pl.ds(off, b), :] += _dot(dz, qm, TN)
            return pre + both[:, b:], dq + _dot(dz, kblk, NN)

        offs = [pl.multiple_of(jnp.maximum(i - j, 0) * b, b) for j in range(SB_FIXED)]
        masks = [lane < (tt if j == 0 else jnp.where(i >= j, b, 0)) for j in range(SB_FIXED)]
        kblks = [k_ref[pl.ds(off, b), :] for off in offs]
        scores = [_sb_scores(qm, kblk, m) for kblk, m in zip(kblks, masks)]
        das = [_dot(dom, v_ref[pl.ds(off, b), :], NT) for off in offs]
        boths = [_xdot_right(lk, ts_v) for _, _, lk in scores]
        run = zero
        dps = []
        for j in range(SB_FIXED):
            z, sp, _ = scores[j]
            a = jnp.where(masks[j], jnp.exp(z - sp + boths[j][:, :b] + run), 0.0)
            dps.append(das[j] * a)
            dv_acc[pl.ds(offs[j], b), :] += _dot(a.astype(BF16), dom, TN)
            run = run + boths[j][:, b:]

        def cond(carry):
            j, run_ = carry
            return (j <= i) & (jnp.max(run_) > SB_DEAD)

        def sweep_down(carry):
            j, run_ = carry
            dp, run_ = down(i - j, b, run_)
            dp_scr[i - j] = dp
            return j + 1, run_

        n_live, _ = lax.while_loop(cond, sweep_down, (jnp.int32(SB_FIXED), run))

        def sweep_up(jj, carry):
            kb = i - n_live + 1 + jj
            return up(kb, b, dp_scr[kb], *carry)

        pre, dq = lax.fori_loop(0, n_live - SB_FIXED, sweep_up, (zero, zero))
        pres = [_xdot_right(dp, tp_v) for dp in dps]
        for j in reversed(range(SB_FIXED)):
            z, sp, _ = scores[j]
            sig = jnp.exp(z - sp)
            dz = jnp.where(masks[j], dps[j] * (1.0 - sig) - sig * (pres[j][:, :b] + pre), 0.0) * scale
            dz = dz.astype(BF16)
            dk_acc[pl.ds(offs[j], b), :] += _dot(dz, qm, TN)
            dq = dq + _dot(dz, kblks[j], NN)
            pre = pre + pres[j][:, b:]
        dq_ref[...] = jnp.where(lane[:b] < SB_DH, dq[:b], dq[b:])

        @pl.when(i == nq - 1)
        def _():
            dk_ref[...] = dk_acc[...]
            dv_ref[...] = dv_acc[...].astype(BF16)

    blk = pl.BlockSpec((b, b), lambda p, i: (i, p))
    full = pl.BlockSpec((s, b), lambda p, i: (0, p))
    tri = pl.BlockSpec(tri_s.shape, lambda p, i: (0, 0))
    return _call(
        body, (qn, kn, vb, do, tri_s, tri_p), grid=(w // b, nq), in_specs=[blk, full, full, blk, tri, tri],
        out_specs=[blk, full, full], out_shape=[_sds((s, w), F32), _sds((s, w), F32), _sds((s, w), BF16)],
        scratch_shapes=[pltpu.VMEM((s, b), F32), pltpu.VMEM((s, b), F32), pltpu.VMEM((nq, 2 * b, b), F32)],
        semantics=("arbitrary", "arbitrary"), name=name, comm=comm)


MIX_T = 256
HALF = 512


def _gate_slices(ga, gb):
    return [(ga[:, 0:512], ga[:, 512:1024]), (ga[:, 1024:1536], gb[:, 0:512]), (gb[:, 512:1024], gb[:, 1024:1536])]


def _mix_fwd(u3, oh, osb, proj, x, pv, wc, wh, ws, wo, name):
    s, d = x.shape
    t = min(MIX_T, s)

    def body(u3_ref, oh_ref, os_ref, ga_ref, gb_ref, x_ref, pv_ref, wc_ref, wh_ref, ws_ref, wo_ref,
             x1_ref, h2_ref, mg_ref, mo_ref):
        ys = [_dot(u3_ref[...], wc_ref[...], NT), _dot(oh_ref[...], wh_ref[...], NT), _dot(os_ref[...], ws_ref[...], NT)]
        gl = _gate_slices(ga_ref[...], gb_ref[...])
        halves = []
        for hf in range(2):
            lo = hf * HALF
            acc = jnp.zeros((t, HALF), F32)
            for br in range(3):
                gate = jax.nn.sigmoid(gl[br][hf] + pv_ref[8 + br:9 + br, lo:lo + HALF])
                acc = acc + gate * ys[br][:, lo:lo + HALF]
            halves.append(acc)
        merged = jnp.concatenate(halves, axis=1).astype(BF16)
        mg_ref[...] = merged
        mo = _dot(merged, wo_ref[...], NN)
        mo_ref[...] = mo.astype(BF16)
        x1 = x_ref[...] + pv_ref[2:3, :] * mo
        x1_ref[...] = x1
        h2_ref[...] = _norm_mod(x1, pv_ref[7:8, :], pv_ref[4:5, :], pv_ref[3:4, :]).astype(BF16)

    br_spec = pl.BlockSpec((t, CONV_CH), lambda i: (i, 0))
    row = pl.BlockSpec((t, d), lambda i: (i, 0))
    wproj = pl.BlockSpec((d, CONV_CH), lambda i: (0, 0))
    return pl.pallas_call(
        body, grid=(s // t,),
        in_specs=[br_spec, br_spec, br_spec, pl.BlockSpec((t, 1536), lambda i: (i, 3)),
                  pl.BlockSpec((t, 1536), lambda i: (i, 4)), row, pl.BlockSpec((16, d), lambda i: (0, 0)),
                  wproj, wproj, wproj, pl.BlockSpec((d, d), lambda i: (0, 0))],
        out_specs=[row, row, row, row],
        out_shape=[_sds((s, d), F32), _sds((s, d), BF16), _sds((s, d), BF16), _sds((s, d), BF16)],
        compiler_params=_params("parallel"), name=name)(u3, oh, osb, proj, proj, x, pv, wc, wh, ws, wo)


def _mix_bwd(dx1, mo1, u3, oh, osb, proj, pv, wc, wh, ws, wo, name):
    s, d = dx1.shape
    t = min(MIX_T, s)

    def body(dx_ref, mo_ref, u3_ref, oh_ref, os_ref, ga_ref, gb_ref, pv_ref, wc_ref, wh_ref, ws_ref, wo_ref,
             dmo_ref, dyc_ref, dyh_ref, dys_ref, doc_ref, doh_ref, dos_ref, dgl_ref, sg_ref):
        i = pl.program_id(0)

        @pl.when(i == 0)
        def _():
            sg_ref[...] = jnp.zeros_like(sg_ref)

        dx = dx_ref[...]
        dmo = (dx * pv_ref[2:3, :]).astype(BF16)
        dmo_ref[...] = dmo
        sg_ref[0:1, :] += jnp.sum(dx * mo_ref[...].astype(F32), axis=0, keepdims=True)
        dmerged = _dot(dmo, wo_ref[...], NT)
        branches = [(u3_ref, wc_ref, dyc_ref, doc_ref), (oh_ref, wh_ref, dyh_ref, doh_ref), (os_ref, ws_ref, dys_ref, dos_ref)]
        gl = _gate_slices(ga_ref[...], gb_ref[...])
        for br, (o_ref, w_ref, dy_ref, do_ref) in enumerate(branches):
            y = _dot(o_ref[...], w_ref[...], NT)
            dys = []
            for hf in range(2):
                lo = hf * HALF
                gate = jax.nn.sigmoid(gl[br][hf] + pv_ref[8 + br:9 + br, lo:lo + HALF])
                dm = dmerged[:, lo:lo + HALF]
                dys.append(dm * gate)
                dgl = dm * y[:, lo:lo + HALF] * gate * (1.0 - gate)
                dgl_ref[:, br * d + lo: br * d + lo + HALF] = dgl.astype(BF16)
                sg_ref[1 + br:2 + br, lo:lo + HALF] += jnp.sum(dgl, axis=0, keepdims=True)
            dy = jnp.concatenate(dys, axis=1).astype(BF16)
            dy_ref[...] = dy
            do_ref[...] = _dot(dy, w_ref[...], NN)

    br_spec = pl.BlockSpec((t, CONV_CH), lambda i: (i, 0))
    row = pl.BlockSpec((t, d), lambda i: (i, 0))
    wproj = pl.BlockSpec((d, CONV_CH), lambda i: (0, 0))
    return pl.pallas_call(
        body, grid=(s // t,),
        in_specs=[row, row, br_spec, br_spec, br_spec, pl.BlockSpec((t, 1536), lambda i: (i, 3)),
                  pl.BlockSpec((t, 1536), lambda i: (i, 4)), pl.BlockSpec((16, d), lambda i: (0, 0)),
                  wproj, wproj, wproj, pl.BlockSpec((d, d), lambda i: (0, 0))],
        out_specs=[row, row, row, row, br_spec, br_spec, br_spec, pl.BlockSpec((t, 3 * d), lambda i: (i, 0)),
                   pl.BlockSpec((8, d), lambda i: (0, 0))],
        out_shape=[_sds((s, d), BF16)] * 4 + [_sds((s, CONV_CH), F32)] * 3 + [_sds((s, 3 * d), BF16), _sds((8, d), F32)],
        compiler_params=_params("arbitrary"), name=name)(dx1, mo1, u3, oh, osb, proj, proj, pv, wc, wh, ws, wo)


MLP_T = 512
MLP_F = 512


def _mlp_fwd(h2, x1, pv, w1t, w2, name):
    s, d = x1.shape
    t = min(MLP_T, s)
    nf = D_FF // MLP_F

    def body(h_ref, x_ref, pv_ref, w1_ref, w2_ref, x2_ref, mo_ref, acc_ref):
        f = pl.program_id(1)

        @pl.when(f == 0)
        def _():
            acc_ref[...] = jnp.zeros_like(acc_ref)

        a = jnp.maximum(_dot(h_ref[...], w1_ref[...], NT), 0.0)
        acc_ref[...] += _dot((a * a).astype(BF16), w2_ref[...], NN)

        @pl.when(f == nf - 1)
        def _():
            mo = acc_ref[...]
            mo_ref[...] = mo.astype(BF16)
            x2_ref[...] = x_ref[...] + pv_ref[5:6, :] * mo

    row = pl.BlockSpec((t, d), lambda i, f: (i, 0))
    wblk = pl.BlockSpec((MLP_F, d), lambda i, f: (f, 0))
    return pl.pallas_call(
        body, grid=(s // t, nf), in_specs=[row, row, pl.BlockSpec((16, d), lambda i, f: (0, 0)), wblk, wblk],
        out_specs=[row, row], out_shape=[_sds((s, d), F32), _sds((s, d), BF16)],
        scratch_shapes=[pltpu.VMEM((t, d), F32)],
        compiler_params=_params("parallel", "arbitrary"), name=name)(h2, x1, pv, w1t, w2)


def _mlp_bwd(dx2, h2, x1, mo2, pv, w1t, w2, name, comm=None):
    s, d = x1.shape
    t = min(MLP_T, s)
    nf = D_FF // MLP_F

    def body(dx_ref, h_ref, x_ref, mo_ref, pv_ref, w1_ref, w2_ref, dx1_ref, da_ref, b_ref, dmo_ref, sg_ref, acc_ref):
        i = pl.program_id(0)
        f = pl.program_id(1)

        @pl.when((i == 0) & (f == 0))
        def _():
            sg_ref[...] = jnp.zeros_like(sg_ref)

        @pl.when(f == 0)
        def _():
            acc_ref[...] = jnp.zeros_like(acc_ref)
            dx = dx_ref[...]
            dmo_ref[...] = (dx * pv_ref[5:6, :]).astype(BF16)
            sg_ref[0:1, :] += jnp.sum(dx * mo_ref[...].astype(F32), axis=0, keepdims=True)

        r = jnp.maximum(_dot(h_ref[...], w1_ref[...], NT), 0.0)
        b_ref[...] = (r * r).astype(BF16)
        da = (_dot(dmo_ref[...], w2_ref[...], NT) * (2.0 * r)).astype(BF16)
        da_ref[...] = da
        acc_ref[...] += _dot(da, w1_ref[...], NN)

        @pl.when(f == nf - 1)
        def _():
            _, vjp = jax.vjp(_norm_mod, x_ref[...], pv_ref[7:8, :], pv_ref[4:5, :], pv_ref[3:4, :])
            dxn, dg, dsc, dsh = vjp(acc_ref[...])
            dx1_ref[...] = dx_ref[...] + dxn
            sg_ref[1:2, :] += dsh
            sg_ref[2:3, :] += dsc
            sg_ref[3:4, :] += dg

    row = pl.BlockSpec((t, d), lambda i, f: (i, 0))
    wblk = pl.BlockSpec((MLP_F, d), lambda i, f: (f, 0))
    hid = pl.BlockSpec((t, MLP_F), lambda i, f: (i, f))
    return _call(
        body, (dx2, h2, x1, mo2, pv, w1t, w2), grid=(s // t, nf),
        in_specs=[row, row, row, row, pl.BlockSpec((16, d), lambda i, f: (0, 0)), wblk, wblk],
        out_specs=[row, hid, hid, row, pl.BlockSpec((8, d), lambda i, f: (0, 0))],
        out_shape=[_sds((s, d), F32), _sds((s, D_FF), BF16), _sds((s, D_FF), BF16), _sds((s, d), BF16), _sds((8, d), F32)],
        scratch_shapes=[pltpu.VMEM((t, d), F32)],
        semantics=("arbitrary", "arbitrary"), name=name, comm=comm)


def _loss_head(y, target, name):
    s, d = y.shape
    t = min(ROW_T, s)

    def body(y_ref, t_ref, dy_ref, ls_ref):
        i = pl.program_id(0)

        @pl.when(i == 0)
        def _():
            ls_ref[...] = jnp.zeros_like(ls_ref)

        e = y_ref[...] - t_ref[...]
        dy_ref[...] = e * (1.0 / d)
        ls_ref[...] += jnp.sum((e * e).reshape(t // 8, 8, d), axis=0)

    row = pl.BlockSpec((t, d), lambda i: (i, 0))
    return pl.pallas_call(
        body, grid=(s // t,), in_specs=[row, row], out_specs=[row, pl.BlockSpec((8, d), lambda i: (0, 0))],
        out_shape=[_sds((s, d), F32), _sds((8, d), F32)],
        compiler_params=_params("arbitrary"), name=name)(y, target)


def _layer_vectors(l, mod, sm):
    d = D_MODEL
    pv = jnp.concatenate([mod[l].reshape(6, d), sm["norm1_g"][l][None], sm["norm2_g"][l][None],
                          sm["gate_b"][l].reshape(3, d), jnp.zeros((5, d), F32)], axis=0)
    cp = jnp.concatenate([sm["conv_b"][l][None], sm["conv_ln_g"][l][None], sm["conv_ln_b"][l][None],
                          jnp.zeros((5, CONV_CH), F32)], axis=0)
    return dict(pv=pv, cp=cp, conv_w=sm["conv_w"][l], lb=(sm["hgrn_lb"] if l > 0 else None),
                ng=sm["hgrn_norm_g"][l][None], gq=jnp.tile(sm["sb_qn_g"][l], SB_HEADS)[None],
                gk=jnp.tile(sm["sb_kn_g"][l], SB_HEADS)[None])


def _hosted(res, comm):
    return res if comm is not None else (res, None)


def _layer_fwd_mixers(x, vec, win_t, tag, comm_proj=None, comm_hgrn=None, comm_sb=None):
    h = _prenorm(x, vec["pv"], f"prenorm{tag}")
    proj, got_proj = _hosted(_matmul(h, win_t, "nt", F32, 1024, 768, 1024, f"proj{tag}", comm_proj), comm_proj)
    u3 = _conv_fwd(proj, vec["conv_w"], vec["cp"], f"conv_fwd{tag}")
    (oh, states), got_hgrn = _hosted(_hgrn_fwd(proj, vec["lb"], vec["ng"], f"hgrn_fwd{tag}", comm_hgrn), comm_hgrn)
    qn, kn, vb = _sb_prep(proj, vec["gq"], vec["gk"], f"sb_prep{tag}")
    (osb,), got_sb = _hosted(_sb_fwd(qn, kn, vb, f"sb_fwd{tag}", comm_sb), comm_sb)
    saved = dict(x=x, h=h, proj=proj, u3=u3, oh=oh, states=states, qn=qn, kn=kn, vb=vb, osb=osb)
    return saved, (got_proj, got_hgrn, got_sb)


def _layer_fwd_out(sv, vec, w, tag):
    x1, h2, merged, mo1 = _mix_fwd(sv["u3"], sv["oh"], sv["osb"], sv["proj"], sv["x"], vec["pv"],
                                   w["wc_t"], w["wh_t"], w["ws_t"], w["wo"], f"mix_fwd{tag}")
    x2, mo2 = _mlp_fwd(h2, x1, vec["pv"], w["w1_t"], w["w2"], f"mlp_fwd{tag}")
    sv.update(x1=x1, h2=h2, merged=merged, mo1=mo1, mo2=mo2)
    return x2


def _layer_bwd(dx2, sv, vec, w, tag, plans=None):
    plans = plans or {}
    got = {}

    def plan_for(key, big_now):
        return plans[key](big_now) if key in plans else None

    pv = vec["pv"]
    big = {}
    comm = plan_for("mlp", big)
    (dx1, da, bsq, dmo2, sg_mlp), got["mlp"] = _hosted(
        _mlp_bwd(dx2, sv["h2"], sv["x1"], sv["mo2"], pv, w["w1_t"], w["w2"], f"mlp_bwd{tag}", comm), comm)
    big["w1_t"] = _matmul(da, sv["h2"], "tn", BF16, 1024, 1024, 1024, f"dw1{tag}")
    big["w2"] = _matmul(bsq, dmo2, "tn", BF16, 1024, 1024, 1024, f"dw2{tag}")
    dmo1, dyc, dyh, dys, doc, doh, dos, dgl, sg_mix = _mix_bwd(
        dx1, sv["mo1"], sv["u3"], sv["oh"], sv["osb"], sv["proj"], pv, w["wc_t"], w["wh_t"], w["ws_t"], w["wo"], f"mix_bwd{tag}")
    big["wo"] = _matmul(sv["merged"], dmo1, "tn", BF16, 1024, 1024, 1024, f"dwo{tag}")
    big["wc_t"] = _matmul(dyc, sv["u3"], "tn", BF16, 1024, 512, 1024, f"dwc{tag}")
    big["wh_t"] = _matmul(dyh, sv["oh"], "tn", BF16, 1024, 512, 1024, f"dwh{tag}")
    big["ws_t"] = _matmul(dys, sv["osb"], "tn", BF16, 1024, 512, 1024, f"dws{tag}")
    comm = plan_for("conv", big)
    (da_c, dg_c, dconv_w, sg_conv), got["conv"] = _hosted(
        _conv_bwd(sv["proj"], doc, vec["conv_w"], vec["cp"], f"conv_bwd{tag}", comm), comm)
    comm = plan_for("hgrn", big)
    (dq_h, df_h, di_h, dg_h, dlb, dng), got["hgrn"] = _hosted(
        _hgrn_bwd(sv["proj"], sv["states"], doh, vec["lb"], vec["ng"], f"hgrn_bwd{tag}", comm), comm)
    comm = plan_for("sb", big)
    (dqn, dkn, dv_s), got["sb"] = _hosted(_sb_bwd(sv["qn"], sv["kn"], sv["vb"], dos, f"sb_bwd{tag}", comm), comm)
    dq_s, dk_s, sg_sb = _sb_prep_bwd(sv["proj"], dqn, dkn, vec["gq"], vec["gk"], f"sb_prep_bwd{tag}")
    dproj = jnp.concatenate([da_c, dg_c, dq_h, df_h, di_h, dg_h, dq_s, dk_s, dv_s, dgl], axis=1)
    half = D_MODEL // 2
    big["win_a"] = _matmul(dproj, sv["h"][:, :half], "tn", BF16, 768, half, 1024, f"dwin_a{tag}")
    comm = plan_for("dwin", big)
    big["win_b"], got["dwin"] = _hosted(
        _matmul(dproj, sv["h"][:, half:], "tn", BF16, 768, half, 1024, f"dwin_b{tag}", comm), comm)
    comm = plan_for("dh", big)
    dh, got["dh"] = _hosted(_matmul(dproj, w["win_t"], "nn", F32, 512, 1024, 1920, f"dh{tag}", comm), comm)
    dx, sg_pre = _prenorm_bwd(dh, dx1, sv["x"], pv, f"prenorm_bwd{tag}")
    small = dict(
        mod=jnp.stack([sg_pre[0], sg_pre[1], sg_mix[0], sg_mlp[1], sg_mlp[2], sg_mlp[0]]).reshape(6 * D_MODEL),
        norm1_g=sg_pre[2], norm2_g=sg_mlp[3], gate_b=sg_mix[1:4].reshape(3 * D_MODEL),
        conv_w=dconv_w, conv_b=sg_conv[0], conv_ln_g=sg_conv[1], conv_ln_b=sg_conv[2],
        hgrn_lb=dlb, hgrn_norm_g=dng[0],
        sb_qn_g=sg_sb[0].reshape(SB_HEADS, SB_DH).sum(0), sb_kn_g=sg_sb[1].reshape(SB_HEADS, SB_DH).sum(0))
    return dx, big, small, got


def _row_tile(r, cap=512):
    t = min(r, cap)
    while r % t or (t % 8 and t != r):
        t -= 1
    return t


def _sum8(z, name):
    _, r, c = z.shape
    t = _row_tile(r, 128 if c >= 1024 else 512)

    def body(z_ref, o_ref):
        acc = z_ref[0].astype(F32)
        for j in range(1, N_DEV):
            acc = acc + z_ref[j].astype(F32)
        o_ref[...] = acc

    return pl.pallas_call(
        body, grid=(r // t,), in_specs=[pl.BlockSpec((N_DEV, t, c), lambda i: (0, i, 0))],
        out_specs=pl.BlockSpec((t, c), lambda i: (i, 0)), out_shape=_sds((r, c), F32),
        compiler_params=_params("parallel"), name=name)(z)


def _adamw(w, g, m, v, name):
    r, c = w.shape
    t = _row_tile(r, 256)

    def body(w_ref, g_ref, m_ref, v_ref, d_ref, nm_ref, nv_ref):
        g_ = g_ref[...]
        nm = ADAM_B1 * m_ref[...] + (1.0 - ADAM_B1) * g_
        nv = ADAM_B2 * v_ref[...] + (1.0 - ADAM_B2) * jnp.square(g_)
        m_hat = nm / (1.0 - ADAM_B1 ** ADAM_STEP)
        v_hat = nv / (1.0 - ADAM_B2 ** ADAM_STEP)
        d_ref[...] = -ADAM_LR * (m_hat / (jnp.sqrt(v_hat) + ADAM_EPS) + ADAM_WD * w_ref[...])
        nm_ref[...] = nm
        nv_ref[...] = nv

    blk = pl.BlockSpec((t, c), lambda i: (i, 0))
    return pl.pallas_call(
        body, grid=(r // t,), in_specs=[blk] * 4, out_specs=[blk] * 3, out_shape=[_sds((r, c), F32)] * 3,
        compiler_params=_params("parallel"), name=name)(w, g, m, v)


def _mod_local(c_all, mod_w, name):
    depth, d, cols = mod_w.shape

    def body(c_ref, w_ref, o_ref):
        cv = c_ref[...]
        act = cv * jax.nn.sigmoid(cv)
        o_ref[...] = jnp.dot(act, w_ref[...], precision=lax.Precision.HIGHEST, preferred_element_type=F32)

    return pl.pallas_call(
        body, grid=(depth,),
        in_specs=[pl.BlockSpec((N_DEV, d), lambda l: (0, 0)), pl.BlockSpec((None, d, cols), lambda l: (l, 0, 0))],
        out_specs=pl.BlockSpec((None, N_DEV, cols), lambda l: (l, 0, 0)), out_shape=_sds((depth, N_DEV, cols), F32),
        compiler_params=_params("parallel"), name=name)(c_all, mod_w)


def _modw_grad(c_all, dmod, name):
    depth, _, cols = dmod.shape
    d = c_all.shape[1]

    def body(c_ref, g_ref, o_ref):
        cv = c_ref[...]
        act = cv * jax.nn.sigmoid(cv)
        o_ref[...] = lax.dot_general(act, g_ref[...], (TN, ((), ())), precision=lax.Precision.HIGHEST,
                                     preferred_element_type=F32)

    return pl.pallas_call(
        body, grid=(depth,),
        in_specs=[pl.BlockSpec((N_DEV, d), lambda l: (0, 0)), pl.BlockSpec((None, N_DEV, cols), lambda l: (l, 0, 0))],
        out_specs=pl.BlockSpec((None, d, cols), lambda l: (l, 0, 0)), out_shape=_sds((depth, d, cols), F32),
        compiler_params=_params("parallel"), name=name)(c_all, dmod)


LANE = 128
W_IN = (("w_in", 960, True),)
WIDE_REST = (("w_out", 128, False), ("mlp_w2", 512, False), ("mlp_w1", 512, True))
NARROW = (("w_conv_proj", 128, True), ("w_hgrn_proj", 128, True), ("w_sb_proj", 128, True))
BIG_KEY = {"w_in": "win_t", "w_out": "wo", "mlp_w2": "w2", "mlp_w1": "w1_t",
           "w_conv_proj": "wc_t", "w_hgrn_proj": "wh_t", "w_sb_proj": "ws_t"}
SMALL = (("mod_b", 6144), ("norm1_g", 1024), ("gate_b", 3072), ("conv_w", CONV_WIDTH * CONV_CH), ("conv_b", 512),
         ("conv_ln_g", 512), ("conv_ln_b", 512), ("hgrn_lb", 512), ("hgrn_norm_g", 128), ("sb_qn_g", 64),
         ("sb_kn_g", 64), ("norm2_g", 1024))


def _pack_rows(parts, width):
    flat = jnp.concatenate([p.reshape(-1) for p in parts])
    rows = -(-flat.shape[0] // width)
    rows = -(-rows // 8) * 8
    return jnp.pad(flat, (0, rows * width - flat.shape[0])).reshape(rows, width)


def _pack_weights(spec, params, l):
    parts = []
    for name, _, transposed in spec:
        w = params[name][l]
        parts.append((w.T if transposed else w).astype(BF16))
    return jnp.concatenate(parts, axis=0)


def _unpack_gathered(spec, g):
    out = {}
    off = 0
    for name, rows, _ in spec:
        out[BIG_KEY[name]] = g[:, off:off + rows].reshape(N_DEV * rows, g.shape[2])
        off += rows
    return out


def _pack_grads(spec, big):
    parts = []
    for name, rows, _ in spec:
        gmat = big[BIG_KEY[name]]
        parts.append(gmat.reshape(N_DEV, rows, gmat.shape[1]))
    return jnp.concatenate(parts, axis=1)


def _unpack_shard_grads(spec, gsum):
    out = {}
    off = 0
    for name, rows, transposed in spec:
        blk = gsum[off:off + rows]
        out[name] = blk.T if transposed else blk
        off += rows
    return out


def _adamw_nd(w, g, m, v, name):
    shape = w.shape
    two = lambda a: a.reshape(-1, shape[-1])
    return [o.reshape(shape) for o in _adamw(two(w), two(g), two(m), two(v), name)]


WEIGHTS = ("mod_w", "mod_b", "norm1_g", "w_in", "gate_b", "conv_w", "conv_b", "conv_ln_g", "conv_ln_b", "w_conv_proj",
           "hgrn_lb", "hgrn_norm_g", "w_hgrn_proj", "sb_qn_g", "sb_kn_g", "w_sb_proj", "w_out", "norm2_g", "mlp_w1",
           "mlp_w2")


def kernel(x, c, mod_w, mod_b, norm1_g, w_in, gate_b, conv_w, conv_b, conv_ln_g, conv_ln_b, w_conv_proj, hgrn_lb, hgrn_norm_g, w_hgrn_proj, sb_qn_g, sb_kn_g, w_sb_proj, w_out, norm2_g, mlp_w1, mlp_w2, loss_target, m_mod_w, m_mod_b, m_norm1_g, m_w_in, m_gate_b, m_conv_w, m_conv_b, m_conv_ln_g, m_conv_ln_b, m_w_conv_proj, m_hgrn_lb, m_hgrn_norm_g, m_w_hgrn_proj, m_sb_qn_g, m_sb_kn_g, m_w_sb_proj, m_w_out, m_norm2_g, m_mlp_w1, m_mlp_w2, v_mod_w, v_mod_b, v_norm1_g, v_w_in, v_gate_b, v_conv_w, v_conv_b, v_conv_ln_g, v_conv_ln_b, v_w_conv_proj, v_hgrn_lb, v_hgrn_norm_g, v_w_hgrn_proj, v_sb_qn_g, v_sb_kn_g, v_w_sb_proj, v_w_out, v_norm2_g, v_mlp_w1, v_mlp_w2):
    params = dict(mod_w=mod_w, mod_b=mod_b, norm1_g=norm1_g, w_in=w_in, gate_b=gate_b, conv_w=conv_w, conv_b=conv_b,
                  conv_ln_g=conv_ln_g, conv_ln_b=conv_ln_b, w_conv_proj=w_conv_proj, hgrn_lb=hgrn_lb,
                  hgrn_norm_g=hgrn_norm_g, w_hgrn_proj=w_hgrn_proj, sb_qn_g=sb_qn_g, sb_kn_g=sb_kn_g,
                  w_sb_proj=w_sb_proj, w_out=w_out, norm2_g=norm2_g, mlp_w1=mlp_w1, mlp_w2=mlp_w2)
    mom1 = dict(mod_w=m_mod_w, mod_b=m_mod_b, norm1_g=m_norm1_g, w_in=m_w_in, gate_b=m_gate_b, conv_w=m_conv_w,
                conv_b=m_conv_b, conv_ln_g=m_conv_ln_g, conv_ln_b=m_conv_ln_b, w_conv_proj=m_w_conv_proj,
                hgrn_lb=m_hgrn_lb, hgrn_norm_g=m_hgrn_norm_g, w_hgrn_proj=m_w_hgrn_proj, sb_qn_g=m_sb_qn_g,
                sb_kn_g=m_sb_kn_g, w_sb_proj=m_w_sb_proj, w_out=m_w_out, norm2_g=m_norm2_g, mlp_w1=m_mlp_w1,
                mlp_w2=m_mlp_w2)
    mom2 = dict(mod_w=v_mod_w, mod_b=v_mod_b, norm1_g=v_norm1_g, w_in=v_w_in, gate_b=v_gate_b, conv_w=v_conv_w,
                conv_b=v_conv_b, conv_ln_g=v_conv_ln_g, conv_ln_b=v_conv_ln_b, w_conv_proj=v_w_conv_proj,
                hgrn_lb=v_hgrn_lb, hgrn_norm_g=v_hgrn_norm_g, w_hgrn_proj=v_w_hgrn_proj, sb_qn_g=v_sb_qn_g,
                sb_kn_g=v_sb_kn_g, w_sb_proj=v_w_sb_proj, w_out=v_w_out, norm2_g=v_norm2_g, mlp_w1=v_mlp_w1,
                mlp_w2=v_mlp_w2)
    xi, yi, ci = _mesh_place()
    me = _block_of(xi, yi, ci)
    cw_cols = conv_w.shape[2]

    tiny = _pack_rows([c, conv_w], LANE)
    g_tiny, g_win0 = _comm_alone(_GatherPlan([tiny, _pack_weights(W_IN, params, 0)]), "gather_first")
    c_rows = D_MODEL // LANE
    c_all = g_tiny[:, :c_rows].reshape(N_DEV, D_MODEL)
    n_cw = DEPTH * CONV_WIDTH * cw_cols
    conv_w_full = g_tiny[:, c_rows:c_rows + n_cw // LANE].reshape(N_DEV, DEPTH, CONV_WIDTH, cw_cols)
    conv_w_full = conv_w_full.transpose(1, 2, 0, 3).reshape(DEPTH, CONV_WIDTH, CONV_CH)

    (g_mod,) = _comm_alone(_GatherPlan([_mod_local(c_all, mod_w, "mod_local")]), "gather_mod")
    mod = lax.dynamic_index_in_dim(g_mod, me, axis=2, keepdims=False)
    mod = mod.transpose(1, 0, 2).reshape(DEPTH, 6 * D_MODEL) + mod_b

    sm = dict(norm1_g=norm1_g, norm2_g=norm2_g, gate_b=gate_b, conv_w=conv_w_full, conv_b=conv_b, conv_ln_g=conv_ln_g,
              conv_ln_b=conv_ln_b, hgrn_lb=hgrn_lb, hgrn_norm_g=hgrn_norm_g, sb_qn_g=sb_qn_g, sb_kn_g=sb_kn_g)
    vecs = [_layer_vectors(l, mod, sm) for l in range(DEPTH)]

    wts = [_unpack_gathered(W_IN, g_win0), None]
    win1 = _GatherPlan([_pack_weights(W_IN, params, 1)])
    rest0 = _GatherPlan([_pack_weights(WIDE_REST, params, 0), _pack_weights(NARROW, params, 0)])
    rest1 = _GatherPlan([_pack_weights(WIDE_REST, params, 1), _pack_weights(NARROW, params, 1)])
    sv0, (got_win1, got_rest0, got_rest1) = _layer_fwd_mixers(x[0], vecs[0], wts[0]["win_t"], "_l0", win1, rest0, rest1)
    wts[0].update(_unpack_gathered(WIDE_REST, got_rest0[0]))
    wts[0].update(_unpack_gathered(NARROW, got_rest0[1]))
    wts[1] = _unpack_gathered(W_IN, got_win1[0])
    wts[1].update(_unpack_gathered(WIDE_REST, got_rest1[0]))
    wts[1].update(_unpack_gathered(NARROW, got_rest1[1]))
    y = _layer_fwd_out(sv0, vecs[0], wts[0], "_l0")
    sv1, _ = _layer_fwd_mixers(y, vecs[1], wts[1]["win_t"], "_l1")
    y = _layer_fwd_out(sv1, vecs[1], wts[1], "_l1")
    dy, sq = _loss_head(y, loss_target[0], "loss_head")
    loss = lax.psum(0.5 * jnp.sum(sq) / D_MODEL, ("x", "y", "c"))

    dy, big1, small1, _ = _layer_bwd(dy, sv1, vecs[1], wts[1], "_l1")
    by_shard = lambda g: g.reshape(N_DEV, g.shape[0] // N_DEV, g.shape[1])
    plans = dict(
        mlp=lambda big: _ExchangePlan([by_shard(big1["win_a"])]),
        conv=lambda big: _ExchangePlan([by_shard(big1["win_b"])]),
        sb=lambda big: _ExchangePlan([_pack_grads(WIDE_REST, big1), _pack_grads(NARROW, big1)]),
        hgrn=lambda big: _ExchangePlan([_pack_grads(WIDE_REST, big), _pack_grads(NARROW, big)]),
        dwin=lambda big: _ExchangePlan([by_shard(big["win_a"])]),
        dh=lambda big: _ExchangePlan([by_shard(big["win_b"])]))
    dx, _, small0, got = _layer_bwd(dy, sv0, vecs[0], wts[0], "_l0", plans)
    smalls = [small0, small1]
    shard = [{}, {}]
    for l, (key_a, key_b) in enumerate((("dwin", "dh"), ("mlp", "conv"))):
        halves = [_sum8(got[key_a][0], f"sum_grads_win_a{l}"), _sum8(got[key_b][0], f"sum_grads_win_b{l}")]
        shard[l]["w_in"] = jnp.concatenate(halves, axis=1).T
    shard[0].update(_unpack_shard_grads(WIDE_REST, _sum8(got["hgrn"][0], "sum_grads_wide0")))
    shard[0].update(_unpack_shard_grads(NARROW, _sum8(got["hgrn"][1], "sum_grads_narrow0")))
    shard[1].update(_unpack_shard_grads(WIDE_REST, _sum8(got["sb"][0], "sum_grads_wide1")))
    shard[1].update(_unpack_shard_grads(NARROW, _sum8(got["sb"][1], "sum_grads_narrow1")))
    grads = {name: jnp.stack([shard[l][name] for l in range(DEPTH)]) for name in shard[0]}

    small_parts = []
    for name, _ in SMALL:
        key = "mod" if name == "mod_b" else name
        if name == "hgrn_lb":
            small_parts.append(smalls[0][key] + smalls[1][key])
        else:
            small_parts.append(jnp.stack([smalls[l][key] for l in range(DEPTH)]))
    (g_small,) = _comm_alone(_GatherPlan([_pack_rows(small_parts, LANE)]), "gather_small_grads")
    small_sum = _sum8(g_small, "sum_small_grads").reshape(-1)
    off = 0
    for name, per_layer in SMALL:
        grads[name] = small_sum[off:off + DEPTH * per_layer].reshape(params[name].shape if name != "conv_w" else (DEPTH, CONV_WIDTH, CONV_CH))
        off += DEPTH * per_layer
    grads["conv_w"] = lax.dynamic_slice_in_dim(grads["conv_w"], me * cw_cols, cw_cols, axis=2)
    cols = mod_w.shape[2]
    dmod_all = g_small.reshape(N_DEV, -1)[:, :DEPTH * 6 * D_MODEL].reshape(N_DEV, DEPTH, 6 * D_MODEL)
    dmod_mine = lax.dynamic_slice_in_dim(dmod_all, me * cols, cols, axis=2).transpose(1, 0, 2)
    grads["mod_w"] = _modw_grad(c_all, dmod_mine, "mod_w_grad")

    delta, new_m, new_v = {}, {}, {}
    small_names = [n for n, _ in SMALL]
    for name in WEIGHTS:
        if name not in small_names:
            delta[name], new_m[name], new_v[name] = _adamw_nd(params[name], grads[name], mom1[name], mom2[name], f"adamw_{name}")
    packed = [_pack_rows([d[n] for n in small_names], LANE) for d in (params, grads, mom1, mom2)]
    outs = [o.reshape(-1) for o in _adamw(*packed, "adamw_small")]
    off = 0
    for name in small_names:
        size = params[name].size
        for dst, o in zip((delta, new_m, new_v), outs):
            dst[name] = o[off:off + size].reshape(params[name].shape)
        off += size
    return (loss, dx[None], *[grads[n] for n in WEIGHTS], *[delta[n] for n in WEIGHTS],
            *[new_m[n] for n in WEIGHTS], *[new_v[n] for n in WEIGHTS])
```

```python
import functools

import jax
import jax.numpy as jnp
import numpy as np
from jax import lax
from jax.experimental import pallas as pl
from jax.experimental.pallas import tpu as pltpu

F32 = jnp.float32
BF16 = jnp.bfloat16

D_MODEL = 1024
DEPTH = 2
N_DEV = 8
CONV_CH = 512
CONV_WIDTH = 31
CONV_HALO = 32
HG_HEADS = 4
HG_DK = 128
SB_HEADS = 8
SB_DH = 64
D_IN = 7680
D_FF = 4096
EPS = 1e-6
SB_BLK = 128
SB_DEAD = -104.0
SB_FIXED = 3
HG_CHUNK = 128

ADAM_LR = 0.001
ADAM_B1 = 0.9
ADAM_B2 = 0.999
ADAM_EPS = 1e-08
ADAM_WD = 0.01
ADAM_STEP = 10

VMEM_LIMIT = 48 * 1024 * 1024

NN = ((1,), (0,))
NT = ((1,), (1,))
TN = ((0,), (0,))
_DIMS = {"nn": NN, "nt": NT, "tn": TN}


def _sds(shape, dtype):
    return jax.ShapeDtypeStruct(shape, dtype)


def _params(*semantics):
    return pltpu.CompilerParams(dimension_semantics=semantics, vmem_limit_bytes=VMEM_LIMIT)


def _dot(a, b, dims):
    return lax.dot_general(a, b, (dims, ((), ())), preferred_element_type=F32)


@functools.partial(jax.custom_vjp, nondiff_argnums=(2,))
def _bdot(a, b, mode):
    return _dot(a.astype(BF16), b.astype(BF16), _DIMS[mode])


def _bdot_fwd(a, b, mode):
    return _bdot(a, b, mode), (a.astype(BF16), b.astype(BF16))


def _bdot_bwd(mode, res, g):
    a, b = res
    g = g.astype(BF16)
    if mode == "nn":
        return _dot(g, b, NT), _dot(a, g, TN)
    if mode == "nt":
        return _dot(g, b, NN), _dot(g, a, TN)
    return _dot(b, g, NT), _dot(a, g, NN)


_bdot.defvjp(_bdot_fwd, _bdot_bwd)


def _split(x):
    hi = x.astype(BF16)
    lo = (x - hi.astype(F32)).astype(BF16)
    return hi, lo


def _xdot_right(x, m, dims=NN):
    hi, lo = _split(x)
    if dims == NN:
        return _dot(jnp.concatenate([hi, lo], axis=1), jnp.concatenate([m, m], axis=0), NN)
    return _dot(jnp.concatenate([hi, lo], axis=1), jnp.concatenate([m, m], axis=1), NT)


def _xdot_left(m, x, dims=NN):
    hi, lo = _split(x)
    if dims == NN:
        return _dot(jnp.concatenate([m, m], axis=1), jnp.concatenate([hi, lo], axis=0), NN)
    return _dot(jnp.concatenate([m, m], axis=0), jnp.concatenate([hi, lo], axis=0), TN)


@jax.custom_vjp
def _xr(x, m):
    return _xdot_right(x, m)


def _xr_fwd(x, m):
    return _xdot_right(x, m), m


def _xr_bwd(m, g):
    return _xdot_right(g, m, NT), jnp.zeros_like(m)


_xr.defvjp(_xr_fwd, _xr_bwd)


def _norm_mod(x, g, sc, sh):
    r = lax.rsqrt(jnp.mean(x * x, axis=-1, keepdims=True) + EPS)
    return x * r * g * (1.0 + sc) + sh


MESH = pl.DeviceIdType.MESH
HBM_SPEC = pl.BlockSpec(memory_space=pltpu.HBM)


def _mesh_place():
    return lax.axis_index("x"), lax.axis_index("y"), lax.axis_index("c")


def _block_of(px, py, pc):
    return 4 * px + 2 * py + pc


def _sem_scratch(n):
    return [pltpu.SemaphoreType.DMA((n, N_DEV - 1)), pltpu.SemaphoreType.DMA((n, N_DEV - 1)), pltpu.SemaphoreType.DMA((n,))]


class _GatherPlan:
    def __init__(self, xs):
        self.xs = list(xs)
        self.n = len(self.xs)
        self.out_shape = [_sds((N_DEV, *v.shape), v.dtype) for v in self.xs]
        self.scratch = _sem_scratch(self.n)

    def _parts(self, x_refs, out_refs, sems):
        send_sems, recv_sems, local_sems = sems
        x, y, c = _mesh_place()
        me, sibling = (x, y, c), (x, y, 1 - c)
        chips = [(1 - x, y), (x, 1 - y), (1 - x, 1 - y)]

        def copy(a, k, block, to, src=None):
            rows = out_refs[a].at[_block_of(*block)]
            return pltpu.make_async_remote_copy(
                src_ref=rows if src is None else src, dst_ref=rows, send_sem=send_sems.at[a, k],
                recv_sem=recv_sems.at[a, k], device_id=to, device_id_type=MESH)

        local = [pltpu.make_async_copy(x_refs[a], out_refs[a].at[_block_of(*me)], local_sems.at[a])
                 for a in range(self.n)]
        first = []
        for a in range(self.n):
            first.append(copy(a, 0, me, sibling, src=x_refs[a]))
            first += [copy(a, 1 + j, me, (*chip, c), src=x_refs[a]) for j, chip in enumerate(chips)]
        return me, sibling, chips, c, copy, local, first

    def start(self, x_refs, out_refs, sems):
        *_, local, first = self._parts(x_refs, out_refs, sems)
        for cp in local + first:
            cp.start()

    def finish(self, x_refs, out_refs, sems):
        me, sibling, chips, c, copy, local, first = self._parts(x_refs, out_refs, sems)
        passed = []
        for j, chip in enumerate(chips):
            for a in range(self.n):
                copy(a, 1 + j, (*chip, c), me).wait_recv()
                fwd = copy(a, 4 + j, (*chip, c), sibling)
                fwd.start()
                passed.append(fwd)
        for a in range(self.n):
            copy(a, 0, sibling, me).wait_recv()
            for j, chip in enumerate(chips):
                copy(a, 4 + j, (*chip, 1 - c), me).wait_recv()
        for cp in first + passed:
            cp.wait_send()
        for cp in local:
            cp.wait()


class _ExchangePlan:
    def __init__(self, xs):
        self.xs = list(xs)
        self.n = len(self.xs)
        self.out_shape = [_sds(v.shape, v.dtype) for v in self.xs]
        self.scratch = _sem_scratch(self.n)

    def _parts(self, in_refs, out_refs, sems):
        send_sems, recv_sems, local_sems = sems
        x, y, c = _mesh_place()
        mine = _block_of(x, y, c)
        peers = [(1 - x if k & 4 else x, 1 - y if k & 2 else y, 1 - c if k & 1 else c) for k in range(1, N_DEV)]

        def copy(a, k, slot_src, slot_dst):
            return pltpu.make_async_remote_copy(
                src_ref=in_refs[a].at[slot_src], dst_ref=out_refs[a].at[slot_dst], send_sem=send_sems.at[a, k],
                recv_sem=recv_sems.at[a, k], device_id=peers[k], device_id_type=MESH)

        local = [pltpu.make_async_copy(in_refs[a].at[mine], out_refs[a].at[mine], local_sems.at[a])
                 for a in range(self.n)]
        sends = [copy(a, k, _block_of(*peers[k]), mine) for a in range(self.n) for k in range(N_DEV - 1)]
        arrivals = [copy(a, k, _block_of(*peers[k]), _block_of(*peers[k])) for a in range(self.n) for k in range(N_DEV - 1)]
        return local, sends, arrivals

    def start(self, in_refs, out_refs, sems):
        local, sends, _ = self._parts(in_refs, out_refs, sems)
        for cp in local + sends:
            cp.start()

    def finish(self, in_refs, out_refs, sems):
        local, sends, arrivals = self._parts(in_refs, out_refs, sems)
        for cp in arrivals:
            cp.wait_recv()
        for cp in sends:
            cp.wait_send()
        for cp in local:
            cp.wait()


def _call(body, args, *, grid, in_specs, out_specs, out_shape, scratch_shapes=(), semantics, name, comm=None):
    if comm is None:
        return pl.pallas_call(
            body, grid=grid, in_specs=list(in_specs), out_specs=list(out_specs), out_shape=list(out_shape),
            scratch_shapes=list(scratch_shapes), compiler_params=_params(*semantics), name=name)(*args)
    n_in, n_out, n_scr, n = len(in_specs), len(out_specs), len(scratch_shapes), comm.n

    def hosted(*refs):
        ins, rest = refs[:n_in], refs[n_in:]
        cin, rest = rest[:n], rest[n:]
        outs, rest = rest[:n_out], rest[n_out:]
        cout, rest = rest[:n], rest[n:]
        scr, sems = rest[:n_scr], rest[n_scr:]
        pids = [pl.program_id(d) for d in range(len(grid))]
        first = functools.reduce(jnp.logical_and, [p == 0 for p in pids])
        last = functools.reduce(jnp.logical_and, [p == g - 1 for p, g in zip(pids, grid)])

        @pl.when(first)
        def _():
            comm.start(cin, cout, sems)

        body(*ins, *outs, *scr)

        @pl.when(last)
        def _():
            comm.finish(cin, cout, sems)

    res = pl.pallas_call(
        hosted, grid=grid, in_specs=list(in_specs) + [HBM_SPEC] * n, out_specs=list(out_specs) + [HBM_SPEC] * n,
        out_shape=list(out_shape) + comm.out_shape, scratch_shapes=list(scratch_shapes) + comm.scratch,
        compiler_params=_params(*["arbitrary"] * len(grid)), name=name)(*args, *comm.xs)
    return res[:n_out], res[n_out:]


def _comm_alone(comm, name):
    def body(*refs):
        n = comm.n
        comm.start(refs[:n], refs[n:2 * n], refs[2 * n:])
        comm.finish(refs[:n], refs[n:2 * n], refs[2 * n:])

    return pl.pallas_call(
        body, in_specs=[HBM_SPEC] * comm.n, out_specs=[HBM_SPEC] * comm.n, out_shape=comm.out_shape,
        scratch_shapes=comm.scratch, name=name)(*comm.xs)


def _matmul(a, b, mode, out_dtype, tm, tn, tk, name, comm=None):
    if mode == "nn":
        (m, k), (_, n) = a.shape, b.shape
    elif mode == "nt":
        (m, k), (n, _) = a.shape, b.shape
    else:
        (k, m), (_, n) = a.shape, b.shape
    tm, tn, tk = min(tm, m), min(tn, n), min(tk, k)
    assert m % tm == 0 and n % tn == 0 and k % tk == 0, (name, m, n, k, tm, tn, tk)
    nk = k // tk
    dims = _DIMS[mode]

    def body(a_ref, b_ref, o_ref, acc_ref):
        if nk == 1:
            o_ref[...] = _dot(a_ref[...], b_ref[...], dims).astype(out_dtype)
            return
        kk = pl.program_id(2)

        @pl.when(kk == 0)
        def _():
            acc_ref[...] = _dot(a_ref[...], b_ref[...], dims)

        @pl.when((kk > 0) & (kk < nk - 1))
        def _():
            acc_ref[...] += _dot(a_ref[...], b_ref[...], dims)

        @pl.when(kk == nk - 1)
        def _():
            o_ref[...] = (acc_ref[...] + _dot(a_ref[...], b_ref[...], dims)).astype(out_dtype)

    if mode == "tn":
        a_spec = pl.BlockSpec((tk, tm), lambda i, j, kk: (kk, i))
        b_spec = pl.BlockSpec((tk, tn), lambda i, j, kk: (kk, j))
    elif mode == "nn":
        a_spec = pl.BlockSpec((tm, tk), lambda i, j, kk: (i, kk))
        b_spec = pl.BlockSpec((tk, tn), lambda i, j, kk: (kk, j))
    else:
        a_spec = pl.BlockSpec((tm, tk), lambda i, j, kk: (i, kk))
        b_spec = pl.BlockSpec((tn, tk), lambda i, j, kk: (j, kk))
    res = _call(
        body, (a, b), grid=(m // tm, n // tn, nk), in_specs=[a_spec, b_spec],
        out_specs=[pl.BlockSpec((tm, tn), lambda i, j, kk: (i, j))],
        out_shape=[_sds((m, n), out_dtype)], scratch_shapes=[pltpu.VMEM((tm, tn), F32)],
        semantics=("parallel", "parallel", "arbitrary"), name=name, comm=comm)
    return res[0] if comm is None else (res[0][0], res[1])


ROW_T = 512


def _prenorm(x, pv, name):
    s, d = x.shape
    t = min(ROW_T, s)

    def body(x_ref, pv_ref, h_ref):
        h = _norm_mod(x_ref[...], pv_ref[6:7, :], pv_ref[1:2, :], pv_ref[0:1, :])
        h_ref[...] = h.astype(BF16)

    return pl.pallas_call(
        body, grid=(s // t,),
        in_specs=[pl.BlockSpec((t, d), lambda i: (i, 0)), pl.BlockSpec((16, d), lambda i: (0, 0))],
        out_specs=pl.BlockSpec((t, d), lambda i: (i, 0)), out_shape=_sds((s, d), BF16),
        compiler_params=_params("parallel"), name=name)(x, pv)


def _prenorm_bwd(dh, dres, x, pv, name):
    s, d = x.shape
    t = min(ROW_T, s)

    def body(dh_ref, dres_ref, x_ref, pv_ref, dx_ref, sg_ref):
        i = pl.program_id(0)

        @pl.when(i == 0)
        def _():
            sg_ref[...] = jnp.zeros_like(sg_ref)

        _, vjp = jax.vjp(_norm_mod, x_ref[...], pv_ref[6:7, :], pv_ref[1:2, :], pv_ref[0:1, :])
        dx, dg, dsc, dsh = vjp(dh_ref[...])
        dx_ref[...] = dres_ref[...] + dx
        sg_ref[0:1, :] += dsh
        sg_ref[1:2, :] += dsc
        sg_ref[2:3, :] += dg

    row = pl.BlockSpec((t, d), lambda i: (i, 0))
    return pl.pallas_call(
        body, grid=(s // t,),
        in_specs=[row, row, row, pl.BlockSpec((16, d), lambda i: (0, 0))],
        out_specs=[row, pl.BlockSpec((8, d), lambda i: (0, 0))],
        out_shape=[_sds((s, d), F32), _sds((8, d), F32)],
        compiler_params=_params("arbitrary"), name=name)(dh, dres, x, pv)


CONV_T = 256


def _conv_tile(a_ext, g_ext, w, b, ln_g, ln_b, n_out):
    u0 = a_ext * jax.nn.sigmoid(g_ext)
    off = CONV_HALO - (CONV_WIDTH - 1)
    acc = jnp.zeros((n_out, u0.shape[1]), F32) + b
    for r in range(8):
        taps = [k for k in range(CONV_WIDTH) if (off + k) % 8 == r]
        rows = n_out if r == 0 else n_out + 8
        part = None
        for k in taps:
            lo = (off + k) // 8 * 8
            term = w[k:k + 1, :] * u0[lo: lo + rows, :]
            part = term if part is None else part + term
        acc = acc + part[r: r + n_out, :]
    mu = jnp.mean(acc, axis=-1, keepdims=True)
    var = jnp.mean(jnp.square(acc - mu), axis=-1, keepdims=True)
    y = (acc - mu) * lax.rsqrt(var + EPS) * ln_g + ln_b
    return y * jax.nn.sigmoid(y)


def _conv_fwd(proj, conv_w, cp, name):
    s = proj.shape[0]
    t = min(CONV_T, s)
    c, h = CONV_CH, CONV_HALO

    def body(ap_ref, ac_ref, gp_ref, gc_ref, w_ref, cp_ref, o_ref):
        i = pl.program_id(0)
        live = (i > 0).astype(F32)
        a_ext = jnp.concatenate([ap_ref[t - h:, :] * live, ac_ref[...]], axis=0)
        g_ext = jnp.concatenate([gp_ref[t - h:, :], gc_ref[...]], axis=0)
        u = _conv_tile(a_ext, g_ext, w_ref[...], cp_ref[0:1, :], cp_ref[1:2, :], cp_ref[2:3, :], t)
        o_ref[...] = u.astype(BF16)

    prev = lambda col: pl.BlockSpec((t, c), lambda i: (jnp.maximum(i - 1, 0), col))
    cur = lambda col: pl.BlockSpec((t, c), lambda i: (i, col))
    return pl.pallas_call(
        body, grid=(s // t,),
        in_specs=[prev(0), cur(0), prev(1), cur(1),
                  pl.BlockSpec((CONV_WIDTH, c), lambda i: (0, 0)), pl.BlockSpec((8, c), lambda i: (0, 0))],
        out_specs=pl.BlockSpec((t, c), lambda i: (i, 0)), out_shape=_sds((s, c), BF16),
        compiler_params=_params("parallel"), name=name)(proj, proj, proj, proj, conv_w, cp)


def _conv_bwd(proj, do, conv_w, cp, name, comm=None):
    s = proj.shape[0]
    t = min(CONV_T, s)
    c, h = CONV_CH, CONV_HALO
    nt = s // t

    def body(ap_ref, ac_ref, an_ref, gp_ref, gc_ref, gn_ref, doc_ref, don_ref, w_ref, cp_ref,
             da_ref, dg_ref, dw_ref, sg_ref):
        i = pl.program_id(0)

        @pl.when(i == 0)
        def _():
            dw_ref[...] = jnp.zeros_like(dw_ref)
            sg_ref[...] = jnp.zeros_like(sg_ref)

        first = (i > 0).astype(F32)
        last = (i < nt - 1).astype(F32)
        a_ext = jnp.concatenate([ap_ref[t - h:, :] * first, ac_ref[...], an_ref[:h, :] * last], axis=0)
        g_ext = jnp.concatenate([gp_ref[t - h:, :], gc_ref[...], gn_ref[:h, :]], axis=0)
        fn = functools.partial(_conv_tile, n_out=t + h)
        _, vjp = jax.vjp(fn, a_ext, g_ext, w_ref[...], cp_ref[0:1, :], cp_ref[1:2, :], cp_ref[2:3, :])
        ct_own = jnp.concatenate([doc_ref[...], jnp.zeros((h, c), F32)], axis=0)
        ct_all = jnp.concatenate([doc_ref[...], don_ref[:h, :] * last], axis=0)
        _, _, dw, db, dlg, dlb = vjp(ct_own)
        da, dg, _, _, _, _ = vjp(ct_all)
        da_ref[...] = da[h:h + t, :].astype(BF16)
        dg_ref[...] = dg[h:h + t, :].astype(BF16)
        dw_ref[...] += dw
        sg_ref[0:1, :] += db
        sg_ref[1:2, :] += dlg
        sg_ref[2:3, :] += dlb

    prev = lambda col: pl.BlockSpec((t, c), lambda i: (jnp.maximum(i - 1, 0), col))
    cur = lambda col: pl.BlockSpec((t, c), lambda i: (i, col))
    nxt = lambda col: pl.BlockSpec((t, c), lambda i: (jnp.minimum(i + 1, nt - 1), col))
    return _call(
        body, (proj, proj, proj, proj, proj, proj, do, do, conv_w, cp), grid=(nt,),
        in_specs=[prev(0), cur(0), nxt(0), prev(1), cur(1), nxt(1), cur(0), nxt(0),
                  pl.BlockSpec((CONV_WIDTH, c), lambda i: (0, 0)), pl.BlockSpec((8, c), lambda i: (0, 0))],
        out_specs=[cur(0), cur(0), pl.BlockSpec((CONV_WIDTH, c), lambda i: (0, 0)),
                   pl.BlockSpec((8, c), lambda i: (0, 0))],
        out_shape=[_sds((s, c), BF16), _sds((s, c), BF16), _sds((CONV_WIDTH, c), F32), _sds((8, c), F32)],
        semantics=("arbitrary",), name=name, comm=comm)


def _hgrn_levels(c):
    out, m = [], c // 2
    while m >= 1:
        out.append(m)
        m //= 2
    return out


def _hgrn_consts(c):
    t = np.arange(c)[:, None]
    j = np.arange(c)[None, :]
    mats = [j <= t, j > t]
    for m in _hgrn_levels(c):
        same = (t // m) == (j // m)
        mats += [same & (j <= t), same & (j > t)]
    return jnp.asarray(np.concatenate(mats, axis=0).astype(np.float32), dtype=BF16)


@jax.custom_vjp
def _cums(lc, mall):
    c = lc.shape[0]
    full = _xdot_left(mall, lc)
    return tuple(full[i * c:(i + 1) * c, :] for i in range(mall.shape[0] // c))


def _cums_fwd(lc, mall):
    return _cums(lc, mall), mall


def _cums_bwd(mall, cts):
    return _xdot_left(mall, jnp.concatenate(cts, axis=0), TN), jnp.zeros_like(mall)


_cums.defvjp(_cums_fwd, _cums_bwd)


def _hgrn_chunk(q, f, v, g, lbs, ng, sts_in, mall):
    c = q.shape[0]
    keep = jax.nn.sigmoid(-f)
    if lbs:
        keep = (1.0 - jax.nn.sigmoid(lbs[1] - lbs[0])) * keep
    lc = jnp.log1p(-keep)
    qs = q * jax.nn.sigmoid(q)
    cs = _cums(lc, mall)
    q_in = qs * jnp.exp(cs[0])
    k_out = keep * jnp.exp(cs[1])
    decay = jnp.exp(jnp.sum(lc, axis=0, keepdims=True))
    qk = qs * keep
    r = lax.broadcasted_iota(jnp.int32, q.shape, 0)
    tt = lax.broadcasted_iota(jnp.int32, (c, c), 0)
    ss = lax.broadcasted_iota(jnp.int32, (c, c), 1)
    levels = []
    for li, m in enumerate(_hgrn_levels(c)):
        lg = m.bit_length() - 1
        odd = ((r >> lg) & 1) == 1
        qm = jnp.where(odd, qs * jnp.exp(cs[2 + 2 * li]), 0.0)
        km = jnp.where(odd, 0.0, keep * jnp.exp(cs[3 + 2 * li]))
        pair = (((tt >> lg) & 1) == 1) & ((ss >> lg) == (tt >> lg) - 1)
        levels.append((qm, km, pair))
    outs, sts_out = [], []
    for h, st_in in enumerate(sts_in):
        hs = slice(h * HG_DK, (h + 1) * HG_DK)
        vh = v[:, hs]
        sc = jnp.where(tt == ss, jnp.sum(qk[:, hs], axis=-1, keepdims=True), 0.0)
        for qm, km, pair in levels:
            sc = sc + jnp.where(pair, _bdot(qm[:, hs], km[:, hs], "nt"), 0.0)
        o = _bdot(q_in[:, hs], st_in, "nt") + _bdot(sc, vh, "nn")
        sts_out.append(st_in * decay[:, hs] + _bdot(vh, k_out[:, hs], "tn"))
        outs.append(o * lax.rsqrt(jnp.mean(o * o, axis=-1, keepdims=True) + EPS) * ng)
    return jnp.concatenate(outs, axis=1) * (g * jax.nn.sigmoid(g)), tuple(sts_out)


def _hgrn_fwd(proj, lb, ng, name, comm=None):
    s = proj.shape[0]
    c = HG_CHUNK
    nc = s // c
    mall = _hgrn_consts(c)
    col0 = 1024 // (HG_HEADS * HG_DK)

    def body(*refs):
        q_ref, f_ref, v_ref, g_ref = refs[:4]
        if lb is None:
            ng_ref, m_ref, y_ref, st_ref, scr = refs[4:]
        else:
            lb_ref, ng_ref, m_ref, y_ref, st_ref, scr = refs[4:]
        ci = pl.program_id(0)

        @pl.when(ci == 0)
        def _():
            scr[...] = jnp.zeros_like(scr)

        lbs = () if lb is None else (lb_ref[0:1, :], lb_ref[1:2, :])
        sts_in = tuple(scr[h] for h in range(HG_HEADS))
        for h in range(HG_HEADS):
            st_ref[h] = sts_in[h]
        y, sts_out = _hgrn_chunk(q_ref[...], f_ref[...], v_ref[...], g_ref[...], lbs, ng_ref[...], sts_in, m_ref[...])
        y_ref[...] = y.astype(BF16)
        for h in range(HG_HEADS):
            scr[h] = sts_out[h]

    w = HG_HEADS * HG_DK
    col = lambda k: pl.BlockSpec((c, w), lambda ci: (ci, col0 + k))
    in_specs = [col(0), col(1), col(2), col(3)]
    args = [proj, proj, proj, proj]
    if lb is not None:
        in_specs.append(pl.BlockSpec((2, w), lambda ci: (0, 0)))
        args.append(lb)
    in_specs += [pl.BlockSpec((1, HG_DK), lambda ci: (0, 0)), pl.BlockSpec(mall.shape, lambda ci: (0, 0))]
    args += [ng, mall]
    return _call(
        body, args, grid=(nc,), in_specs=in_specs,
        out_specs=[pl.BlockSpec((c, w), lambda ci: (ci, 0)),
                   pl.BlockSpec((HG_HEADS, None, HG_DK, HG_DK), lambda ci: (0, ci, 0, 0))],
        out_shape=[_sds((s, w), BF16), _sds((HG_HEADS, nc, HG_DK, HG_DK), F32)],
        scratch_shapes=[pltpu.VMEM((HG_HEADS, HG_DK, HG_DK), F32)],
        semantics=("arbitrary",), name=name, comm=comm)


def _hgrn_bwd(proj, states, dy, lb, ng, name, comm=None):
    s = proj.shape[0]
    c = HG_CHUNK
    nc = s // c
    mall = _hgrn_consts(c)
    col0 = 1024 // (HG_HEADS * HG_DK)

    def body(*refs):
        q_ref, f_ref, v_ref, g_ref, st_ref, dy_ref = refs[:6]
        if lb is None:
            ng_ref, m_ref, dq_ref, df_ref, dv_ref, dg_ref, dlb_ref, dng_ref, scr = refs[6:]
        else:
            lb_ref, ng_ref, m_ref, dq_ref, df_ref, dv_ref, dg_ref, dlb_ref, dng_ref, scr = refs[6:]
        ci = pl.program_id(0)

        @pl.when(ci == 0)
        def _():
            scr[...] = jnp.zeros_like(scr)
            dlb_ref[...] = jnp.zeros_like(dlb_ref)
            dng_ref[...] = jnp.zeros_like(dng_ref)

        mall_v = m_ref[...]
        fn = lambda q, f, v, g, lbs_, ng_, sts: _hgrn_chunk(q, f, v, g, lbs_, ng_, sts, mall_v)
        lbs = () if lb is None else (lb_ref[0:1, :], lb_ref[1:2, :])
        sts_in = tuple(st_ref[h] for h in range(HG_HEADS))
        _, vjp = jax.vjp(fn, q_ref[...], f_ref[...], v_ref[...], g_ref[...], lbs, ng_ref[...], sts_in)
        dq, df, dv, dg, dlbs, dng, dsts = vjp((dy_ref[...], tuple(scr[h] for h in range(HG_HEADS))))
        dq_ref[...] = dq.astype(BF16)
        df_ref[...] = df.astype(BF16)
        dv_ref[...] = dv.astype(BF16)
        dg_ref[...] = dg.astype(BF16)
        for h in range(HG_HEADS):
            scr[h] = dsts[h]
        dng_ref[0:1, :] += dng
        if lbs:
            dlb_ref[0:1, :] += dlbs[0]
            dlb_ref[1:2, :] += dlbs[1]

    w = HG_HEADS * HG_DK
    rev = lambda ci: nc - 1 - ci
    col = lambda k: pl.BlockSpec((c, w), lambda ci: (rev(ci), col0 + k))
    out_col = pl.BlockSpec((c, w), lambda ci: (rev(ci), 0))
    in_specs = [col(0), col(1), col(2), col(3),
                pl.BlockSpec((HG_HEADS, None, HG_DK, HG_DK), lambda ci: (0, rev(ci), 0, 0)), out_col]
    args = [proj, proj, proj, proj, states, dy]
    if lb is not None:
        in_specs.append(pl.BlockSpec((2, w), lambda ci: (0, 0)))
        args.append(lb)
    in_specs += [pl.BlockSpec((1, HG_DK), lambda ci: (0, 0)), pl.BlockSpec(mall.shape, lambda ci: (0, 0))]
    args += [ng, mall]
    return _call(
        body, args, grid=(nc,), in_specs=in_specs,
        out_specs=[out_col, out_col, out_col, out_col,
                   pl.BlockSpec((2, w), lambda ci: (0, 0)), pl.BlockSpec((8, HG_DK), lambda ci: (0, 0))],
        out_shape=[_sds((s, w), BF16)] * 4 + [_sds((2, w), F32), _sds((8, HG_DK), F32)],
        scratch_shapes=[pltpu.VMEM((HG_HEADS, HG_DK, HG_DK), F32)],
        semantics=("arbitrary",), name=name, comm=comm)


def _head_avg():
    w = SB_HEADS * SB_DH
    i = np.arange(w)
    return jnp.asarray(((i[:, None] // SB_DH) == (i[None, :] // SB_DH)).astype(np.float32) / SB_DH, dtype=BF16)


def _sb_norm(x, g_tiled, avg):
    ms = _xr(x * x, avg)
    return x * lax.rsqrt(ms + EPS) * g_tiled


def _sb_prep(proj, gq, gk, name):
    s = proj.shape[0]
    t = min(ROW_T, s)
    w = SB_HEADS * SB_DH
    avg = _head_avg()

    def body(q_ref, k_ref, v_ref, gq_ref, gk_ref, avg_ref, qn_ref, kn_ref, vb_ref):
        qn_ref[...] = _sb_norm(q_ref[...], gq_ref[...], avg_ref[...]).astype(BF16)
        kn_ref[...] = _sb_norm(k_ref[...], gk_ref[...], avg_ref[...]).astype(BF16)
        vb_ref[...] = v_ref[...].astype(BF16)

    col = lambda k: pl.BlockSpec((t, w), lambda i: (i, 6 + k))
    vec = pl.BlockSpec((1, w), lambda i: (0, 0))
    out = pl.BlockSpec((t, w), lambda i: (i, 0))
    return pl.pallas_call(
        body, grid=(s // t,), in_specs=[col(0), col(1), col(2), vec, vec, pl.BlockSpec((w, w), lambda i: (0, 0))],
        out_specs=[out, out, out], out_shape=[_sds((s, w), BF16)] * 3,
        compiler_params=_params("parallel"), name=name)(proj, proj, proj, gq, gk, avg)


def _sb_prep_bwd(proj, dqn, dkn, gq, gk, name):
    s = proj.shape[0]
    t = min(ROW_T, s)
    w = SB_HEADS * SB_DH
    avg = _head_avg()

    def body(q_ref, k_ref, dqn_ref, dkn_ref, gq_ref, gk_ref, avg_ref, dq_ref, dk_ref, sg_ref):
        i = pl.program_id(0)

        @pl.when(i == 0)
        def _():
            sg_ref[...] = jnp.zeros_like(sg_ref)

        avg_v = avg_ref[...]
        fn = lambda x, g: _sb_norm(x, g, avg_v)
        _, vq = jax.vjp(fn, q_ref[...], gq_ref[...])
        dq, dgq = vq(dqn_ref[...])
        _, vk = jax.vjp(fn, k_ref[...], gk_ref[...])
        dk, dgk = vk(dkn_ref[...])
        dq_ref[...] = dq.astype(BF16)
        dk_ref[...] = dk.astype(BF16)
        sg_ref[0:1, :] += dgq
        sg_ref[1:2, :] += dgk

    col = lambda k: pl.BlockSpec((t, w), lambda i: (i, 6 + k))
    vec = pl.BlockSpec((1, w), lambda i: (0, 0))
    row = pl.BlockSpec((t, w), lambda i: (i, 0))
    return pl.pallas_call(
        body, grid=(s // t,),
        in_specs=[col(0), col(1), row, row, vec, vec, pl.BlockSpec((w, w), lambda i: (0, 0))],
        out_specs=[row, row, pl.BlockSpec((8, w), lambda i: (0, 0))],
        out_shape=[_sds((s, w), BF16), _sds((s, w), BF16), _sds((8, w), F32)],
        compiler_params=_params("arbitrary"), name=name)(proj, proj, dqn, dkn, gq, gk, avg)


def _sb_tri(kind):
    j = np.arange(SB_BLK)[:, None]
    s = np.arange(SB_BLK)[None, :]
    tri = (j > s) if kind == "suffix" else (j < s)
    return jnp.asarray(np.concatenate([tri, np.ones_like(tri)], axis=1).astype(np.float32), dtype=BF16)


def _sb_scores(qm, kblk, mask):
    z = _dot(qm, kblk, NT) * (SB_DH ** -0.5)
    sp = jnp.maximum(z, 0.0) + jnp.log(1.0 + jnp.exp(-jnp.abs(z)))
    return z, sp, jnp.where(mask, -sp, 0.0)


def _sb_setup(b):
    lane = lax.broadcasted_iota(jnp.int32, (2 * b, b), 1)
    row = lax.broadcasted_iota(jnp.int32, (2 * b, b), 0)
    mine = (row >> (b.bit_length() - 1)) == (lane >> (SB_DH.bit_length() - 1))
    return lane, row & (b - 1), mine


def _sb_fwd(qn, kn, vb, name, comm=None):
    s, w = qn.shape
    b = SB_BLK
    nq = s // b
    tri = _sb_tri("suffix")

    def body(q_ref, k_ref, v_ref, tri_ref, o_ref):
        i = pl.program_id(1)
        lane, tt, mine = _sb_setup(b)
        q = q_ref[...]
        q2 = jnp.concatenate([q, q], axis=0)
        qm = jnp.where(mine, q2, jnp.zeros_like(q2))
        tri_v = tri_ref[...]

        def block(kb, lim, run, acc):
            off = pl.multiple_of(kb * b, b)
            kblk = k_ref[pl.ds(off, b), :]
            vblk = v_ref[pl.ds(off, b), :]
            mask = lane < lim
            z, sp, lk = _sb_scores(qm, kblk, mask)
            both = _xdot_right(lk, tri_v)
            a = jnp.where(mask, jnp.exp(z - sp + both[:, :b] + run), 0.0)
            return run + both[:, b:], acc + _dot(a.astype(BF16), vblk, NN)

        offs = [pl.multiple_of(jnp.maximum(i - j, 0) * b, b) for j in range(SB_FIXED)]
        masks = [lane < (tt if j == 0 else jnp.where(i >= j, b, 0)) for j in range(SB_FIXED)]
        scores = [_sb_scores(qm, k_ref[pl.ds(off, b), :], m) for off, m in zip(offs, masks)]
        boths = [_xdot_right(lk, tri_v) for _, _, lk in scores]
        run = acc = jnp.zeros((2 * b, b), F32)
        for j in range(SB_FIXED):
            z, sp, _ = scores[j]
            a = jnp.where(masks[j], jnp.exp(z - sp + boths[j][:, :b] + run), 0.0)
            acc = acc + _dot(a.astype(BF16), v_ref[pl.ds(offs[j], b), :], NN)
            run = run + boths[j][:, b:]

        def cond(carry):
            j, run_, _ = carry
            return (j <= i) & (jnp.max(run_) > SB_DEAD)

        def step(carry):
            j, run_, acc_ = carry
            run_, acc_ = block(i - j, b, run_, acc_)
            return j + 1, run_, acc_

        _, _, acc = lax.while_loop(cond, step, (jnp.int32(SB_FIXED), run, acc))
        o_ref[...] = jnp.where(lane[:b] < SB_DH, acc[:b], acc[b:]).astype(BF16)

    blk = pl.BlockSpec((b, b), lambda p, i: (i, p))
    full = pl.BlockSpec((s, b), lambda p, i: (0, p))
    return _call(
        body, (qn, kn, vb, tri), grid=(w // b, nq),
        in_specs=[blk, full, full, pl.BlockSpec(tri.shape, lambda p, i: (0, 0))],
        out_specs=[blk], out_shape=[_sds((s, w), BF16)],
        semantics=("parallel", "arbitrary"), name=name, comm=comm)


def _sb_bwd(qn, kn, vb, do, name, comm=None):
    s, w = qn.shape
    b = SB_BLK
    nq = s // b
    tri_s = _sb_tri("suffix")
    tri_p = _sb_tri("prefix")
    scale = SB_DH ** -0.5

    def body(q_ref, k_ref, v_ref, do_ref, ts_ref, tp_ref, dq_ref, dk_ref, dv_ref, dk_acc, dv_acc, dp_scr):
        i = pl.program_id(1)

        @pl.when(i == 0)
        def _():
            dk_acc[...] = jnp.zeros_like(dk_acc)
            dv_acc[...] = jnp.zeros_like(dv_acc)

        lane, tt, mine = _sb_setup(b)
        q = q_ref[...]
        q2 = jnp.concatenate([q, q], axis=0)
        qm = jnp.where(mine, q2, jnp.zeros_like(q2))
        dout = do_ref[...].astype(BF16)
        d2 = jnp.concatenate([dout, dout], axis=0)
        dom = jnp.where(mine, d2, jnp.zeros_like(d2))
        ts_v = ts_ref[...]
        tp_v = tp_ref[...]
        zero = jnp.zeros((2 * b, b), F32)

        def down(kb, lim, run):
            off = pl.multiple_of(kb * b, b)
            kblk = k_ref[pl.ds(off, b), :]
            vblk = v_ref[pl.ds(off, b), :]
            mask = lane < lim
            z, sp, lk = _sb_scores(qm, kblk, mask)
            both = _xdot_right(lk, ts_v)
            a = jnp.where(mask, jnp.exp(z - sp + both[:, :b] + run), 0.0)
            dv_acc[pl.ds(off, b), :] += _dot(a.astype(BF16), dom, TN)
            return _dot(dom, vblk, NT) * a, run + both[:, b:]

        def up(kb, lim, dp, pre, dq):
            off = pl.multiple_of(kb * b, b)
            kblk = k_ref[pl.ds(off, b), :]
            sig = jax.nn.sigmoid(_dot(qm, kblk, NT) * scale)
            both = _xdot_right(dp, tp_v)
            dz = jnp.where(lane < lim, dp * (1.0 - sig) - sig * (both[:, :b] + pre), 0.0) * scale
            dz = dz.astype(BF16)
            dk_acc[pl.ds(off, b), :] += _dot(dz, qm, TN)
            return pre + both[:, b:], dq + _dot(dz, kblk, NN)

        offs = [pl.multiple_of(jnp.maximum(i - j, 0) * b, b) for j in range(SB_FIXED)]
        masks = [lane < (tt if j == 0 else jnp.where(i >= j, b, 0)) for j in range(SB_FIXED)]
        kblks = [k_ref[pl.ds(off, b), :] for off in offs]
        scores = [_sb_scores(qm, kblk, m) for kblk, m in zip(kblks, masks)]
        das = [_dot(dom, v_ref[pl.ds(off, b), :], NT) for off in offs]
        boths = [_xdot_right(lk, ts_v) for _, _, lk in scores]
        run = zero
        dps = []
        for j in range(SB_FIXED):
            z, sp, _ = scores[j]
            a = jnp.where(masks[j], jnp.exp(z - sp + boths[j][:, :b] + run), 0.0)
            dps.append(das[j] * a)
            dv_acc[pl.ds(offs[j], b), :] += _dot(a.astype(BF16), dom, TN)
            run = run + boths[j][:, b:]

        def cond(carry):
            j, run_ = carry
            return (j <= i) & (jnp.max(run_) > SB_DEAD)

        def sweep_down(carry):
            j, run_ = carry
            dp, run_ = down(i - j, b, run_)
            dp_scr[i - j] = dp
            return j + 1, run_

        n_live, _ = lax.while_loop(cond, sweep_down, (jnp.int32(SB_FIXED), run))

        def sweep_up(jj, carry):
            kb = i - n_live + 1 + jj
            return up(kb, b, dp_scr[kb], *carry)

        pre, dq = lax.fori_loop(0, n_live - SB_FIXED, sweep_up, (zero, zero))
        pres = [_xdot_right(dp, tp_v) for dp in dps]
        for j in reversed(range(SB_FIXED)):
            z, sp, _ = scores[j]
            sig = jnp.exp(z - sp)
            dz = jnp.where(masks[j], dps[j] * (1.0 - sig) - sig * (pres[j][:, :b] + pre), 0.0) * scale
            dz = dz.astype(BF16)
            dk_acc[pl.ds(offs[j], b), :] += _dot(dz, qm, TN)
            dq = dq + _dot(dz, kblks[j], NN)
            pre = pre + pres[j][:, b:]
        dq_ref[...] = jnp.where(lane[:b] < SB_DH, dq[:b], dq[b:])

        @pl.when(i == nq - 1)
        def _():
            dk_ref[...] = dk_acc[...]
            dv_ref[...] = dv_acc[...].astype(BF16)

    blk = pl.BlockSpec((b, b), lambda p, i: (i, p))
    full = pl.BlockSpec((s, b), lambda p, i: (0, p))
    tri = pl.BlockSpec(tri_s.shape, lambda p, i: (0, 0))
    return _call(
        body, (qn, kn, vb, do, tri_s, tri_p), grid=(w // b, nq), in_specs=[blk, full, full, blk, tri, tri],
        out_specs=[blk, full, full], out_shape=[_sds((s, w), F32), _sds((s, w), F32), _sds((s, w), BF16)],
        scratch_shapes=[pltpu.VMEM((s, b), F32), pltpu.VMEM((s, b), F32), pltpu.VMEM((nq, 2 * b, b), F32)],
        semantics=("arbitrary", "arbitrary"), name=name, comm=comm)


MIX_T = 256
HALF = 512


def _gate_slices(ga, gb):
    return [(ga[:, 0:512], ga[:, 512:1024]), (ga[:, 1024:1536], gb[:, 0:512]), (gb[:, 512:1024], gb[:, 1024:1536])]


def _mix_fwd(u3, oh, osb, proj, x, pv, wc, wh, ws, wo, name):
    s, d = x.shape
    t = min(MIX_T, s)

    def body(u3_ref, oh_ref, os_ref, ga_ref, gb_ref, x_ref, pv_ref, wc_ref, wh_ref, ws_ref, wo_ref,
             x1_ref, h2_ref, mg_ref, mo_ref):
        ys = [_dot(u3_ref[...], wc_ref[...], NT), _dot(oh_ref[...], wh_ref[...], NT), _dot(os_ref[...], ws_ref[...], NT)]
        gl = _gate_slices(ga_ref[...], gb_ref[...])
        halves = []
        for hf in range(2):
            lo = hf * HALF
            acc = jnp.zeros((t, HALF), F32)
            for br in range(3):
                gate = jax.nn.sigmoid(gl[br][hf] + pv_ref[8 + br:9 + br, lo:lo + HALF])
                acc = acc + gate * ys[br][:, lo:lo + HALF]
            halves.append(acc)
        merged = jnp.concatenate(halves, axis=1).astype(BF16)
        mg_ref[...] = merged
        mo = _dot(merged, wo_ref[...], NN)
        mo_ref[...] = mo.astype(BF16)
        x1 = x_ref[...] + pv_ref[2:3, :] * mo
        x1_ref[...] = x1
        h2_ref[...] = _norm_mod(x1, pv_ref[7:8, :], pv_ref[4:5, :], pv_ref[3:4, :]).astype(BF16)

    br_spec = pl.BlockSpec((t, CONV_CH), lambda i: (i, 0))
    row = pl.BlockSpec((t, d), lambda i: (i, 0))
    wproj = pl.BlockSpec((d, CONV_CH), lambda i: (0, 0))
    return pl.pallas_call(
        body, grid=(s // t,),
        in_specs=[br_spec, br_spec, br_spec, pl.BlockSpec((t, 1536), lambda i: (i, 3)),
                  pl.BlockSpec((t, 1536), lambda i: (i, 4)), row, pl.BlockSpec((16, d), lambda i: (0, 0)),
                  wproj, wproj, wproj, pl.BlockSpec((d, d), lambda i: (0, 0))],
        out_specs=[row, row, row, row],
        out_shape=[_sds((s, d), F32), _sds((s, d), BF16), _sds((s, d), BF16), _sds((s, d), BF16)],
        compiler_params=_params("parallel"), name=name)(u3, oh, osb, proj, proj, x, pv, wc, wh, ws, wo)


def _mix_bwd(dx1, mo1, u3, oh, osb, proj, pv, wc, wh, ws, wo, name):
    s, d = dx1.shape
    t = min(MIX_T, s)

    def body(dx_ref, mo_ref, u3_ref, oh_ref, os_ref, ga_ref, gb_ref, pv_ref, wc_ref, wh_ref, ws_ref, wo_ref,
             dmo_ref, dyc_ref, dyh_ref, dys_ref, doc_ref, doh_ref, dos_ref, dgl_ref, sg_ref):
        i = pl.program_id(0)

        @pl.when(i == 0)
        def _():
            sg_ref[...] = jnp.zeros_like(sg_ref)

        dx = dx_ref[...]
        dmo = (dx * pv_ref[2:3, :]).astype(BF16)
        dmo_ref[...] = dmo
        sg_ref[0:1, :] += jnp.sum(dx * mo_ref[...].astype(F32), axis=0, keepdims=True)
        dmerged = _dot(dmo, wo_ref[...], NT)
        branches = [(u3_ref, wc_ref, dyc_ref, doc_ref), (oh_ref, wh_ref, dyh_ref, doh_ref), (os_ref, ws_ref, dys_ref, dos_ref)]
        gl = _gate_slices(ga_ref[...], gb_ref[...])
        for br, (o_ref, w_ref, dy_ref, do_ref) in enumerate(branches):
            y = _dot(o_ref[...], w_ref[...], NT)
            dys = []
            for hf in range(2):
                lo = hf * HALF
                gate = jax.nn.sigmoid(gl[br][hf] + pv_ref[8 + br:9 + br, lo:lo + HALF])
                dm = dmerged[:, lo:lo + HALF]
                dys.append(dm * gate)
                dgl = dm * y[:, lo:lo + HALF] * gate * (1.0 - gate)
                dgl_ref[:, br * d + lo: br * d + lo + HALF] = dgl.astype(BF16)
                sg_ref[1 + br:2 + br, lo:lo + HALF] += jnp.sum(dgl, axis=0, keepdims=True)
            dy = jnp.concatenate(dys, axis=1).astype(BF16)
            dy_ref[...] = dy
            do_ref[...] = _dot(dy, w_ref[...], NN)

    br_spec = pl.BlockSpec((t, CONV_CH), lambda i: (i, 0))
    row = pl.BlockSpec((t, d), lambda i: (i, 0))
    wproj = pl.BlockSpec((d, CONV_CH), lambda i: (0, 0))
    return pl.pallas_call(
        body, grid=(s // t,),
        in_specs=[row, row, br_spec, br_spec, br_spec, pl.BlockSpec((t, 1536), lambda i: (i, 3)),
                  pl.BlockSpec((t, 1536), lambda i: (i, 4)), pl.BlockSpec((16, d), lambda i: (0, 0)),
                  wproj, wproj, wproj, pl.BlockSpec((d, d), lambda i: (0, 0))],
        out_specs=[row, row, row, row, br_spec, br_spec, br_spec, pl.BlockSpec((t, 3 * d), lambda i: (i, 0)),
                   pl.BlockSpec((8, d), lambda i: (0, 0))],
        out_shape=[_sds((s, d), BF16)] * 4 + [_sds((s, CONV_CH), F32)] * 3 + [_sds((s, 3 * d), BF16), _sds((8, d), F32)],
        compiler_params=_params("arbitrary"), name=name)(dx1, mo1, u3, oh, osb, proj, proj, pv, wc, wh, ws, wo)


MLP_T = 512
MLP_F = 512


def _mlp_fwd(h2, x1, pv, w1t, w2, name):
    s, d = x1.shape
    t = min(MLP_T, s)
    nf = D_FF // MLP_F

    def body(h_ref, x_ref, pv_ref, w1_ref, w2_ref, x2_ref, mo_ref, acc_ref):
        f = pl.program_id(1)

        @pl.when(f == 0)
        def _():
            acc_ref[...] = jnp.zeros_like(acc_ref)

        a = jnp.maximum(_dot(h_ref[...], w1_ref[...], NT), 0.0)
        acc_ref[...] += _dot((a * a).astype(BF16), w2_ref[...], NN)

        @pl.when(f == nf - 1)
        def _():
            mo = acc_ref[...]
            mo_ref[...] = mo.astype(BF16)
            x2_ref[...] = x_ref[...] + pv_ref[5:6, :] * mo

    row = pl.BlockSpec((t, d), lambda i, f: (i, 0))
    wblk = pl.BlockSpec((MLP_F, d), lambda i, f: (f, 0))
    return pl.pallas_call(
        body, grid=(s // t, nf), in_specs=[row, row, pl.BlockSpec((16, d), lambda i, f: (0, 0)), wblk, wblk],
        out_specs=[row, row], out_shape=[_sds((s, d), F32), _sds((s, d), BF16)],
        scratch_shapes=[pltpu.VMEM((t, d), F32)],
        compiler_params=_params("parallel", "arbitrary"), name=name)(h2, x1, pv, w1t, w2)


def _mlp_bwd(dx2, h2, x1, mo2, pv, w1t, w2, name, comm=None):
    s, d = x1.shape
    t = min(MLP_T, s)
    nf = D_FF // MLP_F

    def body(dx_ref, h_ref, x_ref, mo_ref, pv_ref, w1_ref, w2_ref, dx1_ref, da_ref, b_ref, dmo_ref, sg_ref, acc_ref):
        i = pl.program_id(0)
        f = pl.program_id(1)

        @pl.when((i == 0) & (f == 0))
        def _():
            sg_ref[...] = jnp.zeros_like(sg_ref)

        @pl.when(f == 0)
        def _():
            acc_ref[...] = jnp.zeros_like(acc_ref)
            dx = dx_ref[...]
            dmo_ref[...] = (dx * pv_ref[5:6, :]).astype(BF16)
            sg_ref[0:1, :] += jnp.sum(dx * mo_ref[...].astype(F32), axis=0, keepdims=True)

        r = jnp.maximum(_dot(h_ref[...], w1_ref[...], NT), 0.0)
        b_ref[...] = (r * r).astype(BF16)
        da = (_dot(dmo_ref[...], w2_ref[...], NT) * (2.0 * r)).astype(BF16)
        da_ref[...] = da
        acc_ref[...] += _dot(da, w1_ref[...], NN)

        @pl.when(f == nf - 1)
        def _():
            _, vjp = jax.vjp(_norm_mod, x_ref[...], pv_ref[7:8, :], pv_ref[4:5, :], pv_ref[3:4, :])
            dxn, dg, dsc, dsh = vjp(acc_ref[...])
            dx1_ref[...] = dx_ref[...] + dxn
            sg_ref[1:2, :] += dsh
            sg_ref[2:3, :] += dsc
            sg_ref[3:4, :] += dg

    row = pl.BlockSpec((t, d), lambda i, f: (i, 0))
    wblk = pl.BlockSpec((MLP_F, d), lambda i, f: (f, 0))
    hid = pl.BlockSpec((t, MLP_F), lambda i, f: (i, f))
    return _call(
        body, (dx2, h2, x1, mo2, pv, w1t, w2), grid=(s // t, nf),
        in_specs=[row, row, row, row, pl.BlockSpec((16, d), lambda i, f: (0, 0)), wblk, wblk],
        out_specs=[row, hid, hid, row, pl.BlockSpec((8, d), lambda i, f: (0, 0))],
        out_shape=[_sds((s, d), F32), _sds((s, D_FF), BF16), _sds((s, D_FF), BF16), _sds((s, d), BF16), _sds((8, d), F32)],
        scratch_shapes=[pltpu.VMEM((t, d), F32)],
        semantics=("arbitrary", "arbitrary"), name=name, comm=comm)


def _loss_head(y, target, name):
    s, d = y.shape
    t = min(ROW_T, s)

    def body(y_ref, t_ref, dy_ref, ls_ref):
        i = pl.program_id(0)

        @pl.when(i == 0)
        def _():
            ls_ref[...] = jnp.zeros_like(ls_ref)

        e = y_ref[...] - t_ref[...]
        dy_ref[...] = e * (1.0 / d)
        ls_ref[...] += jnp.sum((e * e).reshape(t // 8, 8, d), axis=0)

    row = pl.BlockSpec((t, d), lambda i: (i, 0))
    return pl.pallas_call(
        body, grid=(s // t,), in_specs=[row, row], out_specs=[row, pl.BlockSpec((8, d), lambda i: (0, 0))],
        out_shape=[_sds((s, d), F32), _sds((8, d), F32)],
        compiler_params=_params("arbitrary"), name=name)(y, target)


def _layer_vectors(l, mod, sm):
    d = D_MODEL
    pv = jnp.concatenate([mod[l].reshape(6, d), sm["norm1_g"][l][None], sm["norm2_g"][l][None],
                          sm["gate_b"][l].reshape(3, d), jnp.zeros((5, d), F32)], axis=0)
    cp = jnp.concatenate([sm["conv_b"][l][None], sm["conv_ln_g"][l][None], sm["conv_ln_b"][l][None],
                          jnp.zeros((5, CONV_CH), F32)], axis=0)
    return dict(pv=pv, cp=cp, conv_w=sm["conv_w"][l], lb=(sm["hgrn_lb"] if l > 0 else None),
                ng=sm["hgrn_norm_g"][l][None], gq=jnp.tile(sm["sb_qn_g"][l], SB_HEADS)[None],
                gk=jnp.tile(sm["sb_kn_g"][l], SB_HEADS)[None])


def _hosted(res, comm):
    return res if comm is not None else (res, None)


def _layer_fwd_mixers(x, vec, win_t, tag, comm_proj=None, comm_hgrn=None, comm_sb=None):
    h = _prenorm(x, vec["pv"], f"prenorm{tag}")
    proj, got_proj = _hosted(_matmul(h, win_t, "nt", F32, 1024, 768, 1024, f"proj{tag}", comm_proj), comm_proj)
    u3 = _conv_fwd(proj, vec["conv_w"], vec["cp"], f"conv_fwd{tag}")
    (oh, states), got_hgrn = _hosted(_hgrn_fwd(proj, vec["lb"], vec["ng"], f"hgrn_fwd{tag}", comm_hgrn), comm_hgrn)
    qn, kn, vb = _sb_prep(proj, vec["gq"], vec["gk"], f"sb_prep{tag}")
    (osb,), got_sb = _hosted(_sb_fwd(qn, kn, vb, f"sb_fwd{tag}", comm_sb), comm_sb)
    saved = dict(x=x, h=h, proj=proj, u3=u3, oh=oh, states=states, qn=qn, kn=kn, vb=vb, osb=osb)
    return saved, (got_proj, got_hgrn, got_sb)


def _layer_fwd_out(sv, vec, w, tag):
    x1, h2, merged, mo1 = _mix_fwd(sv["u3"], sv["oh"], sv["osb"], sv["proj"], sv["x"], vec["pv"],
                                   w["wc_t"], w["wh_t"], w["ws_t"], w["wo"], f"mix_fwd{tag}")
    x2, mo2 = _mlp_fwd(h2, x1, vec["pv"], w["w1_t"], w["w2"], f"mlp_fwd{tag}")
    sv.update(x1=x1, h2=h2, merged=merged, mo1=mo1, mo2=mo2)
    return x2


def _layer_bwd(dx2, sv, vec, w, tag, plans=None):
    plans = plans or {}
    got = {}

    def plan_for(key, big_now):
        return plans[key](big_now) if key in plans else None

    pv = vec["pv"]
    big = {}
    comm = plan_for("mlp", big)
    (dx1, da, bsq, dmo2, sg_mlp), got["mlp"] = _hosted(
        _mlp_bwd(dx2, sv["h2"], sv["x1"], sv["mo2"], pv, w["w1_t"], w["w2"], f"mlp_bwd{tag}", comm), comm)
    big["w1_t"] = _matmul(da, sv["h2"], "tn", BF16, 1024, 1024, 1024, f"dw1{tag}")
    big["w2"] = _matmul(bsq, dmo2, "tn", BF16, 1024, 1024, 1024, f"dw2{tag}")
    dmo1, dyc, dyh, dys, doc, doh, dos, dgl, sg_mix = _mix_bwd(
        dx1, sv["mo1"], sv["u3"], sv["oh"], sv["osb"], sv["proj"], pv, w["wc_t"], w["wh_t"], w["ws_t"], w["wo"], f"mix_bwd{tag}")
    big["wo"] = _matmul(sv["merged"], dmo1, "tn", BF16, 1024, 1024, 1024, f"dwo{tag}")
    big["wc_t"] = _matmul(dyc, sv["u3"], "tn", BF16, 1024, 512, 1024, f"dwc{tag}")
    big["wh_t"] = _matmul(dyh, sv["oh"], "tn", BF16, 1024, 512, 1024, f"dwh{tag}")
    big["ws_t"] = _matmul(dys, sv["osb"], "tn", BF16, 1024, 512, 1024, f"dws{tag}")
    comm = plan_for("conv", big)
    (da_c, dg_c, dconv_w, sg_conv), got["conv"] = _hosted(
        _conv_bwd(sv["proj"], doc, vec["conv_w"], vec["cp"], f"conv_bwd{tag}", comm), comm)
    comm = plan_for("hgrn", big)
    (dq_h, df_h, di_h, dg_h, dlb, dng), got["hgrn"] = _hosted(
        _hgrn_bwd(sv["proj"], sv["states"], doh, vec["lb"], vec["ng"], f"hgrn_bwd{tag}", comm), comm)
    comm = plan_for("sb", big)
    (dqn, dkn, dv_s), got["sb"] = _hosted(_sb_bwd(sv["qn"], sv["kn"], sv["vb"], dos, f"sb_bwd{tag}", comm), comm)
    dq_s, dk_s, sg_sb = _sb_prep_bwd(sv["proj"], dqn, dkn, vec["gq"], vec["gk"], f"sb_prep_bwd{tag}")
    dproj = jnp.concatenate([da_c, dg_c, dq_h, df_h, di_h, dg_h, dq_s, dk_s, dv_s, dgl], axis=1)
    half = D_MODEL // 2
    big["win_a"] = _matmul(dproj, sv["h"][:, :half], "tn", BF16, 768, half, 1024, f"dwin_a{tag}")
    comm = plan_for("dwin", big)
    big["win_b"], got["dwin"] = _hosted(
        _matmul(dproj, sv["h"][:, half:], "tn", BF16, 768, half, 1024, f"dwin_b{tag}", comm), comm)
    comm = plan_for("dh", big)
    dh, got["dh"] = _hosted(_matmul(dproj, w["win_t"], "nn", F32, 512, 1024, 1920, f"dh{tag}", comm), comm)
    dx, sg_pre = _prenorm_bwd(dh, dx1, sv["x"], pv, f"prenorm_bwd{tag}")
    small = dict(
        mod=jnp.stack([sg_pre[0], sg_pre[1], sg_mix[0], sg_mlp[1], sg_mlp[2], sg_mlp[0]]).reshape(6 * D_MODEL),
        norm1_g=sg_pre[2], norm2_g=sg_mlp[3], gate_b=sg_mix[1:4].reshape(3 * D_MODEL),
        conv_w=dconv_w, conv_b=sg_conv[0], conv_ln_g=sg_conv[1], conv_ln_b=sg_conv[2],
        hgrn_lb=dlb, hgrn_norm_g=dng[0],
        sb_qn_g=sg_sb[0].reshape(SB_HEADS, SB_DH).sum(0), sb_kn_g=sg_sb[1].reshape(SB_HEADS, SB_DH).sum(0))
    return dx, big, small, got


def _row_tile(r, cap=512):
    t = min(r, cap)
    while r % t or (t % 8 and t != r):
        t -= 1
    return t


def _sum8(z, name):
    _, r, c = z.shape
    t = _row_tile(r, 128 if c >= 1024 else 512)

    def body(z_ref, o_ref):
        acc = z_ref[0].astype(F32)
        for j in range(1, N_DEV):
            acc = acc + z_ref[j].astype(F32)
        o_ref[...] = acc

    return pl.pallas_call(
        body, grid=(r // t,), in_specs=[pl.BlockSpec((N_DEV, t, c), lambda i: (0, i, 0))],
        out_specs=pl.BlockSpec((t, c), lambda i: (i, 0)), out_shape=_sds((r, c), F32),
        compiler_params=_params("parallel"), name=name)(z)


def _adamw(w, g, m, v, name):
    r, c = w.shape
    t = _row_tile(r, 256)

    def body(w_ref, g_ref, m_ref, v_ref, d_ref, nm_ref, nv_ref):
        g_ = g_ref[...]
        nm = ADAM_B1 * m_ref[...] + (1.0 - ADAM_B1) * g_
        nv = ADAM_B2 * v_ref[...] + (1.0 - ADAM_B2) * jnp.square(g_)
        m_hat = nm / (1.0 - ADAM_B1 ** ADAM_STEP)
        v_hat = nv / (1.0 - ADAM_B2 ** ADAM_STEP)
        d_ref[...] = -ADAM_LR * (m_hat / (jnp.sqrt(v_hat) + ADAM_EPS) + ADAM_WD * w_ref[...])
        nm_ref[...] = nm
        nv_ref[...] = nv

    blk = pl.BlockSpec((t, c), lambda i: (i, 0))
    return pl.pallas_call(
        body, grid=(r // t,), in_specs=[blk] * 4, out_specs=[blk] * 3, out_shape=[_sds((r, c), F32)] * 3,
        compiler_params=_params("parallel"), name=name)(w, g, m, v)


def _mod_local(c_all, mod_w, name):
    depth, d, cols = mod_w.shape

    def body(c_ref, w_ref, o_ref):
        cv = c_ref[...]
        act = cv * jax.nn.sigmoid(cv)
        o_ref[...] = jnp.dot(act, w_ref[...], precision=lax.Precision.HIGHEST, preferred_element_type=F32)

    return pl.pallas_call(
        body, grid=(depth,),
        in_specs=[pl.BlockSpec((N_DEV, d), lambda l: (0, 0)), pl.BlockSpec((None, d, cols), lambda l: (l, 0, 0))],
        out_specs=pl.BlockSpec((None, N_DEV, cols), lambda l: (l, 0, 0)), out_shape=_sds((depth, N_DEV, cols), F32),
        compiler_params=_params("parallel"), name=name)(c_all, mod_w)


def _modw_grad(c_all, dmod, name):
    depth, _, cols = dmod.shape
    d = c_all.shape[1]

    def body(c_ref, g_ref, o_ref):
        cv = c_ref[...]
        act = cv * jax.nn.sigmoid(cv)
        o_ref[...] = lax.dot_general(act, g_ref[...], (TN, ((), ())), precision=lax.Precision.HIGHEST,
                                     preferred_element_type=F32)

    return pl.pallas_call(
        body, grid=(depth,),
        in_specs=[pl.BlockSpec((N_DEV, d), lambda l: (0, 0)), pl.BlockSpec((None, N_DEV, cols), lambda l: (l, 0, 0))],
        out_specs=pl.BlockSpec((None, d, cols), lambda l: (l, 0, 0)), out_shape=_sds((depth, d, cols), F32),
        compiler_params=_params("parallel"), name=name)(c_all, dmod)


LANE = 128
W_IN = (("w_in", 960, True),)
WIDE_REST = (("w_out", 128, False), ("mlp_w2", 512, False), ("mlp_w1", 512, True))
NARROW = (("w_conv_proj", 128, True), ("w_hgrn_proj", 128, True), ("w_sb_proj", 128, True))
BIG_KEY = {"w_in": "win_t", "w_out": "wo", "mlp_w2": "w2", "mlp_w1": "w1_t",
           "w_conv_proj": "wc_t", "w_hgrn_proj": "wh_t", "w_sb_proj": "ws_t"}
SMALL = (("mod_b", 6144), ("norm1_g", 1024), ("gate_b", 3072), ("conv_w", CONV_WIDTH * CONV_CH), ("conv_b", 512),
         ("conv_ln_g", 512), ("conv_ln_b", 512), ("hgrn_lb", 512), ("hgrn_norm_g", 128), ("sb_qn_g", 64),
         ("sb_kn_g", 64), ("norm2_g", 1024))


def _pack_rows(parts, width):
    flat = jnp.concatenate([p.reshape(-1) for p in parts])
    rows = -(-flat.shape[0] // width)
    rows = -(-rows // 8) * 8
    return jnp.pad(flat, (0, rows * width - flat.shape[0])).reshape(rows, width)


def _pack_weights(spec, params, l):
    parts = []
    for name, _, transposed in spec:
        w = params[name][l]
        parts.append((w.T if transposed else w).astype(BF16))
    return jnp.concatenate(parts, axis=0)


def _unpack_gathered(spec, g):
    out = {}
    off = 0
    for name, rows, _ in spec:
        out[BIG_KEY[name]] = g[:, off:off + rows].reshape(N_DEV * rows, g.shape[2])
        off += rows
    return out


def _pack_grads(spec, big):
    parts = []
    for name, rows, _ in spec:
        gmat = big[BIG_KEY[name]]
        parts.append(gmat.reshape(N_DEV, rows, gmat.shape[1]))
    return jnp.concatenate(parts, axis=1)


def _unpack_shard_grads(spec, gsum):
    out = {}
    off = 0
    for name, rows, transposed in spec:
        blk = gsum[off:off + rows]
        out[name] = blk.T if transposed else blk
        off += rows
    return out


def _adamw_nd(w, g, m, v, name):
    shape = w.shape
    two = lambda a: a.reshape(-1, shape[-1])
    return [o.reshape(shape) for o in _adamw(two(w), two(g), two(m), two(v), name)]


WEIGHTS = ("mod_w", "mod_b", "norm1_g", "w_in", "gate_b", "conv_w", "conv_b", "conv_ln_g", "conv_ln_b", "w_conv_proj",
           "hgrn_lb", "hgrn_norm_g", "w_hgrn_proj", "sb_qn_g", "sb_kn_g", "w_sb_proj", "w_out", "norm2_g", "mlp_w1",
           "mlp_w2")


def kernel(x, c, mod_w, mod_b, norm1_g, w_in, gate_b, conv_w, conv_b, conv_ln_g, conv_ln_b, w_conv_proj, hgrn_lb, hgrn_norm_g, w_hgrn_proj, sb_qn_g, sb_kn_g, w_sb_proj, w_out, norm2_g, mlp_w1, mlp_w2, loss_target, m_mod_w, m_mod_b, m_norm1_g, m_w_in, m_gate_b, m_conv_w, m_conv_b, m_conv_ln_g, m_conv_ln_b, m_w_conv_proj, m_hgrn_lb, m_hgrn_norm_g, m_w_hgrn_proj, m_sb_qn_g, m_sb_kn_g, m_w_sb_proj, m_w_out, m_norm2_g, m_mlp_w1, m_mlp_w2, v_mod_w, v_mod_b, v_norm1_g, v_w_in, v_gate_b, v_conv_w, v_conv_b, v_conv_ln_g, v_conv_ln_b, v_w_conv_proj, v_hgrn_lb, v_hgrn_norm_g, v_w_hgrn_proj, v_sb_qn_g, v_sb_kn_g, v_w_sb_proj, v_w_out, v_norm2_g, v_mlp_w1, v_mlp_w2):
    params = dict(mod_w=mod_w, mod_b=mod_b, norm1_g=norm1_g, w_in=w_in, gate_b=gate_b, conv_w=conv_w, conv_b=conv_b,
                  conv_ln_g=conv_ln_g, conv_ln_b=conv_ln_b, w_conv_proj=w_conv_proj, hgrn_lb=hgrn_lb,
                  hgrn_norm_g=hgrn_norm_g, w_hgrn_proj=w_hgrn_proj, sb_qn_g=sb_qn_g, sb_kn_g=sb_kn_g,
                  w_sb_proj=w_sb_proj, w_out=w_out, norm2_g=norm2_g, mlp_w1=mlp_w1, mlp_w2=mlp_w2)
    mom1 = dict(mod_w=m_mod_w, mod_b=m_mod_b, norm1_g=m_norm1_g, w_in=m_w_in, gate_b=m_gate_b, conv_w=m_conv_w,
                conv_b=m_conv_b, conv_ln_g=m_conv_ln_g, conv_ln_b=m_conv_ln_b, w_conv_proj=m_w_conv_proj,
                hgrn_lb=m_hgrn_lb, hgrn_norm_g=m_hgrn_norm_g, w_hgrn_proj=m_w_hgrn_proj, sb_qn_g=m_sb_qn_g,
                sb_kn_g=m_sb_kn_g, w_sb_proj=m_w_sb_proj, w_out=m_w_out, norm2_g=m_norm2_g, mlp_w1=m_mlp_w1,
                mlp_w2=m_mlp_w2)
    mom2 = dict(mod_w=v_mod_w, mod_b=v_mod_b, norm1_g=v_norm1_g, w_in=v_w_in, gate_b=v_gate_b, conv_w=v_conv_w,
                conv_b=v_conv_b, conv_ln_g=v_conv_ln_g, conv_ln_b=v_conv_ln_b, w_conv_proj=v_w_conv_proj,
                hgrn_lb=v_hgrn_lb, hgrn_norm_g=v_hgrn_norm_g, w_hgrn_proj=v_w_hgrn_proj, sb_qn_g=v_sb_qn_g,
                sb_kn_g=v_sb_kn_g, w_sb_proj=v_w_sb_proj, w_out=v_w_out, norm2_g=v_norm2_g, mlp_w1=v_mlp_w1,
                mlp_w2=v_mlp_w2)
    xi, yi, ci = _mesh_place()
    me = _block_of(xi, yi, ci)
    cw_cols = conv_w.shape[2]

    tiny = _pack_rows([c, conv_w], LANE)
    g_tiny, g_win0 = _comm_alone(_GatherPlan([tiny, _pack_weights(W_IN, params, 0)]), "gather_first")
    c_rows = D_MODEL // LANE
    c_all = g_tiny[:, :c_rows].reshape(N_DEV, D_MODEL)
    n_cw = DEPTH * CONV_WIDTH * cw_cols
    conv_w_full = g_tiny[:, c_rows:c_rows + n_cw // LANE].reshape(N_DEV, DEPTH, CONV_WIDTH, cw_cols)
    conv_w_full = conv_w_full.transpose(1, 2, 0, 3).reshape(DEPTH, CONV_WIDTH, CONV_CH)

    (g_mod,) = _comm_alone(_GatherPlan([_mod_local(c_all, mod_w, "mod_local")]), "gather_mod")
    mod = lax.dynamic_index_in_dim(g_mod, me, axis=2, keepdims=False)
    mod = mod.transpose(1, 0, 2).reshape(DEPTH, 6 * D_MODEL) + mod_b

    sm = dict(norm1_g=norm1_g, norm2_g=norm2_g, gate_b=gate_b, conv_w=conv_w_full, conv_b=conv_b, conv_ln_g=conv_ln_g,
              conv_ln_b=conv_ln_b, hgrn_lb=hgrn_lb, hgrn_norm_g=hgrn_norm_g, sb_qn_g=sb_qn_g, sb_kn_g=sb_kn_g)
    vecs = [_layer_vectors(l, mod, sm) for l in range(DEPTH)]

    wts = [_unpack_gathered(W_IN, g_win0), None]
    win1 = _GatherPlan([_pack_weights(W_IN, params, 1)])
    rest0 = _GatherPlan([_pack_weights(WIDE_REST, params, 0), _pack_weights(NARROW, params, 0)])
    rest1 = _GatherPlan([_pack_weights(WIDE_REST, params, 1), _pack_weights(NARROW, params, 1)])
    sv0, (got_win1, got_rest0, got_rest1) = _layer_fwd_mixers(x[0], vecs[0], wts[0]["win_t"], "_l0", win1, rest0, rest1)
    wts[0].update(_unpack_gathered(WIDE_REST, got_rest0[0]))
    wts[0].update(_unpack_gathered(NARROW, got_rest0[1]))
    wts[1] = _unpack_gathered(W_IN, got_win1[0])
    wts[1].update(_unpack_gathered(WIDE_REST, got_rest1[0]))
    wts[1].update(_unpack_gathered(NARROW, got_rest1[1]))
    y = _layer_fwd_out(sv0, vecs[0], wts[0], "_l0")
    sv1, _ = _layer_fwd_mixers(y, vecs[1], wts[1]["win_t"], "_l1")
    y = _layer_fwd_out(sv1, vecs[1], wts[1], "_l1")
    dy, sq = _loss_head(y, loss_target[0], "loss_head")
    loss = lax.psum(0.5 * jnp.sum(sq) / D_MODEL, ("x", "y", "c"))

    dy, big1, small1, _ = _layer_bwd(dy, sv1, vecs[1], wts[1], "_l1")
    by_shard = lambda g: g.reshape(N_DEV, g.shape[0] // N_DEV, g.shape[1])
    plans = dict(
        mlp=lambda big: _ExchangePlan([by_shard(big1["win_a"])]),
        conv=lambda big: _ExchangePlan([by_shard(big1["win_b"])]),
        sb=lambda big: _ExchangePlan([_pack_grads(WIDE_REST, big1), _pack_grads(NARROW, big1)]),
        hgrn=lambda big: _ExchangePlan([_pack_grads(WIDE_REST, big), _pack_grads(NARROW, big)]),
        dwin=lambda big: _ExchangePlan([by_shard(big["win_a"])]),
        dh=lambda big: _ExchangePlan([by_shard(big["win_b"])]))
    dx, _, small0, got = _layer_bwd(dy, sv0, vecs[0], wts[0], "_l0", plans)
    smalls = [small0, small1]
    shard = [{}, {}]
    for l, (key_a, key_b) in enumerate((("dwin", "dh"), ("mlp", "conv"))):
        halves = [_sum8(got[key_a][0], f"sum_grads_win_a{l}"), _sum8(got[key_b][0], f"sum_grads_win_b{l}")]
        shard[l]["w_in"] = jnp.concatenate(halves, axis=1).T
    shard[0].update(_unpack_shard_grads(WIDE_REST, _sum8(got["hgrn"][0], "sum_grads_wide0")))
    shard[0].update(_unpack_shard_grads(NARROW, _sum8(got["hgrn"][1], "sum_grads_narrow0")))
    shard[1].update(_unpack_shard_grads(WIDE_REST, _sum8(got["sb"][0], "sum_grads_wide1")))
    shard[1].update(_unpack_shard_grads(NARROW, _sum8(got["sb"][1], "sum_grads_narrow1")))
    grads = {name: jnp.stack([shard[l][name] for l in range(DEPTH)]) for name in shard[0]}

    small_parts = []
    for name, _ in SMALL:
        key = "mod" if name == "mod_b" else name
        if name == "hgrn_lb":
            small_parts.append(smalls[0][key] + smalls[1][key])
        else:
            small_parts.append(jnp.stack([smalls[l][key] for l in range(DEPTH)]))
    (g_small,) = _comm_alone(_GatherPlan([_pack_rows(small_parts, LANE)]), "gather_small_grads")
    small_sum = _sum8(g_small, "sum_small_grads").reshape(-1)
    off = 0
    for name, per_layer in SMALL:
        grads[name] = small_sum[off:off + DEPTH * per_layer].reshape(params[name].shape if name != "conv_w" else (DEPTH, CONV_WIDTH, CONV_CH))
        off += DEPTH * per_layer
    grads["conv_w"] = lax.dynamic_slice_in_dim(grads["conv_w"], me * cw_cols, cw_cols, axis=2)
    cols = mod_w.shape[2]
    dmod_all = g_small.reshape(N_DEV, -1)[:, :DEPTH * 6 * D_MODEL].reshape(N_DEV, DEPTH, 6 * D_MODEL)
    dmod_mine = lax.dynamic_slice_in_dim(dmod_all, me * cols, cols, axis=2).transpose(1, 0, 2)
    grads["mod_w"] = _modw_grad(c_all, dmod_mine, "mod_w_grad")

    delta, new_m, new_v = {}, {}, {}
    small_names = [n for n, _ in SMALL]
    for name in WEIGHTS:
        if name not in small_names:
            delta[name], new_m[name], new_v[name] = _adamw_nd(params[name], grads[name], mom1[name], mom2[name], f"adamw_{name}")
    packed = [_pack_rows([d[n] for n in small_names], LANE) for d in (params, grads, mom1, mom2)]
    outs = [o.reshape(-1) for o in _adamw(*packed, "adamw_small")]
    off = 0
    for name in small_names:
        size = params[name].size
        for dst, o in zip((delta, new_m, new_v), outs):
            dst[name] = o[off:off + size].reshape(params[name].shape)
        off += size
    return (loss, dx[None], *[grads[n] for n in WEIGHTS], *[delta[n] for n in WEIGHTS],
            *[new_m[n] for n in WEIGHTS], *[new_v[n] for n in WEIGHTS])
```

```python
import functools

import jax
import jax.numpy as jnp
import numpy as np
from jax import lax
from jax.experimental import pallas as pl
from jax.experimental.pallas import tpu as pltpu

F32 = jnp.float32
BF16 = jnp.bfloat16

D_MODEL = 1024
DEPTH = 2
N_DEV = 8
CONV_CH = 512
CONV_WIDTH = 31
CONV_HALO = 32
HG_HEADS = 4
HG_DK = 128
SB_HEADS = 8
SB_DH = 64
D_IN = 7680
D_FF = 4096
EPS = 1e-6
SB_BLK = 128
SB_DEAD = -104.0
SB_FIXED = 3
HG_CHUNK = 128

ADAM_LR = 0.001
ADAM_B1 = 0.9
ADAM_B2 = 0.999
ADAM_EPS = 1e-08
ADAM_WD = 0.01
ADAM_STEP = 10

VMEM_LIMIT = 48 * 1024 * 1024

NN = ((1,), (0,))
NT = ((1,), (1,))
TN = ((0,), (0,))
_DIMS = {"nn": NN, "nt": NT, "tn": TN}


def _sds(shape, dtype):
    return jax.ShapeDtypeStruct(shape, dtype)


def _params(*semantics):
    return pltpu.CompilerParams(dimension_semantics=semantics, vmem_limit_bytes=VMEM_LIMIT)


def _dot(a, b, dims):
    return lax.dot_general(a, b, (dims, ((), ())), preferred_element_type=F32)


@functools.partial(jax.custom_vjp, nondiff_argnums=(2,))
def _bdot(a, b, mode):
    return _dot(a.astype(BF16), b.astype(BF16), _DIMS[mode])


def _bdot_fwd(a, b, mode):
    return _bdot(a, b, mode), (a.astype(BF16), b.astype(BF16))


def _bdot_bwd(mode, res, g):
    a, b = res
    g = g.astype(BF16)
    if mode == "nn":
        return _dot(g, b, NT), _dot(a, g, TN)
    if mode == "nt":
        return _dot(g, b, NN), _dot(g, a, TN)
    return _dot(b, g, NT), _dot(a, g, NN)


_bdot.defvjp(_bdot_fwd, _bdot_bwd)


def _split(x):
    hi = x.astype(BF16)
    lo = (x - hi.astype(F32)).astype(BF16)
    return hi, lo


def _xdot_right(x, m, dims=NN):
    hi, lo = _split(x)
    if dims == NN:
        return _dot(jnp.concatenate([hi, lo], axis=1), jnp.concatenate([m, m], axis=0), NN)
    return _dot(jnp.concatenate([hi, lo], axis=1), jnp.concatenate([m, m], axis=1), NT)


def _xdot_left(m, x, dims=NN):
    hi, lo = _split(x)
    if dims == NN:
        return _dot(jnp.concatenate([m, m], axis=1), jnp.concatenate([hi, lo], axis=0), NN)
    return _dot(jnp.concatenate([m, m], axis=0), jnp.concatenate([hi, lo], axis=0), TN)


@jax.custom_vjp
def _xr(x, m):
    return _xdot_right(x, m)


def _xr_fwd(x, m):
    return _xdot_right(x, m), m


def _xr_bwd(m, g):
    return _xdot_right(g, m, NT), jnp.zeros_like(m)


_xr.defvjp(_xr_fwd, _xr_bwd)


def _norm_mod(x, g, sc, sh):
    r = lax.rsqrt(jnp.mean(x * x, axis=-1, keepdims=True) + EPS)
    return x * r * g * (1.0 + sc) + sh


MESH = pl.DeviceIdType.MESH
HBM_SPEC = pl.BlockSpec(memory_space=pltpu.HBM)


def _mesh_place():
    return lax.axis_index("x"), lax.axis_index("y"), lax.axis_index("c")


def _block_of(px, py, pc):
    return 4 * px + 2 * py + pc


def _sem_scratch(n):
    return [pltpu.SemaphoreType.DMA((n, N_DEV - 1)), pltpu.SemaphoreType.DMA((n, N_DEV - 1)), pltpu.SemaphoreType.DMA((n,))]


class _GatherPlan:
    def __init__(self, xs):
        self.xs = list(xs)
        self.n = len(self.xs)
        self.out_shape = [_sds((N_DEV, *v.shape), v.dtype) for v in self.xs]
        self.scratch = _sem_scratch(self.n)

    def _parts(self, x_refs, out_refs, sems):
        send_sems, recv_sems, local_sems = sems
        x, y, c = _mesh_place()
        me, sibling = (x, y, c), (x, y, 1 - c)
        chips = [(1 - x, y), (x, 1 - y), (1 - x, 1 - y)]

        def copy(a, k, block, to, src=None):
            rows = out_refs[a].at[_block_of(*block)]
            return pltpu.make_async_remote_copy(
                src_ref=rows if src is None else src, dst_ref=rows, send_sem=send_sems.at[a, k],
                recv_sem=recv_sems.at[a, k], device_id=to, device_id_type=MESH)

        local = [pltpu.make_async_copy(x_refs[a], out_refs[a].at[_block_of(*me)], local_sems.at[a])
                 for a in range(self.n)]
        first = []
        for a in range(self.n):
            first.append(copy(a, 0, me, sibling, src=x_refs[a]))
            first += [copy(a, 1 + j, me, (*chip, c), src=x_refs[a]) for j, chip in enumerate(chips)]
        return me, sibling, chips, c, copy, local, first

    def start(self, x_refs, out_refs, sems):
        *_, local, first = self._parts(x_refs, out_refs, sems)
        for cp in local + first:
            cp.start()

    def finish(self, x_refs, out_refs, sems):
        me, sibling, chips, c, copy, local, first = self._parts(x_refs, out_refs, sems)
        passed = []
        for j, chip in enumerate(chips):
            for a in range(self.n):
                copy(a, 1 + j, (*chip, c), me).wait_recv()
                fwd = copy(a, 4 + j, (*chip, c), sibling)
                fwd.start()
                passed.append(fwd)
        for a in range(self.n):
            copy(a, 0, sibling, me).wait_recv()
            for j, chip in enumerate(chips):
                copy(a, 4 + j, (*chip, 1 - c), me).wait_recv()
        for cp in first + passed:
            cp.wait_send()
        for cp in local:
            cp.wait()


class _ExchangePlan:
    def __init__(self, xs):
        self.xs = list(xs)
        self.n = len(self.xs)
        self.out_shape = [_sds(v.shape, v.dtype) for v in self.xs]
        self.scratch = _sem_scratch(self.n)

    def _parts(self, in_refs, out_refs, sems):
        send_sems, recv_sems, local_sems = sems
        x, y, c = _mesh_place()
        mine = _block_of(x, y, c)
        peers = [(1 - x if k & 4 else x, 1 - y if k & 2 else y, 1 - c if k & 1 else c) for k in range(1, N_DEV)]

        def copy(a, k, slot_src, slot_dst):
            return pltpu.make_async_remote_copy(
                src_ref=in_refs[a].at[slot_src], dst_ref=out_refs[a].at[slot_dst], send_sem=send_sems.at[a, k],
                recv_sem=recv_sems.at[a, k], device_id=peers[k], device_id_type=MESH)

        local = [pltpu.make_async_copy(in_refs[a].at[mine], out_refs[a].at[mine], local_sems.at[a])
                 for a in range(self.n)]
        sends = [copy(a, k, _block_of(*peers[k]), mine) for a in range(self.n) for k in range(N_DEV - 1)]
        arrivals = [copy(a, k, _block_of(*peers[k]), _block_of(*peers[k])) for a in range(self.n) for k in range(N_DEV - 1)]
        return local, sends, arrivals

    def start(self, in_refs, out_refs, sems):
        local, sends, _ = self._parts(in_refs, out_refs, sems)
        for cp in local + sends:
            cp.start()

    def finish(self, in_refs, out_refs, sems):
        local, sends, arrivals = self._parts(in_refs, out_refs, sems)
        for cp in arrivals:
            cp.wait_recv()
        for cp in sends:
            cp.wait_send()
        for cp in local:
            cp.wait()


def _call(body, args, *, grid, in_specs, out_specs, out_shape, scratch_shapes=(), semantics, name, comm=None):
    if comm is None:
        return pl.pallas_call(
            body, grid=grid, in_specs=list(in_specs), out_specs=list(out_specs), out_shape=list(out_shape),
            scratch_shapes=list(scratch_shapes), compiler_params=_params(*semantics), name=name)(*args)
    n_in, n_out, n_scr, n = len(in_specs), len(out_specs), len(scratch_shapes), comm.n

    def hosted(*refs):
        ins, rest = refs[:n_in], refs[n_in:]
        cin, rest = rest[:n], rest[n:]
        outs, rest = rest[:n_out], rest[n_out:]
        cout, rest = rest[:n], rest[n:]
        scr, sems = rest[:n_scr], rest[n_scr:]
        pids = [pl.program_id(d) for d in range(len(grid))]
        first = functools.reduce(jnp.logical_and, [p == 0 for p in pids])
        last = functools.reduce(jnp.logical_and, [p == g - 1 for p, g in zip(pids, grid)])

        @pl.when(first)
        def _():
            comm.start(cin, cout, sems)

        body(*ins, *outs, *scr)

        @pl.when(last)
        def _():
            comm.finish(cin, cout, sems)

    res = pl.pallas_call(
        hosted, grid=grid, in_specs=list(in_specs) + [HBM_SPEC] * n, out_specs=list(out_specs) + [HBM_SPEC] * n,
        out_shape=list(out_shape) + comm.out_shape, scratch_shapes=list(scratch_shapes) + comm.scratch,
        compiler_params=_params(*["arbitrary"] * len(grid)), name=name)(*args, *comm.xs)
    return res[:n_out], res[n_out:]


def _comm_alone(comm, name):
    def body(*refs):
        n = comm.n
        comm.start(refs[:n], refs[n:2 * n], refs[2 * n:])
        comm.finish(refs[:n], refs[n:2 * n], refs[2 * n:])

    return pl.pallas_call(
        body, in_specs=[HBM_SPEC] * comm.n, out_specs=[HBM_SPEC] * comm.n, out_shape=comm.out_shape,
        scratch_shapes=comm.scratch, name=name)(*comm.xs)


def _matmul(a, b, mode, out_dtype, tm, tn, tk, name, comm=None):
    if mode == "nn":
        (m, k), (_, n) = a.shape, b.shape
    elif mode == "nt":
        (m, k), (n, _) = a.shape, b.shape
    else:
        (k, m), (_, n) = a.shape, b.shape
    tm, tn, tk = min(tm, m), min(tn, n), min(tk, k)
    assert m % tm == 0 and n % tn == 0 and k % tk == 0, (name, m, n, k, tm, tn, tk)
    nk = k // tk
    dims = _DIMS[mode]

    def body(a_ref, b_ref, o_ref, acc_ref):
        if nk == 1:
            o_ref[...] = _dot(a_ref[...], b_ref[...], dims).astype(out_dtype)
            return
        kk = pl.program_id(2)

        @pl.when(kk == 0)
        def _():
            acc_ref[...] = _dot(a_ref[...], b_ref[...], dims)

        @pl.when((kk > 0) & (kk < nk - 1))
        def _():
            acc_ref[...] += _dot(a_ref[...], b_ref[...], dims)

        @pl.when(kk == nk - 1)
        def _():
            o_ref[...] = (acc_ref[...] + _dot(a_ref[...], b_ref[...], dims)).astype(out_dtype)

    if mode == "tn":
        a_spec = pl.BlockSpec((tk, tm), lambda i, j, kk: (kk, i))
        b_spec = pl.BlockSpec((tk, tn), lambda i, j, kk: (kk, j))
    elif mode == "nn":
        a_spec = pl.BlockSpec((tm, tk), lambda i, j, kk: (i, kk))
        b_spec = pl.BlockSpec((tk, tn), lambda i, j, kk: (kk, j))
    else:
        a_spec = pl.BlockSpec((tm, tk), lambda i, j, kk: (i, kk))
        b_spec = pl.BlockSpec((tn, tk), lambda i, j, kk: (j, kk))
    res = _call(
        body, (a, b), grid=(m // tm, n // tn, nk), in_specs=[a_spec, b_spec],
        out_specs=[pl.BlockSpec((tm, tn), lambda i, j, kk: (i, j))],
        out_shape=[_sds((m, n), out_dtype)], scratch_shapes=[pltpu.VMEM((tm, tn), F32)],
        semantics=("parallel", "parallel", "arbitrary"), name=name, comm=comm)
    return res[0] if comm is None else (res[0][0], res[1])


ROW_T = 512


def _prenorm(x, pv, name):
    s, d = x.shape
    t = min(ROW_T, s)

    def body(x_ref, pv_ref, h_ref):
        h = _norm_mod(x_ref[...], pv_ref[6:7, :], pv_ref[1:2, :], pv_ref[0:1, :])
        h_ref[...] = h.astype(BF16)

    return pl.pallas_call(
        body, grid=(s // t,),
        in_specs=[pl.BlockSpec((t, d), lambda i: (i, 0)), pl.BlockSpec((16, d), lambda i: (0, 0))],
        out_specs=pl.BlockSpec((t, d), lambda i: (i, 0)), out_shape=_sds((s, d), BF16),
        compiler_params=_params("parallel"), name=name)(x, pv)


def _prenorm_bwd(dh, dres, x, pv, name):
    s, d = x.shape
    t = min(ROW_T, s)

    def body(dh_ref, dres_ref, x_ref, pv_ref, dx_ref, sg_ref):
        i = pl.program_id(0)

        @pl.when(i == 0)
        def _():
            sg_ref[...] = jnp.zeros_like(sg_ref)

        _, vjp = jax.vjp(_norm_mod, x_ref[...], pv_ref[6:7, :], pv_ref[1:2, :], pv_ref[0:1, :])
        dx, dg, dsc, dsh = vjp(dh_ref[...])
        dx_ref[...] = dres_ref[...] + dx
        sg_ref[0:1, :] += dsh
        sg_ref[1:2, :] += dsc
        sg_ref[2:3, :] += dg

    row = pl.BlockSpec((t, d), lambda i: (i, 0))
    return pl.pallas_call(
        body, grid=(s // t,),
        in_specs=[row, row, row, pl.BlockSpec((16, d), lambda i: (0, 0))],
        out_specs=[row, pl.BlockSpec((8, d), lambda i: (0, 0))],
        out_shape=[_sds((s, d), F32), _sds((8, d), F32)],
        compiler_params=_params("arbitrary"), name=name)(dh, dres, x, pv)


CONV_T = 256


def _conv_tile(a_ext, g_ext, w, b, ln_g, ln_b, n_out):
    u0 = a_ext * jax.nn.sigmoid(g_ext)
    off = CONV_HALO - (CONV_WIDTH - 1)
    acc = jnp.zeros((n_out, u0.shape[1]), F32) + b
    for r in range(8):
        taps = [k for k in range(CONV_WIDTH) if (off + k) % 8 == r]
        rows = n_out if r == 0 else n_out + 8
        part = None
        for k in taps:
            lo = (off + k) // 8 * 8
            term = w[k:k + 1, :] * u0[lo: lo + rows, :]
            part = term if part is None else part + term
        acc = acc + part[r: r + n_out, :]
    mu = jnp.mean(acc, axis=-1, keepdims=True)
    var = jnp.mean(jnp.square(acc - mu), axis=-1, keepdims=True)
    y = (acc - mu) * lax.rsqrt(var + EPS) * ln_g + ln_b
    return y * jax.nn.sigmoid(y)


def _conv_fwd(proj, conv_w, cp, name):
    s = proj.shape[0]
    t = min(CONV_T, s)
    c, h = CONV_CH, CONV_HALO

    def body(ap_ref, ac_ref, gp_ref, gc_ref, w_ref, cp_ref, o_ref):
        i = pl.program_id(0)
        live = (i > 0).astype(F32)
        a_ext = jnp.concatenate([ap_ref[t - h:, :] * live, ac_ref[...]], axis=0)
        g_ext = jnp.concatenate([gp_ref[t - h:, :], gc_ref[...]], axis=0)
        u = _conv_tile(a_ext, g_ext, w_ref[...], cp_ref[0:1, :], cp_ref[1:2, :], cp_ref[2:3, :], t)
        o_ref[...] = u.astype(BF16)

    prev = lambda col: pl.BlockSpec((t, c), lambda i: (jnp.maximum(i - 1, 0), col))
    cur = lambda col: pl.BlockSpec((t, c), lambda i: (i, col))
    return pl.pallas_call(
        body, grid=(s // t,),
        in_specs=[prev(0), cur(0), prev(1), cur(1),
                  pl.BlockSpec((CONV_WIDTH, c), lambda i: (0, 0)), pl.BlockSpec((8, c), lambda i: (0, 0))],
        out_specs=pl.BlockSpec((t, c), lambda i: (i, 0)), out_shape=_sds((s, c), BF16),
        compiler_params=_params("parallel"), name=name)(proj, proj, proj, proj, conv_w, cp)


def _conv_bwd(proj, do, conv_w, cp, name, comm=None):
    s = proj.shape[0]
    t = min(CONV_T, s)
    c, h = CONV_CH, CONV_HALO
    nt = s // t

    def body(ap_ref, ac_ref, an_ref, gp_ref, gc_ref, gn_ref, doc_ref, don_ref, w_ref, cp_ref,
             da_ref, dg_ref, dw_ref, sg_ref):
        i = pl.program_id(0)

        @pl.when(i == 0)
        def _():
            dw_ref[...] = jnp.zeros_like(dw_ref)
            sg_ref[...] = jnp.zeros_like(sg_ref)

        first = (i > 0).astype(F32)
        last = (i < nt - 1).astype(F32)
        a_ext = jnp.concatenate([ap_ref[t - h:, :] * first, ac_ref[...], an_ref[:h, :] * last], axis=0)
        g_ext = jnp.concatenate([gp_ref[t - h:, :], gc_ref[...], gn_ref[:h, :]], axis=0)
        fn = functools.partial(_conv_tile, n_out=t + h)
        _, vjp = jax.vjp(fn, a_ext, g_ext, w_ref[...], cp_ref[0:1, :], cp_ref[1:2, :], cp_ref[2:3, :])
        ct_own = jnp.concatenate([doc_ref[...], jnp.zeros((h, c), F32)], axis=0)
        ct_all = jnp.concatenate([doc_ref[...], don_ref[:h, :] * last], axis=0)
        _, _, dw, db, dlg, dlb = vjp(ct_own)
        da, dg, _, _, _, _ = vjp(ct_all)
        da_ref[...] = da[h:h + t, :].astype(BF16)
        dg_ref[...] = dg[h:h + t, :].astype(BF16)
        dw_ref[...] += dw
        sg_ref[0:1, :] += db
        sg_ref[1:2, :] += dlg
        sg_ref[2:3, :] += dlb

    prev = lambda col: pl.BlockSpec((t, c), lambda i: (jnp.maximum(i - 1, 0), col))
    cur = lambda col: pl.BlockSpec((t, c), lambda i: (i, col))
    nxt = lambda col: pl.BlockSpec((t, c), lambda i: (jnp.minimum(i + 1, nt - 1), col))
    return _call(
        body, (proj, proj, proj, proj, proj, proj, do, do, conv_w, cp), grid=(nt,),
        in_specs=[prev(0), cur(0), nxt(0), prev(1), cur(1), nxt(1), cur(0), nxt(0),
                  pl.BlockSpec((CONV_WIDTH, c), lambda i: (0, 0)), pl.BlockSpec((8, c), lambda i: (0, 0))],
        out_specs=[cur(0), cur(0), pl.BlockSpec((CONV_WIDTH, c), lambda i: (0, 0)),
                   pl.BlockSpec((8, c), lambda i: (0, 0))],
        out_shape=[_sds((s, c), BF16), _sds((s, c), BF16), _sds((CONV_WIDTH, c), F32), _sds((8, c), F32)],
        semantics=("arbitrary",), name=name, comm=comm)


def _hgrn_levels(c):
    out, m = [], c // 2
    while m >= 1:
        out.append(m)
        m //= 2
    return out


def _hgrn_consts(c):
    t = np.arange(c)[:, None]
    j = np.arange(c)[None, :]
    mats = [j <= t, j > t]
    for m in _hgrn_levels(c):
        same = (t // m) == (j // m)
        mats += [same & (j <= t), same & (j > t)]
    return jnp.asarray(np.concatenate(mats, axis=0).astype(np.float32), dtype=BF16)


@jax.custom_vjp
def _cums(lc, mall):
    c = lc.shape[0]
    full = _xdot_left(mall, lc)
    return tuple(full[i * c:(i + 1) * c, :] for i in range(mall.shape[0] // c))


def _cums_fwd(lc, mall):
    return _cums(lc, mall), mall


def _cums_bwd(mall, cts):
    return _xdot_left(mall, jnp.concatenate(cts, axis=0), TN), jnp.zeros_like(mall)


_cums.defvjp(_cums_fwd, _cums_bwd)


def _hgrn_chunk(q, f, v, g, lbs, ng, sts_in, mall):
    c = q.shape[0]
    keep = jax.nn.sigmoid(-f)
    if lbs:
        keep = (1.0 - jax.nn.sigmoid(lbs[1] - lbs[0])) * keep
    lc = jnp.log1p(-keep)
    qs = q * jax.nn.sigmoid(q)
    cs = _cums(lc, mall)
    q_in = qs * jnp.exp(cs[0])
    k_out = keep * jnp.exp(cs[1])
    decay = jnp.exp(jnp.sum(lc, axis=0, keepdims=True))
    qk = qs * keep
    r = lax.broadcasted_iota(jnp.int32, q.shape, 0)
    tt = lax.broadcasted_iota(jnp.int32, (c, c), 0)
    ss = lax.broadcasted_iota(jnp.int32, (c, c), 1)
    levels = []
    for li, m in enumerate(_hgrn_levels(c)):
        lg = m.bit_length() - 1
        odd = ((r >> lg) & 1) == 1
        qm = jnp.where(odd, qs * jnp.exp(cs[2 + 2 * li]), 0.0)
        km = jnp.where(odd, 0.0, keep * jnp.exp(cs[3 + 2 * li]))
        pair = (((tt >> lg) & 1) == 1) & ((ss >> lg) == (tt >> lg) - 1)
        levels.append((qm, km, pair))
    outs, sts_out = [], []
    for h, st_in in enumerate(sts_in):
        hs = slice(h * HG_DK, (h + 1) * HG_DK)
        vh = v[:, hs]
        sc = jnp.where(tt == ss, jnp.sum(qk[:, hs], axis=-1, keepdims=True), 0.0)
        for qm, km, pair in levels:
            sc = sc + jnp.where(pair, _bdot(qm[:, hs], km[:, hs], "nt"), 0.0)
        o = _bdot(q_in[:, hs], st_in, "nt") + _bdot(sc, vh, "nn")
        sts_out.append(st_in * decay[:, hs] + _bdot(vh, k_out[:, hs], "tn"))
        outs.append(o * lax.rsqrt(jnp.mean(o * o, axis=-1, keepdims=True) + EPS) * ng)
    return jnp.concatenate(outs, axis=1) * (g * jax.nn.sigmoid(g)), tuple(sts_out)


def _hgrn_fwd(proj, lb, ng, name, comm=None):
    s = proj.shape[0]
    c = HG_CHUNK
    nc = s // c
    mall = _hgrn_consts(c)
    col0 = 1024 // (HG_HEADS * HG_DK)

    def body(*refs):
        q_ref, f_ref, v_ref, g_ref = refs[:4]
        if lb is None:
            ng_ref, m_ref, y_ref, st_ref, scr = refs[4:]
        else:
            lb_ref, ng_ref, m_ref, y_ref, st_ref, scr = refs[4:]
        ci = pl.program_id(0)

        @pl.when(ci == 0)
        def _():
            scr[...] = jnp.zeros_like(scr)

        lbs = () if lb is None else (lb_ref[0:1, :], lb_ref[1:2, :])
        sts_in = tuple(scr[h] for h in range(HG_HEADS))
        for h in range(HG_HEADS):
            st_ref[h] = sts_in[h]
        y, sts_out = _hgrn_chunk(q_ref[...], f_ref[...], v_ref[...], g_ref[...], lbs, ng_ref[...], sts_in, m_ref[...])
        y_ref[...] = y.astype(BF16)
        for h in range(HG_HEADS):
            scr[h] = sts_out[h]

    w = HG_HEADS * HG_DK
    col = lambda k: pl.BlockSpec((c, w), lambda ci: (ci, col0 + k))
    in_specs = [col(0), col(1), col(2), col(3)]
    args = [proj, proj, proj, proj]
    if lb is not None:
        in_specs.append(pl.BlockSpec((2, w), lambda ci: (0, 0)))
        args.append(lb)
    in_specs += [pl.BlockSpec((1, HG_DK), lambda ci: (0, 0)), pl.BlockSpec(mall.shape, lambda ci: (0, 0))]
    args += [ng, mall]
    return _call(
        body, args, grid=(nc,), in_specs=in_specs,
        out_specs=[pl.BlockSpec((c, w), lambda ci: (ci, 0)),
                   pl.BlockSpec((HG_HEADS, None, HG_DK, HG_DK), lambda ci: (0, ci, 0, 0))],
        out_shape=[_sds((s, w), BF16), _sds((HG_HEADS, nc, HG_DK, HG_DK), F32)],
        scratch_shapes=[pltpu.VMEM((HG_HEADS, HG_DK, HG_DK), F32)],
        semantics=("arbitrary",), name=name, comm=comm)


def _hgrn_bwd(proj, states, dy, lb, ng, name, comm=None):
    s = proj.shape[0]
    c = HG_CHUNK
    nc = s // c
    mall = _hgrn_consts(c)
    col0 = 1024 // (HG_HEADS * HG_DK)

    def body(*refs):
        q_ref, f_ref, v_ref, g_ref, st_ref, dy_ref = refs[:6]
        if lb is None:
            ng_ref, m_ref, dq_ref, df_ref, dv_ref, dg_ref, dlb_ref, dng_ref, scr = refs[6:]
        else:
            lb_ref, ng_ref, m_ref, dq_ref, df_ref, dv_ref, dg_ref, dlb_ref, dng_ref, scr = refs[6:]
        ci = pl.program_id(0)

        @pl.when(ci == 0)
        def _():
            scr[...] = jnp.zeros_like(scr)
            dlb_ref[...] = jnp.zeros_like(dlb_ref)
            dng_ref[...] = jnp.zeros_like(dng_ref)

        mall_v = m_ref[...]
        fn = lambda q, f, v, g, lbs_, ng_, sts: _hgrn_chunk(q, f, v, g, lbs_, ng_, sts, mall_v)
        lbs = () if lb is None else (lb_ref[0:1, :], lb_ref[1:2, :])
        sts_in = tuple(st_ref[h] for h in range(HG_HEADS))
        _, vjp = jax.vjp(fn, q_ref[...], f_ref[...], v_ref[...], g_ref[...], lbs, ng_ref[...], sts_in)
        dq, df, dv, dg, dlbs, dng, dsts = vjp((dy_ref[...], tuple(scr[h] for h in range(HG_HEADS))))
        dq_ref[...] = dq.astype(BF16)
        df_ref[...] = df.astype(BF16)
        dv_ref[...] = dv.astype(BF16)
        dg_ref[...] = dg.astype(BF16)
        for h in range(HG_HEADS):
            scr[h] = dsts[h]
        dng_ref[0:1, :] += dng
        if lbs:
            dlb_ref[0:1, :] += dlbs[0]
            dlb_ref[1:2, :] += dlbs[1]

    w = HG_HEADS * HG_DK
    rev = lambda ci: nc - 1 - ci
    col = lambda k: pl.BlockSpec((c, w), lambda ci: (rev(ci), col0 + k))
    out_col = pl.BlockSpec((c, w), lambda ci: (rev(ci), 0))
    in_specs = [col(0), col(1), col(2), col(3),
                pl.BlockSpec((HG_HEADS, None, HG_DK, HG_DK), lambda ci: (0, rev(ci), 0, 0)), out_col]
    args = [proj, proj, proj, proj, states, dy]
    if lb is not None:
        in_specs.append(pl.BlockSpec((2, w), lambda ci: (0, 0)))
        args.append(lb)
    in_specs += [pl.BlockSpec((1, HG_DK), lambda ci: (0, 0)), pl.BlockSpec(mall.shape, lambda ci: (0, 0))]
    args += [ng, mall]
    return _call(
        body, args, grid=(nc,), in_specs=in_specs,
        out_specs=[out_col, out_col, out_col, out_col,
                   pl.BlockSpec((2, w), lambda ci: (0, 0)), pl.BlockSpec((8, HG_DK), lambda ci: (0, 0))],
        out_shape=[_sds((s, w), BF16)] * 4 + [_sds((2, w), F32), _sds((8, HG_DK), F32)],
        scratch_shapes=[pltpu.VMEM((HG_HEADS, HG_DK, HG_DK), F32)],
        semantics=("arbitrary",), name=name, comm=comm)


def _head_avg():
    w = SB_HEADS * SB_DH
    i = np.arange(w)
    return jnp.asarray(((i[:, None] // SB_DH) == (i[None, :] // SB_DH)).astype(np.float32) / SB_DH, dtype=BF16)


def _sb_norm(x, g_tiled, avg):
    ms = _xr(x * x, avg)
    return x * lax.rsqrt(ms + EPS) * g_tiled


def _sb_prep(proj, gq, gk, name):
    s = proj.shape[0]
    t = min(ROW_T, s)
    w = SB_HEADS * SB_DH
    avg = _head_avg()

    def body(q_ref, k_ref, v_ref, gq_ref, gk_ref, avg_ref, qn_ref, kn_ref, vb_ref):
        qn_ref[...] = _sb_norm(q_ref[...], gq_ref[...], avg_ref[...]).astype(BF16)
        kn_ref[...] = _sb_norm(k_ref[...], gk_ref[...], avg_ref[...]).astype(BF16)
        vb_ref[...] = v_ref[...].astype(BF16)

    col = lambda k: pl.BlockSpec((t, w), lambda i: (i, 6 + k))
    vec = pl.BlockSpec((1, w), lambda i: (0, 0))
    out = pl.BlockSpec((t, w), lambda i: (i, 0))
    return pl.pallas_call(
        body, grid=(s // t,), in_specs=[col(0), col(1), col(2), vec, vec, pl.BlockSpec((w, w), lambda i: (0, 0))],
        out_specs=[out, out, out], out_shape=[_sds((s, w), BF16)] * 3,
        compiler_params=_params("parallel"), name=name)(proj, proj, proj, gq, gk, avg)


def _sb_prep_bwd(proj, dqn, dkn, gq, gk, name):
    s = proj.shape[0]
    t = min(ROW_T, s)
    w = SB_HEADS * SB_DH
    avg = _head_avg()

    def body(q_ref, k_ref, dqn_ref, dkn_ref, gq_ref, gk_ref, avg_ref, dq_ref, dk_ref, sg_ref):
        i = pl.program_id(0)

        @pl.when(i == 0)
        def _():
            sg_ref[...] = jnp.zeros_like(sg_ref)

        avg_v = avg_ref[...]
        fn = lambda x, g: _sb_norm(x, g, avg_v)
        _, vq = jax.vjp(fn, q_ref[...], gq_ref[...])
        dq, dgq = vq(dqn_ref[...])
        _, vk = jax.vjp(fn, k_ref[...], gk_ref[...])
        dk, dgk = vk(dkn_ref[...])
        dq_ref[...] = dq.astype(BF16)
        dk_ref[...] = dk.astype(BF16)
        sg_ref[0:1, :] += dgq
        sg_ref[1:2, :] += dgk

    col = lambda k: pl.BlockSpec((t, w), lambda i: (i, 6 + k))
    vec = pl.BlockSpec((1, w), lambda i: (0, 0))
    row = pl.BlockSpec((t, w), lambda i: (i, 0))
    return pl.pallas_call(
        body, grid=(s // t,),
        in_specs=[col(0), col(1), row, row, vec, vec, pl.BlockSpec((w, w), lambda i: (0, 0))],
        out_specs=[row, row, pl.BlockSpec((8, w), lambda i: (0, 0))],
        out_shape=[_sds((s, w), BF16), _sds((s, w), BF16), _sds((8, w), F32)],
        compiler_params=_params("arbitrary"), name=name)(proj, proj, dqn, dkn, gq, gk, avg)


def _sb_tri(kind):
    j = np.arange(SB_BLK)[:, None]
    s = np.arange(SB_BLK)[None, :]
    tri = (j > s) if kind == "suffix" else (j < s)
    return jnp.asarray(np.concatenate([tri, np.ones_like(tri)], axis=1).astype(np.float32), dtype=BF16)


def _sb_scores(qm, kblk, mask):
    z = _dot(qm, kblk, NT) * (SB_DH ** -0.5)
    sp = jnp.maximum(z, 0.0) + jnp.log(1.0 + jnp.exp(-jnp.abs(z)))
    return z, sp, jnp.where(mask, -sp, 0.0)


def _sb_setup(b):
    lane = lax.broadcasted_iota(jnp.int32, (2 * b, b), 1)
    row = lax.broadcasted_iota(jnp.int32, (2 * b, b), 0)
    mine = (row >> (b.bit_length() - 1)) == (lane >> (SB_DH.bit_length() - 1))
    return lane, row & (b - 1), mine


def _sb_fwd(qn, kn, vb, name, comm=None):
    s, w = qn.shape
    b = SB_BLK
    nq = s // b
    tri = _sb_tri("suffix")

    def body(q_ref, k_ref, v_ref, tri_ref, o_ref):
        i = pl.program_id(1)
        lane, tt, mine = _sb_setup(b)
        q = q_ref[...]
        q2 = jnp.concatenate([q, q], axis=0)
        qm = jnp.where(mine, q2, jnp.zeros_like(q2))
        tri_v = tri_ref[...]

        def block(kb, lim, run, acc):
            off = pl.multiple_of(kb * b, b)
            kblk = k_ref[pl.ds(off, b), :]
            vblk = v_ref[pl.ds(off, b), :]
            mask = lane < lim
            z, sp, lk = _sb_scores(qm, kblk, mask)
            both = _xdot_right(lk, tri_v)
            a = jnp.where(mask, jnp.exp(z - sp + both[:, :b] + run), 0.0)
            return run + both[:, b:], acc + _dot(a.astype(BF16), vblk, NN)

        offs = [pl.multiple_of(jnp.maximum(i - j, 0) * b, b) for j in range(SB_FIXED)]
        masks = [lane < (tt if j == 0 else jnp.where(i >= j, b, 0)) for j in range(SB_FIXED)]
        scores = [_sb_scores(qm, k_ref[pl.ds(off, b), :], m) for off, m in zip(offs, masks)]
        boths = [_xdot_right(lk, tri_v) for _, _, lk in scores]
        run = acc = jnp.zeros((2 * b, b), F32)
        for j in range(SB_FIXED):
            z, sp, _ = scores[j]
            a = jnp.where(masks[j], jnp.exp(z - sp + boths[j][:, :b] + run), 0.0)
            acc = acc + _dot(a.astype(BF16), v_ref[pl.ds(offs[j], b), :], NN)
            run = run + boths[j][:, b:]

        def cond(carry):
            j, run_, _ = carry
            return (j <= i) & (jnp.max(run_) > SB_DEAD)

        def step(carry):
            j, run_, acc_ = carry
            run_, acc_ = block(i - j, b, run_, acc_)
            return j + 1, run_, acc_

        _, _, acc = lax.while_loop(cond, step, (jnp.int32(SB_FIXED), run, acc))
        o_ref[...] = jnp.where(lane[:b] < SB_DH, acc[:b], acc[b:]).astype(BF16)

    blk = pl.BlockSpec((b, b), lambda p, i: (i, p))
    full = pl.BlockSpec((s, b), lambda p, i: (0, p))
    return _call(
        body, (qn, kn, vb, tri), grid=(w // b, nq),
        in_specs=[blk, full, full, pl.BlockSpec(tri.shape, lambda p, i: (0, 0))],
        out_specs=[blk], out_shape=[_sds((s, w), BF16)],
        semantics=("parallel", "arbitrary"), name=name, comm=comm)


def _sb_bwd(qn, kn, vb, do, name, comm=None):
    s, w = qn.shape
    b = SB_BLK
    nq = s // b
    tri_s = _sb_tri("suffix")
    tri_p = _sb_tri("prefix")
    scale = SB_DH ** -0.5

    def body(q_ref, k_ref, v_ref, do_ref, ts_ref, tp_ref, dq_ref, dk_ref, dv_ref, dk_acc, dv_acc, dp_scr):
        i = pl.program_id(1)

        @pl.when(i == 0)
        def _():
            dk_acc[...] = jnp.zeros_like(dk_acc)
            dv_acc[...] = jnp.zeros_like(dv_acc)

        lane, tt, mine = _sb_setup(b)
        q = q_ref[...]
        q2 = jnp.concatenate([q, q], axis=0)
        qm = jnp.where(mine, q2, jnp.zeros_like(q2))
        dout = do_ref[...].astype(BF16)
        d2 = jnp.concatenate([dout, dout], axis=0)
        dom = jnp.where(mine, d2, jnp.zeros_like(d2))
        ts_v = ts_ref[...]
        tp_v = tp_ref[...]
        zero = jnp.zeros((2 * b, b), F32)

        def down(kb, lim, run):
            off = pl.multiple_of(kb * b, b)
            kblk = k_ref[pl.ds(off, b), :]
            vblk = v_ref[pl.ds(off, b), :]
            mask = lane < lim
            z, sp, lk = _sb_scores(qm, kblk, mask)
            both = _xdot_right(lk, ts_v)
            a = jnp.where(mask, jnp.exp(z - sp + both[:, :b] + run), 0.0)
            dv_acc[pl.ds(off, b), :] += _dot(a.astype(BF16), dom, TN)
            return _dot(dom, vblk, NT) * a, run + both[:, b:]

        def up(kb, lim, dp, pre, dq):
            off = pl.multiple_of(kb * b, b)
            kblk = k_ref[pl.ds(off, b), :]
            sig = jax.nn.sigmoid(_dot(qm, kblk, NT) * scale)
            both = _xdot_right(dp, tp_v)
            dz = jnp.where(lane < lim, dp * (1.0 - sig) - sig * (both[:, :b] + pre), 0.0) * scale
            dz = dz.astype(BF16)
            dk_acc[pl.ds(off, b), :] += _dot(dz, qm, TN)
            return pre + both[:, b:], dq + _dot(dz, kblk, NN)

        offs = [pl.multiple_of(jnp.maximum(i - j, 0) * b, b) for j in range(SB_FIXED)]
        masks = [lane < (tt if j == 0 else jnp.where(i >= j, b, 0)) for j in range(SB_FIXED)]
        kblks = [k_ref[pl.ds(off, b), :] for off in offs]
        scores = [_sb_scores(qm, kblk, m) for kblk, m in zip(kblks, masks)]
        das = [_dot(dom, v_ref[pl.ds(off, b), :], NT) for off in offs]
        boths = [_xdot_right(lk, ts_v) for _, _, lk in scores]
        run = zero
        dps = []
        for j in range(SB_FIXED):
            z, sp, _ = scores[j]
            a = jnp.where(masks[j], jnp.exp(z - sp + boths[j][:, :b] + run), 0.0)
            dps.append(das[j] * a)
            dv_acc[pl.ds(offs[j], b), :] += _dot(a.astype(BF16), dom, TN)
            run = run + boths[j][:, b:]

        def cond(carry):
            j, run_ = carry
            return (j <= i) & (jnp.max(run_) > SB_DEAD)

        def sweep_down(carry):
            j, run_ = carry
            dp, run_ = down(i - j, b, run_)
            dp_scr[i - j] = dp
            return j + 1, run_

        n_live, _ = lax.while_loop(cond, sweep_down, (jnp.int32(SB_FIXED), run))

        def sweep_up(jj, carry):
            kb = i - n_live + 1 + jj
            return up(kb, b, dp_scr[kb], *carry)

        pre, dq = lax.fori_loop(0, n_live - SB_FIXED, sweep_up, (zero, zero))
        pres = [_xdot_right(dp, tp_v) for dp in dps]
        for j in reversed(range(SB_FIXED)):
            z, sp, _ = scores[j]
            sig = jnp.exp(z - sp)
            dz = jnp.where(masks[j], dps[j] * (1.0 - sig) - sig * (pres[j][:, :b] + pre), 0.0) * scale
            dz = dz.astype(BF16)
            dk_acc[pl.ds(offs[j], b), :] += _dot(dz, qm, TN)
            dq = dq + _dot(dz, kblks[j], NN)
            pre = pre + pres[j][:, b:]
        dq_ref[...] = jnp.where(lane[:b] < SB_DH, dq[:b], dq[b:])

        @pl.when(i == nq - 1)
        def _():
            dk_ref[...] = dk_acc[...]
            dv_ref[...] = dv_acc[...].astype(BF16)

    blk = pl.BlockSpec((b, b), lambda p, i: (i, p))
    full = pl.BlockSpec((s, b), lambda p, i: (0, p))
    tri = pl.BlockSpec(tri_s.shape, lambda p, i: (0, 0))
    return _call(
        body, (qn, kn, vb, do, tri_s, tri_p), grid=(w // b, nq), in_specs=[blk, full, full, blk, tri, tri],
        out_specs=[blk, full, full], out_shape=[_sds((s, w), F32), _sds((s, w), F32), _sds((s, w), BF16)],
        scratch_shapes=[pltpu.VMEM((s, b), F32), pltpu.VMEM((s, b), F32), pltpu.VMEM((nq, 2 * b, b), F32)],
        semantics=("arbitrary", "arbitrary"), name=name, comm=comm)


MIX_T = 256
HALF = 512


def _gate_slices(ga, gb):
    return [(ga[:, 0:512], ga[:, 512:1024]), (ga[:, 1024:1536], gb[:, 0:512]), (gb[:, 512:1024], gb[:, 1024:1536])]


def _mix_fwd(u3, oh, osb, proj, x, pv, wc, wh, ws, wo, name):
    s, d = x.shape
    t = min(MIX_T, s)

    def body(u3_ref, oh_ref, os_ref, ga_ref, gb_ref, x_ref, pv_ref, wc_ref, wh_ref, ws_ref, wo_ref,
             x1_ref, h2_ref, mg_ref, mo_ref):
        ys = [_dot(u3_ref[...], wc_ref[...], NT), _dot(oh_ref[...], wh_ref[...], NT), _dot(os_ref[...], ws_ref[...], NT)]
        gl = _gate_slices(ga_ref[...], gb_ref[...])
        halves = []
        for hf in range(2):
            lo = hf * HALF
            acc = jnp.zeros((t, HALF), F32)
            for br in range(3):
                gate = jax.nn.sigmoid(gl[br][hf] + pv_ref[8 + br:9 + br, lo:lo + HALF])
                acc = acc + gate * ys[br][:, lo:lo + HALF]
            halves.append(acc)
        merged = jnp.concatenate(halves, axis=1).astype(BF16)
        mg_ref[...] = merged
        mo = _dot(merged, wo_ref[...], NN)
        mo_ref[...] = mo.astype(BF16)
        x1 = x_ref[...] + pv_ref[2:3, :] * mo
        x1_ref[...] = x1
        h2_ref[...] = _norm_mod(x1, pv_ref[7:8, :], pv_ref[4:5, :], pv_ref[3:4, :]).astype(BF16)

    br_spec = pl.BlockSpec((t, CONV_CH), lambda i: (i, 0))
    row = pl.BlockSpec((t, d), lambda i: (i, 0))
    wproj = pl.BlockSpec((d, CONV_CH), lambda i: (0, 0))
    return pl.pallas_call(
        body, grid=(s // t,),
        in_specs=[br_spec, br_spec, br_spec, pl.BlockSpec((t, 1536), lambda i: (i, 3)),
                  pl.BlockSpec((t, 1536), lambda i: (i, 4)), row, pl.BlockSpec((16, d), lambda i: (0, 0)),
                  wproj, wproj, wproj, pl.BlockSpec((d, d), lambda i: (0, 0))],
        out_specs=[row, row, row, row],
        out_shape=[_sds((s, d), F32), _sds((s, d), BF16), _sds((s, d), BF16), _sds((s, d), BF16)],
        compiler_params=_params("parallel"), name=name)(u3, oh, osb, proj, proj, x, pv, wc, wh, ws, wo)


def _mix_bwd(dx1, mo1, u3, oh, osb, proj, pv, wc, wh, ws, wo, name):
    s, d = dx1.shape
    t = min(MIX_T, s)

    def body(dx_ref, mo_ref, u3_ref, oh_ref, os_ref, ga_ref, gb_ref, pv_ref, wc_ref, wh_ref, ws_ref, wo_ref,
             dmo_ref, dyc_ref, dyh_ref, dys_ref, doc_ref, doh_ref, dos_ref, dgl_ref, sg_ref):
        i = pl.program_id(0)

        @pl.when(i == 0)
        def _():
            sg_ref[...] = jnp.zeros_like(sg_ref)

        dx = dx_ref[...]
        dmo = (dx * pv_ref[2:3, :]).astype(BF16)
        dmo_ref[...] = dmo
        sg_ref[0:1, :] += jnp.sum(dx * mo_ref[...].astype(F32), axis=0, keepdims=True)
        dmerged = _dot(dmo, wo_ref[...], NT)
        branches = [(u3_ref, wc_ref, dyc_ref, doc_ref), (oh_ref, wh_ref, dyh_ref, doh_ref), (os_ref, ws_ref, dys_ref, dos_ref)]
        gl = _gate_slices(ga_ref[...], gb_ref[...])
        for br, (o_ref, w_ref, dy_ref, do_ref) in enumerate(branches):
            y = _dot(o_ref[...], w_ref[...], NT)
            dys = []
            for hf in range(2):
                lo = hf * HALF
                gate = jax.nn.sigmoid(gl[br][hf] + pv_ref[8 + br:9 + br, lo:lo + HALF])
                dm = dmerged[:, lo:lo + HALF]
                dys.append(dm * gate)
                dgl = dm * y[:, lo:lo + HALF] * gate * (1.0 - gate)
                dgl_ref[:, br * d + lo: br * d + lo + HALF] = dgl.astype(BF16)
                sg_ref[1 + br:2 + br, lo:lo + HALF] += jnp.sum(dgl, axis=0, keepdims=True)
            dy = jnp.concatenate(dys, axis=1).astype(BF16)
            dy_ref[...] = dy
            do_ref[...] = _dot(dy, w_ref[...], NN)

    br_spec = pl.BlockSpec((t, CONV_CH), lambda i: (i, 0))
    row = pl.BlockSpec((t, d), lambda i: (i, 0))
    wproj = pl.BlockSpec((d, CONV_CH), lambda i: (0, 0))
    return pl.pallas_call(
        body, grid=(s // t,),
        in_specs=[row, row, br_spec, br_spec, br_spec, pl.BlockSpec((t, 1536), lambda i: (i, 3)),
                  pl.BlockSpec((t, 1536), lambda i: (i, 4)), pl.BlockSpec((16, d), lambda i: (0, 0)),
                  wproj, wproj, wproj, pl.BlockSpec((d, d), lambda i: (0, 0))],
        out_specs=[row, row, row, row, br_spec, br_spec, br_spec, pl.BlockSpec((t, 3 * d), lambda i: (i, 0)),
                   pl.BlockSpec((8, d), lambda i: (0, 0))],
        out_shape=[_sds((s, d), BF16)] * 4 + [_sds((s, CONV_CH), F32)] * 3 + [_sds((s, 3 * d), BF16), _sds((8, d), F32)],
        compiler_params=_params("arbitrary"), name=name)(dx1, mo1, u3, oh, osb, proj, proj, pv, wc, wh, ws, wo)


MLP_T = 512
MLP_F = 512


def _mlp_fwd(h2, x1, pv, w1t, w2, name, comm=None):
    s, d = x1.shape
    t = min(MLP_T, s)
    nf = D_FF // MLP_F

    def body(h_ref, x_ref, pv_ref, w1_ref, w2_ref, x2_ref, mo_ref, acc_ref):
        f = pl.program_id(1)

        @pl.when(f == 0)
        def _():
            acc_ref[...] = jnp.zeros_like(acc_ref)

        a = jnp.maximum(_dot(h_ref[...], w1_ref[...], NT), 0.0)
        acc_ref[...] += _dot((a * a).astype(BF16), w2_ref[...], NN)

        @pl.when(f == nf - 1)
        def _():
            mo = acc_ref[...]
            mo_ref[...] = mo.astype(BF16)
            x2_ref[...] = x_ref[...] + pv_ref[5:6, :] * mo

    row = pl.BlockSpec((t, d), lambda i, f: (i, 0))
    wblk = pl.BlockSpec((MLP_F, d), lambda i, f: (f, 0))
    return _call(
        body, (h2, x1, pv, w1t, w2), grid=(s // t, nf),
        in_specs=[row, row, pl.BlockSpec((16, d), lambda i, f: (0, 0)), wblk, wblk],
        out_specs=[row, row], out_shape=[_sds((s, d), F32), _sds((s, d), BF16)],
        scratch_shapes=[pltpu.VMEM((t, d), F32)],
        semantics=("parallel", "arbitrary"), name=name, comm=comm)


def _mlp_bwd(dx2, h2, x1, mo2, pv, w1t, w2, name, comm=None):
    s, d = x1.shape
    t = min(MLP_T, s)
    nf = D_FF // MLP_F

    def body(dx_ref, h_ref, x_ref, mo_ref, pv_ref, w1_ref, w2_ref, dx1_ref, da_ref, b_ref, dmo_ref, sg_ref, acc_ref):
        i = pl.program_id(0)
        f = pl.program_id(1)

        @pl.when((i == 0) & (f == 0))
        def _():
            sg_ref[...] = jnp.zeros_like(sg_ref)

        @pl.when(f == 0)
        def _():
            acc_ref[...] = jnp.zeros_like(acc_ref)
            dx = dx_ref[...]
            dmo_ref[...] = (dx * pv_ref[5:6, :]).astype(BF16)
            sg_ref[0:1, :] += jnp.sum(dx * mo_ref[...].astype(F32), axis=0, keepdims=True)

        r = jnp.maximum(_dot(h_ref[...], w1_ref[...], NT), 0.0)
        b_ref[...] = (r * r).astype(BF16)
        da = (_dot(dmo_ref[...], w2_ref[...], NT) * (2.0 * r)).astype(BF16)
        da_ref[...] = da
        acc_ref[...] += _dot(da, w1_ref[...], NN)

        @pl.when(f == nf - 1)
        def _():
            _, vjp = jax.vjp(_norm_mod, x_ref[...], pv_ref[7:8, :], pv_ref[4:5, :], pv_ref[3:4, :])
            dxn, dg, dsc, dsh = vjp(acc_ref[...])
            dx1_ref[...] = dx_ref[...] + dxn
            sg_ref[1:2, :] += dsh
            sg_ref[2:3, :] += dsc
            sg_ref[3:4, :] += dg

    row = pl.BlockSpec((t, d), lambda i, f: (i, 0))
    wblk = pl.BlockSpec((MLP_F, d), lambda i, f: (f, 0))
    hid = pl.BlockSpec((t, MLP_F), lambda i, f: (i, f))
    return _call(
        body, (dx2, h2, x1, mo2, pv, w1t, w2), grid=(s // t, nf),
        in_specs=[row, row, row, row, pl.BlockSpec((16, d), lambda i, f: (0, 0)), wblk, wblk],
        out_specs=[row, hid, hid, row, pl.BlockSpec((8, d), lambda i, f: (0, 0))],
        out_shape=[_sds((s, d), F32), _sds((s, D_FF), BF16), _sds((s, D_FF), BF16), _sds((s, d), BF16), _sds((8, d), F32)],
        scratch_shapes=[pltpu.VMEM((t, d), F32)],
        semantics=("arbitrary", "arbitrary"), name=name, comm=comm)


def _loss_head(y, target, name):
    s, d = y.shape
    t = min(ROW_T, s)

    def body(y_ref, t_ref, dy_ref, ls_ref):
        i = pl.program_id(0)

        @pl.when(i == 0)
        def _():
            ls_ref[...] = jnp.zeros_like(ls_ref)

        e = y_ref[...] - t_ref[...]
        dy_ref[...] = e * (1.0 / d)
        ls_ref[...] += jnp.sum((e * e).reshape(t // 8, 8, d), axis=0)

    row = pl.BlockSpec((t, d), lambda i: (i, 0))
    return pl.pallas_call(
        body, grid=(s // t,), in_specs=[row, row], out_specs=[row, pl.BlockSpec((8, d), lambda i: (0, 0))],
        out_shape=[_sds((s, d), F32), _sds((8, d), F32)],
        compiler_params=_params("arbitrary"), name=name)(y, target)


def _layer_vectors(l, mod, sm):
    d = D_MODEL
    pv = jnp.concatenate([mod[l].reshape(6, d), sm["norm1_g"][l][None], sm["norm2_g"][l][None],
                          sm["gate_b"][l].reshape(3, d), jnp.zeros((5, d), F32)], axis=0)
    cp = jnp.concatenate([sm["conv_b"][l][None], sm["conv_ln_g"][l][None], sm["conv_ln_b"][l][None],
                          jnp.zeros((5, CONV_CH), F32)], axis=0)
    return dict(pv=pv, cp=cp, conv_w=sm["conv_w"][l], lb=(sm["hgrn_lb"] if l > 0 else None),
                ng=sm["hgrn_norm_g"][l][None], gq=jnp.tile(sm["sb_qn_g"][l], SB_HEADS)[None],
                gk=jnp.tile(sm["sb_kn_g"][l], SB_HEADS)[None])


def _hosted(res, comm):
    return res if comm is not None else (res, None)


def _layer_fwd_mixers(x, vec, win_t, tag, comm_proj=None, comm_hgrn=None, comm_sb=None):
    h = _prenorm(x, vec["pv"], f"prenorm{tag}")
    proj, got_proj = _hosted(_matmul(h, win_t, "nt", F32, 1024, 768, 1024, f"proj{tag}", comm_proj), comm_proj)
    u3 = _conv_fwd(proj, vec["conv_w"], vec["cp"], f"conv_fwd{tag}")
    (oh, states), got_hgrn = _hosted(_hgrn_fwd(proj, vec["lb"], vec["ng"], f"hgrn_fwd{tag}", comm_hgrn), comm_hgrn)
    qn, kn, vb = _sb_prep(proj, vec["gq"], vec["gk"], f"sb_prep{tag}")
    (osb,), got_sb = _hosted(_sb_fwd(qn, kn, vb, f"sb_fwd{tag}", comm_sb), comm_sb)
    saved = dict(x=x, h=h, proj=proj, u3=u3, oh=oh, states=states, qn=qn, kn=kn, vb=vb, osb=osb)
    return saved, (got_proj, got_hgrn, got_sb)


def _layer_fwd_out(sv, vec, w, tag, comm_mlp=None):
    x1, h2, merged, mo1 = _mix_fwd(sv["u3"], sv["oh"], sv["osb"], sv["proj"], sv["x"], vec["pv"],
                                   w["wc_t"], w["wh_t"], w["ws_t"], w["wo"], f"mix_fwd{tag}")
    (x2, mo2), got = _hosted(_mlp_fwd(h2, x1, vec["pv"], w["w1_t"], w["w2"], f"mlp_fwd{tag}", comm_mlp), comm_mlp)
    sv.update(x1=x1, h2=h2, merged=merged, mo1=mo1, mo2=mo2)
    return x2, got


def _layer_bwd(dx2, sv, vec, w, tag, plans=None):
    plans = plans or {}
    got = {}

    def plan_for(key, big_now):
        return plans[key](big_now) if key in plans else None

    pv = vec["pv"]
    big = {}
    comm = plan_for("mlp", big)
    (dx1, da, bsq, dmo2, sg_mlp), got["mlp"] = _hosted(
        _mlp_bwd(dx2, sv["h2"], sv["x1"], sv["mo2"], pv, w["w1_t"], w["w2"], f"mlp_bwd{tag}", comm), comm)
    big["w1_t"] = _matmul(da, sv["h2"], "tn", BF16, 1024, 1024, 1024, f"dw1{tag}")
    big["w2"] = _matmul(bsq, dmo2, "tn", BF16, 1024, 1024, 1024, f"dw2{tag}")
    dmo1, dyc, dyh, dys, doc, doh, dos, dgl, sg_mix = _mix_bwd(
        dx1, sv["mo1"], sv["u3"], sv["oh"], sv["osb"], sv["proj"], pv, w["wc_t"], w["wh_t"], w["ws_t"], w["wo"], f"mix_bwd{tag}")
    big["wo"] = _matmul(sv["merged"], dmo1, "tn", BF16, 1024, 1024, 1024, f"dwo{tag}")
    big["wc_t"] = _matmul(dyc, sv["u3"], "tn", BF16, 1024, 512, 1024, f"dwc{tag}")
    big["wh_t"] = _matmul(dyh, sv["oh"], "tn", BF16, 1024, 512, 1024, f"dwh{tag}")
    big["ws_t"] = _matmul(dys, sv["osb"], "tn", BF16, 1024, 512, 1024, f"dws{tag}")
    comm = plan_for("conv", big)
    (da_c, dg_c, dconv_w, sg_conv), got["conv"] = _hosted(
        _conv_bwd(sv["proj"], doc, vec["conv_w"], vec["cp"], f"conv_bwd{tag}", comm), comm)
    comm = plan_for("hgrn", big)
    (dq_h, df_h, di_h, dg_h, dlb, dng), got["hgrn"] = _hosted(
        _hgrn_bwd(sv["proj"], sv["states"], doh, vec["lb"], vec["ng"], f"hgrn_bwd{tag}", comm), comm)
    comm = plan_for("sb", big)
    (dqn, dkn, dv_s), got["sb"] = _hosted(_sb_bwd(sv["qn"], sv["kn"], sv["vb"], dos, f"sb_bwd{tag}", comm), comm)
    dq_s, dk_s, sg_sb = _sb_prep_bwd(sv["proj"], dqn, dkn, vec["gq"], vec["gk"], f"sb_prep_bwd{tag}")
    dproj = jnp.concatenate([da_c, dg_c, dq_h, df_h, di_h, dg_h, dq_s, dk_s, dv_s, dgl], axis=1)
    half = D_MODEL // 2
    comm = plan_for("dwin_a", big)
    big["win_a"], got["dwin_a"] = _hosted(
        _matmul(dproj, sv["h"][:, :half], "tn", BF16, 768, half, 1024, f"dwin_a{tag}", comm), comm)
    comm = plan_for("dwin", big)
    big["win_b"], got["dwin"] = _hosted(
        _matmul(dproj, sv["h"][:, half:], "tn", BF16, 768, half, 1024, f"dwin_b{tag}", comm), comm)
    comm = plan_for("dh", big)
    dh, got["dh"] = _hosted(_matmul(dproj, w["win_t"], "nn", F32, 512, 1024, 1920, f"dh{tag}", comm), comm)
    dx, sg_pre = _prenorm_bwd(dh, dx1, sv["x"], pv, f"prenorm_bwd{tag}")
    small = dict(
        mod=jnp.stack([sg_pre[0], sg_pre[1], sg_mix[0], sg_mlp[1], sg_mlp[2], sg_mlp[0]]).reshape(6 * D_MODEL),
        norm1_g=sg_pre[2], norm2_g=sg_mlp[3], gate_b=sg_mix[1:4].reshape(3 * D_MODEL),
        conv_w=dconv_w, conv_b=sg_conv[0], conv_ln_g=sg_conv[1], conv_ln_b=sg_conv[2],
        hgrn_lb=dlb, hgrn_norm_g=dng[0],
        sb_qn_g=sg_sb[0].reshape(SB_HEADS, SB_DH).sum(0), sb_kn_g=sg_sb[1].reshape(SB_HEADS, SB_DH).sum(0))
    return dx, big, small, got


def _row_tile(r, cap=512):
    t = min(r, cap)
    while r % t or (t % 8 and t != r):
        t -= 1
    return t


def _sum8(z, name):
    _, r, c = z.shape
    t = _row_tile(r, 128 if c >= 1024 else 512)

    def body(z_ref, o_ref):
        acc = z_ref[0].astype(F32)
        for j in range(1, N_DEV):
            acc = acc + z_ref[j].astype(F32)
        o_ref[...] = acc

    return pl.pallas_call(
        body, grid=(r // t,), in_specs=[pl.BlockSpec((N_DEV, t, c), lambda i: (0, i, 0))],
        out_specs=pl.BlockSpec((t, c), lambda i: (i, 0)), out_shape=_sds((r, c), F32),
        compiler_params=_params("parallel"), name=name)(z)


def _adamw(w, g, m, v, name):
    r, c = w.shape
    t = _row_tile(r, 256)

    def body(w_ref, g_ref, m_ref, v_ref, d_ref, nm_ref, nv_ref):
        g_ = g_ref[...]
        nm = ADAM_B1 * m_ref[...] + (1.0 - ADAM_B1) * g_
        nv = ADAM_B2 * v_ref[...] + (1.0 - ADAM_B2) * jnp.square(g_)
        m_hat = nm / (1.0 - ADAM_B1 ** ADAM_STEP)
        v_hat = nv / (1.0 - ADAM_B2 ** ADAM_STEP)
        d_ref[...] = -ADAM_LR * (m_hat / (jnp.sqrt(v_hat) + ADAM_EPS) + ADAM_WD * w_ref[...])
        nm_ref[...] = nm
        nv_ref[...] = nv

    blk = pl.BlockSpec((t, c), lambda i: (i, 0))
    return pl.pallas_call(
        body, grid=(r // t,), in_specs=[blk] * 4, out_specs=[blk] * 3, out_shape=[_sds((r, c), F32)] * 3,
        compiler_params=_params("parallel"), name=name)(w, g, m, v)


def _mod_local(c_all, mod_w, name):
    depth, d, cols = mod_w.shape

    def body(c_ref, w_ref, o_ref):
        cv = c_ref[...]
        act = cv * jax.nn.sigmoid(cv)
        o_ref[...] = jnp.dot(act, w_ref[...], precision=lax.Precision.HIGHEST, preferred_element_type=F32)

    return pl.pallas_call(
        body, grid=(depth,),
        in_specs=[pl.BlockSpec((N_DEV, d), lambda l: (0, 0)), pl.BlockSpec((None, d, cols), lambda l: (l, 0, 0))],
        out_specs=pl.BlockSpec((None, N_DEV, cols), lambda l: (l, 0, 0)), out_shape=_sds((depth, N_DEV, cols), F32),
        compiler_params=_params("parallel"), name=name)(c_all, mod_w)


def _modw_grad(c_all, dmod, name):
    depth, _, cols = dmod.shape
    d = c_all.shape[1]

    def body(c_ref, g_ref, o_ref):
        cv = c_ref[...]
        act = cv * jax.nn.sigmoid(cv)
        o_ref[...] = lax.dot_general(act, g_ref[...], (TN, ((), ())), precision=lax.Precision.HIGHEST,
                                     preferred_element_type=F32)

    return pl.pallas_call(
        body, grid=(depth,),
        in_specs=[pl.BlockSpec((N_DEV, d), lambda l: (0, 0)), pl.BlockSpec((None, N_DEV, cols), lambda l: (l, 0, 0))],
        out_specs=pl.BlockSpec((None, d, cols), lambda l: (l, 0, 0)), out_shape=_sds((depth, d, cols), F32),
        compiler_params=_params("parallel"), name=name)(c_all, dmod)


LANE = 128
BIG = {"w_in": ("win_t", True), "w_out": ("wo", False), "mlp_w2": ("w2", False), "mlp_w1": ("w1_t", True),
       "w_conv_proj": ("wc_t", True), "w_hgrn_proj": ("wh_t", True), "w_sb_proj": ("ws_t", True)}
PROJS = ("w_conv_proj", "w_hgrn_proj", "w_sb_proj")
SMALL = (("mod_b", 6144), ("norm1_g", 1024), ("gate_b", 3072), ("conv_w", CONV_WIDTH * CONV_CH), ("conv_b", 512),
         ("conv_ln_g", 512), ("conv_ln_b", 512), ("hgrn_lb", 512), ("hgrn_norm_g", 128), ("sb_qn_g", 64),
         ("sb_kn_g", 64), ("norm2_g", 1024))


def _pack_rows(parts, width):
    flat = jnp.concatenate([p.reshape(-1) for p in parts])
    rows = -(-flat.shape[0] // width)
    rows = -(-rows // 8) * 8
    return jnp.pad(flat, (0, rows * width - flat.shape[0])).reshape(rows, width)


def _shards(params, items):
    return [(params[n][l].T if BIG[n][1] else params[n][l]).astype(BF16) for n, l in items]


def _gathered(items, got):
    return {BIG[n][0]: g.reshape(-1, g.shape[2]) for (n, _), g in zip(items, got)}


def _by_shard(g):
    return g.reshape(N_DEV, g.shape[0] // N_DEV, g.shape[1])


def _adamw_nd(w, g, m, v, name):
    shape = w.shape
    two = lambda a: a.reshape(-1, shape[-1])
    return [o.reshape(shape) for o in _adamw(two(w), two(g), two(m), two(v), name)]


WEIGHTS = ("mod_w", "mod_b", "norm1_g", "w_in", "gate_b", "conv_w", "conv_b", "conv_ln_g", "conv_ln_b", "w_conv_proj",
           "hgrn_lb", "hgrn_norm_g", "w_hgrn_proj", "sb_qn_g", "sb_kn_g", "w_sb_proj", "w_out", "norm2_g", "mlp_w1",
           "mlp_w2")


def kernel(x, c, mod_w, mod_b, norm1_g, w_in, gate_b, conv_w, conv_b, conv_ln_g, conv_ln_b, w_conv_proj, hgrn_lb, hgrn_norm_g, w_hgrn_proj, sb_qn_g, sb_kn_g, w_sb_proj, w_out, norm2_g, mlp_w1, mlp_w2, loss_target, m_mod_w, m_mod_b, m_norm1_g, m_w_in, m_gate_b, m_conv_w, m_conv_b, m_conv_ln_g, m_conv_ln_b, m_w_conv_proj, m_hgrn_lb, m_hgrn_norm_g, m_w_hgrn_proj, m_sb_qn_g, m_sb_kn_g, m_w_sb_proj, m_w_out, m_norm2_g, m_mlp_w1, m_mlp_w2, v_mod_w, v_mod_b, v_norm1_g, v_w_in, v_gate_b, v_conv_w, v_conv_b, v_conv_ln_g, v_conv_ln_b, v_w_conv_proj, v_hgrn_lb, v_hgrn_norm_g, v_w_hgrn_proj, v_sb_qn_g, v_sb_kn_g, v_w_sb_proj, v_w_out, v_norm2_g, v_mlp_w1, v_mlp_w2):
    params = dict(mod_w=mod_w, mod_b=mod_b, norm1_g=norm1_g, w_in=w_in, gate_b=gate_b, conv_w=conv_w, conv_b=conv_b,
                  conv_ln_g=conv_ln_g, conv_ln_b=conv_ln_b, w_conv_proj=w_conv_proj, hgrn_lb=hgrn_lb,
                  hgrn_norm_g=hgrn_norm_g, w_hgrn_proj=w_hgrn_proj, sb_qn_g=sb_qn_g, sb_kn_g=sb_kn_g,
                  w_sb_proj=w_sb_proj, w_out=w_out, norm2_g=norm2_g, mlp_w1=mlp_w1, mlp_w2=mlp_w2)
    mom1 = dict(mod_w=m_mod_w, mod_b=m_mod_b, norm1_g=m_norm1_g, w_in=m_w_in, gate_b=m_gate_b, conv_w=m_conv_w,
                conv_b=m_conv_b, conv_ln_g=m_conv_ln_g, conv_ln_b=m_conv_ln_b, w_conv_proj=m_w_conv_proj,
                hgrn_lb=m_hgrn_lb, hgrn_norm_g=m_hgrn_norm_g, w_hgrn_proj=m_w_hgrn_proj, sb_qn_g=m_sb_qn_g,
                sb_kn_g=m_sb_kn_g, w_sb_proj=m_w_sb_proj, w_out=m_w_out, norm2_g=m_norm2_g, mlp_w1=m_mlp_w1,
                mlp_w2=m_mlp_w2)
    mom2 = dict(mod_w=v_mod_w, mod_b=v_mod_b, norm1_g=v_norm1_g, w_in=v_w_in, gate_b=v_gate_b, conv_w=v_conv_w,
                conv_b=v_conv_b, conv_ln_g=v_conv_ln_g, conv_ln_b=v_conv_ln_b, w_conv_proj=v_w_conv_proj,
                hgrn_lb=v_hgrn_lb, hgrn_norm_g=v_hgrn_norm_g, w_hgrn_proj=v_w_hgrn_proj, sb_qn_g=v_sb_qn_g,
                sb_kn_g=v_sb_kn_g, w_sb_proj=v_w_sb_proj, w_out=v_w_out, norm2_g=v_norm2_g, mlp_w1=v_mlp_w1,
                mlp_w2=v_mlp_w2)
    xi, yi, ci = _mesh_place()
    me = _block_of(xi, yi, ci)
    cw_cols = conv_w.shape[2]

    tiny = _pack_rows([c, conv_w], LANE)
    g_tiny, g_win0 = _comm_alone(_GatherPlan([tiny] + _shards(params, [("w_in", 0)])), "gather_first")
    c_rows = D_MODEL // LANE
    c_all = g_tiny[:, :c_rows].reshape(N_DEV, D_MODEL)
    n_cw = DEPTH * CONV_WIDTH * cw_cols
    conv_w_full = g_tiny[:, c_rows:c_rows + n_cw // LANE].reshape(N_DEV, DEPTH, CONV_WIDTH, cw_cols)
    conv_w_full = conv_w_full.transpose(1, 2, 0, 3).reshape(DEPTH, CONV_WIDTH, CONV_CH)

    (g_mod,) = _comm_alone(_GatherPlan([_mod_local(c_all, mod_w, "mod_local")]), "gather_mod")
    mod = lax.dynamic_index_in_dim(g_mod, me, axis=2, keepdims=False)
    mod = mod.transpose(1, 0, 2).reshape(DEPTH, 6 * D_MODEL) + mod_b

    sm = dict(norm1_g=norm1_g, norm2_g=norm2_g, gate_b=gate_b, conv_w=conv_w_full, conv_b=conv_b, conv_ln_g=conv_ln_g,
              conv_ln_b=conv_ln_b, hgrn_lb=hgrn_lb, hgrn_norm_g=hgrn_norm_g, sb_qn_g=sb_qn_g, sb_kn_g=sb_kn_g)
    vecs = [_layer_vectors(l, mod, sm) for l in range(DEPTH)]

    fwd_hosts = dict(
        proj=[("w_in", 1)],
        hgrn=[("mlp_w2", 0), ("w_out", 0)] + [(n, 0) for n in PROJS],
        sb=[("mlp_w1", 0), ("mlp_w2", 1)],
        mlp=[("mlp_w1", 1), ("w_out", 1)] + [(n, 1) for n in PROJS])
    gather = {host: _GatherPlan(_shards(params, items)) for host, items in fwd_hosts.items()}
    wts = [_gathered([("w_in", 0)], [g_win0]), {}]
    sv0, got_mixers = _layer_fwd_mixers(x[0], vecs[0], wts[0]["win_t"], "_l0", gather["proj"], gather["hgrn"], gather["sb"])
    for host, got in zip(("proj", "hgrn", "sb"), got_mixers):
        for (name, l), g in zip(fwd_hosts[host], got):
            wts[l].update(_gathered([(name, l)], [g]))
    y, got = _layer_fwd_out(sv0, vecs[0], wts[0], "_l0", gather["mlp"])
    wts[1].update(_gathered(fwd_hosts["mlp"], got))
    sv1, _ = _layer_fwd_mixers(y, vecs[1], wts[1]["win_t"], "_l1")
    y, _ = _layer_fwd_out(sv1, vecs[1], wts[1], "_l1")
    dy, sq = _loss_head(y, loss_target[0], "loss_head")
    loss = lax.psum(0.5 * jnp.sum(sq) / D_MODEL, ("x", "y", "c"))

    half = D_MODEL // 2
    dy, big1, small1, _ = _layer_bwd(dy, sv1, vecs[1], wts[1], "_l1")
    bwd_hosts = dict(
        mlp=[(1, "win_a"), (1, "wo"), (1, "wc_t"), (1, "wh_t"), (1, "ws_t")],
        conv=[(1, "win_b"), (0, "wc_t"), (0, "wh_t"), (0, "ws_t")],
        hgrn=[(1, "w2"), (0, "wo"), (0, "w2_a")],
        sb=[(1, "w1_t"), (0, "w1_t")],
        dwin_a=[(0, "w2_b")],
        dwin=[(0, "win_a")],
        dh=[(0, "win_b")])

    def source(l, key, big0):
        big = big1 if l == 1 else big0
        if key in ("w2_a", "w2_b"):
            return big["w2"][:, :half] if key == "w2_a" else big["w2"][:, half:]
        return big[key]

    plans = {host: (lambda big0, items=items: _ExchangePlan([_by_shard(source(l, k, big0)) for l, k in items]))
             for host, items in bwd_hosts.items()}
    dx, _, small0, got = _layer_bwd(dy, sv0, vecs[0], wts[0], "_l0", plans)
    smalls = [small0, small1]
    summed = {}
    for host, items in bwd_hosts.items():
        for (l, key), arrived in zip(items, got[host]):
            summed[l, key] = _sum8(arrived, f"sum_{key}_l{l}")
    summed[0, "w2"] = jnp.concatenate([summed[0, "w2_a"], summed[0, "w2_b"]], axis=1)
    grads = {}
    for name, (key, transposed) in BIG.items():
        per_layer = []
        for l in range(DEPTH):
            if name == "w_in":
                blk = jnp.concatenate([summed[l, "win_a"], summed[l, "win_b"]], axis=1)
            else:
                blk = summed[l, key]
            per_layer.append(blk.T if transposed else blk)
        grads[name] = jnp.stack(per_layer)

    small_parts = []
    for name, _ in SMALL:
        key = "mod" if name == "mod_b" else name
        if name == "hgrn_lb":
            small_parts.append(smalls[0][key] + smalls[1][key])
        else:
            small_parts.append(jnp.stack([smalls[l][key] for l in range(DEPTH)]))
    (g_small,) = _comm_alone(_GatherPlan([_pack_rows(small_parts, LANE)]), "gather_small_grads")
    small_sum = _sum8(g_small, "sum_small_grads").reshape(-1)
    off = 0
    for name, per_layer in SMALL:
        grads[name] = small_sum[off:off + DEPTH * per_layer].reshape(params[name].shape if name != "conv_w" else (DEPTH, CONV_WIDTH, CONV_CH))
        off += DEPTH * per_layer
    grads["conv_w"] = lax.dynamic_slice_in_dim(grads["conv_w"], me * cw_cols, cw_cols, axis=2)
    cols = mod_w.shape[2]
    dmod_all = g_small.reshape(N_DEV, -1)[:, :DEPTH * 6 * D_MODEL].reshape(N_DEV, DEPTH, 6 * D_MODEL)
    dmod_mine = lax.dynamic_slice_in_dim(dmod_all, me * cols, cols, axis=2).transpose(1, 0, 2)
    grads["mod_w"] = _modw_grad(c_all, dmod_mine, "mod_w_grad")

    delta, new_m, new_v = {}, {}, {}
    small_names = [n for n, _ in SMALL]
    for name in WEIGHTS:
        if name not in small_names:
            delta[name], new_m[name], new_v[name] = _adamw_nd(params[name], grads[name], mom1[name], mom2[name], f"adamw_{name}")
    packed = [_pack_rows([d[n] for n in small_names], LANE) for d in (params, grads, mom1, mom2)]
    outs = [o.reshape(-1) for o in _adamw(*packed, "adamw_small")]
    off = 0
    for name in small_names:
        size = params[name].size
        for dst, o in zip((delta, new_m, new_v), outs):
            dst[name] = o[off:off + size].reshape(params[name].shape)
        off += size
    return (loss, dx[None], *[grads[n] for n in WEIGHTS], *[delta[n] for n in WEIGHTS],
            *[new_m[n] for n in WEIGHTS], *[new_v[n] for n in WEIGHTS])
```

```python
import functools

import jax
import jax.numpy as jnp
import numpy as np
from jax import lax
from jax.experimental import pallas as pl
from jax.experimental.pallas import tpu as pltpu

F32 = jnp.float32
BF16 = jnp.bfloat16

D_MODEL = 1024
DEPTH = 2
N_DEV = 8
CONV_CH = 512
CONV_WIDTH = 31
CONV_HALO = 32
HG_HEADS = 4
HG_DK = 128
SB_HEADS = 8
SB_DH = 64
D_IN = 7680
D_FF = 4096
EPS = 1e-6
SB_BLK = 128
SB_DEAD = -104.0
SB_FIXED = 3
HG_CHUNK = 128

ADAM_LR = 0.001
ADAM_B1 = 0.9
ADAM_B2 = 0.999
ADAM_EPS = 1e-08
ADAM_WD = 0.01
ADAM_STEP = 10

VMEM_LIMIT = 48 * 1024 * 1024

NN = ((1,), (0,))
NT = ((1,), (1,))
TN = ((0,), (0,))
_DIMS = {"nn": NN, "nt": NT, "tn": TN}


def _sds(shape, dtype):
    return jax.ShapeDtypeStruct(shape, dtype)


def _params(*semantics):
    return pltpu.CompilerParams(dimension_semantics=semantics, vmem_limit_bytes=VMEM_LIMIT)


def _dot(a, b, dims):
    return lax.dot_general(a, b, (dims, ((), ())), preferred_element_type=F32)


@functools.partial(jax.custom_vjp, nondiff_argnums=(2,))
def _bdot(a, b, mode):
    return _dot(a.astype(BF16), b.astype(BF16), _DIMS[mode])


def _bdot_fwd(a, b, mode):
    return _bdot(a, b, mode), (a.astype(BF16), b.astype(BF16))


def _bdot_bwd(mode, res, g):
    a, b = res
    g = g.astype(BF16)
    if mode == "nn":
        return _dot(g, b, NT), _dot(a, g, TN)
    if mode == "nt":
        return _dot(g, b, NN), _dot(g, a, TN)
    return _dot(b, g, NT), _dot(a, g, NN)


_bdot.defvjp(_bdot_fwd, _bdot_bwd)


def _split(x):
    hi = x.astype(BF16)
    lo = (x - hi.astype(F32)).astype(BF16)
    return hi, lo


def _xdot_right(x, m, dims=NN):
    hi, lo = _split(x)
    if dims == NN:
        return _dot(jnp.concatenate([hi, lo], axis=1), jnp.concatenate([m, m], axis=0), NN)
    return _dot(jnp.concatenate([hi, lo], axis=1), jnp.concatenate([m, m], axis=1), NT)


def _xdot_left(m, x, dims=NN):
    hi, lo = _split(x)
    if dims == NN:
        return _dot(jnp.concatenate([m, m], axis=1), jnp.concatenate([hi, lo], axis=0), NN)
    return _dot(jnp.concatenate([m, m], axis=0), jnp.concatenate([hi, lo], axis=0), TN)


@jax.custom_vjp
def _xr(x, m):
    return _xdot_right(x, m)


def _xr_fwd(x, m):
    return _xdot_right(x, m), m


def _xr_bwd(m, g):
    return _xdot_right(g, m, NT), jnp.zeros_like(m)


_xr.defvjp(_xr_fwd, _xr_bwd)


def _norm_mod(x, g, sc, sh):
    r = lax.rsqrt(jnp.mean(x * x, axis=-1, keepdims=True) + EPS)
    return x * r * g * (1.0 + sc) + sh


MESH = pl.DeviceIdType.MESH
HBM_SPEC = pl.BlockSpec(memory_space=pltpu.HBM)


def _mesh_place():
    return lax.axis_index("x"), lax.axis_index("y"), lax.axis_index("c")


def _block_of(px, py, pc):
    return 4 * px + 2 * py + pc


def _sem_scratch(n):
    return [pltpu.SemaphoreType.DMA((n, N_DEV - 1)), pltpu.SemaphoreType.DMA((n, N_DEV - 1)), pltpu.SemaphoreType.DMA((n,))]


class _GatherPlan:
    def __init__(self, xs):
        self.xs = list(xs)
        self.n = len(self.xs)
        self.out_shape = [_sds((N_DEV, *v.shape), v.dtype) for v in self.xs]
        self.scratch = _sem_scratch(self.n)

    def _parts(self, x_refs, out_refs, sems):
        send_sems, recv_sems, local_sems = sems
        x, y, c = _mesh_place()
        me, sibling = (x, y, c), (x, y, 1 - c)
        chips = [(1 - x, y), (x, 1 - y), (1 - x, 1 - y)]

        def copy(a, k, block, to, src=None):
            rows = out_refs[a].at[_block_of(*block)]
            return pltpu.make_async_remote_copy(
                src_ref=rows if src is None else src, dst_ref=rows, send_sem=send_sems.at[a, k],
                recv_sem=recv_sems.at[a, k], device_id=to, device_id_type=MESH)

        local = [pltpu.make_async_copy(x_refs[a], out_refs[a].at[_block_of(*me)], local_sems.at[a])
                 for a in range(self.n)]
        first = []
        for a in range(self.n):
            first.append(copy(a, 0, me, sibling, src=x_refs[a]))
            first += [copy(a, 1 + j, me, (*chip, c), src=x_refs[a]) for j, chip in enumerate(chips)]
        return me, sibling, chips, c, copy, local, first

    def start(self, x_refs, out_refs, sems):
        *_, local, first = self._parts(x_refs, out_refs, sems)
        for cp in local + first:
            cp.start()

    def finish(self, x_refs, out_refs, sems):
        me, sibling, chips, c, copy, local, first = self._parts(x_refs, out_refs, sems)
        passed = []
        for j, chip in enumerate(chips):
            for a in range(self.n):
                copy(a, 1 + j, (*chip, c), me).wait_recv()
                fwd = copy(a, 4 + j, (*chip, c), sibling)
                fwd.start()
                passed.append(fwd)
        for a in range(self.n):
            copy(a, 0, sibling, me).wait_recv()
            for j, chip in enumerate(chips):
                copy(a, 4 + j, (*chip, 1 - c), me).wait_recv()
        for cp in first + passed:
            cp.wait_send()
        for cp in local:
            cp.wait()


class _ExchangePlan:
    def __init__(self, xs):
        self.xs = list(xs)
        self.n = len(self.xs)
        self.out_shape = [_sds(v.shape, v.dtype) for v in self.xs]
        self.scratch = _sem_scratch(self.n)

    def _parts(self, in_refs, out_refs, sems):
        send_sems, recv_sems, local_sems = sems
        x, y, c = _mesh_place()
        mine = _block_of(x, y, c)
        peers = [(1 - x if k & 4 else x, 1 - y if k & 2 else y, 1 - c if k & 1 else c) for k in range(1, N_DEV)]

        def copy(a, k, slot_src, slot_dst):
            return pltpu.make_async_remote_copy(
                src_ref=in_refs[a].at[slot_src], dst_ref=out_refs[a].at[slot_dst], send_sem=send_sems.at[a, k],
                recv_sem=recv_sems.at[a, k], device_id=peers[k], device_id_type=MESH)

        local = [pltpu.make_async_copy(in_refs[a].at[mine], out_refs[a].at[mine], local_sems.at[a])
                 for a in range(self.n)]
        sends = [copy(a, k, _block_of(*peers[k]), mine) for a in range(self.n) for k in range(N_DEV - 1)]
        arrivals = [copy(a, k, _block_of(*peers[k]), _block_of(*peers[k])) for a in range(self.n) for k in range(N_DEV - 1)]
        return local, sends, arrivals

    def start(self, in_refs, out_refs, sems):
        local, sends, _ = self._parts(in_refs, out_refs, sems)
        for cp in local + sends:
            cp.start()

    def finish(self, in_refs, out_refs, sems):
        local, sends, arrivals = self._parts(in_refs, out_refs, sems)
        for cp in arrivals:
            cp.wait_recv()
        for cp in sends:
            cp.wait_send()
        for cp in local:
            cp.wait()


def _call(body, args, *, grid, in_specs, out_specs, out_shape, scratch_shapes=(), semantics, name, comm=None):
    if comm is None:
        return pl.pallas_call(
            body, grid=grid, in_specs=list(in_specs), out_specs=list(out_specs), out_shape=list(out_shape),
            scratch_shapes=list(scratch_shapes), compiler_params=_params(*semantics), name=name)(*args)
    n_in, n_out, n_scr, n = len(in_specs), len(out_specs), len(scratch_shapes), comm.n

    def hosted(*refs):
        ins, rest = refs[:n_in], refs[n_in:]
        cin, rest = rest[:n], rest[n:]
        outs, rest = rest[:n_out], rest[n_out:]
        cout, rest = rest[:n], rest[n:]
        scr, sems = rest[:n_scr], rest[n_scr:]
        pids = [pl.program_id(d) for d in range(len(grid))]
        first = functools.reduce(jnp.logical_and, [p == 0 for p in pids])
        last = functools.reduce(jnp.logical_and, [p == g - 1 for p, g in zip(pids, grid)])

        @pl.when(first)
        def _():
            comm.start(cin, cout, sems)

        body(*ins, *outs, *scr)

        @pl.when(last)
        def _():
            comm.finish(cin, cout, sems)

    res = pl.pallas_call(
        hosted, grid=grid, in_specs=list(in_specs) + [HBM_SPEC] * n, out_specs=list(out_specs) + [HBM_SPEC] * n,
        out_shape=list(out_shape) + comm.out_shape, scratch_shapes=list(scratch_shapes) + comm.scratch,
        compiler_params=_params(*["arbitrary"] * len(grid)), name=name)(*args, *comm.xs)
    return res[:n_out], res[n_out:]


def _comm_alone(comm, name):
    def body(*refs):
        n = comm.n
        comm.start(refs[:n], refs[n:2 * n], refs[2 * n:])
        comm.finish(refs[:n], refs[n:2 * n], refs[2 * n:])

    return pl.pallas_call(
        body, in_specs=[HBM_SPEC] * comm.n, out_specs=[HBM_SPEC] * comm.n, out_shape=comm.out_shape,
        scratch_shapes=comm.scratch, name=name)(*comm.xs)


def _matmul(a, b, mode, out_dtype, tm, tn, tk, name, comm=None):
    if mode == "nn":
        (m, k), (_, n) = a.shape, b.shape
    elif mode == "nt":
        (m, k), (n, _) = a.shape, b.shape
    else:
        (k, m), (_, n) = a.shape, b.shape
    tm, tn, tk = min(tm, m), min(tn, n), min(tk, k)
    assert m % tm == 0 and n % tn == 0 and k % tk == 0, (name, m, n, k, tm, tn, tk)
    nk = k // tk
    dims = _DIMS[mode]

    def body(a_ref, b_ref, o_ref, acc_ref):
        if nk == 1:
            o_ref[...] = _dot(a_ref[...], b_ref[...], dims).astype(out_dtype)
            return
        kk = pl.program_id(2)

        @pl.when(kk == 0)
        def _():
            acc_ref[...] = _dot(a_ref[...], b_ref[...], dims)

        @pl.when((kk > 0) & (kk < nk - 1))
        def _():
            acc_ref[...] += _dot(a_ref[...], b_ref[...], dims)

        @pl.when(kk == nk - 1)
        def _():
            o_ref[...] = (acc_ref[...] + _dot(a_ref[...], b_ref[...], dims)).astype(out_dtype)

    if mode == "tn":
        a_spec = pl.BlockSpec((tk, tm), lambda i, j, kk: (kk, i))
        b_spec = pl.BlockSpec((tk, tn), lambda i, j, kk: (kk, j))
    elif mode == "nn":
        a_spec = pl.BlockSpec((tm, tk), lambda i, j, kk: (i, kk))
        b_spec = pl.BlockSpec((tk, tn), lambda i, j, kk: (kk, j))
    else:
        a_spec = pl.BlockSpec((tm, tk), lambda i, j, kk: (i, kk))
        b_spec = pl.BlockSpec((tn, tk), lambda i, j, kk: (j, kk))
    res = _call(
        body, (a, b), grid=(m // tm, n // tn, nk), in_specs=[a_spec, b_spec],
        out_specs=[pl.BlockSpec((tm, tn), lambda i, j, kk: (i, j))],
        out_shape=[_sds((m, n), out_dtype)], scratch_shapes=[pltpu.VMEM((tm, tn), F32)],
        semantics=("parallel", "parallel", "arbitrary"), name=name, comm=comm)
    return res[0] if comm is None else (res[0][0], res[1])


ROW_T = 512


def _prenorm(x, pv, name):
    s, d = x.shape
    t = min(ROW_T, s)

    def body(x_ref, pv_ref, h_ref):
        h = _norm_mod(x_ref[...], pv_ref[6:7, :], pv_ref[1:2, :], pv_ref[0:1, :])
        h_ref[...] = h.astype(BF16)

    return pl.pallas_call(
        body, grid=(s // t,),
        in_specs=[pl.BlockSpec((t, d), lambda i: (i, 0)), pl.BlockSpec((16, d), lambda i: (0, 0))],
        out_specs=pl.BlockSpec((t, d), lambda i: (i, 0)), out_shape=_sds((s, d), BF16),
        compiler_params=_params("parallel"), name=name)(x, pv)


DH_TK = 1920


def _dh_prenorm_bwd(dproj, win_t, dres, x, pv, name, comm=None):
    s, k = dproj.shape
    d = x.shape[1]
    t = min(ROW_T, s)
    tk = min(DH_TK, k)
    nk = k // tk

    def body(a_ref, b_ref, dres_ref, x_ref, pv_ref, dx_ref, sg_ref, acc_ref):
        i = pl.program_id(0)
        kk = pl.program_id(1)

        @pl.when((i == 0) & (kk == 0))
        def _():
            sg_ref[...] = jnp.zeros_like(sg_ref)

        @pl.when(kk == 0)
        def _():
            acc_ref[...] = _dot(a_ref[...], b_ref[...], NN)

        @pl.when((kk > 0) & (kk < nk - 1))
        def _():
            acc_ref[...] += _dot(a_ref[...], b_ref[...], NN)

        @pl.when(kk == nk - 1)
        def _():
            dh = acc_ref[...] + _dot(a_ref[...], b_ref[...], NN)
            _, vjp = jax.vjp(_norm_mod, x_ref[...], pv_ref[6:7, :], pv_ref[1:2, :], pv_ref[0:1, :])
            dx, dg, dsc, dsh = vjp(dh)
            dx_ref[...] = dres_ref[...] + dx
            sg_ref[0:1, :] += dsh
            sg_ref[1:2, :] += dsc
            sg_ref[2:3, :] += dg

    assert nk >= 2 and k % tk == 0, (k, tk)
    row = pl.BlockSpec((t, d), lambda i, kk: (i, 0))
    return _call(
        body, (dproj, win_t, dres, x, pv), grid=(s // t, nk),
        in_specs=[pl.BlockSpec((t, tk), lambda i, kk: (i, kk)), pl.BlockSpec((tk, d), lambda i, kk: (kk, 0)),
                  row, row, pl.BlockSpec((16, d), lambda i, kk: (0, 0))],
        out_specs=[row, pl.BlockSpec((8, d), lambda i, kk: (0, 0))],
        out_shape=[_sds((s, d), F32), _sds((8, d), F32)], scratch_shapes=[pltpu.VMEM((t, d), F32)],
        semantics=("arbitrary", "arbitrary"), name=name, comm=comm)


CONV_T = 256


def _conv_tile(a_ext, g_ext, w, b, ln_g, ln_b, n_out):
    u0 = a_ext * jax.nn.sigmoid(g_ext)
    off = CONV_HALO - (CONV_WIDTH - 1)
    acc = jnp.zeros((n_out, u0.shape[1]), F32) + b
    for r in range(8):
        taps = [k for k in range(CONV_WIDTH) if (off + k) % 8 == r]
        rows = n_out if r == 0 else n_out + 8
        part = None
        for k in taps:
            lo = (off + k) // 8 * 8
            term = w[k:k + 1, :] * u0[lo: lo + rows, :]
            part = term if part is None else part + term
        acc = acc + part[r: r + n_out, :]
    mu = jnp.mean(acc, axis=-1, keepdims=True)
    var = jnp.mean(jnp.square(acc - mu), axis=-1, keepdims=True)
    y = (acc - mu) * lax.rsqrt(var + EPS) * ln_g + ln_b
    return y * jax.nn.sigmoid(y)


def _conv_fwd(proj, conv_w, cp, name):
    s = proj.shape[0]
    t = min(CONV_T, s)
    c, h = CONV_CH, CONV_HALO

    def body(ap_ref, ac_ref, gp_ref, gc_ref, w_ref, cp_ref, o_ref):
        i = pl.program_id(0)
        live = (i > 0).astype(F32)
        a_ext = jnp.concatenate([ap_ref[t - h:, :] * live, ac_ref[...]], axis=0)
        g_ext = jnp.concatenate([gp_ref[t - h:, :], gc_ref[...]], axis=0)
        u = _conv_tile(a_ext, g_ext, w_ref[...], cp_ref[0:1, :], cp_ref[1:2, :], cp_ref[2:3, :], t)
        o_ref[...] = u.astype(BF16)

    prev = lambda col: pl.BlockSpec((t, c), lambda i: (jnp.maximum(i - 1, 0), col))
    cur = lambda col: pl.BlockSpec((t, c), lambda i: (i, col))
    return pl.pallas_call(
        body, grid=(s // t,),
        in_specs=[prev(0), cur(0), prev(1), cur(1),
                  pl.BlockSpec((CONV_WIDTH, c), lambda i: (0, 0)), pl.BlockSpec((8, c), lambda i: (0, 0))],
        out_specs=pl.BlockSpec((t, c), lambda i: (i, 0)), out_shape=_sds((s, c), BF16),
        compiler_params=_params("parallel"), name=name)(proj, proj, proj, proj, conv_w, cp)


def _conv_bwd(proj, do, conv_w, cp, name, comm=None):
    s = proj.shape[0]
    t = min(CONV_T, s)
    c, h = CONV_CH, CONV_HALO
    nt = s // t

    def body(ap_ref, ac_ref, an_ref, gp_ref, gc_ref, gn_ref, doc_ref, don_ref, w_ref, cp_ref,
             da_ref, dg_ref, dw_ref, sg_ref):
        i = pl.program_id(0)

        @pl.when(i == 0)
        def _():
            dw_ref[...] = jnp.zeros_like(dw_ref)
            sg_ref[...] = jnp.zeros_like(sg_ref)

        first = (i > 0).astype(F32)
        last = (i < nt - 1).astype(F32)
        a_ext = jnp.concatenate([ap_ref[t - h:, :] * first, ac_ref[...], an_ref[:h, :] * last], axis=0)
        g_ext = jnp.concatenate([gp_ref[t - h:, :], gc_ref[...], gn_ref[:h, :]], axis=0)
        fn = functools.partial(_conv_tile, n_out=t + h)
        _, vjp = jax.vjp(fn, a_ext, g_ext, w_ref[...], cp_ref[0:1, :], cp_ref[1:2, :], cp_ref[2:3, :])
        ct_own = jnp.concatenate([doc_ref[...], jnp.zeros((h, c), F32)], axis=0)
        ct_all = jnp.concatenate([doc_ref[...], don_ref[:h, :] * last], axis=0)
        _, _, dw, db, dlg, dlb = vjp(ct_own)
        da, dg, _, _, _, _ = vjp(ct_all)
        da_ref[...] = da[h:h + t, :].astype(BF16)
        dg_ref[...] = dg[h:h + t, :].astype(BF16)
        dw_ref[...] += dw
        sg_ref[0:1, :] += db
        sg_ref[1:2, :] += dlg
        sg_ref[2:3, :] += dlb

    prev = lambda col: pl.BlockSpec((t, c), lambda i: (jnp.maximum(i - 1, 0), col))
    cur = lambda col: pl.BlockSpec((t, c), lambda i: (i, col))
    nxt = lambda col: pl.BlockSpec((t, c), lambda i: (jnp.minimum(i + 1, nt - 1), col))
    return _call(
        body, (proj, proj, proj, proj, proj, proj, do, do, conv_w, cp), grid=(nt,),
        in_specs=[prev(0), cur(0), nxt(0), prev(1), cur(1), nxt(1), cur(0), nxt(0),
                  pl.BlockSpec((CONV_WIDTH, c), lambda i: (0, 0)), pl.BlockSpec((8, c), lambda i: (0, 0))],
        out_specs=[cur(0), cur(0), pl.BlockSpec((CONV_WIDTH, c), lambda i: (0, 0)),
                   pl.BlockSpec((8, c), lambda i: (0, 0))],
        out_shape=[_sds((s, c), BF16), _sds((s, c), BF16), _sds((CONV_WIDTH, c), F32), _sds((8, c), F32)],
        semantics=("arbitrary",), name=name, comm=comm)


def _hgrn_levels(c):
    out, m = [], c // 2
    while m >= 1:
        out.append(m)
        m //= 2
    return out


def _hgrn_consts(c):
    t = np.arange(c)[:, None]
    j = np.arange(c)[None, :]
    mats = [j <= t, j > t]
    for m in _hgrn_levels(c):
        same = (t // m) == (j // m)
        mats += [same & (j <= t), same & (j > t)]
    return jnp.asarray(np.concatenate(mats, axis=0).astype(np.float32), dtype=BF16)


@jax.custom_vjp
def _cums(lc, mall):
    c = lc.shape[0]
    full = _xdot_left(mall, lc)
    return tuple(full[i * c:(i + 1) * c, :] for i in range(mall.shape[0] // c))


def _cums_fwd(lc, mall):
    return _cums(lc, mall), mall


def _cums_bwd(mall, cts):
    return _xdot_left(mall, jnp.concatenate(cts, axis=0), TN), jnp.zeros_like(mall)


_cums.defvjp(_cums_fwd, _cums_bwd)


def _hgrn_chunk(q, f, v, g, lbs, ng, sts_in, mall):
    c = q.shape[0]
    keep = jax.nn.sigmoid(-f)
    if lbs:
        keep = (1.0 - jax.nn.sigmoid(lbs[1] - lbs[0])) * keep
    lc = jnp.log1p(-keep)
    qs = q * jax.nn.sigmoid(q)
    cs = _cums(lc, mall)
    q_in = qs * jnp.exp(cs[0])
    k_out = keep * jnp.exp(cs[1])
    decay = jnp.exp(jnp.sum(lc, axis=0, keepdims=True))
    qk = qs * keep
    r = lax.broadcasted_iota(jnp.int32, q.shape, 0)
    tt = lax.broadcasted_iota(jnp.int32, (c, c), 0)
    ss = lax.broadcasted_iota(jnp.int32, (c, c), 1)
    levels = []
    for li, m in enumerate(_hgrn_levels(c)):
        lg = m.bit_length() - 1
        odd = ((r >> lg) & 1) == 1
        qm = jnp.where(odd, qs * jnp.exp(cs[2 + 2 * li]), 0.0)
        km = jnp.where(odd, 0.0, keep * jnp.exp(cs[3 + 2 * li]))
        pair = (((tt >> lg) & 1) == 1) & ((ss >> lg) == (tt >> lg) - 1)
        levels.append((qm, km, pair))
    outs, sts_out = [], []
    for h, st_in in enumerate(sts_in):
        hs = slice(h * HG_DK, (h + 1) * HG_DK)
        vh = v[:, hs]
        sc = jnp.where(tt == ss, jnp.sum(qk[:, hs], axis=-1, keepdims=True), 0.0)
        for qm, km, pair in levels:
            sc = sc + jnp.where(pair, _bdot(qm[:, hs], km[:, hs], "nt"), 0.0)
        o = _bdot(q_in[:, hs], st_in, "nt") + _bdot(sc, vh, "nn")
        sts_out.append(st_in * decay[:, hs] + _bdot(vh, k_out[:, hs], "tn"))
        outs.append(o * lax.rsqrt(jnp.mean(o * o, axis=-1, keepdims=True) + EPS) * ng)
    return jnp.concatenate(outs, axis=1) * (g * jax.nn.sigmoid(g)), tuple(sts_out)


def _hgrn_fwd(proj, lb, ng, name, comm=None):
    s = proj.shape[0]
    c = HG_CHUNK
    nc = s // c
    mall = _hgrn_consts(c)
    col0 = 1024 // (HG_HEADS * HG_DK)

    def body(*refs):
        q_ref, f_ref, v_ref, g_ref = refs[:4]
        if lb is None:
            ng_ref, m_ref, y_ref, st_ref, scr = refs[4:]
        else:
            lb_ref, ng_ref, m_ref, y_ref, st_ref, scr = refs[4:]
        ci = pl.program_id(0)

        @pl.when(ci == 0)
        def _():
            scr[...] = jnp.zeros_like(scr)

        lbs = () if lb is None else (lb_ref[0:1, :], lb_ref[1:2, :])
        sts_in = tuple(scr[h] for h in range(HG_HEADS))
        for h in range(HG_HEADS):
            st_ref[h] = sts_in[h]
        y, sts_out = _hgrn_chunk(q_ref[...], f_ref[...], v_ref[...], g_ref[...], lbs, ng_ref[...], sts_in, m_ref[...])
        y_ref[...] = y.astype(BF16)
        for h in range(HG_HEADS):
            scr[h] = sts_out[h]

    w = HG_HEADS * HG_DK
    col = lambda k: pl.BlockSpec((c, w), lambda ci: (ci, col0 + k))
    in_specs = [col(0), col(1), col(2), col(3)]
    args = [proj, proj, proj, proj]
    if lb is not None:
        in_specs.append(pl.BlockSpec((2, w), lambda ci: (0, 0)))
        args.append(lb)
    in_specs += [pl.BlockSpec((1, HG_DK), lambda ci: (0, 0)), pl.BlockSpec(mall.shape, lambda ci: (0, 0))]
    args += [ng, mall]
    return _call(
        body, args, grid=(nc,), in_specs=in_specs,
        out_specs=[pl.BlockSpec((c, w), lambda ci: (ci, 0)),
                   pl.BlockSpec((HG_HEADS, None, HG_DK, HG_DK), lambda ci: (0, ci, 0, 0))],
        out_shape=[_sds((s, w), BF16), _sds((HG_HEADS, nc, HG_DK, HG_DK), F32)],
        scratch_shapes=[pltpu.VMEM((HG_HEADS, HG_DK, HG_DK), F32)],
        semantics=("arbitrary",), name=name, comm=comm)


def _hgrn_bwd(proj, states, dy, lb, ng, name, comm=None):
    s = proj.shape[0]
    c = HG_CHUNK
    nc = s // c
    mall = _hgrn_consts(c)
    col0 = 1024 // (HG_HEADS * HG_DK)

    def body(*refs):
        q_ref, f_ref, v_ref, g_ref, st_ref, dy_ref = refs[:6]
        if lb is None:
            ng_ref, m_ref, dq_ref, df_ref, dv_ref, dg_ref, dlb_ref, dng_ref, scr = refs[6:]
        else:
            lb_ref, ng_ref, m_ref, dq_ref, df_ref, dv_ref, dg_ref, dlb_ref, dng_ref, scr = refs[6:]
        ci = pl.program_id(0)

        @pl.when(ci == 0)
        def _():
            scr[...] = jnp.zeros_like(scr)
            dlb_ref[...] = jnp.zeros_like(dlb_ref)
            dng_ref[...] = jnp.zeros_like(dng_ref)

        mall_v = m_ref[...]
        fn = lambda q, f, v, g, lbs_, ng_, sts: _hgrn_chunk(q, f, v, g, lbs_, ng_, sts, mall_v)
        lbs = () if lb is None else (lb_ref[0:1, :], lb_ref[1:2, :])
        sts_in = tuple(st_ref[h] for h in range(HG_HEADS))
        _, vjp = jax.vjp(fn, q_ref[...], f_ref[...], v_ref[...], g_ref[...], lbs, ng_ref[...], sts_in)
        dq, df, dv, dg, dlbs, dng, dsts = vjp((dy_ref[...], tuple(scr[h] for h in range(HG_HEADS))))
        dq_ref[...] = dq.astype(BF16)
        df_ref[...] = df.astype(BF16)
        dv_ref[...] = dv.astype(BF16)
        dg_ref[...] = dg.astype(BF16)
        for h in range(HG_HEADS):
            scr[h] = dsts[h]
        dng_ref[0:1, :] += dng
        if lbs:
            dlb_ref[0:1, :] += dlbs[0]
            dlb_ref[1:2, :] += dlbs[1]

    w = HG_HEADS * HG_DK
    rev = lambda ci: nc - 1 - ci
    col = lambda k: pl.BlockSpec((c, w), lambda ci: (rev(ci), col0 + k))
    out_col = pl.BlockSpec((c, w), lambda ci: (rev(ci), 0))
    in_specs = [col(0), col(1), col(2), col(3),
                pl.BlockSpec((HG_HEADS, None, HG_DK, HG_DK), lambda ci: (0, rev(ci), 0, 0)), out_col]
    args = [proj, proj, proj, proj, states, dy]
    if lb is not None:
        in_specs.append(pl.BlockSpec((2, w), lambda ci: (0, 0)))
        args.append(lb)
    in_specs += [pl.BlockSpec((1, HG_DK), lambda ci: (0, 0)), pl.BlockSpec(mall.shape, lambda ci: (0, 0))]
    args += [ng, mall]
    return _call(
        body, args, grid=(nc,), in_specs=in_specs,
        out_specs=[out_col, out_col, out_col, out_col,
                   pl.BlockSpec((2, w), lambda ci: (0, 0)), pl.BlockSpec((8, HG_DK), lambda ci: (0, 0))],
        out_shape=[_sds((s, w), BF16)] * 4 + [_sds((2, w), F32), _sds((8, HG_DK), F32)],
        scratch_shapes=[pltpu.VMEM((HG_HEADS, HG_DK, HG_DK), F32)],
        semantics=("arbitrary",), name=name, comm=comm)


def _head_avg():
    w = SB_HEADS * SB_DH
    i = np.arange(w)
    return jnp.asarray(((i[:, None] // SB_DH) == (i[None, :] // SB_DH)).astype(np.float32) / SB_DH, dtype=BF16)


def _sb_norm(x, g_tiled, avg):
    ms = _xr(x * x, avg)
    return x * lax.rsqrt(ms + EPS) * g_tiled


def _sb_prep(proj, gq, gk, name):
    s = proj.shape[0]
    t = min(ROW_T, s)
    w = SB_HEADS * SB_DH
    avg = _head_avg()

    def body(q_ref, k_ref, v_ref, gq_ref, gk_ref, avg_ref, qn_ref, kn_ref, vb_ref):
        qn_ref[...] = _sb_norm(q_ref[...], gq_ref[...], avg_ref[...]).astype(BF16)
        kn_ref[...] = _sb_norm(k_ref[...], gk_ref[...], avg_ref[...]).astype(BF16)
        vb_ref[...] = v_ref[...].astype(BF16)

    col = lambda k: pl.BlockSpec((t, w), lambda i: (i, 6 + k))
    vec = pl.BlockSpec((1, w), lambda i: (0, 0))
    out = pl.BlockSpec((t, w), lambda i: (i, 0))
    return pl.pallas_call(
        body, grid=(s // t,), in_specs=[col(0), col(1), col(2), vec, vec, pl.BlockSpec((w, w), lambda i: (0, 0))],
        out_specs=[out, out, out], out_shape=[_sds((s, w), BF16)] * 3,
        compiler_params=_params("parallel"), name=name)(proj, proj, proj, gq, gk, avg)


def _sb_prep_bwd(proj, dqn, dkn, gq, gk, name):
    s = proj.shape[0]
    t = min(ROW_T, s)
    w = SB_HEADS * SB_DH
    avg = _head_avg()

    def body(q_ref, k_ref, dqn_ref, dkn_ref, gq_ref, gk_ref, avg_ref, dq_ref, dk_ref, sg_ref):
        i = pl.program_id(0)

        @pl.when(i == 0)
        def _():
            sg_ref[...] = jnp.zeros_like(sg_ref)

        avg_v = avg_ref[...]
        fn = lambda x, g: _sb_norm(x, g, avg_v)
        _, vq = jax.vjp(fn, q_ref[...], gq_ref[...])
        dq, dgq = vq(dqn_ref[...])
        _, vk = jax.vjp(fn, k_ref[...], gk_ref[...])
        dk, dgk = vk(dkn_ref[...])
        dq_ref[...] = dq.astype(BF16)
        dk_ref[...] = dk.astype(BF16)
        sg_ref[0:1, :] += dgq
        sg_ref[1:2, :] += dgk

    col = lambda k: pl.BlockSpec((t, w), lambda i: (i, 6 + k))
    vec = pl.BlockSpec((1, w), lambda i: (0, 0))
    row = pl.BlockSpec((t, w), lambda i: (i, 0))
    return pl.pallas_call(
        body, grid=(s // t,),
        in_specs=[col(0), col(1), row, row, vec, vec, pl.BlockSpec((w, w), lambda i: (0, 0))],
        out_specs=[row, row, pl.BlockSpec((8, w), lambda i: (0, 0))],
        out_shape=[_sds((s, w), BF16), _sds((s, w), BF16), _sds((8, w), F32)],
        compiler_params=_params("arbitrary"), name=name)(proj, proj, dqn, dkn, gq, gk, avg)


def _sb_tri(kind):
    j = np.arange(SB_BLK)[:, None]
    s = np.arange(SB_BLK)[None, :]
    tri = (j > s) if kind == "suffix" else (j < s)
    return jnp.asarray(np.concatenate([tri, np.ones_like(tri)], axis=1).astype(np.float32), dtype=BF16)


def _sb_scores(qm, kblk, mask):
    z = _dot(qm, kblk, NT) * (SB_DH ** -0.5)
    sp = jnp.maximum(z, 0.0) + jnp.log(1.0 + jnp.exp(-jnp.abs(z)))
    return z, sp, jnp.where(mask, -sp, 0.0)


def _sb_setup(b):
    lane = lax.broadcasted_iota(jnp.int32, (2 * b, b), 1)
    row = lax.broadcasted_iota(jnp.int32, (2 * b, b), 0)
    mine = (row >> (b.bit_length() - 1)) == (lane >> (SB_DH.bit_length() - 1))
    return lane, row & (b - 1), mine


def _sb_fwd(qn, kn, vb, name, comm=None):
    s, w = qn.shape
    b = SB_BLK
    nq = s // b
    tri = _sb_tri("suffix")

    def body(q_ref, k_ref, v_ref, tri_ref, o_ref):
        i = pl.program_id(1)
        lane, tt, mine = _sb_setup(b)
        q = q_ref[...]
        q2 = jnp.concatenate([q, q], axis=0)
        qm = jnp.where(mine, q2, jnp.zeros_like(q2))
        tri_v = tri_ref[...]

        def block(kb, lim, run, acc):
            off = pl.multiple_of(kb * b, b)
            kblk = k_ref[pl.ds(off, b), :]
            vblk = v_ref[pl.ds(off, b), :]
            mask = lane < lim
            z, sp, lk = _sb_scores(qm, kblk, mask)
            both = _xdot_right(lk, tri_v)
            a = jnp.where(mask, jnp.exp(z - sp + both[:, :b] + run), 0.0)
            return run + both[:, b:], acc + _dot(a.astype(BF16), vblk, NN)

        offs = [pl.multiple_of(jnp.maximum(i - j, 0) * b, b) for j in range(SB_FIXED)]
        masks = [lane < (tt if j == 0 else jnp.where(i >= j, b, 0)) for j in range(SB_FIXED)]
        scores = [_sb_scores(qm, k_ref[pl.ds(off, b), :], m) for off, m in zip(offs, masks)]
        boths = [_xdot_right(lk, tri_v) for _, _, lk in scores]
        run = acc = jnp.zeros((2 * b, b), F32)
        for j in range(SB_FIXED):
            z, sp, _ = scores[j]
            a = jnp.where(masks[j], jnp.exp(z - sp + boths[j][:, :b] + run), 0.0)
            acc = acc + _dot(a.astype(BF16), v_ref[pl.ds(offs[j], b), :], NN)
            run = run + boths[j][:, b:]

        def cond(carry):
            j, run_, _ = carry
            return (j <= i) & (jnp.max(run_) > SB_DEAD)

        def step(carry):
            j, run_, acc_ = carry
            run_, acc_ = block(i - j, b, run_, acc_)
            return j + 1, run_, acc_

        _, _, acc = lax.while_loop(cond, step, (jnp.int32(SB_FIXED), run, acc))
        o_ref[...] = jnp.where(lane[:b] < SB_DH, acc[:b], acc[b:]).astype(BF16)

    blk = pl.BlockSpec((b, b), lambda p, i: (i, p))
    full = pl.BlockSpec((s, b), lambda p, i: (0, p))
    return _call(
        body, (qn, kn, vb, tri), grid=(w // b, nq),
        in_specs=[blk, full, full, pl.BlockSpec(tri.shape, lambda p, i: (0, 0))],
        out_specs=[blk], out_shape=[_sds((s, w), BF16)],
        semantics=("parallel", "arbitrary"), name=name, comm=comm)


def _sb_bwd(qn, kn, vb, do, name, comm=None):
    s, w = qn.shape
    b = SB_BLK
    nq = s // b
    tri_s = _sb_tri("suffix")
    tri_p = _sb_tri("prefix")
    scale = SB_DH ** -0.5

    def body(q_ref, k_ref, v_ref, do_ref, ts_ref, tp_ref, dq_ref, dk_ref, dv_ref, dk_acc, dv_acc, dp_scr):
        i = pl.program_id(1)

        @pl.when(i == 0)
        def _():
            dk_acc[...] = jnp.zeros_like(dk_acc)
            dv_acc[...] = jnp.zeros_like(dv_acc)

        lane, tt, mine = _sb_setup(b)
        q = q_ref[...]
        q2 = jnp.concatenate([q, q], axis=0)
        qm = jnp.where(mine, q2, jnp.zeros_like(q2))
        dout = do_ref[...].astype(BF16)
        d2 = jnp.concatenate([dout, dout], axis=0)
        dom = jnp.where(mine, d2, jnp.zeros_like(d2))
        ts_v = ts_ref[...]
        tp_v = tp_ref[...]
        zero = jnp.zeros((2 * b, b), F32)

        def down(kb, lim, run):
            off = pl.multiple_of(kb * b, b)
            kblk = k_ref[pl.ds(off, b), :]
            vblk = v_ref[pl.ds(off, b), :]
            mask = lane < lim
            z, sp, lk = _sb_scores(qm, kblk, mask)
            both = _xdot_right(lk, ts_v)
            a = jnp.where(mask, jnp.exp(z - sp + both[:, :b] + run), 0.0)
            dv_acc[pl.ds(off, b), :] += _dot(a.astype(BF16), dom, TN)
            return _dot(dom, vblk, NT) * a, run + both[:, b:]

        def up(kb, lim, dp, pre, dq):
            off = pl.multiple_of(kb * b, b)
            kblk = k_ref[pl.ds(off, b), :]
            sig = jax.nn.sigmoid(_dot(qm, kblk, NT) * scale)
            both = _xdot_right(dp, tp_v)
            dz = jnp.where(lane < lim, dp * (1.0 - sig) - sig * (both[:, :b] + pre), 0.0) * scale
            dz = dz.astype(BF16)
            dk_acc[pl.ds(off, b), :] += _dot(dz, qm, TN)
            return pre + both[:, b:], dq + _dot(dz, kblk, NN)

        offs = [pl.multiple_of(jnp.maximum(i - j, 0) * b, b) for j in range(SB_FIXED)]
        masks = [lane < (tt if j == 0 else jnp.where(i >= j, b, 0)) for j in range(SB_FIXED)]
        kblks = [k_ref[pl.ds(off, b), :] for off in offs]
        scores = [_sb_scores(qm, kblk, m) for kblk, m in zip(kblks, masks)]
        das = [_dot(dom, v_ref[pl.ds(off, b), :], NT) for off in offs]
        boths = [_xdot_right(lk, ts_v) for _, _, lk in scores]
        run = zero
        dps = []
        for j in range(SB_FIXED):
            z, sp, _ = scores[j]
            a = jnp.where(masks[j], jnp.exp(z - sp + boths[j][:, :b] + run), 0.0)
            dps.append(das[j] * a)
            dv_acc[pl.ds(offs[j], b), :] += _dot(a.astype(BF16), dom, TN)
            run = run + boths[j][:, b:]

        def cond(carry):
            j, run_ = carry
            return (j <= i) & (jnp.max(run_) > SB_DEAD)

        def sweep_down(carry):
            j, run_ = carry
            dp, run_ = down(i - j, b, run_)
            dp_scr[i - j] = dp
            return j + 1, run_

        n_live, _ = lax.while_loop(cond, sweep_down, (jnp.int32(SB_FIXED), run))

        def sweep_up(jj, carry):
            kb = i - n_live + 1 + jj
            return up(kb, b, dp_scr[kb], *carry)

        pre, dq = lax.fori_loop(0, n_live - SB_FIXED, sweep_up, (zero, zero))
        pres = [_xdot_right(dp, tp_v) for dp in dps]
        for j in reversed(range(SB_FIXED)):
            z, sp, _ = scores[j]
            sig = jnp.exp(z - sp)
            dz = jnp.where(masks[j], dps[j] * (1.0 - sig) - sig * (pres[j][:, :b] + pre), 0.0) * scale
            dz = dz.astype(BF16)
            dk_acc[pl.ds(offs[j], b), :] += _dot(dz, qm, TN)
            dq = dq + _dot(dz, kblks[j], NN)
            pre = pre + pres[j][:, b:]
        dq_ref[...] = jnp.where(lane[:b] < SB_DH, dq[:b], dq[b:])

        @pl.when(i == nq - 1)
        def _():
            dk_ref[...] = dk_acc[...]
            dv_ref[...] = dv_acc[...].astype(BF16)

    blk = pl.BlockSpec((b, b), lambda p, i: (i, p))
    full = pl.BlockSpec((s, b), lambda p, i: (0, p))
    tri = pl.BlockSpec(tri_s.shape, lambda p, i: (0, 0))
    return _call(
        body, (qn, kn, vb, do, tri_s, tri_p), grid=(w // b, nq), in_specs=[blk, full, full, blk, tri, tri],
        out_specs=[blk, full, full], out_shape=[_sds((s, w), F32), _sds((s, w), F32), _sds((s, w), BF16)],
        scratch_shapes=[pltpu.VMEM((s, b), F32), pltpu.VMEM((s, b), F32), pltpu.VMEM((nq, 2 * b, b), F32)],
        semantics=("arbitrary", "arbitrary"), name=name, comm=comm)


MIX_T = 256
HALF = 512


def _gate_slices(ga, gb):
    return [(ga[:, 0:512], ga[:, 512:1024]), (ga[:, 1024:1536], gb[:, 0:512]), (gb[:, 512:1024], gb[:, 1024:1536])]


def _mix_fwd(u3, oh, osb, proj, x, pv, wc, wh, ws, wo, name):
    s, d = x.shape
    t = min(MIX_T, s)

    def body(u3_ref, oh_ref, os_ref, ga_ref, gb_ref, x_ref, pv_ref, wc_ref, wh_ref, ws_ref, wo_ref,
             x1_ref, h2_ref, mg_ref, mo_ref):
        ys = [_dot(u3_ref[...], wc_ref[...], NT), _dot(oh_ref[...], wh_ref[...], NT), _dot(os_ref[...], ws_ref[...], NT)]
        gl = _gate_slices(ga_ref[...], gb_ref[...])
        halves = []
        for hf in range(2):
            lo = hf * HALF
            acc = jnp.zeros((t, HALF), F32)
            for br in range(3):
                gate = jax.nn.sigmoid(gl[br][hf] + pv_ref[8 + br:9 + br, lo:lo + HALF])
                acc = acc + gate * ys[br][:, lo:lo + HALF]
            halves.append(acc)
        merged = jnp.concatenate(halves, axis=1).astype(BF16)
        mg_ref[...] = merged
        mo = _dot(merged, wo_ref[...], NN)
        mo_ref[...] = mo.astype(BF16)
        x1 = x_ref[...] + pv_ref[2:3, :] * mo
        x1_ref[...] = x1
        h2_ref[...] = _norm_mod(x1, pv_ref[7:8, :], pv_ref[4:5, :], pv_ref[3:4, :]).astype(BF16)

    br_spec = pl.BlockSpec((t, CONV_CH), lambda i: (i, 0))
    row = pl.BlockSpec((t, d), lambda i: (i, 0))
    wproj = pl.BlockSpec((d, CONV_CH), lambda i: (0, 0))
    return pl.pallas_call(
        body, grid=(s // t,),
        in_specs=[br_spec, br_spec, br_spec, pl.BlockSpec((t, 1536), lambda i: (i, 3)),
                  pl.BlockSpec((t, 1536), lambda i: (i, 4)), row, pl.BlockSpec((16, d), lambda i: (0, 0)),
                  wproj, wproj, wproj, pl.BlockSpec((d, d), lambda i: (0, 0))],
        out_specs=[row, row, row, row],
        out_shape=[_sds((s, d), F32), _sds((s, d), BF16), _sds((s, d), BF16), _sds((s, d), BF16)],
        compiler_params=_params("parallel"), name=name)(u3, oh, osb, proj, proj, x, pv, wc, wh, ws, wo)


def _mix_bwd(dx1, mo1, u3, oh, osb, proj, pv, wc, wh, ws, wo, name):
    s, d = dx1.shape
    t = min(MIX_T, s)

    def body(dx_ref, mo_ref, u3_ref, oh_ref, os_ref, ga_ref, gb_ref, pv_ref, wc_ref, wh_ref, ws_ref, wo_ref,
             dmo_ref, dyc_ref, dyh_ref, dys_ref, doc_ref, doh_ref, dos_ref, dgl_ref, sg_ref):
        i = pl.program_id(0)

        @pl.when(i == 0)
        def _():
            sg_ref[...] = jnp.zeros_like(sg_ref)

        dx = dx_ref[...]
        dmo = (dx * pv_ref[2:3, :]).astype(BF16)
        dmo_ref[...] = dmo
        sg_ref[0:1, :] += jnp.sum(dx * mo_ref[...].astype(F32), axis=0, keepdims=True)
        dmerged = _dot(dmo, wo_ref[...], NT)
        branches = [(u3_ref, wc_ref, dyc_ref, doc_ref), (oh_ref, wh_ref, dyh_ref, doh_ref), (os_ref, ws_ref, dys_ref, dos_ref)]
        gl = _gate_slices(ga_ref[...], gb_ref[...])
        for br, (o_ref, w_ref, dy_ref, do_ref) in enumerate(branches):
            y = _dot(o_ref[...], w_ref[...], NT)
            dys = []
            for hf in range(2):
                lo = hf * HALF
                gate = jax.nn.sigmoid(gl[br][hf] + pv_ref[8 + br:9 + br, lo:lo + HALF])
                dm = dmerged[:, lo:lo + HALF]
                dys.append(dm * gate)
                dgl = dm * y[:, lo:lo + HALF] * gate * (1.0 - gate)
                dgl_ref[:, br * d + lo: br * d + lo + HALF] = dgl.astype(BF16)
                sg_ref[1 + br:2 + br, lo:lo + HALF] += jnp.sum(dgl, axis=0, keepdims=True)
            dy = jnp.concatenate(dys, axis=1).astype(BF16)
            dy_ref[...] = dy
            do_ref[...] = _dot(dy, w_ref[...], NN)

    br_spec = pl.BlockSpec((t, CONV_CH), lambda i: (i, 0))
    row = pl.BlockSpec((t, d), lambda i: (i, 0))
    wproj = pl.BlockSpec((d, CONV_CH), lambda i: (0, 0))
    return pl.pallas_call(
        body, grid=(s // t,),
        in_specs=[row, row, br_spec, br_spec, br_spec, pl.BlockSpec((t, 1536), lambda i: (i, 3)),
                  pl.BlockSpec((t, 1536), lambda i: (i, 4)), pl.BlockSpec((16, d), lambda i: (0, 0)),
                  wproj, wproj, wproj, pl.BlockSpec((d, d), lambda i: (0, 0))],
        out_specs=[row, row, row, row, br_spec, br_spec, br_spec, pl.BlockSpec((t, 3 * d), lambda i: (i, 0)),
                   pl.BlockSpec((8, d), lambda i: (0, 0))],
        out_shape=[_sds((s, d), BF16)] * 4 + [_sds((s, CONV_CH), F32)] * 3 + [_sds((s, 3 * d), BF16), _sds((8, d), F32)],
        compiler_params=_params("arbitrary"), name=name)(dx1, mo1, u3, oh, osb, proj, proj, pv, wc, wh, ws, wo)


MLP_T = 512
MLP_F = 1024


def _mlp_fwd(h2, x1, pv, w1t, w2, name, comm=None):
    s, d = x1.shape
    t = min(MLP_T, s)
    nf = D_FF // MLP_F

    def body(h_ref, x_ref, pv_ref, w1_ref, w2_ref, x2_ref, mo_ref, acc_ref):
        f = pl.program_id(1)

        @pl.when(f == 0)
        def _():
            acc_ref[...] = jnp.zeros_like(acc_ref)

        a = jnp.maximum(_dot(h_ref[...], w1_ref[...], NT), 0.0)
        acc_ref[...] += _dot((a * a).astype(BF16), w2_ref[...], NN)

        @pl.when(f == nf - 1)
        def _():
            mo = acc_ref[...]
            mo_ref[...] = mo.astype(BF16)
            x2_ref[...] = x_ref[...] + pv_ref[5:6, :] * mo

    row = pl.BlockSpec((t, d), lambda i, f: (i, 0))
    wblk = pl.BlockSpec((MLP_F, d), lambda i, f: (f, 0))
    return _call(
        body, (h2, x1, pv, w1t, w2), grid=(s // t, nf),
        in_specs=[row, row, pl.BlockSpec((16, d), lambda i, f: (0, 0)), wblk, wblk],
        out_specs=[row, row], out_shape=[_sds((s, d), F32), _sds((s, d), BF16)],
        scratch_shapes=[pltpu.VMEM((t, d), F32)],
        semantics=("parallel", "arbitrary"), name=name, comm=comm)


def _mlp_bwd(dx2, h2, x1, mo2, pv, w1t, w2, name, comm=None):
    s, d = x1.shape
    t = min(MLP_T, s)
    nf = D_FF // MLP_F

    def body(dx_ref, h_ref, x_ref, mo_ref, pv_ref, w1_ref, w2_ref, dx1_ref, da_ref, b_ref, dmo_ref, sg_ref, acc_ref):
        i = pl.program_id(0)
        f = pl.program_id(1)

        @pl.when((i == 0) & (f == 0))
        def _():
            sg_ref[...] = jnp.zeros_like(sg_ref)

        @pl.when(f == 0)
        def _():
            acc_ref[...] = jnp.zeros_like(acc_ref)
            dx = dx_ref[...]
            dmo_ref[...] = (dx * pv_ref[5:6, :]).astype(BF16)
            sg_ref[0:1, :] += jnp.sum(dx * mo_ref[...].astype(F32), axis=0, keepdims=True)

        r = jnp.maximum(_dot(h_ref[...], w1_ref[...], NT), 0.0)
        b_ref[...] = (r * r).astype(BF16)
        da = (_dot(dmo_ref[...], w2_ref[...], NT) * (2.0 * r)).astype(BF16)
        da_ref[...] = da
        acc_ref[...] += _dot(da, w1_ref[...], NN)

        @pl.when(f == nf - 1)
        def _():
            _, vjp = jax.vjp(_norm_mod, x_ref[...], pv_ref[7:8, :], pv_ref[4:5, :], pv_ref[3:4, :])
            dxn, dg, dsc, dsh = vjp(acc_ref[...])
            dx1_ref[...] = dx_ref[...] + dxn
            sg_ref[1:2, :] += dsh
            sg_ref[2:3, :] += dsc
            sg_ref[3:4, :] += dg

    row = pl.BlockSpec((t, d), lambda i, f: (i, 0))
    wblk = pl.BlockSpec((MLP_F, d), lambda i, f: (f, 0))
    hid = pl.BlockSpec((t, MLP_F), lambda i, f: (i, f))
    return _call(
        body, (dx2, h2, x1, mo2, pv, w1t, w2), grid=(s // t, nf),
        in_specs=[row, row, row, row, pl.BlockSpec((16, d), lambda i, f: (0, 0)), wblk, wblk],
        out_specs=[row, hid, hid, row, pl.BlockSpec((8, d), lambda i, f: (0, 0))],
        out_shape=[_sds((s, d), F32), _sds((s, D_FF), BF16), _sds((s, D_FF), BF16), _sds((s, d), BF16), _sds((8, d), F32)],
        scratch_shapes=[pltpu.VMEM((t, d), F32)],
        semantics=("arbitrary", "arbitrary"), name=name, comm=comm)


def _loss_head(y, target, name):
    s, d = y.shape
    t = min(ROW_T, s)

    def body(y_ref, t_ref, dy_ref, ls_ref):
        i = pl.program_id(0)

        @pl.when(i == 0)
        def _():
            ls_ref[...] = jnp.zeros_like(ls_ref)

        e = y_ref[...] - t_ref[...]
        dy_ref[...] = e * (1.0 / d)
        ls_ref[...] += jnp.sum((e * e).reshape(t // 8, 8, d), axis=0)

    row = pl.BlockSpec((t, d), lambda i: (i, 0))
    return pl.pallas_call(
        body, grid=(s // t,), in_specs=[row, row], out_specs=[row, pl.BlockSpec((8, d), lambda i: (0, 0))],
        out_shape=[_sds((s, d), F32), _sds((8, d), F32)],
        compiler_params=_params("arbitrary"), name=name)(y, target)


def _layer_vectors(l, mod, sm):
    d = D_MODEL
    pv = jnp.concatenate([mod[l].reshape(6, d), sm["norm1_g"][l][None], sm["norm2_g"][l][None],
                          sm["gate_b"][l].reshape(3, d), jnp.zeros((5, d), F32)], axis=0)
    cp = jnp.concatenate([sm["conv_b"][l][None], sm["conv_ln_g"][l][None], sm["conv_ln_b"][l][None],
                          jnp.zeros((5, CONV_CH), F32)], axis=0)
    return dict(pv=pv, cp=cp, conv_w=sm["conv_w"][l], lb=(sm["hgrn_lb"] if l > 0 else None),
                ng=sm["hgrn_norm_g"][l][None], gq=jnp.tile(sm["sb_qn_g"][l], SB_HEADS)[None],
                gk=jnp.tile(sm["sb_kn_g"][l], SB_HEADS)[None])


def _hosted(res, comm):
    return res if comm is not None else (res, None)


def _layer_fwd_mixers(x, vec, win_t, tag, comm_proj=None, comm_hgrn=None, comm_sb=None):
    h = _prenorm(x, vec["pv"], f"prenorm{tag}")
    proj, got_proj = _hosted(_matmul(h, win_t, "nt", F32, 1024, 768, 1024, f"proj{tag}", comm_proj), comm_proj)
    u3 = _conv_fwd(proj, vec["conv_w"], vec["cp"], f"conv_fwd{tag}")
    (oh, states), got_hgrn = _hosted(_hgrn_fwd(proj, vec["lb"], vec["ng"], f"hgrn_fwd{tag}", comm_hgrn), comm_hgrn)
    qn, kn, vb = _sb_prep(proj, vec["gq"], vec["gk"], f"sb_prep{tag}")
    (osb,), got_sb = _hosted(_sb_fwd(qn, kn, vb, f"sb_fwd{tag}", comm_sb), comm_sb)
    saved = dict(x=x, h=h, proj=proj, u3=u3, oh=oh, states=states, qn=qn, kn=kn, vb=vb, osb=osb)
    return saved, (got_proj, got_hgrn, got_sb)


def _layer_fwd_out(sv, vec, w, tag, comm_mlp=None):
    x1, h2, merged, mo1 = _mix_fwd(sv["u3"], sv["oh"], sv["osb"], sv["proj"], sv["x"], vec["pv"],
                                   w["wc_t"], w["wh_t"], w["ws_t"], w["wo"], f"mix_fwd{tag}")
    (x2, mo2), got = _hosted(_mlp_fwd(h2, x1, vec["pv"], w["w1_t"], w["w2"], f"mlp_fwd{tag}", comm_mlp), comm_mlp)
    sv.update(x1=x1, h2=h2, merged=merged, mo1=mo1, mo2=mo2)
    return x2, got


def _layer_bwd(dx2, sv, vec, w, tag, plans=None):
    plans = plans or {}
    got = {}

    def plan_for(key, big_now):
        return plans[key](big_now) if key in plans else None

    pv = vec["pv"]
    big = {}
    comm = plan_for("mlp", big)
    (dx1, da, bsq, dmo2, sg_mlp), got["mlp"] = _hosted(
        _mlp_bwd(dx2, sv["h2"], sv["x1"], sv["mo2"], pv, w["w1_t"], w["w2"], f"mlp_bwd{tag}", comm), comm)
    big["w1_t"] = _matmul(da, sv["h2"], "tn", BF16, 1024, 1024, 1024, f"dw1{tag}")
    big["w2"] = _matmul(bsq, dmo2, "tn", BF16, 1024, 1024, 1024, f"dw2{tag}")
    dmo1, dyc, dyh, dys, doc, doh, dos, dgl, sg_mix = _mix_bwd(
        dx1, sv["mo1"], sv["u3"], sv["oh"], sv["osb"], sv["proj"], pv, w["wc_t"], w["wh_t"], w["ws_t"], w["wo"], f"mix_bwd{tag}")
    big["wo"] = _matmul(sv["merged"], dmo1, "tn", BF16, 1024, 1024, 1024, f"dwo{tag}")
    big["wc_t"] = _matmul(dyc, sv["u3"], "tn", BF16, 1024, 512, 1024, f"dwc{tag}")
    big["wh_t"] = _matmul(dyh, sv["oh"], "tn", BF16, 1024, 512, 1024, f"dwh{tag}")
    big["ws_t"] = _matmul(dys, sv["osb"], "tn", BF16, 1024, 512, 1024, f"dws{tag}")
    comm = plan_for("conv", big)
    (da_c, dg_c, dconv_w, sg_conv), got["conv"] = _hosted(
        _conv_bwd(sv["proj"], doc, vec["conv_w"], vec["cp"], f"conv_bwd{tag}", comm), comm)
    comm = plan_for("hgrn", big)
    (dq_h, df_h, di_h, dg_h, dlb, dng), got["hgrn"] = _hosted(
        _hgrn_bwd(sv["proj"], sv["states"], doh, vec["lb"], vec["ng"], f"hgrn_bwd{tag}", comm), comm)
    comm = plan_for("sb", big)
    (dqn, dkn, dv_s), got["sb"] = _hosted(_sb_bwd(sv["qn"], sv["kn"], sv["vb"], dos, f"sb_bwd{tag}", comm), comm)
    dq_s, dk_s, sg_sb = _sb_prep_bwd(sv["proj"], dqn, dkn, vec["gq"], vec["gk"], f"sb_prep_bwd{tag}")
    dproj = jnp.concatenate([da_c, dg_c, dq_h, df_h, di_h, dg_h, dq_s, dk_s, dv_s, dgl], axis=1)
    half = D_MODEL // 2
    comm = plan_for("dwin_a", big)
    big["win_a"], got["dwin_a"] = _hosted(
        _matmul(dproj, sv["h"][:, :half], "tn", BF16, 768, half, 1024, f"dwin_a{tag}", comm), comm)
    comm = plan_for("dwin", big)
    big["win_b"], got["dwin"] = _hosted(
        _matmul(dproj, sv["h"][:, half:], "tn", BF16, 768, half, 1024, f"dwin_b{tag}", comm), comm)
    comm = plan_for("dh", big)
    (dx, sg_pre), got["dh"] = _hosted(_dh_prenorm_bwd(dproj, w["win_t"], dx1, sv["x"], pv, f"dh{tag}", comm), comm)
    small = dict(
        mod=jnp.stack([sg_pre[0], sg_pre[1], sg_mix[0], sg_mlp[1], sg_mlp[2], sg_mlp[0]]).reshape(6 * D_MODEL),
        norm1_g=sg_pre[2], norm2_g=sg_mlp[3], gate_b=sg_mix[1:4].reshape(3 * D_MODEL),
        conv_w=dconv_w, conv_b=sg_conv[0], conv_ln_g=sg_conv[1], conv_ln_b=sg_conv[2],
        hgrn_lb=dlb, hgrn_norm_g=dng[0],
        sb_qn_g=sg_sb[0].reshape(SB_HEADS, SB_DH).sum(0), sb_kn_g=sg_sb[1].reshape(SB_HEADS, SB_DH).sum(0))
    return dx, big, small, got


def _row_tile(r, cap=512):
    t = min(r, cap)
    while r % t or (t % 8 and t != r):
        t -= 1
    return t


def _sum8(z, name):
    _, r, c = z.shape
    t = _row_tile(r, 128 if c >= 1024 else 512)

    def body(z_ref, o_ref):
        acc = z_ref[0].astype(F32)
        for j in range(1, N_DEV):
            acc = acc + z_ref[j].astype(F32)
        o_ref[...] = acc

    return pl.pallas_call(
        body, grid=(r // t,), in_specs=[pl.BlockSpec((N_DEV, t, c), lambda i: (0, i, 0))],
        out_specs=pl.BlockSpec((t, c), lambda i: (i, 0)), out_shape=_sds((r, c), F32),
        compiler_params=_params("parallel"), name=name)(z)


def _adamw(w, g, m, v, name):
    r, c = w.shape
    t = _row_tile(r, 256)

    def body(w_ref, g_ref, m_ref, v_ref, d_ref, nm_ref, nv_ref):
        g_ = g_ref[...]
        nm = ADAM_B1 * m_ref[...] + (1.0 - ADAM_B1) * g_
        nv = ADAM_B2 * v_ref[...] + (1.0 - ADAM_B2) * jnp.square(g_)
        m_hat = nm / (1.0 - ADAM_B1 ** ADAM_STEP)
        v_hat = nv / (1.0 - ADAM_B2 ** ADAM_STEP)
        d_ref[...] = -ADAM_LR * (m_hat / (jnp.sqrt(v_hat) + ADAM_EPS) + ADAM_WD * w_ref[...])
        nm_ref[...] = nm
        nv_ref[...] = nv

    blk = pl.BlockSpec((t, c), lambda i: (i, 0))
    return pl.pallas_call(
        body, grid=(r // t,), in_specs=[blk] * 4, out_specs=[blk] * 3, out_shape=[_sds((r, c), F32)] * 3,
        compiler_params=_params("parallel"), name=name)(w, g, m, v)


def _mod_local(c_all, mod_w, name):
    depth, d, cols = mod_w.shape

    def body(c_ref, w_ref, o_ref):
        cv = c_ref[...]
        act = cv * jax.nn.sigmoid(cv)
        o_ref[...] = jnp.dot(act, w_ref[...], precision=lax.Precision.HIGHEST, preferred_element_type=F32)

    return pl.pallas_call(
        body, grid=(depth,),
        in_specs=[pl.BlockSpec((N_DEV, d), lambda l: (0, 0)), pl.BlockSpec((None, d, cols), lambda l: (l, 0, 0))],
        out_specs=pl.BlockSpec((None, N_DEV, cols), lambda l: (l, 0, 0)), out_shape=_sds((depth, N_DEV, cols), F32),
        compiler_params=_params("parallel"), name=name)(c_all, mod_w)


def _modw_grad(c_all, dmod, name):
    depth, _, cols = dmod.shape
    d = c_all.shape[1]

    def body(c_ref, g_ref, o_ref):
        cv = c_ref[...]
        act = cv * jax.nn.sigmoid(cv)
        o_ref[...] = lax.dot_general(act, g_ref[...], (TN, ((), ())), precision=lax.Precision.HIGHEST,
                                     preferred_element_type=F32)

    return pl.pallas_call(
        body, grid=(depth,),
        in_specs=[pl.BlockSpec((N_DEV, d), lambda l: (0, 0)), pl.BlockSpec((None, N_DEV, cols), lambda l: (l, 0, 0))],
        out_specs=pl.BlockSpec((None, d, cols), lambda l: (l, 0, 0)), out_shape=_sds((depth, d, cols), F32),
        compiler_params=_params("parallel"), name=name)(c_all, dmod)


LANE = 128
BIG = {"w_in": ("win_t", True), "w_out": ("wo", False), "mlp_w2": ("w2", False), "mlp_w1": ("w1_t", True),
       "w_conv_proj": ("wc_t", True), "w_hgrn_proj": ("wh_t", True), "w_sb_proj": ("ws_t", True)}
PROJS = ("w_conv_proj", "w_hgrn_proj", "w_sb_proj")
SMALL = (("mod_b", 6144), ("norm1_g", 1024), ("gate_b", 3072), ("conv_w", CONV_WIDTH * CONV_CH), ("conv_b", 512),
         ("conv_ln_g", 512), ("conv_ln_b", 512), ("hgrn_lb", 512), ("hgrn_norm_g", 128), ("sb_qn_g", 64),
         ("sb_kn_g", 64), ("norm2_g", 1024))


def _pack_rows(parts, width):
    flat = jnp.concatenate([p.reshape(-1) for p in parts])
    rows = -(-flat.shape[0] // width)
    rows = -(-rows // 8) * 8
    return jnp.pad(flat, (0, rows * width - flat.shape[0])).reshape(rows, width)


def _shards(params, items):
    return [(params[n][l].T if BIG[n][1] else params[n][l]).astype(BF16) for n, l in items]


def _gathered(items, got):
    return {BIG[n][0]: g.reshape(-1, g.shape[2]) for (n, _), g in zip(items, got)}


def _by_shard(g):
    return g.reshape(N_DEV, g.shape[0] // N_DEV, g.shape[1])


def _adamw_nd(w, g, m, v, name):
    shape = w.shape
    two = lambda a: a.reshape(-1, shape[-1])
    return [o.reshape(shape) for o in _adamw(two(w), two(g), two(m), two(v), name)]


WEIGHTS = ("mod_w", "mod_b", "norm1_g", "w_in", "gate_b", "conv_w", "conv_b", "conv_ln_g", "conv_ln_b", "w_conv_proj",
           "hgrn_lb", "hgrn_norm_g", "w_hgrn_proj", "sb_qn_g", "sb_kn_g", "w_sb_proj", "w_out", "norm2_g", "mlp_w1",
           "mlp_w2")


def kernel(x, c, mod_w, mod_b, norm1_g, w_in, gate_b, conv_w, conv_b, conv_ln_g, conv_ln_b, w_conv_proj, hgrn_lb, hgrn_norm_g, w_hgrn_proj, sb_qn_g, sb_kn_g, w_sb_proj, w_out, norm2_g, mlp_w1, mlp_w2, loss_target, m_mod_w, m_mod_b, m_norm1_g, m_w_in, m_gate_b, m_conv_w, m_conv_b, m_conv_ln_g, m_conv_ln_b, m_w_conv_proj, m_hgrn_lb, m_hgrn_norm_g, m_w_hgrn_proj, m_sb_qn_g, m_sb_kn_g, m_w_sb_proj, m_w_out, m_norm2_g, m_mlp_w1, m_mlp_w2, v_mod_w, v_mod_b, v_norm1_g, v_w_in, v_gate_b, v_conv_w, v_conv_b, v_conv_ln_g, v_conv_ln_b, v_w_conv_proj, v_hgrn_lb, v_hgrn_norm_g, v_w_hgrn_proj, v_sb_qn_g, v_sb_kn_g, v_w_sb_proj, v_w_out, v_norm2_g, v_mlp_w1, v_mlp_w2):
    params = dict(mod_w=mod_w, mod_b=mod_b, norm1_g=norm1_g, w_in=w_in, gate_b=gate_b, conv_w=conv_w, conv_b=conv_b,
                  conv_ln_g=conv_ln_g, conv_ln_b=conv_ln_b, w_conv_proj=w_conv_proj, hgrn_lb=hgrn_lb,
                  hgrn_norm_g=hgrn_norm_g, w_hgrn_proj=w_hgrn_proj, sb_qn_g=sb_qn_g, sb_kn_g=sb_kn_g,
                  w_sb_proj=w_sb_proj, w_out=w_out, norm2_g=norm2_g, mlp_w1=mlp_w1, mlp_w2=mlp_w2)
    mom1 = dict(mod_w=m_mod_w, mod_b=m_mod_b, norm1_g=m_norm1_g, w_in=m_w_in, gate_b=m_gate_b, conv_w=m_conv_w,
                conv_b=m_conv_b, conv_ln_g=m_conv_ln_g, conv_ln_b=m_conv_ln_b, w_conv_proj=m_w_conv_proj,
                hgrn_lb=m_hgrn_lb, hgrn_norm_g=m_hgrn_norm_g, w_hgrn_proj=m_w_hgrn_proj, sb_qn_g=m_sb_qn_g,
                sb_kn_g=m_sb_kn_g, w_sb_proj=m_w_sb_proj, w_out=m_w_out, norm2_g=m_norm2_g, mlp_w1=m_mlp_w1,
                mlp_w2=m_mlp_w2)
    mom2 = dict(mod_w=v_mod_w, mod_b=v_mod_b, norm1_g=v_norm1_g, w_in=v_w_in, gate_b=v_gate_b, conv_w=v_conv_w,
                conv_b=v_conv_b, conv_ln_g=v_conv_ln_g, conv_ln_b=v_conv_ln_b, w_conv_proj=v_w_conv_proj,
                hgrn_lb=v_hgrn_lb, hgrn_norm_g=v_hgrn_norm_g, w_hgrn_proj=v_w_hgrn_proj, sb_qn_g=v_sb_qn_g,
                sb_kn_g=v_sb_kn_g, w_sb_proj=v_w_sb_proj, w_out=v_w_out, norm2_g=v_norm2_g, mlp_w1=v_mlp_w1,
                mlp_w2=v_mlp_w2)
    xi, yi, ci = _mesh_place()
    me = _block_of(xi, yi, ci)
    cw_cols = conv_w.shape[2]

    tiny = _pack_rows([c, conv_w], LANE)
    g_tiny, g_win0 = _comm_alone(_GatherPlan([tiny] + _shards(params, [("w_in", 0)])), "gather_first")
    c_rows = D_MODEL // LANE
    c_all = g_tiny[:, :c_rows].reshape(N_DEV, D_MODEL)
    n_cw = DEPTH * CONV_WIDTH * cw_cols
    conv_w_full = g_tiny[:, c_rows:c_rows + n_cw // LANE].reshape(N_DEV, DEPTH, CONV_WIDTH, cw_cols)
    conv_w_full = conv_w_full.transpose(1, 2, 0, 3).reshape(DEPTH, CONV_WIDTH, CONV_CH)

    (g_mod,) = _comm_alone(_GatherPlan([_mod_local(c_all, mod_w, "mod_local")]), "gather_mod")
    mod = lax.dynamic_index_in_dim(g_mod, me, axis=2, keepdims=False)
    mod = mod.transpose(1, 0, 2).reshape(DEPTH, 6 * D_MODEL) + mod_b

    sm = dict(norm1_g=norm1_g, norm2_g=norm2_g, gate_b=gate_b, conv_w=conv_w_full, conv_b=conv_b, conv_ln_g=conv_ln_g,
              conv_ln_b=conv_ln_b, hgrn_lb=hgrn_lb, hgrn_norm_g=hgrn_norm_g, sb_qn_g=sb_qn_g, sb_kn_g=sb_kn_g)
    vecs = [_layer_vectors(l, mod, sm) for l in range(DEPTH)]

    fwd_hosts = dict(
        proj=[("w_in", 1)],
        hgrn=[("mlp_w2", 0), ("w_out", 0)] + [(n, 0) for n in PROJS],
        sb=[("mlp_w1", 0), ("mlp_w2", 1)],
        mlp=[("mlp_w1", 1), ("w_out", 1)] + [(n, 1) for n in PROJS])
    gather = {host: _GatherPlan(_shards(params, items)) for host, items in fwd_hosts.items()}
    wts = [_gathered([("w_in", 0)], [g_win0]), {}]
    sv0, got_mixers = _layer_fwd_mixers(x[0], vecs[0], wts[0]["win_t"], "_l0", gather["proj"], gather["hgrn"], gather["sb"])
    for host, got in zip(("proj", "hgrn", "sb"), got_mixers):
        for (name, l), g in zip(fwd_hosts[host], got):
            wts[l].update(_gathered([(name, l)], [g]))
    y, got = _layer_fwd_out(sv0, vecs[0], wts[0], "_l0", gather["mlp"])
    wts[1].update(_gathered(fwd_hosts["mlp"], got))
    sv1, _ = _layer_fwd_mixers(y, vecs[1], wts[1]["win_t"], "_l1")
    y, _ = _layer_fwd_out(sv1, vecs[1], wts[1], "_l1")
    dy, sq = _loss_head(y, loss_target[0], "loss_head")
    loss = lax.psum(0.5 * jnp.sum(sq) / D_MODEL, ("x", "y", "c"))

    half = D_MODEL // 2
    dy, big1, small1, _ = _layer_bwd(dy, sv1, vecs[1], wts[1], "_l1")
    bwd_hosts = dict(
        mlp=[(1, "win_a"), (1, "wo"), (1, "wc_t"), (1, "wh_t"), (1, "ws_t")],
        conv=[(1, "win_b"), (0, "wc_t"), (0, "wh_t"), (0, "ws_t")],
        hgrn=[(1, "w2"), (0, "wo"), (0, "w2_a")],
        sb=[(1, "w1_t"), (0, "w1_t")],
        dwin_a=[(0, "w2_b")],
        dwin=[(0, "win_a")],
        dh=[(0, "win_b")])

    def source(l, key, big0):
        big = big1 if l == 1 else big0
        if key in ("w2_a", "w2_b"):
            return big["w2"][:, :half] if key == "w2_a" else big["w2"][:, half:]
        return big[key]

    plans = {host: (lambda big0, items=items: _ExchangePlan([_by_shard(source(l, k, big0)) for l, k in items]))
             for host, items in bwd_hosts.items()}
    dx, _, small0, got = _layer_bwd(dy, sv0, vecs[0], wts[0], "_l0", plans)
    smalls = [small0, small1]
    summed = {}
    for host, items in bwd_hosts.items():
        for (l, key), arrived in zip(items, got[host]):
            summed[l, key] = _sum8(arrived, f"sum_{key}_l{l}")
    summed[0, "w2"] = jnp.concatenate([summed[0, "w2_a"], summed[0, "w2_b"]], axis=1)
    grads = {}
    for name, (key, transposed) in BIG.items():
        per_layer = []
        for l in range(DEPTH):
            if name == "w_in":
                blk = jnp.concatenate([summed[l, "win_a"], summed[l, "win_b"]], axis=1)
            else:
                blk = summed[l, key]
            per_layer.append(blk.T if transposed else blk)
        grads[name] = jnp.stack(per_layer)

    small_parts = []
    for name, _ in SMALL:
        key = "mod" if name == "mod_b" else name
        if name == "hgrn_lb":
            small_parts.append(smalls[0][key] + smalls[1][key])
        else:
            small_parts.append(jnp.stack([smalls[l][key] for l in range(DEPTH)]))
    (g_small,) = _comm_alone(_GatherPlan([_pack_rows(small_parts, LANE)]), "gather_small_grads")
    small_sum = _sum8(g_small, "sum_small_grads").reshape(-1)
    off = 0
    for name, per_layer in SMALL:
        grads[name] = small_sum[off:off + DEPTH * per_layer].reshape(params[name].shape if name != "conv_w" else (DEPTH, CONV_WIDTH, CONV_CH))
        off += DEPTH * per_layer
    grads["conv_w"] = lax.dynamic_slice_in_dim(grads["conv_w"], me * cw_cols, cw_cols, axis=2)
    cols = mod_w.shape[2]
    dmod_all = g_small.reshape(N_DEV, -1)[:, :DEPTH * 6 * D_MODEL].reshape(N_DEV, DEPTH, 6 * D_MODEL)
    dmod_mine = lax.dynamic_slice_in_dim(dmod_all, me * cols, cols, axis=2).transpose(1, 0, 2)
    grads["mod_w"] = _modw_grad(c_all, dmod_mine, "mod_w_grad")

    delta, new_m, new_v = {}, {}, {}
    small_names = [n for n, _ in SMALL]
    for name in WEIGHTS:
        if name not in small_names:
            delta[name], new_m[name], new_v[name] = _adamw_nd(params[name], grads[name], mom1[name], mom2[name], f"adamw_{name}")
    packed = [_pack_rows([d[n] for n in small_names], LANE) for d in (params, grads, mom1, mom2)]
    outs = [o.reshape(-1) for o in _adamw(*packed, "adamw_small")]
    off = 0
    for name in small_names:
        size = params[name].size
        for dst, o in zip((delta, new_m, new_v), outs):
            dst[name] = o[off:off + size].reshape(params[name].shape)
        off += size
    return (loss, dx[None], *[grads[n] for n in WEIGHTS], *[delta[n] for n in WEIGHTS],
            *[new_m[n] for n in WEIGHTS], *[new_v[n] for n in WEIGHTS])
```

```python
import functools

import jax
import jax.numpy as jnp
import numpy as np
from jax import lax
from jax.experimental import pallas as pl
from jax.experimental.pallas import tpu as pltpu

F32 = jnp.float32
BF16 = jnp.bfloat16

D_MODEL = 1024
DEPTH = 2
N_DEV = 8
CONV_CH = 512
CONV_WIDTH = 31
CONV_HALO = 32
HG_HEADS = 4
HG_DK = 128
SB_HEADS = 8
SB_DH = 64
D_IN = 7680
D_FF = 4096
EPS = 1e-6
SB_BLK = 128
SB_DEAD = -104.0
SB_FIXED = 3
HG_CHUNK = 128

ADAM_LR = 0.001
ADAM_B1 = 0.9
ADAM_B2 = 0.999
ADAM_EPS = 1e-08
ADAM_WD = 0.01
ADAM_STEP = 10

VMEM_LIMIT = 48 * 1024 * 1024

NN = ((1,), (0,))
NT = ((1,), (1,))
TN = ((0,), (0,))
_DIMS = {"nn": NN, "nt": NT, "tn": TN}


def _sds(shape, dtype):
    return jax.ShapeDtypeStruct(shape, dtype)


def _params(*semantics):
    return pltpu.CompilerParams(dimension_semantics=semantics, vmem_limit_bytes=VMEM_LIMIT)


def _dot(a, b, dims):
    return lax.dot_general(a, b, (dims, ((), ())), preferred_element_type=F32)


@functools.partial(jax.custom_vjp, nondiff_argnums=(2,))
def _bdot(a, b, mode):
    return _dot(a.astype(BF16), b.astype(BF16), _DIMS[mode])


def _bdot_fwd(a, b, mode):
    return _bdot(a, b, mode), (a.astype(BF16), b.astype(BF16))


def _bdot_bwd(mode, res, g):
    a, b = res
    g = g.astype(BF16)
    if mode == "nn":
        return _dot(g, b, NT), _dot(a, g, TN)
    if mode == "nt":
        return _dot(g, b, NN), _dot(g, a, TN)
    return _dot(b, g, NT), _dot(a, g, NN)


_bdot.defvjp(_bdot_fwd, _bdot_bwd)


def _split(x):
    hi = x.astype(BF16)
    lo = (x - hi.astype(F32)).astype(BF16)
    return hi, lo


def _xdot_right(x, m, dims=NN):
    hi, lo = _split(x)
    if dims == NN:
        return _dot(jnp.concatenate([hi, lo], axis=1), jnp.concatenate([m, m], axis=0), NN)
    return _dot(jnp.concatenate([hi, lo], axis=1), jnp.concatenate([m, m], axis=1), NT)


def _xdot_left(m, x, dims=NN):
    hi, lo = _split(x)
    if dims == NN:
        return _dot(jnp.concatenate([m, m], axis=1), jnp.concatenate([hi, lo], axis=0), NN)
    return _dot(jnp.concatenate([m, m], axis=0), jnp.concatenate([hi, lo], axis=0), TN)


@jax.custom_vjp
def _xr(x, m):
    return _xdot_right(x, m)


def _xr_fwd(x, m):
    return _xdot_right(x, m), m


def _xr_bwd(m, g):
    return _xdot_right(g, m, NT), jnp.zeros_like(m)


_xr.defvjp(_xr_fwd, _xr_bwd)


def _norm_mod(x, g, sc, sh):
    r = lax.rsqrt(jnp.mean(x * x, axis=-1, keepdims=True) + EPS)
    return x * r * g * (1.0 + sc) + sh


MESH = pl.DeviceIdType.MESH
HBM_SPEC = pl.BlockSpec(memory_space=pltpu.HBM)


def _mesh_place():
    return lax.axis_index("x"), lax.axis_index("y"), lax.axis_index("c")


def _block_of(px, py, pc):
    return 4 * px + 2 * py + pc


def _sem_scratch(n):
    return [pltpu.SemaphoreType.DMA((n, N_DEV - 1)), pltpu.SemaphoreType.DMA((n, N_DEV - 1)), pltpu.SemaphoreType.DMA((n,))]


class _GatherPlan:
    def __init__(self, xs):
        self.xs = list(xs)
        self.n = len(self.xs)
        self.out_shape = [_sds((N_DEV, *v.shape), v.dtype) for v in self.xs]
        self.scratch = _sem_scratch(self.n)

    def _parts(self, x_refs, out_refs, sems):
        send_sems, recv_sems, local_sems = sems
        x, y, c = _mesh_place()
        me, sibling = (x, y, c), (x, y, 1 - c)
        chips = [(1 - x, y), (x, 1 - y), (1 - x, 1 - y)]

        def copy(a, k, block, to, src=None):
            rows = out_refs[a].at[_block_of(*block)]
            return pltpu.make_async_remote_copy(
                src_ref=rows if src is None else src, dst_ref=rows, send_sem=send_sems.at[a, k],
                recv_sem=recv_sems.at[a, k], device_id=to, device_id_type=MESH)

        local = [pltpu.make_async_copy(x_refs[a], out_refs[a].at[_block_of(*me)], local_sems.at[a])
                 for a in range(self.n)]
        first = []
        for a in range(self.n):
            first.append(copy(a, 0, me, sibling, src=x_refs[a]))
            first += [copy(a, 1 + j, me, (*chip, c), src=x_refs[a]) for j, chip in enumerate(chips)]
        return me, sibling, chips, c, copy, local, first

    def start(self, x_refs, out_refs, sems):
        *_, local, first = self._parts(x_refs, out_refs, sems)
        for cp in local + first:
            cp.start()

    def finish(self, x_refs, out_refs, sems):
        me, sibling, chips, c, copy, local, first = self._parts(x_refs, out_refs, sems)
        passed = []
        for j, chip in enumerate(chips):
            for a in range(self.n):
                copy(a, 1 + j, (*chip, c), me).wait_recv()
                fwd = copy(a, 4 + j, (*chip, c), sibling)
                fwd.start()
                passed.append(fwd)
        for a in range(self.n):
            copy(a, 0, sibling, me).wait_recv()
            for j, chip in enumerate(chips):
                copy(a, 4 + j, (*chip, 1 - c), me).wait_recv()
        for cp in first + passed:
            cp.wait_send()
        for cp in local:
            cp.wait()


class _ExchangePlan:
    def __init__(self, xs):
        self.xs = list(xs)
        self.n = len(self.xs)
        self.out_shape = [_sds(v.shape, v.dtype) for v in self.xs]
        self.scratch = _sem_scratch(self.n)

    def _parts(self, in_refs, out_refs, sems):
        send_sems, recv_sems, local_sems = sems
        x, y, c = _mesh_place()
        mine = _block_of(x, y, c)
        peers = [(1 - x if k & 4 else x, 1 - y if k & 2 else y, 1 - c if k & 1 else c) for k in range(1, N_DEV)]

        def copy(a, k, slot_src, slot_dst):
            return pltpu.make_async_remote_copy(
                src_ref=in_refs[a].at[slot_src], dst_ref=out_refs[a].at[slot_dst], send_sem=send_sems.at[a, k],
                recv_sem=recv_sems.at[a, k], device_id=peers[k], device_id_type=MESH)

        local = [pltpu.make_async_copy(in_refs[a].at[mine], out_refs[a].at[mine], local_sems.at[a])
                 for a in range(self.n)]
        sends = [copy(a, k, _block_of(*peers[k]), mine) for a in range(self.n) for k in range(N_DEV - 1)]
        arrivals = [copy(a, k, _block_of(*peers[k]), _block_of(*peers[k])) for a in range(self.n) for k in range(N_DEV - 1)]
        return local, sends, arrivals

    def start(self, in_refs, out_refs, sems):
        local, sends, _ = self._parts(in_refs, out_refs, sems)
        for cp in local + sends:
            cp.start()

    def finish(self, in_refs, out_refs, sems):
        local, sends, arrivals = self._parts(in_refs, out_refs, sems)
        for cp in arrivals:
            cp.wait_recv()
        for cp in sends:
            cp.wait_send()
        for cp in local:
            cp.wait()


def _call(body, args, *, grid, in_specs, out_specs, out_shape, scratch_shapes=(), semantics, name, comm=None):
    if comm is None:
        return pl.pallas_call(
            body, grid=grid, in_specs=list(in_specs), out_specs=list(out_specs), out_shape=list(out_shape),
            scratch_shapes=list(scratch_shapes), compiler_params=_params(*semantics), name=name)(*args)
    n_in, n_out, n_scr, n = len(in_specs), len(out_specs), len(scratch_shapes), comm.n

    def hosted(*refs):
        ins, rest = refs[:n_in], refs[n_in:]
        cin, rest = rest[:n], rest[n:]
        outs, rest = rest[:n_out], rest[n_out:]
        cout, rest = rest[:n], rest[n:]
        scr, sems = rest[:n_scr], rest[n_scr:]
        pids = [pl.program_id(d) for d in range(len(grid))]
        first = functools.reduce(jnp.logical_and, [p == 0 for p in pids])
        last = functools.reduce(jnp.logical_and, [p == g - 1 for p, g in zip(pids, grid)])

        @pl.when(first)
        def _():
            comm.start(cin, cout, sems)

        body(*ins, *outs, *scr)

        @pl.when(last)
        def _():
            comm.finish(cin, cout, sems)

    res = pl.pallas_call(
        hosted, grid=grid, in_specs=list(in_specs) + [HBM_SPEC] * n, out_specs=list(out_specs) + [HBM_SPEC] * n,
        out_shape=list(out_shape) + comm.out_shape, scratch_shapes=list(scratch_shapes) + comm.scratch,
        compiler_params=_params(*["arbitrary"] * len(grid)), name=name)(*args, *comm.xs)
    return res[:n_out], res[n_out:]


def _comm_alone(comm, name):
    def body(*refs):
        n = comm.n
        comm.start(refs[:n], refs[n:2 * n], refs[2 * n:])
        comm.finish(refs[:n], refs[n:2 * n], refs[2 * n:])

    return pl.pallas_call(
        body, in_specs=[HBM_SPEC] * comm.n, out_specs=[HBM_SPEC] * comm.n, out_shape=comm.out_shape,
        scratch_shapes=comm.scratch, name=name)(*comm.xs)


def _matmul(a, b, mode, out_dtype, tm, tn, tk, name, comm=None):
    if mode == "nn":
        (m, k), (_, n) = a.shape, b.shape
    elif mode == "nt":
        (m, k), (n, _) = a.shape, b.shape
    else:
        (k, m), (_, n) = a.shape, b.shape
    tm, tn, tk = min(tm, m), min(tn, n), min(tk, k)
    assert m % tm == 0 and n % tn == 0 and k % tk == 0, (name, m, n, k, tm, tn, tk)
    nk = k // tk
    dims = _DIMS[mode]

    def body(a_ref, b_ref, o_ref, acc_ref):
        if nk == 1:
            o_ref[...] = _dot(a_ref[...], b_ref[...], dims).astype(out_dtype)
            return
        kk = pl.program_id(2)

        @pl.when(kk == 0)
        def _():
            acc_ref[...] = _dot(a_ref[...], b_ref[...], dims)

        @pl.when((kk > 0) & (kk < nk - 1))
        def _():
            acc_ref[...] += _dot(a_ref[...], b_ref[...], dims)

        @pl.when(kk == nk - 1)
        def _():
            o_ref[...] = (acc_ref[...] + _dot(a_ref[...], b_ref[...], dims)).astype(out_dtype)

    if mode == "tn":
        a_spec = pl.BlockSpec((tk, tm), lambda i, j, kk: (kk, i))
        b_spec = pl.BlockSpec((tk, tn), lambda i, j, kk: (kk, j))
    elif mode == "nn":
        a_spec = pl.BlockSpec((tm, tk), lambda i, j, kk: (i, kk))
        b_spec = pl.BlockSpec((tk, tn), lambda i, j, kk: (kk, j))
    else:
        a_spec = pl.BlockSpec((tm, tk), lambda i, j, kk: (i, kk))
        b_spec = pl.BlockSpec((tn, tk), lambda i, j, kk: (j, kk))
    res = _call(
        body, (a, b), grid=(m // tm, n // tn, nk), in_specs=[a_spec, b_spec],
        out_specs=[pl.BlockSpec((tm, tn), lambda i, j, kk: (i, j))],
        out_shape=[_sds((m, n), out_dtype)], scratch_shapes=[pltpu.VMEM((tm, tn), F32)],
        semantics=("parallel", "parallel", "arbitrary"), name=name, comm=comm)
    return res[0] if comm is None else (res[0][0], res[1])


ROW_T = 512


def _prenorm(x, pv, name):
    s, d = x.shape
    t = min(ROW_T, s)

    def body(x_ref, pv_ref, h_ref):
        h = _norm_mod(x_ref[...], pv_ref[6:7, :], pv_ref[1:2, :], pv_ref[0:1, :])
        h_ref[...] = h.astype(BF16)

    return pl.pallas_call(
        body, grid=(s // t,),
        in_specs=[pl.BlockSpec((t, d), lambda i: (i, 0)), pl.BlockSpec((16, d), lambda i: (0, 0))],
        out_specs=pl.BlockSpec((t, d), lambda i: (i, 0)), out_shape=_sds((s, d), BF16),
        compiler_params=_params("parallel"), name=name)(x, pv)


DH_TK = 1920


def _dh_prenorm_bwd(dproj, win_t, dres, x, pv, name, comm=None):
    s, k = dproj.shape
    d = x.shape[1]
    t = min(ROW_T, s)
    tk = min(DH_TK, k)
    nk = k // tk

    def body(a_ref, b_ref, dres_ref, x_ref, pv_ref, dx_ref, sg_ref, acc_ref):
        i = pl.program_id(0)
        kk = pl.program_id(1)

        @pl.when((i == 0) & (kk == 0))
        def _():
            sg_ref[...] = jnp.zeros_like(sg_ref)

        @pl.when(kk == 0)
        def _():
            acc_ref[...] = _dot(a_ref[...], b_ref[...], NN)

        @pl.when((kk > 0) & (kk < nk - 1))
        def _():
            acc_ref[...] += _dot(a_ref[...], b_ref[...], NN)

        @pl.when(kk == nk - 1)
        def _():
            dh = acc_ref[...] + _dot(a_ref[...], b_ref[...], NN)
            _, vjp = jax.vjp(_norm_mod, x_ref[...], pv_ref[6:7, :], pv_ref[1:2, :], pv_ref[0:1, :])
            dx, dg, dsc, dsh = vjp(dh)
            dx_ref[...] = dres_ref[...] + dx
            sg_ref[0:1, :] += dsh
            sg_ref[1:2, :] += dsc
            sg_ref[2:3, :] += dg

    assert nk >= 2 and k % tk == 0, (k, tk)
    row = pl.BlockSpec((t, d), lambda i, kk: (i, 0))
    return _call(
        body, (dproj, win_t, dres, x, pv), grid=(s // t, nk),
        in_specs=[pl.BlockSpec((t, tk), lambda i, kk: (i, kk)), pl.BlockSpec((tk, d), lambda i, kk: (kk, 0)),
                  row, row, pl.BlockSpec((16, d), lambda i, kk: (0, 0))],
        out_specs=[row, pl.BlockSpec((8, d), lambda i, kk: (0, 0))],
        out_shape=[_sds((s, d), F32), _sds((8, d), F32)], scratch_shapes=[pltpu.VMEM((t, d), F32)],
        semantics=("arbitrary", "arbitrary"), name=name, comm=comm)


CONV_T = 512


def _conv_tile(a_ext, g_ext, w, b, ln_g, ln_b, n_out):
    u0 = a_ext * jax.nn.sigmoid(g_ext)
    off = CONV_HALO - (CONV_WIDTH - 1)
    acc = jnp.zeros((n_out, u0.shape[1]), F32) + b
    for r in range(8):
        taps = [k for k in range(CONV_WIDTH) if (off + k) % 8 == r]
        rows = n_out if r == 0 else n_out + 8
        part = None
        for k in taps:
            lo = (off + k) // 8 * 8
            term = w[k:k + 1, :] * u0[lo: lo + rows, :]
            part = term if part is None else part + term
        acc = acc + part[r: r + n_out, :]
    mu = jnp.mean(acc, axis=-1, keepdims=True)
    var = jnp.mean(jnp.square(acc - mu), axis=-1, keepdims=True)
    y = (acc - mu) * lax.rsqrt(var + EPS) * ln_g + ln_b
    return y * jax.nn.sigmoid(y)


def _conv_fwd(proj, conv_w, cp, name):
    s = proj.shape[0]
    t = min(CONV_T, s)
    c, h = CONV_CH, CONV_HALO

    def body(ap_ref, ac_ref, gp_ref, gc_ref, w_ref, cp_ref, o_ref):
        i = pl.program_id(0)
        live = (i > 0).astype(F32)
        a_ext = jnp.concatenate([ap_ref[t - h:, :] * live, ac_ref[...]], axis=0)
        g_ext = jnp.concatenate([gp_ref[t - h:, :], gc_ref[...]], axis=0)
        u = _conv_tile(a_ext, g_ext, w_ref[...], cp_ref[0:1, :], cp_ref[1:2, :], cp_ref[2:3, :], t)
        o_ref[...] = u.astype(BF16)

    prev = lambda col: pl.BlockSpec((t, c), lambda i: (jnp.maximum(i - 1, 0), col))
    cur = lambda col: pl.BlockSpec((t, c), lambda i: (i, col))
    return pl.pallas_call(
        body, grid=(s // t,),
        in_specs=[prev(0), cur(0), prev(1), cur(1),
                  pl.BlockSpec((CONV_WIDTH, c), lambda i: (0, 0)), pl.BlockSpec((8, c), lambda i: (0, 0))],
        out_specs=pl.BlockSpec((t, c), lambda i: (i, 0)), out_shape=_sds((s, c), BF16),
        compiler_params=_params("parallel"), name=name)(proj, proj, proj, proj, conv_w, cp)


def _conv_bwd(proj, do, conv_w, cp, name, comm=None):
    s = proj.shape[0]
    t = min(CONV_T, s)
    c, h = CONV_CH, CONV_HALO
    nt = s // t

    def body(ap_ref, ac_ref, an_ref, gp_ref, gc_ref, gn_ref, doc_ref, don_ref, w_ref, cp_ref,
             da_ref, dg_ref, dw_ref, sg_ref):
        i = pl.program_id(0)

        @pl.when(i == 0)
        def _():
            dw_ref[...] = jnp.zeros_like(dw_ref)
            sg_ref[...] = jnp.zeros_like(sg_ref)

        first = (i > 0).astype(F32)
        last = (i < nt - 1).astype(F32)
        a_ext = jnp.concatenate([ap_ref[t - h:, :] * first, ac_ref[...], an_ref[:h, :] * last], axis=0)
        g_ext = jnp.concatenate([gp_ref[t - h:, :], gc_ref[...], gn_ref[:h, :]], axis=0)
        fn = functools.partial(_conv_tile, n_out=t + h)
        _, vjp = jax.vjp(fn, a_ext, g_ext, w_ref[...], cp_ref[0:1, :], cp_ref[1:2, :], cp_ref[2:3, :])
        ct_own = jnp.concatenate([doc_ref[...], jnp.zeros((h, c), F32)], axis=0)
        ct_all = jnp.concatenate([doc_ref[...], don_ref[:h, :] * last], axis=0)
        _, _, dw, db, dlg, dlb = vjp(ct_own)
        da, dg, _, _, _, _ = vjp(ct_all)
        da_ref[...] = da[h:h + t, :].astype(BF16)
        dg_ref[...] = dg[h:h + t, :].astype(BF16)
        dw_ref[...] += dw
        sg_ref[0:1, :] += db
        sg_ref[1:2, :] += dlg
        sg_ref[2:3, :] += dlb

    prev = lambda col: pl.BlockSpec((t, c), lambda i: (jnp.maximum(i - 1, 0), col))
    cur = lambda col: pl.BlockSpec((t, c), lambda i: (i, col))
    nxt = lambda col: pl.BlockSpec((t, c), lambda i: (jnp.minimum(i + 1, nt - 1), col))
    return _call(
        body, (proj, proj, proj, proj, proj, proj, do, do, conv_w, cp), grid=(nt,),
        in_specs=[prev(0), cur(0), nxt(0), prev(1), cur(1), nxt(1), cur(0), nxt(0),
                  pl.BlockSpec((CONV_WIDTH, c), lambda i: (0, 0)), pl.BlockSpec((8, c), lambda i: (0, 0))],
        out_specs=[cur(0), cur(0), pl.BlockSpec((CONV_WIDTH, c), lambda i: (0, 0)),
                   pl.BlockSpec((8, c), lambda i: (0, 0))],
        out_shape=[_sds((s, c), BF16), _sds((s, c), BF16), _sds((CONV_WIDTH, c), F32), _sds((8, c), F32)],
        semantics=("arbitrary",), name=name, comm=comm)


def _hgrn_levels(c):
    out, m = [], c // 2
    while m >= 1:
        out.append(m)
        m //= 2
    return out


def _hgrn_consts(c):
    t = np.arange(c)[:, None]
    j = np.arange(c)[None, :]
    mats = [j <= t, j > t]
    for m in _hgrn_levels(c):
        same = (t // m) == (j // m)
        mats += [same & (j <= t), same & (j > t)]
    return jnp.asarray(np.concatenate(mats, axis=0).astype(np.float32), dtype=BF16)


@jax.custom_vjp
def _cums(lc, mall):
    c = lc.shape[0]
    full = _xdot_left(mall, lc)
    return tuple(full[i * c:(i + 1) * c, :] for i in range(mall.shape[0] // c))


def _cums_fwd(lc, mall):
    return _cums(lc, mall), mall


def _cums_bwd(mall, cts):
    return _xdot_left(mall, jnp.concatenate(cts, axis=0), TN), jnp.zeros_like(mall)


_cums.defvjp(_cums_fwd, _cums_bwd)


def _hgrn_chunk(q, f, v, g, lbs, ng, sts_in, mall):
    c = q.shape[0]
    keep = jax.nn.sigmoid(-f)
    if lbs:
        keep = (1.0 - jax.nn.sigmoid(lbs[1] - lbs[0])) * keep
    lc = jnp.log1p(-keep)
    qs = q * jax.nn.sigmoid(q)
    cs = _cums(lc, mall)
    q_in = qs * jnp.exp(cs[0])
    k_out = keep * jnp.exp(cs[1])
    decay = jnp.exp(jnp.sum(lc, axis=0, keepdims=True))
    qk = qs * keep
    r = lax.broadcasted_iota(jnp.int32, q.shape, 0)
    tt = lax.broadcasted_iota(jnp.int32, (c, c), 0)
    ss = lax.broadcasted_iota(jnp.int32, (c, c), 1)
    levels = []
    for li, m in enumerate(_hgrn_levels(c)):
        lg = m.bit_length() - 1
        odd = ((r >> lg) & 1) == 1
        qm = jnp.where(odd, qs * jnp.exp(cs[2 + 2 * li]), 0.0)
        km = jnp.where(odd, 0.0, keep * jnp.exp(cs[3 + 2 * li]))
        pair = (((tt >> lg) & 1) == 1) & ((ss >> lg) == (tt >> lg) - 1)
        levels.append((qm, km, pair))
    outs, sts_out = [], []
    for h, st_in in enumerate(sts_in):
        hs = slice(h * HG_DK, (h + 1) * HG_DK)
        vh = v[:, hs]
        sc = jnp.where(tt == ss, jnp.sum(qk[:, hs], axis=-1, keepdims=True), 0.0)
        for qm, km, pair in levels:
            sc = sc + jnp.where(pair, _bdot(qm[:, hs], km[:, hs], "nt"), 0.0)
        o = _bdot(q_in[:, hs], st_in, "nt") + _bdot(sc, vh, "nn")
        sts_out.append(st_in * decay[:, hs] + _bdot(vh, k_out[:, hs], "tn"))
        outs.append(o * lax.rsqrt(jnp.mean(o * o, axis=-1, keepdims=True) + EPS) * ng)
    return jnp.concatenate(outs, axis=1) * (g * jax.nn.sigmoid(g)), tuple(sts_out)


def _hgrn_fwd(proj, lb, ng, name, comm=None):
    s = proj.shape[0]
    c = HG_CHUNK
    nc = s // c
    mall = _hgrn_consts(c)
    col0 = 1024 // (HG_HEADS * HG_DK)

    def body(*refs):
        q_ref, f_ref, v_ref, g_ref = refs[:4]
        if lb is None:
            ng_ref, m_ref, y_ref, st_ref, scr = refs[4:]
        else:
            lb_ref, ng_ref, m_ref, y_ref, st_ref, scr = refs[4:]
        ci = pl.program_id(0)

        @pl.when(ci == 0)
        def _():
            scr[...] = jnp.zeros_like(scr)

        lbs = () if lb is None else (lb_ref[0:1, :], lb_ref[1:2, :])
        sts_in = tuple(scr[h] for h in range(HG_HEADS))
        for h in range(HG_HEADS):
            st_ref[h] = sts_in[h]
        y, sts_out = _hgrn_chunk(q_ref[...], f_ref[...], v_ref[...], g_ref[...], lbs, ng_ref[...], sts_in, m_ref[...])
        y_ref[...] = y.astype(BF16)
        for h in range(HG_HEADS):
            scr[h] = sts_out[h]

    w = HG_HEADS * HG_DK
    col = lambda k: pl.BlockSpec((c, w), lambda ci: (ci, col0 + k))
    in_specs = [col(0), col(1), col(2), col(3)]
    args = [proj, proj, proj, proj]
    if lb is not None:
        in_specs.append(pl.BlockSpec((2, w), lambda ci: (0, 0)))
        args.append(lb)
    in_specs += [pl.BlockSpec((1, HG_DK), lambda ci: (0, 0)), pl.BlockSpec(mall.shape, lambda ci: (0, 0))]
    args += [ng, mall]
    return _call(
        body, args, grid=(nc,), in_specs=in_specs,
        out_specs=[pl.BlockSpec((c, w), lambda ci: (ci, 0)),
                   pl.BlockSpec((HG_HEADS, None, HG_DK, HG_DK), lambda ci: (0, ci, 0, 0))],
        out_shape=[_sds((s, w), BF16), _sds((HG_HEADS, nc, HG_DK, HG_DK), F32)],
        scratch_shapes=[pltpu.VMEM((HG_HEADS, HG_DK, HG_DK), F32)],
        semantics=("arbitrary",), name=name, comm=comm)


def _hgrn_bwd(proj, states, dy, lb, ng, name, comm=None):
    s = proj.shape[0]
    c = HG_CHUNK
    nc = s // c
    mall = _hgrn_consts(c)
    col0 = 1024 // (HG_HEADS * HG_DK)

    def body(*refs):
        q_ref, f_ref, v_ref, g_ref, st_ref, dy_ref = refs[:6]
        if lb is None:
            ng_ref, m_ref, dq_ref, df_ref, dv_ref, dg_ref, dlb_ref, dng_ref, scr = refs[6:]
        else:
            lb_ref, ng_ref, m_ref, dq_ref, df_ref, dv_ref, dg_ref, dlb_ref, dng_ref, scr = refs[6:]
        ci = pl.program_id(0)

        @pl.when(ci == 0)
        def _():
            scr[...] = jnp.zeros_like(scr)
            dlb_ref[...] = jnp.zeros_like(dlb_ref)
            dng_ref[...] = jnp.zeros_like(dng_ref)

        mall_v = m_ref[...]
        fn = lambda q, f, v, g, lbs_, ng_, sts: _hgrn_chunk(q, f, v, g, lbs_, ng_, sts, mall_v)
        lbs = () if lb is None else (lb_ref[0:1, :], lb_ref[1:2, :])
        sts_in = tuple(st_ref[h] for h in range(HG_HEADS))
        _, vjp = jax.vjp(fn, q_ref[...], f_ref[...], v_ref[...], g_ref[...], lbs, ng_ref[...], sts_in)
        dq, df, dv, dg, dlbs, dng, dsts = vjp((dy_ref[...], tuple(scr[h] for h in range(HG_HEADS))))
        dq_ref[...] = dq.astype(BF16)
        df_ref[...] = df.astype(BF16)
        dv_ref[...] = dv.astype(BF16)
        dg_ref[...] = dg.astype(BF16)
        for h in range(HG_HEADS):
            scr[h] = dsts[h]
        dng_ref[0:1, :] += dng
        if lbs:
            dlb_ref[0:1, :] += dlbs[0]
            dlb_ref[1:2, :] += dlbs[1]

    w = HG_HEADS * HG_DK
    rev = lambda ci: nc - 1 - ci
    col = lambda k: pl.BlockSpec((c, w), lambda ci: (rev(ci), col0 + k))
    out_col = pl.BlockSpec((c, w), lambda ci: (rev(ci), 0))
    in_specs = [col(0), col(1), col(2), col(3),
                pl.BlockSpec((HG_HEADS, None, HG_DK, HG_DK), lambda ci: (0, rev(ci), 0, 0)), out_col]
    args = [proj, proj, proj, proj, states, dy]
    if lb is not None:
        in_specs.append(pl.BlockSpec((2, w), lambda ci: (0, 0)))
        args.append(lb)
    in_specs += [pl.BlockSpec((1, HG_DK), lambda ci: (0, 0)), pl.BlockSpec(mall.shape, lambda ci: (0, 0))]
    args += [ng, mall]
    return _call(
        body, args, grid=(nc,), in_specs=in_specs,
        out_specs=[out_col, out_col, out_col, out_col,
                   pl.BlockSpec((2, w), lambda ci: (0, 0)), pl.BlockSpec((8, HG_DK), lambda ci: (0, 0))],
        out_shape=[_sds((s, w), BF16)] * 4 + [_sds((2, w), F32), _sds((8, HG_DK), F32)],
        scratch_shapes=[pltpu.VMEM((HG_HEADS, HG_DK, HG_DK), F32)],
        semantics=("arbitrary",), name=name, comm=comm)


def _head_avg():
    w = SB_HEADS * SB_DH
    i = np.arange(w)
    return jnp.asarray(((i[:, None] // SB_DH) == (i[None, :] // SB_DH)).astype(np.float32) / SB_DH, dtype=BF16)


def _sb_norm(x, g_tiled, avg):
    ms = _xr(x * x, avg)
    return x * lax.rsqrt(ms + EPS) * g_tiled


def _sb_prep(proj, gq, gk, name):
    s = proj.shape[0]
    t = min(ROW_T, s)
    w = SB_HEADS * SB_DH
    avg = _head_avg()

    def body(q_ref, k_ref, v_ref, gq_ref, gk_ref, avg_ref, qn_ref, kn_ref, vb_ref):
        qn_ref[...] = _sb_norm(q_ref[...], gq_ref[...], avg_ref[...]).astype(BF16)
        kn_ref[...] = _sb_norm(k_ref[...], gk_ref[...], avg_ref[...]).astype(BF16)
        vb_ref[...] = v_ref[...].astype(BF16)

    col = lambda k: pl.BlockSpec((t, w), lambda i: (i, 6 + k))
    vec = pl.BlockSpec((1, w), lambda i: (0, 0))
    out = pl.BlockSpec((t, w), lambda i: (i, 0))
    return pl.pallas_call(
        body, grid=(s // t,), in_specs=[col(0), col(1), col(2), vec, vec, pl.BlockSpec((w, w), lambda i: (0, 0))],
        out_specs=[out, out, out], out_shape=[_sds((s, w), BF16)] * 3,
        compiler_params=_params("parallel"), name=name)(proj, proj, proj, gq, gk, avg)


def _sb_prep_bwd(proj, dqn, dkn, gq, gk, name):
    s = proj.shape[0]
    t = min(ROW_T, s)
    w = SB_HEADS * SB_DH
    avg = _head_avg()

    def body(q_ref, k_ref, dqn_ref, dkn_ref, gq_ref, gk_ref, avg_ref, dq_ref, dk_ref, sg_ref):
        i = pl.program_id(0)

        @pl.when(i == 0)
        def _():
            sg_ref[...] = jnp.zeros_like(sg_ref)

        avg_v = avg_ref[...]
        fn = lambda x, g: _sb_norm(x, g, avg_v)
        _, vq = jax.vjp(fn, q_ref[...], gq_ref[...])
        dq, dgq = vq(dqn_ref[...])
        _, vk = jax.vjp(fn, k_ref[...], gk_ref[...])
        dk, dgk = vk(dkn_ref[...])
        dq_ref[...] = dq.astype(BF16)
        dk_ref[...] = dk.astype(BF16)
        sg_ref[0:1, :] += dgq
        sg_ref[1:2, :] += dgk

    col = lambda k: pl.BlockSpec((t, w), lambda i: (i, 6 + k))
    vec = pl.BlockSpec((1, w), lambda i: (0, 0))
    row = pl.BlockSpec((t, w), lambda i: (i, 0))
    return pl.pallas_call(
        body, grid=(s // t,),
        in_specs=[col(0), col(1), row, row, vec, vec, pl.BlockSpec((w, w), lambda i: (0, 0))],
        out_specs=[row, row, pl.BlockSpec((8, w), lambda i: (0, 0))],
        out_shape=[_sds((s, w), BF16), _sds((s, w), BF16), _sds((8, w), F32)],
        compiler_params=_params("arbitrary"), name=name)(proj, proj, dqn, dkn, gq, gk, avg)


def _sb_tri(kind):
    j = np.arange(SB_BLK)[:, None]
    s = np.arange(SB_BLK)[None, :]
    tri = (j > s) if kind == "suffix" else (j < s)
    return jnp.asarray(np.concatenate([tri, np.ones_like(tri)], axis=1).astype(np.float32), dtype=BF16)


def _sb_scores(qm, kblk, mask):
    z = _dot(qm, kblk, NT) * (SB_DH ** -0.5)
    sp = jnp.maximum(z, 0.0) + jnp.log(1.0 + jnp.exp(-jnp.abs(z)))
    return z, sp, jnp.where(mask, -sp, 0.0)


def _sb_setup(b):
    lane = lax.broadcasted_iota(jnp.int32, (2 * b, b), 1)
    row = lax.broadcasted_iota(jnp.int32, (2 * b, b), 0)
    mine = (row >> (b.bit_length() - 1)) == (lane >> (SB_DH.bit_length() - 1))
    return lane, row & (b - 1), mine


def _sb_fwd(qn, kn, vb, name, comm=None):
    s, w = qn.shape
    b = SB_BLK
    nq = s // b
    tri = _sb_tri("suffix")

    def body(q_ref, k_ref, v_ref, tri_ref, o_ref):
        i = pl.program_id(1)
        lane, tt, mine = _sb_setup(b)
        q = q_ref[...]
        q2 = jnp.concatenate([q, q], axis=0)
        qm = jnp.where(mine, q2, jnp.zeros_like(q2))
        tri_v = tri_ref[...]

        def block(kb, lim, run, acc):
            off = pl.multiple_of(kb * b, b)
            kblk = k_ref[pl.ds(off, b), :]
            vblk = v_ref[pl.ds(off, b), :]
            mask = lane < lim
            z, sp, lk = _sb_scores(qm, kblk, mask)
            both = _xdot_right(lk, tri_v)
            a = jnp.where(mask, jnp.exp(z - sp + both[:, :b] + run), 0.0)
            return run + both[:, b:], acc + _dot(a.astype(BF16), vblk, NN)

        offs = [pl.multiple_of(jnp.maximum(i - j, 0) * b, b) for j in range(SB_FIXED)]
        masks = [lane < (tt if j == 0 else jnp.where(i >= j, b, 0)) for j in range(SB_FIXED)]
        scores = [_sb_scores(qm, k_ref[pl.ds(off, b), :], m) for off, m in zip(offs, masks)]
        boths = [_xdot_right(lk, tri_v) for _, _, lk in scores]
        run = acc = jnp.zeros((2 * b, b), F32)
        for j in range(SB_FIXED):
            z, sp, _ = scores[j]
            a = jnp.where(masks[j], jnp.exp(z - sp + boths[j][:, :b] + run), 0.0)
            acc = acc + _dot(a.astype(BF16), v_ref[pl.ds(offs[j], b), :], NN)
            run = run + boths[j][:, b:]

        def cond(carry):
            j, run_, _ = carry
            return (j <= i) & (jnp.max(run_) > SB_DEAD)

        def step(carry):
            j, run_, acc_ = carry
            run_, acc_ = block(i - j, b, run_, acc_)
            return j + 1, run_, acc_

        _, _, acc = lax.while_loop(cond, step, (jnp.int32(SB_FIXED), run, acc))
        o_ref[...] = jnp.where(lane[:b] < SB_DH, acc[:b], acc[b:]).astype(BF16)

    blk = pl.BlockSpec((b, b), lambda p, i: (i, p))
    full = pl.BlockSpec((s, b), lambda p, i: (0, p))
    return _call(
        body, (qn, kn, vb, tri), grid=(w // b, nq),
        in_specs=[blk, full, full, pl.BlockSpec(tri.shape, lambda p, i: (0, 0))],
        out_specs=[blk], out_shape=[_sds((s, w), BF16)],
        semantics=("parallel", "arbitrary"), name=name, comm=comm)


def _sb_bwd(qn, kn, vb, do, name, comm=None):
    s, w = qn.shape
    b = SB_BLK
    nq = s // b
    tri_s = _sb_tri("suffix")
    tri_p = _sb_tri("prefix")
    scale = SB_DH ** -0.5

    def body(q_ref, k_ref, v_ref, do_ref, ts_ref, tp_ref, dq_ref, dk_ref, dv_ref, dk_acc, dv_acc, dp_scr):
        i = pl.program_id(1)

        @pl.when(i == 0)
        def _():
            dk_acc[...] = jnp.zeros_like(dk_acc)
            dv_acc[...] = jnp.zeros_like(dv_acc)

        lane, tt, mine = _sb_setup(b)
        q = q_ref[...]
        q2 = jnp.concatenate([q, q], axis=0)
        qm = jnp.where(mine, q2, jnp.zeros_like(q2))
        dout = do_ref[...].astype(BF16)
        d2 = jnp.concatenate([dout, dout], axis=0)
        dom = jnp.where(mine, d2, jnp.zeros_like(d2))
        ts_v = ts_ref[...]
        tp_v = tp_ref[...]
        zero = jnp.zeros((2 * b, b), F32)

        def down(kb, lim, run):
            off = pl.multiple_of(kb * b, b)
            kblk = k_ref[pl.ds(off, b), :]
            vblk = v_ref[pl.ds(off, b), :]
            mask = lane < lim
            z, sp, lk = _sb_scores(qm, kblk, mask)
            both = _xdot_right(lk, ts_v)
            a = jnp.where(mask, jnp.exp(z - sp + both[:, :b] + run), 0.0)
            dv_acc[pl.ds(off, b), :] += _dot(a.astype(BF16), dom, TN)
            return _dot(dom, vblk, NT) * a, run + both[:, b:]

        def up(kb, lim, dp, pre, dq):
            off = pl.multiple_of(kb * b, b)
            kblk = k_ref[pl.ds(off, b), :]
            sig = jax.nn.sigmoid(_dot(qm, kblk, NT) * scale)
            both = _xdot_right(dp, tp_v)
            dz = jnp.where(lane < lim, dp * (1.0 - sig) - sig * (both[:, :b] + pre), 0.0) * scale
            dz = dz.astype(BF16)
            dk_acc[pl.ds(off, b), :] += _dot(dz, qm, TN)
            return pre + both[:, b:], dq + _dot(dz, kblk, NN)

        offs = [pl.multiple_of(jnp.maximum(i - j, 0) * b, b) for j in range(SB_FIXED)]
        masks = [lane < (tt if j == 0 else jnp.where(i >= j, b, 0)) for j in range(SB_FIXED)]
        kblks = [k_ref[pl.ds(off, b), :] for off in offs]
        scores = [_sb_scores(qm, kblk, m) for kblk, m in zip(kblks, masks)]
        das = [_dot(dom, v_ref[pl.ds(off, b), :], NT) for off in offs]
        boths = [_xdot_right(lk, ts_v) for _, _, lk in scores]
        run = zero
        dps = []
        for j in range(SB_FIXED):
            z, sp, _ = scores[j]
            a = jnp.where(masks[j], jnp.exp(z - sp + boths[j][:, :b] + run), 0.0)
            dps.append(das[j] * a)
            dv_acc[pl.ds(offs[j], b), :] += _dot(a.astype(BF16), dom, TN)
            run = run + boths[j][:, b:]

        def cond(carry):
            j, run_ = carry
            return (j <= i) & (jnp.max(run_) > SB_DEAD)

        def sweep_down(carry):
            j, run_ = carry
            dp, run_ = down(i - j, b, run_)
            dp_scr[i - j] = dp
            return j + 1, run_

        n_live, _ = lax.while_loop(cond, sweep_down, (jnp.int32(SB_FIXED), run))

        def sweep_up(jj, carry):
            kb = i - n_live + 1 + jj
            return up(kb, b, dp_scr[kb], *carry)

        pre, dq = lax.fori_loop(0, n_live - SB_FIXED, sweep_up, (zero, zero))
        pres = [_xdot_right(dp, tp_v) for dp in dps]
        for j in reversed(range(SB_FIXED)):
            z, sp, _ = scores[j]
            sig = jnp.exp(z - sp)
            dz = jnp.where(masks[j], dps[j] * (1.0 - sig) - sig * (pres[j][:, :b] + pre), 0.0) * scale
            dz = dz.astype(BF16)
            dk_acc[pl.ds(offs[j], b), :] += _dot(dz, qm, TN)
            dq = dq + _dot(dz, kblks[j], NN)
            pre = pre + pres[j][:, b:]
        dq_ref[...] = jnp.where(lane[:b] < SB_DH, dq[:b], dq[b:])

        @pl.when(i == nq - 1)
        def _():
            dk_ref[...] = dk_acc[...]
            dv_ref[...] = dv_acc[...].astype(BF16)

    blk = pl.BlockSpec((b, b), lambda p, i: (i, p))
    full = pl.BlockSpec((s, b), lambda p, i: (0, p))
    tri = pl.BlockSpec(tri_s.shape, lambda p, i: (0, 0))
    return _call(
        body, (qn, kn, vb, do, tri_s, tri_p), grid=(w // b, nq), in_specs=[blk, full, full, blk, tri, tri],
        out_specs=[blk, full, full], out_shape=[_sds((s, w), F32), _sds((s, w), F32), _sds((s, w), BF16)],
        scratch_shapes=[pltpu.VMEM((s, b), F32), pltpu.VMEM((s, b), F32), pltpu.VMEM((nq, 2 * b, b), F32)],
        semantics=("arbitrary", "arbitrary"), name=name, comm=comm)


MIX_T = 256
HALF = 512


def _gate_slices(ga, gb):
    return [(ga[:, 0:512], ga[:, 512:1024]), (ga[:, 1024:1536], gb[:, 0:512]), (gb[:, 512:1024], gb[:, 1024:1536])]


def _mix_fwd(u3, oh, osb, proj, x, pv, wc, wh, ws, wo, name):
    s, d = x.shape
    t = min(MIX_T, s)

    def body(u3_ref, oh_ref, os_ref, ga_ref, gb_ref, x_ref, pv_ref, wc_ref, wh_ref, ws_ref, wo_ref,
             x1_ref, h2_ref, mg_ref, mo_ref):
        ys = [_dot(u3_ref[...], wc_ref[...], NT), _dot(oh_ref[...], wh_ref[...], NT), _dot(os_ref[...], ws_ref[...], NT)]
        gl = _gate_slices(ga_ref[...], gb_ref[...])
        halves = []
        for hf in range(2):
            lo = hf * HALF
            acc = jnp.zeros((t, HALF), F32)
            for br in range(3):
                gate = jax.nn.sigmoid(gl[br][hf] + pv_ref[8 + br:9 + br, lo:lo + HALF])
                acc = acc + gate * ys[br][:, lo:lo + HALF]
            halves.append(acc)
        merged = jnp.concatenate(halves, axis=1).astype(BF16)
        mg_ref[...] = merged
        mo = _dot(merged, wo_ref[...], NN)
        mo_ref[...] = mo.astype(BF16)
        x1 = x_ref[...] + pv_ref[2:3, :] * mo
        x1_ref[...] = x1
        h2_ref[...] = _norm_mod(x1, pv_ref[7:8, :], pv_ref[4:5, :], pv_ref[3:4, :]).astype(BF16)

    br_spec = pl.BlockSpec((t, CONV_CH), lambda i: (i, 0))
    row = pl.BlockSpec((t, d), lambda i: (i, 0))
    wproj = pl.BlockSpec((d, CONV_CH), lambda i: (0, 0))
    return pl.pallas_call(
        body, grid=(s // t,),
        in_specs=[br_spec, br_spec, br_spec, pl.BlockSpec((t, 1536), lambda i: (i, 3)),
                  pl.BlockSpec((t, 1536), lambda i: (i, 4)), row, pl.BlockSpec((16, d), lambda i: (0, 0)),
                  wproj, wproj, wproj, pl.BlockSpec((d, d), lambda i: (0, 0))],
        out_specs=[row, row, row, row],
        out_shape=[_sds((s, d), F32), _sds((s, d), BF16), _sds((s, d), BF16), _sds((s, d), BF16)],
        compiler_params=_params("parallel"), name=name)(u3, oh, osb, proj, proj, x, pv, wc, wh, ws, wo)


def _mix_bwd(dx1, mo1, u3, oh, osb, proj, pv, wc, wh, ws, wo, name):
    s, d = dx1.shape
    t = min(MIX_T, s)

    def body(dx_ref, mo_ref, u3_ref, oh_ref, os_ref, ga_ref, gb_ref, pv_ref, wc_ref, wh_ref, ws_ref, wo_ref,
             dmo_ref, dyc_ref, dyh_ref, dys_ref, doc_ref, doh_ref, dos_ref, dgl_ref, sg_ref):
        i = pl.program_id(0)

        @pl.when(i == 0)
        def _():
            sg_ref[...] = jnp.zeros_like(sg_ref)

        dx = dx_ref[...]
        dmo = (dx * pv_ref[2:3, :]).astype(BF16)
        dmo_ref[...] = dmo
        sg_ref[0:1, :] += jnp.sum(dx * mo_ref[...].astype(F32), axis=0, keepdims=True)
        dmerged = _dot(dmo, wo_ref[...], NT)
        branches = [(u3_ref, wc_ref, dyc_ref, doc_ref), (oh_ref, wh_ref, dyh_ref, doh_ref), (os_ref, ws_ref, dys_ref, dos_ref)]
        gl = _gate_slices(ga_ref[...], gb_ref[...])
        for br, (o_ref, w_ref, dy_ref, do_ref) in enumerate(branches):
            y = _dot(o_ref[...], w_ref[...], NT)
            dys = []
            for hf in range(2):
                lo = hf * HALF
                gate = jax.nn.sigmoid(gl[br][hf] + pv_ref[8 + br:9 + br, lo:lo + HALF])
                dm = dmerged[:, lo:lo + HALF]
                dys.append(dm * gate)
                dgl = dm * y[:, lo:lo + HALF] * gate * (1.0 - gate)
                dgl_ref[:, br * d + lo: br * d + lo + HALF] = dgl.astype(BF16)
                sg_ref[1 + br:2 + br, lo:lo + HALF] += jnp.sum(dgl, axis=0, keepdims=True)
            dy = jnp.concatenate(dys, axis=1).astype(BF16)
            dy_ref[...] = dy
            do_ref[...] = _dot(dy, w_ref[...], NN)

    br_spec = pl.BlockSpec((t, CONV_CH), lambda i: (i, 0))
    row = pl.BlockSpec((t, d), lambda i: (i, 0))
    wproj = pl.BlockSpec((d, CONV_CH), lambda i: (0, 0))
    return pl.pallas_call(
        body, grid=(s // t,),
        in_specs=[row, row, br_spec, br_spec, br_spec, pl.BlockSpec((t, 1536), lambda i: (i, 3)),
                  pl.BlockSpec((t, 1536), lambda i: (i, 4)), pl.BlockSpec((16, d), lambda i: (0, 0)),
                  wproj, wproj, wproj, pl.BlockSpec((d, d), lambda i: (0, 0))],
        out_specs=[row, row, row, row, br_spec, br_spec, br_spec, pl.BlockSpec((t, 3 * d), lambda i: (i, 0)),
                   pl.BlockSpec((8, d), lambda i: (0, 0))],
        out_shape=[_sds((s, d), BF16)] * 4 + [_sds((s, CONV_CH), F32)] * 3 + [_sds((s, 3 * d), BF16), _sds((8, d), F32)],
        compiler_params=_params("arbitrary"), name=name)(dx1, mo1, u3, oh, osb, proj, proj, pv, wc, wh, ws, wo)


MLP_T = 512
MLP_F = 1024


def _mlp_fwd(h2, x1, pv, w1t, w2, name, comm=None):
    s, d = x1.shape
    t = min(MLP_T, s)
    nf = D_FF // MLP_F

    def body(h_ref, x_ref, pv_ref, w1_ref, w2_ref, x2_ref, mo_ref, acc_ref):
        f = pl.program_id(1)

        @pl.when(f == 0)
        def _():
            acc_ref[...] = jnp.zeros_like(acc_ref)

        a = jnp.maximum(_dot(h_ref[...], w1_ref[...], NT), 0.0)
        acc_ref[...] += _dot((a * a).astype(BF16), w2_ref[...], NN)

        @pl.when(f == nf - 1)
        def _():
            mo = acc_ref[...]
            mo_ref[...] = mo.astype(BF16)
            x2_ref[...] = x_ref[...] + pv_ref[5:6, :] * mo

    row = pl.BlockSpec((t, d), lambda i, f: (i, 0))
    wblk = pl.BlockSpec((MLP_F, d), lambda i, f: (f, 0))
    return _call(
        body, (h2, x1, pv, w1t, w2), grid=(s // t, nf),
        in_specs=[row, row, pl.BlockSpec((16, d), lambda i, f: (0, 0)), wblk, wblk],
        out_specs=[row, row], out_shape=[_sds((s, d), F32), _sds((s, d), BF16)],
        scratch_shapes=[pltpu.VMEM((t, d), F32)],
        semantics=("parallel", "arbitrary"), name=name, comm=comm)


def _mlp_bwd(dx2, h2, x1, mo2, pv, w1t, w2, name, comm=None):
    s, d = x1.shape
    t = min(MLP_T, s)
    nf = D_FF // MLP_F

    def body(dx_ref, h_ref, x_ref, mo_ref, pv_ref, w1_ref, w2_ref, dx1_ref, da_ref, b_ref, dmo_ref, sg_ref, acc_ref):
        i = pl.program_id(0)
        f = pl.program_id(1)

        @pl.when((i == 0) & (f == 0))
        def _():
            sg_ref[...] = jnp.zeros_like(sg_ref)

        @pl.when(f == 0)
        def _():
            acc_ref[...] = jnp.zeros_like(acc_ref)
            dx = dx_ref[...]
            dmo_ref[...] = (dx * pv_ref[5:6, :]).astype(BF16)
            sg_ref[0:1, :] += jnp.sum(dx * mo_ref[...].astype(F32), axis=0, keepdims=True)

        r = jnp.maximum(_dot(h_ref[...], w1_ref[...], NT), 0.0)
        b_ref[...] = (r * r).astype(BF16)
        da = (_dot(dmo_ref[...], w2_ref[...], NT) * (2.0 * r)).astype(BF16)
        da_ref[...] = da
        acc_ref[...] += _dot(da, w1_ref[...], NN)

        @pl.when(f == nf - 1)
        def _():
            _, vjp = jax.vjp(_norm_mod, x_ref[...], pv_ref[7:8, :], pv_ref[4:5, :], pv_ref[3:4, :])
            dxn, dg, dsc, dsh = vjp(acc_ref[...])
            dx1_ref[...] = dx_ref[...] + dxn
            sg_ref[1:2, :] += dsh
            sg_ref[2:3, :] += dsc
            sg_ref[3:4, :] += dg

    row = pl.BlockSpec((t, d), lambda i, f: (i, 0))
    wblk = pl.BlockSpec((MLP_F, d), lambda i, f: (f, 0))
    hid = pl.BlockSpec((t, MLP_F), lambda i, f: (i, f))
    return _call(
        body, (dx2, h2, x1, mo2, pv, w1t, w2), grid=(s // t, nf),
        in_specs=[row, row, row, row, pl.BlockSpec((16, d), lambda i, f: (0, 0)), wblk, wblk],
        out_specs=[row, hid, hid, row, pl.BlockSpec((8, d), lambda i, f: (0, 0))],
        out_shape=[_sds((s, d), F32), _sds((s, D_FF), BF16), _sds((s, D_FF), BF16), _sds((s, d), BF16), _sds((8, d), F32)],
        scratch_shapes=[pltpu.VMEM((t, d), F32)],
        semantics=("arbitrary", "arbitrary"), name=name, comm=comm)


def _loss_head(y, target, name):
    s, d = y.shape
    t = min(ROW_T, s)

    def body(y_ref, t_ref, dy_ref, ls_ref):
        i = pl.program_id(0)

        @pl.when(i == 0)
        def _():
            ls_ref[...] = jnp.zeros_like(ls_ref)

        e = y_ref[...] - t_ref[...]
        dy_ref[...] = e * (1.0 / d)
        ls_ref[...] += jnp.sum((e * e).reshape(t // 8, 8, d), axis=0)

    row = pl.BlockSpec((t, d), lambda i: (i, 0))
    return pl.pallas_call(
        body, grid=(s // t,), in_specs=[row, row], out_specs=[row, pl.BlockSpec((8, d), lambda i: (0, 0))],
        out_shape=[_sds((s, d), F32), _sds((8, d), F32)],
        compiler_params=_params("arbitrary"), name=name)(y, target)


def _layer_vectors(l, mod, sm):
    d = D_MODEL
    pv = jnp.concatenate([mod[l].reshape(6, d), sm["norm1_g"][l][None], sm["norm2_g"][l][None],
                          sm["gate_b"][l].reshape(3, d), jnp.zeros((5, d), F32)], axis=0)
    cp = jnp.concatenate([sm["conv_b"][l][None], sm["conv_ln_g"][l][None], sm["conv_ln_b"][l][None],
                          jnp.zeros((5, CONV_CH), F32)], axis=0)
    return dict(pv=pv, cp=cp, conv_w=sm["conv_w"][l], lb=(sm["hgrn_lb"] if l > 0 else None),
                ng=sm["hgrn_norm_g"][l][None], gq=jnp.tile(sm["sb_qn_g"][l], SB_HEADS)[None],
                gk=jnp.tile(sm["sb_kn_g"][l], SB_HEADS)[None])


def _hosted(res, comm):
    return res if comm is not None else (res, None)


def _layer_fwd_mixers(x, vec, win_t, tag, comm_proj=None, comm_hgrn=None, comm_sb=None):
    h = _prenorm(x, vec["pv"], f"prenorm{tag}")
    proj, got_proj = _hosted(_matmul(h, win_t, "nt", F32, 1024, 768, 1024, f"proj{tag}", comm_proj), comm_proj)
    u3 = _conv_fwd(proj, vec["conv_w"], vec["cp"], f"conv_fwd{tag}")
    (oh, states), got_hgrn = _hosted(_hgrn_fwd(proj, vec["lb"], vec["ng"], f"hgrn_fwd{tag}", comm_hgrn), comm_hgrn)
    qn, kn, vb = _sb_prep(proj, vec["gq"], vec["gk"], f"sb_prep{tag}")
    (osb,), got_sb = _hosted(_sb_fwd(qn, kn, vb, f"sb_fwd{tag}", comm_sb), comm_sb)
    saved = dict(x=x, h=h, proj=proj, u3=u3, oh=oh, states=states, qn=qn, kn=kn, vb=vb, osb=osb)
    return saved, (got_proj, got_hgrn, got_sb)


def _layer_fwd_out(sv, vec, w, tag, comm_mlp=None):
    x1, h2, merged, mo1 = _mix_fwd(sv["u3"], sv["oh"], sv["osb"], sv["proj"], sv["x"], vec["pv"],
                                   w["wc_t"], w["wh_t"], w["ws_t"], w["wo"], f"mix_fwd{tag}")
    (x2, mo2), got = _hosted(_mlp_fwd(h2, x1, vec["pv"], w["w1_t"], w["w2"], f"mlp_fwd{tag}", comm_mlp), comm_mlp)
    sv.update(x1=x1, h2=h2, merged=merged, mo1=mo1, mo2=mo2)
    return x2, got


def _layer_bwd(dx2, sv, vec, w, tag, plans=None):
    plans = plans or {}
    got = {}

    def plan_for(key, big_now):
        return plans[key](big_now) if key in plans else None

    pv = vec["pv"]
    big = {}
    comm = plan_for("mlp", big)
    (dx1, da, bsq, dmo2, sg_mlp), got["mlp"] = _hosted(
        _mlp_bwd(dx2, sv["h2"], sv["x1"], sv["mo2"], pv, w["w1_t"], w["w2"], f"mlp_bwd{tag}", comm), comm)
    big["w1_t"] = _matmul(da, sv["h2"], "tn", BF16, 1024, 1024, 1024, f"dw1{tag}")
    big["w2"] = _matmul(bsq, dmo2, "tn", BF16, 1024, 1024, 1024, f"dw2{tag}")
    dmo1, dyc, dyh, dys, doc, doh, dos, dgl, sg_mix = _mix_bwd(
        dx1, sv["mo1"], sv["u3"], sv["oh"], sv["osb"], sv["proj"], pv, w["wc_t"], w["wh_t"], w["ws_t"], w["wo"], f"mix_bwd{tag}")
    big["wo"] = _matmul(sv["merged"], dmo1, "tn", BF16, 1024, 1024, 1024, f"dwo{tag}")
    big["wc_t"] = _matmul(dyc, sv["u3"], "tn", BF16, 1024, 512, 1024, f"dwc{tag}")
    big["wh_t"] = _matmul(dyh, sv["oh"], "tn", BF16, 1024, 512, 1024, f"dwh{tag}")
    big["ws_t"] = _matmul(dys, sv["osb"], "tn", BF16, 1024, 512, 1024, f"dws{tag}")
    comm = plan_for("conv", big)
    (da_c, dg_c, dconv_w, sg_conv), got["conv"] = _hosted(
        _conv_bwd(sv["proj"], doc, vec["conv_w"], vec["cp"], f"conv_bwd{tag}", comm), comm)
    comm = plan_for("hgrn", big)
    (dq_h, df_h, di_h, dg_h, dlb, dng), got["hgrn"] = _hosted(
        _hgrn_bwd(sv["proj"], sv["states"], doh, vec["lb"], vec["ng"], f"hgrn_bwd{tag}", comm), comm)
    comm = plan_for("sb", big)
    (dqn, dkn, dv_s), got["sb"] = _hosted(_sb_bwd(sv["qn"], sv["kn"], sv["vb"], dos, f"sb_bwd{tag}", comm), comm)
    dq_s, dk_s, sg_sb = _sb_prep_bwd(sv["proj"], dqn, dkn, vec["gq"], vec["gk"], f"sb_prep_bwd{tag}")
    dproj = jnp.concatenate([da_c, dg_c, dq_h, df_h, di_h, dg_h, dq_s, dk_s, dv_s, dgl], axis=1)
    half = D_MODEL // 2
    comm = plan_for("dwin_a", big)
    big["win_a"], got["dwin_a"] = _hosted(
        _matmul(dproj, sv["h"][:, :half], "tn", BF16, 768, half, 1024, f"dwin_a{tag}", comm), comm)
    comm = plan_for("dwin", big)
    big["win_b"], got["dwin"] = _hosted(
        _matmul(dproj, sv["h"][:, half:], "tn", BF16, 768, half, 1024, f"dwin_b{tag}", comm), comm)
    comm = plan_for("dh", big)
    (dx, sg_pre), got["dh"] = _hosted(_dh_prenorm_bwd(dproj, w["win_t"], dx1, sv["x"], pv, f"dh{tag}", comm), comm)
    small = dict(
        mod=jnp.stack([sg_pre[0], sg_pre[1], sg_mix[0], sg_mlp[1], sg_mlp[2], sg_mlp[0]]).reshape(6 * D_MODEL),
        norm1_g=sg_pre[2], norm2_g=sg_mlp[3], gate_b=sg_mix[1:4].reshape(3 * D_MODEL),
        conv_w=dconv_w, conv_b=sg_conv[0], conv_ln_g=sg_conv[1], conv_ln_b=sg_conv[2],
        hgrn_lb=dlb, hgrn_norm_g=dng[0],
        sb_qn_g=sg_sb[0].reshape(SB_HEADS, SB_DH).sum(0), sb_kn_g=sg_sb[1].reshape(SB_HEADS, SB_DH).sum(0))
    return dx, big, small, got


def _row_tile(r, cap=512):
    t = min(r, cap)
    while r % t or (t % 8 and t != r):
        t -= 1
    return t


def _sum8(z, name):
    _, r, c = z.shape
    t = _row_tile(r, 128 if c >= 1024 else 512)

    def body(z_ref, o_ref):
        acc = z_ref[0].astype(F32)
        for j in range(1, N_DEV):
            acc = acc + z_ref[j].astype(F32)
        o_ref[...] = acc

    return pl.pallas_call(
        body, grid=(r // t,), in_specs=[pl.BlockSpec((N_DEV, t, c), lambda i: (0, i, 0))],
        out_specs=pl.BlockSpec((t, c), lambda i: (i, 0)), out_shape=_sds((r, c), F32),
        compiler_params=_params("parallel"), name=name)(z)


def _adamw(w, g, m, v, name):
    r, c = w.shape
    t = _row_tile(r, 256)

    def body(w_ref, g_ref, m_ref, v_ref, d_ref, nm_ref, nv_ref):
        g_ = g_ref[...]
        nm = ADAM_B1 * m_ref[...] + (1.0 - ADAM_B1) * g_
        nv = ADAM_B2 * v_ref[...] + (1.0 - ADAM_B2) * jnp.square(g_)
        m_hat = nm / (1.0 - ADAM_B1 ** ADAM_STEP)
        v_hat = nv / (1.0 - ADAM_B2 ** ADAM_STEP)
        d_ref[...] = -ADAM_LR * (m_hat / (jnp.sqrt(v_hat) + ADAM_EPS) + ADAM_WD * w_ref[...])
        nm_ref[...] = nm
        nv_ref[...] = nv

    blk = pl.BlockSpec((t, c), lambda i: (i, 0))
    return pl.pallas_call(
        body, grid=(r // t,), in_specs=[blk] * 4, out_specs=[blk] * 3, out_shape=[_sds((r, c), F32)] * 3,
        compiler_params=_params("parallel"), name=name)(w, g, m, v)


def _mod_local(c_all, mod_w, name):
    depth, d, cols = mod_w.shape

    def body(c_ref, w_ref, o_ref):
        cv = c_ref[...]
        act = cv * jax.nn.sigmoid(cv)
        o_ref[...] = jnp.dot(act, w_ref[...], precision=lax.Precision.HIGHEST, preferred_element_type=F32)

    return pl.pallas_call(
        body, grid=(depth,),
        in_specs=[pl.BlockSpec((N_DEV, d), lambda l: (0, 0)), pl.BlockSpec((None, d, cols), lambda l: (l, 0, 0))],
        out_specs=pl.BlockSpec((None, N_DEV, cols), lambda l: (l, 0, 0)), out_shape=_sds((depth, N_DEV, cols), F32),
        compiler_params=_params("parallel"), name=name)(c_all, mod_w)


def _modw_grad(c_all, dmod, name):
    depth, _, cols = dmod.shape
    d = c_all.shape[1]

    def body(c_ref, g_ref, o_ref):
        cv = c_ref[...]
        act = cv * jax.nn.sigmoid(cv)
        o_ref[...] = lax.dot_general(act, g_ref[...], (TN, ((), ())), precision=lax.Precision.HIGHEST,
                                     preferred_element_type=F32)

    return pl.pallas_call(
        body, grid=(depth,),
        in_specs=[pl.BlockSpec((N_DEV, d), lambda l: (0, 0)), pl.BlockSpec((None, N_DEV, cols), lambda l: (l, 0, 0))],
        out_specs=pl.BlockSpec((None, d, cols), lambda l: (l, 0, 0)), out_shape=_sds((depth, d, cols), F32),
        compiler_params=_params("parallel"), name=name)(c_all, dmod)


LANE = 128
BIG = {"w_in": ("win_t", True), "w_out": ("wo", False), "mlp_w2": ("w2", False), "mlp_w1": ("w1_t", True),
       "w_conv_proj": ("wc_t", True), "w_hgrn_proj": ("wh_t", True), "w_sb_proj": ("ws_t", True)}
PROJS = ("w_conv_proj", "w_hgrn_proj", "w_sb_proj")
SMALL = (("mod_b", 6144), ("norm1_g", 1024), ("gate_b", 3072), ("conv_w", CONV_WIDTH * CONV_CH), ("conv_b", 512),
         ("conv_ln_g", 512), ("conv_ln_b", 512), ("hgrn_lb", 512), ("hgrn_norm_g", 128), ("sb_qn_g", 64),
         ("sb_kn_g", 64), ("norm2_g", 1024))


def _pack_rows(parts, width):
    flat = jnp.concatenate([p.reshape(-1) for p in parts])
    rows = -(-flat.shape[0] // width)
    rows = -(-rows // 8) * 8
    return jnp.pad(flat, (0, rows * width - flat.shape[0])).reshape(rows, width)


def _shards(params, items):
    return [(params[n][l].T if BIG[n][1] else params[n][l]).astype(BF16) for n, l in items]


def _gathered(items, got):
    return {BIG[n][0]: g.reshape(-1, g.shape[2]) for (n, _), g in zip(items, got)}


def _by_shard(g):
    return g.reshape(N_DEV, g.shape[0] // N_DEV, g.shape[1])


def _adamw_nd(w, g, m, v, name):
    shape = w.shape
    two = lambda a: a.reshape(-1, shape[-1])
    return [o.reshape(shape) for o in _adamw(two(w), two(g), two(m), two(v), name)]


WEIGHTS = ("mod_w", "mod_b", "norm1_g", "w_in", "gate_b", "conv_w", "conv_b", "conv_ln_g", "conv_ln_b", "w_conv_proj",
           "hgrn_lb", "hgrn_norm_g", "w_hgrn_proj", "sb_qn_g", "sb_kn_g", "w_sb_proj", "w_out", "norm2_g", "mlp_w1",
           "mlp_w2")


def kernel(x, c, mod_w, mod_b, norm1_g, w_in, gate_b, conv_w, conv_b, conv_ln_g, conv_ln_b, w_conv_proj, hgrn_lb, hgrn_norm_g, w_hgrn_proj, sb_qn_g, sb_kn_g, w_sb_proj, w_out, norm2_g, mlp_w1, mlp_w2, loss_target, m_mod_w, m_mod_b, m_norm1_g, m_w_in, m_gate_b, m_conv_w, m_conv_b, m_conv_ln_g, m_conv_ln_b, m_w_conv_proj, m_hgrn_lb, m_hgrn_norm_g, m_w_hgrn_proj, m_sb_qn_g, m_sb_kn_g, m_w_sb_proj, m_w_out, m_norm2_g, m_mlp_w1, m_mlp_w2, v_mod_w, v_mod_b, v_norm1_g, v_w_in, v_gate_b, v_conv_w, v_conv_b, v_conv_ln_g, v_conv_ln_b, v_w_conv_proj, v_hgrn_lb, v_hgrn_norm_g, v_w_hgrn_proj, v_sb_qn_g, v_sb_kn_g, v_w_sb_proj, v_w_out, v_norm2_g, v_mlp_w1, v_mlp_w2):
    params = dict(mod_w=mod_w, mod_b=mod_b, norm1_g=norm1_g, w_in=w_in, gate_b=gate_b, conv_w=conv_w, conv_b=conv_b,
                  conv_ln_g=conv_ln_g, conv_ln_b=conv_ln_b, w_conv_proj=w_conv_proj, hgrn_lb=hgrn_lb,
                  hgrn_norm_g=hgrn_norm_g, w_hgrn_proj=w_hgrn_proj, sb_qn_g=sb_qn_g, sb_kn_g=sb_kn_g,
                  w_sb_proj=w_sb_proj, w_out=w_out, norm2_g=norm2_g, mlp_w1=mlp_w1, mlp_w2=mlp_w2)
    mom1 = dict(mod_w=m_mod_w, mod_b=m_mod_b, norm1_g=m_norm1_g, w_in=m_w_in, gate_b=m_gate_b, conv_w=m_conv_w,
                conv_b=m_conv_b, conv_ln_g=m_conv_ln_g, conv_ln_b=m_conv_ln_b, w_conv_proj=m_w_conv_proj,
                hgrn_lb=m_hgrn_lb, hgrn_norm_g=m_hgrn_norm_g, w_hgrn_proj=m_w_hgrn_proj, sb_qn_g=m_sb_qn_g,
                sb_kn_g=m_sb_kn_g, w_sb_proj=m_w_sb_proj, w_out=m_w_out, norm2_g=m_norm2_g, mlp_w1=m_mlp_w1,
                mlp_w2=m_mlp_w2)
    mom2 = dict(mod_w=v_mod_w, mod_b=v_mod_b, norm1_g=v_norm1_g, w_in=v_w_in, gate_b=v_gate_b, conv_w=v_conv_w,
                conv_b=v_conv_b, conv_ln_g=v_conv_ln_g, conv_ln_b=v_conv_ln_b, w_conv_proj=v_w_conv_proj,
                hgrn_lb=v_hgrn_lb, hgrn_norm_g=v_hgrn_norm_g, w_hgrn_proj=v_w_hgrn_proj, sb_qn_g=v_sb_qn_g,
                sb_kn_g=v_sb_kn_g, w_sb_proj=v_w_sb_proj, w_out=v_w_out, norm2_g=v_norm2_g, mlp_w1=v_mlp_w1,
                mlp_w2=v_mlp_w2)
    xi, yi, ci = _mesh_place()
    me = _block_of(xi, yi, ci)
    cw_cols = conv_w.shape[2]

    tiny = _pack_rows([c, conv_w], LANE)
    g_tiny, g_win0 = _comm_alone(_GatherPlan([tiny] + _shards(params, [("w_in", 0)])), "gather_first")
    c_rows = D_MODEL // LANE
    c_all = g_tiny[:, :c_rows].reshape(N_DEV, D_MODEL)
    n_cw = DEPTH * CONV_WIDTH * cw_cols
    conv_w_full = g_tiny[:, c_rows:c_rows + n_cw // LANE].reshape(N_DEV, DEPTH, CONV_WIDTH, cw_cols)
    conv_w_full = conv_w_full.transpose(1, 2, 0, 3).reshape(DEPTH, CONV_WIDTH, CONV_CH)

    (g_mod,) = _comm_alone(_GatherPlan([_mod_local(c_all, mod_w, "mod_local")]), "gather_mod")
    mod = lax.dynamic_index_in_dim(g_mod, me, axis=2, keepdims=False)
    mod = mod.transpose(1, 0, 2).reshape(DEPTH, 6 * D_MODEL) + mod_b

    sm = dict(norm1_g=norm1_g, norm2_g=norm2_g, gate_b=gate_b, conv_w=conv_w_full, conv_b=conv_b, conv_ln_g=conv_ln_g,
              conv_ln_b=conv_ln_b, hgrn_lb=hgrn_lb, hgrn_norm_g=hgrn_norm_g, sb_qn_g=sb_qn_g, sb_kn_g=sb_kn_g)
    vecs = [_layer_vectors(l, mod, sm) for l in range(DEPTH)]

    fwd_hosts = dict(
        proj=[("w_in", 1)],
        hgrn=[("mlp_w2", 0), ("w_out", 0)] + [(n, 0) for n in PROJS],
        sb=[("mlp_w1", 0), ("mlp_w2", 1)],
        mlp=[("mlp_w1", 1), ("w_out", 1)] + [(n, 1) for n in PROJS])
    gather = {host: _GatherPlan(_shards(params, items)) for host, items in fwd_hosts.items()}
    wts = [_gathered([("w_in", 0)], [g_win0]), {}]
    sv0, got_mixers = _layer_fwd_mixers(x[0], vecs[0], wts[0]["win_t"], "_l0", gather["proj"], gather["hgrn"], gather["sb"])
    for host, got in zip(("proj", "hgrn", "sb"), got_mixers):
        for (name, l), g in zip(fwd_hosts[host], got):
            wts[l].update(_gathered([(name, l)], [g]))
    y, got = _layer_fwd_out(sv0, vecs[0], wts[0], "_l0", gather["mlp"])
    wts[1].update(_gathered(fwd_hosts["mlp"], got))
    sv1, _ = _layer_fwd_mixers(y, vecs[1], wts[1]["win_t"], "_l1")
    y, _ = _layer_fwd_out(sv1, vecs[1], wts[1], "_l1")
    dy, sq = _loss_head(y, loss_target[0], "loss_head")
    loss = lax.psum(0.5 * jnp.sum(sq) / D_MODEL, ("x", "y", "c"))

    half = D_MODEL // 2
    dy, big1, small1, _ = _layer_bwd(dy, sv1, vecs[1], wts[1], "_l1")
    bwd_hosts = dict(
        mlp=[(1, "win_a"), (1, "wo")],
        conv=[(1, "win_b"), (0, "wc_t"), (0, "wh_t"), (0, "ws_t")],
        hgrn=[(1, "w2"), (0, "wo"), (0, "w2_a")],
        sb=[(1, "w1_t"), (0, "w1_t"), (1, "wc_t"), (1, "wh_t"), (1, "ws_t")],
        dwin_a=[(0, "w2_b")],
        dwin=[(0, "win_a")],
        dh=[(0, "win_b")])

    def source(l, key, big0):
        big = big1 if l == 1 else big0
        if key in ("w2_a", "w2_b"):
            return big["w2"][:, :half] if key == "w2_a" else big["w2"][:, half:]
        return big[key]

    plans = {host: (lambda big0, items=items: _ExchangePlan([_by_shard(source(l, k, big0)) for l, k in items]))
             for host, items in bwd_hosts.items()}
    dx, _, small0, got = _layer_bwd(dy, sv0, vecs[0], wts[0], "_l0", plans)
    smalls = [small0, small1]
    summed = {}
    for host, items in bwd_hosts.items():
        for (l, key), arrived in zip(items, got[host]):
            summed[l, key] = _sum8(arrived, f"sum_{key}_l{l}")
    summed[0, "w2"] = jnp.concatenate([summed[0, "w2_a"], summed[0, "w2_b"]], axis=1)
    grads = {}
    for name, (key, transposed) in BIG.items():
        per_layer = []
        for l in range(DEPTH):
            if name == "w_in":
                blk = jnp.concatenate([summed[l, "win_a"], summed[l, "win_b"]], axis=1)
            else:
                blk = summed[l, key]
            per_layer.append(blk.T if transposed else blk)
        grads[name] = jnp.stack(per_layer)

    small_parts = []
    for name, _ in SMALL:
        key = "mod" if name == "mod_b" else name
        if name == "hgrn_lb":
            small_parts.append(smalls[0][key] + smalls[1][key])
        else:
            small_parts.append(jnp.stack([smalls[l][key] for l in range(DEPTH)]))
    (g_small,) = _comm_alone(_GatherPlan([_pack_rows(small_parts, LANE)]), "gather_small_grads")
    small_sum = _sum8(g_small, "sum_small_grads").reshape(-1)
    off = 0
    for name, per_layer in SMALL:
        grads[name] = small_sum[off:off + DEPTH * per_layer].reshape(params[name].shape if name != "conv_w" else (DEPTH, CONV_WIDTH, CONV_CH))
        off += DEPTH * per_layer
    grads["conv_w"] = lax.dynamic_slice_in_dim(grads["conv_w"], me * cw_cols, cw_cols, axis=2)
    cols = mod_w.shape[2]
    dmod_all = g_small.reshape(N_DEV, -1)[:, :DEPTH * 6 * D_MODEL].reshape(N_DEV, DEPTH, 6 * D_MODEL)
    dmod_mine = lax.dynamic_slice_in_dim(dmod_all, me * cols, cols, axis=2).transpose(1, 0, 2)
    grads["mod_w"] = _modw_grad(c_all, dmod_mine, "mod_w_grad")

    delta, new_m, new_v = {}, {}, {}
    small_names = [n for n, _ in SMALL]
    for name in WEIGHTS:
        if name not in small_names:
            delta[name], new_m[name], new_v[name] = _adamw_nd(params[name], grads[name], mom1[name], mom2[name], f"adamw_{name}")
    packed = [_pack_rows([d[n] for n in small_names], LANE) for d in (params, grads, mom1, mom2)]
    outs = [o.reshape(-1) for o in _adamw(*packed, "adamw_small")]
    off = 0
    for name in small_names:
        size = params[name].size
        for dst, o in zip((delta, new_m, new_v), outs):
            dst[name] = o[off:off + size].reshape(params[name].shape)
        off += size
    return (loss, dx[None], *[grads[n] for n in WEIGHTS], *[delta[n] for n in WEIGHTS],
            *[new_m[n] for n in WEIGHTS], *[new_v[n] for n in WEIGHTS])
```

```python
import functools

import jax
import jax.numpy as jnp
import numpy as np
from jax import lax
from jax.experimental import pallas as pl
from jax.experimental.pallas import tpu as pltpu

F32 = jnp.float32
BF16 = jnp.bfloat16

D_MODEL = 1024
DEPTH = 2
N_DEV = 8
CONV_CH = 512
CONV_WIDTH = 31
CONV_HALO = 32
HG_HEADS = 4
HG_DK = 128
SB_HEADS = 8
SB_DH = 64
D_IN = 7680
D_FF = 4096
EPS = 1e-6
SB_BLK = 128
SB_DEAD = -104.0
SB_FIXED = 3
SB_PAIRS = 2
HG_CHUNK = 128

ADAM_LR = 0.001
ADAM_B1 = 0.9
ADAM_B2 = 0.999
ADAM_EPS = 1e-08
ADAM_WD = 0.01
ADAM_STEP = 10

VMEM_LIMIT = 48 * 1024 * 1024

NN = ((1,), (0,))
NT = ((1,), (1,))
TN = ((0,), (0,))
_DIMS = {"nn": NN, "nt": NT, "tn": TN}


def _sds(shape, dtype):
    return jax.ShapeDtypeStruct(shape, dtype)


def _params(*semantics):
    return pltpu.CompilerParams(dimension_semantics=semantics, vmem_limit_bytes=VMEM_LIMIT)


def _dot(a, b, dims):
    return lax.dot_general(a, b, (dims, ((), ())), preferred_element_type=F32)


@functools.partial(jax.custom_vjp, nondiff_argnums=(2,))
def _bdot(a, b, mode):
    return _dot(a.astype(BF16), b.astype(BF16), _DIMS[mode])


def _bdot_fwd(a, b, mode):
    return _bdot(a, b, mode), (a.astype(BF16), b.astype(BF16))


def _bdot_bwd(mode, res, g):
    a, b = res
    g = g.astype(BF16)
    if mode == "nn":
        return _dot(g, b, NT), _dot(a, g, TN)
    if mode == "nt":
        return _dot(g, b, NN), _dot(g, a, TN)
    return _dot(b, g, NT), _dot(a, g, NN)


_bdot.defvjp(_bdot_fwd, _bdot_bwd)


def _split(x):
    hi = x.astype(BF16)
    lo = (x - hi.astype(F32)).astype(BF16)
    return hi, lo


def _xdot_right(x, m, dims=NN):
    hi, lo = _split(x)
    if dims == NN:
        return _dot(jnp.concatenate([hi, lo], axis=1), jnp.concatenate([m, m], axis=0), NN)
    return _dot(jnp.concatenate([hi, lo], axis=1), jnp.concatenate([m, m], axis=1), NT)


def _xdot_left(m, x, dims=NN):
    hi, lo = _split(x)
    if dims == NN:
        return _dot(jnp.concatenate([m, m], axis=1), jnp.concatenate([hi, lo], axis=0), NN)
    return _dot(jnp.concatenate([m, m], axis=0), jnp.concatenate([hi, lo], axis=0), TN)


@jax.custom_vjp
def _xr(x, m):
    return _xdot_right(x, m)


def _xr_fwd(x, m):
    return _xdot_right(x, m), m


def _xr_bwd(m, g):
    return _xdot_right(g, m, NT), jnp.zeros_like(m)


_xr.defvjp(_xr_fwd, _xr_bwd)


def _norm_mod(x, g, sc, sh):
    r = lax.rsqrt(jnp.mean(x * x, axis=-1, keepdims=True) + EPS)
    return x * r * g * (1.0 + sc) + sh


MESH = pl.DeviceIdType.MESH
HBM_SPEC = pl.BlockSpec(memory_space=pltpu.HBM)


def _mesh_place():
    return lax.axis_index("x"), lax.axis_index("y"), lax.axis_index("c")


def _block_of(px, py, pc):
    return 4 * px + 2 * py + pc


def _sem_scratch(n):
    return [pltpu.SemaphoreType.DMA((n, N_DEV - 1)), pltpu.SemaphoreType.DMA((n, N_DEV - 1)), pltpu.SemaphoreType.DMA((n,))]


class _GatherPlan:
    def __init__(self, xs):
        self.xs = list(xs)
        self.n = len(self.xs)
        self.out_shape = [_sds((N_DEV, *v.shape), v.dtype) for v in self.xs]
        self.scratch = _sem_scratch(self.n)

    def _parts(self, x_refs, out_refs, sems):
        send_sems, recv_sems, local_sems = sems
        x, y, c = _mesh_place()
        me, sibling = (x, y, c), (x, y, 1 - c)
        chips = [(1 - x, y), (x, 1 - y), (1 - x, 1 - y)]

        def copy(a, k, block, to, src=None):
            rows = out_refs[a].at[_block_of(*block)]
            return pltpu.make_async_remote_copy(
                src_ref=rows if src is None else src, dst_ref=rows, send_sem=send_sems.at[a, k],
                recv_sem=recv_sems.at[a, k], device_id=to, device_id_type=MESH)

        local = [pltpu.make_async_copy(x_refs[a], out_refs[a].at[_block_of(*me)], local_sems.at[a])
                 for a in range(self.n)]
        first = []
        for a in range(self.n):
            first.append(copy(a, 0, me, sibling, src=x_refs[a]))
            first += [copy(a, 1 + j, me, (*chip, c), src=x_refs[a]) for j, chip in enumerate(chips)]
        return me, sibling, chips, c, copy, local, first

    def start(self, x_refs, out_refs, sems):
        *_, local, first = self._parts(x_refs, out_refs, sems)
        for cp in local + first:
            cp.start()

    def finish(self, x_refs, out_refs, sems):
        me, sibling, chips, c, copy, local, first = self._parts(x_refs, out_refs, sems)
        passed = []
        for j, chip in enumerate(chips):
            for a in range(self.n):
                copy(a, 1 + j, (*chip, c), me).wait_recv()
                fwd = copy(a, 4 + j, (*chip, c), sibling)
                fwd.start()
                passed.append(fwd)
        for a in range(self.n):
            copy(a, 0, sibling, me).wait_recv()
            for j, chip in enumerate(chips):
                copy(a, 4 + j, (*chip, 1 - c), me).wait_recv()
        for cp in first + passed:
            cp.wait_send()
        for cp in local:
            cp.wait()


class _ExchangePlan:
    def __init__(self, xs):
        self.xs = list(xs)
        self.n = len(self.xs)
        self.out_shape = [_sds(v.shape, v.dtype) for v in self.xs]
        self.scratch = _sem_scratch(self.n)

    def _parts(self, in_refs, out_refs, sems):
        send_sems, recv_sems, local_sems = sems
        x, y, c = _mesh_place()
        mine = _block_of(x, y, c)
        peers = [(1 - x if k & 4 else x, 1 - y if k & 2 else y, 1 - c if k & 1 else c) for k in range(1, N_DEV)]

        def copy(a, k, slot_src, slot_dst):
            return pltpu.make_async_remote_copy(
                src_ref=in_refs[a].at[slot_src], dst_ref=out_refs[a].at[slot_dst], send_sem=send_sems.at[a, k],
                recv_sem=recv_sems.at[a, k], device_id=peers[k], device_id_type=MESH)

        local = [pltpu.make_async_copy(in_refs[a].at[mine], out_refs[a].at[mine], local_sems.at[a])
                 for a in range(self.n)]
        sends = [copy(a, k, _block_of(*peers[k]), mine) for a in range(self.n) for k in range(N_DEV - 1)]
        arrivals = [copy(a, k, _block_of(*peers[k]), _block_of(*peers[k])) for a in range(self.n) for k in range(N_DEV - 1)]
        return local, sends, arrivals

    def start(self, in_refs, out_refs, sems):
        local, sends, _ = self._parts(in_refs, out_refs, sems)
        for cp in local + sends:
            cp.start()

    def finish(self, in_refs, out_refs, sems):
        local, sends, arrivals = self._parts(in_refs, out_refs, sems)
        for cp in arrivals:
            cp.wait_recv()
        for cp in sends:
            cp.wait_send()
        for cp in local:
            cp.wait()


def _call(body, args, *, grid, in_specs, out_specs, out_shape, scratch_shapes=(), semantics, name, comm=None):
    if comm is None:
        return pl.pallas_call(
            body, grid=grid, in_specs=list(in_specs), out_specs=list(out_specs), out_shape=list(out_shape),
            scratch_shapes=list(scratch_shapes), compiler_params=_params(*semantics), name=name)(*args)
    n_in, n_out, n_scr, n = len(in_specs), len(out_specs), len(scratch_shapes), comm.n

    def hosted(*refs):
        ins, rest = refs[:n_in], refs[n_in:]
        cin, rest = rest[:n], rest[n:]
        outs, rest = rest[:n_out], rest[n_out:]
        cout, rest = rest[:n], rest[n:]
        scr, sems = rest[:n_scr], rest[n_scr:]
        pids = [pl.program_id(d) for d in range(len(grid))]
        first = functools.reduce(jnp.logical_and, [p == 0 for p in pids])
        last = functools.reduce(jnp.logical_and, [p == g - 1 for p, g in zip(pids, grid)])

        @pl.when(first)
        def _():
            comm.start(cin, cout, sems)

        body(*ins, *outs, *scr)

        @pl.when(last)
        def _():
            comm.finish(cin, cout, sems)

    res = pl.pallas_call(
        hosted, grid=grid, in_specs=list(in_specs) + [HBM_SPEC] * n, out_specs=list(out_specs) + [HBM_SPEC] * n,
        out_shape=list(out_shape) + comm.out_shape, scratch_shapes=list(scratch_shapes) + comm.scratch,
        compiler_params=_params(*["arbitrary"] * len(grid)), name=name)(*args, *comm.xs)
    return res[:n_out], res[n_out:]


def _comm_alone(comm, name):
    def body(*refs):
        n = comm.n
        comm.start(refs[:n], refs[n:2 * n], refs[2 * n:])
        comm.finish(refs[:n], refs[n:2 * n], refs[2 * n:])

    return pl.pallas_call(
        body, in_specs=[HBM_SPEC] * comm.n, out_specs=[HBM_SPEC] * comm.n, out_shape=comm.out_shape,
        scratch_shapes=comm.scratch, name=name)(*comm.xs)


def _matmul(a, b, mode, out_dtype, tm, tn, tk, name, comm=None):
    if mode == "nn":
        (m, k), (_, n) = a.shape, b.shape
    elif mode == "nt":
        (m, k), (n, _) = a.shape, b.shape
    else:
        (k, m), (_, n) = a.shape, b.shape
    tm, tn, tk = min(tm, m), min(tn, n), min(tk, k)
    assert m % tm == 0 and n % tn == 0 and k % tk == 0, (name, m, n, k, tm, tn, tk)
    nk = k // tk
    dims = _DIMS[mode]

    def body(a_ref, b_ref, o_ref, acc_ref):
        if nk == 1:
            o_ref[...] = _dot(a_ref[...], b_ref[...], dims).astype(out_dtype)
            return
        kk = pl.program_id(2)

        @pl.when(kk == 0)
        def _():
            acc_ref[...] = _dot(a_ref[...], b_ref[...], dims)

        @pl.when((kk > 0) & (kk < nk - 1))
        def _():
            acc_ref[...] += _dot(a_ref[...], b_ref[...], dims)

        @pl.when(kk == nk - 1)
        def _():
            o_ref[...] = (acc_ref[...] + _dot(a_ref[...], b_ref[...], dims)).astype(out_dtype)

    if mode == "tn":
        a_spec = pl.BlockSpec((tk, tm), lambda i, j, kk: (kk, i))
        b_spec = pl.BlockSpec((tk, tn), lambda i, j, kk: (kk, j))
    elif mode == "nn":
        a_spec = pl.BlockSpec((tm, tk), lambda i, j, kk: (i, kk))
        b_spec = pl.BlockSpec((tk, tn), lambda i, j, kk: (kk, j))
    else:
        a_spec = pl.BlockSpec((tm, tk), lambda i, j, kk: (i, kk))
        b_spec = pl.BlockSpec((tn, tk), lambda i, j, kk: (j, kk))
    res = _call(
        body, (a, b), grid=(m // tm, n // tn, nk), in_specs=[a_spec, b_spec],
        out_specs=[pl.BlockSpec((tm, tn), lambda i, j, kk: (i, j))],
        out_shape=[_sds((m, n), out_dtype)], scratch_shapes=[pltpu.VMEM((tm, tn), F32)],
        semantics=("parallel", "parallel", "arbitrary"), name=name, comm=comm)
    return res[0] if comm is None else (res[0][0], res[1])


ROW_T = 512


def _prenorm(x, pv, name):
    s, d = x.shape
    t = min(ROW_T, s)

    def body(x_ref, pv_ref, h_ref):
        h = _norm_mod(x_ref[...], pv_ref[6:7, :], pv_ref[1:2, :], pv_ref[0:1, :])
        h_ref[...] = h.astype(BF16)

    return pl.pallas_call(
        body, grid=(s // t,),
        in_specs=[pl.BlockSpec((t, d), lambda i: (i, 0)), pl.BlockSpec((16, d), lambda i: (0, 0))],
        out_specs=pl.BlockSpec((t, d), lambda i: (i, 0)), out_shape=_sds((s, d), BF16),
        compiler_params=_params("parallel"), name=name)(x, pv)


DH_TK = 1920


def _dh_prenorm_bwd(dproj, win_t, dres, x, pv, name, comm=None):
    s, k = dproj.shape
    d = x.shape[1]
    t = min(ROW_T, s)
    tk = min(DH_TK, k)
    nk = k // tk

    def body(a_ref, b_ref, dres_ref, x_ref, pv_ref, dx_ref, sg_ref, acc_ref):
        i = pl.program_id(0)
        kk = pl.program_id(1)

        @pl.when((i == 0) & (kk == 0))
        def _():
            sg_ref[...] = jnp.zeros_like(sg_ref)

        @pl.when(kk == 0)
        def _():
            acc_ref[...] = _dot(a_ref[...], b_ref[...], NN)

        @pl.when((kk > 0) & (kk < nk - 1))
        def _():
            acc_ref[...] += _dot(a_ref[...], b_ref[...], NN)

        @pl.when(kk == nk - 1)
        def _():
            dh = acc_ref[...] + _dot(a_ref[...], b_ref[...], NN)
            _, vjp = jax.vjp(_norm_mod, x_ref[...], pv_ref[6:7, :], pv_ref[1:2, :], pv_ref[0:1, :])
            dx, dg, dsc, dsh = vjp(dh)
            dx_ref[...] = dres_ref[...] + dx
            sg_ref[0:1, :] += dsh
            sg_ref[1:2, :] += dsc
            sg_ref[2:3, :] += dg

    assert nk >= 2 and k % tk == 0, (k, tk)
    row = pl.BlockSpec((t, d), lambda i, kk: (i, 0))
    return _call(
        body, (dproj, win_t, dres, x, pv), grid=(s // t, nk),
        in_specs=[pl.BlockSpec((t, tk), lambda i, kk: (i, kk)), pl.BlockSpec((tk, d), lambda i, kk: (kk, 0)),
                  row, row, pl.BlockSpec((16, d), lambda i, kk: (0, 0))],
        out_specs=[row, pl.BlockSpec((8, d), lambda i, kk: (0, 0))],
        out_shape=[_sds((s, d), F32), _sds((8, d), F32)], scratch_shapes=[pltpu.VMEM((t, d), F32)],
        semantics=("arbitrary", "arbitrary"), name=name, comm=comm)


CONV_T = 512


def _conv_tile(a_ext, g_ext, w, b, ln_g, ln_b, n_out):
    u0 = a_ext * jax.nn.sigmoid(g_ext)
    off = CONV_HALO - (CONV_WIDTH - 1)
    acc = jnp.zeros((n_out, u0.shape[1]), F32) + b
    for r in range(8):
        taps = [k for k in range(CONV_WIDTH) if (off + k) % 8 == r]
        rows = n_out if r == 0 else n_out + 8
        part = None
        for k in taps:
            lo = (off + k) // 8 * 8
            term = w[k:k + 1, :] * u0[lo: lo + rows, :]
            part = term if part is None else part + term
        acc = acc + part[r: r + n_out, :]
    mu = jnp.mean(acc, axis=-1, keepdims=True)
    var = jnp.mean(jnp.square(acc - mu), axis=-1, keepdims=True)
    y = (acc - mu) * lax.rsqrt(var + EPS) * ln_g + ln_b
    return y * jax.nn.sigmoid(y)


def _conv_fwd(proj, conv_w, cp, name):
    s = proj.shape[0]
    t = min(CONV_T, s)
    c, h = CONV_CH, CONV_HALO

    def body(ap_ref, ac_ref, gp_ref, gc_ref, w_ref, cp_ref, o_ref):
        i = pl.program_id(0)
        live = (i > 0).astype(F32)
        a_ext = jnp.concatenate([ap_ref[t - h:, :] * live, ac_ref[...]], axis=0)
        g_ext = jnp.concatenate([gp_ref[t - h:, :], gc_ref[...]], axis=0)
        u = _conv_tile(a_ext, g_ext, w_ref[...], cp_ref[0:1, :], cp_ref[1:2, :], cp_ref[2:3, :], t)
        o_ref[...] = u.astype(BF16)

    prev = lambda col: pl.BlockSpec((t, c), lambda i: (jnp.maximum(i - 1, 0), col))
    cur = lambda col: pl.BlockSpec((t, c), lambda i: (i, col))
    return pl.pallas_call(
        body, grid=(s // t,),
        in_specs=[prev(0), cur(0), prev(1), cur(1),
                  pl.BlockSpec((CONV_WIDTH, c), lambda i: (0, 0)), pl.BlockSpec((8, c), lambda i: (0, 0))],
        out_specs=pl.BlockSpec((t, c), lambda i: (i, 0)), out_shape=_sds((s, c), BF16),
        compiler_params=_params("parallel"), name=name)(proj, proj, proj, proj, conv_w, cp)


def _conv_bwd(proj, do, conv_w, cp, name, comm=None):
    s = proj.shape[0]
    t = min(CONV_T, s)
    c, h = CONV_CH, CONV_HALO
    nt = s // t

    def body(ap_ref, ac_ref, an_ref, gp_ref, gc_ref, gn_ref, doc_ref, don_ref, w_ref, cp_ref,
             da_ref, dg_ref, dw_ref, sg_ref):
        i = pl.program_id(0)

        @pl.when(i == 0)
        def _():
            dw_ref[...] = jnp.zeros_like(dw_ref)
            sg_ref[...] = jnp.zeros_like(sg_ref)

        first = (i > 0).astype(F32)
        last = (i < nt - 1).astype(F32)
        a_ext = jnp.concatenate([ap_ref[t - h:, :] * first, ac_ref[...], an_ref[:h, :] * last], axis=0)
        g_ext = jnp.concatenate([gp_ref[t - h:, :], gc_ref[...], gn_ref[:h, :]], axis=0)
        fn = functools.partial(_conv_tile, n_out=t + h)
        _, vjp = jax.vjp(fn, a_ext, g_ext, w_ref[...], cp_ref[0:1, :], cp_ref[1:2, :], cp_ref[2:3, :])
        ct_own = jnp.concatenate([doc_ref[...], jnp.zeros((h, c), F32)], axis=0)
        ct_all = jnp.concatenate([doc_ref[...], don_ref[:h, :] * last], axis=0)
        _, _, dw, db, dlg, dlb = vjp(ct_own)
        da, dg, _, _, _, _ = vjp(ct_all)
        da_ref[...] = da[h:h + t, :].astype(BF16)
        dg_ref[...] = dg[h:h + t, :].astype(BF16)
        dw_ref[...] += dw
        sg_ref[0:1, :] += db
        sg_ref[1:2, :] += dlg
        sg_ref[2:3, :] += dlb

    prev = lambda col: pl.BlockSpec((t, c), lambda i: (jnp.maximum(i - 1, 0), col))
    cur = lambda col: pl.BlockSpec((t, c), lambda i: (i, col))
    nxt = lambda col: pl.BlockSpec((t, c), lambda i: (jnp.minimum(i + 1, nt - 1), col))
    return _call(
        body, (proj, proj, proj, proj, proj, proj, do, do, conv_w, cp), grid=(nt,),
        in_specs=[prev(0), cur(0), nxt(0), prev(1), cur(1), nxt(1), cur(0), nxt(0),
                  pl.BlockSpec((CONV_WIDTH, c), lambda i: (0, 0)), pl.BlockSpec((8, c), lambda i: (0, 0))],
        out_specs=[cur(0), cur(0), pl.BlockSpec((CONV_WIDTH, c), lambda i: (0, 0)),
                   pl.BlockSpec((8, c), lambda i: (0, 0))],
        out_shape=[_sds((s, c), BF16), _sds((s, c), BF16), _sds((CONV_WIDTH, c), F32), _sds((8, c), F32)],
        semantics=("arbitrary",), name=name, comm=comm)


def _hgrn_levels(c):
    out, m = [], c // 2
    while m >= 1:
        out.append(m)
        m //= 2
    return out


def _hgrn_consts(c):
    t = np.arange(c)[:, None]
    j = np.arange(c)[None, :]
    mats = [j <= t, j > t]
    for m in _hgrn_levels(c):
        same = (t // m) == (j // m)
        mats += [same & (j <= t), same & (j > t)]
    return jnp.asarray(np.concatenate(mats, axis=0).astype(np.float32), dtype=BF16)


@jax.custom_vjp
def _cums(lc, mall):
    c = lc.shape[0]
    full = _xdot_left(mall, lc)
    return tuple(full[i * c:(i + 1) * c, :] for i in range(mall.shape[0] // c))


def _cums_fwd(lc, mall):
    return _cums(lc, mall), mall


def _cums_bwd(mall, cts):
    return _xdot_left(mall, jnp.concatenate(cts, axis=0), TN), jnp.zeros_like(mall)


_cums.defvjp(_cums_fwd, _cums_bwd)


def _hgrn_chunk(q, f, v, g, lbs, ng, sts_in, mall):
    c = q.shape[0]
    keep = jax.nn.sigmoid(-f)
    if lbs:
        keep = (1.0 - jax.nn.sigmoid(lbs[1] - lbs[0])) * keep
    lc = jnp.log1p(-keep)
    qs = q * jax.nn.sigmoid(q)
    cs = _cums(lc, mall)
    q_in = qs * jnp.exp(cs[0])
    k_out = keep * jnp.exp(cs[1])
    decay = jnp.exp(jnp.sum(lc, axis=0, keepdims=True))
    qk = qs * keep
    r = lax.broadcasted_iota(jnp.int32, q.shape, 0)
    tt = lax.broadcasted_iota(jnp.int32, (c, c), 0)
    ss = lax.broadcasted_iota(jnp.int32, (c, c), 1)
    levels = []
    for li, m in enumerate(_hgrn_levels(c)):
        lg = m.bit_length() - 1
        odd = ((r >> lg) & 1) == 1
        qm = jnp.where(odd, qs * jnp.exp(cs[2 + 2 * li]), 0.0)
        km = jnp.where(odd, 0.0, keep * jnp.exp(cs[3 + 2 * li]))
        pair = (((tt >> lg) & 1) == 1) & ((ss >> lg) == (tt >> lg) - 1)
        levels.append((qm, km, pair))
    outs, sts_out = [], []
    for h, st_in in enumerate(sts_in):
        hs = slice(h * HG_DK, (h + 1) * HG_DK)
        vh = v[:, hs]
        sc = jnp.where(tt == ss, jnp.sum(qk[:, hs], axis=-1, keepdims=True), 0.0)
        for qm, km, pair in levels:
            sc = sc + jnp.where(pair, _bdot(qm[:, hs], km[:, hs], "nt"), 0.0)
        o = _bdot(q_in[:, hs], st_in, "nt") + _bdot(sc, vh, "nn")
        sts_out.append(st_in * decay[:, hs] + _bdot(vh, k_out[:, hs], "tn"))
        outs.append(o * lax.rsqrt(jnp.mean(o * o, axis=-1, keepdims=True) + EPS) * ng)
    return jnp.concatenate(outs, axis=1) * (g * jax.nn.sigmoid(g)), tuple(sts_out)


def _hgrn_fwd(proj, lb, ng, name, comm=None):
    s = proj.shape[0]
    c = HG_CHUNK
    nc = s // c
    mall = _hgrn_consts(c)
    col0 = 1024 // (HG_HEADS * HG_DK)

    def body(*refs):
        q_ref, f_ref, v_ref, g_ref = refs[:4]
        if lb is None:
            ng_ref, m_ref, y_ref, st_ref, scr = refs[4:]
        else:
            lb_ref, ng_ref, m_ref, y_ref, st_ref, scr = refs[4:]
        ci = pl.program_id(0)

        @pl.when(ci == 0)
        def _():
            scr[...] = jnp.zeros_like(scr)

        lbs = () if lb is None else (lb_ref[0:1, :], lb_ref[1:2, :])
        sts_in = tuple(scr[h] for h in range(HG_HEADS))
        for h in range(HG_HEADS):
            st_ref[h] = sts_in[h]
        y, sts_out = _hgrn_chunk(q_ref[...], f_ref[...], v_ref[...], g_ref[...], lbs, ng_ref[...], sts_in, m_ref[...])
        y_ref[...] = y.astype(BF16)
        for h in range(HG_HEADS):
            scr[h] = sts_out[h]

    w = HG_HEADS * HG_DK
    col = lambda k: pl.BlockSpec((c, w), lambda ci: (ci, col0 + k))
    in_specs = [col(0), col(1), col(2), col(3)]
    args = [proj, proj, proj, proj]
    if lb is not None:
        in_specs.append(pl.BlockSpec((2, w), lambda ci: (0, 0)))
        args.append(lb)
    in_specs += [pl.BlockSpec((1, HG_DK), lambda ci: (0, 0)), pl.BlockSpec(mall.shape, lambda ci: (0, 0))]
    args += [ng, mall]
    return _call(
        body, args, grid=(nc,), in_specs=in_specs,
        out_specs=[pl.BlockSpec((c, w), lambda ci: (ci, 0)),
                   pl.BlockSpec((HG_HEADS, None, HG_DK, HG_DK), lambda ci: (0, ci, 0, 0))],
        out_shape=[_sds((s, w), BF16), _sds((HG_HEADS, nc, HG_DK, HG_DK), F32)],
        scratch_shapes=[pltpu.VMEM((HG_HEADS, HG_DK, HG_DK), F32)],
        semantics=("arbitrary",), name=name, comm=comm)


def _hgrn_bwd(proj, states, dy, lb, ng, name, comm=None):
    s = proj.shape[0]
    c = HG_CHUNK
    nc = s // c
    mall = _hgrn_consts(c)
    col0 = 1024 // (HG_HEADS * HG_DK)

    def body(*refs):
        q_ref, f_ref, v_ref, g_ref, st_ref, dy_ref = refs[:6]
        if lb is None:
            ng_ref, m_ref, dq_ref, df_ref, dv_ref, dg_ref, dlb_ref, dng_ref, scr = refs[6:]
        else:
            lb_ref, ng_ref, m_ref, dq_ref, df_ref, dv_ref, dg_ref, dlb_ref, dng_ref, scr = refs[6:]
        ci = pl.program_id(0)

        @pl.when(ci == 0)
        def _():
            scr[...] = jnp.zeros_like(scr)
            dlb_ref[...] = jnp.zeros_like(dlb_ref)
            dng_ref[...] = jnp.zeros_like(dng_ref)

        mall_v = m_ref[...]
        fn = lambda q, f, v, g, lbs_, ng_, sts: _hgrn_chunk(q, f, v, g, lbs_, ng_, sts, mall_v)
        lbs = () if lb is None else (lb_ref[0:1, :], lb_ref[1:2, :])
        sts_in = tuple(st_ref[h] for h in range(HG_HEADS))
        _, vjp = jax.vjp(fn, q_ref[...], f_ref[...], v_ref[...], g_ref[...], lbs, ng_ref[...], sts_in)
        dq, df, dv, dg, dlbs, dng, dsts = vjp((dy_ref[...], tuple(scr[h] for h in range(HG_HEADS))))
        dq_ref[...] = dq.astype(BF16)
        df_ref[...] = df.astype(BF16)
        dv_ref[...] = dv.astype(BF16)
        dg_ref[...] = dg.astype(BF16)
        for h in range(HG_HEADS):
            scr[h] = dsts[h]
        dng_ref[0:1, :] += dng
        if lbs:
            dlb_ref[0:1, :] += dlbs[0]
            dlb_ref[1:2, :] += dlbs[1]

    w = HG_HEADS * HG_DK
    rev = lambda ci: nc - 1 - ci
    col = lambda k: pl.BlockSpec((c, w), lambda ci: (rev(ci), col0 + k))
    out_col = pl.BlockSpec((c, w), lambda ci: (rev(ci), 0))
    in_specs = [col(0), col(1), col(2), col(3),
                pl.BlockSpec((HG_HEADS, None, HG_DK, HG_DK), lambda ci: (0, rev(ci), 0, 0)), out_col]
    args = [proj, proj, proj, proj, states, dy]
    if lb is not None:
        in_specs.append(pl.BlockSpec((2, w), lambda ci: (0, 0)))
        args.append(lb)
    in_specs += [pl.BlockSpec((1, HG_DK), lambda ci: (0, 0)), pl.BlockSpec(mall.shape, lambda ci: (0, 0))]
    args += [ng, mall]
    return _call(
        body, args, grid=(nc,), in_specs=in_specs,
        out_specs=[out_col, out_col, out_col, out_col,
                   pl.BlockSpec((2, w), lambda ci: (0, 0)), pl.BlockSpec((8, HG_DK), lambda ci: (0, 0))],
        out_shape=[_sds((s, w), BF16)] * 4 + [_sds((2, w), F32), _sds((8, HG_DK), F32)],
        scratch_shapes=[pltpu.VMEM((HG_HEADS, HG_DK, HG_DK), F32)],
        semantics=("arbitrary",), name=name, comm=comm)


def _head_avg():
    w = SB_HEADS * SB_DH
    i = np.arange(w)
    return jnp.asarray(((i[:, None] // SB_DH) == (i[None, :] // SB_DH)).astype(np.float32) / SB_DH, dtype=BF16)


def _sb_norm(x, g_tiled, avg):
    ms = _xr(x * x, avg)
    return x * lax.rsqrt(ms + EPS) * g_tiled


def _sb_prep(proj, gq, gk, name):
    s = proj.shape[0]
    t = min(ROW_T, s)
    w = SB_HEADS * SB_DH
    avg = _head_avg()

    def body(q_ref, k_ref, v_ref, gq_ref, gk_ref, avg_ref, qn_ref, kn_ref, vb_ref):
        qn_ref[...] = _sb_norm(q_ref[...], gq_ref[...], avg_ref[...]).astype(BF16)
        kn_ref[...] = _sb_norm(k_ref[...], gk_ref[...], avg_ref[...]).astype(BF16)
        vb_ref[...] = v_ref[...].astype(BF16)

    col = lambda k: pl.BlockSpec((t, w), lambda i: (i, 6 + k))
    vec = pl.BlockSpec((1, w), lambda i: (0, 0))
    out = pl.BlockSpec((t, w), lambda i: (i, 0))
    return pl.pallas_call(
        body, grid=(s // t,), in_specs=[col(0), col(1), col(2), vec, vec, pl.BlockSpec((w, w), lambda i: (0, 0))],
        out_specs=[out, out, out], out_shape=[_sds((s, w), BF16)] * 3,
        compiler_params=_params("parallel"), name=name)(proj, proj, proj, gq, gk, avg)


def _sb_prep_bwd(proj, dqn, dkn, gq, gk, name):
    s = proj.shape[0]
    t = min(ROW_T, s)
    w = SB_HEADS * SB_DH
    avg = _head_avg()

    def body(q_ref, k_ref, dqn_ref, dkn_ref, gq_ref, gk_ref, avg_ref, dq_ref, dk_ref, sg_ref):
        i = pl.program_id(0)

        @pl.when(i == 0)
        def _():
            sg_ref[...] = jnp.zeros_like(sg_ref)

        avg_v = avg_ref[...]
        fn = lambda x, g: _sb_norm(x, g, avg_v)
        _, vq = jax.vjp(fn, q_ref[...], gq_ref[...])
        dq, dgq = vq(dqn_ref[...])
        _, vk = jax.vjp(fn, k_ref[...], gk_ref[...])
        dk, dgk = vk(dkn_ref[...])
        dq_ref[...] = dq.astype(BF16)
        dk_ref[...] = dk.astype(BF16)
        sg_ref[0:1, :] += dgq
        sg_ref[1:2, :] += dgk

    col = lambda k: pl.BlockSpec((t, w), lambda i: (i, 6 + k))
    vec = pl.BlockSpec((1, w), lambda i: (0, 0))
    row = pl.BlockSpec((t, w), lambda i: (i, 0))
    return pl.pallas_call(
        body, grid=(s // t,),
        in_specs=[col(0), col(1), row, row, vec, vec, pl.BlockSpec((w, w), lambda i: (0, 0))],
        out_specs=[row, row, pl.BlockSpec((8, w), lambda i: (0, 0))],
        out_shape=[_sds((s, w), BF16), _sds((s, w), BF16), _sds((8, w), F32)],
        compiler_params=_params("arbitrary"), name=name)(proj, proj, dqn, dkn, gq, gk, avg)


def _sb_tri(kind):
    j = np.arange(SB_BLK)[:, None]
    s = np.arange(SB_BLK)[None, :]
    tri = (j > s) if kind == "suffix" else (j < s)
    return jnp.asarray(np.concatenate([tri, np.ones_like(tri)], axis=1).astype(np.float32), dtype=BF16)


def _sb_scores(qm, kblk, mask):
    z = _dot(qm, kblk, NT) * (SB_DH ** -0.5)
    sp = jnp.maximum(z, 0.0) + jnp.log(1.0 + jnp.exp(-jnp.abs(z)))
    return z, sp, jnp.where(mask, -sp, 0.0)


def _sb_setup(b):
    lane = lax.broadcasted_iota(jnp.int32, (2 * b, b), 1)
    row = lax.broadcasted_iota(jnp.int32, (2 * b, b), 0)
    mine = (row >> (b.bit_length() - 1)) == (lane >> (SB_DH.bit_length() - 1))
    return lane, row & (b - 1), mine


def _sb_fwd(qn, kn, vb, name, comm=None):
    s, w = qn.shape
    b = SB_BLK
    nq = s // b
    tri = _sb_tri("suffix")

    def body(q_ref, k_ref, v_ref, tri_ref, o_ref):
        i = pl.program_id(1)
        lane, tt, mine = _sb_setup(b)
        tri_v = tri_ref[...]
        pairs = [slice(u * b, (u + 1) * b) for u in range(SB_PAIRS)]
        qms = []
        for ls in pairs:
            q = q_ref[:, ls]
            q2 = jnp.concatenate([q, q], axis=0)
            qms.append(jnp.where(mine, q2, jnp.zeros_like(q2)))

        offs = [pl.multiple_of(jnp.maximum(i - j, 0) * b, b) for j in range(SB_FIXED)]
        masks = [lane < (tt if j == 0 else jnp.where(i >= j, b, 0)) for j in range(SB_FIXED)]
        scores = [[_sb_scores(qm, k_ref[pl.ds(off, b), ls], m) for off, m in zip(offs, masks)]
                  for qm, ls in zip(qms, pairs)]
        boths = [[_xdot_right(lk, tri_v) for _, _, lk in per_pair] for per_pair in scores]
        for u, ls in enumerate(pairs):
            qm = qms[u]
            run = acc = jnp.zeros((2 * b, b), F32)
            for j in range(SB_FIXED):
                z, sp, _ = scores[u][j]
                a = jnp.where(masks[j], jnp.exp(z - sp + boths[u][j][:, :b] + run), 0.0)
                acc = acc + _dot(a.astype(BF16), v_ref[pl.ds(offs[j], b), ls], NN)
                run = run + boths[u][j][:, b:]

            def cond(carry):
                j, run_, _ = carry
                return (j <= i) & (jnp.max(run_) > SB_DEAD)

            def step(carry, qm=qm, ls=ls):
                j, run_, acc_ = carry
                off = pl.multiple_of((i - j) * b, b)
                z, sp, lk = _sb_scores(qm, k_ref[pl.ds(off, b), ls], lane < b)
                both = _xdot_right(lk, tri_v)
                a = jnp.exp(z - sp + both[:, :b] + run_)
                return j + 1, run_ + both[:, b:], acc_ + _dot(a.astype(BF16), v_ref[pl.ds(off, b), ls], NN)

            _, _, acc = lax.while_loop(cond, step, (jnp.int32(SB_FIXED), run, acc))
            o_ref[:, ls] = jnp.where(lane[:b] < SB_DH, acc[:b], acc[b:]).astype(BF16)

    wide = SB_PAIRS * b
    blk = pl.BlockSpec((b, wide), lambda p, i: (i, p))
    full = pl.BlockSpec((s, wide), lambda p, i: (0, p))
    return _call(
        body, (qn, kn, vb, tri), grid=(w // wide, nq),
        in_specs=[blk, full, full, pl.BlockSpec(tri.shape, lambda p, i: (0, 0))],
        out_specs=[blk], out_shape=[_sds((s, w), BF16)],
        semantics=("parallel", "arbitrary"), name=name, comm=comm)


def _sb_bwd(qn, kn, vb, do, name, comm=None):
    s, w = qn.shape
    b = SB_BLK
    nq = s // b
    tri_s = _sb_tri("suffix")
    tri_p = _sb_tri("prefix")
    scale = SB_DH ** -0.5

    def body(q_ref, k_ref, v_ref, do_ref, ts_ref, tp_ref, dq_ref, dk_ref, dv_ref, dk_acc, dv_acc, dp_scr):
        i = pl.program_id(1)

        @pl.when(i == 0)
        def _():
            dk_acc[...] = jnp.zeros_like(dk_acc)
            dv_acc[...] = jnp.zeros_like(dv_acc)

        lane, tt, mine = _sb_setup(b)
        ts_v = ts_ref[...]
        tp_v = tp_ref[...]
        zero = jnp.zeros((2 * b, b), F32)
        pairs = [slice(u * b, (u + 1) * b) for u in range(SB_PAIRS)]
        qms, doms = [], []
        for ls in pairs:
            q = q_ref[:, ls]
            q2 = jnp.concatenate([q, q], axis=0)
            qms.append(jnp.where(mine, q2, jnp.zeros_like(q2)))
            dout = do_ref[:, ls].astype(BF16)
            d2 = jnp.concatenate([dout, dout], axis=0)
            doms.append(jnp.where(mine, d2, jnp.zeros_like(d2)))

        def down(u, kb, run):
            ls, qm, dom = pairs[u], qms[u], doms[u]
            off = pl.multiple_of(kb * b, b)
            z, sp, lk = _sb_scores(qm, k_ref[pl.ds(off, b), ls], lane < b)
            both = _xdot_right(lk, ts_v)
            a = jnp.exp(z - sp + both[:, :b] + run)
            dv_acc[pl.ds(off, b), ls] += _dot(a.astype(BF16), dom, TN)
            return _dot(dom, v_ref[pl.ds(off, b), ls], NT) * a, run + both[:, b:]

        def up(u, kb, dp, pre, dq):
            ls, qm = pairs[u], qms[u]
            off = pl.multiple_of(kb * b, b)
            kblk = k_ref[pl.ds(off, b), ls]
            sig = jax.nn.sigmoid(_dot(qm, kblk, NT) * scale)
            both = _xdot_right(dp, tp_v)
            dz = ((dp * (1.0 - sig) - sig * (both[:, :b] + pre)) * scale).astype(BF16)
            dk_acc[pl.ds(off, b), ls] += _dot(dz, qm, TN)
            return pre + both[:, b:], dq + _dot(dz, kblk, NN)

        offs = [pl.multiple_of(jnp.maximum(i - j, 0) * b, b) for j in range(SB_FIXED)]
        masks = [lane < (tt if j == 0 else jnp.where(i >= j, b, 0)) for j in range(SB_FIXED)]
        kblks = [[k_ref[pl.ds(off, b), ls] for off in offs] for ls in pairs]
        scores = [[_sb_scores(qms[u], kblks[u][j], masks[j]) for j in range(SB_FIXED)] for u in range(SB_PAIRS)]
        das = [[_dot(doms[u], v_ref[pl.ds(off, b), pairs[u]], NT) for off in offs] for u in range(SB_PAIRS)]
        boths = [[_xdot_right(lk, ts_v) for _, _, lk in scores[u]] for u in range(SB_PAIRS)]
        runs, dps = [], []
        for u in range(SB_PAIRS):
            run = zero
            mine_dps = []
            for j in range(SB_FIXED):
                z, sp, _ = scores[u][j]
                a = jnp.where(masks[j], jnp.exp(z - sp + boths[u][j][:, :b] + run), 0.0)
                mine_dps.append(das[u][j] * a)
                dv_acc[pl.ds(offs[j], b), pairs[u]] += _dot(a.astype(BF16), doms[u], TN)
                run = run + boths[u][j][:, b:]
            runs.append(run)
            dps.append(mine_dps)

        carries = []
        for u in range(SB_PAIRS):

            def cond(carry):
                j, run_ = carry
                return (j <= i) & (jnp.max(run_) > SB_DEAD)

            def sweep_down(carry, u=u):
                j, run_ = carry
                dp, run_ = down(u, i - j, run_)
                dp_scr[i - j] = dp
                return j + 1, run_

            n_live, _ = lax.while_loop(cond, sweep_down, (jnp.int32(SB_FIXED), runs[u]))

            def sweep_up(jj, carry, u=u, n_live=n_live):
                kb = i - n_live + 1 + jj
                return up(u, kb, dp_scr[kb], *carry)

            carries.append(lax.fori_loop(0, n_live - SB_FIXED, sweep_up, (zero, zero)))

        pres = [[_xdot_right(dp, tp_v) for dp in dps[u]] for u in range(SB_PAIRS)]
        for u, ls in enumerate(pairs):
            pre, dq = carries[u]
            for j in reversed(range(SB_FIXED)):
                z, sp, _ = scores[u][j]
                sig = jnp.exp(z - sp)
                dz = jnp.where(masks[j], dps[u][j] * (1.0 - sig) - sig * (pres[u][j][:, :b] + pre), 0.0) * scale
                dz = dz.astype(BF16)
                dk_acc[pl.ds(offs[j], b), ls] += _dot(dz, qms[u], TN)
                dq = dq + _dot(dz, kblks[u][j], NN)
                pre = pre + pres[u][j][:, b:]
            dq_ref[:, ls] = jnp.where(lane[:b] < SB_DH, dq[:b], dq[b:])

        @pl.when(i == nq - 1)
        def _():
            dk_ref[...] = dk_acc[...]
            dv_ref[...] = dv_acc[...].astype(BF16)

    wide = SB_PAIRS * b
    blk = pl.BlockSpec((b, wide), lambda p, i: (i, p))
    full = pl.BlockSpec((s, wide), lambda p, i: (0, p))
    tri = pl.BlockSpec(tri_s.shape, lambda p, i: (0, 0))
    return _call(
        body, (qn, kn, vb, do, tri_s, tri_p), grid=(w // wide, nq), in_specs=[blk, full, full, blk, tri, tri],
        out_specs=[blk, full, full], out_shape=[_sds((s, w), F32), _sds((s, w), F32), _sds((s, w), BF16)],
        scratch_shapes=[pltpu.VMEM((s, wide), F32), pltpu.VMEM((s, wide), F32), pltpu.VMEM((nq, 2 * b, b), F32)],
        semantics=("arbitrary", "arbitrary"), name=name, comm=comm)


MIX_T = 256
HALF = 512


def _gate_slices(ga, gb):
    return [(ga[:, 0:512], ga[:, 512:1024]), (ga[:, 1024:1536], gb[:, 0:512]), (gb[:, 512:1024], gb[:, 1024:1536])]


def _mix_fwd(u3, oh, osb, proj, x, pv, wc, wh, ws, wo, name):
    s, d = x.shape
    t = min(MIX_T, s)

    def body(u3_ref, oh_ref, os_ref, ga_ref, gb_ref, x_ref, pv_ref, wc_ref, wh_ref, ws_ref, wo_ref,
             x1_ref, h2_ref, mg_ref, mo_ref):
        ys = [_dot(u3_ref[...], wc_ref[...], NT), _dot(oh_ref[...], wh_ref[...], NT), _dot(os_ref[...], ws_ref[...], NT)]
        gl = _gate_slices(ga_ref[...], gb_ref[...])
        halves = []
        for hf in range(2):
            lo = hf * HALF
            acc = jnp.zeros((t, HALF), F32)
            for br in range(3):
                gate = jax.nn.sigmoid(gl[br][hf] + pv_ref[8 + br:9 + br, lo:lo + HALF])
                acc = acc + gate * ys[br][:, lo:lo + HALF]
            halves.append(acc)
        merged = jnp.concatenate(halves, axis=1).astype(BF16)
        mg_ref[...] = merged
        mo = _dot(merged, wo_ref[...], NN)
        mo_ref[...] = mo.astype(BF16)
        x1 = x_ref[...] + pv_ref[2:3, :] * mo
        x1_ref[...] = x1
        h2_ref[...] = _norm_mod(x1, pv_ref[7:8, :], pv_ref[4:5, :], pv_ref[3:4, :]).astype(BF16)

    br_spec = pl.BlockSpec((t, CONV_CH), lambda i: (i, 0))
    row = pl.BlockSpec((t, d), lambda i: (i, 0))
    wproj = pl.BlockSpec((d, CONV_CH), lambda i: (0, 0))
    return pl.pallas_call(
        body, grid=(s // t,),
        in_specs=[br_spec, br_spec, br_spec, pl.BlockSpec((t, 1536), lambda i: (i, 3)),
                  pl.BlockSpec((t, 1536), lambda i: (i, 4)), row, pl.BlockSpec((16, d), lambda i: (0, 0)),
                  wproj, wproj, wproj, pl.BlockSpec((d, d), lambda i: (0, 0))],
        out_specs=[row, row, row, row],
        out_shape=[_sds((s, d), F32), _sds((s, d), BF16), _sds((s, d), BF16), _sds((s, d), BF16)],
        compiler_params=_params("parallel"), name=name)(u3, oh, osb, proj, proj, x, pv, wc, wh, ws, wo)


def _mix_bwd(dx1, mo1, u3, oh, osb, proj, pv, wc, wh, ws, wo, name):
    s, d = dx1.shape
    t = min(MIX_T, s)

    def body(dx_ref, mo_ref, u3_ref, oh_ref, os_ref, ga_ref, gb_ref, pv_ref, wc_ref, wh_ref, ws_ref, wo_ref,
             dmo_ref, dyc_ref, dyh_ref, dys_ref, doc_ref, doh_ref, dos_ref, dgl_ref, sg_ref):
        i = pl.program_id(0)

        @pl.when(i == 0)
        def _():
            sg_ref[...] = jnp.zeros_like(sg_ref)

        dx = dx_ref[...]
        dmo = (dx * pv_ref[2:3, :]).astype(BF16)
        dmo_ref[...] = dmo
        sg_ref[0:1, :] += jnp.sum(dx * mo_ref[...].astype(F32), axis=0, keepdims=True)
        dmerged = _dot(dmo, wo_ref[...], NT)
        branches = [(u3_ref, wc_ref, dyc_ref, doc_ref), (oh_ref, wh_ref, dyh_ref, doh_ref), (os_ref, ws_ref, dys_ref, dos_ref)]
        gl = _gate_slices(ga_ref[...], gb_ref[...])
        for br, (o_ref, w_ref, dy_ref, do_ref) in enumerate(branches):
            y = _dot(o_ref[...], w_ref[...], NT)
            dys = []
            for hf in range(2):
                lo = hf * HALF
                gate = jax.nn.sigmoid(gl[br][hf] + pv_ref[8 + br:9 + br, lo:lo + HALF])
                dm = dmerged[:, lo:lo + HALF]
                dys.append(dm * gate)
                dgl = dm * y[:, lo:lo + HALF] * gate * (1.0 - gate)
                dgl_ref[:, br * d + lo: br * d + lo + HALF] = dgl.astype(BF16)
                sg_ref[1 + br:2 + br, lo:lo + HALF] += jnp.sum(dgl, axis=0, keepdims=True)
            dy = jnp.concatenate(dys, axis=1).astype(BF16)
            dy_ref[...] = dy
            do_ref[...] = _dot(dy, w_ref[...], NN)

    br_spec = pl.BlockSpec((t, CONV_CH), lambda i: (i, 0))
    row = pl.BlockSpec((t, d), lambda i: (i, 0))
    wproj = pl.BlockSpec((d, CONV_CH), lambda i: (0, 0))
    return pl.pallas_call(
        body, grid=(s // t,),
        in_specs=[row, row, br_spec, br_spec, br_spec, pl.BlockSpec((t, 1536), lambda i: (i, 3)),
                  pl.BlockSpec((t, 1536), lambda i: (i, 4)), pl.BlockSpec((16, d), lambda i: (0, 0)),
                  wproj, wproj, wproj, pl.BlockSpec((d, d), lambda i: (0, 0))],
        out_specs=[row, row, row, row, br_spec, br_spec, br_spec, pl.BlockSpec((t, 3 * d), lambda i: (i, 0)),
                   pl.BlockSpec((8, d), lambda i: (0, 0))],
        out_shape=[_sds((s, d), BF16)] * 4 + [_sds((s, CONV_CH), F32)] * 3 + [_sds((s, 3 * d), BF16), _sds((8, d), F32)],
        compiler_params=_params("arbitrary"), name=name)(dx1, mo1, u3, oh, osb, proj, proj, pv, wc, wh, ws, wo)


MLP_T = 512
MLP_F = 1024


def _mlp_fwd(h2, x1, pv, w1t, w2, name, comm=None):
    s, d = x1.shape
    t = min(MLP_T, s)
    nf = D_FF // MLP_F

    def body(h_ref, x_ref, pv_ref, w1_ref, w2_ref, x2_ref, mo_ref, acc_ref):
        f = pl.program_id(1)

        @pl.when(f == 0)
        def _():
            acc_ref[...] = jnp.zeros_like(acc_ref)

        a = jnp.maximum(_dot(h_ref[...], w1_ref[...], NT), 0.0)
        acc_ref[...] += _dot((a * a).astype(BF16), w2_ref[...], NN)

        @pl.when(f == nf - 1)
        def _():
            mo = acc_ref[...]
            mo_ref[...] = mo.astype(BF16)
            x2_ref[...] = x_ref[...] + pv_ref[5:6, :] * mo

    row = pl.BlockSpec((t, d), lambda i, f: (i, 0))
    wblk = pl.BlockSpec((MLP_F, d), lambda i, f: (f, 0))
    return _call(
        body, (h2, x1, pv, w1t, w2), grid=(s // t, nf),
        in_specs=[row, row, pl.BlockSpec((16, d), lambda i, f: (0, 0)), wblk, wblk],
        out_specs=[row, row], out_shape=[_sds((s, d), F32), _sds((s, d), BF16)],
        scratch_shapes=[pltpu.VMEM((t, d), F32)],
        semantics=("parallel", "arbitrary"), name=name, comm=comm)


def _mlp_bwd(dx2, h2, x1, mo2, pv, w1t, w2, name, comm=None):
    s, d = x1.shape
    t = min(MLP_T, s)
    nf = D_FF // MLP_F

    def body(dx_ref, h_ref, x_ref, mo_ref, pv_ref, w1_ref, w2_ref, dx1_ref, da_ref, b_ref, dmo_ref, sg_ref, acc_ref):
        i = pl.program_id(0)
        f = pl.program_id(1)

        @pl.when((i == 0) & (f == 0))
        def _():
            sg_ref[...] = jnp.zeros_like(sg_ref)

        @pl.when(f == 0)
        def _():
            acc_ref[...] = jnp.zeros_like(acc_ref)
            dx = dx_ref[...]
            dmo_ref[...] = (dx * pv_ref[5:6, :]).astype(BF16)
            sg_ref[0:1, :] += jnp.sum(dx * mo_ref[...].astype(F32), axis=0, keepdims=True)

        r = jnp.maximum(_dot(h_ref[...], w1_ref[...], NT), 0.0)
        b_ref[...] = (r * r).astype(BF16)
        da = (_dot(dmo_ref[...], w2_ref[...], NT) * (2.0 * r)).astype(BF16)
        da_ref[...] = da
        acc_ref[...] += _dot(da, w1_ref[...], NN)

        @pl.when(f == nf - 1)
        def _():
            _, vjp = jax.vjp(_norm_mod, x_ref[...], pv_ref[7:8, :], pv_ref[4:5, :], pv_ref[3:4, :])
            dxn, dg, dsc, dsh = vjp(acc_ref[...])
            dx1_ref[...] = dx_ref[...] + dxn
            sg_ref[1:2, :] += dsh
            sg_ref[2:3, :] += dsc
            sg_ref[3:4, :] += dg

    row = pl.BlockSpec((t, d), lambda i, f: (i, 0))
    wblk = pl.BlockSpec((MLP_F, d), lambda i, f: (f, 0))
    hid = pl.BlockSpec((t, MLP_F), lambda i, f: (i, f))
    return _call(
        body, (dx2, h2, x1, mo2, pv, w1t, w2), grid=(s // t, nf),
        in_specs=[row, row, row, row, pl.BlockSpec((16, d), lambda i, f: (0, 0)), wblk, wblk],
        out_specs=[row, hid, hid, row, pl.BlockSpec((8, d), lambda i, f: (0, 0))],
        out_shape=[_sds((s, d), F32), _sds((s, D_FF), BF16), _sds((s, D_FF), BF16), _sds((s, d), BF16), _sds((8, d), F32)],
        scratch_shapes=[pltpu.VMEM((t, d), F32)],
        semantics=("arbitrary", "arbitrary"), name=name, comm=comm)


def _loss_head(y, target, name):
    s, d = y.shape
    t = min(ROW_T, s)

    def body(y_ref, t_ref, dy_ref, ls_ref):
        i = pl.program_id(0)

        @pl.when(i == 0)
        def _():
            ls_ref[...] = jnp.zeros_like(ls_ref)

        e = y_ref[...] - t_ref[...]
        dy_ref[...] = e * (1.0 / d)
        ls_ref[...] += jnp.sum((e * e).reshape(t // 8, 8, d), axis=0)

    row = pl.BlockSpec((t, d), lambda i: (i, 0))
    return pl.pallas_call(
        body, grid=(s // t,), in_specs=[row, row], out_specs=[row, pl.BlockSpec((8, d), lambda i: (0, 0))],
        out_shape=[_sds((s, d), F32), _sds((8, d), F32)],
        compiler_params=_params("arbitrary"), name=name)(y, target)


def _layer_vectors(l, mod, sm):
    d = D_MODEL
    pv = jnp.concatenate([mod[l].reshape(6, d), sm["norm1_g"][l][None], sm["norm2_g"][l][None],
                          sm["gate_b"][l].reshape(3, d), jnp.zeros((5, d), F32)], axis=0)
    cp = jnp.concatenate([sm["conv_b"][l][None], sm["conv_ln_g"][l][None], sm["conv_ln_b"][l][None],
                          jnp.zeros((5, CONV_CH), F32)], axis=0)
    return dict(pv=pv, cp=cp, conv_w=sm["conv_w"][l], lb=(sm["hgrn_lb"] if l > 0 else None),
                ng=sm["hgrn_norm_g"][l][None], gq=jnp.tile(sm["sb_qn_g"][l], SB_HEADS)[None],
                gk=jnp.tile(sm["sb_kn_g"][l], SB_HEADS)[None])


def _hosted(res, comm):
    return res if comm is not None else (res, None)


def _layer_fwd_mixers(x, vec, win_t, tag, comm_proj=None, comm_hgrn=None, comm_sb=None):
    h = _prenorm(x, vec["pv"], f"prenorm{tag}")
    proj, got_proj = _hosted(_matmul(h, win_t, "nt", F32, 1024, 768, 1024, f"proj{tag}", comm_proj), comm_proj)
    u3 = _conv_fwd(proj, vec["conv_w"], vec["cp"], f"conv_fwd{tag}")
    (oh, states), got_hgrn = _hosted(_hgrn_fwd(proj, vec["lb"], vec["ng"], f"hgrn_fwd{tag}", comm_hgrn), comm_hgrn)
    qn, kn, vb = _sb_prep(proj, vec["gq"], vec["gk"], f"sb_prep{tag}")
    (osb,), got_sb = _hosted(_sb_fwd(qn, kn, vb, f"sb_fwd{tag}", comm_sb), comm_sb)
    saved = dict(x=x, h=h, proj=proj, u3=u3, oh=oh, states=states, qn=qn, kn=kn, vb=vb, osb=osb)
    return saved, (got_proj, got_hgrn, got_sb)


def _layer_fwd_out(sv, vec, w, tag, comm_mlp=None):
    x1, h2, merged, mo1 = _mix_fwd(sv["u3"], sv["oh"], sv["osb"], sv["proj"], sv["x"], vec["pv"],
                                   w["wc_t"], w["wh_t"], w["ws_t"], w["wo"], f"mix_fwd{tag}")
    (x2, mo2), got = _hosted(_mlp_fwd(h2, x1, vec["pv"], w["w1_t"], w["w2"], f"mlp_fwd{tag}", comm_mlp), comm_mlp)
    sv.update(x1=x1, h2=h2, merged=merged, mo1=mo1, mo2=mo2)
    return x2, got


def _layer_bwd(dx2, sv, vec, w, tag, plans=None):
    plans = plans or {}
    got = {}

    def plan_for(key, big_now):
        return plans[key](big_now) if key in plans else None

    pv = vec["pv"]
    big = {}
    comm = plan_for("mlp", big)
    (dx1, da, bsq, dmo2, sg_mlp), got["mlp"] = _hosted(
        _mlp_bwd(dx2, sv["h2"], sv["x1"], sv["mo2"], pv, w["w1_t"], w["w2"], f"mlp_bwd{tag}", comm), comm)
    big["w1_t"] = _matmul(da, sv["h2"], "tn", BF16, 1024, 1024, 1024, f"dw1{tag}")
    big["w2"] = _matmul(bsq, dmo2, "tn", BF16, 1024, 1024, 1024, f"dw2{tag}")
    dmo1, dyc, dyh, dys, doc, doh, dos, dgl, sg_mix = _mix_bwd(
        dx1, sv["mo1"], sv["u3"], sv["oh"], sv["osb"], sv["proj"], pv, w["wc_t"], w["wh_t"], w["ws_t"], w["wo"], f"mix_bwd{tag}")
    big["wo"] = _matmul(sv["merged"], dmo1, "tn", BF16, 1024, 1024, 1024, f"dwo{tag}")
    big["wc_t"] = _matmul(dyc, sv["u3"], "tn", BF16, 1024, 512, 1024, f"dwc{tag}")
    big["wh_t"] = _matmul(dyh, sv["oh"], "tn", BF16, 1024, 512, 1024, f"dwh{tag}")
    big["ws_t"] = _matmul(dys, sv["osb"], "tn", BF16, 1024, 512, 1024, f"dws{tag}")
    comm = plan_for("conv", big)
    (da_c, dg_c, dconv_w, sg_conv), got["conv"] = _hosted(
        _conv_bwd(sv["proj"], doc, vec["conv_w"], vec["cp"], f"conv_bwd{tag}", comm), comm)
    comm = plan_for("hgrn", big)
    (dq_h, df_h, di_h, dg_h, dlb, dng), got["hgrn"] = _hosted(
        _hgrn_bwd(sv["proj"], sv["states"], doh, vec["lb"], vec["ng"], f"hgrn_bwd{tag}", comm), comm)
    comm = plan_for("sb", big)
    (dqn, dkn, dv_s), got["sb"] = _hosted(_sb_bwd(sv["qn"], sv["kn"], sv["vb"], dos, f"sb_bwd{tag}", comm), comm)
    dq_s, dk_s, sg_sb = _sb_prep_bwd(sv["proj"], dqn, dkn, vec["gq"], vec["gk"], f"sb_prep_bwd{tag}")
    dproj = jnp.concatenate([da_c, dg_c, dq_h, df_h, di_h, dg_h, dq_s, dk_s, dv_s, dgl], axis=1)
    half = D_MODEL // 2
    comm = plan_for("dwin_a", big)
    big["win_a"], got["dwin_a"] = _hosted(
        _matmul(dproj, sv["h"][:, :half], "tn", BF16, 768, half, 1024, f"dwin_a{tag}", comm), comm)
    comm = plan_for("dwin", big)
    big["win_b"], got["dwin"] = _hosted(
        _matmul(dproj, sv["h"][:, half:], "tn", BF16, 768, half, 1024, f"dwin_b{tag}", comm), comm)
    comm = plan_for("dh", big)
    (dx, sg_pre), got["dh"] = _hosted(_dh_prenorm_bwd(dproj, w["win_t"], dx1, sv["x"], pv, f"dh{tag}", comm), comm)
    small = dict(
        mod=jnp.stack([sg_pre[0], sg_pre[1], sg_mix[0], sg_mlp[1], sg_mlp[2], sg_mlp[0]]).reshape(6 * D_MODEL),
        norm1_g=sg_pre[2], norm2_g=sg_mlp[3], gate_b=sg_mix[1:4].reshape(3 * D_MODEL),
        conv_w=dconv_w, conv_b=sg_conv[0], conv_ln_g=sg_conv[1], conv_ln_b=sg_conv[2],
        hgrn_lb=dlb, hgrn_norm_g=dng[0],
        sb_qn_g=sg_sb[0].reshape(SB_HEADS, SB_DH).sum(0), sb_kn_g=sg_sb[1].reshape(SB_HEADS, SB_DH).sum(0))
    return dx, big, small, got


def _row_tile(r, cap=512):
    t = min(r, cap)
    while r % t or (t % 8 and t != r):
        t -= 1
    return t


def _sum8(z, name):
    _, r, c = z.shape
    t = _row_tile(r, 128 if c >= 1024 else 512)

    def body(z_ref, o_ref):
        acc = z_ref[0].astype(F32)
        for j in range(1, N_DEV):
            acc = acc + z_ref[j].astype(F32)
        o_ref[...] = acc

    return pl.pallas_call(
        body, grid=(r // t,), in_specs=[pl.BlockSpec((N_DEV, t, c), lambda i: (0, i, 0))],
        out_specs=pl.BlockSpec((t, c), lambda i: (i, 0)), out_shape=_sds((r, c), F32),
        compiler_params=_params("parallel"), name=name)(z)


def _adamw(w, g, m, v, name):
    r, c = w.shape
    t = _row_tile(r, 256)

    def body(w_ref, g_ref, m_ref, v_ref, d_ref, nm_ref, nv_ref):
        g_ = g_ref[...]
        nm = ADAM_B1 * m_ref[...] + (1.0 - ADAM_B1) * g_
        nv = ADAM_B2 * v_ref[...] + (1.0 - ADAM_B2) * jnp.square(g_)
        m_hat = nm / (1.0 - ADAM_B1 ** ADAM_STEP)
        v_hat = nv / (1.0 - ADAM_B2 ** ADAM_STEP)
        d_ref[...] = -ADAM_LR * (m_hat / (jnp.sqrt(v_hat) + ADAM_EPS) + ADAM_WD * w_ref[...])
        nm_ref[...] = nm
        nv_ref[...] = nv

    blk = pl.BlockSpec((t, c), lambda i: (i, 0))
    return pl.pallas_call(
        body, grid=(r // t,), in_specs=[blk] * 4, out_specs=[blk] * 3, out_shape=[_sds((r, c), F32)] * 3,
        compiler_params=_params("parallel"), name=name)(w, g, m, v)


def _mod_local(c_all, mod_w, name):
    depth, d, cols = mod_w.shape

    def body(c_ref, w_ref, o_ref):
        cv = c_ref[...]
        act = cv * jax.nn.sigmoid(cv)
        o_ref[...] = jnp.dot(act, w_ref[...], precision=lax.Precision.HIGHEST, preferred_element_type=F32)

    return pl.pallas_call(
        body, grid=(depth,),
        in_specs=[pl.BlockSpec((N_DEV, d), lambda l: (0, 0)), pl.BlockSpec((None, d, cols), lambda l: (l, 0, 0))],
        out_specs=pl.BlockSpec((None, N_DEV, cols), lambda l: (l, 0, 0)), out_shape=_sds((depth, N_DEV, cols), F32),
        compiler_params=_params("parallel"), name=name)(c_all, mod_w)


def _modw_grad(c_all, dmod, name):
    depth, _, cols = dmod.shape
    d = c_all.shape[1]

    def body(c_ref, g_ref, o_ref):
        cv = c_ref[...]
        act = cv * jax.nn.sigmoid(cv)
        o_ref[...] = lax.dot_general(act, g_ref[...], (TN, ((), ())), precision=lax.Precision.HIGHEST,
                                     preferred_element_type=F32)

    return pl.pallas_call(
        body, grid=(depth,),
        in_specs=[pl.BlockSpec((N_DEV, d), lambda l: (0, 0)), pl.BlockSpec((None, N_DEV, cols), lambda l: (l, 0, 0))],
        out_specs=pl.BlockSpec((None, d, cols), lambda l: (l, 0, 0)), out_shape=_sds((depth, d, cols), F32),
        compiler_params=_params("parallel"), name=name)(c_all, dmod)


LANE = 128
BIG = {"w_in": ("win_t", True), "w_out": ("wo", False), "mlp_w2": ("w2", False), "mlp_w1": ("w1_t", True),
       "w_conv_proj": ("wc_t", True), "w_hgrn_proj": ("wh_t", True), "w_sb_proj": ("ws_t", True)}
PROJS = ("w_conv_proj", "w_hgrn_proj", "w_sb_proj")
SMALL = (("mod_b", 6144), ("norm1_g", 1024), ("gate_b", 3072), ("conv_w", CONV_WIDTH * CONV_CH), ("conv_b", 512),
         ("conv_ln_g", 512), ("conv_ln_b", 512), ("hgrn_lb", 512), ("hgrn_norm_g", 128), ("sb_qn_g", 64),
         ("sb_kn_g", 64), ("norm2_g", 1024))


def _pack_rows(parts, width):
    flat = jnp.concatenate([p.reshape(-1) for p in parts])
    rows = -(-flat.shape[0] // width)
    rows = -(-rows // 8) * 8
    return jnp.pad(flat, (0, rows * width - flat.shape[0])).reshape(rows, width)


def _shards(params, items):
    return [(params[n][l].T if BIG[n][1] else params[n][l]).astype(BF16) for n, l in items]


def _gathered(items, got):
    return {BIG[n][0]: g.reshape(-1, g.shape[2]) for (n, _), g in zip(items, got)}


def _by_shard(g):
    return g.reshape(N_DEV, g.shape[0] // N_DEV, g.shape[1])


def _adamw_nd(w, g, m, v, name):
    shape = w.shape
    two = lambda a: a.reshape(-1, shape[-1])
    return [o.reshape(shape) for o in _adamw(two(w), two(g), two(m), two(v), name)]


WEIGHTS = ("mod_w", "mod_b", "norm1_g", "w_in", "gate_b", "conv_w", "conv_b", "conv_ln_g", "conv_ln_b", "w_conv_proj",
           "hgrn_lb", "hgrn_norm_g", "w_hgrn_proj", "sb_qn_g", "sb_kn_g", "w_sb_proj", "w_out", "norm2_g", "mlp_w1",
           "mlp_w2")


def kernel(x, c, mod_w, mod_b, norm1_g, w_in, gate_b, conv_w, conv_b, conv_ln_g, conv_ln_b, w_conv_proj, hgrn_lb, hgrn_norm_g, w_hgrn_proj, sb_qn_g, sb_kn_g, w_sb_proj, w_out, norm2_g, mlp_w1, mlp_w2, loss_target, m_mod_w, m_mod_b, m_norm1_g, m_w_in, m_gate_b, m_conv_w, m_conv_b, m_conv_ln_g, m_conv_ln_b, m_w_conv_proj, m_hgrn_lb, m_hgrn_norm_g, m_w_hgrn_proj, m_sb_qn_g, m_sb_kn_g, m_w_sb_proj, m_w_out, m_norm2_g, m_mlp_w1, m_mlp_w2, v_mod_w, v_mod_b, v_norm1_g, v_w_in, v_gate_b, v_conv_w, v_conv_b, v_conv_ln_g, v_conv_ln_b, v_w_conv_proj, v_hgrn_lb, v_hgrn_norm_g, v_w_hgrn_proj, v_sb_qn_g, v_sb_kn_g, v_w_sb_proj, v_w_out, v_norm2_g, v_mlp_w1, v_mlp_w2):
    params = dict(mod_w=mod_w, mod_b=mod_b, norm1_g=norm1_g, w_in=w_in, gate_b=gate_b, conv_w=conv_w, conv_b=conv_b,
                  conv_ln_g=conv_ln_g, conv_ln_b=conv_ln_b, w_conv_proj=w_conv_proj, hgrn_lb=hgrn_lb,
                  hgrn_norm_g=hgrn_norm_g, w_hgrn_proj=w_hgrn_proj, sb_qn_g=sb_qn_g, sb_kn_g=sb_kn_g,
                  w_sb_proj=w_sb_proj, w_out=w_out, norm2_g=norm2_g, mlp_w1=mlp_w1, mlp_w2=mlp_w2)
    mom1 = dict(mod_w=m_mod_w, mod_b=m_mod_b, norm1_g=m_norm1_g, w_in=m_w_in, gate_b=m_gate_b, conv_w=m_conv_w,
                conv_b=m_conv_b, conv_ln_g=m_conv_ln_g, conv_ln_b=m_conv_ln_b, w_conv_proj=m_w_conv_proj,
                hgrn_lb=m_hgrn_lb, hgrn_norm_g=m_hgrn_norm_g, w_hgrn_proj=m_w_hgrn_proj, sb_qn_g=m_sb_qn_g,
                sb_kn_g=m_sb_kn_g, w_sb_proj=m_w_sb_proj, w_out=m_w_out, norm2_g=m_norm2_g, mlp_w1=m_mlp_w1,
                mlp_w2=m_mlp_w2)
    mom2 = dict(mod_w=v_mod_w, mod_b=v_mod_b, norm1_g=v_norm1_g, w_in=v_w_in, gate_b=v_gate_b, conv_w=v_conv_w,
                conv_b=v_conv_b, conv_ln_g=v_conv_ln_g, conv_ln_b=v_conv_ln_b, w_conv_proj=v_w_conv_proj,
                hgrn_lb=v_hgrn_lb, hgrn_norm_g=v_hgrn_norm_g, w_hgrn_proj=v_w_hgrn_proj, sb_qn_g=v_sb_qn_g,
                sb_kn_g=v_sb_kn_g, w_sb_proj=v_w_sb_proj, w_out=v_w_out, norm2_g=v_norm2_g, mlp_w1=v_mlp_w1,
                mlp_w2=v_mlp_w2)
    xi, yi, ci = _mesh_place()
    me = _block_of(xi, yi, ci)
    cw_cols = conv_w.shape[2]

    tiny = _pack_rows([c, conv_w], LANE)
    g_tiny, g_win0 = _comm_alone(_GatherPlan([tiny] + _shards(params, [("w_in", 0)])), "gather_first")
    c_rows = D_MODEL // LANE
    c_all = g_tiny[:, :c_rows].reshape(N_DEV, D_MODEL)
    n_cw = DEPTH * CONV_WIDTH * cw_cols
    conv_w_full = g_tiny[:, c_rows:c_rows + n_cw // LANE].reshape(N_DEV, DEPTH, CONV_WIDTH, cw_cols)
    conv_w_full = conv_w_full.transpose(1, 2, 0, 3).reshape(DEPTH, CONV_WIDTH, CONV_CH)

    (g_mod,) = _comm_alone(_GatherPlan([_mod_local(c_all, mod_w, "mod_local")]), "gather_mod")
    mod = lax.dynamic_index_in_dim(g_mod, me, axis=2, keepdims=False)
    mod = mod.transpose(1, 0, 2).reshape(DEPTH, 6 * D_MODEL) + mod_b

    sm = dict(norm1_g=norm1_g, norm2_g=norm2_g, gate_b=gate_b, conv_w=conv_w_full, conv_b=conv_b, conv_ln_g=conv_ln_g,
              conv_ln_b=conv_ln_b, hgrn_lb=hgrn_lb, hgrn_norm_g=hgrn_norm_g, sb_qn_g=sb_qn_g, sb_kn_g=sb_kn_g)
    vecs = [_layer_vectors(l, mod, sm) for l in range(DEPTH)]

    fwd_hosts = dict(
        proj=[("w_in", 1)],
        hgrn=[("mlp_w2", 0), ("w_out", 0)] + [(n, 0) for n in PROJS],
        sb=[("mlp_w1", 0), ("mlp_w2", 1)],
        mlp=[("mlp_w1", 1), ("w_out", 1)] + [(n, 1) for n in PROJS])
    gather = {host: _GatherPlan(_shards(params, items)) for host, items in fwd_hosts.items()}
    wts = [_gathered([("w_in", 0)], [g_win0]), {}]
    sv0, got_mixers = _layer_fwd_mixers(x[0], vecs[0], wts[0]["win_t"], "_l0", gather["proj"], gather["hgrn"], gather["sb"])
    for host, got in zip(("proj", "hgrn", "sb"), got_mixers):
        for (name, l), g in zip(fwd_hosts[host], got):
            wts[l].update(_gathered([(name, l)], [g]))
    y, got = _layer_fwd_out(sv0, vecs[0], wts[0], "_l0", gather["mlp"])
    wts[1].update(_gathered(fwd_hosts["mlp"], got))
    sv1, _ = _layer_fwd_mixers(y, vecs[1], wts[1]["win_t"], "_l1")
    y, _ = _layer_fwd_out(sv1, vecs[1], wts[1], "_l1")
    dy, sq = _loss_head(y, loss_target[0], "loss_head")
    loss = lax.psum(0.5 * jnp.sum(sq) / D_MODEL, ("x", "y", "c"))

    half = D_MODEL // 2
    dy, big1, small1, _ = _layer_bwd(dy, sv1, vecs[1], wts[1], "_l1")
    bwd_hosts = dict(
        mlp=[(1, "win_a"), (1, "wo")],
        conv=[(1, "win_b"), (0, "wc_t"), (0, "wh_t"), (0, "ws_t")],
        hgrn=[(1, "w2"), (0, "wo"), (0, "w2_a")],
        sb=[(1, "w1_t"), (0, "w1_t"), (1, "wc_t"), (1, "wh_t"), (1, "ws_t")],
        dwin_a=[(0, "w2_b")],
        dwin=[(0, "win_a")],
        dh=[(0, "win_b")])

    def source(l, key, big0):
        big = big1 if l == 1 else big0
        if key in ("w2_a", "w2_b"):
            return big["w2"][:, :half] if key == "w2_a" else big["w2"][:, half:]
        return big[key]

    plans = {host: (lambda big0, items=items: _ExchangePlan([_by_shard(source(l, k, big0)) for l, k in items]))
             for host, items in bwd_hosts.items()}
    dx, _, small0, got = _layer_bwd(dy, sv0, vecs[0], wts[0], "_l0", plans)
    smalls = [small0, small1]
    summed = {}
    for host, items in bwd_hosts.items():
        for (l, key), arrived in zip(items, got[host]):
            summed[l, key] = _sum8(arrived, f"sum_{key}_l{l}")
    summed[0, "w2"] = jnp.concatenate([summed[0, "w2_a"], summed[0, "w2_b"]], axis=1)
    grads = {}
    for name, (key, transposed) in BIG.items():
        per_layer = []
        for l in range(DEPTH):
            if name == "w_in":
                blk = jnp.concatenate([summed[l, "win_a"], summed[l, "win_b"]], axis=1)
            else:
                blk = summed[l, key]
            per_layer.append(blk.T if transposed else blk)
        grads[name] = jnp.stack(per_layer)

    small_parts = []
    for name, _ in SMALL:
        key = "mod" if name == "mod_b" else name
        if name == "hgrn_lb":
            small_parts.append(smalls[0][key] + smalls[1][key])
        else:
            small_parts.append(jnp.stack([smalls[l][key] for l in range(DEPTH)]))
    (g_small,) = _comm_alone(_GatherPlan([_pack_rows(small_parts, LANE)]), "gather_small_grads")
    small_sum = _sum8(g_small, "sum_small_grads").reshape(-1)
    off = 0
    for name, per_layer in SMALL:
        grads[name] = small_sum[off:off + DEPTH * per_layer].reshape(params[name].shape if name != "conv_w" else (DEPTH, CONV_WIDTH, CONV_CH))
        off += DEPTH * per_layer
    grads["conv_w"] = lax.dynamic_slice_in_dim(grads["conv_w"], me * cw_cols, cw_cols, axis=2)
    cols = mod_w.shape[2]
    dmod_all = g_small.reshape(N_DEV, -1)[:, :DEPTH * 6 * D_MODEL].reshape(N_DEV, DEPTH, 6 * D_MODEL)
    dmod_mine = lax.dynamic_slice_in_dim(dmod_all, me * cols, cols, axis=2).transpose(1, 0, 2)
    grads["mod_w"] = _modw_grad(c_all, dmod_mine, "mod_w_grad")

    delta, new_m, new_v = {}, {}, {}
    small_names = [n for n, _ in SMALL]
    for name in WEIGHTS:
        if name not in small_names:
            delta[name], new_m[name], new_v[name] = _adamw_nd(params[name], grads[name], mom1[name], mom2[name], f"adamw_{name}")
    packed = [_pack_rows([d[n] for n in small_names], LANE) for d in (params, grads, mom1, mom2)]
    outs = [o.reshape(-1) for o in _adamw(*packed, "adamw_small")]
    off = 0
    for name in small_names:
        size = params[name].size
        for dst, o in zip((delta, new_m, new_v), outs):
            dst[name] = o[off:off + size].reshape(params[name].shape)
        off += size
    return (loss, dx[None], *[grads[n] for n in WEIGHTS], *[delta[n] for n in WEIGHTS],
            *[new_m[n] for n in WEIGHTS], *[new_v[n] for n in WEIGHTS])
```

```python
import functools

import jax
import jax.numpy as jnp
import numpy as np
from jax import lax
from jax.experimental import pallas as pl
from jax.experimental.pallas import tpu as pltpu

F32 = jnp.float32
BF16 = jnp.bfloat16

D_MODEL = 1024
DEPTH = 2
N_DEV = 8
CONV_CH = 512
CONV_WIDTH = 31
CONV_HALO = 32
HG_HEADS = 4
HG_DK = 128
SB_HEADS = 8
SB_DH = 64
D_IN = 7680
D_FF = 4096
EPS = 1e-6
SB_BLK = 128
SB_DEAD = -104.0
SB_FIXED = 3
SB_PAIRS = 2
HG_CHUNK = 128

ADAM_LR = 0.001
ADAM_B1 = 0.9
ADAM_B2 = 0.999
ADAM_EPS = 1e-08
ADAM_WD = 0.01
ADAM_STEP = 10

VMEM_LIMIT = 48 * 1024 * 1024

NN = ((1,), (0,))
NT = ((1,), (1,))
TN = ((0,), (0,))
_DIMS = {"nn": NN, "nt": NT, "tn": TN}


def _sds(shape, dtype):
    return jax.ShapeDtypeStruct(shape, dtype)


def _params(*semantics):
    return pltpu.CompilerParams(dimension_semantics=semantics, vmem_limit_bytes=VMEM_LIMIT)


def _dot(a, b, dims):
    return lax.dot_general(a, b, (dims, ((), ())), preferred_element_type=F32)


@functools.partial(jax.custom_vjp, nondiff_argnums=(2,))
def _bdot(a, b, mode):
    return _dot(a.astype(BF16), b.astype(BF16), _DIMS[mode])


def _bdot_fwd(a, b, mode):
    return _bdot(a, b, mode), (a.astype(BF16), b.astype(BF16))


def _bdot_bwd(mode, res, g):
    a, b = res
    g = g.astype(BF16)
    if mode == "nn":
        return _dot(g, b, NT), _dot(a, g, TN)
    if mode == "nt":
        return _dot(g, b, NN), _dot(g, a, TN)
    return _dot(b, g, NT), _dot(a, g, NN)


_bdot.defvjp(_bdot_fwd, _bdot_bwd)


def _split(x):
    hi = x.astype(BF16)
    lo = (x - hi.astype(F32)).astype(BF16)
    return hi, lo


def _xdot_right(x, m, dims=NN):
    hi, lo = _split(x)
    if dims == NN:
        return _dot(jnp.concatenate([hi, lo], axis=1), jnp.concatenate([m, m], axis=0), NN)
    return _dot(jnp.concatenate([hi, lo], axis=1), jnp.concatenate([m, m], axis=1), NT)


def _xdot_left(m, x, dims=NN):
    hi, lo = _split(x)
    if dims == NN:
        return _dot(jnp.concatenate([m, m], axis=1), jnp.concatenate([hi, lo], axis=0), NN)
    return _dot(jnp.concatenate([m, m], axis=0), jnp.concatenate([hi, lo], axis=0), TN)


@jax.custom_vjp
def _xr(x, m):
    return _xdot_right(x, m)


def _xr_fwd(x, m):
    return _xdot_right(x, m), m


def _xr_bwd(m, g):
    return _xdot_right(g, m, NT), jnp.zeros_like(m)


_xr.defvjp(_xr_fwd, _xr_bwd)


def _norm_mod(x, g, sc, sh):
    r = lax.rsqrt(jnp.mean(x * x, axis=-1, keepdims=True) + EPS)
    return x * r * g * (1.0 + sc) + sh


MESH = pl.DeviceIdType.MESH
HBM_SPEC = pl.BlockSpec(memory_space=pltpu.HBM)


def _mesh_place():
    return lax.axis_index("x"), lax.axis_index("y"), lax.axis_index("c")


def _block_of(px, py, pc):
    return 4 * px + 2 * py + pc


def _sem_scratch(n):
    return [pltpu.SemaphoreType.DMA((n, N_DEV - 1)), pltpu.SemaphoreType.DMA((n, N_DEV - 1)), pltpu.SemaphoreType.DMA((n,))]


class _GatherPlan:
    def __init__(self, xs):
        self.xs = list(xs)
        self.n = len(self.xs)
        self.out_shape = [_sds((N_DEV, *v.shape), v.dtype) for v in self.xs]
        self.scratch = _sem_scratch(self.n)

    def _parts(self, x_refs, out_refs, sems):
        send_sems, recv_sems, local_sems = sems
        x, y, c = _mesh_place()
        me, sibling = (x, y, c), (x, y, 1 - c)
        chips = [(1 - x, y), (x, 1 - y), (1 - x, 1 - y)]

        def copy(a, k, block, to, src=None):
            rows = out_refs[a].at[_block_of(*block)]
            return pltpu.make_async_remote_copy(
                src_ref=rows if src is None else src, dst_ref=rows, send_sem=send_sems.at[a, k],
                recv_sem=recv_sems.at[a, k], device_id=to, device_id_type=MESH)

        local = [pltpu.make_async_copy(x_refs[a], out_refs[a].at[_block_of(*me)], local_sems.at[a])
                 for a in range(self.n)]
        first = []
        for a in range(self.n):
            first.append(copy(a, 0, me, sibling, src=x_refs[a]))
            first += [copy(a, 1 + j, me, (*chip, c), src=x_refs[a]) for j, chip in enumerate(chips)]
        return me, sibling, chips, c, copy, local, first

    def start(self, x_refs, out_refs, sems):
        *_, local, first = self._parts(x_refs, out_refs, sems)
        for cp in local + first:
            cp.start()

    def finish(self, x_refs, out_refs, sems):
        me, sibling, chips, c, copy, local, first = self._parts(x_refs, out_refs, sems)
        passed = []
        for j, chip in enumerate(chips):
            for a in range(self.n):
                copy(a, 1 + j, (*chip, c), me).wait_recv()
                fwd = copy(a, 4 + j, (*chip, c), sibling)
                fwd.start()
                passed.append(fwd)
        for a in range(self.n):
            copy(a, 0, sibling, me).wait_recv()
            for j, chip in enumerate(chips):
                copy(a, 4 + j, (*chip, 1 - c), me).wait_recv()
        for cp in first + passed:
            cp.wait_send()
        for cp in local:
            cp.wait()


class _ExchangePlan:
    def __init__(self, xs):
        self.xs = list(xs)
        self.n = len(self.xs)
        self.out_shape = [_sds(v.shape, v.dtype) for v in self.xs]
        self.scratch = _sem_scratch(self.n)

    def _parts(self, in_refs, out_refs, sems):
        send_sems, recv_sems, local_sems = sems
        x, y, c = _mesh_place()
        mine = _block_of(x, y, c)
        peers = [(1 - x if k & 4 else x, 1 - y if k & 2 else y, 1 - c if k & 1 else c) for k in range(1, N_DEV)]

        def copy(a, k, slot_src, slot_dst):
            return pltpu.make_async_remote_copy(
                src_ref=in_refs[a].at[slot_src], dst_ref=out_refs[a].at[slot_dst], send_sem=send_sems.at[a, k],
                recv_sem=recv_sems.at[a, k], device_id=peers[k], device_id_type=MESH)

        local = [pltpu.make_async_copy(in_refs[a].at[mine], out_refs[a].at[mine], local_sems.at[a])
                 for a in range(self.n)]
        sends = [copy(a, k, _block_of(*peers[k]), mine) for a in range(self.n) for k in range(N_DEV - 1)]
        arrivals = [copy(a, k, _block_of(*peers[k]), _block_of(*peers[k])) for a in range(self.n) for k in range(N_DEV - 1)]
        return local, sends, arrivals

    def start(self, in_refs, out_refs, sems):
        local, sends, _ = self._parts(in_refs, out_refs, sems)
        for cp in local + sends:
            cp.start()

    def finish(self, in_refs, out_refs, sems):
        local, sends, arrivals = self._parts(in_refs, out_refs, sems)
        for cp in arrivals:
            cp.wait_recv()
        for cp in sends:
            cp.wait_send()
        for cp in local:
            cp.wait()


def _call(body, args, *, grid, in_specs, out_specs, out_shape, scratch_shapes=(), semantics, name, comm=None):
    if comm is None:
        return pl.pallas_call(
            body, grid=grid, in_specs=list(in_specs), out_specs=list(out_specs), out_shape=list(out_shape),
            scratch_shapes=list(scratch_shapes), compiler_params=_params(*semantics), name=name)(*args)
    n_in, n_out, n_scr, n = len(in_specs), len(out_specs), len(scratch_shapes), comm.n

    def hosted(*refs):
        ins, rest = refs[:n_in], refs[n_in:]
        cin, rest = rest[:n], rest[n:]
        outs, rest = rest[:n_out], rest[n_out:]
        cout, rest = rest[:n], rest[n:]
        scr, sems = rest[:n_scr], rest[n_scr:]
        pids = [pl.program_id(d) for d in range(len(grid))]
        first = functools.reduce(jnp.logical_and, [p == 0 for p in pids])
        last = functools.reduce(jnp.logical_and, [p == g - 1 for p, g in zip(pids, grid)])

        @pl.when(first)
        def _():
            comm.start(cin, cout, sems)

        body(*ins, *outs, *scr)

        @pl.when(last)
        def _():
            comm.finish(cin, cout, sems)

    res = pl.pallas_call(
        hosted, grid=grid, in_specs=list(in_specs) + [HBM_SPEC] * n, out_specs=list(out_specs) + [HBM_SPEC] * n,
        out_shape=list(out_shape) + comm.out_shape, scratch_shapes=list(scratch_shapes) + comm.scratch,
        compiler_params=_params(*["arbitrary"] * len(grid)), name=name)(*args, *comm.xs)
    return res[:n_out], res[n_out:]


def _comm_alone(comm, name):
    def body(*refs):
        n = comm.n
        comm.start(refs[:n], refs[n:2 * n], refs[2 * n:])
        comm.finish(refs[:n], refs[n:2 * n], refs[2 * n:])

    return pl.pallas_call(
        body, in_specs=[HBM_SPEC] * comm.n, out_specs=[HBM_SPEC] * comm.n, out_shape=comm.out_shape,
        scratch_shapes=comm.scratch, name=name)(*comm.xs)


def _matmul(a, b, mode, out_dtype, tm, tn, tk, name, comm=None):
    if mode == "nn":
        (m, k), (_, n) = a.shape, b.shape
    elif mode == "nt":
        (m, k), (n, _) = a.shape, b.shape
    else:
        (k, m), (_, n) = a.shape, b.shape
    tm, tn, tk = min(tm, m), min(tn, n), min(tk, k)
    assert m % tm == 0 and n % tn == 0 and k % tk == 0, (name, m, n, k, tm, tn, tk)
    nk = k // tk
    dims = _DIMS[mode]

    def body(a_ref, b_ref, o_ref, acc_ref):
        if nk == 1:
            o_ref[...] = _dot(a_ref[...], b_ref[...], dims).astype(out_dtype)
            return
        kk = pl.program_id(2)

        @pl.when(kk == 0)
        def _():
            acc_ref[...] = _dot(a_ref[...], b_ref[...], dims)

        @pl.when((kk > 0) & (kk < nk - 1))
        def _():
            acc_ref[...] += _dot(a_ref[...], b_ref[...], dims)

        @pl.when(kk == nk - 1)
        def _():
            o_ref[...] = (acc_ref[...] + _dot(a_ref[...], b_ref[...], dims)).astype(out_dtype)

    if mode == "tn":
        a_spec = pl.BlockSpec((tk, tm), lambda i, j, kk: (kk, i))
        b_spec = pl.BlockSpec((tk, tn), lambda i, j, kk: (kk, j))
    elif mode == "nn":
        a_spec = pl.BlockSpec((tm, tk), lambda i, j, kk: (i, kk))
        b_spec = pl.BlockSpec((tk, tn), lambda i, j, kk: (kk, j))
    else:
        a_spec = pl.BlockSpec((tm, tk), lambda i, j, kk: (i, kk))
        b_spec = pl.BlockSpec((tn, tk), lambda i, j, kk: (j, kk))
    res = _call(
        body, (a, b), grid=(m // tm, n // tn, nk), in_specs=[a_spec, b_spec],
        out_specs=[pl.BlockSpec((tm, tn), lambda i, j, kk: (i, j))],
        out_shape=[_sds((m, n), out_dtype)], scratch_shapes=[pltpu.VMEM((tm, tn), F32)],
        semantics=("parallel", "parallel", "arbitrary"), name=name, comm=comm)
    return res[0] if comm is None else (res[0][0], res[1])


ROW_T = 512


def _prenorm(x, pv, name):
    s, d = x.shape
    t = min(ROW_T, s)

    def body(x_ref, pv_ref, h_ref):
        h = _norm_mod(x_ref[...], pv_ref[6:7, :], pv_ref[1:2, :], pv_ref[0:1, :])
        h_ref[...] = h.astype(BF16)

    return pl.pallas_call(
        body, grid=(s // t,),
        in_specs=[pl.BlockSpec((t, d), lambda i: (i, 0)), pl.BlockSpec((16, d), lambda i: (0, 0))],
        out_specs=pl.BlockSpec((t, d), lambda i: (i, 0)), out_shape=_sds((s, d), BF16),
        compiler_params=_params("parallel"), name=name)(x, pv)


DH_TK = 1920


def _dh_prenorm_bwd(dproj, win_t, dres, x, pv, name, comm=None):
    s, k = dproj.shape
    d = x.shape[1]
    t = min(ROW_T, s)
    tk = min(DH_TK, k)
    nk = k // tk

    def body(a_ref, b_ref, dres_ref, x_ref, pv_ref, dx_ref, sg_ref, acc_ref):
        i = pl.program_id(0)
        kk = pl.program_id(1)

        @pl.when((i == 0) & (kk == 0))
        def _():
            sg_ref[...] = jnp.zeros_like(sg_ref)

        @pl.when(kk == 0)
        def _():
            acc_ref[...] = _dot(a_ref[...], b_ref[...], NN)

        @pl.when((kk > 0) & (kk < nk - 1))
        def _():
            acc_ref[...] += _dot(a_ref[...], b_ref[...], NN)

        @pl.when(kk == nk - 1)
        def _():
            dh = acc_ref[...] + _dot(a_ref[...], b_ref[...], NN)
            _, vjp = jax.vjp(_norm_mod, x_ref[...], pv_ref[6:7, :], pv_ref[1:2, :], pv_ref[0:1, :])
            dx, dg, dsc, dsh = vjp(dh)
            dx_ref[...] = dres_ref[...] + dx
            sg_ref[0:1, :] += dsh
            sg_ref[1:2, :] += dsc
            sg_ref[2:3, :] += dg

    assert nk >= 2 and k % tk == 0, (k, tk)
    row = pl.BlockSpec((t, d), lambda i, kk: (i, 0))
    return _call(
        body, (dproj, win_t, dres, x, pv), grid=(s // t, nk),
        in_specs=[pl.BlockSpec((t, tk), lambda i, kk: (i, kk)), pl.BlockSpec((tk, d), lambda i, kk: (kk, 0)),
                  row, row, pl.BlockSpec((16, d), lambda i, kk: (0, 0))],
        out_specs=[row, pl.BlockSpec((8, d), lambda i, kk: (0, 0))],
        out_shape=[_sds((s, d), F32), _sds((8, d), F32)], scratch_shapes=[pltpu.VMEM((t, d), F32)],
        semantics=("arbitrary", "arbitrary"), name=name, comm=comm)


CONV_T = 512


def _conv_tile(a_ext, g_ext, w, b, ln_g, ln_b, n_out):
    u0 = a_ext * jax.nn.sigmoid(g_ext)
    off = CONV_HALO - (CONV_WIDTH - 1)
    acc = jnp.zeros((n_out, u0.shape[1]), F32) + b
    for r in range(8):
        taps = [k for k in range(CONV_WIDTH) if (off + k) % 8 == r]
        rows = n_out if r == 0 else n_out + 8
        part = None
        for k in taps:
            lo = (off + k) // 8 * 8
            term = w[k:k + 1, :] * u0[lo: lo + rows, :]
            part = term if part is None else part + term
        acc = acc + part[r: r + n_out, :]
    mu = jnp.mean(acc, axis=-1, keepdims=True)
    var = jnp.mean(jnp.square(acc - mu), axis=-1, keepdims=True)
    y = (acc - mu) * lax.rsqrt(var + EPS) * ln_g + ln_b
    return y * jax.nn.sigmoid(y)


def _conv_fwd(proj, conv_w, cp, name):
    s = proj.shape[0]
    t = min(CONV_T, s)
    c, h = CONV_CH, CONV_HALO

    def body(ap_ref, ac_ref, gp_ref, gc_ref, w_ref, cp_ref, o_ref):
        i = pl.program_id(0)
        live = (i > 0).astype(F32)
        a_ext = jnp.concatenate([ap_ref[t - h:, :] * live, ac_ref[...]], axis=0)
        g_ext = jnp.concatenate([gp_ref[t - h:, :], gc_ref[...]], axis=0)
        u = _conv_tile(a_ext, g_ext, w_ref[...], cp_ref[0:1, :], cp_ref[1:2, :], cp_ref[2:3, :], t)
        o_ref[...] = u.astype(BF16)

    prev = lambda col: pl.BlockSpec((t, c), lambda i: (jnp.maximum(i - 1, 0), col))
    cur = lambda col: pl.BlockSpec((t, c), lambda i: (i, col))
    return pl.pallas_call(
        body, grid=(s // t,),
        in_specs=[prev(0), cur(0), prev(1), cur(1),
                  pl.BlockSpec((CONV_WIDTH, c), lambda i: (0, 0)), pl.BlockSpec((8, c), lambda i: (0, 0))],
        out_specs=pl.BlockSpec((t, c), lambda i: (i, 0)), out_shape=_sds((s, c), BF16),
        compiler_params=_params("parallel"), name=name)(proj, proj, proj, proj, conv_w, cp)


def _conv_bwd(proj, do, conv_w, cp, name, comm=None):
    s = proj.shape[0]
    t = min(CONV_T, s)
    c, h = CONV_CH, CONV_HALO
    nt = s // t

    def body(ap_ref, ac_ref, an_ref, gp_ref, gc_ref, gn_ref, doc_ref, don_ref, w_ref, cp_ref,
             da_ref, dg_ref, dw_ref, sg_ref):
        i = pl.program_id(0)

        @pl.when(i == 0)
        def _():
            dw_ref[...] = jnp.zeros_like(dw_ref)
            sg_ref[...] = jnp.zeros_like(sg_ref)

        first = (i > 0).astype(F32)
        last = (i < nt - 1).astype(F32)
        a_ext = jnp.concatenate([ap_ref[t - h:, :] * first, ac_ref[...], an_ref[:h, :] * last], axis=0)
        g_ext = jnp.concatenate([gp_ref[t - h:, :], gc_ref[...], gn_ref[:h, :]], axis=0)
        fn = functools.partial(_conv_tile, n_out=t + h)
        _, vjp = jax.vjp(fn, a_ext, g_ext, w_ref[...], cp_ref[0:1, :], cp_ref[1:2, :], cp_ref[2:3, :])
        ct_own = jnp.concatenate([doc_ref[...], jnp.zeros((h, c), F32)], axis=0)
        ct_all = jnp.concatenate([doc_ref[...], don_ref[:h, :] * last], axis=0)
        _, _, dw, db, dlg, dlb = vjp(ct_own)
        da, dg, _, _, _, _ = vjp(ct_all)
        da_ref[...] = da[h:h + t, :].astype(BF16)
        dg_ref[...] = dg[h:h + t, :].astype(BF16)
        dw_ref[...] += dw
        sg_ref[0:1, :] += db
        sg_ref[1:2, :] += dlg
        sg_ref[2:3, :] += dlb

    prev = lambda col: pl.BlockSpec((t, c), lambda i: (jnp.maximum(i - 1, 0), col))
    cur = lambda col: pl.BlockSpec((t, c), lambda i: (i, col))
    nxt = lambda col: pl.BlockSpec((t, c), lambda i: (jnp.minimum(i + 1, nt - 1), col))
    return _call(
        body, (proj, proj, proj, proj, proj, proj, do, do, conv_w, cp), grid=(nt,),
        in_specs=[prev(0), cur(0), nxt(0), prev(1), cur(1), nxt(1), cur(0), nxt(0),
                  pl.BlockSpec((CONV_WIDTH, c), lambda i: (0, 0)), pl.BlockSpec((8, c), lambda i: (0, 0))],
        out_specs=[cur(0), cur(0), pl.BlockSpec((CONV_WIDTH, c), lambda i: (0, 0)),
                   pl.BlockSpec((8, c), lambda i: (0, 0))],
        out_shape=[_sds((s, c), BF16), _sds((s, c), BF16), _sds((CONV_WIDTH, c), F32), _sds((8, c), F32)],
        semantics=("arbitrary",), name=name, comm=comm)


def _hgrn_levels(c):
    out, m = [], c // 2
    while m >= 1:
        out.append(m)
        m //= 2
    return out


def _hgrn_consts(c):
    t = np.arange(c)[:, None]
    j = np.arange(c)[None, :]
    mats = [j <= t, j > t]
    for m in _hgrn_levels(c):
        same = (t // m) == (j // m)
        mats += [same & (j <= t), same & (j > t)]
    return jnp.asarray(np.concatenate(mats, axis=0).astype(np.float32), dtype=BF16)


@jax.custom_vjp
def _cums(lc, mall):
    c = lc.shape[0]
    full = _xdot_left(mall, lc)
    return tuple(full[i * c:(i + 1) * c, :] for i in range(mall.shape[0] // c))


def _cums_fwd(lc, mall):
    return _cums(lc, mall), mall


def _cums_bwd(mall, cts):
    return _xdot_left(mall, jnp.concatenate(cts, axis=0), TN), jnp.zeros_like(mall)


_cums.defvjp(_cums_fwd, _cums_bwd)


def _hgrn_chunk(q, f, v, g, lbs, ng, sts_in, mall):
    c = q.shape[0]
    keep = jax.nn.sigmoid(-f)
    if lbs:
        keep = (1.0 - jax.nn.sigmoid(lbs[1] - lbs[0])) * keep
    lc = jnp.log1p(-keep)
    qs = q * jax.nn.sigmoid(q)
    cs = _cums(lc, mall)
    q_in = qs * jnp.exp(cs[0])
    k_out = keep * jnp.exp(cs[1])
    decay = jnp.exp(jnp.sum(lc, axis=0, keepdims=True))
    qk = qs * keep
    r = lax.broadcasted_iota(jnp.int32, q.shape, 0)
    tt = lax.broadcasted_iota(jnp.int32, (c, c), 0)
    ss = lax.broadcasted_iota(jnp.int32, (c, c), 1)
    levels = []
    for li, m in enumerate(_hgrn_levels(c)):
        lg = m.bit_length() - 1
        odd = ((r >> lg) & 1) == 1
        qm = jnp.where(odd, qs * jnp.exp(cs[2 + 2 * li]), 0.0)
        km = jnp.where(odd, 0.0, keep * jnp.exp(cs[3 + 2 * li]))
        pair = (((tt >> lg) & 1) == 1) & ((ss >> lg) == (tt >> lg) - 1)
        levels.append((qm, km, pair))
    outs, sts_out = [], []
    for h, st_in in enumerate(sts_in):
        hs = slice(h * HG_DK, (h + 1) * HG_DK)
        vh = v[:, hs]
        sc = jnp.where(tt == ss, jnp.sum(qk[:, hs], axis=-1, keepdims=True), 0.0)
        for qm, km, pair in levels:
            sc = sc + jnp.where(pair, _bdot(qm[:, hs], km[:, hs], "nt"), 0.0)
        o = _bdot(q_in[:, hs], st_in, "nt") + _bdot(sc, vh, "nn")
        sts_out.append(st_in * decay[:, hs] + _bdot(vh, k_out[:, hs], "tn"))
        outs.append(o * lax.rsqrt(jnp.mean(o * o, axis=-1, keepdims=True) + EPS) * ng)
    return jnp.concatenate(outs, axis=1) * (g * jax.nn.sigmoid(g)), tuple(sts_out)


def _hgrn_fwd(proj, lb, ng, name, comm=None):
    s = proj.shape[0]
    c = HG_CHUNK
    nc = s // c
    mall = _hgrn_consts(c)
    col0 = 1024 // (HG_HEADS * HG_DK)

    def body(*refs):
        q_ref, f_ref, v_ref, g_ref = refs[:4]
        if lb is None:
            ng_ref, m_ref, y_ref, st_ref, scr = refs[4:]
        else:
            lb_ref, ng_ref, m_ref, y_ref, st_ref, scr = refs[4:]
        ci = pl.program_id(0)

        @pl.when(ci == 0)
        def _():
            scr[...] = jnp.zeros_like(scr)

        lbs = () if lb is None else (lb_ref[0:1, :], lb_ref[1:2, :])
        sts_in = tuple(scr[h] for h in range(HG_HEADS))
        for h in range(HG_HEADS):
            st_ref[h] = sts_in[h]
        y, sts_out = _hgrn_chunk(q_ref[...], f_ref[...], v_ref[...], g_ref[...], lbs, ng_ref[...], sts_in, m_ref[...])
        y_ref[...] = y.astype(BF16)
        for h in range(HG_HEADS):
            scr[h] = sts_out[h]

    w = HG_HEADS * HG_DK
    col = lambda k: pl.BlockSpec((c, w), lambda ci: (ci, col0 + k))
    in_specs = [col(0), col(1), col(2), col(3)]
    args = [proj, proj, proj, proj]
    if lb is not None:
        in_specs.append(pl.BlockSpec((2, w), lambda ci: (0, 0)))
        args.append(lb)
    in_specs += [pl.BlockSpec((1, HG_DK), lambda ci: (0, 0)), pl.BlockSpec(mall.shape, lambda ci: (0, 0))]
    args += [ng, mall]
    return _call(
        body, args, grid=(nc,), in_specs=in_specs,
        out_specs=[pl.BlockSpec((c, w), lambda ci: (ci, 0)),
                   pl.BlockSpec((HG_HEADS, None, HG_DK, HG_DK), lambda ci: (0, ci, 0, 0))],
        out_shape=[_sds((s, w), BF16), _sds((HG_HEADS, nc, HG_DK, HG_DK), F32)],
        scratch_shapes=[pltpu.VMEM((HG_HEADS, HG_DK, HG_DK), F32)],
        semantics=("arbitrary",), name=name, comm=comm)


def _hgrn_bwd(proj, states, dy, lb, ng, name, comm=None):
    s = proj.shape[0]
    c = HG_CHUNK
    nc = s // c
    mall = _hgrn_consts(c)
    col0 = 1024 // (HG_HEADS * HG_DK)

    def body(*refs):
        q_ref, f_ref, v_ref, g_ref, st_ref, dy_ref = refs[:6]
        if lb is None:
            ng_ref, m_ref, dq_ref, df_ref, dv_ref, dg_ref, dlb_ref, dng_ref, scr = refs[6:]
        else:
            lb_ref, ng_ref, m_ref, dq_ref, df_ref, dv_ref, dg_ref, dlb_ref, dng_ref, scr = refs[6:]
        ci = pl.program_id(0)

        @pl.when(ci == 0)
        def _():
            scr[...] = jnp.zeros_like(scr)
            dlb_ref[...] = jnp.zeros_like(dlb_ref)
            dng_ref[...] = jnp.zeros_like(dng_ref)

        mall_v = m_ref[...]
        fn = lambda q, f, v, g, lbs_, ng_, sts: _hgrn_chunk(q, f, v, g, lbs_, ng_, sts, mall_v)
        lbs = () if lb is None else (lb_ref[0:1, :], lb_ref[1:2, :])
        sts_in = tuple(st_ref[h] for h in range(HG_HEADS))
        _, vjp = jax.vjp(fn, q_ref[...], f_ref[...], v_ref[...], g_ref[...], lbs, ng_ref[...], sts_in)
        dq, df, dv, dg, dlbs, dng, dsts = vjp((dy_ref[...], tuple(scr[h] for h in range(HG_HEADS))))
        dq_ref[...] = dq.astype(BF16)
        df_ref[...] = df.astype(BF16)
        dv_ref[...] = dv.astype(BF16)
        dg_ref[...] = dg.astype(BF16)
        for h in range(HG_HEADS):
            scr[h] = dsts[h]
        dng_ref[0:1, :] += dng
        if lbs:
            dlb_ref[0:1, :] += dlbs[0]
            dlb_ref[1:2, :] += dlbs[1]

    w = HG_HEADS * HG_DK
    rev = lambda ci: nc - 1 - ci
    col = lambda k: pl.BlockSpec((c, w), lambda ci: (rev(ci), col0 + k))
    out_col = pl.BlockSpec((c, w), lambda ci: (rev(ci), 0))
    in_specs = [col(0), col(1), col(2), col(3),
                pl.BlockSpec((HG_HEADS, None, HG_DK, HG_DK), lambda ci: (0, rev(ci), 0, 0)), out_col]
    args = [proj, proj, proj, proj, states, dy]
    if lb is not None:
        in_specs.append(pl.BlockSpec((2, w), lambda ci: (0, 0)))
        args.append(lb)
    in_specs += [pl.BlockSpec((1, HG_DK), lambda ci: (0, 0)), pl.BlockSpec(mall.shape, lambda ci: (0, 0))]
    args += [ng, mall]
    return _call(
        body, args, grid=(nc,), in_specs=in_specs,
        out_specs=[out_col, out_col, out_col, out_col,
                   pl.BlockSpec((2, w), lambda ci: (0, 0)), pl.BlockSpec((8, HG_DK), lambda ci: (0, 0))],
        out_shape=[_sds((s, w), BF16)] * 4 + [_sds((2, w), F32), _sds((8, HG_DK), F32)],
        scratch_shapes=[pltpu.VMEM((HG_HEADS, HG_DK, HG_DK), F32)],
        semantics=("arbitrary",), name=name, comm=comm)


def _head_avg():
    w = SB_HEADS * SB_DH
    i = np.arange(w)
    return jnp.asarray(((i[:, None] // SB_DH) == (i[None, :] // SB_DH)).astype(np.float32) / SB_DH, dtype=BF16)


def _sb_norm(x, g_tiled, avg):
    ms = _xr(x * x, avg)
    return x * lax.rsqrt(ms + EPS) * g_tiled


def _sb_prep(proj, gq, gk, name):
    s = proj.shape[0]
    t = min(ROW_T, s)
    w = SB_HEADS * SB_DH
    avg = _head_avg()

    def body(q_ref, k_ref, v_ref, gq_ref, gk_ref, avg_ref, qn_ref, kn_ref, vb_ref):
        qn_ref[...] = _sb_norm(q_ref[...], gq_ref[...], avg_ref[...]).astype(BF16)
        kn_ref[...] = _sb_norm(k_ref[...], gk_ref[...], avg_ref[...]).astype(BF16)
        vb_ref[...] = v_ref[...].astype(BF16)

    col = lambda k: pl.BlockSpec((t, w), lambda i: (i, 6 + k))
    vec = pl.BlockSpec((1, w), lambda i: (0, 0))
    out = pl.BlockSpec((t, w), lambda i: (i, 0))
    return pl.pallas_call(
        body, grid=(s // t,), in_specs=[col(0), col(1), col(2), vec, vec, pl.BlockSpec((w, w), lambda i: (0, 0))],
        out_specs=[out, out, out], out_shape=[_sds((s, w), BF16)] * 3,
        compiler_params=_params("parallel"), name=name)(proj, proj, proj, gq, gk, avg)


def _sb_prep_bwd(proj, dqn, dkn, gq, gk, name):
    s = proj.shape[0]
    t = min(ROW_T, s)
    w = SB_HEADS * SB_DH
    avg = _head_avg()

    def body(q_ref, k_ref, dqn_ref, dkn_ref, gq_ref, gk_ref, avg_ref, dq_ref, dk_ref, sg_ref):
        i = pl.program_id(0)

        @pl.when(i == 0)
        def _():
            sg_ref[...] = jnp.zeros_like(sg_ref)

        avg_v = avg_ref[...]
        fn = lambda x, g: _sb_norm(x, g, avg_v)
        _, vq = jax.vjp(fn, q_ref[...], gq_ref[...])
        dq, dgq = vq(dqn_ref[...])
        _, vk = jax.vjp(fn, k_ref[...], gk_ref[...])
        dk, dgk = vk(dkn_ref[...])
        dq_ref[...] = dq.astype(BF16)
        dk_ref[...] = dk.astype(BF16)
        sg_ref[0:1, :] += dgq
        sg_ref[1:2, :] += dgk

    col = lambda k: pl.BlockSpec((t, w), lambda i: (i, 6 + k))
    vec = pl.BlockSpec((1, w), lambda i: (0, 0))
    row = pl.BlockSpec((t, w), lambda i: (i, 0))
    return pl.pallas_call(
        body, grid=(s // t,),
        in_specs=[col(0), col(1), row, row, vec, vec, pl.BlockSpec((w, w), lambda i: (0, 0))],
        out_specs=[row, row, pl.BlockSpec((8, w), lambda i: (0, 0))],
        out_shape=[_sds((s, w), BF16), _sds((s, w), BF16), _sds((8, w), F32)],
        compiler_params=_params("arbitrary"), name=name)(proj, proj, dqn, dkn, gq, gk, avg)


def _sb_tri(kind):
    j = np.arange(SB_BLK)[:, None]
    s = np.arange(SB_BLK)[None, :]
    tri = (j > s) if kind == "suffix" else (j < s)
    return jnp.asarray(np.concatenate([tri, np.ones_like(tri)], axis=1).astype(np.float32), dtype=BF16)


def _sb_scores(qm, kblk, mask):
    z = _dot(qm, kblk, NT) * (SB_DH ** -0.5)
    sp = jnp.maximum(z, 0.0) + jnp.log(1.0 + jnp.exp(-jnp.abs(z)))
    return z, sp, jnp.where(mask, -sp, 0.0)


def _sb_setup(b):
    lane = lax.broadcasted_iota(jnp.int32, (2 * b, b), 1)
    row = lax.broadcasted_iota(jnp.int32, (2 * b, b), 0)
    mine = (row >> (b.bit_length() - 1)) == (lane >> (SB_DH.bit_length() - 1))
    return lane, row & (b - 1), mine


def _sb_fwd(qn, kn, vb, name, comm=None):
    s, w = qn.shape
    b = SB_BLK
    nq = s // b
    tri = _sb_tri("suffix")

    def body(q_ref, k_ref, v_ref, tri_ref, o_ref):
        i = pl.program_id(1)
        lane, tt, mine = _sb_setup(b)
        tri_v = tri_ref[...]
        pairs = [slice(u * b, (u + 1) * b) for u in range(SB_PAIRS)]
        qms = []
        for ls in pairs:
            q = q_ref[:, ls]
            q2 = jnp.concatenate([q, q], axis=0)
            qms.append(jnp.where(mine, q2, jnp.zeros_like(q2)))

        offs = [pl.multiple_of(jnp.maximum(i - j, 0) * b, b) for j in range(SB_FIXED)]
        masks = [lane < (tt if j == 0 else jnp.where(i >= j, b, 0)) for j in range(SB_FIXED)]
        scores = [[_sb_scores(qm, k_ref[pl.ds(off, b), ls], m) for off, m in zip(offs, masks)]
                  for qm, ls in zip(qms, pairs)]
        boths = [[_xdot_right(lk, tri_v) for _, _, lk in per_pair] for per_pair in scores]
        for u, ls in enumerate(pairs):
            qm = qms[u]
            run = acc = jnp.zeros((2 * b, b), F32)
            for j in range(SB_FIXED):
                z, sp, _ = scores[u][j]
                a = jnp.where(masks[j], jnp.exp(z - sp + boths[u][j][:, :b] + run), 0.0)
                acc = acc + _dot(a.astype(BF16), v_ref[pl.ds(offs[j], b), ls], NN)
                run = run + boths[u][j][:, b:]

            def cond(carry):
                j, run_, _ = carry
                return (j <= i) & (jnp.max(run_) > SB_DEAD)

            def step(carry, qm=qm, ls=ls):
                j, run_, acc_ = carry
                off = pl.multiple_of((i - j) * b, b)
                z, sp, lk = _sb_scores(qm, k_ref[pl.ds(off, b), ls], lane < b)
                both = _xdot_right(lk, tri_v)
                a = jnp.exp(z - sp + both[:, :b] + run_)
                return j + 1, run_ + both[:, b:], acc_ + _dot(a.astype(BF16), v_ref[pl.ds(off, b), ls], NN)

            _, _, acc = lax.while_loop(cond, step, (jnp.int32(SB_FIXED), run, acc))
            o_ref[:, ls] = jnp.where(lane[:b] < SB_DH, acc[:b], acc[b:]).astype(BF16)

    wide = SB_PAIRS * b
    blk = pl.BlockSpec((b, wide), lambda p, i: (i, p))
    full = pl.BlockSpec((s, wide), lambda p, i: (0, p))
    return _call(
        body, (qn, kn, vb, tri), grid=(w // wide, nq),
        in_specs=[blk, full, full, pl.BlockSpec(tri.shape, lambda p, i: (0, 0))],
        out_specs=[blk], out_shape=[_sds((s, w), BF16)],
        semantics=("parallel", "arbitrary"), name=name, comm=comm)


def _sb_bwd(qn, kn, vb, do, name, comm=None):
    s, w = qn.shape
    b = SB_BLK
    nq = s // b
    tri_s = _sb_tri("suffix")
    tri_p = _sb_tri("prefix")
    scale = SB_DH ** -0.5

    def body(q_ref, k_ref, v_ref, do_ref, ts_ref, tp_ref, dq_ref, dk_ref, dv_ref, dk_acc, dv_acc, dp_scr):
        i = pl.program_id(1)

        @pl.when(i == 0)
        def _():
            dk_acc[...] = jnp.zeros_like(dk_acc)
            dv_acc[...] = jnp.zeros_like(dv_acc)

        lane, tt, mine = _sb_setup(b)
        ts_v = ts_ref[...]
        tp_v = tp_ref[...]
        zero = jnp.zeros((2 * b, b), F32)
        pairs = [slice(u * b, (u + 1) * b) for u in range(SB_PAIRS)]
        qms, doms = [], []
        for ls in pairs:
            q = q_ref[:, ls]
            q2 = jnp.concatenate([q, q], axis=0)
            qms.append(jnp.where(mine, q2, jnp.zeros_like(q2)))
            dout = do_ref[:, ls].astype(BF16)
            d2 = jnp.concatenate([dout, dout], axis=0)
            doms.append(jnp.where(mine, d2, jnp.zeros_like(d2)))

        def down(u, kb, run):
            ls, qm, dom = pairs[u], qms[u], doms[u]
            off = pl.multiple_of(kb * b, b)
            z, sp, lk = _sb_scores(qm, k_ref[pl.ds(off, b), ls], lane < b)
            both = _xdot_right(lk, ts_v)
            a = jnp.exp(z - sp + both[:, :b] + run)
            dv_acc[pl.ds(off, b), ls] += _dot(a.astype(BF16), dom, TN)
            return _dot(dom, v_ref[pl.ds(off, b), ls], NT) * a, run + both[:, b:]

        def up(u, kb, dp, pre, dq):
            ls, qm = pairs[u], qms[u]
            off = pl.multiple_of(kb * b, b)
            kblk = k_ref[pl.ds(off, b), ls]
            sig = jax.nn.sigmoid(_dot(qm, kblk, NT) * scale)
            both = _xdot_right(dp, tp_v)
            dz = ((dp * (1.0 - sig) - sig * (both[:, :b] + pre)) * scale).astype(BF16)
            dk_acc[pl.ds(off, b), ls] += _dot(dz, qm, TN)
            return pre + both[:, b:], dq + _dot(dz, kblk, NN)

        offs = [pl.multiple_of(jnp.maximum(i - j, 0) * b, b) for j in range(SB_FIXED)]
        masks = [lane < (tt if j == 0 else jnp.where(i >= j, b, 0)) for j in range(SB_FIXED)]
        kblks = [[k_ref[pl.ds(off, b), ls] for off in offs] for ls in pairs]
        scores = [[_sb_scores(qms[u], kblks[u][j], masks[j]) for j in range(SB_FIXED)] for u in range(SB_PAIRS)]
        das = [[_dot(doms[u], v_ref[pl.ds(off, b), pairs[u]], NT) for off in offs] for u in range(SB_PAIRS)]
        boths = [[_xdot_right(lk, ts_v) for _, _, lk in scores[u]] for u in range(SB_PAIRS)]
        runs, dps = [], []
        for u in range(SB_PAIRS):
            run = zero
            mine_dps = []
            for j in range(SB_FIXED):
                z, sp, _ = scores[u][j]
                a = jnp.where(masks[j], jnp.exp(z - sp + boths[u][j][:, :b] + run), 0.0)
                mine_dps.append(das[u][j] * a)
                dv_acc[pl.ds(offs[j], b), pairs[u]] += _dot(a.astype(BF16), doms[u], TN)
                run = run + boths[u][j][:, b:]
            runs.append(run)
            dps.append(mine_dps)

        carries = []
        for u in range(SB_PAIRS):

            def cond(carry):
                j, run_ = carry
                return (j <= i) & (jnp.max(run_) > SB_DEAD)

            def sweep_down(carry, u=u):
                j, run_ = carry
                dp, run_ = down(u, i - j, run_)
                dp_scr[i - j] = dp
                return j + 1, run_

            n_live, _ = lax.while_loop(cond, sweep_down, (jnp.int32(SB_FIXED), runs[u]))

            def sweep_up(jj, carry, u=u, n_live=n_live):
                kb = i - n_live + 1 + jj
                return up(u, kb, dp_scr[kb], *carry)

            carries.append(lax.fori_loop(0, n_live - SB_FIXED, sweep_up, (zero, zero)))

        pres = [[_xdot_right(dp, tp_v) for dp in dps[u]] for u in range(SB_PAIRS)]
        for u, ls in enumerate(pairs):
            pre, dq = carries[u]
            for j in reversed(range(SB_FIXED)):
                z, sp, _ = scores[u][j]
                sig = jnp.exp(z - sp)
                dz = jnp.where(masks[j], dps[u][j] * (1.0 - sig) - sig * (pres[u][j][:, :b] + pre), 0.0) * scale
                dz = dz.astype(BF16)
                dk_acc[pl.ds(offs[j], b), ls] += _dot(dz, qms[u], TN)
                dq = dq + _dot(dz, kblks[u][j], NN)
                pre = pre + pres[u][j][:, b:]
            dq_ref[:, ls] = jnp.where(lane[:b] < SB_DH, dq[:b], dq[b:])

        @pl.when(i == nq - 1)
        def _():
            dk_ref[...] = dk_acc[...]
            dv_ref[...] = dv_acc[...].astype(BF16)

    wide = SB_PAIRS * b
    blk = pl.BlockSpec((b, wide), lambda p, i: (i, p))
    full = pl.BlockSpec((s, wide), lambda p, i: (0, p))
    tri = pl.BlockSpec(tri_s.shape, lambda p, i: (0, 0))
    return _call(
        body, (qn, kn, vb, do, tri_s, tri_p), grid=(w // wide, nq), in_specs=[blk, full, full, blk, tri, tri],
        out_specs=[blk, full, full], out_shape=[_sds((s, w), F32), _sds((s, w), F32), _sds((s, w), BF16)],
        scratch_shapes=[pltpu.VMEM((s, wide), F32), pltpu.VMEM((s, wide), F32), pltpu.VMEM((nq, 2 * b, b), F32)],
        semantics=("arbitrary", "arbitrary"), name=name, comm=comm)


MIX_T = 256
HALF = 512


def _gate_slices(ga, gb):
    return [(ga[:, 0:512], ga[:, 512:1024]), (ga[:, 1024:1536], gb[:, 0:512]), (gb[:, 512:1024], gb[:, 1024:1536])]


def _mix_fwd(u3, oh, osb, proj, x, pv, wc, wh, ws, wo, name):
    s, d = x.shape
    t = min(MIX_T, s)

    def body(u3_ref, oh_ref, os_ref, ga_ref, gb_ref, x_ref, pv_ref, wc_ref, wh_ref, ws_ref, wo_ref,
             x1_ref, h2_ref, mg_ref, mo_ref):
        ys = [_dot(u3_ref[...], wc_ref[...], NT), _dot(oh_ref[...], wh_ref[...], NT), _dot(os_ref[...], ws_ref[...], NT)]
        gl = _gate_slices(ga_ref[...], gb_ref[...])
        halves = []
        for hf in range(2):
            lo = hf * HALF
            acc = jnp.zeros((t, HALF), F32)
            for br in range(3):
                gate = jax.nn.sigmoid(gl[br][hf] + pv_ref[8 + br:9 + br, lo:lo + HALF])
                acc = acc + gate * ys[br][:, lo:lo + HALF]
            halves.append(acc)
        merged = jnp.concatenate(halves, axis=1).astype(BF16)
        mg_ref[...] = merged
        mo = _dot(merged, wo_ref[...], NN)
        mo_ref[...] = mo.astype(BF16)
        x1 = x_ref[...] + pv_ref[2:3, :] * mo
        x1_ref[...] = x1
        h2_ref[...] = _norm_mod(x1, pv_ref[7:8, :], pv_ref[4:5, :], pv_ref[3:4, :]).astype(BF16)

    br_spec = pl.BlockSpec((t, CONV_CH), lambda i: (i, 0))
    row = pl.BlockSpec((t, d), lambda i: (i, 0))
    wproj = pl.BlockSpec((d, CONV_CH), lambda i: (0, 0))
    return pl.pallas_call(
        body, grid=(s // t,),
        in_specs=[br_spec, br_spec, br_spec, pl.BlockSpec((t, 1536), lambda i: (i, 3)),
                  pl.BlockSpec((t, 1536), lambda i: (i, 4)), row, pl.BlockSpec((16, d), lambda i: (0, 0)),
                  wproj, wproj, wproj, pl.BlockSpec((d, d), lambda i: (0, 0))],
        out_specs=[row, row, row, row],
        out_shape=[_sds((s, d), F32), _sds((s, d), BF16), _sds((s, d), BF16), _sds((s, d), BF16)],
        compiler_params=_params("parallel"), name=name)(u3, oh, osb, proj, proj, x, pv, wc, wh, ws, wo)


def _mix_bwd(dx1, mo1, u3, oh, osb, proj, pv, wc, wh, ws, wo, name):
    s, d = dx1.shape
    t = min(MIX_T, s)

    def body(dx_ref, mo_ref, u3_ref, oh_ref, os_ref, ga_ref, gb_ref, pv_ref, wc_ref, wh_ref, ws_ref, wo_ref,
             dmo_ref, dyc_ref, dyh_ref, dys_ref, doc_ref, doh_ref, dos_ref, dgl_ref, sg_ref):
        i = pl.program_id(0)

        @pl.when(i == 0)
        def _():
            sg_ref[...] = jnp.zeros_like(sg_ref)

        dx = dx_ref[...]
        dmo = (dx * pv_ref[2:3, :]).astype(BF16)
        dmo_ref[...] = dmo
        sg_ref[0:1, :] += jnp.sum(dx * mo_ref[...].astype(F32), axis=0, keepdims=True)
        dmerged = _dot(dmo, wo_ref[...], NT)
        branches = [(u3_ref, wc_ref, dyc_ref, doc_ref), (oh_ref, wh_ref, dyh_ref, doh_ref), (os_ref, ws_ref, dys_ref, dos_ref)]
        gl = _gate_slices(ga_ref[...], gb_ref[...])
        for br, (o_ref, w_ref, dy_ref, do_ref) in enumerate(branches):
            y = _dot(o_ref[...], w_ref[...], NT)
            dys = []
            for hf in range(2):
                lo = hf * HALF
                gate = jax.nn.sigmoid(gl[br][hf] + pv_ref[8 + br:9 + br, lo:lo + HALF])
                dm = dmerged[:, lo:lo + HALF]
                dys.append(dm * gate)
                dgl = dm * y[:, lo:lo + HALF] * gate * (1.0 - gate)
                dgl_ref[:, br * d + lo: br * d + lo + HALF] = dgl.astype(BF16)
                sg_ref[1 + br:2 + br, lo:lo + HALF] += jnp.sum(dgl, axis=0, keepdims=True)
            dy = jnp.concatenate(dys, axis=1).astype(BF16)
            dy_ref[...] = dy
            do_ref[...] = _dot(dy, w_ref[...], NN)

    br_spec = pl.BlockSpec((t, CONV_CH), lambda i: (i, 0))
    row = pl.BlockSpec((t, d), lambda i: (i, 0))
    wproj = pl.BlockSpec((d, CONV_CH), lambda i: (0, 0))
    return pl.pallas_call(
        body, grid=(s // t,),
        in_specs=[row, row, br_spec, br_spec, br_spec, pl.BlockSpec((t, 1536), lambda i: (i, 3)),
                  pl.BlockSpec((t, 1536), lambda i: (i, 4)), pl.BlockSpec((16, d), lambda i: (0, 0)),
                  wproj, wproj, wproj, pl.BlockSpec((d, d), lambda i: (0, 0))],
        out_specs=[row, row, row, row, br_spec, br_spec, br_spec, pl.BlockSpec((t, 3 * d), lambda i: (i, 0)),
                   pl.BlockSpec((8, d), lambda i: (0, 0))],
        out_shape=[_sds((s, d), BF16)] * 4 + [_sds((s, CONV_CH), F32)] * 3 + [_sds((s, 3 * d), BF16), _sds((8, d), F32)],
        compiler_params=_params("arbitrary"), name=name)(dx1, mo1, u3, oh, osb, proj, proj, pv, wc, wh, ws, wo)


MLP_T = 512
MLP_F = 1024


def _mlp_fwd(h2, x1, pv, w1t, w2, name, comm=None):
    s, d = x1.shape
    t = min(MLP_T, s)
    nf = D_FF // MLP_F

    def body(h_ref, x_ref, pv_ref, w1_ref, w2_ref, x2_ref, mo_ref, acc_ref):
        f = pl.program_id(1)

        @pl.when(f == 0)
        def _():
            acc_ref[...] = jnp.zeros_like(acc_ref)

        a = jnp.maximum(_dot(h_ref[...], w1_ref[...], NT), 0.0)
        acc_ref[...] += _dot((a * a).astype(BF16), w2_ref[...], NN)

        @pl.when(f == nf - 1)
        def _():
            mo = acc_ref[...]
            mo_ref[...] = mo.astype(BF16)
            x2_ref[...] = x_ref[...] + pv_ref[5:6, :] * mo

    row = pl.BlockSpec((t, d), lambda i, f: (i, 0))
    wblk = pl.BlockSpec((MLP_F, d), lambda i, f: (f, 0))
    return _call(
        body, (h2, x1, pv, w1t, w2), grid=(s // t, nf),
        in_specs=[row, row, pl.BlockSpec((16, d), lambda i, f: (0, 0)), wblk, wblk],
        out_specs=[row, row], out_shape=[_sds((s, d), F32), _sds((s, d), BF16)],
        scratch_shapes=[pltpu.VMEM((t, d), F32)],
        semantics=("parallel", "arbitrary"), name=name, comm=comm)


def _mlp_bwd(dx2, h2, x1, mo2, pv, w1t, w2, name, comm=None):
    s, d = x1.shape
    t = min(MLP_T, s)
    nf = D_FF // MLP_F

    def body(dx_ref, h_ref, x_ref, mo_ref, pv_ref, w1_ref, w2_ref, dx1_ref, da_ref, b_ref, dmo_ref, sg_ref, acc_ref):
        i = pl.program_id(0)
        f = pl.program_id(1)

        @pl.when((i == 0) & (f == 0))
        def _():
            sg_ref[...] = jnp.zeros_like(sg_ref)

        @pl.when(f == 0)
        def _():
            acc_ref[...] = jnp.zeros_like(acc_ref)
            dx = dx_ref[...]
            dmo_ref[...] = (dx * pv_ref[5:6, :]).astype(BF16)
            sg_ref[0:1, :] += jnp.sum(dx * mo_ref[...].astype(F32), axis=0, keepdims=True)

        r = jnp.maximum(_dot(h_ref[...], w1_ref[...], NT), 0.0)
        b_ref[...] = (r * r).astype(BF16)
        da = (_dot(dmo_ref[...], w2_ref[...], NT) * (2.0 * r)).astype(BF16)
        da_ref[...] = da
        acc_ref[...] += _dot(da, w1_ref[...], NN)

        @pl.when(f == nf - 1)
        def _():
            _, vjp = jax.vjp(_norm_mod, x_ref[...], pv_ref[7:8, :], pv_ref[4:5, :], pv_ref[3:4, :])
            dxn, dg, dsc, dsh = vjp(acc_ref[...])
            dx1_ref[...] = dx_ref[...] + dxn
            sg_ref[1:2, :] += dsh
            sg_ref[2:3, :] += dsc
            sg_ref[3:4, :] += dg

    row = pl.BlockSpec((t, d), lambda i, f: (i, 0))
    wblk = pl.BlockSpec((MLP_F, d), lambda i, f: (f, 0))
    hid = pl.BlockSpec((t, MLP_F), lambda i, f: (i, f))
    return _call(
        body, (dx2, h2, x1, mo2, pv, w1t, w2), grid=(s // t, nf),
        in_specs=[row, row, row, row, pl.BlockSpec((16, d), lambda i, f: (0, 0)), wblk, wblk],
        out_specs=[row, hid, hid, row, pl.BlockSpec((8, d), lambda i, f: (0, 0))],
        out_shape=[_sds((s, d), F32), _sds((s, D_FF), BF16), _sds((s, D_FF), BF16), _sds((s, d), BF16), _sds((8, d), F32)],
        scratch_shapes=[pltpu.VMEM((t, d), F32)],
        semantics=("arbitrary", "arbitrary"), name=name, comm=comm)


def _loss_head(y, target, name):
    s, d = y.shape
    t = min(ROW_T, s)

    def body(y_ref, t_ref, dy_ref, ls_ref):
        i = pl.program_id(0)

        @pl.when(i == 0)
        def _():
            ls_ref[...] = jnp.zeros_like(ls_ref)

        e = y_ref[...] - t_ref[...]
        dy_ref[...] = e * (1.0 / d)
        ls_ref[...] += jnp.sum((e * e).reshape(t // 8, 8, d), axis=0)

    row = pl.BlockSpec((t, d), lambda i: (i, 0))
    return pl.pallas_call(
        body, grid=(s // t,), in_specs=[row, row], out_specs=[row, pl.BlockSpec((8, d), lambda i: (0, 0))],
        out_shape=[_sds((s, d), F32), _sds((8, d), F32)],
        compiler_params=_params("arbitrary"), name=name)(y, target)


def _layer_vectors(l, mod, sm):
    d = D_MODEL
    pv = jnp.concatenate([mod[l].reshape(6, d), sm["norm1_g"][l][None], sm["norm2_g"][l][None],
                          sm["gate_b"][l].reshape(3, d), jnp.zeros((5, d), F32)], axis=0)
    cp = jnp.concatenate([sm["conv_b"][l][None], sm["conv_ln_g"][l][None], sm["conv_ln_b"][l][None],
                          jnp.zeros((5, CONV_CH), F32)], axis=0)
    return dict(pv=pv, cp=cp, conv_w=sm["conv_w"][l], lb=(sm["hgrn_lb"] if l > 0 else None),
                ng=sm["hgrn_norm_g"][l][None], gq=jnp.tile(sm["sb_qn_g"][l], SB_HEADS)[None],
                gk=jnp.tile(sm["sb_kn_g"][l], SB_HEADS)[None])


def _hosted(res, comm):
    return res if comm is not None else (res, None)


def _layer_fwd_mixers(x, vec, win_t, tag, comm_proj=None, comm_hgrn=None, comm_sb=None):
    h = _prenorm(x, vec["pv"], f"prenorm{tag}")
    proj, got_proj = _hosted(_matmul(h, win_t, "nt", F32, 1024, 768, 1024, f"proj{tag}", comm_proj), comm_proj)
    u3 = _conv_fwd(proj, vec["conv_w"], vec["cp"], f"conv_fwd{tag}")
    (oh, states), got_hgrn = _hosted(_hgrn_fwd(proj, vec["lb"], vec["ng"], f"hgrn_fwd{tag}", comm_hgrn), comm_hgrn)
    qn, kn, vb = _sb_prep(proj, vec["gq"], vec["gk"], f"sb_prep{tag}")
    (osb,), got_sb = _hosted(_sb_fwd(qn, kn, vb, f"sb_fwd{tag}", comm_sb), comm_sb)
    saved = dict(x=x, h=h, proj=proj, u3=u3, oh=oh, states=states, qn=qn, kn=kn, vb=vb, osb=osb)
    return saved, (got_proj, got_hgrn, got_sb)


def _layer_fwd_out(sv, vec, w, tag, comm_mlp=None):
    x1, h2, merged, mo1 = _mix_fwd(sv["u3"], sv["oh"], sv["osb"], sv["proj"], sv["x"], vec["pv"],
                                   w["wc_t"], w["wh_t"], w["ws_t"], w["wo"], f"mix_fwd{tag}")
    (x2, mo2), got = _hosted(_mlp_fwd(h2, x1, vec["pv"], w["w1_t"], w["w2"], f"mlp_fwd{tag}", comm_mlp), comm_mlp)
    sv.update(x1=x1, h2=h2, merged=merged, mo1=mo1, mo2=mo2)
    return x2, got


def _layer_bwd(dx2, sv, vec, w, tag, plans=None):
    plans = plans or {}
    got = {}

    def plan_for(key, big_now):
        return plans[key](big_now) if key in plans else None

    pv = vec["pv"]
    big = {}
    comm = plan_for("mlp", big)
    (dx1, da, bsq, dmo2, sg_mlp), got["mlp"] = _hosted(
        _mlp_bwd(dx2, sv["h2"], sv["x1"], sv["mo2"], pv, w["w1_t"], w["w2"], f"mlp_bwd{tag}", comm), comm)
    big["w1_t"] = _matmul(da, sv["h2"], "tn", BF16, 1024, 1024, 1024, f"dw1{tag}")
    big["w2"] = _matmul(bsq, dmo2, "tn", BF16, 1024, 1024, 1024, f"dw2{tag}")
    dmo1, dyc, dyh, dys, doc, doh, dos, dgl, sg_mix = _mix_bwd(
        dx1, sv["mo1"], sv["u3"], sv["oh"], sv["osb"], sv["proj"], pv, w["wc_t"], w["wh_t"], w["ws_t"], w["wo"], f"mix_bwd{tag}")
    big["wo"] = _matmul(sv["merged"], dmo1, "tn", BF16, 1024, 1024, 1024, f"dwo{tag}")
    big["wc_t"] = _matmul(dyc, sv["u3"], "tn", BF16, 1024, 512, 1024, f"dwc{tag}")
    big["wh_t"] = _matmul(dyh, sv["oh"], "tn", BF16, 1024, 512, 1024, f"dwh{tag}")
    big["ws_t"] = _matmul(dys, sv["osb"], "tn", BF16, 1024, 512, 1024, f"dws{tag}")
    comm = plan_for("conv", big)
    (da_c, dg_c, dconv_w, sg_conv), got["conv"] = _hosted(
        _conv_bwd(sv["proj"], doc, vec["conv_w"], vec["cp"], f"conv_bwd{tag}", comm), comm)
    comm = plan_for("hgrn", big)
    (dq_h, df_h, di_h, dg_h, dlb, dng), got["hgrn"] = _hosted(
        _hgrn_bwd(sv["proj"], sv["states"], doh, vec["lb"], vec["ng"], f"hgrn_bwd{tag}", comm), comm)
    comm = plan_for("sb", big)
    (dqn, dkn, dv_s), got["sb"] = _hosted(_sb_bwd(sv["qn"], sv["kn"], sv["vb"], dos, f"sb_bwd{tag}", comm), comm)
    dq_s, dk_s, sg_sb = _sb_prep_bwd(sv["proj"], dqn, dkn, vec["gq"], vec["gk"], f"sb_prep_bwd{tag}")
    dproj = jnp.concatenate([da_c, dg_c, dq_h, df_h, di_h, dg_h, dq_s, dk_s, dv_s, dgl], axis=1)
    half = D_MODEL // 2
    comm = plan_for("dwin_a", big)
    big["win_a"], got["dwin_a"] = _hosted(
        _matmul(dproj, sv["h"][:, :half], "tn", BF16, 768, half, 1024, f"dwin_a{tag}", comm), comm)
    comm = plan_for("dwin", big)
    big["win_b"], got["dwin"] = _hosted(
        _matmul(dproj, sv["h"][:, half:], "tn", BF16, 768, half, 1024, f"dwin_b{tag}", comm), comm)
    comm = plan_for("dh", big)
    (dx, sg_pre), got["dh"] = _hosted(_dh_prenorm_bwd(dproj, w["win_t"], dx1, sv["x"], pv, f"dh{tag}", comm), comm)
    small = dict(
        mod=jnp.stack([sg_pre[0], sg_pre[1], sg_mix[0], sg_mlp[1], sg_mlp[2], sg_mlp[0]]).reshape(6 * D_MODEL),
        norm1_g=sg_pre[2], norm2_g=sg_mlp[3], gate_b=sg_mix[1:4].reshape(3 * D_MODEL),
        conv_w=dconv_w, conv_b=sg_conv[0], conv_ln_g=sg_conv[1], conv_ln_b=sg_conv[2],
        hgrn_lb=dlb, hgrn_norm_g=dng[0],
        sb_qn_g=sg_sb[0].reshape(SB_HEADS, SB_DH).sum(0), sb_kn_g=sg_sb[1].reshape(SB_HEADS, SB_DH).sum(0))
    return dx, big, small, got


def _row_tile(r, cap=512):
    t = min(r, cap)
    while r % t or (t % 8 and t != r):
        t -= 1
    return t


def _sum8(z, name):
    _, r, c = z.shape
    t = _row_tile(r, 128 if c >= 1024 else 512)

    def body(z_ref, o_ref):
        acc = z_ref[0].astype(F32)
        for j in range(1, N_DEV):
            acc = acc + z_ref[j].astype(F32)
        o_ref[...] = acc

    return pl.pallas_call(
        body, grid=(r // t,), in_specs=[pl.BlockSpec((N_DEV, t, c), lambda i: (0, i, 0))],
        out_specs=pl.BlockSpec((t, c), lambda i: (i, 0)), out_shape=_sds((r, c), F32),
        compiler_params=_params("parallel"), name=name)(z)


def _adamw(w, g, m, v, name):
    r, c = w.shape
    t = _row_tile(r, 256)

    def body(w_ref, g_ref, m_ref, v_ref, d_ref, nm_ref, nv_ref):
        g_ = g_ref[...]
        nm = ADAM_B1 * m_ref[...] + (1.0 - ADAM_B1) * g_
        nv = ADAM_B2 * v_ref[...] + (1.0 - ADAM_B2) * jnp.square(g_)
        m_hat = nm / (1.0 - ADAM_B1 ** ADAM_STEP)
        v_hat = nv / (1.0 - ADAM_B2 ** ADAM_STEP)
        d_ref[...] = -ADAM_LR * (m_hat / (jnp.sqrt(v_hat) + ADAM_EPS) + ADAM_WD * w_ref[...])
        nm_ref[...] = nm
        nv_ref[...] = nv

    blk = pl.BlockSpec((t, c), lambda i: (i, 0))
    return pl.pallas_call(
        body, grid=(r // t,), in_specs=[blk] * 4, out_specs=[blk] * 3, out_shape=[_sds((r, c), F32)] * 3,
        compiler_params=_params("parallel"), name=name)(w, g, m, v)


def _mod_local(c_all, mod_w, name):
    depth, d, cols = mod_w.shape

    def body(c_ref, w_ref, o_ref):
        cv = c_ref[...]
        act = cv * jax.nn.sigmoid(cv)
        o_ref[...] = jnp.dot(act, w_ref[...], precision=lax.Precision.HIGHEST, preferred_element_type=F32)

    return pl.pallas_call(
        body, grid=(depth,),
        in_specs=[pl.BlockSpec((N_DEV, d), lambda l: (0, 0)), pl.BlockSpec((None, d, cols), lambda l: (l, 0, 0))],
        out_specs=pl.BlockSpec((None, N_DEV, cols), lambda l: (l, 0, 0)), out_shape=_sds((depth, N_DEV, cols), F32),
        compiler_params=_params("parallel"), name=name)(c_all, mod_w)


def _modw_grad(c_all, dmod, name):
    depth, _, cols = dmod.shape
    d = c_all.shape[1]

    def body(c_ref, g_ref, o_ref):
        cv = c_ref[...]
        act = cv * jax.nn.sigmoid(cv)
        o_ref[...] = lax.dot_general(act, g_ref[...], (TN, ((), ())), precision=lax.Precision.HIGHEST,
                                     preferred_element_type=F32)

    return pl.pallas_call(
        body, grid=(depth,),
        in_specs=[pl.BlockSpec((N_DEV, d), lambda l: (0, 0)), pl.BlockSpec((None, N_DEV, cols), lambda l: (l, 0, 0))],
        out_specs=pl.BlockSpec((None, d, cols), lambda l: (l, 0, 0)), out_shape=_sds((depth, d, cols), F32),
        compiler_params=_params("parallel"), name=name)(c_all, dmod)


LANE = 128
BIG = {"w_in": ("win_t", True), "w_out": ("wo", False), "mlp_w2": ("w2", False), "mlp_w1": ("w1_t", True),
       "w_conv_proj": ("wc_t", True), "w_hgrn_proj": ("wh_t", True), "w_sb_proj": ("ws_t", True)}
PROJS = ("w_conv_proj", "w_hgrn_proj", "w_sb_proj")
SMALL = (("mod_b", 6144), ("norm1_g", 1024), ("gate_b", 3072), ("conv_w", CONV_WIDTH * CONV_CH), ("conv_b", 512),
         ("conv_ln_g", 512), ("conv_ln_b", 512), ("hgrn_lb", 512), ("hgrn_norm_g", 128), ("sb_qn_g", 64),
         ("sb_kn_g", 64), ("norm2_g", 1024))


def _pack_rows(parts, width):
    flat = jnp.concatenate([p.reshape(-1) for p in parts])
    rows = -(-flat.shape[0] // width)
    rows = -(-rows // 8) * 8
    return jnp.pad(flat, (0, rows * width - flat.shape[0])).reshape(rows, width)


def _shards(params, items):
    return [(params[n][l].T if BIG[n][1] else params[n][l]).astype(BF16) for n, l in items]


def _gathered(items, got):
    return {BIG[n][0]: g.reshape(-1, g.shape[2]) for (n, _), g in zip(items, got)}


def _by_shard(g):
    return g.reshape(N_DEV, g.shape[0] // N_DEV, g.shape[1])


def _adamw_nd(w, g, m, v, name):
    shape = w.shape
    two = lambda a: a.reshape(-1, shape[-1])
    return [o.reshape(shape) for o in _adamw(two(w), two(g), two(m), two(v), name)]


WEIGHTS = ("mod_w", "mod_b", "norm1_g", "w_in", "gate_b", "conv_w", "conv_b", "conv_ln_g", "conv_ln_b", "w_conv_proj",
           "hgrn_lb", "hgrn_norm_g", "w_hgrn_proj", "sb_qn_g", "sb_kn_g", "w_sb_proj", "w_out", "norm2_g", "mlp_w1",
           "mlp_w2")


def kernel(x, c, mod_w, mod_b, norm1_g, w_in, gate_b, conv_w, conv_b, conv_ln_g, conv_ln_b, w_conv_proj, hgrn_lb, hgrn_norm_g, w_hgrn_proj, sb_qn_g, sb_kn_g, w_sb_proj, w_out, norm2_g, mlp_w1, mlp_w2, loss_target, m_mod_w, m_mod_b, m_norm1_g, m_w_in, m_gate_b, m_conv_w, m_conv_b, m_conv_ln_g, m_conv_ln_b, m_w_conv_proj, m_hgrn_lb, m_hgrn_norm_g, m_w_hgrn_proj, m_sb_qn_g, m_sb_kn_g, m_w_sb_proj, m_w_out, m_norm2_g, m_mlp_w1, m_mlp_w2, v_mod_w, v_mod_b, v_norm1_g, v_w_in, v_gate_b, v_conv_w, v_conv_b, v_conv_ln_g, v_conv_ln_b, v_w_conv_proj, v_hgrn_lb, v_hgrn_norm_g, v_w_hgrn_proj, v_sb_qn_g, v_sb_kn_g, v_w_sb_proj, v_w_out, v_norm2_g, v_mlp_w1, v_mlp_w2):
    params = dict(mod_w=mod_w, mod_b=mod_b, norm1_g=norm1_g, w_in=w_in, gate_b=gate_b, conv_w=conv_w, conv_b=conv_b,
                  conv_ln_g=conv_ln_g, conv_ln_b=conv_ln_b, w_conv_proj=w_conv_proj, hgrn_lb=hgrn_lb,
                  hgrn_norm_g=hgrn_norm_g, w_hgrn_proj=w_hgrn_proj, sb_qn_g=sb_qn_g, sb_kn_g=sb_kn_g,
                  w_sb_proj=w_sb_proj, w_out=w_out, norm2_g=norm2_g, mlp_w1=mlp_w1, mlp_w2=mlp_w2)
    mom1 = dict(mod_w=m_mod_w, mod_b=m_mod_b, norm1_g=m_norm1_g, w_in=m_w_in, gate_b=m_gate_b, conv_w=m_conv_w,
                conv_b=m_conv_b, conv_ln_g=m_conv_ln_g, conv_ln_b=m_conv_ln_b, w_conv_proj=m_w_conv_proj,
                hgrn_lb=m_hgrn_lb, hgrn_norm_g=m_hgrn_norm_g, w_hgrn_proj=m_w_hgrn_proj, sb_qn_g=m_sb_qn_g,
                sb_kn_g=m_sb_kn_g, w_sb_proj=m_w_sb_proj, w_out=m_w_out, norm2_g=m_norm2_g, mlp_w1=m_mlp_w1,
                mlp_w2=m_mlp_w2)
    mom2 = dict(mod_w=v_mod_w, mod_b=v_mod_b, norm1_g=v_norm1_g, w_in=v_w_in, gate_b=v_gate_b, conv_w=v_conv_w,
                conv_b=v_conv_b, conv_ln_g=v_conv_ln_g, conv_ln_b=v_conv_ln_b, w_conv_proj=v_w_conv_proj,
                hgrn_lb=v_hgrn_lb, hgrn_norm_g=v_hgrn_norm_g, w_hgrn_proj=v_w_hgrn_proj, sb_qn_g=v_sb_qn_g,
                sb_kn_g=v_sb_kn_g, w_sb_proj=v_w_sb_proj, w_out=v_w_out, norm2_g=v_norm2_g, mlp_w1=v_mlp_w1,
                mlp_w2=v_mlp_w2)
    xi, yi, ci = _mesh_place()
    me = _block_of(xi, yi, ci)
    cw_cols = conv_w.shape[2]

    tiny = _pack_rows([c, conv_w], LANE)
    g_tiny, g_win0 = _comm_alone(_GatherPlan([tiny] + _shards(params, [("w_in", 0)])), "gather_first")
    c_rows = D_MODEL // LANE
    c_all = g_tiny[:, :c_rows].reshape(N_DEV, D_MODEL)
    n_cw = DEPTH * CONV_WIDTH * cw_cols
    conv_w_full = g_tiny[:, c_rows:c_rows + n_cw // LANE].reshape(N_DEV, DEPTH, CONV_WIDTH, cw_cols)
    conv_w_full = conv_w_full.transpose(1, 2, 0, 3).reshape(DEPTH, CONV_WIDTH, CONV_CH)

    (g_mod,) = _comm_alone(_GatherPlan([_mod_local(c_all, mod_w, "mod_local")]), "gather_mod")
    mod = lax.dynamic_index_in_dim(g_mod, me, axis=2, keepdims=False)
    mod = mod.transpose(1, 0, 2).reshape(DEPTH, 6 * D_MODEL) + mod_b

    sm = dict(norm1_g=norm1_g, norm2_g=norm2_g, gate_b=gate_b, conv_w=conv_w_full, conv_b=conv_b, conv_ln_g=conv_ln_g,
              conv_ln_b=conv_ln_b, hgrn_lb=hgrn_lb, hgrn_norm_g=hgrn_norm_g, sb_qn_g=sb_qn_g, sb_kn_g=sb_kn_g)
    vecs = [_layer_vectors(l, mod, sm) for l in range(DEPTH)]

    fwd_hosts = dict(
        proj=[("w_in", 1)],
        hgrn=[("mlp_w2", 0), ("w_out", 0)] + [(n, 0) for n in PROJS],
        sb=[("mlp_w1", 0), ("mlp_w2", 1)],
        mlp=[("mlp_w1", 1), ("w_out", 1)] + [(n, 1) for n in PROJS])
    gather = {host: _GatherPlan(_shards(params, items)) for host, items in fwd_hosts.items()}
    wts = [_gathered([("w_in", 0)], [g_win0]), {}]
    sv0, got_mixers = _layer_fwd_mixers(x[0], vecs[0], wts[0]["win_t"], "_l0", gather["proj"], gather["hgrn"], gather["sb"])
    for host, got in zip(("proj", "hgrn", "sb"), got_mixers):
        for (name, l), g in zip(fwd_hosts[host], got):
            wts[l].update(_gathered([(name, l)], [g]))
    y, got = _layer_fwd_out(sv0, vecs[0], wts[0], "_l0", gather["mlp"])
    wts[1].update(_gathered(fwd_hosts["mlp"], got))
    sv1, _ = _layer_fwd_mixers(y, vecs[1], wts[1]["win_t"], "_l1")
    y, _ = _layer_fwd_out(sv1, vecs[1], wts[1], "_l1")
    dy, sq = _loss_head(y, loss_target[0], "loss_head")
    loss = lax.psum(0.5 * jnp.sum(sq) / D_MODEL, ("x", "y", "c"))

    own_hosts = dict(
        hgrn=[(1, "w1_t"), (1, "wo"), (1, "wc_t"), (1, "wh_t"), (1, "ws_t")],
        sb=[(1, "w2")])
    own_plans = {host: (lambda big, items=items: _ExchangePlan([_by_shard(big[k]) for _, k in items]))
                 for host, items in own_hosts.items()}
    dy, big1, small1, got1 = _layer_bwd(dy, sv1, vecs[1], wts[1], "_l1", own_plans)
    bwd_hosts = dict(
        mlp=[(1, "win_a")],
        conv=[(1, "win_b"), (0, "wc_t"), (0, "wh_t"), (0, "ws_t")],
        hgrn=[(0, "wo"), (0, "w2")],
        sb=[(0, "w1_t")],
        dwin=[(0, "win_a")],
        dh=[(0, "win_b")])
    plans = {host: (lambda big0, items=items: _ExchangePlan([_by_shard((big1 if l else big0)[k]) for l, k in items]))
             for host, items in bwd_hosts.items()}
    dx, _, small0, got = _layer_bwd(dy, sv0, vecs[0], wts[0], "_l0", plans)
    smalls = [small0, small1]
    summed = {}
    for hosts, arrived_by_host in ((own_hosts, got1), (bwd_hosts, got)):
        for host, items in hosts.items():
            for (l, key), arrived in zip(items, arrived_by_host[host]):
                summed[l, key] = _sum8(arrived, f"sum_{key}_l{l}")
    grads = {}
    for name, (key, transposed) in BIG.items():
        per_layer = []
        for l in range(DEPTH):
            if name == "w_in":
                blk = jnp.concatenate([summed[l, "win_a"], summed[l, "win_b"]], axis=1)
            else:
                blk = summed[l, key]
            per_layer.append(blk.T if transposed else blk)
        grads[name] = jnp.stack(per_layer)

    small_parts = []
    for name, _ in SMALL:
        key = "mod" if name == "mod_b" else name
        if name == "hgrn_lb":
            small_parts.append(smalls[0][key] + smalls[1][key])
        else:
            small_parts.append(jnp.stack([smalls[l][key] for l in range(DEPTH)]))
    (g_small,) = _comm_alone(_GatherPlan([_pack_rows(small_parts, LANE)]), "gather_small_grads")
    small_sum = _sum8(g_small, "sum_small_grads").reshape(-1)
    off = 0
    for name, per_layer in SMALL:
        grads[name] = small_sum[off:off + DEPTH * per_layer].reshape(params[name].shape if name != "conv_w" else (DEPTH, CONV_WIDTH, CONV_CH))
        off += DEPTH * per_layer
    grads["conv_w"] = lax.dynamic_slice_in_dim(grads["conv_w"], me * cw_cols, cw_cols, axis=2)
    cols = mod_w.shape[2]
    dmod_all = g_small.reshape(N_DEV, -1)[:, :DEPTH * 6 * D_MODEL].reshape(N_DEV, DEPTH, 6 * D_MODEL)
    dmod_mine = lax.dynamic_slice_in_dim(dmod_all, me * cols, cols, axis=2).transpose(1, 0, 2)
    grads["mod_w"] = _modw_grad(c_all, dmod_mine, "mod_w_grad")

    delta, new_m, new_v = {}, {}, {}
    small_names = [n for n, _ in SMALL]
    for name in WEIGHTS:
        if name not in small_names:
            delta[name], new_m[name], new_v[name] = _adamw_nd(params[name], grads[name], mom1[name], mom2[name], f"adamw_{name}")
    packed = [_pack_rows([d[n] for n in small_names], LANE) for d in (params, grads, mom1, mom2)]
    outs = [o.reshape(-1) for o in _adamw(*packed, "adamw_small")]
    off = 0
    for name in small_names:
        size = params[name].size
        for dst, o in zip((delta, new_m, new_v), outs):
            dst[name] = o[off:off + size].reshape(params[name].shape)
        off += size
    return (loss, dx[None], *[grads[n] for n in WEIGHTS], *[delta[n] for n in WEIGHTS],
            *[new_m[n] for n in WEIGHTS], *[new_v[n] for n in WEIGHTS])
```

```python
import functools

import jax
import jax.numpy as jnp
import numpy as np
from jax import lax
from jax.experimental import pallas as pl
from jax.experimental.pallas import tpu as pltpu

F32 = jnp.float32
BF16 = jnp.bfloat16

D_MODEL = 1024
DEPTH = 2
N_DEV = 8
CONV_CH = 512
CONV_WIDTH = 31
CONV_HALO = 32
HG_HEADS = 4
HG_DK = 128
SB_HEADS = 8
SB_DH = 64
D_IN = 7680
D_FF = 4096
EPS = 1e-6
SB_BLK = 128
SB_DEAD = -104.0
SB_FIXED = 3
SB_PAIRS = 4
HG_CHUNK = 128

ADAM_LR = 0.001
ADAM_B1 = 0.9
ADAM_B2 = 0.999
ADAM_EPS = 1e-08
ADAM_WD = 0.01
ADAM_STEP = 10

VMEM_LIMIT = 48 * 1024 * 1024

NN = ((1,), (0,))
NT = ((1,), (1,))
TN = ((0,), (0,))
_DIMS = {"nn": NN, "nt": NT, "tn": TN}


def _sds(shape, dtype):
    return jax.ShapeDtypeStruct(shape, dtype)


def _params(*semantics):
    return pltpu.CompilerParams(dimension_semantics=semantics, vmem_limit_bytes=VMEM_LIMIT)


def _dot(a, b, dims):
    return lax.dot_general(a, b, (dims, ((), ())), preferred_element_type=F32)


@functools.partial(jax.custom_vjp, nondiff_argnums=(2,))
def _bdot(a, b, mode):
    return _dot(a.astype(BF16), b.astype(BF16), _DIMS[mode])


def _bdot_fwd(a, b, mode):
    return _bdot(a, b, mode), (a.astype(BF16), b.astype(BF16))


def _bdot_bwd(mode, res, g):
    a, b = res
    g = g.astype(BF16)
    if mode == "nn":
        return _dot(g, b, NT), _dot(a, g, TN)
    if mode == "nt":
        return _dot(g, b, NN), _dot(g, a, TN)
    return _dot(b, g, NT), _dot(a, g, NN)


_bdot.defvjp(_bdot_fwd, _bdot_bwd)


def _split(x):
    hi = x.astype(BF16)
    lo = (x - hi.astype(F32)).astype(BF16)
    return hi, lo


def _xdot_right(x, m, dims=NN):
    hi, lo = _split(x)
    if dims == NN:
        return _dot(jnp.concatenate([hi, lo], axis=1), jnp.concatenate([m, m], axis=0), NN)
    return _dot(jnp.concatenate([hi, lo], axis=1), jnp.concatenate([m, m], axis=1), NT)


def _xdot_left(m, x, dims=NN):
    hi, lo = _split(x)
    if dims == NN:
        return _dot(jnp.concatenate([m, m], axis=1), jnp.concatenate([hi, lo], axis=0), NN)
    return _dot(jnp.concatenate([m, m], axis=0), jnp.concatenate([hi, lo], axis=0), TN)


@jax.custom_vjp
def _xr(x, m):
    return _xdot_right(x, m)


def _xr_fwd(x, m):
    return _xdot_right(x, m), m


def _xr_bwd(m, g):
    return _xdot_right(g, m, NT), jnp.zeros_like(m)


_xr.defvjp(_xr_fwd, _xr_bwd)


def _norm_mod(x, g, sc, sh):
    r = lax.rsqrt(jnp.mean(x * x, axis=-1, keepdims=True) + EPS)
    return x * r * g * (1.0 + sc) + sh


MESH = pl.DeviceIdType.MESH
HBM_SPEC = pl.BlockSpec(memory_space=pltpu.HBM)


def _mesh_place():
    return lax.axis_index("x"), lax.axis_index("y"), lax.axis_index("c")


def _block_of(px, py, pc):
    return 4 * px + 2 * py + pc


def _sem_scratch(n):
    return [pltpu.SemaphoreType.DMA((n, N_DEV - 1)), pltpu.SemaphoreType.DMA((n, N_DEV - 1)), pltpu.SemaphoreType.DMA((n,))]


class _GatherPlan:
    def __init__(self, xs):
        self.xs = list(xs)
        self.n = len(self.xs)
        self.out_shape = [_sds((N_DEV, *v.shape), v.dtype) for v in self.xs]
        self.scratch = _sem_scratch(self.n)

    def _parts(self, x_refs, out_refs, sems):
        send_sems, recv_sems, local_sems = sems
        x, y, c = _mesh_place()
        me, sibling = (x, y, c), (x, y, 1 - c)
        chips = [(1 - x, y), (x, 1 - y), (1 - x, 1 - y)]

        def copy(a, k, block, to, src=None):
            rows = out_refs[a].at[_block_of(*block)]
            return pltpu.make_async_remote_copy(
                src_ref=rows if src is None else src, dst_ref=rows, send_sem=send_sems.at[a, k],
                recv_sem=recv_sems.at[a, k], device_id=to, device_id_type=MESH)

        local = [pltpu.make_async_copy(x_refs[a], out_refs[a].at[_block_of(*me)], local_sems.at[a])
                 for a in range(self.n)]
        first = []
        for a in range(self.n):
            first.append(copy(a, 0, me, sibling, src=x_refs[a]))
            first += [copy(a, 1 + j, me, (*chip, c), src=x_refs[a]) for j, chip in enumerate(chips)]
        return me, sibling, chips, c, copy, local, first

    def start(self, x_refs, out_refs, sems):
        *_, local, first = self._parts(x_refs, out_refs, sems)
        for cp in local + first:
            cp.start()

    def finish(self, x_refs, out_refs, sems):
        me, sibling, chips, c, copy, local, first = self._parts(x_refs, out_refs, sems)
        passed = []
        for j, chip in enumerate(chips):
            for a in range(self.n):
                copy(a, 1 + j, (*chip, c), me).wait_recv()
                fwd = copy(a, 4 + j, (*chip, c), sibling)
                fwd.start()
                passed.append(fwd)
        for a in range(self.n):
            copy(a, 0, sibling, me).wait_recv()
            for j, chip in enumerate(chips):
                copy(a, 4 + j, (*chip, 1 - c), me).wait_recv()
        for cp in first + passed:
            cp.wait_send()
        for cp in local:
            cp.wait()


class _ExchangePlan:
    def __init__(self, xs):
        self.xs = list(xs)
        self.n = len(self.xs)
        self.out_shape = [_sds(v.shape, v.dtype) for v in self.xs]
        self.scratch = _sem_scratch(self.n)

    def _parts(self, in_refs, out_refs, sems):
        send_sems, recv_sems, local_sems = sems
        x, y, c = _mesh_place()
        mine = _block_of(x, y, c)
        peers = [(1 - x if k & 4 else x, 1 - y if k & 2 else y, 1 - c if k & 1 else c) for k in range(1, N_DEV)]

        def copy(a, k, slot_src, slot_dst):
            return pltpu.make_async_remote_copy(
                src_ref=in_refs[a].at[slot_src], dst_ref=out_refs[a].at[slot_dst], send_sem=send_sems.at[a, k],
                recv_sem=recv_sems.at[a, k], device_id=peers[k], device_id_type=MESH)

        local = [pltpu.make_async_copy(in_refs[a].at[mine], out_refs[a].at[mine], local_sems.at[a])
                 for a in range(self.n)]
        sends = [copy(a, k, _block_of(*peers[k]), mine) for a in range(self.n) for k in range(N_DEV - 1)]
        arrivals = [copy(a, k, _block_of(*peers[k]), _block_of(*peers[k])) for a in range(self.n) for k in range(N_DEV - 1)]
        return local, sends, arrivals

    def start(self, in_refs, out_refs, sems):
        local, sends, _ = self._parts(in_refs, out_refs, sems)
        for cp in local + sends:
            cp.start()

    def finish(self, in_refs, out_refs, sems):
        local, sends, arrivals = self._parts(in_refs, out_refs, sems)
        for cp in arrivals:
            cp.wait_recv()
        for cp in sends:
            cp.wait_send()
        for cp in local:
            cp.wait()


def _call(body, args, *, grid, in_specs, out_specs, out_shape, scratch_shapes=(), semantics, name, comm=None):
    if comm is None:
        return pl.pallas_call(
            body, grid=grid, in_specs=list(in_specs), out_specs=list(out_specs), out_shape=list(out_shape),
            scratch_shapes=list(scratch_shapes), compiler_params=_params(*semantics), name=name)(*args)
    n_in, n_out, n_scr, n = len(in_specs), len(out_specs), len(scratch_shapes), comm.n

    def hosted(*refs):
        ins, rest = refs[:n_in], refs[n_in:]
        cin, rest = rest[:n], rest[n:]
        outs, rest = rest[:n_out], rest[n_out:]
        cout, rest = rest[:n], rest[n:]
        scr, sems = rest[:n_scr], rest[n_scr:]
        pids = [pl.program_id(d) for d in range(len(grid))]
        first = functools.reduce(jnp.logical_and, [p == 0 for p in pids])
        last = functools.reduce(jnp.logical_and, [p == g - 1 for p, g in zip(pids, grid)])

        @pl.when(first)
        def _():
            comm.start(cin, cout, sems)

        body(*ins, *outs, *scr)

        @pl.when(last)
        def _():
            comm.finish(cin, cout, sems)

    res = pl.pallas_call(
        hosted, grid=grid, in_specs=list(in_specs) + [HBM_SPEC] * n, out_specs=list(out_specs) + [HBM_SPEC] * n,
        out_shape=list(out_shape) + comm.out_shape, scratch_shapes=list(scratch_shapes) + comm.scratch,
        compiler_params=_params(*["arbitrary"] * len(grid)), name=name)(*args, *comm.xs)
    return res[:n_out], res[n_out:]


def _comm_alone(comm, name):
    def body(*refs):
        n = comm.n
        comm.start(refs[:n], refs[n:2 * n], refs[2 * n:])
        comm.finish(refs[:n], refs[n:2 * n], refs[2 * n:])

    return pl.pallas_call(
        body, in_specs=[HBM_SPEC] * comm.n, out_specs=[HBM_SPEC] * comm.n, out_shape=comm.out_shape,
        scratch_shapes=comm.scratch, name=name)(*comm.xs)


def _matmul(a, b, mode, out_dtype, tm, tn, tk, name, comm=None):
    if mode == "nn":
        (m, k), (_, n) = a.shape, b.shape
    elif mode == "nt":
        (m, k), (n, _) = a.shape, b.shape
    else:
        (k, m), (_, n) = a.shape, b.shape
    tm, tn, tk = min(tm, m), min(tn, n), min(tk, k)
    assert m % tm == 0 and n % tn == 0 and k % tk == 0, (name, m, n, k, tm, tn, tk)
    nk = k // tk
    dims = _DIMS[mode]

    def body(a_ref, b_ref, o_ref, acc_ref):
        if nk == 1:
            o_ref[...] = _dot(a_ref[...], b_ref[...], dims).astype(out_dtype)
            return
        kk = pl.program_id(2)

        @pl.when(kk == 0)
        def _():
            acc_ref[...] = _dot(a_ref[...], b_ref[...], dims)

        @pl.when((kk > 0) & (kk < nk - 1))
        def _():
            acc_ref[...] += _dot(a_ref[...], b_ref[...], dims)

        @pl.when(kk == nk - 1)
        def _():
            o_ref[...] = (acc_ref[...] + _dot(a_ref[...], b_ref[...], dims)).astype(out_dtype)

    if mode == "tn":
        a_spec = pl.BlockSpec((tk, tm), lambda i, j, kk: (kk, i))
        b_spec = pl.BlockSpec((tk, tn), lambda i, j, kk: (kk, j))
    elif mode == "nn":
        a_spec = pl.BlockSpec((tm, tk), lambda i, j, kk: (i, kk))
        b_spec = pl.BlockSpec((tk, tn), lambda i, j, kk: (kk, j))
    else:
        a_spec = pl.BlockSpec((tm, tk), lambda i, j, kk: (i, kk))
        b_spec = pl.BlockSpec((tn, tk), lambda i, j, kk: (j, kk))
    res = _call(
        body, (a, b), grid=(m // tm, n // tn, nk), in_specs=[a_spec, b_spec],
        out_specs=[pl.BlockSpec((tm, tn), lambda i, j, kk: (i, j))],
        out_shape=[_sds((m, n), out_dtype)], scratch_shapes=[pltpu.VMEM((tm, tn), F32)],
        semantics=("parallel", "parallel", "arbitrary"), name=name, comm=comm)
    return res[0] if comm is None else (res[0][0], res[1])


ROW_T = 512


def _prenorm(x, pv, name):
    s, d = x.shape
    t = min(ROW_T, s)

    def body(x_ref, pv_ref, h_ref):
        h = _norm_mod(x_ref[...], pv_ref[6:7, :], pv_ref[1:2, :], pv_ref[0:1, :])
        h_ref[...] = h.astype(BF16)

    return pl.pallas_call(
        body, grid=(s // t,),
        in_specs=[pl.BlockSpec((t, d), lambda i: (i, 0)), pl.BlockSpec((16, d), lambda i: (0, 0))],
        out_specs=pl.BlockSpec((t, d), lambda i: (i, 0)), out_shape=_sds((s, d), BF16),
        compiler_params=_params("parallel"), name=name)(x, pv)


DH_TK = 1920


def _dh_prenorm_bwd(dproj, win_t, dres, x, pv, name, comm=None):
    s, k = dproj.shape
    d = x.shape[1]
    t = min(ROW_T, s)
    tk = min(DH_TK, k)
    nk = k // tk

    def body(a_ref, b_ref, dres_ref, x_ref, pv_ref, dx_ref, sg_ref, acc_ref):
        i = pl.program_id(0)
        kk = pl.program_id(1)

        @pl.when((i == 0) & (kk == 0))
        def _():
            sg_ref[...] = jnp.zeros_like(sg_ref)

        @pl.when(kk == 0)
        def _():
            acc_ref[...] = _dot(a_ref[...], b_ref[...], NN)

        @pl.when((kk > 0) & (kk < nk - 1))
        def _():
            acc_ref[...] += _dot(a_ref[...], b_ref[...], NN)

        @pl.when(kk == nk - 1)
        def _():
            dh = acc_ref[...] + _dot(a_ref[...], b_ref[...], NN)
            _, vjp = jax.vjp(_norm_mod, x_ref[...], pv_ref[6:7, :], pv_ref[1:2, :], pv_ref[0:1, :])
            dx, dg, dsc, dsh = vjp(dh)
            dx_ref[...] = dres_ref[...] + dx
            sg_ref[0:1, :] += dsh
            sg_ref[1:2, :] += dsc
            sg_ref[2:3, :] += dg

    assert nk >= 2 and k % tk == 0, (k, tk)
    row = pl.BlockSpec((t, d), lambda i, kk: (i, 0))
    return _call(
        body, (dproj, win_t, dres, x, pv), grid=(s // t, nk),
        in_specs=[pl.BlockSpec((t, tk), lambda i, kk: (i, kk)), pl.BlockSpec((tk, d), lambda i, kk: (kk, 0)),
                  row, row, pl.BlockSpec((16, d), lambda i, kk: (0, 0))],
        out_specs=[row, pl.BlockSpec((8, d), lambda i, kk: (0, 0))],
        out_shape=[_sds((s, d), F32), _sds((8, d), F32)], scratch_shapes=[pltpu.VMEM((t, d), F32)],
        semantics=("arbitrary", "arbitrary"), name=name, comm=comm)


CONV_T = 512


def _conv_tile(a_ext, g_ext, w, b, ln_g, ln_b, n_out):
    u0 = a_ext * jax.nn.sigmoid(g_ext)
    off = CONV_HALO - (CONV_WIDTH - 1)
    acc = jnp.zeros((n_out, u0.shape[1]), F32) + b
    for r in range(8):
        taps = [k for k in range(CONV_WIDTH) if (off + k) % 8 == r]
        rows = n_out if r == 0 else n_out + 8
        part = None
        for k in taps:
            lo = (off + k) // 8 * 8
            term = w[k:k + 1, :] * u0[lo: lo + rows, :]
            part = term if part is None else part + term
        acc = acc + part[r: r + n_out, :]
    mu = jnp.mean(acc, axis=-1, keepdims=True)
    var = jnp.mean(jnp.square(acc - mu), axis=-1, keepdims=True)
    y = (acc - mu) * lax.rsqrt(var + EPS) * ln_g + ln_b
    return y * jax.nn.sigmoid(y)


def _conv_fwd(proj, conv_w, cp, name):
    s = proj.shape[0]
    t = min(CONV_T, s)
    c, h = CONV_CH, CONV_HALO

    def body(ap_ref, ac_ref, gp_ref, gc_ref, w_ref, cp_ref, o_ref):
        i = pl.program_id(0)
        live = (i > 0).astype(F32)
        a_ext = jnp.concatenate([ap_ref[t - h:, :] * live, ac_ref[...]], axis=0)
        g_ext = jnp.concatenate([gp_ref[t - h:, :], gc_ref[...]], axis=0)
        u = _conv_tile(a_ext, g_ext, w_ref[...], cp_ref[0:1, :], cp_ref[1:2, :], cp_ref[2:3, :], t)
        o_ref[...] = u.astype(BF16)

    prev = lambda col: pl.BlockSpec((t, c), lambda i: (jnp.maximum(i - 1, 0), col))
    cur = lambda col: pl.BlockSpec((t, c), lambda i: (i, col))
    return pl.pallas_call(
        body, grid=(s // t,),
        in_specs=[prev(0), cur(0), prev(1), cur(1),
                  pl.BlockSpec((CONV_WIDTH, c), lambda i: (0, 0)), pl.BlockSpec((8, c), lambda i: (0, 0))],
        out_specs=pl.BlockSpec((t, c), lambda i: (i, 0)), out_shape=_sds((s, c), BF16),
        compiler_params=_params("parallel"), name=name)(proj, proj, proj, proj, conv_w, cp)


def _conv_bwd(proj, do, conv_w, cp, name, comm=None):
    s = proj.shape[0]
    t = min(CONV_T, s)
    c, h = CONV_CH, CONV_HALO
    nt = s // t

    def body(ap_ref, ac_ref, an_ref, gp_ref, gc_ref, gn_ref, doc_ref, don_ref, w_ref, cp_ref,
             da_ref, dg_ref, dw_ref, sg_ref):
        i = pl.program_id(0)

        @pl.when(i == 0)
        def _():
            dw_ref[...] = jnp.zeros_like(dw_ref)
            sg_ref[...] = jnp.zeros_like(sg_ref)

        first = (i > 0).astype(F32)
        last = (i < nt - 1).astype(F32)
        a_ext = jnp.concatenate([ap_ref[t - h:, :] * first, ac_ref[...], an_ref[:h, :] * last], axis=0)
        g_ext = jnp.concatenate([gp_ref[t - h:, :], gc_ref[...], gn_ref[:h, :]], axis=0)
        fn = functools.partial(_conv_tile, n_out=t + h)
        _, vjp = jax.vjp(fn, a_ext, g_ext, w_ref[...], cp_ref[0:1, :], cp_ref[1:2, :], cp_ref[2:3, :])
        ct_own = jnp.concatenate([doc_ref[...], jnp.zeros((h, c), F32)], axis=0)
        ct_all = jnp.concatenate([doc_ref[...], don_ref[:h, :] * last], axis=0)
        _, _, dw, db, dlg, dlb = vjp(ct_own)
        da, dg, _, _, _, _ = vjp(ct_all)
        da_ref[...] = da[h:h + t, :].astype(BF16)
        dg_ref[...] = dg[h:h + t, :].astype(BF16)
        dw_ref[...] += dw
        sg_ref[0:1, :] += db
        sg_ref[1:2, :] += dlg
        sg_ref[2:3, :] += dlb

    prev = lambda col: pl.BlockSpec((t, c), lambda i: (jnp.maximum(i - 1, 0), col))
    cur = lambda col: pl.BlockSpec((t, c), lambda i: (i, col))
    nxt = lambda col: pl.BlockSpec((t, c), lambda i: (jnp.minimum(i + 1, nt - 1), col))
    return _call(
        body, (proj, proj, proj, proj, proj, proj, do, do, conv_w, cp), grid=(nt,),
        in_specs=[prev(0), cur(0), nxt(0), prev(1), cur(1), nxt(1), cur(0), nxt(0),
                  pl.BlockSpec((CONV_WIDTH, c), lambda i: (0, 0)), pl.BlockSpec((8, c), lambda i: (0, 0))],
        out_specs=[cur(0), cur(0), pl.BlockSpec((CONV_WIDTH, c), lambda i: (0, 0)),
                   pl.BlockSpec((8, c), lambda i: (0, 0))],
        out_shape=[_sds((s, c), BF16), _sds((s, c), BF16), _sds((CONV_WIDTH, c), F32), _sds((8, c), F32)],
        semantics=("arbitrary",), name=name, comm=comm)


def _hgrn_levels(c):
    out, m = [], c // 2
    while m >= 1:
        out.append(m)
        m //= 2
    return out


def _hgrn_consts(c):
    t = np.arange(c)[:, None]
    j = np.arange(c)[None, :]
    mats = [j <= t, j > t]
    for m in _hgrn_levels(c):
        same = (t // m) == (j // m)
        mats += [same & (j <= t), same & (j > t)]
    return jnp.asarray(np.concatenate(mats, axis=0).astype(np.float32), dtype=BF16)


@jax.custom_vjp
def _cums(lc, mall):
    c = lc.shape[0]
    full = _xdot_left(mall, lc)
    return tuple(full[i * c:(i + 1) * c, :] for i in range(mall.shape[0] // c))


def _cums_fwd(lc, mall):
    return _cums(lc, mall), mall


def _cums_bwd(mall, cts):
    return _xdot_left(mall, jnp.concatenate(cts, axis=0), TN), jnp.zeros_like(mall)


_cums.defvjp(_cums_fwd, _cums_bwd)


def _hgrn_chunk(q, f, v, g, lbs, ng, sts_in, mall):
    c = q.shape[0]
    keep = jax.nn.sigmoid(-f)
    if lbs:
        keep = (1.0 - jax.nn.sigmoid(lbs[1] - lbs[0])) * keep
    lc = jnp.log1p(-keep)
    qs = q * jax.nn.sigmoid(q)
    cs = _cums(lc, mall)
    q_in = qs * jnp.exp(cs[0])
    k_out = keep * jnp.exp(cs[1])
    decay = jnp.exp(jnp.sum(lc, axis=0, keepdims=True))
    qk = qs * keep
    r = lax.broadcasted_iota(jnp.int32, q.shape, 0)
    tt = lax.broadcasted_iota(jnp.int32, (c, c), 0)
    ss = lax.broadcasted_iota(jnp.int32, (c, c), 1)
    levels = []
    for li, m in enumerate(_hgrn_levels(c)):
        lg = m.bit_length() - 1
        odd = ((r >> lg) & 1) == 1
        qm = jnp.where(odd, qs * jnp.exp(cs[2 + 2 * li]), 0.0)
        km = jnp.where(odd, 0.0, keep * jnp.exp(cs[3 + 2 * li]))
        pair = (((tt >> lg) & 1) == 1) & ((ss >> lg) == (tt >> lg) - 1)
        levels.append((qm, km, pair))
    outs, sts_out = [], []
    for h, st_in in enumerate(sts_in):
        hs = slice(h * HG_DK, (h + 1) * HG_DK)
        vh = v[:, hs]
        sc = jnp.where(tt == ss, jnp.sum(qk[:, hs], axis=-1, keepdims=True), 0.0)
        for qm, km, pair in levels:
            sc = sc + jnp.where(pair, _bdot(qm[:, hs], km[:, hs], "nt"), 0.0)
        o = _bdot(q_in[:, hs], st_in, "nt") + _bdot(sc, vh, "nn")
        sts_out.append(st_in * decay[:, hs] + _bdot(vh, k_out[:, hs], "tn"))
        outs.append(o * lax.rsqrt(jnp.mean(o * o, axis=-1, keepdims=True) + EPS) * ng)
    return jnp.concatenate(outs, axis=1) * (g * jax.nn.sigmoid(g)), tuple(sts_out)


def _hgrn_fwd(proj, lb, ng, name, comm=None):
    s = proj.shape[0]
    c = HG_CHUNK
    nc = s // c
    mall = _hgrn_consts(c)
    col0 = 1024 // (HG_HEADS * HG_DK)

    def body(*refs):
        q_ref, f_ref, v_ref, g_ref = refs[:4]
        if lb is None:
            ng_ref, m_ref, y_ref, st_ref, scr = refs[4:]
        else:
            lb_ref, ng_ref, m_ref, y_ref, st_ref, scr = refs[4:]
        ci = pl.program_id(0)

        @pl.when(ci == 0)
        def _():
            scr[...] = jnp.zeros_like(scr)

        lbs = () if lb is None else (lb_ref[0:1, :], lb_ref[1:2, :])
        sts_in = tuple(scr[h] for h in range(HG_HEADS))
        for h in range(HG_HEADS):
            st_ref[h] = sts_in[h]
        y, sts_out = _hgrn_chunk(q_ref[...], f_ref[...], v_ref[...], g_ref[...], lbs, ng_ref[...], sts_in, m_ref[...])
        y_ref[...] = y.astype(BF16)
        for h in range(HG_HEADS):
            scr[h] = sts_out[h]

    w = HG_HEADS * HG_DK
    col = lambda k: pl.BlockSpec((c, w), lambda ci: (ci, col0 + k))
    in_specs = [col(0), col(1), col(2), col(3)]
    args = [proj, proj, proj, proj]
    if lb is not None:
        in_specs.append(pl.BlockSpec((2, w), lambda ci: (0, 0)))
        args.append(lb)
    in_specs += [pl.BlockSpec((1, HG_DK), lambda ci: (0, 0)), pl.BlockSpec(mall.shape, lambda ci: (0, 0))]
    args += [ng, mall]
    return _call(
        body, args, grid=(nc,), in_specs=in_specs,
        out_specs=[pl.BlockSpec((c, w), lambda ci: (ci, 0)),
                   pl.BlockSpec((HG_HEADS, None, HG_DK, HG_DK), lambda ci: (0, ci, 0, 0))],
        out_shape=[_sds((s, w), BF16), _sds((HG_HEADS, nc, HG_DK, HG_DK), F32)],
        scratch_shapes=[pltpu.VMEM((HG_HEADS, HG_DK, HG_DK), F32)],
        semantics=("arbitrary",), name=name, comm=comm)


def _hgrn_bwd(proj, states, dy, lb, ng, name, comm=None):
    s = proj.shape[0]
    c = HG_CHUNK
    nc = s // c
    mall = _hgrn_consts(c)
    col0 = 1024 // (HG_HEADS * HG_DK)

    def body(*refs):
        q_ref, f_ref, v_ref, g_ref, st_ref, dy_ref = refs[:6]
        if lb is None:
            ng_ref, m_ref, dq_ref, df_ref, dv_ref, dg_ref, dlb_ref, dng_ref, scr = refs[6:]
        else:
            lb_ref, ng_ref, m_ref, dq_ref, df_ref, dv_ref, dg_ref, dlb_ref, dng_ref, scr = refs[6:]
        ci = pl.program_id(0)

        @pl.when(ci == 0)
        def _():
            scr[...] = jnp.zeros_like(scr)
            dlb_ref[...] = jnp.zeros_like(dlb_ref)
            dng_ref[...] = jnp.zeros_like(dng_ref)

        mall_v = m_ref[...]
        fn = lambda q, f, v, g, lbs_, ng_, sts: _hgrn_chunk(q, f, v, g, lbs_, ng_, sts, mall_v)
        lbs = () if lb is None else (lb_ref[0:1, :], lb_ref[1:2, :])
        sts_in = tuple(st_ref[h] for h in range(HG_HEADS))
        _, vjp = jax.vjp(fn, q_ref[...], f_ref[...], v_ref[...], g_ref[...], lbs, ng_ref[...], sts_in)
        dq, df, dv, dg, dlbs, dng, dsts = vjp((dy_ref[...], tuple(scr[h] for h in range(HG_HEADS))))
        dq_ref[...] = dq.astype(BF16)
        df_ref[...] = df.astype(BF16)
        dv_ref[...] = dv.astype(BF16)
        dg_ref[...] = dg.astype(BF16)
        for h in range(HG_HEADS):
            scr[h] = dsts[h]
        dng_ref[0:1, :] += dng
        if lbs:
            dlb_ref[0:1, :] += dlbs[0]
            dlb_ref[1:2, :] += dlbs[1]

    w = HG_HEADS * HG_DK
    rev = lambda ci: nc - 1 - ci
    col = lambda k: pl.BlockSpec((c, w), lambda ci: (rev(ci), col0 + k))
    out_col = pl.BlockSpec((c, w), lambda ci: (rev(ci), 0))
    in_specs = [col(0), col(1), col(2), col(3),
                pl.BlockSpec((HG_HEADS, None, HG_DK, HG_DK), lambda ci: (0, rev(ci), 0, 0)), out_col]
    args = [proj, proj, proj, proj, states, dy]
    if lb is not None:
        in_specs.append(pl.BlockSpec((2, w), lambda ci: (0, 0)))
        args.append(lb)
    in_specs += [pl.BlockSpec((1, HG_DK), lambda ci: (0, 0)), pl.BlockSpec(mall.shape, lambda ci: (0, 0))]
    args += [ng, mall]
    return _call(
        body, args, grid=(nc,), in_specs=in_specs,
        out_specs=[out_col, out_col, out_col, out_col,
                   pl.BlockSpec((2, w), lambda ci: (0, 0)), pl.BlockSpec((8, HG_DK), lambda ci: (0, 0))],
        out_shape=[_sds((s, w), BF16)] * 4 + [_sds((2, w), F32), _sds((8, HG_DK), F32)],
        scratch_shapes=[pltpu.VMEM((HG_HEADS, HG_DK, HG_DK), F32)],
        semantics=("arbitrary",), name=name, comm=comm)


def _head_avg():
    w = SB_HEADS * SB_DH
    i = np.arange(w)
    return jnp.asarray(((i[:, None] // SB_DH) == (i[None, :] // SB_DH)).astype(np.float32) / SB_DH, dtype=BF16)


def _sb_norm(x, g_tiled, avg):
    ms = _xr(x * x, avg)
    return x * lax.rsqrt(ms + EPS) * g_tiled


def _sb_prep(proj, gq, gk, name):
    s = proj.shape[0]
    t = min(ROW_T, s)
    w = SB_HEADS * SB_DH
    avg = _head_avg()

    def body(q_ref, k_ref, v_ref, gq_ref, gk_ref, avg_ref, qn_ref, kn_ref, vb_ref):
        qn_ref[...] = _sb_norm(q_ref[...], gq_ref[...], avg_ref[...]).astype(BF16)
        kn_ref[...] = _sb_norm(k_ref[...], gk_ref[...], avg_ref[...]).astype(BF16)
        vb_ref[...] = v_ref[...].astype(BF16)

    col = lambda k: pl.BlockSpec((t, w), lambda i: (i, 6 + k))
    vec = pl.BlockSpec((1, w), lambda i: (0, 0))
    out = pl.BlockSpec((t, w), lambda i: (i, 0))
    return pl.pallas_call(
        body, grid=(s // t,), in_specs=[col(0), col(1), col(2), vec, vec, pl.BlockSpec((w, w), lambda i: (0, 0))],
        out_specs=[out, out, out], out_shape=[_sds((s, w), BF16)] * 3,
        compiler_params=_params("parallel"), name=name)(proj, proj, proj, gq, gk, avg)


def _sb_prep_bwd(proj, dqn, dkn, gq, gk, name):
    s = proj.shape[0]
    t = min(ROW_T, s)
    w = SB_HEADS * SB_DH
    avg = _head_avg()

    def body(q_ref, k_ref, dqn_ref, dkn_ref, gq_ref, gk_ref, avg_ref, dq_ref, dk_ref, sg_ref):
        i = pl.program_id(0)

        @pl.when(i == 0)
        def _():
            sg_ref[...] = jnp.zeros_like(sg_ref)

        avg_v = avg_ref[...]
        fn = lambda x, g: _sb_norm(x, g, avg_v)
        _, vq = jax.vjp(fn, q_ref[...], gq_ref[...])
        dq, dgq = vq(dqn_ref[...])
        _, vk = jax.vjp(fn, k_ref[...], gk_ref[...])
        dk, dgk = vk(dkn_ref[...])
        dq_ref[...] = dq.astype(BF16)
        dk_ref[...] = dk.astype(BF16)
        sg_ref[0:1, :] += dgq
        sg_ref[1:2, :] += dgk

    col = lambda k: pl.BlockSpec((t, w), lambda i: (i, 6 + k))
    vec = pl.BlockSpec((1, w), lambda i: (0, 0))
    row = pl.BlockSpec((t, w), lambda i: (i, 0))
    return pl.pallas_call(
        body, grid=(s // t,),
        in_specs=[col(0), col(1), row, row, vec, vec, pl.BlockSpec((w, w), lambda i: (0, 0))],
        out_specs=[row, row, pl.BlockSpec((8, w), lambda i: (0, 0))],
        out_shape=[_sds((s, w), BF16), _sds((s, w), BF16), _sds((8, w), F32)],
        compiler_params=_params("arbitrary"), name=name)(proj, proj, dqn, dkn, gq, gk, avg)


def _sb_tri(kind):
    j = np.arange(SB_BLK)[:, None]
    s = np.arange(SB_BLK)[None, :]
    tri = (j > s) if kind == "suffix" else (j < s)
    return jnp.asarray(np.concatenate([tri, np.ones_like(tri)], axis=1).astype(np.float32), dtype=BF16)


def _sb_scores(qm, kblk, mask):
    z = _dot(qm, kblk, NT) * (SB_DH ** -0.5)
    sp = jnp.maximum(z, 0.0) + jnp.log(1.0 + jnp.exp(-jnp.abs(z)))
    return z, sp, jnp.where(mask, -sp, 0.0)


def _sb_setup(b):
    lane = lax.broadcasted_iota(jnp.int32, (2 * b, b), 1)
    row = lax.broadcasted_iota(jnp.int32, (2 * b, b), 0)
    mine = (row >> (b.bit_length() - 1)) == (lane >> (SB_DH.bit_length() - 1))
    return lane, row & (b - 1), mine


def _sb_fwd(qn, kn, vb, name, comm=None):
    s, w = qn.shape
    b = SB_BLK
    nq = s // b
    tri = _sb_tri("suffix")

    def body(q_ref, k_ref, v_ref, tri_ref, o_ref):
        i = pl.program_id(1)
        lane, tt, mine = _sb_setup(b)
        tri_v = tri_ref[...]
        pairs = [slice(u * b, (u + 1) * b) for u in range(SB_PAIRS)]
        qms = []
        for ls in pairs:
            q = q_ref[:, ls]
            q2 = jnp.concatenate([q, q], axis=0)
            qms.append(jnp.where(mine, q2, jnp.zeros_like(q2)))

        offs = [pl.multiple_of(jnp.maximum(i - j, 0) * b, b) for j in range(SB_FIXED)]
        masks = [lane < (tt if j == 0 else jnp.where(i >= j, b, 0)) for j in range(SB_FIXED)]
        scores = [[_sb_scores(qm, k_ref[pl.ds(off, b), ls], m) for off, m in zip(offs, masks)]
                  for qm, ls in zip(qms, pairs)]
        boths = [[_xdot_right(lk, tri_v) for _, _, lk in per_pair] for per_pair in scores]
        for u, ls in enumerate(pairs):
            qm = qms[u]
            run = acc = jnp.zeros((2 * b, b), F32)
            for j in range(SB_FIXED):
                z, sp, _ = scores[u][j]
                a = jnp.where(masks[j], jnp.exp(z - sp + boths[u][j][:, :b] + run), 0.0)
                acc = acc + _dot(a.astype(BF16), v_ref[pl.ds(offs[j], b), ls], NN)
                run = run + boths[u][j][:, b:]

            def cond(carry):
                j, run_, _ = carry
                return (j <= i) & (jnp.max(run_) > SB_DEAD)

            def step(carry, qm=qm, ls=ls):
                j, run_, acc_ = carry
                off = pl.multiple_of((i - j) * b, b)
                z, sp, lk = _sb_scores(qm, k_ref[pl.ds(off, b), ls], lane < b)
                both = _xdot_right(lk, tri_v)
                a = jnp.exp(z - sp + both[:, :b] + run_)
                return j + 1, run_ + both[:, b:], acc_ + _dot(a.astype(BF16), v_ref[pl.ds(off, b), ls], NN)

            _, _, acc = lax.while_loop(cond, step, (jnp.int32(SB_FIXED), run, acc))
            o_ref[:, ls] = jnp.where(lane[:b] < SB_DH, acc[:b], acc[b:]).astype(BF16)

    wide = SB_PAIRS * b
    blk = pl.BlockSpec((b, wide), lambda p, i: (i, p))
    full = pl.BlockSpec((s, wide), lambda p, i: (0, p))
    return _call(
        body, (qn, kn, vb, tri), grid=(w // wide, nq),
        in_specs=[blk, full, full, pl.BlockSpec(tri.shape, lambda p, i: (0, 0))],
        out_specs=[blk], out_shape=[_sds((s, w), BF16)],
        semantics=("parallel", "arbitrary"), name=name, comm=comm)


def _sb_bwd(qn, kn, vb, do, name, comm=None):
    s, w = qn.shape
    b = SB_BLK
    nq = s // b
    tri_s = _sb_tri("suffix")
    tri_p = _sb_tri("prefix")
    scale = SB_DH ** -0.5

    def body(q_ref, k_ref, v_ref, do_ref, ts_ref, tp_ref, dq_ref, dk_ref, dv_ref, dk_acc, dv_acc, dp_scr):
        i = pl.program_id(1)

        @pl.when(i == 0)
        def _():
            dk_acc[...] = jnp.zeros_like(dk_acc)
            dv_acc[...] = jnp.zeros_like(dv_acc)

        lane, tt, mine = _sb_setup(b)
        ts_v = ts_ref[...]
        tp_v = tp_ref[...]
        zero = jnp.zeros((2 * b, b), F32)
        pairs = [slice(u * b, (u + 1) * b) for u in range(SB_PAIRS)]
        qms, doms = [], []
        for ls in pairs:
            q = q_ref[:, ls]
            q2 = jnp.concatenate([q, q], axis=0)
            qms.append(jnp.where(mine, q2, jnp.zeros_like(q2)))
            dout = do_ref[:, ls].astype(BF16)
            d2 = jnp.concatenate([dout, dout], axis=0)
            doms.append(jnp.where(mine, d2, jnp.zeros_like(d2)))

        def down(u, kb, run):
            ls, qm, dom = pairs[u], qms[u], doms[u]
            off = pl.multiple_of(kb * b, b)
            z, sp, lk = _sb_scores(qm, k_ref[pl.ds(off, b), ls], lane < b)
            both = _xdot_right(lk, ts_v)
            a = jnp.exp(z - sp + both[:, :b] + run)
            dv_acc[pl.ds(off, b), ls] += _dot(a.astype(BF16), dom, TN)
            return _dot(dom, v_ref[pl.ds(off, b), ls], NT) * a, run + both[:, b:]

        def up(u, kb, dp, pre, dq):
            ls, qm = pairs[u], qms[u]
            off = pl.multiple_of(kb * b, b)
            kblk = k_ref[pl.ds(off, b), ls]
            sig = jax.nn.sigmoid(_dot(qm, kblk, NT) * scale)
            both = _xdot_right(dp, tp_v)
            dz = ((dp * (1.0 - sig) - sig * (both[:, :b] + pre)) * scale).astype(BF16)
            dk_acc[pl.ds(off, b), ls] += _dot(dz, qm, TN)
            return pre + both[:, b:], dq + _dot(dz, kblk, NN)

        offs = [pl.multiple_of(jnp.maximum(i - j, 0) * b, b) for j in range(SB_FIXED)]
        masks = [lane < (tt if j == 0 else jnp.where(i >= j, b, 0)) for j in range(SB_FIXED)]
        kblks = [[k_ref[pl.ds(off, b), ls] for off in offs] for ls in pairs]
        scores = [[_sb_scores(qms[u], kblks[u][j], masks[j]) for j in range(SB_FIXED)] for u in range(SB_PAIRS)]
        das = [[_dot(doms[u], v_ref[pl.ds(off, b), pairs[u]], NT) for off in offs] for u in range(SB_PAIRS)]
        boths = [[_xdot_right(lk, ts_v) for _, _, lk in scores[u]] for u in range(SB_PAIRS)]
        runs, dps = [], []
        for u in range(SB_PAIRS):
            run = zero
            mine_dps = []
            for j in range(SB_FIXED):
                z, sp, _ = scores[u][j]
                a = jnp.where(masks[j], jnp.exp(z - sp + boths[u][j][:, :b] + run), 0.0)
                mine_dps.append(das[u][j] * a)
                dv_acc[pl.ds(offs[j], b), pairs[u]] += _dot(a.astype(BF16), doms[u], TN)
                run = run + boths[u][j][:, b:]
            runs.append(run)
            dps.append(mine_dps)

        carries = []
        for u in range(SB_PAIRS):

            def cond(carry):
                j, run_ = carry
                return (j <= i) & (jnp.max(run_) > SB_DEAD)

            def sweep_down(carry, u=u):
                j, run_ = carry
                dp, run_ = down(u, i - j, run_)
                dp_scr[i - j] = dp
                return j + 1, run_

            n_live, _ = lax.while_loop(cond, sweep_down, (jnp.int32(SB_FIXED), runs[u]))

            def sweep_up(jj, carry, u=u, n_live=n_live):
                kb = i - n_live + 1 + jj
                return up(u, kb, dp_scr[kb], *carry)

            carries.append(lax.fori_loop(0, n_live - SB_FIXED, sweep_up, (zero, zero)))

        pres = [[_xdot_right(dp, tp_v) for dp in dps[u]] for u in range(SB_PAIRS)]
        for u, ls in enumerate(pairs):
            pre, dq = carries[u]
            for j in reversed(range(SB_FIXED)):
                z, sp, _ = scores[u][j]
                sig = jnp.exp(z - sp)
                dz = jnp.where(masks[j], dps[u][j] * (1.0 - sig) - sig * (pres[u][j][:, :b] + pre), 0.0) * scale
                dz = dz.astype(BF16)
                dk_acc[pl.ds(offs[j], b), ls] += _dot(dz, qms[u], TN)
                dq = dq + _dot(dz, kblks[u][j], NN)
                pre = pre + pres[u][j][:, b:]
            dq_ref[:, ls] = jnp.where(lane[:b] < SB_DH, dq[:b], dq[b:])

        @pl.when(i == nq - 1)
        def _():
            dk_ref[...] = dk_acc[...]
            dv_ref[...] = dv_acc[...].astype(BF16)

    wide = SB_PAIRS * b
    blk = pl.BlockSpec((b, wide), lambda p, i: (i, p))
    full = pl.BlockSpec((s, wide), lambda p, i: (0, p))
    tri = pl.BlockSpec(tri_s.shape, lambda p, i: (0, 0))
    return _call(
        body, (qn, kn, vb, do, tri_s, tri_p), grid=(w // wide, nq), in_specs=[blk, full, full, blk, tri, tri],
        out_specs=[blk, full, full], out_shape=[_sds((s, w), F32), _sds((s, w), F32), _sds((s, w), BF16)],
        scratch_shapes=[pltpu.VMEM((s, wide), F32), pltpu.VMEM((s, wide), F32), pltpu.VMEM((nq, 2 * b, b), F32)],
        semantics=("arbitrary", "arbitrary"), name=name, comm=comm)


MIX_T = 256
HALF = 512


def _gate_slices(ga, gb):
    return [(ga[:, 0:512], ga[:, 512:1024]), (ga[:, 1024:1536], gb[:, 0:512]), (gb[:, 512:1024], gb[:, 1024:1536])]


def _mix_fwd(u3, oh, osb, proj, x, pv, wc, wh, ws, wo, name):
    s, d = x.shape
    t = min(MIX_T, s)

    def body(u3_ref, oh_ref, os_ref, ga_ref, gb_ref, x_ref, pv_ref, wc_ref, wh_ref, ws_ref, wo_ref,
             x1_ref, h2_ref, mg_ref, mo_ref):
        ys = [_dot(u3_ref[...], wc_ref[...], NT), _dot(oh_ref[...], wh_ref[...], NT), _dot(os_ref[...], ws_ref[...], NT)]
        gl = _gate_slices(ga_ref[...], gb_ref[...])
        halves = []
        for hf in range(2):
            lo = hf * HALF
            acc = jnp.zeros((t, HALF), F32)
            for br in range(3):
                gate = jax.nn.sigmoid(gl[br][hf] + pv_ref[8 + br:9 + br, lo:lo + HALF])
                acc = acc + gate * ys[br][:, lo:lo + HALF]
            halves.append(acc)
        merged = jnp.concatenate(halves, axis=1).astype(BF16)
        mg_ref[...] = merged
        mo = _dot(merged, wo_ref[...], NN)
        mo_ref[...] = mo.astype(BF16)
        x1 = x_ref[...] + pv_ref[2:3, :] * mo
        x1_ref[...] = x1
        h2_ref[...] = _norm_mod(x1, pv_ref[7:8, :], pv_ref[4:5, :], pv_ref[3:4, :]).astype(BF16)

    br_spec = pl.BlockSpec((t, CONV_CH), lambda i: (i, 0))
    row = pl.BlockSpec((t, d), lambda i: (i, 0))
    wproj = pl.BlockSpec((d, CONV_CH), lambda i: (0, 0))
    return pl.pallas_call(
        body, grid=(s // t,),
        in_specs=[br_spec, br_spec, br_spec, pl.BlockSpec((t, 1536), lambda i: (i, 3)),
                  pl.BlockSpec((t, 1536), lambda i: (i, 4)), row, pl.BlockSpec((16, d), lambda i: (0, 0)),
                  wproj, wproj, wproj, pl.BlockSpec((d, d), lambda i: (0, 0))],
        out_specs=[row, row, row, row],
        out_shape=[_sds((s, d), F32), _sds((s, d), BF16), _sds((s, d), BF16), _sds((s, d), BF16)],
        compiler_params=_params("parallel"), name=name)(u3, oh, osb, proj, proj, x, pv, wc, wh, ws, wo)


def _mix_bwd(dx1, mo1, u3, oh, osb, proj, pv, wc, wh, ws, wo, name):
    s, d = dx1.shape
    t = min(MIX_T, s)

    def body(dx_ref, mo_ref, u3_ref, oh_ref, os_ref, ga_ref, gb_ref, pv_ref, wc_ref, wh_ref, ws_ref, wo_ref,
             dmo_ref, dyc_ref, dyh_ref, dys_ref, doc_ref, doh_ref, dos_ref, dgl_ref, sg_ref):
        i = pl.program_id(0)

        @pl.when(i == 0)
        def _():
            sg_ref[...] = jnp.zeros_like(sg_ref)

        dx = dx_ref[...]
        dmo = (dx * pv_ref[2:3, :]).astype(BF16)
        dmo_ref[...] = dmo
        sg_ref[0:1, :] += jnp.sum(dx * mo_ref[...].astype(F32), axis=0, keepdims=True)
        dmerged = _dot(dmo, wo_ref[...], NT)
        branches = [(u3_ref, wc_ref, dyc_ref, doc_ref), (oh_ref, wh_ref, dyh_ref, doh_ref), (os_ref, ws_ref, dys_ref, dos_ref)]
        gl = _gate_slices(ga_ref[...], gb_ref[...])
        for br, (o_ref, w_ref, dy_ref, do_ref) in enumerate(branches):
            y = _dot(o_ref[...], w_ref[...], NT)
            dys = []
            for hf in range(2):
                lo = hf * HALF
                gate = jax.nn.sigmoid(gl[br][hf] + pv_ref[8 + br:9 + br, lo:lo + HALF])
                dm = dmerged[:, lo:lo + HALF]
                dys.append(dm * gate)
                dgl = dm * y[:, lo:lo + HALF] * gate * (1.0 - gate)
                dgl_ref[:, br * d + lo: br * d + lo + HALF] = dgl.astype(BF16)
                sg_ref[1 + br:2 + br, lo:lo + HALF] += jnp.sum(dgl, axis=0, keepdims=True)
            dy = jnp.concatenate(dys, axis=1).astype(BF16)
            dy_ref[...] = dy
            do_ref[...] = _dot(dy, w_ref[...], NN)

    br_spec = pl.BlockSpec((t, CONV_CH), lambda i: (i, 0))
    row = pl.BlockSpec((t, d), lambda i: (i, 0))
    wproj = pl.BlockSpec((d, CONV_CH), lambda i: (0, 0))
    return pl.pallas_call(
        body, grid=(s // t,),
        in_specs=[row, row, br_spec, br_spec, br_spec, pl.BlockSpec((t, 1536), lambda i: (i, 3)),
                  pl.BlockSpec((t, 1536), lambda i: (i, 4)), pl.BlockSpec((16, d), lambda i: (0, 0)),
                  wproj, wproj, wproj, pl.BlockSpec((d, d), lambda i: (0, 0))],
        out_specs=[row, row, row, row, br_spec, br_spec, br_spec, pl.BlockSpec((t, 3 * d), lambda i: (i, 0)),
                   pl.BlockSpec((8, d), lambda i: (0, 0))],
        out_shape=[_sds((s, d), BF16)] * 4 + [_sds((s, CONV_CH), F32)] * 3 + [_sds((s, 3 * d), BF16), _sds((8, d), F32)],
        compiler_params=_params("arbitrary"), name=name)(dx1, mo1, u3, oh, osb, proj, proj, pv, wc, wh, ws, wo)


MLP_T = 512
MLP_F = 1024


def _mlp_fwd(h2, x1, pv, w1t, w2, name, comm=None):
    s, d = x1.shape
    t = min(MLP_T, s)
    nf = D_FF // MLP_F

    def body(h_ref, x_ref, pv_ref, w1_ref, w2_ref, x2_ref, mo_ref, acc_ref):
        f = pl.program_id(1)

        @pl.when(f == 0)
        def _():
            acc_ref[...] = jnp.zeros_like(acc_ref)

        a = jnp.maximum(_dot(h_ref[...], w1_ref[...], NT), 0.0)
        acc_ref[...] += _dot((a * a).astype(BF16), w2_ref[...], NN)

        @pl.when(f == nf - 1)
        def _():
            mo = acc_ref[...]
            mo_ref[...] = mo.astype(BF16)
            x2_ref[...] = x_ref[...] + pv_ref[5:6, :] * mo

    row = pl.BlockSpec((t, d), lambda i, f: (i, 0))
    wblk = pl.BlockSpec((MLP_F, d), lambda i, f: (f, 0))
    return _call(
        body, (h2, x1, pv, w1t, w2), grid=(s // t, nf),
        in_specs=[row, row, pl.BlockSpec((16, d), lambda i, f: (0, 0)), wblk, wblk],
        out_specs=[row, row], out_shape=[_sds((s, d), F32), _sds((s, d), BF16)],
        scratch_shapes=[pltpu.VMEM((t, d), F32)],
        semantics=("parallel", "arbitrary"), name=name, comm=comm)


def _mlp_bwd(dx2, h2, x1, mo2, pv, w1t, w2, name, comm=None):
    s, d = x1.shape
    t = min(MLP_T, s)
    nf = D_FF // MLP_F

    def body(dx_ref, h_ref, x_ref, mo_ref, pv_ref, w1_ref, w2_ref, dx1_ref, da_ref, b_ref, dmo_ref, sg_ref, acc_ref):
        i = pl.program_id(0)
        f = pl.program_id(1)

        @pl.when((i == 0) & (f == 0))
        def _():
            sg_ref[...] = jnp.zeros_like(sg_ref)

        @pl.when(f == 0)
        def _():
            acc_ref[...] = jnp.zeros_like(acc_ref)
            dx = dx_ref[...]
            dmo_ref[...] = (dx * pv_ref[5:6, :]).astype(BF16)
            sg_ref[0:1, :] += jnp.sum(dx * mo_ref[...].astype(F32), axis=0, keepdims=True)

        r = jnp.maximum(_dot(h_ref[...], w1_ref[...], NT), 0.0)
        b_ref[...] = (r * r).astype(BF16)
        da = (_dot(dmo_ref[...], w2_ref[...], NT) * (2.0 * r)).astype(BF16)
        da_ref[...] = da
        acc_ref[...] += _dot(da, w1_ref[...], NN)

        @pl.when(f == nf - 1)
        def _():
            _, vjp = jax.vjp(_norm_mod, x_ref[...], pv_ref[7:8, :], pv_ref[4:5, :], pv_ref[3:4, :])
            dxn, dg, dsc, dsh = vjp(acc_ref[...])
            dx1_ref[...] = dx_ref[...] + dxn
            sg_ref[1:2, :] += dsh
            sg_ref[2:3, :] += dsc
            sg_ref[3:4, :] += dg

    row = pl.BlockSpec((t, d), lambda i, f: (i, 0))
    wblk = pl.BlockSpec((MLP_F, d), lambda i, f: (f, 0))
    hid = pl.BlockSpec((t, MLP_F), lambda i, f: (i, f))
    return _call(
        body, (dx2, h2, x1, mo2, pv, w1t, w2), grid=(s // t, nf),
        in_specs=[row, row, row, row, pl.BlockSpec((16, d), lambda i, f: (0, 0)), wblk, wblk],
        out_specs=[row, hid, hid, row, pl.BlockSpec((8, d), lambda i, f: (0, 0))],
        out_shape=[_sds((s, d), F32), _sds((s, D_FF), BF16), _sds((s, D_FF), BF16), _sds((s, d), BF16), _sds((8, d), F32)],
        scratch_shapes=[pltpu.VMEM((t, d), F32)],
        semantics=("arbitrary", "arbitrary"), name=name, comm=comm)


def _loss_head(y, target, name):
    s, d = y.shape
    t = min(ROW_T, s)

    def body(y_ref, t_ref, dy_ref, ls_ref):
        i = pl.program_id(0)

        @pl.when(i == 0)
        def _():
            ls_ref[...] = jnp.zeros_like(ls_ref)

        e = y_ref[...] - t_ref[...]
        dy_ref[...] = e * (1.0 / d)
        ls_ref[...] += jnp.sum((e * e).reshape(t // 8, 8, d), axis=0)

    row = pl.BlockSpec((t, d), lambda i: (i, 0))
    return pl.pallas_call(
        body, grid=(s // t,), in_specs=[row, row], out_specs=[row, pl.BlockSpec((8, d), lambda i: (0, 0))],
        out_shape=[_sds((s, d), F32), _sds((8, d), F32)],
        compiler_params=_params("arbitrary"), name=name)(y, target)


def _layer_vectors(l, mod, sm):
    d = D_MODEL
    pv = jnp.concatenate([mod[l].reshape(6, d), sm["norm1_g"][l][None], sm["norm2_g"][l][None],
                          sm["gate_b"][l].reshape(3, d), jnp.zeros((5, d), F32)], axis=0)
    cp = jnp.concatenate([sm["conv_b"][l][None], sm["conv_ln_g"][l][None], sm["conv_ln_b"][l][None],
                          jnp.zeros((5, CONV_CH), F32)], axis=0)
    return dict(pv=pv, cp=cp, conv_w=sm["conv_w"][l], lb=(sm["hgrn_lb"] if l > 0 else None),
                ng=sm["hgrn_norm_g"][l][None], gq=jnp.tile(sm["sb_qn_g"][l], SB_HEADS)[None],
                gk=jnp.tile(sm["sb_kn_g"][l], SB_HEADS)[None])


def _hosted(res, comm):
    return res if comm is not None else (res, None)


def _layer_fwd_mixers(x, vec, win_t, tag, comm_proj=None, comm_hgrn=None, comm_sb=None):
    h = _prenorm(x, vec["pv"], f"prenorm{tag}")
    proj, got_proj = _hosted(_matmul(h, win_t, "nt", F32, 1024, 768, 1024, f"proj{tag}", comm_proj), comm_proj)
    u3 = _conv_fwd(proj, vec["conv_w"], vec["cp"], f"conv_fwd{tag}")
    (oh, states), got_hgrn = _hosted(_hgrn_fwd(proj, vec["lb"], vec["ng"], f"hgrn_fwd{tag}", comm_hgrn), comm_hgrn)
    qn, kn, vb = _sb_prep(proj, vec["gq"], vec["gk"], f"sb_prep{tag}")
    (osb,), got_sb = _hosted(_sb_fwd(qn, kn, vb, f"sb_fwd{tag}", comm_sb), comm_sb)
    saved = dict(x=x, h=h, proj=proj, u3=u3, oh=oh, states=states, qn=qn, kn=kn, vb=vb, osb=osb)
    return saved, (got_proj, got_hgrn, got_sb)


def _layer_fwd_out(sv, vec, w, tag, comm_mlp=None):
    x1, h2, merged, mo1 = _mix_fwd(sv["u3"], sv["oh"], sv["osb"], sv["proj"], sv["x"], vec["pv"],
                                   w["wc_t"], w["wh_t"], w["ws_t"], w["wo"], f"mix_fwd{tag}")
    (x2, mo2), got = _hosted(_mlp_fwd(h2, x1, vec["pv"], w["w1_t"], w["w2"], f"mlp_fwd{tag}", comm_mlp), comm_mlp)
    sv.update(x1=x1, h2=h2, merged=merged, mo1=mo1, mo2=mo2)
    return x2, got


def _layer_bwd(dx2, sv, vec, w, tag, plans=None):
    plans = plans or {}
    got = {}

    def plan_for(key, big_now):
        return plans[key](big_now) if key in plans else None

    pv = vec["pv"]
    big = {}
    comm = plan_for("mlp", big)
    (dx1, da, bsq, dmo2, sg_mlp), got["mlp"] = _hosted(
        _mlp_bwd(dx2, sv["h2"], sv["x1"], sv["mo2"], pv, w["w1_t"], w["w2"], f"mlp_bwd{tag}", comm), comm)
    big["w1_t"] = _matmul(da, sv["h2"], "tn", BF16, 1024, 1024, 1024, f"dw1{tag}")
    big["w2"] = _matmul(bsq, dmo2, "tn", BF16, 1024, 1024, 1024, f"dw2{tag}")
    dmo1, dyc, dyh, dys, doc, doh, dos, dgl, sg_mix = _mix_bwd(
        dx1, sv["mo1"], sv["u3"], sv["oh"], sv["osb"], sv["proj"], pv, w["wc_t"], w["wh_t"], w["ws_t"], w["wo"], f"mix_bwd{tag}")
    big["wo"] = _matmul(sv["merged"], dmo1, "tn", BF16, 1024, 1024, 1024, f"dwo{tag}")
    big["wc_t"] = _matmul(dyc, sv["u3"], "tn", BF16, 1024, 512, 1024, f"dwc{tag}")
    big["wh_t"] = _matmul(dyh, sv["oh"], "tn", BF16, 1024, 512, 1024, f"dwh{tag}")
    big["ws_t"] = _matmul(dys, sv["osb"], "tn", BF16, 1024, 512, 1024, f"dws{tag}")
    comm = plan_for("conv", big)
    (da_c, dg_c, dconv_w, sg_conv), got["conv"] = _hosted(
        _conv_bwd(sv["proj"], doc, vec["conv_w"], vec["cp"], f"conv_bwd{tag}", comm), comm)
    comm = plan_for("hgrn", big)
    (dq_h, df_h, di_h, dg_h, dlb, dng), got["hgrn"] = _hosted(
        _hgrn_bwd(sv["proj"], sv["states"], doh, vec["lb"], vec["ng"], f"hgrn_bwd{tag}", comm), comm)
    comm = plan_for("sb", big)
    (dqn, dkn, dv_s), got["sb"] = _hosted(_sb_bwd(sv["qn"], sv["kn"], sv["vb"], dos, f"sb_bwd{tag}", comm), comm)
    dq_s, dk_s, sg_sb = _sb_prep_bwd(sv["proj"], dqn, dkn, vec["gq"], vec["gk"], f"sb_prep_bwd{tag}")
    dproj = jnp.concatenate([da_c, dg_c, dq_h, df_h, di_h, dg_h, dq_s, dk_s, dv_s, dgl], axis=1)
    half = D_MODEL // 2
    comm = plan_for("dwin_a", big)
    big["win_a"], got["dwin_a"] = _hosted(
        _matmul(dproj, sv["h"][:, :half], "tn", BF16, 768, half, 1024, f"dwin_a{tag}", comm), comm)
    comm = plan_for("dwin", big)
    big["win_b"], got["dwin"] = _hosted(
        _matmul(dproj, sv["h"][:, half:], "tn", BF16, 768, half, 1024, f"dwin_b{tag}", comm), comm)
    comm = plan_for("dh", big)
    (dx, sg_pre), got["dh"] = _hosted(_dh_prenorm_bwd(dproj, w["win_t"], dx1, sv["x"], pv, f"dh{tag}", comm), comm)
    small = dict(
        mod=jnp.stack([sg_pre[0], sg_pre[1], sg_mix[0], sg_mlp[1], sg_mlp[2], sg_mlp[0]]).reshape(6 * D_MODEL),
        norm1_g=sg_pre[2], norm2_g=sg_mlp[3], gate_b=sg_mix[1:4].reshape(3 * D_MODEL),
        conv_w=dconv_w, conv_b=sg_conv[0], conv_ln_g=sg_conv[1], conv_ln_b=sg_conv[2],
        hgrn_lb=dlb, hgrn_norm_g=dng[0],
        sb_qn_g=sg_sb[0].reshape(SB_HEADS, SB_DH).sum(0), sb_kn_g=sg_sb[1].reshape(SB_HEADS, SB_DH).sum(0))
    return dx, big, small, got


def _row_tile(r, cap=512):
    t = min(r, cap)
    while r % t or (t % 8 and t != r):
        t -= 1
    return t


def _sum8(z, name):
    _, r, c = z.shape
    t = _row_tile(r, 128 if c >= 1024 else 512)

    def body(z_ref, o_ref):
        acc = z_ref[0].astype(F32)
        for j in range(1, N_DEV):
            acc = acc + z_ref[j].astype(F32)
        o_ref[...] = acc

    return pl.pallas_call(
        body, grid=(r // t,), in_specs=[pl.BlockSpec((N_DEV, t, c), lambda i: (0, i, 0))],
        out_specs=pl.BlockSpec((t, c), lambda i: (i, 0)), out_shape=_sds((r, c), F32),
        compiler_params=_params("parallel"), name=name)(z)


def _adamw(w, g, m, v, name):
    r, c = w.shape
    t = _row_tile(r, 256)

    def body(w_ref, g_ref, m_ref, v_ref, d_ref, nm_ref, nv_ref):
        g_ = g_ref[...]
        nm = ADAM_B1 * m_ref[...] + (1.0 - ADAM_B1) * g_
        nv = ADAM_B2 * v_ref[...] + (1.0 - ADAM_B2) * jnp.square(g_)
        m_hat = nm / (1.0 - ADAM_B1 ** ADAM_STEP)
        v_hat = nv / (1.0 - ADAM_B2 ** ADAM_STEP)
        d_ref[...] = -ADAM_LR * (m_hat / (jnp.sqrt(v_hat) + ADAM_EPS) + ADAM_WD * w_ref[...])
        nm_ref[...] = nm
        nv_ref[...] = nv

    blk = pl.BlockSpec((t, c), lambda i: (i, 0))
    return pl.pallas_call(
        body, grid=(r // t,), in_specs=[blk] * 4, out_specs=[blk] * 3, out_shape=[_sds((r, c), F32)] * 3,
        compiler_params=_params("parallel"), name=name)(w, g, m, v)


def _mod_local(c_all, mod_w, name):
    depth, d, cols = mod_w.shape

    def body(c_ref, w_ref, o_ref):
        cv = c_ref[...]
        act = cv * jax.nn.sigmoid(cv)
        o_ref[...] = jnp.dot(act, w_ref[...], precision=lax.Precision.HIGHEST, preferred_element_type=F32)

    return pl.pallas_call(
        body, grid=(depth,),
        in_specs=[pl.BlockSpec((N_DEV, d), lambda l: (0, 0)), pl.BlockSpec((None, d, cols), lambda l: (l, 0, 0))],
        out_specs=pl.BlockSpec((None, N_DEV, cols), lambda l: (l, 0, 0)), out_shape=_sds((depth, N_DEV, cols), F32),
        compiler_params=_params("parallel"), name=name)(c_all, mod_w)


def _modw_grad(c_all, dmod, name):
    depth, _, cols = dmod.shape
    d = c_all.shape[1]

    def body(c_ref, g_ref, o_ref):
        cv = c_ref[...]
        act = cv * jax.nn.sigmoid(cv)
        o_ref[...] = lax.dot_general(act, g_ref[...], (TN, ((), ())), precision=lax.Precision.HIGHEST,
                                     preferred_element_type=F32)

    return pl.pallas_call(
        body, grid=(depth,),
        in_specs=[pl.BlockSpec((N_DEV, d), lambda l: (0, 0)), pl.BlockSpec((None, N_DEV, cols), lambda l: (l, 0, 0))],
        out_specs=pl.BlockSpec((None, d, cols), lambda l: (l, 0, 0)), out_shape=_sds((depth, d, cols), F32),
        compiler_params=_params("parallel"), name=name)(c_all, dmod)


LANE = 128
BIG = {"w_in": ("win_t", True), "w_out": ("wo", False), "mlp_w2": ("w2", False), "mlp_w1": ("w1_t", True),
       "w_conv_proj": ("wc_t", True), "w_hgrn_proj": ("wh_t", True), "w_sb_proj": ("ws_t", True)}
PROJS = ("w_conv_proj", "w_hgrn_proj", "w_sb_proj")
SMALL = (("mod_b", 6144), ("norm1_g", 1024), ("gate_b", 3072), ("conv_w", CONV_WIDTH * CONV_CH), ("conv_b", 512),
         ("conv_ln_g", 512), ("conv_ln_b", 512), ("hgrn_lb", 512), ("hgrn_norm_g", 128), ("sb_qn_g", 64),
         ("sb_kn_g", 64), ("norm2_g", 1024))


def _pack_rows(parts, width):
    flat = jnp.concatenate([p.reshape(-1) for p in parts])
    rows = -(-flat.shape[0] // width)
    rows = -(-rows // 8) * 8
    return jnp.pad(flat, (0, rows * width - flat.shape[0])).reshape(rows, width)


def _shards(params, items):
    return [(params[n][l].T if BIG[n][1] else params[n][l]).astype(BF16) for n, l in items]


def _gathered(items, got):
    return {BIG[n][0]: g.reshape(-1, g.shape[2]) for (n, _), g in zip(items, got)}


def _by_shard(g):
    return g.reshape(N_DEV, g.shape[0] // N_DEV, g.shape[1])


def _adamw_nd(w, g, m, v, name):
    shape = w.shape
    two = lambda a: a.reshape(-1, shape[-1])
    return [o.reshape(shape) for o in _adamw(two(w), two(g), two(m), two(v), name)]


WEIGHTS = ("mod_w", "mod_b", "norm1_g", "w_in", "gate_b", "conv_w", "conv_b", "conv_ln_g", "conv_ln_b", "w_conv_proj",
           "hgrn_lb", "hgrn_norm_g", "w_hgrn_proj", "sb_qn_g", "sb_kn_g", "w_sb_proj", "w_out", "norm2_g", "mlp_w1",
           "mlp_w2")


def kernel(x, c, mod_w, mod_b, norm1_g, w_in, gate_b, conv_w, conv_b, conv_ln_g, conv_ln_b, w_conv_proj, hgrn_lb, hgrn_norm_g, w_hgrn_proj, sb_qn_g, sb_kn_g, w_sb_proj, w_out, norm2_g, mlp_w1, mlp_w2, loss_target, m_mod_w, m_mod_b, m_norm1_g, m_w_in, m_gate_b, m_conv_w, m_conv_b, m_conv_ln_g, m_conv_ln_b, m_w_conv_proj, m_hgrn_lb, m_hgrn_norm_g, m_w_hgrn_proj, m_sb_qn_g, m_sb_kn_g, m_w_sb_proj, m_w_out, m_norm2_g, m_mlp_w1, m_mlp_w2, v_mod_w, v_mod_b, v_norm1_g, v_w_in, v_gate_b, v_conv_w, v_conv_b, v_conv_ln_g, v_conv_ln_b, v_w_conv_proj, v_hgrn_lb, v_hgrn_norm_g, v_w_hgrn_proj, v_sb_qn_g, v_sb_kn_g, v_w_sb_proj, v_w_out, v_norm2_g, v_mlp_w1, v_mlp_w2):
    params = dict(mod_w=mod_w, mod_b=mod_b, norm1_g=norm1_g, w_in=w_in, gate_b=gate_b, conv_w=conv_w, conv_b=conv_b,
                  conv_ln_g=conv_ln_g, conv_ln_b=conv_ln_b, w_conv_proj=w_conv_proj, hgrn_lb=hgrn_lb,
                  hgrn_norm_g=hgrn_norm_g, w_hgrn_proj=w_hgrn_proj, sb_qn_g=sb_qn_g, sb_kn_g=sb_kn_g,
                  w_sb_proj=w_sb_proj, w_out=w_out, norm2_g=norm2_g, mlp_w1=mlp_w1, mlp_w2=mlp_w2)
    mom1 = dict(mod_w=m_mod_w, mod_b=m_mod_b, norm1_g=m_norm1_g, w_in=m_w_in, gate_b=m_gate_b, conv_w=m_conv_w,
                conv_b=m_conv_b, conv_ln_g=m_conv_ln_g, conv_ln_b=m_conv_ln_b, w_conv_proj=m_w_conv_proj,
                hgrn_lb=m_hgrn_lb, hgrn_norm_g=m_hgrn_norm_g, w_hgrn_proj=m_w_hgrn_proj, sb_qn_g=m_sb_qn_g,
                sb_kn_g=m_sb_kn_g, w_sb_proj=m_w_sb_proj, w_out=m_w_out, norm2_g=m_norm2_g, mlp_w1=m_mlp_w1,
                mlp_w2=m_mlp_w2)
    mom2 = dict(mod_w=v_mod_w, mod_b=v_mod_b, norm1_g=v_norm1_g, w_in=v_w_in, gate_b=v_gate_b, conv_w=v_conv_w,
                conv_b=v_conv_b, conv_ln_g=v_conv_ln_g, conv_ln_b=v_conv_ln_b, w_conv_proj=v_w_conv_proj,
                hgrn_lb=v_hgrn_lb, hgrn_norm_g=v_hgrn_norm_g, w_hgrn_proj=v_w_hgrn_proj, sb_qn_g=v_sb_qn_g,
                sb_kn_g=v_sb_kn_g, w_sb_proj=v_w_sb_proj, w_out=v_w_out, norm2_g=v_norm2_g, mlp_w1=v_mlp_w1,
                mlp_w2=v_mlp_w2)
    xi, yi, ci = _mesh_place()
    me = _block_of(xi, yi, ci)
    cw_cols = conv_w.shape[2]

    tiny = _pack_rows([c, conv_w], LANE)
    g_tiny, g_win0 = _comm_alone(_GatherPlan([tiny] + _shards(params, [("w_in", 0)])), "gather_first")
    c_rows = D_MODEL // LANE
    c_all = g_tiny[:, :c_rows].reshape(N_DEV, D_MODEL)
    n_cw = DEPTH * CONV_WIDTH * cw_cols
    conv_w_full = g_tiny[:, c_rows:c_rows + n_cw // LANE].reshape(N_DEV, DEPTH, CONV_WIDTH, cw_cols)
    conv_w_full = conv_w_full.transpose(1, 2, 0, 3).reshape(DEPTH, CONV_WIDTH, CONV_CH)

    (g_mod,) = _comm_alone(_GatherPlan([_mod_local(c_all, mod_w, "mod_local")]), "gather_mod")
    mod = lax.dynamic_index_in_dim(g_mod, me, axis=2, keepdims=False)
    mod = mod.transpose(1, 0, 2).reshape(DEPTH, 6 * D_MODEL) + mod_b

    sm = dict(norm1_g=norm1_g, norm2_g=norm2_g, gate_b=gate_b, conv_w=conv_w_full, conv_b=conv_b, conv_ln_g=conv_ln_g,
              conv_ln_b=conv_ln_b, hgrn_lb=hgrn_lb, hgrn_norm_g=hgrn_norm_g, sb_qn_g=sb_qn_g, sb_kn_g=sb_kn_g)
    vecs = [_layer_vectors(l, mod, sm) for l in range(DEPTH)]

    fwd_hosts = dict(
        proj=[("w_in", 1)],
        hgrn=[("mlp_w2", 0), ("w_out", 0)] + [(n, 0) for n in PROJS],
        sb=[("mlp_w1", 0), ("mlp_w2", 1)],
        mlp=[("mlp_w1", 1), ("w_out", 1)] + [(n, 1) for n in PROJS])
    gather = {host: _GatherPlan(_shards(params, items)) for host, items in fwd_hosts.items()}
    wts = [_gathered([("w_in", 0)], [g_win0]), {}]
    sv0, got_mixers = _layer_fwd_mixers(x[0], vecs[0], wts[0]["win_t"], "_l0", gather["proj"], gather["hgrn"], gather["sb"])
    for host, got in zip(("proj", "hgrn", "sb"), got_mixers):
        for (name, l), g in zip(fwd_hosts[host], got):
            wts[l].update(_gathered([(name, l)], [g]))
    y, got = _layer_fwd_out(sv0, vecs[0], wts[0], "_l0", gather["mlp"])
    wts[1].update(_gathered(fwd_hosts["mlp"], got))
    sv1, _ = _layer_fwd_mixers(y, vecs[1], wts[1]["win_t"], "_l1")
    y, _ = _layer_fwd_out(sv1, vecs[1], wts[1], "_l1")
    dy, sq = _loss_head(y, loss_target[0], "loss_head")
    loss = lax.psum(0.5 * jnp.sum(sq) / D_MODEL, ("x", "y", "c"))

    own_hosts = dict(
        hgrn=[(1, "w1_t"), (1, "wo"), (1, "wc_t"), (1, "wh_t"), (1, "ws_t")],
        sb=[(1, "w2")])
    own_plans = {host: (lambda big, items=items: _ExchangePlan([_by_shard(big[k]) for _, k in items]))
                 for host, items in own_hosts.items()}
    dy, big1, small1, got1 = _layer_bwd(dy, sv1, vecs[1], wts[1], "_l1", own_plans)
    bwd_hosts = dict(
        mlp=[(1, "win_a")],
        conv=[(1, "win_b"), (0, "wc_t"), (0, "wh_t"), (0, "ws_t")],
        hgrn=[(0, "wo"), (0, "w2")],
        sb=[(0, "w1_t")],
        dwin=[(0, "win_a")],
        dh=[(0, "win_b")])
    plans = {host: (lambda big0, items=items: _ExchangePlan([_by_shard((big1 if l else big0)[k]) for l, k in items]))
             for host, items in bwd_hosts.items()}
    dx, _, small0, got = _layer_bwd(dy, sv0, vecs[0], wts[0], "_l0", plans)
    smalls = [small0, small1]
    summed = {}
    for hosts, arrived_by_host in ((own_hosts, got1), (bwd_hosts, got)):
        for host, items in hosts.items():
            for (l, key), arrived in zip(items, arrived_by_host[host]):
                summed[l, key] = _sum8(arrived, f"sum_{key}_l{l}")
    grads = {}
    for name, (key, transposed) in BIG.items():
        per_layer = []
        for l in range(DEPTH):
            if name == "w_in":
                blk = jnp.concatenate([summed[l, "win_a"], summed[l, "win_b"]], axis=1)
            else:
                blk = summed[l, key]
            per_layer.append(blk.T if transposed else blk)
        grads[name] = jnp.stack(per_layer)

    small_parts = []
    for name, _ in SMALL:
        key = "mod" if name == "mod_b" else name
        if name == "hgrn_lb":
            small_parts.append(smalls[0][key] + smalls[1][key])
        else:
            small_parts.append(jnp.stack([smalls[l][key] for l in range(DEPTH)]))
    (g_small,) = _comm_alone(_GatherPlan([_pack_rows(small_parts, LANE)]), "gather_small_grads")
    small_sum = _sum8(g_small, "sum_small_grads").reshape(-1)
    off = 0
    for name, per_layer in SMALL:
        grads[name] = small_sum[off:off + DEPTH * per_layer].reshape(params[name].shape if name != "conv_w" else (DEPTH, CONV_WIDTH, CONV_CH))
        off += DEPTH * per_layer
    grads["conv_w"] = lax.dynamic_slice_in_dim(grads["conv_w"], me * cw_cols, cw_cols, axis=2)
    cols = mod_w.shape[2]
    dmod_all = g_small.reshape(N_DEV, -1)[:, :DEPTH * 6 * D_MODEL].reshape(N_DEV, DEPTH, 6 * D_MODEL)
    dmod_mine = lax.dynamic_slice_in_dim(dmod_all, me * cols, cols, axis=2).transpose(1, 0, 2)
    grads["mod_w"] = _modw_grad(c_all, dmod_mine, "mod_w_grad")

    delta, new_m, new_v = {}, {}, {}
    small_names = [n for n, _ in SMALL]
    for name in WEIGHTS:
        if name not in small_names:
            delta[name], new_m[name], new_v[name] = _adamw_nd(params[name], grads[name], mom1[name], mom2[name], f"adamw_{name}")
    packed = [_pack_rows([d[n] for n in small_names], LANE) for d in (params, grads, mom1, mom2)]
    outs = [o.reshape(-1) for o in _adamw(*packed, "adamw_small")]
    off = 0
    for name in small_names:
        size = params[name].size
        for dst, o in zip((delta, new_m, new_v), outs):
            dst[name] = o[off:off + size].reshape(params[name].shape)
        off += size
    return (loss, dx[None], *[grads[n] for n in WEIGHTS], *[delta[n] for n in WEIGHTS],
            *[new_m[n] for n in WEIGHTS], *[new_v[n] for n in WEIGHTS])
```
